```python
import jax, jax.numpy as jnp
from jax import lax
import numpy as np

D_MODEL = 1024
BATCH = 8
SEQ = 8192
DEPTH = 1

N_ATTN_HEADS = 8
HEAD_DIM = 64
ATTN_WIDTH = N_ATTN_HEADS * HEAD_DIM
CONV_GROUPS = 8
CONV_WIDTH = 512
CONV_K = 3
D_FF = 4 * D_MODEL
PLE_DIM = 256
Q_BLOCK = 128
EPS = 1e-6
SPLIT_SIZES = (ATTN_WIDTH, ATTN_WIDTH, ATTN_WIDTH, CONV_WIDTH, CONV_WIDTH, CONV_WIDTH, D_MODEL, D_MODEL)
SPLIT_POINTS = tuple(int(v) for v in np.cumsum(SPLIT_SIZES)[:-1])
D_IN = sum(SPLIT_SIZES)

kernel_name = 'hybrid_stickbreak_shortconv_block'


def rms_norm(x, g):
    xf = x.astype(jnp.float32)
    var = jnp.mean(xf * xf, axis=-1, keepdims=True)
    return (xf * lax.rsqrt(var + EPS) * g.astype(jnp.float32)).astype(x.dtype)


def stick_breaking_attention(q, k, v):
    b, h, s, dh = q.shape
    nblk = s // Q_BLOCK
    scale = dh ** -0.5
    kf = k.astype(jnp.float32)
    vf = v.astype(jnp.float32)
    q_blocks = q.reshape(b, h, nblk, Q_BLOCK, dh).transpose(2, 0, 1, 3, 4)
    key_pos = jnp.arange(s, dtype=jnp.int32)
    starts = jnp.arange(nblk, dtype=jnp.int32) * Q_BLOCK

    def block(args):
        qb, start = args
        z = jnp.einsum('bhqd,bhkd->bhqk', qb.astype(jnp.float32), kf) * scale
        q_pos = start + jnp.arange(Q_BLOCK, dtype=jnp.int32)
        causal = key_pos[None, :] < q_pos[:, None]
        log_beta = jax.nn.log_sigmoid(z)
        log_keep = jnp.where(causal, log_beta - z, 0.0)
        between = lax.cumsum(log_keep, axis=3, reverse=True) - log_keep
        w = jnp.where(causal, jnp.exp(log_beta + between), 0.0)
        return jnp.einsum('bhqk,bhkd->bhqd', w, vf)

    out = lax.map(block, (q_blocks, starts))
    return out.transpose(1, 2, 0, 3, 4).reshape(b, h, s, dh).astype(q.dtype)


def causal_depthwise_conv(u, w):
    c = u.shape[-1]
    return lax.conv_general_dilated(
        u, w[:, None, :].astype(u.dtype), window_strides=(1,),
        padding=((CONV_K - 1, 0),), dimension_numbers=('NWC', 'WIO', 'NWC'),
        feature_group_count=c)


def _fwd_setup_inputs(seed: int = 0) -> dict:
    key = jax.random.key(seed)
    ks = jax.random.split(key, 20)
    f32 = jnp.float32

    def nrm(k, shape, fan_in):
        return jax.random.normal(k, shape, f32) * (fan_in ** -0.5)

    def gain(k, shape):
        return jnp.ones(shape, f32) + 0.05 * jax.random.normal(k, shape, f32)

    return {
        'x': jax.random.normal(ks[0], (BATCH, SEQ, D_MODEL), f32),
        'p': jax.random.normal(ks[1], (DEPTH, BATCH, SEQ, PLE_DIM), f32),
        'g_pre_mix': gain(ks[2], (DEPTH, D_MODEL)),
        'w_in': nrm(ks[3], (DEPTH, D_MODEL, D_IN), D_MODEL),
        'b_gate': 0.1 * jax.random.normal(ks[4], (DEPTH, 2 * D_MODEL), f32),
        'w_conv': nrm(ks[5], (DEPTH, CONV_K, CONV_WIDTH), CONV_K),
        'w_attn_out': nrm(ks[6], (DEPTH, ATTN_WIDTH, D_MODEL), ATTN_WIDTH),
        'w_conv_out': nrm(ks[7], (DEPTH, CONV_WIDTH, D_MODEL), CONV_WIDTH),
        'w_o': nrm(ks[8], (DEPTH, D_MODEL, D_MODEL), D_MODEL),
        'g_post_mix': gain(ks[9], (DEPTH, D_MODEL)),
        'g_pre_mlp': gain(ks[10], (DEPTH, D_MODEL)),
        'w_up': nrm(ks[11], (DEPTH, D_MODEL, D_FF), D_MODEL),
        'w_down': nrm(ks[12], (DEPTH, D_FF, D_MODEL), D_FF),
        'g_post_mlp': gain(ks[13], (DEPTH, D_MODEL)),
        'g_ple': gain(ks[14], (DEPTH, D_MODEL)),
        'w_ple_gate': nrm(ks[15], (DEPTH, D_MODEL, D_MODEL), D_MODEL),
        'w_ple_proj': nrm(ks[16], (DEPTH, PLE_DIM, D_MODEL), PLE_DIM),
    }


def _fwd_reference(x, p, g_pre_mix, w_in, b_gate, w_conv, w_attn_out, w_conv_out, w_o,
              g_post_mix, g_pre_mlp, w_up, w_down, g_post_mlp, g_ple, w_ple_gate, w_ple_proj):
    bsz, seq, _ = x.shape
    for i in range(DEPTH):
        h = rms_norm(x, g_pre_mix[i])
        proj = h @ w_in[i]
        q, k, v, cb, cc, cu, ga, gc = jnp.split(proj, SPLIT_POINTS, axis=-1)

        def heads(t):
            return t.reshape(bsz, seq, N_ATTN_HEADS, HEAD_DIM).transpose(0, 2, 1, 3)

        o = stick_breaking_attention(heads(q), heads(k), heads(v))
        o = o.transpose(0, 2, 1, 3).reshape(bsz, seq, ATTN_WIDTH)
        y_attn = o @ w_attn_out[i]

        y_conv = (cb * causal_depthwise_conv(cc * cu, w_conv[i])) @ w_conv_out[i]

        gates = jax.nn.sigmoid(jnp.concatenate([ga, gc], axis=-1) + b_gate[i])
        gate_attn, gate_conv = jnp.split(gates, 2, axis=-1)
        mixed = (gate_attn * y_attn + gate_conv * y_conv) @ w_o[i]
        x = x + rms_norm(mixed, g_post_mix[i])

        h = rms_norm(x, g_pre_mlp[i])
        f = jnp.square(jax.nn.relu(h @ w_up[i])) @ w_down[i]
        x = x + rms_norm(f, g_post_mlp[i])

        ple_gate = jax.nn.sigmoid(rms_norm(x, g_ple[i]) @ w_ple_gate[i])
        x = x + ple_gate * (p[i] @ w_ple_proj[i])
    return x


import jax as _jax
import jax.numpy as _jnp

TWIN_FORMAT = 'train_step'
FWD_PARAMS = ['x', 'p', 'g_pre_mix', 'w_in', 'b_gate', 'w_conv', 'w_attn_out', 'w_conv_out', 'w_o', 'g_post_mix', 'g_pre_mlp', 'w_up', 'w_down', 'g_post_mlp', 'g_ple', 'w_ple_gate', 'w_ple_proj']
TWIN_WEIGHTS = ['g_pre_mix', 'w_in', 'b_gate', 'w_conv', 'w_attn_out', 'w_conv_out', 'w_o', 'g_post_mix', 'g_pre_mlp', 'w_up', 'w_down', 'g_post_mlp', 'g_ple', 'w_ple_gate', 'w_ple_proj']
TWIN_DIFF_INPUT = 'x'
TWIN_INPUTS = ['x', 'p', 'g_pre_mix', 'w_in', 'b_gate', 'w_conv', 'w_attn_out', 'w_conv_out', 'w_o', 'g_post_mix', 'g_pre_mlp', 'w_up', 'w_down', 'g_post_mlp', 'g_ple', 'w_ple_gate', 'w_ple_proj', 'loss_target', 'm_g_pre_mix', 'm_w_in', 'm_b_gate', 'm_w_conv', 'm_w_attn_out', 'm_w_conv_out', 'm_w_o', 'm_g_post_mix', 'm_g_pre_mlp', 'm_w_up', 'm_w_down', 'm_g_post_mlp', 'm_g_ple', 'm_w_ple_gate', 'm_w_ple_proj', 'v_g_pre_mix', 'v_w_in', 'v_b_gate', 'v_w_conv', 'v_w_attn_out', 'v_w_conv_out', 'v_w_o', 'v_g_post_mix', 'v_g_pre_mlp', 'v_w_up', 'v_w_down', 'v_g_post_mlp', 'v_g_ple', 'v_w_ple_gate', 'v_w_ple_proj']
TWIN_OUTPUTS = ['loss', 'grad_x', 'grad_g_pre_mix', 'grad_w_in', 'grad_b_gate', 'grad_w_conv', 'grad_w_attn_out', 'grad_w_conv_out', 'grad_w_o', 'grad_g_post_mix', 'grad_g_pre_mlp', 'grad_w_up', 'grad_w_down', 'grad_g_post_mlp', 'grad_g_ple', 'grad_w_ple_gate', 'grad_w_ple_proj', 'delta_g_pre_mix', 'delta_w_in', 'delta_b_gate', 'delta_w_conv', 'delta_w_attn_out', 'delta_w_conv_out', 'delta_w_o', 'delta_g_post_mix', 'delta_g_pre_mlp', 'delta_w_up', 'delta_w_down', 'delta_g_post_mlp', 'delta_g_ple', 'delta_w_ple_gate', 'delta_w_ple_proj', 'new_m_g_pre_mix', 'new_m_w_in', 'new_m_b_gate', 'new_m_w_conv', 'new_m_w_attn_out', 'new_m_w_conv_out', 'new_m_w_o', 'new_m_g_post_mix', 'new_m_g_pre_mlp', 'new_m_w_up', 'new_m_w_down', 'new_m_g_post_mlp', 'new_m_g_ple', 'new_m_w_ple_gate', 'new_m_w_ple_proj', 'new_v_g_pre_mix', 'new_v_w_in', 'new_v_b_gate', 'new_v_w_conv', 'new_v_w_attn_out', 'new_v_w_conv_out', 'new_v_w_o', 'new_v_g_post_mix', 'new_v_g_pre_mlp', 'new_v_w_up', 'new_v_w_down', 'new_v_g_post_mlp', 'new_v_g_ple', 'new_v_w_ple_gate', 'new_v_w_ple_proj']
TWIN_LEAF_KINDS = {'loss': 'loss', 'grad_x': 'grad_x', 'grad_g_pre_mix': 'grad_w', 'grad_w_in': 'grad_w', 'grad_b_gate': 'grad_w', 'grad_w_conv': 'grad_w', 'grad_w_attn_out': 'grad_w', 'grad_w_conv_out': 'grad_w', 'grad_w_o': 'grad_w', 'grad_g_post_mix': 'grad_w', 'grad_g_pre_mlp': 'grad_w', 'grad_w_up': 'grad_w', 'grad_w_down': 'grad_w', 'grad_g_post_mlp': 'grad_w', 'grad_g_ple': 'grad_w', 'grad_w_ple_gate': 'grad_w', 'grad_w_ple_proj': 'grad_w', 'delta_g_pre_mix': 'delta_w', 'delta_w_in': 'delta_w', 'delta_b_gate': 'delta_w', 'delta_w_conv': 'delta_w', 'delta_w_attn_out': 'delta_w', 'delta_w_conv_out': 'delta_w', 'delta_w_o': 'delta_w', 'delta_g_post_mix': 'delta_w', 'delta_g_pre_mlp': 'delta_w', 'delta_w_up': 'delta_w', 'delta_w_down': 'delta_w', 'delta_g_post_mlp': 'delta_w', 'delta_g_ple': 'delta_w', 'delta_w_ple_gate': 'delta_w', 'delta_w_ple_proj': 'delta_w', 'new_m_g_pre_mix': 'new_m', 'new_m_w_in': 'new_m', 'new_m_b_gate': 'new_m', 'new_m_w_conv': 'new_m', 'new_m_w_attn_out': 'new_m', 'new_m_w_conv_out': 'new_m', 'new_m_w_o': 'new_m', 'new_m_g_post_mix': 'new_m', 'new_m_g_pre_mlp': 'new_m', 'new_m_w_up': 'new_m', 'new_m_w_down': 'new_m', 'new_m_g_post_mlp': 'new_m', 'new_m_g_ple': 'new_m', 'new_m_w_ple_gate': 'new_m', 'new_m_w_ple_proj': 'new_m', 'new_v_g_pre_mix': 'new_v', 'new_v_w_in': 'new_v', 'new_v_b_gate': 'new_v', 'new_v_w_conv': 'new_v', 'new_v_w_attn_out': 'new_v', 'new_v_w_conv_out': 'new_v', 'new_v_w_o': 'new_v', 'new_v_g_post_mix': 'new_v', 'new_v_g_pre_mlp': 'new_v', 'new_v_w_up': 'new_v', 'new_v_w_down': 'new_v', 'new_v_g_post_mlp': 'new_v', 'new_v_g_ple': 'new_v', 'new_v_w_ple_gate': 'new_v', 'new_v_w_ple_proj': 'new_v'}


def _forward(args):
    return _fwd_reference(*[args[k] for k in FWD_PARAMS])


def _output_shape():
    def fwd():
        inp = _fwd_setup_inputs(0)
        return _fwd_reference(*[inp[k] for k in FWD_PARAMS])
    out = _jax.eval_shape(fwd)
    return out.shape, out.dtype

N_MICROBATCH = 1
ADAM_LR = 0.001
ADAM_B1 = 0.9
ADAM_B2 = 0.999
ADAM_EPS = 1e-08
ADAM_WD = 0.01
ADAM_STEP = 10
PER_EXAMPLE_BATCH_AXIS = {'x': 0, 'p': 1, 'loss_target': 0}
SHARED_INPUTS = []
_WEIGHT_DTYPES = {'g_pre_mix': _jnp.float32, 'w_in': _jnp.float32, 'b_gate': _jnp.float32, 'w_conv': _jnp.float32, 'w_attn_out': _jnp.float32, 'w_conv_out': _jnp.float32, 'w_o': _jnp.float32, 'g_post_mix': _jnp.float32, 'g_pre_mlp': _jnp.float32, 'w_up': _jnp.float32, 'w_down': _jnp.float32, 'g_post_mlp': _jnp.float32, 'g_ple': _jnp.float32, 'w_ple_gate': _jnp.float32, 'w_ple_proj': _jnp.float32}
MOMENT_SCALE = {'g_pre_mix': 1.017373e+00, 'w_in': 4.442877e-01, 'b_gate': 3.943781e-01, 'w_conv': 9.873979e-01, 'w_attn_out': 4.435868e-01, 'w_conv_out': 1.022158e+00, 'w_o': 1.443169e+00, 'g_post_mix': 6.402833e+01, 'g_pre_mlp': 1.711975e+00, 'w_up': 8.649609e-01, 'w_down': 2.487352e+00, 'g_post_mlp': 6.556566e+01, 'g_ple': 2.487856e+00, 'w_ple_gate': 1.498129e+00, 'w_ple_proj': 8.718560e-01}


def _to_microbatches(a, axis):
    t = _jnp.moveaxis(a, axis, 0)
    t = t.reshape((N_MICROBATCH, t.shape[0] // N_MICROBATCH) + t.shape[1:])
    return _jnp.moveaxis(t, 1, axis + 1)


def setup_inputs(seed: int = 0) -> dict:
    inp = _fwd_setup_inputs(seed)
    key = _jax.random.fold_in(_jax.random.key(seed), 7919)
    shape, _ = _output_shape()
    out = dict(inp)
    out["loss_target"] = _jax.random.normal(_jax.random.fold_in(key, 0), shape, _jnp.float32)
    for i, name in enumerate(TWIN_WEIGHTS):
        w = inp[name].astype(_jnp.float32)
        if MOMENT_SCALE is None:
            s = _jnp.sqrt(_jnp.mean(_jnp.square(w)) + 1e-30)
        else:
            s = MOMENT_SCALE[name]
        km, kv = _jax.random.split(_jax.random.fold_in(key, i + 1))
        out[name] = w
        out["m_" + name] = s * _jax.random.normal(km, w.shape, _jnp.float32)
        out["v_" + name] = (s * s) * _jax.random.uniform(kv, w.shape, _jnp.float32, 0.5, 1.5)
    if N_MICROBATCH > 1:
        for name, axis in PER_EXAMPLE_BATCH_AXIS.items():
            out[name] = _to_microbatches(out[name], axis)
    return {'x': out['x'], 'p': out['p'], 'g_pre_mix': out['g_pre_mix'], 'w_in': out['w_in'], 'b_gate': out['b_gate'], 'w_conv': out['w_conv'], 'w_attn_out': out['w_attn_out'], 'w_conv_out': out['w_conv_out'], 'w_o': out['w_o'], 'g_post_mix': out['g_post_mix'], 'g_pre_mlp': out['g_pre_mlp'], 'w_up': out['w_up'], 'w_down': out['w_down'], 'g_post_mlp': out['g_post_mlp'], 'g_ple': out['g_ple'], 'w_ple_gate': out['w_ple_gate'], 'w_ple_proj': out['w_ple_proj'], 'loss_target': out['loss_target'], 'm_g_pre_mix': out['m_g_pre_mix'], 'm_w_in': out['m_w_in'], 'm_b_gate': out['m_b_gate'], 'm_w_conv': out['m_w_conv'], 'm_w_attn_out': out['m_w_attn_out'], 'm_w_conv_out': out['m_w_conv_out'], 'm_w_o': out['m_w_o'], 'm_g_post_mix': out['m_g_post_mix'], 'm_g_pre_mlp': out['m_g_pre_mlp'], 'm_w_up': out['m_w_up'], 'm_w_down': out['m_w_down'], 'm_g_post_mlp': out['m_g_post_mlp'], 'm_g_ple': out['m_g_ple'], 'm_w_ple_gate': out['m_w_ple_gate'], 'm_w_ple_proj': out['m_w_ple_proj'], 'v_g_pre_mix': out['v_g_pre_mix'], 'v_w_in': out['v_w_in'], 'v_b_gate': out['v_b_gate'], 'v_w_conv': out['v_w_conv'], 'v_w_attn_out': out['v_w_attn_out'], 'v_w_conv_out': out['v_w_conv_out'], 'v_w_o': out['v_w_o'], 'v_g_post_mix': out['v_g_post_mix'], 'v_g_pre_mlp': out['v_g_pre_mlp'], 'v_w_up': out['v_w_up'], 'v_w_down': out['v_w_down'], 'v_g_post_mlp': out['v_g_post_mlp'], 'v_g_ple': out['v_g_ple'], 'v_w_ple_gate': out['v_w_ple_gate'], 'v_w_ple_proj': out['v_w_ple_proj']}


def _loss(weights, diff, rest, loss_target):
    with _jax.named_scope("forward"):
        args = {**rest, TWIN_DIFF_INPUT: diff, **{k: w.astype(_WEIGHT_DTYPES[k]) for k, w in weights.items()}}
        y = _forward(args)
    with _jax.named_scope("loss_head"):
        err = _jnp.square(y.astype(_jnp.float32) - loss_target)
        return 0.5 * _jnp.sum(_jnp.mean(err, axis=-1)) if err.ndim else 0.5 * err


def _adamw(w, g, m, v):
    m = ADAM_B1 * m + (1.0 - ADAM_B1) * g
    v = ADAM_B2 * v + (1.0 - ADAM_B2) * _jnp.square(g)
    m_hat = m / (1.0 - ADAM_B1 ** ADAM_STEP)
    v_hat = v / (1.0 - ADAM_B2 ** ADAM_STEP)
    delta = -ADAM_LR * (m_hat / (_jnp.sqrt(v_hat) + ADAM_EPS) + ADAM_WD * w)
    return delta, m, v


def reference(x, p, g_pre_mix, w_in, b_gate, w_conv, w_attn_out, w_conv_out, w_o, g_post_mix, g_pre_mlp, w_up, w_down, g_post_mlp, g_ple, w_ple_gate, w_ple_proj, loss_target, m_g_pre_mix, m_w_in, m_b_gate, m_w_conv, m_w_attn_out, m_w_conv_out, m_w_o, m_g_post_mix, m_g_pre_mlp, m_w_up, m_w_down, m_g_post_mlp, m_g_ple, m_w_ple_gate, m_w_ple_proj, v_g_pre_mix, v_w_in, v_b_gate, v_w_conv, v_w_attn_out, v_w_conv_out, v_w_o, v_g_post_mix, v_g_pre_mlp, v_w_up, v_w_down, v_g_post_mlp, v_g_ple, v_w_ple_gate, v_w_ple_proj):
    given = dict(x=x, p=p, g_pre_mix=g_pre_mix, w_in=w_in, b_gate=b_gate, w_conv=w_conv, w_attn_out=w_attn_out, w_conv_out=w_conv_out, w_o=w_o, g_post_mix=g_post_mix, g_pre_mlp=g_pre_mlp, w_up=w_up, w_down=w_down, g_post_mlp=g_post_mlp, g_ple=g_ple, w_ple_gate=w_ple_gate, w_ple_proj=w_ple_proj, loss_target=loss_target, m_g_pre_mix=m_g_pre_mix, m_w_in=m_w_in, m_b_gate=m_b_gate, m_w_conv=m_w_conv, m_w_attn_out=m_w_attn_out, m_w_conv_out=m_w_conv_out, m_w_o=m_w_o, m_g_post_mix=m_g_post_mix, m_g_pre_mlp=m_g_pre_mlp, m_w_up=m_w_up, m_w_down=m_w_down, m_g_post_mlp=m_g_post_mlp, m_g_ple=m_g_ple, m_w_ple_gate=m_w_ple_gate, m_w_ple_proj=m_w_ple_proj, v_g_pre_mix=v_g_pre_mix, v_w_in=v_w_in, v_b_gate=v_b_gate, v_w_conv=v_w_conv, v_w_attn_out=v_w_attn_out, v_w_conv_out=v_w_conv_out, v_w_o=v_w_o, v_g_post_mix=v_g_post_mix, v_g_pre_mlp=v_g_pre_mlp, v_w_up=v_w_up, v_w_down=v_w_down, v_g_post_mlp=v_g_post_mlp, v_g_ple=v_g_ple, v_w_ple_gate=v_w_ple_gate, v_w_ple_proj=v_w_ple_proj)
    weights = {n: given[n] for n in TWIN_WEIGHTS}
    shared = {n: given[n] for n in SHARED_INPUTS}
    per_example = {n: given[n] for n in ['x', 'p']}
    grad_fn = _jax.value_and_grad(_loss, argnums=(0, 1))

    def one_microbatch(ex, loss_target):
        ex = dict(ex)
        diff = ex.pop(TWIN_DIFF_INPUT)
        return grad_fn(weights, diff, {**shared, **ex}, loss_target)

    if N_MICROBATCH == 1:
        loss, (grad_w, grad_x) = one_microbatch(per_example, given["loss_target"])
    else:
        def body(carry, xs):
            loss_sum, grad_sum = carry
            l_k, (gw_k, gx_k) = one_microbatch(xs[0], xs[1])
            with _jax.named_scope("update"):
                return (loss_sum + l_k, _jax.tree.map(_jnp.add, grad_sum, gw_k)), gx_k

        init = (_jnp.zeros((), _jnp.float32), _jax.tree.map(_jnp.zeros_like, weights))
        (loss, grad_w), grad_x = _jax.lax.scan(body, init, (per_example, given["loss_target"]))
    with _jax.named_scope("update"):
        delta_w, new_m, new_v = {}, {}, {}
        for n in TWIN_WEIGHTS:
            delta_w[n], new_m[n], new_v[n] = _adamw(weights[n], grad_w[n], given["m_" + n], given["v_" + n])
    return (loss, grad_x, *[grad_w[n] for n in TWIN_WEIGHTS], *[delta_w[n] for n in TWIN_WEIGHTS],
            *[new_m[n] for n in TWIN_WEIGHTS], *[new_v[n] for n in TWIN_WEIGHTS])
```

```python
import functools

import jax
import jax.numpy as jnp
from jax import lax
from jax.experimental import pallas as pl
from jax.experimental.pallas import tpu as pltpu

F32 = jnp.float32
BF16 = jnp.bfloat16
RMS_EPS = 1e-6
N_DEV = 8
MESH_AXES = ("x", "y", "c")
LANES = 128
HEAD_DIM = 64
HEADS_PER_GROUP = LANES // HEAD_DIM
CONV_K = 3
HALO = 8
VMEM_LIMIT = 56 * 1024 * 1024

ADAM_LR = 0.001
ADAM_B1 = 0.9
ADAM_B2 = 0.999
ADAM_EPS = 1e-08
ADAM_WD = 0.01
ADAM_STEP = 10

ROW_BLOCK = 256
ATTN_BLOCK = 256
DW_TOKENS = 512
DW_TILE = 1024
FF_CHUNK = 1024
PROJ_CHUNK = 512


def _dot(a, b):
    return lax.dot_general(a, b, (((1,), (0,)), ((), ())), preferred_element_type=F32)


def _dot_nt(a, b):
    return lax.dot_general(a, b, (((1,), (1,)), ((), ())), preferred_element_type=F32)


def _dot_tn(a, b):
    return lax.dot_general(a, b, (((0,), (0,)), ((), ())), preferred_element_type=F32)


def _sigmoid(z):
    return 1.0 / (1.0 + jnp.exp(-z))


def _rms_scale(x):
    return lax.rsqrt(jnp.mean(x * x, axis=-1, keepdims=True) + RMS_EPS)


def _rms_bwd(xhat, r, g, dy):
    gd = dy * g
    return r * (gd - xhat * jnp.mean(gd * xhat, axis=-1, keepdims=True))


def _params(n_axes, **kw):
    return pltpu.CompilerParams(dimension_semantics=("arbitrary",) * n_axes, vmem_limit_bytes=VMEM_LIMIT, **kw)


def _load_resident(pairs, sem):
    @pl.when(pl.program_id(0) == 0)
    def _():
        copies = [pltpu.make_async_copy(src, dst, sem.at[i]) for i, (src, dst) in enumerate(pairs)]
        for cp in copies:
            cp.start()
        for cp in copies:
            cp.wait()


def _row_spec(tm, width):
    return pl.BlockSpec((tm, width), lambda i: (i, 0))


def _col_spec(tm, width, col):
    return pl.BlockSpec((tm, width), lambda i: (i, col))


def _prev_halo_spec(tm, width, col=0):
    per = tm // HALO
    return pl.BlockSpec((HALO, width), lambda i: (jnp.maximum(i * per - 1, 0), col))


def _next_halo_spec(tm, width, n_rows):
    per = tm // HALO
    last = n_rows // HALO - 1
    return pl.BlockSpec((HALO, width), lambda i: (jnp.minimum((i + 1) * per, last), 0))


def _const_spec(shape):
    return pl.BlockSpec(shape, lambda i: (0,) * len(shape))


ANY = pl.BlockSpec(memory_space=pl.ANY)


def _shift_down(cur, prev, n):
    rows = lax.broadcasted_iota(jnp.int32, cur.shape, 0)
    out = pltpu.roll(cur, n, 0)
    for j in range(n):
        out = jnp.where(rows == j, prev[HALO - n + j:HALO - n + j + 1, :], out)
    return out


def _shift_up(cur, nxt, n):
    tm = cur.shape[0]
    rows = lax.broadcasted_iota(jnp.int32, cur.shape, 0)
    out = pltpu.roll(cur, tm - n, 0)
    for j in range(n):
        out = jnp.where(rows == tm - n + j, nxt[j:j + 1, :], out)
    return out


def _conv_taps(cm, cm_prev, wconv):
    cm1 = _shift_down(cm, cm_prev, 1)
    cm2 = _shift_down(cm, cm_prev, 2)
    cv = wconv[2:3, :] * cm + wconv[1:2, :] * cm1 + wconv[0:1, :] * cm2
    return cv, cm1, cm2


def _in_proj_fwd(x, g1, b_gate, w_in, aw, cw, tm):
    s, d = x.shape
    ni = w_in.shape[1]
    n_qkv, n_conv = 3 * aw, 3 * cw
    ch = PROJ_CHUNK

    def body(x_ref, g_ref, b_ref, w_hbm, qkv_ref, conv_ref, gate_ref, w_vmem, sem):
        _load_resident([(w_hbm, w_vmem)], sem)
        xv = x_ref[...]
        h = (xv * _rms_scale(xv) * g_ref[...]).astype(BF16)
        for c0 in range(0, ni, ch):
            pc = _dot(h, w_vmem[:, c0:c0 + ch])
            if c0 < n_qkv:
                qkv_ref[:, c0:c0 + ch] = pc.astype(BF16)
            elif c0 < n_qkv + n_conv:
                conv_ref[:, c0 - n_qkv:c0 - n_qkv + ch] = pc
            else:
                g0 = c0 - n_qkv - n_conv
                gate_ref[:, g0:g0 + ch] = _sigmoid(pc + b_ref[:, g0:g0 + ch])

    return pl.pallas_call(
        body, name="in_proj_fwd", grid=(s // tm,),
        in_specs=[_row_spec(tm, d), _const_spec((1, d)), _const_spec((1, 2 * d)), ANY],
        out_specs=[_row_spec(tm, n_qkv), _row_spec(tm, n_conv), _row_spec(tm, 2 * d)],
        out_shape=[jax.ShapeDtypeStruct((s, n_qkv), BF16), jax.ShapeDtypeStruct((s, n_conv), F32),
                   jax.ShapeDtypeStruct((s, 2 * d), F32)],
        scratch_shapes=[pltpu.VMEM((d, ni), BF16), pltpu.SemaphoreType.DMA((1,))],
        compiler_params=_params(1),
    )(x, g1, b_gate, w_in)


def _split_hi_lo(a):
    hi = a.astype(BF16)
    return hi, (a - hi.astype(F32)).astype(BF16)


def _log_gates(z):
    t = jnp.exp(-jnp.abs(z))
    u = 1.0 + t
    sp = jnp.log(u)
    return jnp.minimum(z, 0.0) - sp, jnp.minimum(-z, 0.0) - sp, t, u


def _attn_masks(t):
    row = lax.broadcasted_iota(jnp.int32, (t, t), 0)
    col = lax.broadcasted_iota(jnp.int32, (t, t), 1)
    return col < row, (row > col).astype(BF16), (row >= col).astype(BF16)


def _head_lanes(h):
    lane = lax.broadcasted_iota(jnp.int32, (1, LANES), 1)
    return (lane >= HEAD_DIM * h) & (lane < HEAD_DIM * (h + 1))


def _attn_fwd(qkv, aw, t):
    s = qkv.shape[0]
    groups = aw // LANES
    scale = HEAD_DIM ** -0.5

    def body(q_ref, k_ref, v_ref, o_ref):
        qi = pl.program_id(1)
        causal, upper, _ = _attn_masks(t)
        q = q_ref[...] * scale
        outs = []
        for h in range(HEADS_PER_GROUP):
            qm = jnp.where(_head_lanes(h), q, jnp.zeros_like(q))

            def block(kb, run, acc, diag):
                rows = pl.ds(pl.multiple_of(kb * t, t), t)
                z = _dot_nt(qm, k_ref[rows, :])
                log_b, log_keep, _, _ = _log_gates(z)
                if diag:
                    log_keep = jnp.where(causal, log_keep, 0.0)
                hi, lo = _split_hi_lo(log_keep)
                between = _dot(hi, upper) + _dot(lo, upper) + run
                w = jnp.exp(log_b + between)
                if diag:
                    w = jnp.where(causal, w, 0.0)
                acc = acc + _dot(w.astype(BF16), v_ref[rows, :])
                return run + jnp.sum(log_keep, axis=1, keepdims=True), acc

            run, acc = block(qi, jnp.zeros((t, 1), F32), jnp.zeros((t, LANES), F32), True)

            def step(j, carry):
                return block(qi - 1 - j, *carry, False)

            _, acc = lax.fori_loop(0, qi, step, (run, acc))
            outs.append(acc)
        o_ref[...] = jnp.where(_head_lanes(0), outs[0], outs[1])

    return pl.pallas_call(
        body, name="attn_fwd", grid=(groups, s // t),
        in_specs=[pl.BlockSpec((t, LANES), lambda g, i: (i, g)),
                  pl.BlockSpec((s, LANES), lambda g, i: (0, groups + g)),
                  pl.BlockSpec((s, LANES), lambda g, i: (0, 2 * groups + g))],
        out_specs=pl.BlockSpec((t, LANES), lambda g, i: (i, g)),
        out_shape=jax.ShapeDtypeStruct((s, aw), F32),
        compiler_params=_params(2),
    )(qkv, qkv, qkv)


def _attn_bwd(qkv, o, do, aw, t):
    s = qkv.shape[0]
    groups = aw // LANES
    nq = s // t
    scale = HEAD_DIM ** -0.5

    def body(q_ref, k_ref, v_ref, o_ref, do_ref, dq_ref, dk_ref, dv_ref, dk_acc, dv_acc):
        qi = pl.program_id(1)

        @pl.when(qi == 0)
        def _():
            dk_acc[...] = jnp.zeros_like(dk_acc)
            dv_acc[...] = jnp.zeros_like(dv_acc)

        causal, upper, lower_incl = _attn_masks(t)
        q = q_ref[...] * scale
        do_b = do_ref[...]
        do_o = do_b.astype(F32) * o_ref[...]
        dqs = []
        for h in range(HEADS_PER_GROUP):
            lanes = _head_lanes(h)
            qm = jnp.where(lanes, q, jnp.zeros_like(q))
            dom = jnp.where(lanes, do_b, jnp.zeros_like(do_b))
            e_total = jnp.sum(jnp.where(lanes, do_o, 0.0), axis=1, keepdims=True)

            def block(kb, run, e_run, dq, diag):
                rows = pl.ds(pl.multiple_of(kb * t, t), t)
                k = k_ref[rows, :]
                v = v_ref[rows, :]
                z = _dot_nt(qm, k)
                log_b, log_keep, tt, u = _log_gates(z)
                r = 1.0 / u
                beta = jnp.where(z >= 0.0, r, tt * r)
                keep = jnp.where(z >= 0.0, tt * r, r)
                if diag:
                    log_keep = jnp.where(causal, log_keep, 0.0)
                hi, lo = _split_hi_lo(log_keep)
                between = _dot(hi, upper) + _dot(lo, upper) + run
                w = jnp.exp(log_b + between)
                if diag:
                    w = jnp.where(causal, w, 0.0)
                wb = w.astype(BF16)
                e = _dot_nt(dom, v) * wb.astype(F32)
                hi, lo = _split_hi_lo(e)
                e_suffix = _dot(hi, lower_incl) + _dot(lo, lower_incl) + e_run
                dz = e * keep - (e_total - e_suffix) * beta
                if diag:
                    dz = jnp.where(causal, dz, 0.0)
                dzb = dz.astype(BF16)
                dk_acc[rows, :] += _dot_tn(dzb, qm)
                dv_acc[rows, :] += _dot_tn(wb, dom)
                return (run + jnp.sum(log_keep, axis=1, keepdims=True), e_suffix[:, 0:1], dq + _dot(dzb, k))

            zero_col = jnp.zeros((t, 1), F32)
            run, e_run, dq = block(qi, zero_col, zero_col, jnp.zeros((t, LANES), F32), True)

            def step(j, carry):
                return block(qi - 1 - j, *carry, False)

            _, _, dq = lax.fori_loop(0, qi, step, (run, e_run, dq))
            dqs.append(dq)
        dq_ref[...] = (jnp.where(_head_lanes(0), dqs[0], dqs[1]) * scale).astype(BF16)

        @pl.when(qi == nq - 1)
        def _():
            dk_ref[...] = dk_acc[...].astype(BF16)
            dv_ref[...] = dv_acc[...].astype(BF16)

    blk = pl.BlockSpec((t, LANES), lambda g, i: (i, g))
    slab = pl.BlockSpec((s, LANES), lambda g, i: (0, g))
    return pl.pallas_call(
        body, name="attn_bwd", grid=(groups, nq),
        in_specs=[blk, pl.BlockSpec((s, LANES), lambda g, i: (0, groups + g)),
                  pl.BlockSpec((s, LANES), lambda g, i: (0, 2 * groups + g)), blk, blk],
        out_specs=[blk, slab, slab],
        out_shape=[jax.ShapeDtypeStruct((s, aw), BF16)] * 3,
        scratch_shapes=[pltpu.VMEM((s, LANES), F32), pltpu.VMEM((s, LANES), F32)],
        compiler_params=_params(2),
    )(qkv, qkv, qkv, o, do)


def _branches(o_b, conv, conv_prev, wconv, w_ao, w_co, cw, first):
    cb = conv[:, 0:cw]
    cm = conv[:, cw:2 * cw] * conv[:, 2 * cw:3 * cw]
    cm_prev = conv_prev[:, cw:2 * cw] * conv_prev[:, 2 * cw:3 * cw]
    cm_prev = jnp.where(first, 0.0, cm_prev)
    cv, cm1, cm2 = _conv_taps(cm, cm_prev, wconv)
    conv_in = (cb * cv).astype(BF16)
    return _dot(o_b, w_ao), _dot(conv_in, w_co), conv_in, cb, cv, cm, cm1, cm2


def _mix_fwd(x, o, conv, gate, wconv, g_post, w_ao, w_co, w_o, tm):
    s, d = x.shape
    aw, cw = w_ao.shape[0], w_co.shape[0]

    def body(x_ref, o_ref, conv_ref, prev_ref, gate_ref, wc_ref, g_ref, wao_hbm, wco_hbm, wo_hbm,
             x1_ref, mixed_ref, mixin_ref, convin_ref, wao, wco, wo, sem):
        _load_resident([(wao_hbm, wao), (wco_hbm, wco), (wo_hbm, wo)], sem)
        y_attn, y_conv, conv_in, *_ = _branches(
            o_ref[...].astype(BF16), conv_ref[...], prev_ref[...], wc_ref[...], wao[...], wco[...], cw,
            pl.program_id(0) == 0)
        mix_in = (gate_ref[:, 0:d] * y_attn + gate_ref[:, d:2 * d] * y_conv).astype(BF16)
        mixed = _dot(mix_in, wo[...])
        x1_ref[...] = x_ref[...] + mixed * _rms_scale(mixed) * g_ref[...]
        mixed_ref[...] = mixed
        mixin_ref[...] = mix_in
        convin_ref[...] = conv_in

    return pl.pallas_call(
        body, name="mix_fwd", grid=(s // tm,),
        in_specs=[_row_spec(tm, d), _row_spec(tm, aw), _row_spec(tm, 3 * cw), _prev_halo_spec(tm, 3 * cw),
                  _row_spec(tm, 2 * d), _const_spec((CONV_K, cw)), _const_spec((1, d)), ANY, ANY, ANY],
        out_specs=[_row_spec(tm, d), _row_spec(tm, d), _row_spec(tm, d), _row_spec(tm, cw)],
        out_shape=[jax.ShapeDtypeStruct((s, d), F32), jax.ShapeDtypeStruct((s, d), F32),
                   jax.ShapeDtypeStruct((s, d), BF16), jax.ShapeDtypeStruct((s, cw), BF16)],
        scratch_shapes=[pltpu.VMEM(w_ao.shape, BF16), pltpu.VMEM(w_co.shape, BF16), pltpu.VMEM(w_o.shape, BF16),
                        pltpu.SemaphoreType.DMA((3,))],
        compiler_params=_params(1),
    )(x, o, conv, conv, gate, wconv, g_post, w_ao, w_co, w_o)


def _mix_bwd(dx1, mixed, o, conv, gate, wconv, g_post, w_ao, w_co, w_o, tm):
    s, d = dx1.shape
    aw, cw = w_ao.shape[0], w_co.shape[0]

    def body(dx1_ref, mixed_ref, o_ref, conv_ref, prev_ref, gate_ref, wc_ref, g_ref, wao_hbm, wco_hbm, wo_hbm,
             dmixed_ref, dattn_ref, dconvout_ref, do_ref, dgate_ref, dcb_ref, dcv_ref, dg_ref, dbias_ref,
             wao, wco, wo, sem):
        i = pl.program_id(0)
        _load_resident([(wao_hbm, wao), (wco_hbm, wco), (wo_hbm, wo)], sem)

        @pl.when(i == 0)
        def _():
            dg_ref[...] = jnp.zeros_like(dg_ref)
            dbias_ref[...] = jnp.zeros_like(dbias_ref)

        mixed = mixed_ref[...]
        r = _rms_scale(mixed)
        mhat = mixed * r
        dn = dx1_ref[...]
        dg_ref[...] += jnp.sum(dn * mhat, axis=0, keepdims=True)
        dmixed = _rms_bwd(mhat, r, g_ref[...], dn).astype(BF16)
        dmixed_ref[...] = dmixed
        dmi = _dot_nt(dmixed, wo[...])

        y_attn, y_conv, _, cb, cv, *_ = _branches(
            o_ref[...].astype(BF16), conv_ref[...], prev_ref[...], wc_ref[...], wao[...], wco[...], cw, i == 0)
        ga = gate_ref[:, 0:d]
        gc = gate_ref[:, d:2 * d]
        dpre_a = dmi * y_attn * ga * (1.0 - ga)
        dpre_c = dmi * y_conv * gc * (1.0 - gc)
        dgate_ref[:, 0:d] = dpre_a.astype(BF16)
        dgate_ref[:, d:2 * d] = dpre_c.astype(BF16)
        dbias_ref[:, 0:d] += jnp.sum(dpre_a, axis=0, keepdims=True)
        dbias_ref[:, d:2 * d] += jnp.sum(dpre_c, axis=0, keepdims=True)

        dattn = (dmi * ga).astype(BF16)
        dattn_ref[...] = dattn
        do_ref[...] = _dot_nt(dattn, wao[...]).astype(BF16)
        dconvout = (dmi * gc).astype(BF16)
        dconvout_ref[...] = dconvout
        dconv_in = _dot_nt(dconvout, wco[...])
        dcb_ref[...] = (dconv_in * cv).astype(BF16)
        dcv_ref[...] = dconv_in * cb

    return pl.pallas_call(
        body, name="mix_bwd", grid=(s // tm,),
        in_specs=[_row_spec(tm, d), _row_spec(tm, d), _row_spec(tm, aw), _row_spec(tm, 3 * cw),
                  _prev_halo_spec(tm, 3 * cw), _row_spec(tm, 2 * d), _const_spec((CONV_K, cw)), _const_spec((1, d)),
                  ANY, ANY, ANY],
        out_specs=[_row_spec(tm, d), _row_spec(tm, d), _row_spec(tm, d), _row_spec(tm, aw), _row_spec(tm, 2 * d),
                   _row_spec(tm, cw), _row_spec(tm, cw), _const_spec((1, d)), _const_spec((1, 2 * d))],
        out_shape=[jax.ShapeDtypeStruct((s, d), BF16), jax.ShapeDtypeStruct((s, d), BF16),
                   jax.ShapeDtypeStruct((s, d), BF16), jax.ShapeDtypeStruct((s, aw), BF16),
                   jax.ShapeDtypeStruct((s, 2 * d), BF16), jax.ShapeDtypeStruct((s, cw), BF16),
                   jax.ShapeDtypeStruct((s, cw), F32), jax.ShapeDtypeStruct((1, d), F32),
                   jax.ShapeDtypeStruct((1, 2 * d), F32)],
        scratch_shapes=[pltpu.VMEM(w_ao.shape, BF16), pltpu.VMEM(w_co.shape, BF16), pltpu.VMEM(w_o.shape, BF16),
                        pltpu.SemaphoreType.DMA((3,))],
        compiler_params=_params(1),
    )(dx1, mixed, o, conv, conv, gate, wconv, g_post, w_ao, w_co, w_o)


def _mlp_ple_loss(x1, p, target, g_pre, g_post, g_ple, w_up, w_dn, w_pg, w_pp, tm):
    s, d = x1.shape
    ff = w_up.shape[1]
    pd = p.shape[1]
    fc = FF_CHUNK

    def body(x1_ref, p_ref, t_ref, gpre_ref, gpost_ref, gple_ref, wup_hbm, wdn_hbm, wpg_hbm, wpp_hbm,
             dx1_ref, h2_ref, du_ref, a_ref, df_ref, h3_ref, ds3_ref, dpp_ref, loss_ref, dgpre_ref, dgpost_ref,
             dgple_ref, wup, wdn, wpg, wpp, u_scr, sem):
        _load_resident([(wup_hbm, wup), (wdn_hbm, wdn), (wpg_hbm, wpg), (wpp_hbm, wpp)], sem)

        @pl.when(pl.program_id(0) == 0)
        def _():
            for ref in (loss_ref, dgpre_ref, dgpost_ref, dgple_ref):
                ref[...] = jnp.zeros_like(ref)

        x1v = x1_ref[...]
        r2 = _rms_scale(x1v)
        x1hat = x1v * r2
        h2 = (x1hat * gpre_ref[...]).astype(BF16)
        h2_ref[...] = h2
        f = jnp.zeros((tm, d), F32)
        for c0 in range(0, ff, fc):
            u = _dot(h2, wup[:, c0:c0 + fc])
            u_scr[:, c0:c0 + fc] = u
            a = jnp.square(jnp.maximum(u, 0.0)).astype(BF16)
            a_ref[:, c0:c0 + fc] = a
            f = f + _dot(a, wdn[c0:c0 + fc, :])
        rf = _rms_scale(f)
        fhat = f * rf
        x2 = x1v + fhat * gpost_ref[...]
        r3 = _rms_scale(x2)
        x2hat = x2 * r3
        h3 = (x2hat * gple_ref[...]).astype(BF16)
        h3_ref[...] = h3
        pg = _sigmoid(_dot(h3, wpg[...]))
        pp = _dot(p_ref[...].astype(BF16), wpp[...])
        diff = x2 + pg * pp - t_ref[...]
        loss_ref[...] += 0.5 * jnp.sum(jnp.mean(diff * diff, axis=-1, keepdims=True), axis=0, keepdims=True)

        dy = diff * (1.0 / d)
        dpp_ref[...] = (dy * pg).astype(BF16)
        ds3 = (dy * pp * pg * (1.0 - pg)).astype(BF16)
        ds3_ref[...] = ds3
        dh3 = _dot_nt(ds3, wpg[...])
        dgple_ref[...] += jnp.sum(dh3 * x2hat, axis=0, keepdims=True)
        dx2 = dy + _rms_bwd(x2hat, r3, gple_ref[...], dh3)
        dgpost_ref[...] += jnp.sum(dx2 * fhat, axis=0, keepdims=True)
        df = _rms_bwd(fhat, rf, gpost_ref[...], dx2).astype(BF16)
        df_ref[...] = df
        dh2 = jnp.zeros((tm, d), F32)
        for c0 in range(0, ff, fc):
            da = _dot_nt(df, wdn[c0:c0 + fc, :])
            du = (da * (2.0 * jnp.maximum(u_scr[:, c0:c0 + fc], 0.0))).astype(BF16)
            du_ref[:, c0:c0 + fc] = du
            dh2 = dh2 + _dot_nt(du, wup[:, c0:c0 + fc])
        dgpre_ref[...] += jnp.sum(dh2 * x1hat, axis=0, keepdims=True)
        dx1_ref[...] = dx2 + _rms_bwd(x1hat, r2, gpre_ref[...], dh2)

    vec = _const_spec((1, d))
    return pl.pallas_call(
        body, name="mlp_ple_loss", grid=(s // tm,),
        in_specs=[_row_spec(tm, d), _row_spec(tm, pd), _row_spec(tm, d), vec, vec, vec, ANY, ANY, ANY, ANY],
        out_specs=[_row_spec(tm, d), _row_spec(tm, d), _row_spec(tm, ff), _row_spec(tm, ff), _row_spec(tm, d),
                   _row_spec(tm, d), _row_spec(tm, d), _row_spec(tm, d), _const_spec((1, 1)), vec, vec, vec],
        out_shape=[jax.ShapeDtypeStruct((s, d), F32), jax.ShapeDtypeStruct((s, d), BF16),
                   jax.ShapeDtypeStruct((s, ff), BF16), jax.ShapeDtypeStruct((s, ff), BF16),
                   jax.ShapeDtypeStruct((s, d), BF16), jax.ShapeDtypeStruct((s, d), BF16),
                   jax.ShapeDtypeStruct((s, d), BF16), jax.ShapeDtypeStruct((s, d), BF16),
                   jax.ShapeDtypeStruct((1, 1), F32), jax.ShapeDtypeStruct((1, d), F32),
                   jax.ShapeDtypeStruct((1, d), F32), jax.ShapeDtypeStruct((1, d), F32)],
        scratch_shapes=[pltpu.VMEM(w_up.shape, BF16), pltpu.VMEM(w_dn.shape, BF16), pltpu.VMEM(w_pg.shape, BF16),
                        pltpu.VMEM(w_pp.shape, BF16), pltpu.VMEM((tm, ff), F32), pltpu.SemaphoreType.DMA((4,))],
        compiler_params=_params(1),
    )(x1, p, target, g_pre, g_post, g_ple, w_up, w_dn, w_pg, w_pp)


def _in_proj_bwd(x, dx1, dq, dk, dv, dcb, dcv, conv, dgate, wconv, g1, w_in, tm):
    s, d = x.shape
    aw, cw = dq.shape[1], dcb.shape[1]
    ni = w_in.shape[1]

    def body(x_ref, dx1_ref, dq_ref, dk_ref, dv_ref, dcb_ref, dcv_ref, dcvn_ref, cc_ref, cu_ref, ccp_ref, cup_ref,
             dgate_ref, wc_ref, g_ref, w_hbm, dx_ref, dproj_ref, h1_ref, dg_ref, dwc_ref, w_vmem, sem):
        i = pl.program_id(0)
        _load_resident([(w_hbm, w_vmem)], sem)

        @pl.when(i == 0)
        def _():
            dg_ref[...] = jnp.zeros_like(dg_ref)
            dwc_ref[...] = jnp.zeros_like(dwc_ref)

        wc = wc_ref[...]
        cc = cc_ref[...]
        cu = cu_ref[...]
        cm = cc * cu
        cm_prev = jnp.where(i == 0, 0.0, ccp_ref[...] * cup_ref[...])
        _, cm1, cm2 = _conv_taps(cm, cm_prev, wc)
        dcv_cur = dcv_ref[...]
        dcv_next = jnp.where(i == pl.num_programs(0) - 1, 0.0, dcvn_ref[...])
        dcm = (wc[2:3, :] * dcv_cur + wc[1:2, :] * _shift_up(dcv_cur, dcv_next, 1)
               + wc[0:1, :] * _shift_up(dcv_cur, dcv_next, 2))
        for tap, shifted in enumerate((cm2, cm1, cm)):
            dwc_ref[tap:tap + 1, :] += jnp.sum(dcv_cur * shifted, axis=0, keepdims=True)

        pieces = [(dq_ref[...], aw), (dk_ref[...], aw), (dv_ref[...], aw), (dcb_ref[...], cw),
                  ((dcm * cu).astype(BF16), cw), ((dcm * cc).astype(BF16), cw), (dgate_ref[...], 2 * d)]
        dh = jnp.zeros((tm, d), F32)
        c0 = 0
        for piece, width in pieces:
            dproj_ref[:, c0:c0 + width] = piece
            dh = dh + _dot_nt(piece, w_vmem[:, c0:c0 + width])
            c0 += width

        xv = x_ref[...]
        r = _rms_scale(xv)
        xhat = xv * r
        h1_ref[...] = (xhat * g_ref[...]).astype(BF16)
        dg_ref[...] += jnp.sum(dh * xhat, axis=0, keepdims=True)
        dx_ref[...] = dx1_ref[...] + _rms_bwd(xhat, r, g_ref[...], dh)

    return pl.pallas_call(
        body, name="in_proj_bwd", grid=(s // tm,),
        in_specs=[_row_spec(tm, d), _row_spec(tm, d), _row_spec(tm, aw), _row_spec(tm, aw), _row_spec(tm, aw),
                  _row_spec(tm, cw), _row_spec(tm, cw), _next_halo_spec(tm, cw, s), _col_spec(tm, cw, 1),
                  _col_spec(tm, cw, 2), _prev_halo_spec(tm, cw, 1), _prev_halo_spec(tm, cw, 2), _row_spec(tm, 2 * d),
                  _const_spec((CONV_K, cw)), _const_spec((1, d)), ANY],
        out_specs=[_row_spec(tm, d), _row_spec(tm, ni), _row_spec(tm, d), _const_spec((1, d)),
                   _const_spec((CONV_K, cw))],
        out_shape=[jax.ShapeDtypeStruct((s, d), F32), jax.ShapeDtypeStruct((s, ni), BF16),
                   jax.ShapeDtypeStruct((s, d), BF16), jax.ShapeDtypeStruct((1, d), F32),
                   jax.ShapeDtypeStruct((CONV_K, cw), F32)],
        scratch_shapes=[pltpu.VMEM((d, ni), BF16), pltpu.SemaphoreType.DMA((1,))],
        compiler_params=_params(1),
    )(x, dx1, dq, dk, dv, dcb, dcv, dcv, conv, conv, conv, conv, dgate, wconv, g1, w_in)


def _weight_grad(a, b, name):
    s, m = a.shape
    n = b.shape[1]
    tm, tn, tk = min(m, DW_TILE), min(n, DW_TILE), min(s, DW_TOKENS)
    nk = s // tk

    def body(a_ref, b_ref, o_ref, acc):
        k = pl.program_id(2)

        @pl.when(k == 0)
        def _():
            acc[...] = jnp.zeros_like(acc)

        acc[...] += _dot_tn(a_ref[...].astype(BF16), b_ref[...].astype(BF16))

        @pl.when(k == nk - 1)
        def _():
            o_ref[...] = acc[...].astype(BF16)

    return pl.pallas_call(
        body, name=name, grid=(m // tm, n // tn, nk),
        in_specs=[pl.BlockSpec((tk, tm), lambda i, j, k: (k, i)), pl.BlockSpec((tk, tn), lambda i, j, k: (k, j))],
        out_specs=pl.BlockSpec((tm, tn), lambda i, j, k: (i, j)),
        out_shape=jax.ShapeDtypeStruct((m, n), BF16),
        scratch_shapes=[pltpu.VMEM((tm, tn), F32)],
        compiler_params=_params(3),
    )(a, b)


def _mesh_position():
    return tuple(lax.axis_index(a) for a in MESH_AXES)


def _peer(me, k):
    bits = ((k >> 2) & 1, (k >> 1) & 1, k & 1)
    pos = tuple(1 - m if b else m for m, b in zip(me, bits))
    return pos, 4 * pos[0] + 2 * pos[1] + pos[2]


def _gather_weights(shards, col_sharded, conv_tile):
    n = len(shards)
    full_shapes = [(r, N_DEV * c) if by_col else (N_DEV * r, c) for (r, c), by_col in
                   zip((a.shape for a in shards), col_sharded)]

    def body(*refs):
        ins, conv_in = refs[:n], refs[n]
        outs, conv_out = refs[n + 1:2 * n + 1], refs[2 * n + 1]
        stage = refs[2 * n + 2:3 * n + 2]
        send_sems, recv_sems, local_sems = refs[3 * n + 2:]
        me = _mesh_position()
        mine = 4 * me[0] + 2 * me[1] + me[2]
        for src, dst in zip(ins, stage):
            dst[...] = src[...].astype(BF16)

        def my_block(w):
            if w == n:
                return conv_out.at[mine]
            r, c = shards[w].shape
            if col_sharded[w]:
                return outs[w].at[:, pl.ds(pl.multiple_of(mine * c, LANES), c)]
            return outs[w].at[pl.ds(pl.multiple_of(mine * r, 16), r), :]

        srcs = list(stage) + [conv_in]
        copies = []
        for w, src in enumerate(srcs):
            local = pltpu.make_async_copy(src, my_block(w), local_sems.at[w])
            local.start()
            copies.append(local)
            for k in range(1, N_DEV):
                peer, _ = _peer(me, k)
                cp = pltpu.make_async_remote_copy(
                    src_ref=src, dst_ref=my_block(w), send_sem=send_sems.at[w, k - 1],
                    recv_sem=recv_sems.at[w, k - 1], device_id=peer, device_id_type=pl.DeviceIdType.MESH)
                cp.start()
                copies.append(cp)
        for cp in copies:
            cp.wait()

    vmem = pl.BlockSpec(memory_space=pltpu.VMEM)
    all_in = list(shards) + [conv_tile]
    return pl.pallas_call(
        body, name="gather_weights",
        in_specs=[vmem] * (n + 1), out_specs=[ANY] * (n + 1),
        out_shape=[jax.ShapeDtypeStruct(sh, BF16) for sh in full_shapes]
        + [jax.ShapeDtypeStruct((N_DEV,) + conv_tile.shape, F32)],
        scratch_shapes=[pltpu.VMEM(a.shape, BF16) for a in shards]
        + [pltpu.SemaphoreType.DMA((n + 1, N_DEV - 1)), pltpu.SemaphoreType.DMA((n + 1, N_DEV - 1)),
           pltpu.SemaphoreType.DMA((n + 1,))],
        compiler_params=pltpu.CompilerParams(vmem_limit_bytes=VMEM_LIMIT),
    )(*all_in)


def _scatter_grads(grads, col_sharded, small):
    n = len(grads)
    shard_shapes = []
    for g, by_col in zip(grads, col_sharded):
        r, c = g.shape
        shard_shapes.append((r, c // N_DEV) if by_col else (r // N_DEV, c))

    def body(*refs):
        ins, small_in = refs[:n], refs[n]
        outs, small_out = refs[n + 1:2 * n + 1], refs[2 * n + 1]
        send_sems, recv_sems, local_sems = refs[2 * n + 2:]
        me = _mesh_position()
        mine = 4 * me[0] + 2 * me[1] + me[2]

        def block(w, idx):
            r, c = shard_shapes[w]
            if col_sharded[w]:
                return ins[w].at[:, pl.ds(pl.multiple_of(idx * c, LANES), c)]
            return ins[w].at[pl.ds(pl.multiple_of(idx * r, 16), r), :]

        copies = []
        for w in range(n + 1):
            src_of = (lambda idx, w=w: block(w, idx)) if w < n else (lambda idx: small_in)
            dst = outs[w] if w < n else small_out
            local = pltpu.make_async_copy(src_of(mine), dst.at[mine], local_sems.at[w])
            local.start()
            copies.append(local)
            for k in range(1, N_DEV):
                peer, peer_idx = _peer(me, k)
                cp = pltpu.make_async_remote_copy(
                    src_ref=src_of(peer_idx), dst_ref=dst.at[mine], send_sem=send_sems.at[w, k - 1],
                    recv_sem=recv_sems.at[w, k - 1], device_id=peer, device_id_type=pl.DeviceIdType.MESH)
                cp.start()
                copies.append(cp)
        for cp in copies:
            cp.wait()

    return pl.pallas_call(
        body, name="scatter_grads",
        in_specs=[ANY] * (n + 1), out_specs=[ANY] * (n + 1),
        out_shape=[jax.ShapeDtypeStruct((N_DEV,) + sh, BF16) for sh in shard_shapes]
        + [jax.ShapeDtypeStruct((N_DEV,) + small.shape, F32)],
        scratch_shapes=[pltpu.SemaphoreType.DMA((n + 1, N_DEV - 1)), pltpu.SemaphoreType.DMA((n + 1, N_DEV - 1)),
                        pltpu.SemaphoreType.DMA((n + 1,))],
        compiler_params=pltpu.CompilerParams(vmem_limit_bytes=VMEM_LIMIT),
    )(*grads, small)


def _adamw(w, g, m, v):
    m = ADAM_B1 * m + (1.0 - ADAM_B1) * g
    v = ADAM_B2 * v + (1.0 - ADAM_B2) * jnp.square(g)
    m_hat = m / (1.0 - ADAM_B1 ** ADAM_STEP)
    v_hat = v / (1.0 - ADAM_B2 ** ADAM_STEP)
    delta = -ADAM_LR * (m_hat / (jnp.sqrt(v_hat) + ADAM_EPS) + ADAM_WD * w)
    return delta, m, v


def _sum_and_adamw(parts, w, m, v, name):
    r, c = w.shape
    tr = min(r, 256)

    def body(p_ref, w_ref, m_ref, v_ref, g_out, d_out, m_out, v_out):
        g = p_ref[0].astype(F32)
        for dev in range(1, N_DEV):
            g = g + p_ref[dev].astype(F32)
        g_out[...] = g
        d_out[...], m_out[...], v_out[...] = _adamw(w_ref[...], g, m_ref[...], v_ref[...])

    blk = pl.BlockSpec((tr, c), lambda i: (i, 0))
    return pl.pallas_call(
        body, name=name, grid=(r // tr,),
        in_specs=[pl.BlockSpec((N_DEV, tr, c), lambda i: (0, i, 0)), blk, blk, blk],
        out_specs=[blk] * 4, out_shape=[jax.ShapeDtypeStruct((r, c), F32)] * 4,
        compiler_params=_params(1),
    )(parts, w, m, v)


BIG = ("w_in", "w_attn_out", "w_conv_out", "w_o", "w_up", "w_down", "w_ple_gate", "w_ple_proj")
COL_SHARDED = {"w_in": True, "w_attn_out": True, "w_conv_out": True, "w_o": False, "w_up": True, "w_down": False,
               "w_ple_gate": False, "w_ple_proj": True}
SMALL = ("g_pre_mix", "b_gate", "g_post_mix", "g_pre_mlp", "g_post_mlp", "g_ple")


def _local_grads(x, p, target, small, wconv, full, tm, t):
    aw, cw = full["w_attn_out"].shape[0], full["w_conv_out"].shape[0]
    qkv, conv, gate = _in_proj_fwd(x, small["g_pre_mix"], small["b_gate"], full["w_in"], aw, cw, tm)
    o = _attn_fwd(qkv, aw, t)
    x1, mixed, mix_in, conv_in = _mix_fwd(x, o, conv, gate, wconv, small["g_post_mix"], full["w_attn_out"],
                                          full["w_conv_out"], full["w_o"], tm)
    (dx1, h2, du, a, df, h3, ds3, dpp, loss, dg_pre_mlp, dg_post_mlp, dg_ple) = _mlp_ple_loss(
        x1, p, target, small["g_pre_mlp"], small["g_post_mlp"], small["g_ple"], full["w_up"], full["w_down"],
        full["w_ple_gate"], full["w_ple_proj"], tm)
    (dmixed, dattn, dconvout, do, dgate, dcb, dcv, dg_post_mix, db_gate) = _mix_bwd(
        dx1, mixed, o, conv, gate, wconv, small["g_post_mix"], full["w_attn_out"], full["w_conv_out"], full["w_o"],
        tm)
    dq, dk, dv = _attn_bwd(qkv, o, do, aw, t)
    dx, dproj, h1, dg_pre_mix, dwconv = _in_proj_bwd(x, dx1, dq, dk, dv, dcb, dcv, conv, dgate, wconv,
                                                    small["g_pre_mix"], full["w_in"], tm)
    big = {
        "w_in": _weight_grad(h1, dproj, "dw_in"),
        "w_attn_out": _weight_grad(o, dattn, "dw_attn_out"),
        "w_conv_out": _weight_grad(conv_in, dconvout, "dw_conv_out"),
        "w_o": _weight_grad(mix_in, dmixed, "dw_o"),
        "w_up": _weight_grad(h2, du, "dw_up"),
        "w_down": _weight_grad(a, df, "dw_down"),
        "w_ple_gate": _weight_grad(h3, ds3, "dw_ple_gate"),
        "w_ple_proj": _weight_grad(p, dpp, "dw_ple_proj"),
    }
    small_grads = {"g_pre_mix": dg_pre_mix, "b_gate": db_gate, "g_post_mix": dg_post_mix, "g_pre_mlp": dg_pre_mlp,
                   "g_post_mlp": dg_post_mlp, "g_ple": dg_ple, "w_conv": dwconv}
    return loss[0, 0], dx, big, small_grads


def _pack_small(vals, d):
    parts = []
    for a in vals:
        a = jnp.pad(a.reshape(-1), (0, -a.size % d)).reshape(-1, d)
        parts.append(jnp.pad(a, ((0, HALO - a.shape[0]), (0, 0))))
    return jnp.concatenate(parts, axis=0)


def _unpack_small(pack, shapes, d):
    out = []
    for i, shp in enumerate(shapes):
        n = 1
        for v in shp:
            n *= v
        rows = -(-n // d)
        out.append(pack[i * HALO:i * HALO + rows].reshape(-1)[:n].reshape(shp))
    return out


def kernel(x, p, g_pre_mix, w_in, b_gate, w_conv, w_attn_out, w_conv_out, w_o, g_post_mix, g_pre_mlp, w_up, w_down, g_post_mlp, g_ple, w_ple_gate, w_ple_proj, loss_target, m_g_pre_mix, m_w_in, m_b_gate, m_w_conv, m_w_attn_out, m_w_conv_out, m_w_o, m_g_post_mix, m_g_pre_mlp, m_w_up, m_w_down, m_g_post_mlp, m_g_ple, m_w_ple_gate, m_w_ple_proj, v_g_pre_mix, v_w_in, v_b_gate, v_w_conv, v_w_attn_out, v_w_conv_out, v_w_o, v_g_post_mix, v_g_pre_mlp, v_w_up, v_w_down, v_g_post_mlp, v_g_ple, v_w_ple_gate, v_w_ple_proj):
    given = dict(locals())
    order = ["g_pre_mix", "w_in", "b_gate", "w_conv", "w_attn_out", "w_conv_out", "w_o", "g_post_mix", "g_pre_mlp",
             "w_up", "w_down", "g_post_mlp", "g_ple", "w_ple_gate", "w_ple_proj"]
    d = x.shape[-1]
    me = 4 * lax.axis_index("x") + 2 * lax.axis_index("y") + lax.axis_index("c")

    shards = [given[n][0] for n in BIG]
    cw_shard = w_conv.shape[-1]
    conv_tile = jnp.pad(w_conv[0], ((0, HALO - CONV_K), (0, LANES - cw_shard)))
    *gathered, conv_g = _gather_weights(shards, [COL_SHARDED[n] for n in BIG], conv_tile)
    full = dict(zip(BIG, gathered))
    wconv = jnp.concatenate([conv_g[dev, :CONV_K, :cw_shard] for dev in range(N_DEV)], axis=1)

    small = {n: given[n] for n in SMALL}
    loss, dx, big_grads, small_grads = _local_grads(x[0], p[0, 0], loss_target[0], small, wconv, full, ROW_BLOCK,
                                                    ATTN_BLOCK)
    loss = lax.psum(loss, MESH_AXES)

    small_names = list(SMALL) + ["w_conv"]
    pack = _pack_small([small_grads[n] for n in small_names], d)
    *parts, packs = _scatter_grads([big_grads[n] for n in BIG], [COL_SHARDED[n] for n in BIG], pack)

    grads, deltas, new_m, new_v = {}, {}, {}, {}
    for n, part in zip(BIG, parts):
        grads[n], deltas[n], new_m[n], new_v[n] = (
            a[None] for a in _sum_and_adamw(part, given[n][0], given["m_" + n][0], given["v_" + n][0], "adamw_" + n))

    full_conv = lambda a: lax.dynamic_update_slice(jnp.zeros((CONV_K, N_DEV * cw_shard), F32), a[0],
                                                   (jnp.int32(0), me * cw_shard))
    state = [_pack_small([given[pre + n] for n in SMALL] + [full_conv(given[pre + "w_conv"])], d)
             for pre in ("", "m_", "v_")]
    outs = _sum_and_adamw(packs, *state, "adamw_small")
    shapes = [given[n].shape for n in SMALL] + [(CONV_K, N_DEV * cw_shard)]
    for res, dst in zip(outs, (grads, deltas, new_m, new_v)):
        for n, a in zip(small_names, _unpack_small(res, shapes, d)):
            dst[n] = (lax.dynamic_slice(a, (jnp.int32(0), me * cw_shard), (CONV_K, cw_shard))[None]
                      if n == "w_conv" else a)

    return (loss, dx[None], *[grads[n] for n in order], *[deltas[n] for n in order],
            *[new_m[n] for n in order], *[new_v[n] for n in order])
```

```python
import functools

import jax
import jax.numpy as jnp
from jax import lax
from jax.experimental import pallas as pl
from jax.experimental.pallas import tpu as pltpu

F32 = jnp.float32
BF16 = jnp.bfloat16
RMS_EPS = 1e-6
N_DEV = 8
MESH_AXES = ("x", "y", "c")
LANES = 128
HEAD_DIM = 64
HEADS_PER_GROUP = LANES // HEAD_DIM
CONV_K = 3
HALO = 8
VMEM_LIMIT = 56 * 1024 * 1024
EXP_ZERO = -104.0

ADAM_LR = 0.001
ADAM_B1 = 0.9
ADAM_B2 = 0.999
ADAM_EPS = 1e-08
ADAM_WD = 0.01
ADAM_STEP = 10

ROW_BLOCK = 256
ATTN_BLOCK = 256
DW_TOKENS = 512
DW_TILE = 1024
FF_CHUNK = 1024
PROJ_CHUNK = 512


def _dot(a, b):
    return lax.dot_general(a, b, (((1,), (0,)), ((), ())), preferred_element_type=F32)


def _dot_nt(a, b):
    return lax.dot_general(a, b, (((1,), (1,)), ((), ())), preferred_element_type=F32)


def _dot_tn(a, b):
    return lax.dot_general(a, b, (((0,), (0,)), ((), ())), preferred_element_type=F32)


def _sigmoid(z):
    return 1.0 / (1.0 + jnp.exp(-z))


def _rms_scale(x):
    return lax.rsqrt(jnp.mean(x * x, axis=-1, keepdims=True) + RMS_EPS)


def _rms_bwd(xhat, r, g, dy):
    gd = dy * g
    return r * (gd - xhat * jnp.mean(gd * xhat, axis=-1, keepdims=True))


def _params(n_axes, **kw):
    return pltpu.CompilerParams(dimension_semantics=("arbitrary",) * n_axes, vmem_limit_bytes=VMEM_LIMIT, **kw)


def _load_resident(pairs, sem):
    @pl.when(pl.program_id(0) == 0)
    def _():
        copies = [pltpu.make_async_copy(src, dst, sem.at[i]) for i, (src, dst) in enumerate(pairs)]
        for cp in copies:
            cp.start()
        for cp in copies:
            cp.wait()


def _row_spec(tm, width):
    return pl.BlockSpec((tm, width), lambda i: (i, 0))


def _col_spec(tm, width, col):
    return pl.BlockSpec((tm, width), lambda i: (i, col))


def _prev_halo_spec(tm, width, col=0):
    per = tm // HALO
    return pl.BlockSpec((HALO, width), lambda i: (jnp.maximum(i * per - 1, 0), col))


def _next_halo_spec(tm, width, n_rows):
    per = tm // HALO
    last = n_rows // HALO - 1
    return pl.BlockSpec((HALO, width), lambda i: (jnp.minimum((i + 1) * per, last), 0))


def _const_spec(shape):
    return pl.BlockSpec(shape, lambda i: (0,) * len(shape))


ANY = pl.BlockSpec(memory_space=pl.ANY)


def _shift_down(cur, prev, n):
    rows = lax.broadcasted_iota(jnp.int32, cur.shape, 0)
    out = pltpu.roll(cur, n, 0)
    for j in range(n):
        out = jnp.where(rows == j, prev[HALO - n + j:HALO - n + j + 1, :], out)
    return out


def _shift_up(cur, nxt, n):
    tm = cur.shape[0]
    rows = lax.broadcasted_iota(jnp.int32, cur.shape, 0)
    out = pltpu.roll(cur, tm - n, 0)
    for j in range(n):
        out = jnp.where(rows == tm - n + j, nxt[j:j + 1, :], out)
    return out


def _conv_taps(cm, cm_prev, wconv):
    cm1 = _shift_down(cm, cm_prev, 1)
    cm2 = _shift_down(cm, cm_prev, 2)
    cv = wconv[2:3, :] * cm + wconv[1:2, :] * cm1 + wconv[0:1, :] * cm2
    return cv, cm1, cm2


def _in_proj_fwd(x, g1, b_gate, w_in, aw, cw, tm):
    s, d = x.shape
    ni = w_in.shape[1]
    n_qkv, n_conv = 3 * aw, 3 * cw
    ch = PROJ_CHUNK

    def body(x_ref, g_ref, b_ref, w_hbm, qkv_ref, conv_ref, gate_ref, w_vmem, sem):
        _load_resident([(w_hbm, w_vmem)], sem)
        xv = x_ref[...]
        h = (xv * _rms_scale(xv) * g_ref[...]).astype(BF16)
        for c0 in range(0, ni, ch):
            pc = _dot(h, w_vmem[:, c0:c0 + ch])
            if c0 < n_qkv:
                qkv_ref[:, c0:c0 + ch] = pc.astype(BF16)
            elif c0 < n_qkv + n_conv:
                conv_ref[:, c0 - n_qkv:c0 - n_qkv + ch] = pc
            else:
                g0 = c0 - n_qkv - n_conv
                gate_ref[:, g0:g0 + ch] = _sigmoid(pc + b_ref[:, g0:g0 + ch])

    return pl.pallas_call(
        body, name="in_proj_fwd", grid=(s // tm,),
        in_specs=[_row_spec(tm, d), _const_spec((1, d)), _const_spec((1, 2 * d)), ANY],
        out_specs=[_row_spec(tm, n_qkv), _row_spec(tm, n_conv), _row_spec(tm, 2 * d)],
        out_shape=[jax.ShapeDtypeStruct((s, n_qkv), BF16), jax.ShapeDtypeStruct((s, n_conv), F32),
                   jax.ShapeDtypeStruct((s, 2 * d), F32)],
        scratch_shapes=[pltpu.VMEM((d, ni), BF16), pltpu.SemaphoreType.DMA((1,))],
        compiler_params=_params(1),
    )(x, g1, b_gate, w_in)


def _split_hi_lo(a):
    hi = a.astype(BF16)
    return hi, (a - hi.astype(F32)).astype(BF16)


def _log_gates(z):
    t = jnp.exp(-jnp.abs(z))
    u = 1.0 + t
    sp = jnp.log(u)
    return jnp.minimum(z, 0.0) - sp, jnp.minimum(-z, 0.0) - sp, t, u


def _attn_masks(t):
    row = lax.broadcasted_iota(jnp.int32, (t, t), 0)
    col = lax.broadcasted_iota(jnp.int32, (t, t), 1)
    return col < row, (row > col).astype(BF16), (row >= col).astype(BF16)


def _while_weights_live(qi, block, carry):
    def cond(state):
        j, carry = state
        return jnp.logical_and(j < qi, jnp.max(carry[0]) >= EXP_ZERO)

    def step(state):
        j, carry = state
        return j + 1, block(qi - 1 - j, carry)

    return lax.while_loop(cond, step, (jnp.int32(0), carry))[1]


def _head_lanes(h):
    lane = lax.broadcasted_iota(jnp.int32, (1, LANES), 1)
    return (lane >= HEAD_DIM * h) & (lane < HEAD_DIM * (h + 1))


def _attn_fwd(qkv, aw, t):
    s = qkv.shape[0]
    groups = aw // LANES
    scale = HEAD_DIM ** -0.5

    def body(q_ref, k_ref, v_ref, o_ref):
        qi = pl.program_id(1)
        causal, upper, _ = _attn_masks(t)
        q = q_ref[...] * scale
        outs = []
        for h in range(HEADS_PER_GROUP):
            qm = jnp.where(_head_lanes(h), q, jnp.zeros_like(q))

            def block(kb, run, acc, diag):
                rows = pl.ds(pl.multiple_of(kb * t, t), t)
                z = _dot_nt(qm, k_ref[rows, :])
                log_b, log_keep, _, _ = _log_gates(z)
                if diag:
                    log_keep = jnp.where(causal, log_keep, 0.0)
                hi, lo = _split_hi_lo(log_keep)
                between = _dot(hi, upper) + _dot(lo, upper) + run
                w = jnp.exp(log_b + between)
                if diag:
                    w = jnp.where(causal, w, 0.0)
                acc = acc + _dot(w.astype(BF16), v_ref[rows, :])
                return run + jnp.sum(log_keep, axis=1, keepdims=True), acc

            run, acc = block(qi, jnp.zeros((t, 1), F32), jnp.zeros((t, LANES), F32), True)

            _, acc = _while_weights_live(qi, lambda kb, carry: block(kb, *carry, False), (run, acc))
            outs.append(acc)
        o_ref[...] = jnp.where(_head_lanes(0), outs[0], outs[1])

    return pl.pallas_call(
        body, name="attn_fwd", grid=(groups, s // t),
        in_specs=[pl.BlockSpec((t, LANES), lambda g, i: (i, g)),
                  pl.BlockSpec((s, LANES), lambda g, i: (0, groups + g)),
                  pl.BlockSpec((s, LANES), lambda g, i: (0, 2 * groups + g))],
        out_specs=pl.BlockSpec((t, LANES), lambda g, i: (i, g)),
        out_shape=jax.ShapeDtypeStruct((s, aw), F32),
        compiler_params=_params(2),
    )(qkv, qkv, qkv)


def _attn_bwd(qkv, o, do, aw, t):
    s = qkv.shape[0]
    groups = aw // LANES
    nq = s // t
    scale = HEAD_DIM ** -0.5

    def body(q_ref, k_ref, v_ref, o_ref, do_ref, dq_ref, dk_ref, dv_ref, dk_acc, dv_acc):
        qi = pl.program_id(1)

        @pl.when(qi == 0)
        def _():
            dk_acc[...] = jnp.zeros_like(dk_acc)
            dv_acc[...] = jnp.zeros_like(dv_acc)

        causal, upper, lower_incl = _attn_masks(t)
        q = q_ref[...] * scale
        do_b = do_ref[...]
        do_o = do_b.astype(F32) * o_ref[...]
        dqs = []
        for h in range(HEADS_PER_GROUP):
            lanes = _head_lanes(h)
            qm = jnp.where(lanes, q, jnp.zeros_like(q))
            dom = jnp.where(lanes, do_b, jnp.zeros_like(do_b))
            e_total = jnp.sum(jnp.where(lanes, do_o, 0.0), axis=1, keepdims=True)

            def block(kb, run, e_run, dq, diag):
                rows = pl.ds(pl.multiple_of(kb * t, t), t)
                k = k_ref[rows, :]
                v = v_ref[rows, :]
                z = _dot_nt(qm, k)
                log_b, log_keep, tt, u = _log_gates(z)
                r = 1.0 / u
                beta = jnp.where(z >= 0.0, r, tt * r)
                keep = jnp.where(z >= 0.0, tt * r, r)
                if diag:
                    log_keep = jnp.where(causal, log_keep, 0.0)
                hi, lo = _split_hi_lo(log_keep)
                between = _dot(hi, upper) + _dot(lo, upper) + run
                w = jnp.exp(log_b + between)
                if diag:
                    w = jnp.where(causal, w, 0.0)
                wb = w.astype(BF16)
                e = _dot_nt(dom, v) * wb.astype(F32)
                hi, lo = _split_hi_lo(e)
                e_suffix = _dot(hi, lower_incl) + _dot(lo, lower_incl) + e_run
                dz = e * keep - (e_total - e_suffix) * beta
                if diag:
                    dz = jnp.where(causal, dz, 0.0)
                dzb = dz.astype(BF16)
                dk_acc[rows, :] += _dot_tn(dzb, qm)
                dv_acc[rows, :] += _dot_tn(wb, dom)
                return (run + jnp.sum(log_keep, axis=1, keepdims=True), e_suffix[:, 0:1], dq + _dot(dzb, k))

            zero_col = jnp.zeros((t, 1), F32)
            run, e_run, dq = block(qi, zero_col, zero_col, jnp.zeros((t, LANES), F32), True)

            _, _, dq = _while_weights_live(qi, lambda kb, carry: block(kb, *carry, False), (run, e_run, dq))
            dqs.append(dq)
        dq_ref[...] = (jnp.where(_head_lanes(0), dqs[0], dqs[1]) * scale).astype(BF16)

        @pl.when(qi == nq - 1)
        def _():
            dk_ref[...] = dk_acc[...].astype(BF16)
            dv_ref[...] = dv_acc[...].astype(BF16)

    blk = pl.BlockSpec((t, LANES), lambda g, i: (i, g))
    slab = pl.BlockSpec((s, LANES), lambda g, i: (0, g))
    return pl.pallas_call(
        body, name="attn_bwd", grid=(groups, nq),
        in_specs=[blk, pl.BlockSpec((s, LANES), lambda g, i: (0, groups + g)),
                  pl.BlockSpec((s, LANES), lambda g, i: (0, 2 * groups + g)), blk, blk],
        out_specs=[blk, slab, slab],
        out_shape=[jax.ShapeDtypeStruct((s, aw), BF16)] * 3,
        scratch_shapes=[pltpu.VMEM((s, LANES), F32), pltpu.VMEM((s, LANES), F32)],
        compiler_params=_params(2),
    )(qkv, qkv, qkv, o, do)


def _branches(o_b, conv, conv_prev, wconv, w_ao, w_co, cw, first):
    cb = conv[:, 0:cw]
    cm = conv[:, cw:2 * cw] * conv[:, 2 * cw:3 * cw]
    cm_prev = conv_prev[:, cw:2 * cw] * conv_prev[:, 2 * cw:3 * cw]
    cm_prev = jnp.where(first, 0.0, cm_prev)
    cv, cm1, cm2 = _conv_taps(cm, cm_prev, wconv)
    conv_in = (cb * cv).astype(BF16)
    return _dot(o_b, w_ao), _dot(conv_in, w_co), conv_in, cb, cv, cm, cm1, cm2


def _mix_fwd(x, o, conv, gate, wconv, g_post, w_ao, w_co, w_o, tm):
    s, d = x.shape
    aw, cw = w_ao.shape[0], w_co.shape[0]

    def body(x_ref, o_ref, conv_ref, prev_ref, gate_ref, wc_ref, g_ref, wao_hbm, wco_hbm, wo_hbm,
             x1_ref, mixed_ref, mixin_ref, convin_ref, wao, wco, wo, sem):
        _load_resident([(wao_hbm, wao), (wco_hbm, wco), (wo_hbm, wo)], sem)
        y_attn, y_conv, conv_in, *_ = _branches(
            o_ref[...].astype(BF16), conv_ref[...], prev_ref[...], wc_ref[...], wao[...], wco[...], cw,
            pl.program_id(0) == 0)
        mix_in = (gate_ref[:, 0:d] * y_attn + gate_ref[:, d:2 * d] * y_conv).astype(BF16)
        mixed = _dot(mix_in, wo[...])
        x1_ref[...] = x_ref[...] + mixed * _rms_scale(mixed) * g_ref[...]
        mixed_ref[...] = mixed
        mixin_ref[...] = mix_in
        convin_ref[...] = conv_in

    return pl.pallas_call(
        body, name="mix_fwd", grid=(s // tm,),
        in_specs=[_row_spec(tm, d), _row_spec(tm, aw), _row_spec(tm, 3 * cw), _prev_halo_spec(tm, 3 * cw),
                  _row_spec(tm, 2 * d), _const_spec((CONV_K, cw)), _const_spec((1, d)), ANY, ANY, ANY],
        out_specs=[_row_spec(tm, d), _row_spec(tm, d), _row_spec(tm, d), _row_spec(tm, cw)],
        out_shape=[jax.ShapeDtypeStruct((s, d), F32), jax.ShapeDtypeStruct((s, d), F32),
                   jax.ShapeDtypeStruct((s, d), BF16), jax.ShapeDtypeStruct((s, cw), BF16)],
        scratch_shapes=[pltpu.VMEM(w_ao.shape, BF16), pltpu.VMEM(w_co.shape, BF16), pltpu.VMEM(w_o.shape, BF16),
                        pltpu.SemaphoreType.DMA((3,))],
        compiler_params=_params(1),
    )(x, o, conv, conv, gate, wconv, g_post, w_ao, w_co, w_o)


def _mix_bwd(dx1, mixed, o, conv, gate, wconv, g_post, w_ao, w_co, w_o, tm):
    s, d = dx1.shape
    aw, cw = w_ao.shape[0], w_co.shape[0]

    def body(dx1_ref, mixed_ref, o_ref, conv_ref, prev_ref, gate_ref, wc_ref, g_ref, wao_hbm, wco_hbm, wo_hbm,
             dmixed_ref, dattn_ref, dconvout_ref, do_ref, dgate_ref, dcb_ref, dcv_ref, dg_ref, dbias_ref,
             wao, wco, wo, sem):
        i = pl.program_id(0)
        _load_resident([(wao_hbm, wao), (wco_hbm, wco), (wo_hbm, wo)], sem)

        @pl.when(i == 0)
        def _():
            dg_ref[...] = jnp.zeros_like(dg_ref)
            dbias_ref[...] = jnp.zeros_like(dbias_ref)

        mixed = mixed_ref[...]
        r = _rms_scale(mixed)
        mhat = mixed * r
        dn = dx1_ref[...]
        dg_ref[...] += jnp.sum(dn * mhat, axis=0, keepdims=True)
        dmixed = _rms_bwd(mhat, r, g_ref[...], dn).astype(BF16)
        dmixed_ref[...] = dmixed
        dmi = _dot_nt(dmixed, wo[...])

        y_attn, y_conv, _, cb, cv, *_ = _branches(
            o_ref[...].astype(BF16), conv_ref[...], prev_ref[...], wc_ref[...], wao[...], wco[...], cw, i == 0)
        ga = gate_ref[:, 0:d]
        gc = gate_ref[:, d:2 * d]
        dpre_a = dmi * y_attn * ga * (1.0 - ga)
        dpre_c = dmi * y_conv * gc * (1.0 - gc)
        dgate_ref[:, 0:d] = dpre_a.astype(BF16)
        dgate_ref[:, d:2 * d] = dpre_c.astype(BF16)
        dbias_ref[:, 0:d] += jnp.sum(dpre_a, axis=0, keepdims=True)
        dbias_ref[:, d:2 * d] += jnp.sum(dpre_c, axis=0, keepdims=True)

        dattn = (dmi * ga).astype(BF16)
        dattn_ref[...] = dattn
        do_ref[...] = _dot_nt(dattn, wao[...]).astype(BF16)
        dconvout = (dmi * gc).astype(BF16)
        dconvout_ref[...] = dconvout
        dconv_in = _dot_nt(dconvout, wco[...])
        dcb_ref[...] = (dconv_in * cv).astype(BF16)
        dcv_ref[...] = dconv_in * cb

    return pl.pallas_call(
        body, name="mix_bwd", grid=(s // tm,),
        in_specs=[_row_spec(tm, d), _row_spec(tm, d), _row_spec(tm, aw), _row_spec(tm, 3 * cw),
                  _prev_halo_spec(tm, 3 * cw), _row_spec(tm, 2 * d), _const_spec((CONV_K, cw)), _const_spec((1, d)),
                  ANY, ANY, ANY],
        out_specs=[_row_spec(tm, d), _row_spec(tm, d), _row_spec(tm, d), _row_spec(tm, aw), _row_spec(tm, 2 * d),
                   _row_spec(tm, cw), _row_spec(tm, cw), _const_spec((1, d)), _const_spec((1, 2 * d))],
        out_shape=[jax.ShapeDtypeStruct((s, d), BF16), jax.ShapeDtypeStruct((s, d), BF16),
                   jax.ShapeDtypeStruct((s, d), BF16), jax.ShapeDtypeStruct((s, aw), BF16),
                   jax.ShapeDtypeStruct((s, 2 * d), BF16), jax.ShapeDtypeStruct((s, cw), BF16),
                   jax.ShapeDtypeStruct((s, cw), F32), jax.ShapeDtypeStruct((1, d), F32),
                   jax.ShapeDtypeStruct((1, 2 * d), F32)],
        scratch_shapes=[pltpu.VMEM(w_ao.shape, BF16), pltpu.VMEM(w_co.shape, BF16), pltpu.VMEM(w_o.shape, BF16),
                        pltpu.SemaphoreType.DMA((3,))],
        compiler_params=_params(1),
    )(dx1, mixed, o, conv, conv, gate, wconv, g_post, w_ao, w_co, w_o)


def _mlp_ple_loss(x1, p, target, g_pre, g_post, g_ple, w_up, w_dn, w_pg, w_pp, tm):
    s, d = x1.shape
    ff = w_up.shape[1]
    pd = p.shape[1]
    fc = FF_CHUNK

    def body(x1_ref, p_ref, t_ref, gpre_ref, gpost_ref, gple_ref, wup_hbm, wdn_hbm, wpg_hbm, wpp_hbm,
             dx1_ref, h2_ref, du_ref, a_ref, df_ref, h3_ref, ds3_ref, dpp_ref, loss_ref, dgpre_ref, dgpost_ref,
             dgple_ref, wup, wdn, wpg, wpp, u_scr, sem):
        _load_resident([(wup_hbm, wup), (wdn_hbm, wdn), (wpg_hbm, wpg), (wpp_hbm, wpp)], sem)

        @pl.when(pl.program_id(0) == 0)
        def _():
            for ref in (loss_ref, dgpre_ref, dgpost_ref, dgple_ref):
                ref[...] = jnp.zeros_like(ref)

        x1v = x1_ref[...]
        r2 = _rms_scale(x1v)
        x1hat = x1v * r2
        h2 = (x1hat * gpre_ref[...]).astype(BF16)
        h2_ref[...] = h2
        f = jnp.zeros((tm, d), F32)
        for c0 in range(0, ff, fc):
            u = _dot(h2, wup[:, c0:c0 + fc])
            u_scr[:, c0:c0 + fc] = u
            a = jnp.square(jnp.maximum(u, 0.0)).astype(BF16)
            a_ref[:, c0:c0 + fc] = a
            f = f + _dot(a, wdn[c0:c0 + fc, :])
        rf = _rms_scale(f)
        fhat = f * rf
        x2 = x1v + fhat * gpost_ref[...]
        r3 = _rms_scale(x2)
        x2hat = x2 * r3
        h3 = (x2hat * gple_ref[...]).astype(BF16)
        h3_ref[...] = h3
        pg = _sigmoid(_dot(h3, wpg[...]))
        pp = _dot(p_ref[...].astype(BF16), wpp[...])
        diff = x2 + pg * pp - t_ref[...]
        loss_ref[...] += 0.5 * jnp.sum(jnp.mean(diff * diff, axis=-1, keepdims=True), axis=0, keepdims=True)

        dy = diff * (1.0 / d)
        dpp_ref[...] = (dy * pg).astype(BF16)
        ds3 = (dy * pp * pg * (1.0 - pg)).astype(BF16)
        ds3_ref[...] = ds3
        dh3 = _dot_nt(ds3, wpg[...])
        dgple_ref[...] += jnp.sum(dh3 * x2hat, axis=0, keepdims=True)
        dx2 = dy + _rms_bwd(x2hat, r3, gple_ref[...], dh3)
        dgpost_ref[...] += jnp.sum(dx2 * fhat, axis=0, keepdims=True)
        df = _rms_bwd(fhat, rf, gpost_ref[...], dx2).astype(BF16)
        df_ref[...] = df
        dh2 = jnp.zeros((tm, d), F32)
        for c0 in range(0, ff, fc):
            da = _dot_nt(df, wdn[c0:c0 + fc, :])
            du = (da * (2.0 * jnp.maximum(u_scr[:, c0:c0 + fc], 0.0))).astype(BF16)
            du_ref[:, c0:c0 + fc] = du
            dh2 = dh2 + _dot_nt(du, wup[:, c0:c0 + fc])
        dgpre_ref[...] += jnp.sum(dh2 * x1hat, axis=0, keepdims=True)
        dx1_ref[...] = dx2 + _rms_bwd(x1hat, r2, gpre_ref[...], dh2)

    vec = _const_spec((1, d))
    return pl.pallas_call(
        body, name="mlp_ple_loss", grid=(s // tm,),
        in_specs=[_row_spec(tm, d), _row_spec(tm, pd), _row_spec(tm, d), vec, vec, vec, ANY, ANY, ANY, ANY],
        out_specs=[_row_spec(tm, d), _row_spec(tm, d), _row_spec(tm, ff), _row_spec(tm, ff), _row_spec(tm, d),
                   _row_spec(tm, d), _row_spec(tm, d), _row_spec(tm, d), _const_spec((1, 1)), vec, vec, vec],
        out_shape=[jax.ShapeDtypeStruct((s, d), F32), jax.ShapeDtypeStruct((s, d), BF16),
                   jax.ShapeDtypeStruct((s, ff), BF16), jax.ShapeDtypeStruct((s, ff), BF16),
                   jax.ShapeDtypeStruct((s, d), BF16), jax.ShapeDtypeStruct((s, d), BF16),
                   jax.ShapeDtypeStruct((s, d), BF16), jax.ShapeDtypeStruct((s, d), BF16),
                   jax.ShapeDtypeStruct((1, 1), F32), jax.ShapeDtypeStruct((1, d), F32),
                   jax.ShapeDtypeStruct((1, d), F32), jax.ShapeDtypeStruct((1, d), F32)],
        scratch_shapes=[pltpu.VMEM(w_up.shape, BF16), pltpu.VMEM(w_dn.shape, BF16), pltpu.VMEM(w_pg.shape, BF16),
                        pltpu.VMEM(w_pp.shape, BF16), pltpu.VMEM((tm, ff), F32), pltpu.SemaphoreType.DMA((4,))],
        compiler_params=_params(1),
    )(x1, p, target, g_pre, g_post, g_ple, w_up, w_dn, w_pg, w_pp)


def _in_proj_bwd(x, dx1, dq, dk, dv, dcb, dcv, conv, dgate, wconv, g1, w_in, tm):
    s, d = x.shape
    aw, cw = dq.shape[1], dcb.shape[1]
    ni = w_in.shape[1]

    def body(x_ref, dx1_ref, dq_ref, dk_ref, dv_ref, dcb_ref, dcv_ref, dcvn_ref, cc_ref, cu_ref, ccp_ref, cup_ref,
             dgate_ref, wc_ref, g_ref, w_hbm, dx_ref, dproj_ref, h1_ref, dg_ref, dwc_ref, w_vmem, sem):
        i = pl.program_id(0)
        _load_resident([(w_hbm, w_vmem)], sem)

        @pl.when(i == 0)
        def _():
            dg_ref[...] = jnp.zeros_like(dg_ref)
            dwc_ref[...] = jnp.zeros_like(dwc_ref)

        wc = wc_ref[...]
        cc = cc_ref[...]
        cu = cu_ref[...]
        cm = cc * cu
        cm_prev = jnp.where(i == 0, 0.0, ccp_ref[...] * cup_ref[...])
        _, cm1, cm2 = _conv_taps(cm, cm_prev, wc)
        dcv_cur = dcv_ref[...]
        dcv_next = jnp.where(i == pl.num_programs(0) - 1, 0.0, dcvn_ref[...])
        dcm = (wc[2:3, :] * dcv_cur + wc[1:2, :] * _shift_up(dcv_cur, dcv_next, 1)
               + wc[0:1, :] * _shift_up(dcv_cur, dcv_next, 2))
        for tap, shifted in enumerate((cm2, cm1, cm)):
            dwc_ref[tap:tap + 1, :] += jnp.sum(dcv_cur * shifted, axis=0, keepdims=True)

        pieces = [(dq_ref[...], aw), (dk_ref[...], aw), (dv_ref[...], aw), (dcb_ref[...], cw),
                  ((dcm * cu).astype(BF16), cw), ((dcm * cc).astype(BF16), cw), (dgate_ref[...], 2 * d)]
        dh = jnp.zeros((tm, d), F32)
        c0 = 0
        for piece, width in pieces:
            dproj_ref[:, c0:c0 + width] = piece
            dh = dh + _dot_nt(piece, w_vmem[:, c0:c0 + width])
            c0 += width

        xv = x_ref[...]
        r = _rms_scale(xv)
        xhat = xv * r
        h1_ref[...] = (xhat * g_ref[...]).astype(BF16)
        dg_ref[...] += jnp.sum(dh * xhat, axis=0, keepdims=True)
        dx_ref[...] = dx1_ref[...] + _rms_bwd(xhat, r, g_ref[...], dh)

    return pl.pallas_call(
        body, name="in_proj_bwd", grid=(s // tm,),
        in_specs=[_row_spec(tm, d), _row_spec(tm, d), _row_spec(tm, aw), _row_spec(tm, aw), _row_spec(tm, aw),
                  _row_spec(tm, cw), _row_spec(tm, cw), _next_halo_spec(tm, cw, s), _col_spec(tm, cw, 1),
                  _col_spec(tm, cw, 2), _prev_halo_spec(tm, cw, 1), _prev_halo_spec(tm, cw, 2), _row_spec(tm, 2 * d),
                  _const_spec((CONV_K, cw)), _const_spec((1, d)), ANY],
        out_specs=[_row_spec(tm, d), _row_spec(tm, ni), _row_spec(tm, d), _const_spec((1, d)),
                   _const_spec((CONV_K, cw))],
        out_shape=[jax.ShapeDtypeStruct((s, d), F32), jax.ShapeDtypeStruct((s, ni), BF16),
                   jax.ShapeDtypeStruct((s, d), BF16), jax.ShapeDtypeStruct((1, d), F32),
                   jax.ShapeDtypeStruct((CONV_K, cw), F32)],
        scratch_shapes=[pltpu.VMEM((d, ni), BF16), pltpu.SemaphoreType.DMA((1,))],
        compiler_params=_params(1),
    )(x, dx1, dq, dk, dv, dcb, dcv, dcv, conv, conv, conv, conv, dgate, wconv, g1, w_in)


def _weight_grad(a, b, name):
    s, m = a.shape
    n = b.shape[1]
    tm, tn, tk = min(m, DW_TILE), min(n, DW_TILE), min(s, DW_TOKENS)
    nk = s // tk

    def body(a_ref, b_ref, o_ref, acc):
        k = pl.program_id(2)

        @pl.when(k == 0)
        def _():
            acc[...] = jnp.zeros_like(acc)

        acc[...] += _dot_tn(a_ref[...].astype(BF16), b_ref[...].astype(BF16))

        @pl.when(k == nk - 1)
        def _():
            o_ref[...] = acc[...].astype(BF16)

    return pl.pallas_call(
        body, name=name, grid=(m // tm, n // tn, nk),
        in_specs=[pl.BlockSpec((tk, tm), lambda i, j, k: (k, i)), pl.BlockSpec((tk, tn), lambda i, j, k: (k, j))],
        out_specs=pl.BlockSpec((tm, tn), lambda i, j, k: (i, j)),
        out_shape=jax.ShapeDtypeStruct((m, n), BF16),
        scratch_shapes=[pltpu.VMEM((tm, tn), F32)],
        compiler_params=_params(3),
    )(a, b)


def _mesh_position():
    return tuple(lax.axis_index(a) for a in MESH_AXES)


def _peer(me, k):
    bits = ((k >> 2) & 1, (k >> 1) & 1, k & 1)
    pos = tuple(1 - m if b else m for m, b in zip(me, bits))
    return pos, 4 * pos[0] + 2 * pos[1] + pos[2]


def _gather_weights(shards, col_sharded, conv_tile):
    n = len(shards)
    full_shapes = [(r, N_DEV * c) if by_col else (N_DEV * r, c) for (r, c), by_col in
                   zip((a.shape for a in shards), col_sharded)]

    def body(*refs):
        ins, conv_in = refs[:n], refs[n]
        outs, conv_out = refs[n + 1:2 * n + 1], refs[2 * n + 1]
        stage = refs[2 * n + 2:3 * n + 2]
        send_sems, recv_sems, local_sems = refs[3 * n + 2:]
        me = _mesh_position()
        mine = 4 * me[0] + 2 * me[1] + me[2]
        for src, dst in zip(ins, stage):
            dst[...] = src[...].astype(BF16)

        def my_block(w):
            if w == n:
                return conv_out.at[mine]
            r, c = shards[w].shape
            if col_sharded[w]:
                return outs[w].at[:, pl.ds(pl.multiple_of(mine * c, LANES), c)]
            return outs[w].at[pl.ds(pl.multiple_of(mine * r, 16), r), :]

        srcs = list(stage) + [conv_in]
        copies = []
        for w, src in enumerate(srcs):
            local = pltpu.make_async_copy(src, my_block(w), local_sems.at[w])
            local.start()
            copies.append(local)
            for k in range(1, N_DEV):
                peer, _ = _peer(me, k)
                cp = pltpu.make_async_remote_copy(
                    src_ref=src, dst_ref=my_block(w), send_sem=send_sems.at[w, k - 1],
                    recv_sem=recv_sems.at[w, k - 1], device_id=peer, device_id_type=pl.DeviceIdType.MESH)
                cp.start()
                copies.append(cp)
        for cp in copies:
            cp.wait()

    vmem = pl.BlockSpec(memory_space=pltpu.VMEM)
    all_in = list(shards) + [conv_tile]
    return pl.pallas_call(
        body, name="gather_weights",
        in_specs=[vmem] * (n + 1), out_specs=[ANY] * (n + 1),
        out_shape=[jax.ShapeDtypeStruct(sh, BF16) for sh in full_shapes]
        + [jax.ShapeDtypeStruct((N_DEV,) + conv_tile.shape, F32)],
        scratch_shapes=[pltpu.VMEM(a.shape, BF16) for a in shards]
        + [pltpu.SemaphoreType.DMA((n + 1, N_DEV - 1)), pltpu.SemaphoreType.DMA((n + 1, N_DEV - 1)),
           pltpu.SemaphoreType.DMA((n + 1,))],
        compiler_params=pltpu.CompilerParams(vmem_limit_bytes=VMEM_LIMIT),
    )(*all_in)


def _scatter_grads(grads, col_sharded, small):
    n = len(grads)
    shard_shapes = []
    for g, by_col in zip(grads, col_sharded):
        r, c = g.shape
        shard_shapes.append((r, c // N_DEV) if by_col else (r // N_DEV, c))

    def body(*refs):
        ins, small_in = refs[:n], refs[n]
        outs, small_out = refs[n + 1:2 * n + 1], refs[2 * n + 1]
        send_sems, recv_sems, local_sems = refs[2 * n + 2:]
        me = _mesh_position()
        mine = 4 * me[0] + 2 * me[1] + me[2]

        def block(w, idx):
            r, c = shard_shapes[w]
            if col_sharded[w]:
                return ins[w].at[:, pl.ds(pl.multiple_of(idx * c, LANES), c)]
            return ins[w].at[pl.ds(pl.multiple_of(idx * r, 16), r), :]

        copies = []
        for w in range(n + 1):
            src_of = (lambda idx, w=w: block(w, idx)) if w < n else (lambda idx: small_in)
            dst = outs[w] if w < n else small_out
            local = pltpu.make_async_copy(src_of(mine), dst.at[mine], local_sems.at[w])
            local.start()
            copies.append(local)
            for k in range(1, N_DEV):
                peer, peer_idx = _peer(me, k)
                cp = pltpu.make_async_remote_copy(
                    src_ref=src_of(peer_idx), dst_ref=dst.at[mine], send_sem=send_sems.at[w, k - 1],
                    recv_sem=recv_sems.at[w, k - 1], device_id=peer, device_id_type=pl.DeviceIdType.MESH)
                cp.start()
                copies.append(cp)
        for cp in copies:
            cp.wait()

    return pl.pallas_call(
        body, name="scatter_grads",
        in_specs=[ANY] * (n + 1), out_specs=[ANY] * (n + 1),
        out_shape=[jax.ShapeDtypeStruct((N_DEV,) + sh, BF16) for sh in shard_shapes]
        + [jax.ShapeDtypeStruct((N_DEV,) + small.shape, F32)],
        scratch_shapes=[pltpu.SemaphoreType.DMA((n + 1, N_DEV - 1)), pltpu.SemaphoreType.DMA((n + 1, N_DEV - 1)),
                        pltpu.SemaphoreType.DMA((n + 1,))],
        compiler_params=pltpu.CompilerParams(vmem_limit_bytes=VMEM_LIMIT),
    )(*grads, small)


def _adamw(w, g, m, v):
    m = ADAM_B1 * m + (1.0 - ADAM_B1) * g
    v = ADAM_B2 * v + (1.0 - ADAM_B2) * jnp.square(g)
    m_hat = m / (1.0 - ADAM_B1 ** ADAM_STEP)
    v_hat = v / (1.0 - ADAM_B2 ** ADAM_STEP)
    delta = -ADAM_LR * (m_hat / (jnp.sqrt(v_hat) + ADAM_EPS) + ADAM_WD * w)
    return delta, m, v


def _sum_and_adamw(parts, w, m, v, name):
    r, c = w.shape
    tr = min(r, 256)

    def body(p_ref, w_ref, m_ref, v_ref, g_out, d_out, m_out, v_out):
        g = p_ref[0].astype(F32)
        for dev in range(1, N_DEV):
            g = g + p_ref[dev].astype(F32)
        g_out[...] = g
        d_out[...], m_out[...], v_out[...] = _adamw(w_ref[...], g, m_ref[...], v_ref[...])

    blk = pl.BlockSpec((tr, c), lambda i: (i, 0))
    return pl.pallas_call(
        body, name=name, grid=(r // tr,),
        in_specs=[pl.BlockSpec((N_DEV, tr, c), lambda i: (0, i, 0)), blk, blk, blk],
        out_specs=[blk] * 4, out_shape=[jax.ShapeDtypeStruct((r, c), F32)] * 4,
        compiler_params=_params(1),
    )(parts, w, m, v)


BIG = ("w_in", "w_attn_out", "w_conv_out", "w_o", "w_up", "w_down", "w_ple_gate", "w_ple_proj")
COL_SHARDED = {"w_in": True, "w_attn_out": True, "w_conv_out": True, "w_o": False, "w_up": True, "w_down": False,
               "w_ple_gate": False, "w_ple_proj": True}
SMALL = ("g_pre_mix", "b_gate", "g_post_mix", "g_pre_mlp", "g_post_mlp", "g_ple")


def _local_grads(x, p, target, small, wconv, full, tm, t):
    aw, cw = full["w_attn_out"].shape[0], full["w_conv_out"].shape[0]
    qkv, conv, gate = _in_proj_fwd(x, small["g_pre_mix"], small["b_gate"], full["w_in"], aw, cw, tm)
    o = _attn_fwd(qkv, aw, t)
    x1, mixed, mix_in, conv_in = _mix_fwd(x, o, conv, gate, wconv, small["g_post_mix"], full["w_attn_out"],
                                          full["w_conv_out"], full["w_o"], tm)
    (dx1, h2, du, a, df, h3, ds3, dpp, loss, dg_pre_mlp, dg_post_mlp, dg_ple) = _mlp_ple_loss(
        x1, p, target, small["g_pre_mlp"], small["g_post_mlp"], small["g_ple"], full["w_up"], full["w_down"],
        full["w_ple_gate"], full["w_ple_proj"], tm)
    (dmixed, dattn, dconvout, do, dgate, dcb, dcv, dg_post_mix, db_gate) = _mix_bwd(
        dx1, mixed, o, conv, gate, wconv, small["g_post_mix"], full["w_attn_out"], full["w_conv_out"], full["w_o"],
        tm)
    dq, dk, dv = _attn_bwd(qkv, o, do, aw, t)
    dx, dproj, h1, dg_pre_mix, dwconv = _in_proj_bwd(x, dx1, dq, dk, dv, dcb, dcv, conv, dgate, wconv,
                                                    small["g_pre_mix"], full["w_in"], tm)
    big = {
        "w_in": _weight_grad(h1, dproj, "dw_in"),
        "w_attn_out": _weight_grad(o, dattn, "dw_attn_out"),
        "w_conv_out": _weight_grad(conv_in, dconvout, "dw_conv_out"),
        "w_o": _weight_grad(mix_in, dmixed, "dw_o"),
        "w_up": _weight_grad(h2, du, "dw_up"),
        "w_down": _weight_grad(a, df, "dw_down"),
        "w_ple_gate": _weight_grad(h3, ds3, "dw_ple_gate"),
        "w_ple_proj": _weight_grad(p, dpp, "dw_ple_proj"),
    }
    small_grads = {"g_pre_mix": dg_pre_mix, "b_gate": db_gate, "g_post_mix": dg_post_mix, "g_pre_mlp": dg_pre_mlp,
                   "g_post_mlp": dg_post_mlp, "g_ple": dg_ple, "w_conv": dwconv}
    return loss[0, 0], dx, big, small_grads


def _pack_small(vals, d):
    parts = []
    for a in vals:
        a = jnp.pad(a.reshape(-1), (0, -a.size % d)).reshape(-1, d)
        parts.append(jnp.pad(a, ((0, HALO - a.shape[0]), (0, 0))))
    return jnp.concatenate(parts, axis=0)


def _unpack_small(pack, shapes, d):
    out = []
    for i, shp in enumerate(shapes):
        n = 1
        for v in shp:
            n *= v
        rows = -(-n // d)
        out.append(pack[i * HALO:i * HALO + rows].reshape(-1)[:n].reshape(shp))
    return out


def kernel(x, p, g_pre_mix, w_in, b_gate, w_conv, w_attn_out, w_conv_out, w_o, g_post_mix, g_pre_mlp, w_up, w_down, g_post_mlp, g_ple, w_ple_gate, w_ple_proj, loss_target, m_g_pre_mix, m_w_in, m_b_gate, m_w_conv, m_w_attn_out, m_w_conv_out, m_w_o, m_g_post_mix, m_g_pre_mlp, m_w_up, m_w_down, m_g_post_mlp, m_g_ple, m_w_ple_gate, m_w_ple_proj, v_g_pre_mix, v_w_in, v_b_gate, v_w_conv, v_w_attn_out, v_w_conv_out, v_w_o, v_g_post_mix, v_g_pre_mlp, v_w_up, v_w_down, v_g_post_mlp, v_g_ple, v_w_ple_gate, v_w_ple_proj):
    given = dict(locals())
    order = ["g_pre_mix", "w_in", "b_gate", "w_conv", "w_attn_out", "w_conv_out", "w_o", "g_post_mix", "g_pre_mlp",
             "w_up", "w_down", "g_post_mlp", "g_ple", "w_ple_gate", "w_ple_proj"]
    d = x.shape[-1]
    me = 4 * lax.axis_index("x") + 2 * lax.axis_index("y") + lax.axis_index("c")

    shards = [given[n][0] for n in BIG]
    cw_shard = w_conv.shape[-1]
    conv_tile = jnp.pad(w_conv[0], ((0, HALO - CONV_K), (0, LANES - cw_shard)))
    *gathered, conv_g = _gather_weights(shards, [COL_SHARDED[n] for n in BIG], conv_tile)
    full = dict(zip(BIG, gathered))
    wconv = jnp.concatenate([conv_g[dev, :CONV_K, :cw_shard] for dev in range(N_DEV)], axis=1)

    small = {n: given[n] for n in SMALL}
    loss, dx, big_grads, small_grads = _local_grads(x[0], p[0, 0], loss_target[0], small, wconv, full, ROW_BLOCK,
                                                    ATTN_BLOCK)
    loss = lax.psum(loss, MESH_AXES)

    small_names = list(SMALL) + ["w_conv"]
    pack = _pack_small([small_grads[n] for n in small_names], d)
    *parts, packs = _scatter_grads([big_grads[n] for n in BIG], [COL_SHARDED[n] for n in BIG], pack)

    grads, deltas, new_m, new_v = {}, {}, {}, {}
    for n, part in zip(BIG, parts):
        grads[n], deltas[n], new_m[n], new_v[n] = (
            a[None] for a in _sum_and_adamw(part, given[n][0], given["m_" + n][0], given["v_" + n][0], "adamw_" + n))

    full_conv = lambda a: lax.dynamic_update_slice(jnp.zeros((CONV_K, N_DEV * cw_shard), F32), a[0],
                                                   (jnp.int32(0), me * cw_shard))
    state = [_pack_small([given[pre + n] for n in SMALL] + [full_conv(given[pre + "w_conv"])], d)
             for pre in ("", "m_", "v_")]
    outs = _sum_and_adamw(packs, *state, "adamw_small")
    shapes = [given[n].shape for n in SMALL] + [(CONV_K, N_DEV * cw_shard)]
    for res, dst in zip(outs, (grads, deltas, new_m, new_v)):
        for n, a in zip(small_names, _unpack_small(res, shapes, d)):
            dst[n] = (lax.dynamic_slice(a, (jnp.int32(0), me * cw_shard), (CONV_K, cw_shard))[None]
                      if n == "w_conv" else a)

    return (loss, dx[None], *[grads[n] for n in order], *[deltas[n] for n in order],
            *[new_m[n] for n in order], *[new_v[n] for n in order])
```

```python
import jax
import jax.numpy as jnp
from jax import lax
from jax.experimental import pallas as pl
from jax.experimental.pallas import tpu as pltpu

F32 = jnp.float32
BF16 = jnp.bfloat16
RMS_EPS = 1e-6
N_DEV = 8
MESH_AXES = ("x", "y", "c")
LANES = 128
HEAD_DIM = 64
HEADS_PER_GROUP = LANES // HEAD_DIM
CONV_K = 3
HALO = 8
VMEM_LIMIT = 56 * 1024 * 1024
EXP_ZERO = -104.0

ADAM_LR = 0.001
ADAM_B1 = 0.9
ADAM_B2 = 0.999
ADAM_EPS = 1e-08
ADAM_WD = 0.01
ADAM_STEP = 10

ROW_BLOCK = 256
ATTN_BLOCK = 256
DW_TOKENS = 512
DW_TILE = 1024
FF_CHUNK = 1024
PROJ_CHUNK = 512


def _dot(a, b):
    return lax.dot_general(a, b, (((1,), (0,)), ((), ())), preferred_element_type=F32)


def _dot_nt(a, b):
    return lax.dot_general(a, b, (((1,), (1,)), ((), ())), preferred_element_type=F32)


def _dot_tn(a, b):
    return lax.dot_general(a, b, (((0,), (0,)), ((), ())), preferred_element_type=F32)


def _sigmoid(z):
    return 1.0 / (1.0 + jnp.exp(-z))


def _rms_scale(x):
    return lax.rsqrt(jnp.mean(x * x, axis=-1, keepdims=True) + RMS_EPS)


def _rms_bwd(xhat, r, g, dy):
    gd = dy * g
    return r * (gd - xhat * jnp.mean(gd * xhat, axis=-1, keepdims=True))


def _params(n_axes, **kw):
    return pltpu.CompilerParams(dimension_semantics=("arbitrary",) * n_axes, vmem_limit_bytes=VMEM_LIMIT, **kw)


def _load_resident(pairs, sem):
    @pl.when(pl.program_id(0) == 0)
    def _():
        copies = [pltpu.make_async_copy(src, dst, sem.at[i]) for i, (src, dst) in enumerate(pairs)]
        for cp in copies:
            cp.start()
        for cp in copies:
            cp.wait()


def _row_spec(tm, width):
    return pl.BlockSpec((tm, width), lambda i: (i, 0))


def _col_spec(tm, width, col):
    return pl.BlockSpec((tm, width), lambda i: (i, col))


def _prev_halo_spec(tm, width, col=0):
    per = tm // HALO
    return pl.BlockSpec((HALO, width), lambda i: (jnp.maximum(i * per - 1, 0), col))


def _next_halo_spec(tm, width, n_rows):
    per = tm // HALO
    last = n_rows // HALO - 1
    return pl.BlockSpec((HALO, width), lambda i: (jnp.minimum((i + 1) * per, last), 0))


def _const_spec(shape):
    return pl.BlockSpec(shape, lambda i: (0,) * len(shape))


ANY = pl.BlockSpec(memory_space=pl.ANY)


def _shift_down(cur, prev, n):
    rows = lax.broadcasted_iota(jnp.int32, cur.shape, 0)
    out = pltpu.roll(cur, n, 0)
    for j in range(n):
        out = jnp.where(rows == j, prev[HALO - n + j:HALO - n + j + 1, :], out)
    return out


def _shift_up(cur, nxt, n):
    tm = cur.shape[0]
    rows = lax.broadcasted_iota(jnp.int32, cur.shape, 0)
    out = pltpu.roll(cur, tm - n, 0)
    for j in range(n):
        out = jnp.where(rows == tm - n + j, nxt[j:j + 1, :], out)
    return out


def _conv_taps(cm, cm_prev, wconv):
    cm1 = _shift_down(cm, cm_prev, 1)
    cm2 = _shift_down(cm, cm_prev, 2)
    cv = wconv[2:3, :] * cm + wconv[1:2, :] * cm1 + wconv[0:1, :] * cm2
    return cv, cm1, cm2


def _in_proj_fwd(x, g1, b_gate, w_in, aw, cw, tm):
    s, d = x.shape
    ni = w_in.shape[1]
    n_qkv, n_conv = 3 * aw, 3 * cw
    ch = PROJ_CHUNK

    def body(x_ref, g_ref, b_ref, w_hbm, qkv_ref, conv_ref, gate_ref, w_vmem, sem):
        _load_resident([(w_hbm, w_vmem)], sem)
        xv = x_ref[...]
        h = (xv * _rms_scale(xv) * g_ref[...]).astype(BF16)
        for c0 in range(0, ni, ch):
            pc = _dot(h, w_vmem[:, c0:c0 + ch])
            if c0 < n_qkv:
                qkv_ref[:, c0:c0 + ch] = pc.astype(BF16)
            elif c0 < n_qkv + n_conv:
                conv_ref[:, c0 - n_qkv:c0 - n_qkv + ch] = pc
            else:
                g0 = c0 - n_qkv - n_conv
                gate_ref[:, g0:g0 + ch] = _sigmoid(pc + b_ref[:, g0:g0 + ch])

    return pl.pallas_call(
        body, name="in_proj_fwd", grid=(s // tm,),
        in_specs=[_row_spec(tm, d), _const_spec((1, d)), _const_spec((1, 2 * d)), ANY],
        out_specs=[_row_spec(tm, n_qkv), _row_spec(tm, n_conv), _row_spec(tm, 2 * d)],
        out_shape=[jax.ShapeDtypeStruct((s, n_qkv), BF16), jax.ShapeDtypeStruct((s, n_conv), F32),
                   jax.ShapeDtypeStruct((s, 2 * d), F32)],
        scratch_shapes=[pltpu.VMEM((d, ni), BF16), pltpu.SemaphoreType.DMA((1,))],
        compiler_params=_params(1),
    )(x, g1, b_gate, w_in)


def _split_hi_lo(a):
    hi = a.astype(BF16)
    return hi, (a - hi.astype(F32)).astype(BF16)


def _log_gates(z):
    t = jnp.exp(-jnp.abs(z))
    u = 1.0 + t
    sp = jnp.log(u)
    return jnp.minimum(z, 0.0) - sp, jnp.minimum(-z, 0.0) - sp, t, u


def _attn_masks(t):
    row = lax.broadcasted_iota(jnp.int32, (t, t), 0)
    col = lax.broadcasted_iota(jnp.int32, (t, t), 1)
    return col < row, (row > col).astype(BF16), (row >= col).astype(BF16)


def _while_weights_live(qi, block, carry):
    def cond(state):
        j, carry = state
        return jnp.logical_and(j < qi, jnp.max(carry[0]) >= EXP_ZERO)

    def step(state):
        j, carry = state
        return j + 1, block(qi - 1 - j, carry)

    return lax.while_loop(cond, step, (jnp.int32(0), carry))[1]


def _head_lanes(h):
    lane = lax.broadcasted_iota(jnp.int32, (1, LANES), 1)
    return (lane >= HEAD_DIM * h) & (lane < HEAD_DIM * (h + 1))


def _attn_fwd(qkv, aw, t, exchange=None):
    s = qkv.shape[0]
    groups = aw // LANES
    nq = s // t
    scale = HEAD_DIM ** -0.5
    ex = exchange or _NO_EXCHANGE

    def body(q_ref, k_ref, v_ref, *rest):
        ex_in, (o_ref,), ex_out, sems = _split_refs(rest, ex, 1)
        qi = pl.program_id(1)
        _exchange_start(ex, ex_in, ex_out, sems)
        causal, upper, _ = _attn_masks(t)
        q = q_ref[...] * scale
        outs = []
        for h in range(HEADS_PER_GROUP):
            qm = jnp.where(_head_lanes(h), q, jnp.zeros_like(q))

            def block(kb, run, acc, diag):
                rows = pl.ds(pl.multiple_of(kb * t, t), t)
                z = _dot_nt(qm, k_ref[rows, :])
                log_b, log_keep, _, _ = _log_gates(z)
                if diag:
                    log_keep = jnp.where(causal, log_keep, 0.0)
                hi, lo = _split_hi_lo(log_keep)
                between = _dot(hi, upper) + _dot(lo, upper) + run
                w = jnp.exp(log_b + between)
                if diag:
                    w = jnp.where(causal, w, 0.0)
                acc = acc + _dot(w.astype(BF16), v_ref[rows, :])
                return run + jnp.sum(log_keep, axis=1, keepdims=True), acc

            run, acc = block(qi, jnp.zeros((t, 1), F32), jnp.zeros((t, LANES), F32), True)

            _, acc = _while_weights_live(qi, lambda kb, carry: block(kb, *carry, False), (run, acc))
            outs.append(acc)
        o_ref[...] = jnp.where(_head_lanes(0), outs[0], outs[1])
        _exchange_wait(ex, ex_in, ex_out, sems, groups, nq)

    return pl.pallas_call(
        body, name="attn_fwd", grid=(groups, nq),
        in_specs=[pl.BlockSpec((t, LANES), lambda g, i: (i, g)),
                  pl.BlockSpec((s, LANES), lambda g, i: (0, groups + g)),
                  pl.BlockSpec((s, LANES), lambda g, i: (0, 2 * groups + g))] + [ANY] * len(ex.arrays),
        out_specs=[pl.BlockSpec((t, LANES), lambda g, i: (i, g))] + [ANY] * len(ex.out_shapes),
        out_shape=[jax.ShapeDtypeStruct((s, aw), F32)] + ex.out_shapes,
        scratch_shapes=_exchange_sems(ex),
        compiler_params=_params(2),
    )(qkv, qkv, qkv, *ex.arrays)


def _attn_bwd(qkv, o, do, aw, t, exchange=None):
    s = qkv.shape[0]
    groups = aw // LANES
    nq = s // t
    scale = HEAD_DIM ** -0.5
    ex = exchange or _NO_EXCHANGE

    def body(q_ref, k_ref, v_ref, o_ref, do_ref, *rest):
        ex_in, (dq_ref, dk_ref, dv_ref), ex_out, (dk_acc, dv_acc, *sems) = _split_refs(rest, ex, 3)
        qi = pl.program_id(1)
        _exchange_start(ex, ex_in, ex_out, sems)

        @pl.when(qi == 0)
        def _():
            dk_acc[...] = jnp.zeros_like(dk_acc)
            dv_acc[...] = jnp.zeros_like(dv_acc)

        causal, upper, lower_incl = _attn_masks(t)
        q = q_ref[...] * scale
        do_b = do_ref[...]
        do_o = do_b.astype(F32) * o_ref[...]
        dqs = []
        for h in range(HEADS_PER_GROUP):
            lanes = _head_lanes(h)
            qm = jnp.where(lanes, q, jnp.zeros_like(q))
            dom = jnp.where(lanes, do_b, jnp.zeros_like(do_b))
            e_total = jnp.sum(jnp.where(lanes, do_o, 0.0), axis=1, keepdims=True)

            def block(kb, run, e_run, dq, diag):
                rows = pl.ds(pl.multiple_of(kb * t, t), t)
                k = k_ref[rows, :]
                v = v_ref[rows, :]
                z = _dot_nt(qm, k)
                log_b, log_keep, tt, u = _log_gates(z)
                r = 1.0 / u
                beta = jnp.where(z >= 0.0, r, tt * r)
                keep = jnp.where(z >= 0.0, tt * r, r)
                if diag:
                    log_keep = jnp.where(causal, log_keep, 0.0)
                hi, lo = _split_hi_lo(log_keep)
                between = _dot(hi, upper) + _dot(lo, upper) + run
                w = jnp.exp(log_b + between)
                if diag:
                    w = jnp.where(causal, w, 0.0)
                wb = w.astype(BF16)
                e = _dot_nt(dom, v) * wb.astype(F32)
                hi, lo = _split_hi_lo(e)
                e_suffix = _dot(hi, lower_incl) + _dot(lo, lower_incl) + e_run
                dz = e * keep - (e_total - e_suffix) * beta
                if diag:
                    dz = jnp.where(causal, dz, 0.0)
                dzb = dz.astype(BF16)
                dk_acc[rows, :] += _dot_tn(dzb, qm)
                dv_acc[rows, :] += _dot_tn(wb, dom)
                return (run + jnp.sum(log_keep, axis=1, keepdims=True), e_suffix[:, 0:1], dq + _dot(dzb, k))

            zero_col = jnp.zeros((t, 1), F32)
            run, e_run, dq = block(qi, zero_col, zero_col, jnp.zeros((t, LANES), F32), True)

            _, _, dq = _while_weights_live(qi, lambda kb, carry: block(kb, *carry, False), (run, e_run, dq))
            dqs.append(dq)
        dq_ref[...] = (jnp.where(_head_lanes(0), dqs[0], dqs[1]) * scale).astype(BF16)

        @pl.when(qi == nq - 1)
        def _():
            dk_ref[...] = dk_acc[...].astype(BF16)
            dv_ref[...] = dv_acc[...].astype(BF16)

        _exchange_wait(ex, ex_in, ex_out, sems, groups, nq)

    blk = pl.BlockSpec((t, LANES), lambda g, i: (i, g))
    slab = pl.BlockSpec((s, LANES), lambda g, i: (0, g))
    return pl.pallas_call(
        body, name="attn_bwd", grid=(groups, nq),
        in_specs=[blk, pl.BlockSpec((s, LANES), lambda g, i: (0, groups + g)),
                  pl.BlockSpec((s, LANES), lambda g, i: (0, 2 * groups + g)), blk, blk] + [ANY] * len(ex.arrays),
        out_specs=[blk, slab, slab] + [ANY] * len(ex.out_shapes),
        out_shape=[jax.ShapeDtypeStruct((s, aw), BF16)] * 3 + ex.out_shapes,
        scratch_shapes=[pltpu.VMEM((s, LANES), F32), pltpu.VMEM((s, LANES), F32)] + _exchange_sems(ex),
        compiler_params=_params(2),
    )(qkv, qkv, qkv, o, do, *ex.arrays)


def _branches(o_b, conv, conv_prev, wconv, w_ao, w_co, cw, first):
    cb = conv[:, 0:cw]
    cm = conv[:, cw:2 * cw] * conv[:, 2 * cw:3 * cw]
    cm_prev = conv_prev[:, cw:2 * cw] * conv_prev[:, 2 * cw:3 * cw]
    cm_prev = jnp.where(first, 0.0, cm_prev)
    cv, cm1, cm2 = _conv_taps(cm, cm_prev, wconv)
    conv_in = (cb * cv).astype(BF16)
    return _dot(o_b, w_ao), _dot(conv_in, w_co), conv_in, cb, cv, cm, cm1, cm2


def _mix_fwd(x, o, conv, gate, wconv, g_post, w_ao, w_co, w_o, tm):
    s, d = x.shape
    aw, cw = w_ao.shape[0], w_co.shape[0]

    def body(x_ref, o_ref, conv_ref, prev_ref, gate_ref, wc_ref, g_ref, wao_hbm, wco_hbm, wo_hbm,
             x1_ref, mixed_ref, mixin_ref, convin_ref, wao, wco, wo, sem):
        _load_resident([(wao_hbm, wao), (wco_hbm, wco), (wo_hbm, wo)], sem)
        y_attn, y_conv, conv_in, *_ = _branches(
            o_ref[...].astype(BF16), conv_ref[...], prev_ref[...], wc_ref[...], wao[...], wco[...], cw,
            pl.program_id(0) == 0)
        mix_in = (gate_ref[:, 0:d] * y_attn + gate_ref[:, d:2 * d] * y_conv).astype(BF16)
        mixed = _dot(mix_in, wo[...])
        x1_ref[...] = x_ref[...] + mixed * _rms_scale(mixed) * g_ref[...]
        mixed_ref[...] = mixed
        mixin_ref[...] = mix_in
        convin_ref[...] = conv_in

    return pl.pallas_call(
        body, name="mix_fwd", grid=(s // tm,),
        in_specs=[_row_spec(tm, d), _row_spec(tm, aw), _row_spec(tm, 3 * cw), _prev_halo_spec(tm, 3 * cw),
                  _row_spec(tm, 2 * d), _const_spec((CONV_K, cw)), _const_spec((1, d)), ANY, ANY, ANY],
        out_specs=[_row_spec(tm, d), _row_spec(tm, d), _row_spec(tm, d), _row_spec(tm, cw)],
        out_shape=[jax.ShapeDtypeStruct((s, d), F32), jax.ShapeDtypeStruct((s, d), F32),
                   jax.ShapeDtypeStruct((s, d), BF16), jax.ShapeDtypeStruct((s, cw), BF16)],
        scratch_shapes=[pltpu.VMEM(w_ao.shape, BF16), pltpu.VMEM(w_co.shape, BF16), pltpu.VMEM(w_o.shape, BF16),
                        pltpu.SemaphoreType.DMA((3,))],
        compiler_params=_params(1),
    )(x, o, conv, conv, gate, wconv, g_post, w_ao, w_co, w_o)


def _mix_bwd(dx1, mixed, o, conv, gate, wconv, g_post, w_ao, w_co, w_o, tm):
    s, d = dx1.shape
    aw, cw = w_ao.shape[0], w_co.shape[0]

    def body(dx1_ref, mixed_ref, o_ref, conv_ref, prev_ref, gate_ref, wc_ref, g_ref, wao_hbm, wco_hbm, wo_hbm,
             dmixed_ref, dattn_ref, dconvout_ref, do_ref, dgate_ref, dcb_ref, dcv_ref, dg_ref, dbias_ref,
             wao, wco, wo, sem):
        i = pl.program_id(0)
        _load_resident([(wao_hbm, wao), (wco_hbm, wco), (wo_hbm, wo)], sem)

        @pl.when(i == 0)
        def _():
            dg_ref[...] = jnp.zeros_like(dg_ref)
            dbias_ref[...] = jnp.zeros_like(dbias_ref)

        mixed = mixed_ref[...]
        r = _rms_scale(mixed)
        mhat = mixed * r
        dn = dx1_ref[...]
        dg_ref[...] += jnp.sum(dn * mhat, axis=0, keepdims=True)
        dmixed = _rms_bwd(mhat, r, g_ref[...], dn).astype(BF16)
        dmixed_ref[...] = dmixed
        dmi = _dot_nt(dmixed, wo[...])

        y_attn, y_conv, _, cb, cv, *_ = _branches(
            o_ref[...].astype(BF16), conv_ref[...], prev_ref[...], wc_ref[...], wao[...], wco[...], cw, i == 0)
        ga = gate_ref[:, 0:d]
        gc = gate_ref[:, d:2 * d]
        dpre_a = dmi * y_attn * ga * (1.0 - ga)
        dpre_c = dmi * y_conv * gc * (1.0 - gc)
        dgate_ref[:, 0:d] = dpre_a.astype(BF16)
        dgate_ref[:, d:2 * d] = dpre_c.astype(BF16)
        dbias_ref[:, 0:d] += jnp.sum(dpre_a, axis=0, keepdims=True)
        dbias_ref[:, d:2 * d] += jnp.sum(dpre_c, axis=0, keepdims=True)

        dattn = (dmi * ga).astype(BF16)
        dattn_ref[...] = dattn
        do_ref[...] = _dot_nt(dattn, wao[...]).astype(BF16)
        dconvout = (dmi * gc).astype(BF16)
        dconvout_ref[...] = dconvout
        dconv_in = _dot_nt(dconvout, wco[...])
        dcb_ref[...] = (dconv_in * cv).astype(BF16)
        dcv_ref[...] = dconv_in * cb

    return pl.pallas_call(
        body, name="mix_bwd", grid=(s // tm,),
        in_specs=[_row_spec(tm, d), _row_spec(tm, d), _row_spec(tm, aw), _row_spec(tm, 3 * cw),
                  _prev_halo_spec(tm, 3 * cw), _row_spec(tm, 2 * d), _const_spec((CONV_K, cw)), _const_spec((1, d)),
                  ANY, ANY, ANY],
        out_specs=[_row_spec(tm, d), _row_spec(tm, d), _row_spec(tm, d), _row_spec(tm, aw), _row_spec(tm, 2 * d),
                   _row_spec(tm, cw), _row_spec(tm, cw), _const_spec((1, d)), _const_spec((1, 2 * d))],
        out_shape=[jax.ShapeDtypeStruct((s, d), BF16), jax.ShapeDtypeStruct((s, d), BF16),
                   jax.ShapeDtypeStruct((s, d), BF16), jax.ShapeDtypeStruct((s, aw), BF16),
                   jax.ShapeDtypeStruct((s, 2 * d), BF16), jax.ShapeDtypeStruct((s, cw), BF16),
                   jax.ShapeDtypeStruct((s, cw), F32), jax.ShapeDtypeStruct((1, d), F32),
                   jax.ShapeDtypeStruct((1, 2 * d), F32)],
        scratch_shapes=[pltpu.VMEM(w_ao.shape, BF16), pltpu.VMEM(w_co.shape, BF16), pltpu.VMEM(w_o.shape, BF16),
                        pltpu.SemaphoreType.DMA((3,))],
        compiler_params=_params(1),
    )(dx1, mixed, o, conv, conv, gate, wconv, g_post, w_ao, w_co, w_o)


def _mlp_ple_loss(x1, p, target, g_pre, g_post, g_ple, w_up, w_dn, w_pg, w_pp, tm):
    s, d = x1.shape
    ff = w_up.shape[1]
    pd = p.shape[1]
    fc = FF_CHUNK

    def body(x1_ref, p_ref, t_ref, gpre_ref, gpost_ref, gple_ref, wup_hbm, wdn_hbm, wpg_hbm, wpp_hbm,
             dx1_ref, h2_ref, du_ref, a_ref, df_ref, h3_ref, ds3_ref, dpp_ref, loss_ref, dgpre_ref, dgpost_ref,
             dgple_ref, wup, wdn, wpg, wpp, u_scr, sem):
        _load_resident([(wup_hbm, wup), (wdn_hbm, wdn), (wpg_hbm, wpg), (wpp_hbm, wpp)], sem)

        @pl.when(pl.program_id(0) == 0)
        def _():
            for ref in (loss_ref, dgpre_ref, dgpost_ref, dgple_ref):
                ref[...] = jnp.zeros_like(ref)

        x1v = x1_ref[...]
        r2 = _rms_scale(x1v)
        x1hat = x1v * r2
        h2 = (x1hat * gpre_ref[...]).astype(BF16)
        h2_ref[...] = h2
        f = jnp.zeros((tm, d), F32)
        for c0 in range(0, ff, fc):
            u = _dot(h2, wup[:, c0:c0 + fc])
            u_scr[:, c0:c0 + fc] = u
            a = jnp.square(jnp.maximum(u, 0.0)).astype(BF16)
            a_ref[:, c0:c0 + fc] = a
            f = f + _dot(a, wdn[c0:c0 + fc, :])
        rf = _rms_scale(f)
        fhat = f * rf
        x2 = x1v + fhat * gpost_ref[...]
        r3 = _rms_scale(x2)
        x2hat = x2 * r3
        h3 = (x2hat * gple_ref[...]).astype(BF16)
        h3_ref[...] = h3
        pg = _sigmoid(_dot(h3, wpg[...]))
        pp = _dot(p_ref[...].astype(BF16), wpp[...])
        diff = x2 + pg * pp - t_ref[...]
        loss_ref[...] += 0.5 * jnp.sum(jnp.mean(diff * diff, axis=-1, keepdims=True), axis=0, keepdims=True)

        dy = diff * (1.0 / d)
        dpp_ref[...] = (dy * pg).astype(BF16)
        ds3 = (dy * pp * pg * (1.0 - pg)).astype(BF16)
        ds3_ref[...] = ds3
        dh3 = _dot_nt(ds3, wpg[...])
        dgple_ref[...] += jnp.sum(dh3 * x2hat, axis=0, keepdims=True)
        dx2 = dy + _rms_bwd(x2hat, r3, gple_ref[...], dh3)
        dgpost_ref[...] += jnp.sum(dx2 * fhat, axis=0, keepdims=True)
        df = _rms_bwd(fhat, rf, gpost_ref[...], dx2).astype(BF16)
        df_ref[...] = df
        dh2 = jnp.zeros((tm, d), F32)
        for c0 in range(0, ff, fc):
            da = _dot_nt(df, wdn[c0:c0 + fc, :])
            du = (da * (2.0 * jnp.maximum(u_scr[:, c0:c0 + fc], 0.0))).astype(BF16)
            du_ref[:, c0:c0 + fc] = du
            dh2 = dh2 + _dot_nt(du, wup[:, c0:c0 + fc])
        dgpre_ref[...] += jnp.sum(dh2 * x1hat, axis=0, keepdims=True)
        dx1_ref[...] = dx2 + _rms_bwd(x1hat, r2, gpre_ref[...], dh2)

    vec = _const_spec((1, d))
    return pl.pallas_call(
        body, name="mlp_ple_loss", grid=(s // tm,),
        in_specs=[_row_spec(tm, d), _row_spec(tm, pd), _row_spec(tm, d), vec, vec, vec, ANY, ANY, ANY, ANY],
        out_specs=[_row_spec(tm, d), _row_spec(tm, d), _row_spec(tm, ff), _row_spec(tm, ff), _row_spec(tm, d),
                   _row_spec(tm, d), _row_spec(tm, d), _row_spec(tm, d), _const_spec((1, 1)), vec, vec, vec],
        out_shape=[jax.ShapeDtypeStruct((s, d), F32), jax.ShapeDtypeStruct((s, d), BF16),
                   jax.ShapeDtypeStruct((s, ff), BF16), jax.ShapeDtypeStruct((s, ff), BF16),
                   jax.ShapeDtypeStruct((s, d), BF16), jax.ShapeDtypeStruct((s, d), BF16),
                   jax.ShapeDtypeStruct((s, d), BF16), jax.ShapeDtypeStruct((s, d), BF16),
                   jax.ShapeDtypeStruct((1, 1), F32), jax.ShapeDtypeStruct((1, d), F32),
                   jax.ShapeDtypeStruct((1, d), F32), jax.ShapeDtypeStruct((1, d), F32)],
        scratch_shapes=[pltpu.VMEM(w_up.shape, BF16), pltpu.VMEM(w_dn.shape, BF16), pltpu.VMEM(w_pg.shape, BF16),
                        pltpu.VMEM(w_pp.shape, BF16), pltpu.VMEM((tm, ff), F32), pltpu.SemaphoreType.DMA((4,))],
        compiler_params=_params(1),
    )(x1, p, target, g_pre, g_post, g_ple, w_up, w_dn, w_pg, w_pp)


def _in_proj_bwd(x, dx1, dq, dk, dv, dcb, dcv, conv, dgate, wconv, g1, w_in, tm):
    s, d = x.shape
    aw, cw = dq.shape[1], dcb.shape[1]
    ni = w_in.shape[1]

    def body(x_ref, dx1_ref, dq_ref, dk_ref, dv_ref, dcb_ref, dcv_ref, dcvn_ref, cc_ref, cu_ref, ccp_ref, cup_ref,
             dgate_ref, wc_ref, g_ref, w_hbm, dx_ref, dproj_ref, h1_ref, dg_ref, dwc_ref, w_vmem, sem):
        i = pl.program_id(0)
        _load_resident([(w_hbm, w_vmem)], sem)

        @pl.when(i == 0)
        def _():
            dg_ref[...] = jnp.zeros_like(dg_ref)
            dwc_ref[...] = jnp.zeros_like(dwc_ref)

        wc = wc_ref[...]
        cc = cc_ref[...]
        cu = cu_ref[...]
        cm = cc * cu
        cm_prev = jnp.where(i == 0, 0.0, ccp_ref[...] * cup_ref[...])
        _, cm1, cm2 = _conv_taps(cm, cm_prev, wc)
        dcv_cur = dcv_ref[...]
        dcv_next = jnp.where(i == pl.num_programs(0) - 1, 0.0, dcvn_ref[...])
        dcm = (wc[2:3, :] * dcv_cur + wc[1:2, :] * _shift_up(dcv_cur, dcv_next, 1)
               + wc[0:1, :] * _shift_up(dcv_cur, dcv_next, 2))
        for tap, shifted in enumerate((cm2, cm1, cm)):
            dwc_ref[tap:tap + 1, :] += jnp.sum(dcv_cur * shifted, axis=0, keepdims=True)

        pieces = [(dq_ref[...], aw), (dk_ref[...], aw), (dv_ref[...], aw), (dcb_ref[...], cw),
                  ((dcm * cu).astype(BF16), cw), ((dcm * cc).astype(BF16), cw), (dgate_ref[...], 2 * d)]
        dh = jnp.zeros((tm, d), F32)
        c0 = 0
        for piece, width in pieces:
            dproj_ref[:, c0:c0 + width] = piece
            dh = dh + _dot_nt(piece, w_vmem[:, c0:c0 + width])
            c0 += width

        xv = x_ref[...]
        r = _rms_scale(xv)
        xhat = xv * r
        h1_ref[...] = (xhat * g_ref[...]).astype(BF16)
        dg_ref[...] += jnp.sum(dh * xhat, axis=0, keepdims=True)
        dx_ref[...] = dx1_ref[...] + _rms_bwd(xhat, r, g_ref[...], dh)

    return pl.pallas_call(
        body, name="in_proj_bwd", grid=(s // tm,),
        in_specs=[_row_spec(tm, d), _row_spec(tm, d), _row_spec(tm, aw), _row_spec(tm, aw), _row_spec(tm, aw),
                  _row_spec(tm, cw), _row_spec(tm, cw), _next_halo_spec(tm, cw, s), _col_spec(tm, cw, 1),
                  _col_spec(tm, cw, 2), _prev_halo_spec(tm, cw, 1), _prev_halo_spec(tm, cw, 2), _row_spec(tm, 2 * d),
                  _const_spec((CONV_K, cw)), _const_spec((1, d)), ANY],
        out_specs=[_row_spec(tm, d), _row_spec(tm, ni), _row_spec(tm, d), _const_spec((1, d)),
                   _const_spec((CONV_K, cw))],
        out_shape=[jax.ShapeDtypeStruct((s, d), F32), jax.ShapeDtypeStruct((s, ni), BF16),
                   jax.ShapeDtypeStruct((s, d), BF16), jax.ShapeDtypeStruct((1, d), F32),
                   jax.ShapeDtypeStruct((CONV_K, cw), F32)],
        scratch_shapes=[pltpu.VMEM((d, ni), BF16), pltpu.SemaphoreType.DMA((1,))],
        compiler_params=_params(1),
    )(x, dx1, dq, dk, dv, dcb, dcv, dcv, conv, conv, conv, conv, dgate, wconv, g1, w_in)


def _weight_grad(a, b, name):
    s, m = a.shape
    n = b.shape[1]
    tm, tn, tk = min(m, DW_TILE), min(n, DW_TILE), min(s, DW_TOKENS)
    nk = s // tk

    def body(a_ref, b_ref, o_ref, acc):
        k = pl.program_id(2)

        @pl.when(k == 0)
        def _():
            acc[...] = jnp.zeros_like(acc)

        acc[...] += _dot_tn(a_ref[...].astype(BF16), b_ref[...].astype(BF16))

        @pl.when(k == nk - 1)
        def _():
            o_ref[...] = acc[...].astype(BF16)

    return pl.pallas_call(
        body, name=name, grid=(m // tm, n // tn, nk),
        in_specs=[pl.BlockSpec((tk, tm), lambda i, j, k: (k, i)), pl.BlockSpec((tk, tn), lambda i, j, k: (k, j))],
        out_specs=pl.BlockSpec((tm, tn), lambda i, j, k: (i, j)),
        out_shape=jax.ShapeDtypeStruct((m, n), BF16),
        scratch_shapes=[pltpu.VMEM((tm, tn), F32)],
        compiler_params=_params(3),
    )(a, b)


def _mesh_position():
    return tuple(lax.axis_index(a) for a in MESH_AXES)


def _peer(me, k):
    bits = ((k >> 2) & 1, (k >> 1) & 1, k & 1)
    pos = tuple(1 - m if b else m for m, b in zip(me, bits))
    return pos, 4 * pos[0] + 2 * pos[1] + pos[2]


class _Exchange:
    def __init__(self, arrays, out_shapes, src, dst):
        self.arrays, self.out_shapes, self.src, self.dst = list(arrays), list(out_shapes), src, dst


_NO_EXCHANGE = _Exchange([], [], None, None)


def _exchange_sems(ex):
    n = len(ex.arrays)
    if n == 0:
        return []
    return [pltpu.SemaphoreType.DMA((n, N_DEV - 1)), pltpu.SemaphoreType.DMA((n, N_DEV - 1)),
            pltpu.SemaphoreType.DMA((n,))]


def _split_refs(rest, ex, n_own_outs):
    n_in, n_out = len(ex.arrays), len(ex.out_shapes)
    ex_in, rest = rest[:n_in], rest[n_in:]
    own, rest = rest[:n_own_outs], rest[n_own_outs:]
    return ex_in, own, rest[:n_out], rest[n_out:]


def _exchange_copies(ex, in_refs, out_refs, sems):
    send_sems, recv_sems, local_sems = sems
    me = _mesh_position()
    mine = 4 * me[0] + 2 * me[1] + me[2]
    copies = []
    for w in range(len(ex.arrays)):
        landing = ex.dst(w, out_refs, mine)
        copies.append(pltpu.make_async_copy(ex.src(w, in_refs, mine), landing, local_sems.at[w]))
        for k in range(1, N_DEV):
            peer, peer_idx = _peer(me, k)
            copies.append(pltpu.make_async_remote_copy(
                src_ref=ex.src(w, in_refs, peer_idx), dst_ref=landing, send_sem=send_sems.at[w, k - 1],
                recv_sem=recv_sems.at[w, k - 1], device_id=peer, device_id_type=pl.DeviceIdType.MESH))
    return copies


def _exchange_start(ex, in_refs, out_refs, sems):
    if ex.arrays:
        @pl.when(jnp.logical_and(pl.program_id(0) == 0, pl.program_id(1) == 0))
        def _():
            for cp in _exchange_copies(ex, in_refs, out_refs, sems):
                cp.start()


def _exchange_wait(ex, in_refs, out_refs, sems, n0, n1):
    if ex.arrays:
        @pl.when(jnp.logical_and(pl.program_id(0) == n0 - 1, pl.program_id(1) == n1 - 1))
        def _():
            for cp in _exchange_copies(ex, in_refs, out_refs, sems):
                cp.wait()


def _shard_block(ref, shard_shape, by_col, idx):
    r, c = shard_shape
    if by_col:
        return ref.at[:, pl.ds(pl.multiple_of(idx * c, LANES), c)]
    return ref.at[pl.ds(pl.multiple_of(idx * r, 16), r), :]


def _full_shape(shard_shape, by_col):
    r, c = shard_shape
    return (r, N_DEV * c) if by_col else (N_DEV * r, c)


def _gather_exchange(shards, col_sharded):
    shapes = [a.shape for a in shards]
    return _Exchange(
        shards, [jax.ShapeDtypeStruct(_full_shape(sh, bc), a.dtype) for a, sh, bc in zip(shards, shapes, col_sharded)],
        lambda w, refs, idx: refs[w],
        lambda w, refs, mine: _shard_block(refs[w], shapes[w], col_sharded[w], mine))


def _scatter_exchange(grads, col_sharded):
    shapes = []
    for g, by_col in zip(grads, col_sharded):
        r, c = g.shape
        shapes.append((r, c // N_DEV) if by_col else (r // N_DEV, c))
    return _Exchange(
        grads, [jax.ShapeDtypeStruct((N_DEV,) + sh, g.dtype) for g, sh in zip(grads, shapes)],
        lambda w, refs, idx: _shard_block(refs[w], shapes[w], col_sharded[w], idx),
        lambda w, refs, mine: refs[w].at[mine])


def _broadcast_exchange(arrays):
    return _Exchange(arrays, [jax.ShapeDtypeStruct((N_DEV,) + a.shape, a.dtype) for a in arrays],
                     lambda w, refs, idx: refs[w], lambda w, refs, mine: refs[w].at[mine])


def _join(*exs):
    arrays, shapes, owner = [], [], []
    for e in exs:
        for w in range(len(e.arrays)):
            owner.append((e, w, len(arrays), len(shapes)))
        arrays += e.arrays
        shapes += e.out_shapes

    def src(w, refs, idx):
        e, w0, i0, _ = owner[w]
        return e.src(w0, refs[i0:i0 + len(e.arrays)], idx)

    def dst(w, refs, mine):
        e, w0, _, o0 = owner[w]
        return e.dst(w0, refs[o0:o0 + len(e.out_shapes)], mine)

    return _Exchange(arrays, shapes, src, dst)


def _exchange_call(ex, name):
    n_in = len(ex.arrays)

    def body(*refs):
        in_refs, _, out_refs, sems = _split_refs(refs, ex, 0)
        copies = _exchange_copies(ex, in_refs, out_refs, sems)
        for cp in copies:
            cp.start()
        for cp in copies:
            cp.wait()

    return pl.pallas_call(
        body, name=name, in_specs=[ANY] * n_in, out_specs=[ANY] * len(ex.out_shapes), out_shape=ex.out_shapes,
        scratch_shapes=_exchange_sems(ex), compiler_params=pltpu.CompilerParams(vmem_limit_bytes=VMEM_LIMIT),
    )(*ex.arrays)


def _to_bf16(arrays):
    def body(*refs):
        for src, dst in zip(refs[:len(arrays)], refs[len(arrays):]):
            dst[...] = src[...].astype(BF16)

    vmem = pl.BlockSpec(memory_space=pltpu.VMEM)
    return pl.pallas_call(
        body, name="weights_to_bf16", in_specs=[vmem] * len(arrays), out_specs=[vmem] * len(arrays),
        out_shape=[jax.ShapeDtypeStruct(a.shape, BF16) for a in arrays],
        compiler_params=pltpu.CompilerParams(vmem_limit_bytes=VMEM_LIMIT),
    )(*arrays)


def _adamw(w, g, m, v):
    m = ADAM_B1 * m + (1.0 - ADAM_B1) * g
    v = ADAM_B2 * v + (1.0 - ADAM_B2) * jnp.square(g)
    m_hat = m / (1.0 - ADAM_B1 ** ADAM_STEP)
    v_hat = v / (1.0 - ADAM_B2 ** ADAM_STEP)
    delta = -ADAM_LR * (m_hat / (jnp.sqrt(v_hat) + ADAM_EPS) + ADAM_WD * w)
    return delta, m, v


def _sum_and_adamw(parts, w, m, v, name):
    r, c = w.shape
    tr = min(r, 256)

    def body(p_ref, w_ref, m_ref, v_ref, g_out, d_out, m_out, v_out):
        g = p_ref[0].astype(F32)
        for dev in range(1, N_DEV):
            g = g + p_ref[dev].astype(F32)
        g_out[...] = g
        d_out[...], m_out[...], v_out[...] = _adamw(w_ref[...], g, m_ref[...], v_ref[...])

    blk = pl.BlockSpec((tr, c), lambda i: (i, 0))
    return pl.pallas_call(
        body, name=name, grid=(r // tr,),
        in_specs=[pl.BlockSpec((N_DEV, tr, c), lambda i: (0, i, 0)), blk, blk, blk],
        out_specs=[blk] * 4, out_shape=[jax.ShapeDtypeStruct((r, c), F32)] * 4,
        compiler_params=_params(1),
    )(parts, w, m, v)


BIG = ("w_in", "w_attn_out", "w_conv_out", "w_o", "w_up", "w_down", "w_ple_gate", "w_ple_proj")
COL_SHARDED = {"w_in": True, "w_attn_out": True, "w_conv_out": True, "w_o": False, "w_up": True, "w_down": False,
               "w_ple_gate": False, "w_ple_proj": True}
SMALL = ("g_pre_mix", "b_gate", "g_post_mix", "g_pre_mlp", "g_post_mlp", "g_ple")


REST = BIG[1:]


def _local_grads(x, p, target, small, wconv, full, aw, cw, tm, t, gather_rest=None, scatter_rest=None):
    full = dict(full)
    qkv, conv, gate = _in_proj_fwd(x, small["g_pre_mix"], small["b_gate"], full["w_in"], aw, cw, tm)
    o, *rest = _attn_fwd(qkv, aw, t, gather_rest)
    full.update(zip(REST, rest))
    x1, mixed, mix_in, conv_in = _mix_fwd(x, o, conv, gate, wconv, small["g_post_mix"], full["w_attn_out"],
                                          full["w_conv_out"], full["w_o"], tm)
    (dx1, h2, du, a, df, h3, ds3, dpp, loss, dg_pre_mlp, dg_post_mlp, dg_ple) = _mlp_ple_loss(
        x1, p, target, small["g_pre_mlp"], small["g_post_mlp"], small["g_ple"], full["w_up"], full["w_down"],
        full["w_ple_gate"], full["w_ple_proj"], tm)
    big = {"w_up": _weight_grad(h2, du, "dw_up"), "w_down": _weight_grad(a, df, "dw_down"),
           "w_ple_gate": _weight_grad(h3, ds3, "dw_ple_gate"), "w_ple_proj": _weight_grad(p, dpp, "dw_ple_proj")}
    (dmixed, dattn, dconvout, do, dgate, dcb, dcv, dg_post_mix, db_gate) = _mix_bwd(
        dx1, mixed, o, conv, gate, wconv, small["g_post_mix"], full["w_attn_out"], full["w_conv_out"], full["w_o"],
        tm)
    big.update({"w_attn_out": _weight_grad(o, dattn, "dw_attn_out"),
                "w_conv_out": _weight_grad(conv_in, dconvout, "dw_conv_out"),
                "w_o": _weight_grad(mix_in, dmixed, "dw_o")})
    dq, dk, dv, *scattered = _attn_bwd(qkv, o, do, aw, t, scatter_rest and scatter_rest([big[n] for n in REST]))
    dx, dproj, h1, dg_pre_mix, dwconv = _in_proj_bwd(x, dx1, dq, dk, dv, dcb, dcv, conv, dgate, wconv,
                                                    small["g_pre_mix"], full["w_in"], tm)
    big["w_in"] = _weight_grad(h1, dproj, "dw_in")
    small_grads = {"g_pre_mix": dg_pre_mix, "b_gate": db_gate, "g_post_mix": dg_post_mix, "g_pre_mlp": dg_pre_mlp,
                   "g_post_mlp": dg_post_mlp, "g_ple": dg_ple, "w_conv": dwconv}
    return loss[0, 0], dx, big, small_grads, scattered


def _pack_small(vals, d):
    parts = []
    for a in vals:
        a = jnp.pad(a.reshape(-1), (0, -a.size % d)).reshape(-1, d)
        parts.append(jnp.pad(a, ((0, HALO - a.shape[0]), (0, 0))))
    return jnp.concatenate(parts, axis=0)


def _unpack_small(pack, shapes, d):
    out = []
    for i, shp in enumerate(shapes):
        n = 1
        for v in shp:
            n *= v
        rows = -(-n // d)
        out.append(pack[i * HALO:i * HALO + rows].reshape(-1)[:n].reshape(shp))
    return out


def kernel(x, p, g_pre_mix, w_in, b_gate, w_conv, w_attn_out, w_conv_out, w_o, g_post_mix, g_pre_mlp, w_up, w_down, g_post_mlp, g_ple, w_ple_gate, w_ple_proj, loss_target, m_g_pre_mix, m_w_in, m_b_gate, m_w_conv, m_w_attn_out, m_w_conv_out, m_w_o, m_g_post_mix, m_g_pre_mlp, m_w_up, m_w_down, m_g_post_mlp, m_g_ple, m_w_ple_gate, m_w_ple_proj, v_g_pre_mix, v_w_in, v_b_gate, v_w_conv, v_w_attn_out, v_w_conv_out, v_w_o, v_g_post_mix, v_g_pre_mlp, v_w_up, v_w_down, v_g_post_mlp, v_g_ple, v_w_ple_gate, v_w_ple_proj):
    given = dict(locals())
    order = ["g_pre_mix", "w_in", "b_gate", "w_conv", "w_attn_out", "w_conv_out", "w_o", "g_post_mix", "g_pre_mlp",
             "w_up", "w_down", "g_post_mlp", "g_ple", "w_ple_gate", "w_ple_proj"]
    d = x.shape[-1]
    me = 4 * lax.axis_index("x") + 2 * lax.axis_index("y") + lax.axis_index("c")

    col = [COL_SHARDED[n] for n in BIG]
    shards = _to_bf16([given[n][0] for n in BIG])
    cw_shard = w_conv.shape[-1]
    conv_tile = jnp.pad(w_conv[0], ((0, HALO - CONV_K), (0, LANES - cw_shard)))
    w_in_full, conv_g = _exchange_call(
        _join(_gather_exchange(shards[:1], col[:1]), _broadcast_exchange([conv_tile])), "gather_w_in")
    wconv = jnp.concatenate([conv_g[dev, :CONV_K, :cw_shard] for dev in range(N_DEV)], axis=1)

    small = {n: given[n] for n in SMALL}
    loss, dx, big_grads, small_grads, parts_rest = _local_grads(
        x[0], p[0, 0], loss_target[0], small, wconv, {"w_in": w_in_full}, w_attn_out.shape[1], w_conv_out.shape[1],
        ROW_BLOCK, ATTN_BLOCK,
        _gather_exchange(shards[1:], col[1:]), lambda grads: _scatter_exchange(grads, col[1:]))
    loss = lax.psum(loss, MESH_AXES)

    small_names = list(SMALL) + ["w_conv"]
    pack = _pack_small([small_grads[n] for n in small_names], d)
    part_in, packs = _exchange_call(
        _join(_scatter_exchange([big_grads["w_in"]], col[:1]), _broadcast_exchange([pack])), "scatter_dw_in")
    parts = [part_in] + list(parts_rest)

    grads, deltas, new_m, new_v = {}, {}, {}, {}
    for n, part in zip(BIG, parts):
        grads[n], deltas[n], new_m[n], new_v[n] = (
            a[None] for a in _sum_and_adamw(part, given[n][0], given["m_" + n][0], given["v_" + n][0], "adamw_" + n))

    full_conv = lambda a: lax.dynamic_update_slice(jnp.zeros((CONV_K, N_DEV * cw_shard), F32), a[0],
                                                   (jnp.int32(0), me * cw_shard))
    state = [_pack_small([given[pre + n] for n in SMALL] + [full_conv(given[pre + "w_conv"])], d)
             for pre in ("", "m_", "v_")]
    outs = _sum_and_adamw(packs, *state, "adamw_small")
    shapes = [given[n].shape for n in SMALL] + [(CONV_K, N_DEV * cw_shard)]
    for res, dst in zip(outs, (grads, deltas, new_m, new_v)):
        for n, a in zip(small_names, _unpack_small(res, shapes, d)):
            dst[n] = (lax.dynamic_slice(a, (jnp.int32(0), me * cw_shard), (CONV_K, cw_shard))[None]
                      if n == "w_conv" else a)

    return (loss, dx[None], *[grads[n] for n in order], *[deltas[n] for n in order],
            *[new_m[n] for n in order], *[new_v[n] for n in order])
```

```python
import jax
import jax.numpy as jnp
from jax import lax
from jax.experimental import pallas as pl
from jax.experimental.pallas import tpu as pltpu

F32 = jnp.float32
BF16 = jnp.bfloat16
RMS_EPS = 1e-6
N_DEV = 8
MESH_AXES = ("x", "y", "c")
LANES = 128
HEAD_DIM = 64
HEADS_PER_GROUP = LANES // HEAD_DIM
CONV_K = 3
HALO = 8
VMEM_LIMIT = 56 * 1024 * 1024
EXP_ZERO = -104.0

ADAM_LR = 0.001
ADAM_B1 = 0.9
ADAM_B2 = 0.999
ADAM_EPS = 1e-08
ADAM_WD = 0.01
ADAM_STEP = 10

ROW_BLOCK = 256
ATTN_BLOCK = 256
DW_TOKENS = 512
DW_TILE = 1024
FF_CHUNK = 1024
PROJ_CHUNK = 512


def _dot(a, b):
    return lax.dot_general(a, b, (((1,), (0,)), ((), ())), preferred_element_type=F32)


def _dot_nt(a, b):
    return lax.dot_general(a, b, (((1,), (1,)), ((), ())), preferred_element_type=F32)


def _dot_tn(a, b):
    return lax.dot_general(a, b, (((0,), (0,)), ((), ())), preferred_element_type=F32)


def _sigmoid(z):
    return 1.0 / (1.0 + jnp.exp(-z))


def _rms_scale(x):
    return lax.rsqrt(jnp.mean(x * x, axis=-1, keepdims=True) + RMS_EPS)


def _rms_bwd(xhat, r, g, dy):
    gd = dy * g
    return r * (gd - xhat * jnp.mean(gd * xhat, axis=-1, keepdims=True))


def _params(n_axes, **kw):
    return pltpu.CompilerParams(dimension_semantics=("arbitrary",) * n_axes, vmem_limit_bytes=VMEM_LIMIT, **kw)


def _load_resident(pairs, sem):
    @pl.when(pl.program_id(0) == 0)
    def _():
        copies = [pltpu.make_async_copy(src, dst, sem.at[i]) for i, (src, dst) in enumerate(pairs)]
        for cp in copies:
            cp.start()
        for cp in copies:
            cp.wait()


def _row_spec(tm, width):
    return pl.BlockSpec((tm, width), lambda i: (i, 0))


def _col_spec(tm, width, col):
    return pl.BlockSpec((tm, width), lambda i: (i, col))


def _prev_halo_spec(tm, width, col=0):
    per = tm // HALO
    return pl.BlockSpec((HALO, width), lambda i: (jnp.maximum(i * per - 1, 0), col))


def _next_halo_spec(tm, width, n_rows):
    per = tm // HALO
    last = n_rows // HALO - 1
    return pl.BlockSpec((HALO, width), lambda i: (jnp.minimum((i + 1) * per, last), 0))


def _const_spec(shape):
    return pl.BlockSpec(shape, lambda i: (0,) * len(shape))


ANY = pl.BlockSpec(memory_space=pl.ANY)


def _shift_down(cur, prev, n):
    rows = lax.broadcasted_iota(jnp.int32, cur.shape, 0)
    out = pltpu.roll(cur, n, 0)
    for j in range(n):
        out = jnp.where(rows == j, prev[HALO - n + j:HALO - n + j + 1, :], out)
    return out


def _shift_up(cur, nxt, n):
    tm = cur.shape[0]
    rows = lax.broadcasted_iota(jnp.int32, cur.shape, 0)
    out = pltpu.roll(cur, tm - n, 0)
    for j in range(n):
        out = jnp.where(rows == tm - n + j, nxt[j:j + 1, :], out)
    return out


def _conv_taps(cm, cm_prev, wconv):
    cm1 = _shift_down(cm, cm_prev, 1)
    cm2 = _shift_down(cm, cm_prev, 2)
    cv = wconv[2:3, :] * cm + wconv[1:2, :] * cm1 + wconv[0:1, :] * cm2
    return cv, cm1, cm2


def _in_proj_fwd(x, g1, b_gate, w_in, aw, cw, tm):
    s, d = x.shape
    ni = w_in.shape[1]
    n_qkv, n_conv = 3 * aw, 3 * cw
    ch = PROJ_CHUNK

    def body(x_ref, g_ref, b_ref, w_hbm, qkv_ref, conv_ref, gate_ref, w_vmem, sem):
        _load_resident([(w_hbm, w_vmem)], sem)
        xv = x_ref[...]
        h = (xv * _rms_scale(xv) * g_ref[...]).astype(BF16)
        for c0 in range(0, ni, ch):
            pc = _dot(h, w_vmem[:, c0:c0 + ch])
            if c0 < n_qkv:
                qkv_ref[:, c0:c0 + ch] = pc.astype(BF16)
            elif c0 < n_qkv + n_conv:
                conv_ref[:, c0 - n_qkv:c0 - n_qkv + ch] = pc
            else:
                g0 = c0 - n_qkv - n_conv
                gate_ref[:, g0:g0 + ch] = _sigmoid(pc + b_ref[:, g0:g0 + ch])

    return pl.pallas_call(
        body, name="in_proj_fwd", grid=(s // tm,),
        in_specs=[_row_spec(tm, d), _const_spec((1, d)), _const_spec((1, 2 * d)), ANY],
        out_specs=[_row_spec(tm, n_qkv), _row_spec(tm, n_conv), _row_spec(tm, 2 * d)],
        out_shape=[jax.ShapeDtypeStruct((s, n_qkv), BF16), jax.ShapeDtypeStruct((s, n_conv), F32),
                   jax.ShapeDtypeStruct((s, 2 * d), F32)],
        scratch_shapes=[pltpu.VMEM((d, ni), BF16), pltpu.SemaphoreType.DMA((1,))],
        compiler_params=_params(1),
    )(x, g1, b_gate, w_in)


def _split_hi_lo(a):
    hi = a.astype(BF16)
    return hi, (a - hi.astype(F32)).astype(BF16)


def _log_gates(z):
    t = jnp.exp(-jnp.abs(z))
    u = 1.0 + t
    sp = jnp.log(u)
    return jnp.minimum(z, 0.0) - sp, jnp.minimum(-z, 0.0) - sp, t, u


def _attn_masks(t):
    row = lax.broadcasted_iota(jnp.int32, (t, t), 0)
    col = lax.broadcasted_iota(jnp.int32, (t, t), 1)
    causal = jnp.concatenate([col < row] * HEADS_PER_GROUP, axis=0)
    return causal, (row > col).astype(BF16), (row >= col).astype(BF16)


def _stack_heads(a):
    return jnp.concatenate([jnp.where(_head_lanes(h), a, jnp.zeros_like(a)) for h in range(HEADS_PER_GROUP)], axis=0)


def _unstack_heads(a, t):
    out = a[0:t]
    for h in range(1, HEADS_PER_GROUP):
        out = jnp.where(_head_lanes(h), a[h * t:(h + 1) * t], out)
    return out


def _while_weights_live(qi, block, carry):
    def cond(state):
        j, carry = state
        return jnp.logical_and(j < qi, jnp.max(carry[0]) >= EXP_ZERO)

    def step(state):
        j, carry = state
        return j + 1, block(qi - 1 - j, carry)

    return lax.while_loop(cond, step, (jnp.int32(0), carry))[1]


def _head_lanes(h):
    lane = lax.broadcasted_iota(jnp.int32, (1, LANES), 1)
    return (lane >= HEAD_DIM * h) & (lane < HEAD_DIM * (h + 1))


def _attn_fwd(qkv, aw, t, exchange=None):
    s = qkv.shape[0]
    groups = aw // LANES
    nq = s // t
    scale = HEAD_DIM ** -0.5
    ex = exchange or _NO_EXCHANGE

    def body(q_ref, k_ref, v_ref, *rest):
        ex_in, (o_ref,), ex_out, sems = _split_refs(rest, ex, 1)
        qi = pl.program_id(1)
        _exchange_start(ex, ex_in, ex_out, sems)
        causal, upper, _ = _attn_masks(t)
        qs = _stack_heads(q_ref[...] * scale)

        def block(kb, run, acc, diag):
            rows = pl.ds(pl.multiple_of(kb * t, t), t)
            z = _dot_nt(qs, k_ref[rows, :])
            log_b, log_keep, _, _ = _log_gates(z)
            if diag:
                log_keep = jnp.where(causal, log_keep, 0.0)
            hi, lo = _split_hi_lo(log_keep)
            between = _dot(hi, upper) + _dot(lo, upper) + run
            w = jnp.exp(log_b + between)
            if diag:
                w = jnp.where(causal, w, 0.0)
            acc = acc + _dot(w.astype(BF16), v_ref[rows, :])
            return run + jnp.sum(log_keep, axis=1, keepdims=True), acc

        rows_all = HEADS_PER_GROUP * t
        carry = block(qi, jnp.zeros((rows_all, 1), F32), jnp.zeros((rows_all, LANES), F32), True)
        _, acc = _while_weights_live(qi, lambda kb, carry: block(kb, *carry, False), carry)
        o_ref[...] = _unstack_heads(acc, t)
        _exchange_wait(ex, ex_in, ex_out, sems, groups, nq)

    return pl.pallas_call(
        body, name="attn_fwd", grid=(groups, nq),
        in_specs=[pl.BlockSpec((t, LANES), lambda g, i: (i, g)),
                  pl.BlockSpec((s, LANES), lambda g, i: (0, groups + g)),
                  pl.BlockSpec((s, LANES), lambda g, i: (0, 2 * groups + g))] + [ANY] * len(ex.arrays),
        out_specs=[pl.BlockSpec((t, LANES), lambda g, i: (i, g))] + [ANY] * len(ex.out_shapes),
        out_shape=[jax.ShapeDtypeStruct((s, aw), F32)] + ex.out_shapes,
        scratch_shapes=_exchange_sems(ex),
        compiler_params=_params(2),
    )(qkv, qkv, qkv, *ex.arrays)


def _attn_bwd(qkv, o, do, aw, t, exchange=None):
    s = qkv.shape[0]
    groups = aw // LANES
    nq = s // t
    scale = HEAD_DIM ** -0.5
    ex = exchange or _NO_EXCHANGE

    def body(q_ref, k_ref, v_ref, o_ref, do_ref, *rest):
        ex_in, (dq_ref, dk_ref, dv_ref), ex_out, (dk_acc, dv_acc, *sems) = _split_refs(rest, ex, 3)
        qi = pl.program_id(1)
        _exchange_start(ex, ex_in, ex_out, sems)

        @pl.when(qi == 0)
        def _():
            dk_acc[...] = jnp.zeros_like(dk_acc)
            dv_acc[...] = jnp.zeros_like(dv_acc)

        causal, upper, lower_incl = _attn_masks(t)
        qs = _stack_heads(q_ref[...] * scale)
        do_b = do_ref[...]
        dos = _stack_heads(do_b)
        e_total = jnp.sum(_stack_heads(do_b.astype(F32) * o_ref[...]), axis=1, keepdims=True)

        def block(kb, run, e_run, dq, diag):
            rows = pl.ds(pl.multiple_of(kb * t, t), t)
            k = k_ref[rows, :]
            v = v_ref[rows, :]
            z = _dot_nt(qs, k)
            log_b, log_keep, tt, u = _log_gates(z)
            r = 1.0 / u
            beta = jnp.where(z >= 0.0, r, tt * r)
            keep = jnp.where(z >= 0.0, tt * r, r)
            if diag:
                log_keep = jnp.where(causal, log_keep, 0.0)
            hi, lo = _split_hi_lo(log_keep)
            between = _dot(hi, upper) + _dot(lo, upper) + run
            w = jnp.exp(log_b + between)
            if diag:
                w = jnp.where(causal, w, 0.0)
            wb = w.astype(BF16)
            e = _dot_nt(dos, v) * wb.astype(F32)
            hi, lo = _split_hi_lo(e)
            e_suffix = _dot(hi, lower_incl) + _dot(lo, lower_incl) + e_run
            dz = e * keep - (e_total - e_suffix) * beta
            if diag:
                dz = jnp.where(causal, dz, 0.0)
            dzb = dz.astype(BF16)
            dk_acc[rows, :] += _dot_tn(dzb, qs)
            dv_acc[rows, :] += _dot_tn(wb, dos)
            return (run + jnp.sum(log_keep, axis=1, keepdims=True), e_suffix[:, 0:1], dq + _dot(dzb, k))

        rows_all = HEADS_PER_GROUP * t
        zero_col = jnp.zeros((rows_all, 1), F32)
        carry = block(qi, zero_col, zero_col, jnp.zeros((rows_all, LANES), F32), True)
        _, _, dq = _while_weights_live(qi, lambda kb, carry: block(kb, *carry, False), carry)
        dq_ref[...] = (_unstack_heads(dq, t) * scale).astype(BF16)

        @pl.when(qi == nq - 1)
        def _():
            dk_ref[...] = dk_acc[...].astype(BF16)
            dv_ref[...] = dv_acc[...].astype(BF16)

        _exchange_wait(ex, ex_in, ex_out, sems, groups, nq)

    blk = pl.BlockSpec((t, LANES), lambda g, i: (i, g))
    slab = pl.BlockSpec((s, LANES), lambda g, i: (0, g))
    return pl.pallas_call(
        body, name="attn_bwd", grid=(groups, nq),
        in_specs=[blk, pl.BlockSpec((s, LANES), lambda g, i: (0, groups + g)),
                  pl.BlockSpec((s, LANES), lambda g, i: (0, 2 * groups + g)), blk, blk] + [ANY] * len(ex.arrays),
        out_specs=[blk, slab, slab] + [ANY] * len(ex.out_shapes),
        out_shape=[jax.ShapeDtypeStruct((s, aw), BF16)] * 3 + ex.out_shapes,
        scratch_shapes=[pltpu.VMEM((s, LANES), F32), pltpu.VMEM((s, LANES), F32)] + _exchange_sems(ex),
        compiler_params=_params(2),
    )(qkv, qkv, qkv, o, do, *ex.arrays)


def _branches(o_b, conv, conv_prev, wconv, w_ao, w_co, cw, first):
    cb = conv[:, 0:cw]
    cm = conv[:, cw:2 * cw] * conv[:, 2 * cw:3 * cw]
    cm_prev = conv_prev[:, cw:2 * cw] * conv_prev[:, 2 * cw:3 * cw]
    cm_prev = jnp.where(first, 0.0, cm_prev)
    cv, cm1, cm2 = _conv_taps(cm, cm_prev, wconv)
    conv_in = (cb * cv).astype(BF16)
    return _dot(o_b, w_ao), _dot(conv_in, w_co), conv_in, cb, cv, cm, cm1, cm2


def _mix_fwd(x, o, conv, gate, wconv, g_post, w_ao, w_co, w_o, tm):
    s, d = x.shape
    aw, cw = w_ao.shape[0], w_co.shape[0]

    def body(x_ref, o_ref, conv_ref, prev_ref, gate_ref, wc_ref, g_ref, wao_hbm, wco_hbm, wo_hbm,
             x1_ref, mixed_ref, mixin_ref, convin_ref, wao, wco, wo, sem):
        _load_resident([(wao_hbm, wao), (wco_hbm, wco), (wo_hbm, wo)], sem)
        y_attn, y_conv, conv_in, *_ = _branches(
            o_ref[...].astype(BF16), conv_ref[...], prev_ref[...], wc_ref[...], wao[...], wco[...], cw,
            pl.program_id(0) == 0)
        mix_in = (gate_ref[:, 0:d] * y_attn + gate_ref[:, d:2 * d] * y_conv).astype(BF16)
        mixed = _dot(mix_in, wo[...])
        x1_ref[...] = x_ref[...] + mixed * _rms_scale(mixed) * g_ref[...]
        mixed_ref[...] = mixed
        mixin_ref[...] = mix_in
        convin_ref[...] = conv_in

    return pl.pallas_call(
        body, name="mix_fwd", grid=(s // tm,),
        in_specs=[_row_spec(tm, d), _row_spec(tm, aw), _row_spec(tm, 3 * cw), _prev_halo_spec(tm, 3 * cw),
                  _row_spec(tm, 2 * d), _const_spec((CONV_K, cw)), _const_spec((1, d)), ANY, ANY, ANY],
        out_specs=[_row_spec(tm, d), _row_spec(tm, d), _row_spec(tm, d), _row_spec(tm, cw)],
        out_shape=[jax.ShapeDtypeStruct((s, d), F32), jax.ShapeDtypeStruct((s, d), F32),
                   jax.ShapeDtypeStruct((s, d), BF16), jax.ShapeDtypeStruct((s, cw), BF16)],
        scratch_shapes=[pltpu.VMEM(w_ao.shape, BF16), pltpu.VMEM(w_co.shape, BF16), pltpu.VMEM(w_o.shape, BF16),
                        pltpu.SemaphoreType.DMA((3,))],
        compiler_params=_params(1),
    )(x, o, conv, conv, gate, wconv, g_post, w_ao, w_co, w_o)


def _mix_bwd(dx1, mixed, o, conv, gate, wconv, g_post, w_ao, w_co, w_o, tm):
    s, d = dx1.shape
    aw, cw = w_ao.shape[0], w_co.shape[0]

    def body(dx1_ref, mixed_ref, o_ref, conv_ref, prev_ref, gate_ref, wc_ref, g_ref, wao_hbm, wco_hbm, wo_hbm,
             dmixed_ref, dattn_ref, dconvout_ref, do_ref, dgate_ref, dcb_ref, dcv_ref, dg_ref, dbias_ref,
             wao, wco, wo, sem):
        i = pl.program_id(0)
        _load_resident([(wao_hbm, wao), (wco_hbm, wco), (wo_hbm, wo)], sem)

        @pl.when(i == 0)
        def _():
            dg_ref[...] = jnp.zeros_like(dg_ref)
            dbias_ref[...] = jnp.zeros_like(dbias_ref)

        mixed = mixed_ref[...]
        r = _rms_scale(mixed)
        mhat = mixed * r
        dn = dx1_ref[...]
        dg_ref[...] += jnp.sum(dn * mhat, axis=0, keepdims=True)
        dmixed = _rms_bwd(mhat, r, g_ref[...], dn).astype(BF16)
        dmixed_ref[...] = dmixed
        dmi = _dot_nt(dmixed, wo[...])

        y_attn, y_conv, _, cb, cv, *_ = _branches(
            o_ref[...].astype(BF16), conv_ref[...], prev_ref[...], wc_ref[...], wao[...], wco[...], cw, i == 0)
        ga = gate_ref[:, 0:d]
        gc = gate_ref[:, d:2 * d]
        dpre_a = dmi * y_attn * ga * (1.0 - ga)
        dpre_c = dmi * y_conv * gc * (1.0 - gc)
        dgate_ref[:, 0:d] = dpre_a.astype(BF16)
        dgate_ref[:, d:2 * d] = dpre_c.astype(BF16)
        dbias_ref[:, 0:d] += jnp.sum(dpre_a, axis=0, keepdims=True)
        dbias_ref[:, d:2 * d] += jnp.sum(dpre_c, axis=0, keepdims=True)

        dattn = (dmi * ga).astype(BF16)
        dattn_ref[...] = dattn
        do_ref[...] = _dot_nt(dattn, wao[...]).astype(BF16)
        dconvout = (dmi * gc).astype(BF16)
        dconvout_ref[...] = dconvout
        dconv_in = _dot_nt(dconvout, wco[...])
        dcb_ref[...] = (dconv_in * cv).astype(BF16)
        dcv_ref[...] = dconv_in * cb

    return pl.pallas_call(
        body, name="mix_bwd", grid=(s // tm,),
        in_specs=[_row_spec(tm, d), _row_spec(tm, d), _row_spec(tm, aw), _row_spec(tm, 3 * cw),
                  _prev_halo_spec(tm, 3 * cw), _row_spec(tm, 2 * d), _const_spec((CONV_K, cw)), _const_spec((1, d)),
                  ANY, ANY, ANY],
        out_specs=[_row_spec(tm, d), _row_spec(tm, d), _row_spec(tm, d), _row_spec(tm, aw), _row_spec(tm, 2 * d),
                   _row_spec(tm, cw), _row_spec(tm, cw), _const_spec((1, d)), _const_spec((1, 2 * d))],
        out_shape=[jax.ShapeDtypeStruct((s, d), BF16), jax.ShapeDtypeStruct((s, d), BF16),
                   jax.ShapeDtypeStruct((s, d), BF16), jax.ShapeDtypeStruct((s, aw), BF16),
                   jax.ShapeDtypeStruct((s, 2 * d), BF16), jax.ShapeDtypeStruct((s, cw), BF16),
                   jax.ShapeDtypeStruct((s, cw), F32), jax.ShapeDtypeStruct((1, d), F32),
                   jax.ShapeDtypeStruct((1, 2 * d), F32)],
        scratch_shapes=[pltpu.VMEM(w_ao.shape, BF16), pltpu.VMEM(w_co.shape, BF16), pltpu.VMEM(w_o.shape, BF16),
                        pltpu.SemaphoreType.DMA((3,))],
        compiler_params=_params(1),
    )(dx1, mixed, o, conv, conv, gate, wconv, g_post, w_ao, w_co, w_o)


def _mlp_ple_loss(x1, p, target, g_pre, g_post, g_ple, w_up, w_dn, w_pg, w_pp, tm):
    s, d = x1.shape
    ff = w_up.shape[1]
    pd = p.shape[1]
    fc = FF_CHUNK

    def body(x1_ref, p_ref, t_ref, gpre_ref, gpost_ref, gple_ref, wup_hbm, wdn_hbm, wpg_hbm, wpp_hbm,
             dx1_ref, h2_ref, du_ref, a_ref, df_ref, h3_ref, ds3_ref, dpp_ref, loss_ref, dgpre_ref, dgpost_ref,
             dgple_ref, wup, wdn, wpg, wpp, u_scr, sem):
        _load_resident([(wup_hbm, wup), (wdn_hbm, wdn), (wpg_hbm, wpg), (wpp_hbm, wpp)], sem)

        @pl.when(pl.program_id(0) == 0)
        def _():
            for ref in (loss_ref, dgpre_ref, dgpost_ref, dgple_ref):
                ref[...] = jnp.zeros_like(ref)

        x1v = x1_ref[...]
        r2 = _rms_scale(x1v)
        x1hat = x1v * r2
        h2 = (x1hat * gpre_ref[...]).astype(BF16)
        h2_ref[...] = h2
        f = jnp.zeros((tm, d), F32)
        for c0 in range(0, ff, fc):
            u = _dot(h2, wup[:, c0:c0 + fc])
            u_scr[:, c0:c0 + fc] = u
            a = jnp.square(jnp.maximum(u, 0.0)).astype(BF16)
            a_ref[:, c0:c0 + fc] = a
            f = f + _dot(a, wdn[c0:c0 + fc, :])
        rf = _rms_scale(f)
        fhat = f * rf
        x2 = x1v + fhat * gpost_ref[...]
        r3 = _rms_scale(x2)
        x2hat = x2 * r3
        h3 = (x2hat * gple_ref[...]).astype(BF16)
        h3_ref[...] = h3
        pg = _sigmoid(_dot(h3, wpg[...]))
        pp = _dot(p_ref[...].astype(BF16), wpp[...])
        diff = x2 + pg * pp - t_ref[...]
        loss_ref[...] += 0.5 * jnp.sum(jnp.mean(diff * diff, axis=-1, keepdims=True), axis=0, keepdims=True)

        dy = diff * (1.0 / d)
        dpp_ref[...] = (dy * pg).astype(BF16)
        ds3 = (dy * pp * pg * (1.0 - pg)).astype(BF16)
        ds3_ref[...] = ds3
        dh3 = _dot_nt(ds3, wpg[...])
        dgple_ref[...] += jnp.sum(dh3 * x2hat, axis=0, keepdims=True)
        dx2 = dy + _rms_bwd(x2hat, r3, gple_ref[...], dh3)
        dgpost_ref[...] += jnp.sum(dx2 * fhat, axis=0, keepdims=True)
        df = _rms_bwd(fhat, rf, gpost_ref[...], dx2).astype(BF16)
        df_ref[...] = df
        dh2 = jnp.zeros((tm, d), F32)
        for c0 in range(0, ff, fc):
            da = _dot_nt(df, wdn[c0:c0 + fc, :])
            du = (da * (2.0 * jnp.maximum(u_scr[:, c0:c0 + fc], 0.0))).astype(BF16)
            du_ref[:, c0:c0 + fc] = du
            dh2 = dh2 + _dot_nt(du, wup[:, c0:c0 + fc])
        dgpre_ref[...] += jnp.sum(dh2 * x1hat, axis=0, keepdims=True)
        dx1_ref[...] = dx2 + _rms_bwd(x1hat, r2, gpre_ref[...], dh2)

    vec = _const_spec((1, d))
    return pl.pallas_call(
        body, name="mlp_ple_loss", grid=(s // tm,),
        in_specs=[_row_spec(tm, d), _row_spec(tm, pd), _row_spec(tm, d), vec, vec, vec, ANY, ANY, ANY, ANY],
        out_specs=[_row_spec(tm, d), _row_spec(tm, d), _row_spec(tm, ff), _row_spec(tm, ff), _row_spec(tm, d),
                   _row_spec(tm, d), _row_spec(tm, d), _row_spec(tm, d), _const_spec((1, 1)), vec, vec, vec],
        out_shape=[jax.ShapeDtypeStruct((s, d), F32), jax.ShapeDtypeStruct((s, d), BF16),
                   jax.ShapeDtypeStruct((s, ff), BF16), jax.ShapeDtypeStruct((s, ff), BF16),
                   jax.ShapeDtypeStruct((s, d), BF16), jax.ShapeDtypeStruct((s, d), BF16),
                   jax.ShapeDtypeStruct((s, d), BF16), jax.ShapeDtypeStruct((s, d), BF16),
                   jax.ShapeDtypeStruct((1, 1), F32), jax.ShapeDtypeStruct((1, d), F32),
                   jax.ShapeDtypeStruct((1, d), F32), jax.ShapeDtypeStruct((1, d), F32)],
        scratch_shapes=[pltpu.VMEM(w_up.shape, BF16), pltpu.VMEM(w_dn.shape, BF16), pltpu.VMEM(w_pg.shape, BF16),
                        pltpu.VMEM(w_pp.shape, BF16), pltpu.VMEM((tm, ff), F32), pltpu.SemaphoreType.DMA((4,))],
        compiler_params=_params(1),
    )(x1, p, target, g_pre, g_post, g_ple, w_up, w_dn, w_pg, w_pp)


def _in_proj_bwd(x, dx1, dq, dk, dv, dcb, dcv, conv, dgate, wconv, g1, w_in, tm):
    s, d = x.shape
    aw, cw = dq.shape[1], dcb.shape[1]
    ni = w_in.shape[1]

    def body(x_ref, dx1_ref, dq_ref, dk_ref, dv_ref, dcb_ref, dcv_ref, dcvn_ref, cc_ref, cu_ref, ccp_ref, cup_ref,
             dgate_ref, wc_ref, g_ref, w_hbm, dx_ref, dproj_ref, h1_ref, dg_ref, dwc_ref, w_vmem, sem):
        i = pl.program_id(0)
        _load_resident([(w_hbm, w_vmem)], sem)

        @pl.when(i == 0)
        def _():
            dg_ref[...] = jnp.zeros_like(dg_ref)
            dwc_ref[...] = jnp.zeros_like(dwc_ref)

        wc = wc_ref[...]
        cc = cc_ref[...]
        cu = cu_ref[...]
        cm = cc * cu
        cm_prev = jnp.where(i == 0, 0.0, ccp_ref[...] * cup_ref[...])
        _, cm1, cm2 = _conv_taps(cm, cm_prev, wc)
        dcv_cur = dcv_ref[...]
        dcv_next = jnp.where(i == pl.num_programs(0) - 1, 0.0, dcvn_ref[...])
        dcm = (wc[2:3, :] * dcv_cur + wc[1:2, :] * _shift_up(dcv_cur, dcv_next, 1)
               + wc[0:1, :] * _shift_up(dcv_cur, dcv_next, 2))
        for tap, shifted in enumerate((cm2, cm1, cm)):
            dwc_ref[tap:tap + 1, :] += jnp.sum(dcv_cur * shifted, axis=0, keepdims=True)

        pieces = [(dq_ref[...], aw), (dk_ref[...], aw), (dv_ref[...], aw), (dcb_ref[...], cw),
                  ((dcm * cu).astype(BF16), cw), ((dcm * cc).astype(BF16), cw), (dgate_ref[...], 2 * d)]
        dh = jnp.zeros((tm, d), F32)
        c0 = 0
        for piece, width in pieces:
            dproj_ref[:, c0:c0 + width] = piece
            dh = dh + _dot_nt(piece, w_vmem[:, c0:c0 + width])
            c0 += width

        xv = x_ref[...]
        r = _rms_scale(xv)
        xhat = xv * r
        h1_ref[...] = (xhat * g_ref[...]).astype(BF16)
        dg_ref[...] += jnp.sum(dh * xhat, axis=0, keepdims=True)
        dx_ref[...] = dx1_ref[...] + _rms_bwd(xhat, r, g_ref[...], dh)

    return pl.pallas_call(
        body, name="in_proj_bwd", grid=(s // tm,),
        in_specs=[_row_spec(tm, d), _row_spec(tm, d), _row_spec(tm, aw), _row_spec(tm, aw), _row_spec(tm, aw),
                  _row_spec(tm, cw), _row_spec(tm, cw), _next_halo_spec(tm, cw, s), _col_spec(tm, cw, 1),
                  _col_spec(tm, cw, 2), _prev_halo_spec(tm, cw, 1), _prev_halo_spec(tm, cw, 2), _row_spec(tm, 2 * d),
                  _const_spec((CONV_K, cw)), _const_spec((1, d)), ANY],
        out_specs=[_row_spec(tm, d), _row_spec(tm, ni), _row_spec(tm, d), _const_spec((1, d)),
                   _const_spec((CONV_K, cw))],
        out_shape=[jax.ShapeDtypeStruct((s, d), F32), jax.ShapeDtypeStruct((s, ni), BF16),
                   jax.ShapeDtypeStruct((s, d), BF16), jax.ShapeDtypeStruct((1, d), F32),
                   jax.ShapeDtypeStruct((CONV_K, cw), F32)],
        scratch_shapes=[pltpu.VMEM((d, ni), BF16), pltpu.SemaphoreType.DMA((1,))],
        compiler_params=_params(1),
    )(x, dx1, dq, dk, dv, dcb, dcv, dcv, conv, conv, conv, conv, dgate, wconv, g1, w_in)


def _weight_grad(a, b, name):
    s, m = a.shape
    n = b.shape[1]
    tm, tn, tk = min(m, DW_TILE), min(n, DW_TILE), min(s, DW_TOKENS)
    nk = s // tk

    def body(a_ref, b_ref, o_ref, acc):
        k = pl.program_id(2)

        @pl.when(k == 0)
        def _():
            acc[...] = jnp.zeros_like(acc)

        acc[...] += _dot_tn(a_ref[...].astype(BF16), b_ref[...].astype(BF16))

        @pl.when(k == nk - 1)
        def _():
            o_ref[...] = acc[...].astype(BF16)

    return pl.pallas_call(
        body, name=name, grid=(m // tm, n // tn, nk),
        in_specs=[pl.BlockSpec((tk, tm), lambda i, j, k: (k, i)), pl.BlockSpec((tk, tn), lambda i, j, k: (k, j))],
        out_specs=pl.BlockSpec((tm, tn), lambda i, j, k: (i, j)),
        out_shape=jax.ShapeDtypeStruct((m, n), BF16),
        scratch_shapes=[pltpu.VMEM((tm, tn), F32)],
        compiler_params=_params(3),
    )(a, b)


def _mesh_position():
    return tuple(lax.axis_index(a) for a in MESH_AXES)


def _peer(me, k):
    bits = ((k >> 2) & 1, (k >> 1) & 1, k & 1)
    pos = tuple(1 - m if b else m for m, b in zip(me, bits))
    return pos, 4 * pos[0] + 2 * pos[1] + pos[2]


class _Exchange:
    def __init__(self, arrays, out_shapes, src, dst):
        self.arrays, self.out_shapes, self.src, self.dst = list(arrays), list(out_shapes), src, dst


_NO_EXCHANGE = _Exchange([], [], None, None)


def _exchange_sems(ex):
    n = len(ex.arrays)
    if n == 0:
        return []
    return [pltpu.SemaphoreType.DMA((n, N_DEV - 1)), pltpu.SemaphoreType.DMA((n, N_DEV - 1)),
            pltpu.SemaphoreType.DMA((n,))]


def _split_refs(rest, ex, n_own_outs):
    n_in, n_out = len(ex.arrays), len(ex.out_shapes)
    ex_in, rest = rest[:n_in], rest[n_in:]
    own, rest = rest[:n_own_outs], rest[n_own_outs:]
    return ex_in, own, rest[:n_out], rest[n_out:]


def _exchange_copies(ex, in_refs, out_refs, sems):
    send_sems, recv_sems, local_sems = sems
    me = _mesh_position()
    mine = 4 * me[0] + 2 * me[1] + me[2]
    copies = []
    for w in range(len(ex.arrays)):
        landing = ex.dst(w, out_refs, mine)
        copies.append(pltpu.make_async_copy(ex.src(w, in_refs, mine), landing, local_sems.at[w]))
        for k in range(1, N_DEV):
            peer, peer_idx = _peer(me, k)
            copies.append(pltpu.make_async_remote_copy(
                src_ref=ex.src(w, in_refs, peer_idx), dst_ref=landing, send_sem=send_sems.at[w, k - 1],
                recv_sem=recv_sems.at[w, k - 1], device_id=peer, device_id_type=pl.DeviceIdType.MESH))
    return copies


def _exchange_start(ex, in_refs, out_refs, sems):
    if ex.arrays:
        @pl.when(jnp.logical_and(pl.program_id(0) == 0, pl.program_id(1) == 0))
        def _():
            for cp in _exchange_copies(ex, in_refs, out_refs, sems):
                cp.start()


def _exchange_wait(ex, in_refs, out_refs, sems, n0, n1):
    if ex.arrays:
        @pl.when(jnp.logical_and(pl.program_id(0) == n0 - 1, pl.program_id(1) == n1 - 1))
        def _():
            for cp in _exchange_copies(ex, in_refs, out_refs, sems):
                cp.wait()


def _shard_block(ref, shard_shape, by_col, idx):
    r, c = shard_shape
    if by_col:
        return ref.at[:, pl.ds(pl.multiple_of(idx * c, LANES), c)]
    return ref.at[pl.ds(pl.multiple_of(idx * r, 16), r), :]


def _full_shape(shard_shape, by_col):
    r, c = shard_shape
    return (r, N_DEV * c) if by_col else (N_DEV * r, c)


def _gather_exchange(shards, col_sharded):
    shapes = [a.shape for a in shards]
    return _Exchange(
        shards, [jax.ShapeDtypeStruct(_full_shape(sh, bc), a.dtype) for a, sh, bc in zip(shards, shapes, col_sharded)],
        lambda w, refs, idx: refs[w],
        lambda w, refs, mine: _shard_block(refs[w], shapes[w], col_sharded[w], mine))


def _scatter_exchange(grads, col_sharded):
    shapes = []
    for g, by_col in zip(grads, col_sharded):
        r, c = g.shape
        shapes.append((r, c // N_DEV) if by_col else (r // N_DEV, c))
    return _Exchange(
        grads, [jax.ShapeDtypeStruct((N_DEV,) + sh, g.dtype) for g, sh in zip(grads, shapes)],
        lambda w, refs, idx: _shard_block(refs[w], shapes[w], col_sharded[w], idx),
        lambda w, refs, mine: refs[w].at[mine])


def _broadcast_exchange(arrays):
    return _Exchange(arrays, [jax.ShapeDtypeStruct((N_DEV,) + a.shape, a.dtype) for a in arrays],
                     lambda w, refs, idx: refs[w], lambda w, refs, mine: refs[w].at[mine])


def _join(*exs):
    arrays, shapes, owner = [], [], []
    for e in exs:
        for w in range(len(e.arrays)):
            owner.append((e, w, len(arrays), len(shapes)))
        arrays += e.arrays
        shapes += e.out_shapes

    def src(w, refs, idx):
        e, w0, i0, _ = owner[w]
        return e.src(w0, refs[i0:i0 + len(e.arrays)], idx)

    def dst(w, refs, mine):
        e, w0, _, o0 = owner[w]
        return e.dst(w0, refs[o0:o0 + len(e.out_shapes)], mine)

    return _Exchange(arrays, shapes, src, dst)


def _exchange_call(ex, name):
    n_in = len(ex.arrays)

    def body(*refs):
        in_refs, _, out_refs, sems = _split_refs(refs, ex, 0)
        copies = _exchange_copies(ex, in_refs, out_refs, sems)
        for cp in copies:
            cp.start()
        for cp in copies:
            cp.wait()

    return pl.pallas_call(
        body, name=name, in_specs=[ANY] * n_in, out_specs=[ANY] * len(ex.out_shapes), out_shape=ex.out_shapes,
        scratch_shapes=_exchange_sems(ex), compiler_params=pltpu.CompilerParams(vmem_limit_bytes=VMEM_LIMIT),
    )(*ex.arrays)


def _to_bf16(arrays):
    def body(*refs):
        for src, dst in zip(refs[:len(arrays)], refs[len(arrays):]):
            dst[...] = src[...].astype(BF16)

    vmem = pl.BlockSpec(memory_space=pltpu.VMEM)
    return pl.pallas_call(
        body, name="weights_to_bf16", in_specs=[vmem] * len(arrays), out_specs=[vmem] * len(arrays),
        out_shape=[jax.ShapeDtypeStruct(a.shape, BF16) for a in arrays],
        compiler_params=pltpu.CompilerParams(vmem_limit_bytes=VMEM_LIMIT),
    )(*arrays)


def _adamw(w, g, m, v):
    m = ADAM_B1 * m + (1.0 - ADAM_B1) * g
    v = ADAM_B2 * v + (1.0 - ADAM_B2) * jnp.square(g)
    m_hat = m / (1.0 - ADAM_B1 ** ADAM_STEP)
    v_hat = v / (1.0 - ADAM_B2 ** ADAM_STEP)
    delta = -ADAM_LR * (m_hat / (jnp.sqrt(v_hat) + ADAM_EPS) + ADAM_WD * w)
    return delta, m, v


def _sum_and_adamw(parts, w, m, v, name):
    r, c = w.shape
    tr = min(r, 256)

    def body(p_ref, w_ref, m_ref, v_ref, g_out, d_out, m_out, v_out):
        g = p_ref[0].astype(F32)
        for dev in range(1, N_DEV):
            g = g + p_ref[dev].astype(F32)
        g_out[...] = g
        d_out[...], m_out[...], v_out[...] = _adamw(w_ref[...], g, m_ref[...], v_ref[...])

    blk = pl.BlockSpec((tr, c), lambda i: (i, 0))
    return pl.pallas_call(
        body, name=name, grid=(r // tr,),
        in_specs=[pl.BlockSpec((N_DEV, tr, c), lambda i: (0, i, 0)), blk, blk, blk],
        out_specs=[blk] * 4, out_shape=[jax.ShapeDtypeStruct((r, c), F32)] * 4,
        compiler_params=_params(1),
    )(parts, w, m, v)


BIG = ("w_in", "w_attn_out", "w_conv_out", "w_o", "w_up", "w_down", "w_ple_gate", "w_ple_proj")
COL_SHARDED = {"w_in": True, "w_attn_out": True, "w_conv_out": True, "w_o": False, "w_up": True, "w_down": False,
               "w_ple_gate": False, "w_ple_proj": True}
SMALL = ("g_pre_mix", "b_gate", "g_post_mix", "g_pre_mlp", "g_post_mlp", "g_ple")


REST = BIG[1:]


def _local_grads(x, p, target, small, wconv, full, aw, cw, tm, t, gather_rest=None, scatter_rest=None):
    full = dict(full)
    qkv, conv, gate = _in_proj_fwd(x, small["g_pre_mix"], small["b_gate"], full["w_in"], aw, cw, tm)
    o, *rest = _attn_fwd(qkv, aw, t, gather_rest)
    full.update(zip(REST, rest))
    x1, mixed, mix_in, conv_in = _mix_fwd(x, o, conv, gate, wconv, small["g_post_mix"], full["w_attn_out"],
                                          full["w_conv_out"], full["w_o"], tm)
    (dx1, h2, du, a, df, h3, ds3, dpp, loss, dg_pre_mlp, dg_post_mlp, dg_ple) = _mlp_ple_loss(
        x1, p, target, small["g_pre_mlp"], small["g_post_mlp"], small["g_ple"], full["w_up"], full["w_down"],
        full["w_ple_gate"], full["w_ple_proj"], tm)
    big = {"w_up": _weight_grad(h2, du, "dw_up"), "w_down": _weight_grad(a, df, "dw_down"),
           "w_ple_gate": _weight_grad(h3, ds3, "dw_ple_gate"), "w_ple_proj": _weight_grad(p, dpp, "dw_ple_proj")}
    (dmixed, dattn, dconvout, do, dgate, dcb, dcv, dg_post_mix, db_gate) = _mix_bwd(
        dx1, mixed, o, conv, gate, wconv, small["g_post_mix"], full["w_attn_out"], full["w_conv_out"], full["w_o"],
        tm)
    big.update({"w_attn_out": _weight_grad(o, dattn, "dw_attn_out"),
                "w_conv_out": _weight_grad(conv_in, dconvout, "dw_conv_out"),
                "w_o": _weight_grad(mix_in, dmixed, "dw_o")})
    dq, dk, dv, *scattered = _attn_bwd(qkv, o, do, aw, t, scatter_rest and scatter_rest([big[n] for n in REST]))
    dx, dproj, h1, dg_pre_mix, dwconv = _in_proj_bwd(x, dx1, dq, dk, dv, dcb, dcv, conv, dgate, wconv,
                                                    small["g_pre_mix"], full["w_in"], tm)
    big["w_in"] = _weight_grad(h1, dproj, "dw_in")
    small_grads = {"g_pre_mix": dg_pre_mix, "b_gate": db_gate, "g_post_mix": dg_post_mix, "g_pre_mlp": dg_pre_mlp,
                   "g_post_mlp": dg_post_mlp, "g_ple": dg_ple, "w_conv": dwconv}
    return loss[0, 0], dx, big, small_grads, scattered


def _pack_small(vals, d):
    parts = []
    for a in vals:
        a = jnp.pad(a.reshape(-1), (0, -a.size % d)).reshape(-1, d)
        parts.append(jnp.pad(a, ((0, HALO - a.shape[0]), (0, 0))))
    return jnp.concatenate(parts, axis=0)


def _unpack_small(pack, shapes, d):
    out = []
    for i, shp in enumerate(shapes):
        n = 1
        for v in shp:
            n *= v
        rows = -(-n // d)
        out.append(pack[i * HALO:i * HALO + rows].reshape(-1)[:n].reshape(shp))
    return out


def kernel(x, p, g_pre_mix, w_in, b_gate, w_conv, w_attn_out, w_conv_out, w_o, g_post_mix, g_pre_mlp, w_up, w_down, g_post_mlp, g_ple, w_ple_gate, w_ple_proj, loss_target, m_g_pre_mix, m_w_in, m_b_gate, m_w_conv, m_w_attn_out, m_w_conv_out, m_w_o, m_g_post_mix, m_g_pre_mlp, m_w_up, m_w_down, m_g_post_mlp, m_g_ple, m_w_ple_gate, m_w_ple_proj, v_g_pre_mix, v_w_in, v_b_gate, v_w_conv, v_w_attn_out, v_w_conv_out, v_w_o, v_g_post_mix, v_g_pre_mlp, v_w_up, v_w_down, v_g_post_mlp, v_g_ple, v_w_ple_gate, v_w_ple_proj):
    given = dict(locals())
    order = ["g_pre_mix", "w_in", "b_gate", "w_conv", "w_attn_out", "w_conv_out", "w_o", "g_post_mix", "g_pre_mlp",
             "w_up", "w_down", "g_post_mlp", "g_ple", "w_ple_gate", "w_ple_proj"]
    d = x.shape[-1]
    me = 4 * lax.axis_index("x") + 2 * lax.axis_index("y") + lax.axis_index("c")

    col = [COL_SHARDED[n] for n in BIG]
    shards = _to_bf16([given[n][0] for n in BIG])
    cw_shard = w_conv.shape[-1]
    conv_tile = jnp.pad(w_conv[0], ((0, HALO - CONV_K), (0, LANES - cw_shard)))
    w_in_full, conv_g = _exchange_call(
        _join(_gather_exchange(shards[:1], col[:1]), _broadcast_exchange([conv_tile])), "gather_w_in")
    wconv = jnp.concatenate([conv_g[dev, :CONV_K, :cw_shard] for dev in range(N_DEV)], axis=1)

    small = {n: given[n] for n in SMALL}
    loss, dx, big_grads, small_grads, parts_rest = _local_grads(
        x[0], p[0, 0], loss_target[0], small, wconv, {"w_in": w_in_full}, w_attn_out.shape[1], w_conv_out.shape[1],
        ROW_BLOCK, ATTN_BLOCK,
        _gather_exchange(shards[1:], col[1:]), lambda grads: _scatter_exchange(grads, col[1:]))
    loss = lax.psum(loss, MESH_AXES)

    small_names = list(SMALL) + ["w_conv"]
    pack = _pack_small([small_grads[n] for n in small_names], d)
    part_in, packs = _exchange_call(
        _join(_scatter_exchange([big_grads["w_in"]], col[:1]), _broadcast_exchange([pack])), "scatter_dw_in")
    parts = [part_in] + list(parts_rest)

    grads, deltas, new_m, new_v = {}, {}, {}, {}
    for n, part in zip(BIG, parts):
        grads[n], deltas[n], new_m[n], new_v[n] = (
            a[None] for a in _sum_and_adamw(part, given[n][0], given["m_" + n][0], given["v_" + n][0], "adamw_" + n))

    full_conv = lambda a: lax.dynamic_update_slice(jnp.zeros((CONV_K, N_DEV * cw_shard), F32), a[0],
                                                   (jnp.int32(0), me * cw_shard))
    state = [_pack_small([given[pre + n] for n in SMALL] + [full_conv(given[pre + "w_conv"])], d)
             for pre in ("", "m_", "v_")]
    outs = _sum_and_adamw(packs, *state, "adamw_small")
    shapes = [given[n].shape for n in SMALL] + [(CONV_K, N_DEV * cw_shard)]
    for res, dst in zip(outs, (grads, deltas, new_m, new_v)):
        for n, a in zip(small_names, _unpack_small(res, shapes, d)):
            dst[n] = (lax.dynamic_slice(a, (jnp.int32(0), me * cw_shard), (CONV_K, cw_shard))[None]
                      if n == "w_conv" else a)

    return (loss, dx[None], *[grads[n] for n in order], *[deltas[n] for n in order],
            *[new_m[n] for n in order], *[new_v[n] for n in order])
```

```python
import jax
import jax.numpy as jnp
from jax import lax
from jax.experimental import pallas as pl
from jax.experimental.pallas import tpu as pltpu

F32 = jnp.float32
BF16 = jnp.bfloat16
RMS_EPS = 1e-6
N_DEV = 8
MESH_AXES = ("x", "y", "c")
LANES = 128
HEAD_DIM = 64
HEADS_PER_GROUP = LANES // HEAD_DIM
CONV_K = 3
HALO = 8
VMEM_LIMIT = 56 * 1024 * 1024
EXP_ZERO = -104.0

ADAM_LR = 0.001
ADAM_B1 = 0.9
ADAM_B2 = 0.999
ADAM_EPS = 1e-08
ADAM_WD = 0.01
ADAM_STEP = 10

ROW_BLOCK = 256
ATTN_BLOCK = 128
DW_TOKENS = 2048
DW_TILE = 1024
FF_CHUNK = 1024
PROJ_CHUNK = 512


def _dot(a, b):
    return lax.dot_general(a, b, (((1,), (0,)), ((), ())), preferred_element_type=F32)


def _dot_nt(a, b):
    return lax.dot_general(a, b, (((1,), (1,)), ((), ())), preferred_element_type=F32)


def _dot_tn(a, b):
    return lax.dot_general(a, b, (((0,), (0,)), ((), ())), preferred_element_type=F32)


def _sigmoid(z):
    return 1.0 / (1.0 + jnp.exp(-z))


def _rms_scale(x):
    return lax.rsqrt(jnp.mean(x * x, axis=-1, keepdims=True) + RMS_EPS)


def _rms_bwd(xhat, r, g, dy):
    gd = dy * g
    return r * (gd - xhat * jnp.mean(gd * xhat, axis=-1, keepdims=True))


def _params(n_axes, **kw):
    return pltpu.CompilerParams(dimension_semantics=("arbitrary",) * n_axes, vmem_limit_bytes=VMEM_LIMIT, **kw)


def _load_resident(pairs, sem):
    @pl.when(pl.program_id(0) == 0)
    def _():
        copies = [pltpu.make_async_copy(src, dst, sem.at[i]) for i, (src, dst) in enumerate(pairs)]
        for cp in copies:
            cp.start()
        for cp in copies:
            cp.wait()


def _row_spec(tm, width):
    return pl.BlockSpec((tm, width), lambda i: (i, 0))


def _col_spec(tm, width, col):
    return pl.BlockSpec((tm, width), lambda i: (i, col))


def _prev_halo_spec(tm, width, col=0):
    per = tm // HALO
    return pl.BlockSpec((HALO, width), lambda i: (jnp.maximum(i * per - 1, 0), col))


def _next_halo_spec(tm, width, n_rows):
    per = tm // HALO
    last = n_rows // HALO - 1
    return pl.BlockSpec((HALO, width), lambda i: (jnp.minimum((i + 1) * per, last), 0))


def _const_spec(shape):
    return pl.BlockSpec(shape, lambda i: (0,) * len(shape))


ANY = pl.BlockSpec(memory_space=pl.ANY)


def _shift_down(cur, prev, n):
    rows = lax.broadcasted_iota(jnp.int32, cur.shape, 0)
    out = pltpu.roll(cur, n, 0)
    for j in range(n):
        out = jnp.where(rows == j, prev[HALO - n + j:HALO - n + j + 1, :], out)
    return out


def _shift_up(cur, nxt, n):
    tm = cur.shape[0]
    rows = lax.broadcasted_iota(jnp.int32, cur.shape, 0)
    out = pltpu.roll(cur, tm - n, 0)
    for j in range(n):
        out = jnp.where(rows == tm - n + j, nxt[j:j + 1, :], out)
    return out


def _conv_taps(cm, cm_prev, wconv):
    cm1 = _shift_down(cm, cm_prev, 1)
    cm2 = _shift_down(cm, cm_prev, 2)
    cv = wconv[2:3, :] * cm + wconv[1:2, :] * cm1 + wconv[0:1, :] * cm2
    return cv, cm1, cm2


def _in_proj_fwd(x, g1, b_gate, w_in, aw, cw, tm):
    s, d = x.shape
    ni = w_in.shape[1]
    n_qkv, n_conv = 3 * aw, 3 * cw
    ch = PROJ_CHUNK

    def body(x_ref, g_ref, b_ref, w_hbm, qkv_ref, conv_ref, gate_ref, w_vmem, sem):
        _load_resident([(w_hbm, w_vmem)], sem)
        xv = x_ref[...]
        h = (xv * _rms_scale(xv) * g_ref[...]).astype(BF16)
        for c0 in range(0, ni, ch):
            pc = _dot(h, w_vmem[:, c0:c0 + ch])
            if c0 < n_qkv:
                qkv_ref[:, c0:c0 + ch] = pc.astype(BF16)
            elif c0 < n_qkv + n_conv:
                conv_ref[:, c0 - n_qkv:c0 - n_qkv + ch] = pc
            else:
                g0 = c0 - n_qkv - n_conv
                gate_ref[:, g0:g0 + ch] = _sigmoid(pc + b_ref[:, g0:g0 + ch])

    return pl.pallas_call(
        body, name="in_proj_fwd", grid=(s // tm,),
        in_specs=[_row_spec(tm, d), _const_spec((1, d)), _const_spec((1, 2 * d)), ANY],
        out_specs=[_row_spec(tm, n_qkv), _row_spec(tm, n_conv), _row_spec(tm, 2 * d)],
        out_shape=[jax.ShapeDtypeStruct((s, n_qkv), BF16), jax.ShapeDtypeStruct((s, n_conv), F32),
                   jax.ShapeDtypeStruct((s, 2 * d), F32)],
        scratch_shapes=[pltpu.VMEM((d, ni), BF16), pltpu.SemaphoreType.DMA((1,))],
        compiler_params=_params(1),
    )(x, g1, b_gate, w_in)


def _split_hi_lo(a):
    hi = a.astype(BF16)
    return hi, (a - hi.astype(F32)).astype(BF16)


def _log_gates(z):
    t = jnp.exp(-jnp.abs(z))
    u = 1.0 + t
    sp = jnp.log(u)
    return jnp.minimum(z, 0.0) - sp, jnp.minimum(-z, 0.0) - sp, t, u


def _attn_masks(t):
    row = lax.broadcasted_iota(jnp.int32, (t, t), 0)
    col = lax.broadcasted_iota(jnp.int32, (t, t), 1)
    causal = jnp.concatenate([col < row] * HEADS_PER_GROUP, axis=0)
    return causal, (row > col).astype(BF16), (row >= col).astype(BF16)


def _stack_heads(a):
    return jnp.concatenate([jnp.where(_head_lanes(h), a, jnp.zeros_like(a)) for h in range(HEADS_PER_GROUP)], axis=0)


def _unstack_heads(a, t):
    out = a[0:t]
    for h in range(1, HEADS_PER_GROUP):
        out = jnp.where(_head_lanes(h), a[h * t:(h + 1) * t], out)
    return out


def _while_weights_live(qi, block, carry):
    def cond(state):
        j, carry = state
        return jnp.logical_and(j < qi, jnp.max(carry[0]) >= EXP_ZERO)

    def step(state):
        j, carry = state
        return j + 1, block(qi - 1 - j, carry)

    return lax.while_loop(cond, step, (jnp.int32(0), carry))[1]


def _head_lanes(h):
    lane = lax.broadcasted_iota(jnp.int32, (1, LANES), 1)
    return (lane >= HEAD_DIM * h) & (lane < HEAD_DIM * (h + 1))


def _attn_fwd(qkv, aw, t, exchange=None):
    s = qkv.shape[0]
    groups = aw // LANES
    nq = s // t
    scale = HEAD_DIM ** -0.5
    ex = exchange or _NO_EXCHANGE

    def body(q_ref, k_ref, v_ref, *rest):
        ex_in, (o_ref,), ex_out, sems = _split_refs(rest, ex, 1)
        qi = pl.program_id(1)
        _exchange_start(ex, ex_in, ex_out, sems)
        causal, upper, _ = _attn_masks(t)
        qs = _stack_heads(q_ref[...] * scale)

        def block(kb, run, acc, diag):
            rows = pl.ds(pl.multiple_of(kb * t, t), t)
            z = _dot_nt(qs, k_ref[rows, :])
            log_b, log_keep, _, _ = _log_gates(z)
            if diag:
                log_keep = jnp.where(causal, log_keep, 0.0)
            hi, lo = _split_hi_lo(log_keep)
            between = _dot(hi, upper) + _dot(lo, upper) + run
            w = jnp.exp(log_b + between)
            if diag:
                w = jnp.where(causal, w, 0.0)
            acc = acc + _dot(w.astype(BF16), v_ref[rows, :])
            return run + jnp.sum(log_keep, axis=1, keepdims=True), acc

        rows_all = HEADS_PER_GROUP * t
        carry = block(qi, jnp.zeros((rows_all, 1), F32), jnp.zeros((rows_all, LANES), F32), True)
        _, acc = _while_weights_live(qi, lambda kb, carry: block(kb, *carry, False), carry)
        o_ref[...] = _unstack_heads(acc, t)
        _exchange_wait(ex, ex_in, ex_out, sems, groups, nq)

    return pl.pallas_call(
        body, name="attn_fwd", grid=(groups, nq),
        in_specs=[pl.BlockSpec((t, LANES), lambda g, i: (i, g)),
                  pl.BlockSpec((s, LANES), lambda g, i: (0, groups + g)),
                  pl.BlockSpec((s, LANES), lambda g, i: (0, 2 * groups + g))] + [ANY] * len(ex.arrays),
        out_specs=[pl.BlockSpec((t, LANES), lambda g, i: (i, g))] + [ANY] * len(ex.out_shapes),
        out_shape=[jax.ShapeDtypeStruct((s, aw), F32)] + ex.out_shapes,
        scratch_shapes=_exchange_sems(ex),
        compiler_params=_params(2),
    )(qkv, qkv, qkv, *ex.arrays)


def _attn_bwd(qkv, o, do, aw, t, exchange=None):
    s = qkv.shape[0]
    groups = aw // LANES
    nq = s // t
    scale = HEAD_DIM ** -0.5
    ex = exchange or _NO_EXCHANGE

    def body(q_ref, k_ref, v_ref, o_ref, do_ref, *rest):
        ex_in, (dq_ref, dk_ref, dv_ref), ex_out, (dk_acc, dv_acc, *sems) = _split_refs(rest, ex, 3)
        qi = pl.program_id(1)
        _exchange_start(ex, ex_in, ex_out, sems)

        @pl.when(qi == 0)
        def _():
            dk_acc[...] = jnp.zeros_like(dk_acc)
            dv_acc[...] = jnp.zeros_like(dv_acc)

        causal, upper, lower_incl = _attn_masks(t)
        qs = _stack_heads(q_ref[...] * scale)
        do_b = do_ref[...]
        dos = _stack_heads(do_b)
        e_total = jnp.sum(_stack_heads(do_b.astype(F32) * o_ref[...]), axis=1, keepdims=True)

        def block(kb, run, e_run, dq, diag):
            rows = pl.ds(pl.multiple_of(kb * t, t), t)
            k = k_ref[rows, :]
            v = v_ref[rows, :]
            z = _dot_nt(qs, k)
            log_b, log_keep, tt, u = _log_gates(z)
            r = 1.0 / u
            beta = jnp.where(z >= 0.0, r, tt * r)
            keep = jnp.where(z >= 0.0, tt * r, r)
            if diag:
                log_keep = jnp.where(causal, log_keep, 0.0)
            hi, lo = _split_hi_lo(log_keep)
            between = _dot(hi, upper) + _dot(lo, upper) + run
            w = jnp.exp(log_b + between)
            if diag:
                w = jnp.where(causal, w, 0.0)
            wb = w.astype(BF16)
            e = _dot_nt(dos, v) * wb.astype(F32)
            hi, lo = _split_hi_lo(e)
            e_suffix = _dot(hi, lower_incl) + _dot(lo, lower_incl) + e_run
            dz = e * keep - (e_total - e_suffix) * beta
            if diag:
                dz = jnp.where(causal, dz, 0.0)
            dzb = dz.astype(BF16)
            dk_acc[rows, :] += _dot_tn(dzb, qs)
            dv_acc[rows, :] += _dot_tn(wb, dos)
            return (run + jnp.sum(log_keep, axis=1, keepdims=True), e_suffix[:, 0:1], dq + _dot(dzb, k))

        rows_all = HEADS_PER_GROUP * t
        zero_col = jnp.zeros((rows_all, 1), F32)
        carry = block(qi, zero_col, zero_col, jnp.zeros((rows_all, LANES), F32), True)
        _, _, dq = _while_weights_live(qi, lambda kb, carry: block(kb, *carry, False), carry)
        dq_ref[...] = (_unstack_heads(dq, t) * scale).astype(BF16)

        @pl.when(qi == nq - 1)
        def _():
            dk_ref[...] = dk_acc[...].astype(BF16)
            dv_ref[...] = dv_acc[...].astype(BF16)

        _exchange_wait(ex, ex_in, ex_out, sems, groups, nq)

    blk = pl.BlockSpec((t, LANES), lambda g, i: (i, g))
    slab = pl.BlockSpec((s, LANES), lambda g, i: (0, g))
    return pl.pallas_call(
        body, name="attn_bwd", grid=(groups, nq),
        in_specs=[blk, pl.BlockSpec((s, LANES), lambda g, i: (0, groups + g)),
                  pl.BlockSpec((s, LANES), lambda g, i: (0, 2 * groups + g)), blk, blk] + [ANY] * len(ex.arrays),
        out_specs=[blk, slab, slab] + [ANY] * len(ex.out_shapes),
        out_shape=[jax.ShapeDtypeStruct((s, aw), BF16)] * 3 + ex.out_shapes,
        scratch_shapes=[pltpu.VMEM((s, LANES), F32), pltpu.VMEM((s, LANES), F32)] + _exchange_sems(ex),
        compiler_params=_params(2),
    )(qkv, qkv, qkv, o, do, *ex.arrays)


def _branches(o_b, conv, conv_prev, wconv, w_ao, w_co, cw, first):
    cb = conv[:, 0:cw]
    cm = conv[:, cw:2 * cw] * conv[:, 2 * cw:3 * cw]
    cm_prev = conv_prev[:, cw:2 * cw] * conv_prev[:, 2 * cw:3 * cw]
    cm_prev = jnp.where(first, 0.0, cm_prev)
    cv, cm1, cm2 = _conv_taps(cm, cm_prev, wconv)
    conv_in = (cb * cv).astype(BF16)
    return _dot(o_b, w_ao), _dot(conv_in, w_co), conv_in, cb, cv, cm, cm1, cm2


def _mix_fwd(x, o, conv, gate, wconv, g_post, w_ao, w_co, w_o, tm):
    s, d = x.shape
    aw, cw = w_ao.shape[0], w_co.shape[0]

    def body(x_ref, o_ref, conv_ref, prev_ref, gate_ref, wc_ref, g_ref, wao_hbm, wco_hbm, wo_hbm,
             x1_ref, mixed_ref, mixin_ref, convin_ref, wao, wco, wo, sem):
        _load_resident([(wao_hbm, wao), (wco_hbm, wco), (wo_hbm, wo)], sem)
        y_attn, y_conv, conv_in, *_ = _branches(
            o_ref[...].astype(BF16), conv_ref[...], prev_ref[...], wc_ref[...], wao[...], wco[...], cw,
            pl.program_id(0) == 0)
        mix_in = (gate_ref[:, 0:d] * y_attn + gate_ref[:, d:2 * d] * y_conv).astype(BF16)
        mixed = _dot(mix_in, wo[...])
        x1_ref[...] = x_ref[...] + mixed * _rms_scale(mixed) * g_ref[...]
        mixed_ref[...] = mixed
        mixin_ref[...] = mix_in
        convin_ref[...] = conv_in

    return pl.pallas_call(
        body, name="mix_fwd", grid=(s // tm,),
        in_specs=[_row_spec(tm, d), _row_spec(tm, aw), _row_spec(tm, 3 * cw), _prev_halo_spec(tm, 3 * cw),
                  _row_spec(tm, 2 * d), _const_spec((CONV_K, cw)), _const_spec((1, d)), ANY, ANY, ANY],
        out_specs=[_row_spec(tm, d), _row_spec(tm, d), _row_spec(tm, d), _row_spec(tm, cw)],
        out_shape=[jax.ShapeDtypeStruct((s, d), F32), jax.ShapeDtypeStruct((s, d), F32),
                   jax.ShapeDtypeStruct((s, d), BF16), jax.ShapeDtypeStruct((s, cw), BF16)],
        scratch_shapes=[pltpu.VMEM(w_ao.shape, BF16), pltpu.VMEM(w_co.shape, BF16), pltpu.VMEM(w_o.shape, BF16),
                        pltpu.SemaphoreType.DMA((3,))],
        compiler_params=_params(1),
    )(x, o, conv, conv, gate, wconv, g_post, w_ao, w_co, w_o)


def _mix_bwd(dx1, mixed, o, conv, gate, wconv, g_post, w_ao, w_co, w_o, tm):
    s, d = dx1.shape
    aw, cw = w_ao.shape[0], w_co.shape[0]

    def body(dx1_ref, mixed_ref, o_ref, conv_ref, prev_ref, gate_ref, wc_ref, g_ref, wao_hbm, wco_hbm, wo_hbm,
             dmixed_ref, dattn_ref, dconvout_ref, do_ref, dgate_ref, dcb_ref, dcv_ref, dg_ref, dbias_ref,
             wao, wco, wo, sem):
        i = pl.program_id(0)
        _load_resident([(wao_hbm, wao), (wco_hbm, wco), (wo_hbm, wo)], sem)

        @pl.when(i == 0)
        def _():
            dg_ref[...] = jnp.zeros_like(dg_ref)
            dbias_ref[...] = jnp.zeros_like(dbias_ref)

        mixed = mixed_ref[...]
        r = _rms_scale(mixed)
        mhat = mixed * r
        dn = dx1_ref[...]
        dg_ref[...] += jnp.sum(dn * mhat, axis=0, keepdims=True)
        dmixed = _rms_bwd(mhat, r, g_ref[...], dn).astype(BF16)
        dmixed_ref[...] = dmixed
        dmi = _dot_nt(dmixed, wo[...])

        y_attn, y_conv, _, cb, cv, *_ = _branches(
            o_ref[...].astype(BF16), conv_ref[...], prev_ref[...], wc_ref[...], wao[...], wco[...], cw, i == 0)
        ga = gate_ref[:, 0:d]
        gc = gate_ref[:, d:2 * d]
        dpre_a = dmi * y_attn * ga * (1.0 - ga)
        dpre_c = dmi * y_conv * gc * (1.0 - gc)
        dgate_ref[:, 0:d] = dpre_a.astype(BF16)
        dgate_ref[:, d:2 * d] = dpre_c.astype(BF16)
        dbias_ref[:, 0:d] += jnp.sum(dpre_a, axis=0, keepdims=True)
        dbias_ref[:, d:2 * d] += jnp.sum(dpre_c, axis=0, keepdims=True)

        dattn = (dmi * ga).astype(BF16)
        dattn_ref[...] = dattn
        do_ref[...] = _dot_nt(dattn, wao[...]).astype(BF16)
        dconvout = (dmi * gc).astype(BF16)
        dconvout_ref[...] = dconvout
        dconv_in = _dot_nt(dconvout, wco[...])
        dcb_ref[...] = (dconv_in * cv).astype(BF16)
        dcv_ref[...] = dconv_in * cb

    return pl.pallas_call(
        body, name="mix_bwd", grid=(s // tm,),
        in_specs=[_row_spec(tm, d), _row_spec(tm, d), _row_spec(tm, aw), _row_spec(tm, 3 * cw),
                  _prev_halo_spec(tm, 3 * cw), _row_spec(tm, 2 * d), _const_spec((CONV_K, cw)), _const_spec((1, d)),
                  ANY, ANY, ANY],
        out_specs=[_row_spec(tm, d), _row_spec(tm, d), _row_spec(tm, d), _row_spec(tm, aw), _row_spec(tm, 2 * d),
                   _row_spec(tm, cw), _row_spec(tm, cw), _const_spec((1, d)), _const_spec((1, 2 * d))],
        out_shape=[jax.ShapeDtypeStruct((s, d), BF16), jax.ShapeDtypeStruct((s, d), BF16),
                   jax.ShapeDtypeStruct((s, d), BF16), jax.ShapeDtypeStruct((s, aw), BF16),
                   jax.ShapeDtypeStruct((s, 2 * d), BF16), jax.ShapeDtypeStruct((s, cw), BF16),
                   jax.ShapeDtypeStruct((s, cw), F32), jax.ShapeDtypeStruct((1, d), F32),
                   jax.ShapeDtypeStruct((1, 2 * d), F32)],
        scratch_shapes=[pltpu.VMEM(w_ao.shape, BF16), pltpu.VMEM(w_co.shape, BF16), pltpu.VMEM(w_o.shape, BF16),
                        pltpu.SemaphoreType.DMA((3,))],
        compiler_params=_params(1),
    )(dx1, mixed, o, conv, conv, gate, wconv, g_post, w_ao, w_co, w_o)


def _mlp_ple_loss(x1, p, target, g_pre, g_post, g_ple, w_up, w_dn, w_pg, w_pp, tm):
    s, d = x1.shape
    ff = w_up.shape[1]
    pd = p.shape[1]
    fc = FF_CHUNK

    def body(x1_ref, p_ref, t_ref, gpre_ref, gpost_ref, gple_ref, wup_hbm, wdn_hbm, wpg_hbm, wpp_hbm,
             dx1_ref, h2_ref, du_ref, a_ref, df_ref, h3_ref, ds3_ref, dpp_ref, loss_ref, dgpre_ref, dgpost_ref,
             dgple_ref, wup, wdn, wpg, wpp, u_scr, sem):
        _load_resident([(wup_hbm, wup), (wdn_hbm, wdn), (wpg_hbm, wpg), (wpp_hbm, wpp)], sem)

        @pl.when(pl.program_id(0) == 0)
        def _():
            for ref in (loss_ref, dgpre_ref, dgpost_ref, dgple_ref):
                ref[...] = jnp.zeros_like(ref)

        x1v = x1_ref[...]
        r2 = _rms_scale(x1v)
        x1hat = x1v * r2
        h2 = (x1hat * gpre_ref[...]).astype(BF16)
        h2_ref[...] = h2
        f = jnp.zeros((tm, d), F32)
        for c0 in range(0, ff, fc):
            u = _dot(h2, wup[:, c0:c0 + fc])
            u_scr[:, c0:c0 + fc] = u
            a = jnp.square(jnp.maximum(u, 0.0)).astype(BF16)
            a_ref[:, c0:c0 + fc] = a
            f = f + _dot(a, wdn[c0:c0 + fc, :])
        rf = _rms_scale(f)
        fhat = f * rf
        x2 = x1v + fhat * gpost_ref[...]
        r3 = _rms_scale(x2)
        x2hat = x2 * r3
        h3 = (x2hat * gple_ref[...]).astype(BF16)
        h3_ref[...] = h3
        pg = _sigmoid(_dot(h3, wpg[...]))
        pp = _dot(p_ref[...].astype(BF16), wpp[...])
        diff = x2 + pg * pp - t_ref[...]
        loss_ref[...] += 0.5 * jnp.sum(jnp.mean(diff * diff, axis=-1, keepdims=True), axis=0, keepdims=True)

        dy = diff * (1.0 / d)
        dpp_ref[...] = (dy * pg).astype(BF16)
        ds3 = (dy * pp * pg * (1.0 - pg)).astype(BF16)
        ds3_ref[...] = ds3
        dh3 = _dot_nt(ds3, wpg[...])
        dgple_ref[...] += jnp.sum(dh3 * x2hat, axis=0, keepdims=True)
        dx2 = dy + _rms_bwd(x2hat, r3, gple_ref[...], dh3)
        dgpost_ref[...] += jnp.sum(dx2 * fhat, axis=0, keepdims=True)
        df = _rms_bwd(fhat, rf, gpost_ref[...], dx2).astype(BF16)
        df_ref[...] = df
        dh2 = jnp.zeros((tm, d), F32)
        for c0 in range(0, ff, fc):
            da = _dot_nt(df, wdn[c0:c0 + fc, :])
            du = (da * (2.0 * jnp.maximum(u_scr[:, c0:c0 + fc], 0.0))).astype(BF16)
            du_ref[:, c0:c0 + fc] = du
            dh2 = dh2 + _dot_nt(du, wup[:, c0:c0 + fc])
        dgpre_ref[...] += jnp.sum(dh2 * x1hat, axis=0, keepdims=True)
        dx1_ref[...] = dx2 + _rms_bwd(x1hat, r2, gpre_ref[...], dh2)

    vec = _const_spec((1, d))
    return pl.pallas_call(
        body, name="mlp_ple_loss", grid=(s // tm,),
        in_specs=[_row_spec(tm, d), _row_spec(tm, pd), _row_spec(tm, d), vec, vec, vec, ANY, ANY, ANY, ANY],
        out_specs=[_row_spec(tm, d), _row_spec(tm, d), _row_spec(tm, ff), _row_spec(tm, ff), _row_spec(tm, d),
                   _row_spec(tm, d), _row_spec(tm, d), _row_spec(tm, d), _const_spec((1, 1)), vec, vec, vec],
        out_shape=[jax.ShapeDtypeStruct((s, d), F32), jax.ShapeDtypeStruct((s, d), BF16),
                   jax.ShapeDtypeStruct((s, ff), BF16), jax.ShapeDtypeStruct((s, ff), BF16),
                   jax.ShapeDtypeStruct((s, d), BF16), jax.ShapeDtypeStruct((s, d), BF16),
                   jax.ShapeDtypeStruct((s, d), BF16), jax.ShapeDtypeStruct((s, d), BF16),
                   jax.ShapeDtypeStruct((1, 1), F32), jax.ShapeDtypeStruct((1, d), F32),
                   jax.ShapeDtypeStruct((1, d), F32), jax.ShapeDtypeStruct((1, d), F32)],
        scratch_shapes=[pltpu.VMEM(w_up.shape, BF16), pltpu.VMEM(w_dn.shape, BF16), pltpu.VMEM(w_pg.shape, BF16),
                        pltpu.VMEM(w_pp.shape, BF16), pltpu.VMEM((tm, ff), F32), pltpu.SemaphoreType.DMA((4,))],
        compiler_params=_params(1),
    )(x1, p, target, g_pre, g_post, g_ple, w_up, w_dn, w_pg, w_pp)


def _in_proj_bwd(x, dx1, dq, dk, dv, dcb, dcv, conv, dgate, wconv, g1, w_in, tm):
    s, d = x.shape
    aw, cw = dq.shape[1], dcb.shape[1]
    ni = w_in.shape[1]

    def body(x_ref, dx1_ref, dq_ref, dk_ref, dv_ref, dcb_ref, dcv_ref, dcvn_ref, cc_ref, cu_ref, ccp_ref, cup_ref,
             dgate_ref, wc_ref, g_ref, w_hbm, dx_ref, dproj_ref, h1_ref, dg_ref, dwc_ref, w_vmem, sem):
        i = pl.program_id(0)
        _load_resident([(w_hbm, w_vmem)], sem)

        @pl.when(i == 0)
        def _():
            dg_ref[...] = jnp.zeros_like(dg_ref)
            dwc_ref[...] = jnp.zeros_like(dwc_ref)

        wc = wc_ref[...]
        cc = cc_ref[...]
        cu = cu_ref[...]
        cm = cc * cu
        cm_prev = jnp.where(i == 0, 0.0, ccp_ref[...] * cup_ref[...])
        _, cm1, cm2 = _conv_taps(cm, cm_prev, wc)
        dcv_cur = dcv_ref[...]
        dcv_next = jnp.where(i == pl.num_programs(0) - 1, 0.0, dcvn_ref[...])
        dcm = (wc[2:3, :] * dcv_cur + wc[1:2, :] * _shift_up(dcv_cur, dcv_next, 1)
               + wc[0:1, :] * _shift_up(dcv_cur, dcv_next, 2))
        for tap, shifted in enumerate((cm2, cm1, cm)):
            dwc_ref[tap:tap + 1, :] += jnp.sum(dcv_cur * shifted, axis=0, keepdims=True)

        pieces = [(dq_ref[...], aw), (dk_ref[...], aw), (dv_ref[...], aw), (dcb_ref[...], cw),
                  ((dcm * cu).astype(BF16), cw), ((dcm * cc).astype(BF16), cw), (dgate_ref[...], 2 * d)]
        dh = jnp.zeros((tm, d), F32)
        c0 = 0
        for piece, width in pieces:
            dproj_ref[:, c0:c0 + width] = piece
            dh = dh + _dot_nt(piece, w_vmem[:, c0:c0 + width])
            c0 += width

        xv = x_ref[...]
        r = _rms_scale(xv)
        xhat = xv * r
        h1_ref[...] = (xhat * g_ref[...]).astype(BF16)
        dg_ref[...] += jnp.sum(dh * xhat, axis=0, keepdims=True)
        dx_ref[...] = dx1_ref[...] + _rms_bwd(xhat, r, g_ref[...], dh)

    return pl.pallas_call(
        body, name="in_proj_bwd", grid=(s // tm,),
        in_specs=[_row_spec(tm, d), _row_spec(tm, d), _row_spec(tm, aw), _row_spec(tm, aw), _row_spec(tm, aw),
                  _row_spec(tm, cw), _row_spec(tm, cw), _next_halo_spec(tm, cw, s), _col_spec(tm, cw, 1),
                  _col_spec(tm, cw, 2), _prev_halo_spec(tm, cw, 1), _prev_halo_spec(tm, cw, 2), _row_spec(tm, 2 * d),
                  _const_spec((CONV_K, cw)), _const_spec((1, d)), ANY],
        out_specs=[_row_spec(tm, d), _row_spec(tm, ni), _row_spec(tm, d), _const_spec((1, d)),
                   _const_spec((CONV_K, cw))],
        out_shape=[jax.ShapeDtypeStruct((s, d), F32), jax.ShapeDtypeStruct((s, ni), BF16),
                   jax.ShapeDtypeStruct((s, d), BF16), jax.ShapeDtypeStruct((1, d), F32),
                   jax.ShapeDtypeStruct((CONV_K, cw), F32)],
        scratch_shapes=[pltpu.VMEM((d, ni), BF16), pltpu.SemaphoreType.DMA((1,))],
        compiler_params=_params(1),
    )(x, dx1, dq, dk, dv, dcb, dcv, dcv, conv, conv, conv, conv, dgate, wconv, g1, w_in)


def _weight_grad(a, b, name):
    s, m = a.shape
    n = b.shape[1]
    tm, tn, tk = min(m, DW_TILE), min(n, DW_TILE), min(s, DW_TOKENS)
    nk = s // tk

    def body(a_ref, b_ref, o_ref, acc):
        k = pl.program_id(2)

        @pl.when(k == 0)
        def _():
            acc[...] = jnp.zeros_like(acc)

        acc[...] += _dot_tn(a_ref[...].astype(BF16), b_ref[...].astype(BF16))

        @pl.when(k == nk - 1)
        def _():
            o_ref[...] = acc[...].astype(BF16)

    return pl.pallas_call(
        body, name=name, grid=(m // tm, n // tn, nk),
        in_specs=[pl.BlockSpec((tk, tm), lambda i, j, k: (k, i)), pl.BlockSpec((tk, tn), lambda i, j, k: (k, j))],
        out_specs=pl.BlockSpec((tm, tn), lambda i, j, k: (i, j)),
        out_shape=jax.ShapeDtypeStruct((m, n), BF16),
        scratch_shapes=[pltpu.VMEM((tm, tn), F32)],
        compiler_params=_params(3),
    )(a, b)


def _mesh_position():
    return tuple(lax.axis_index(a) for a in MESH_AXES)


def _peer(me, k):
    bits = ((k >> 2) & 1, (k >> 1) & 1, k & 1)
    pos = tuple(1 - m if b else m for m, b in zip(me, bits))
    return pos, 4 * pos[0] + 2 * pos[1] + pos[2]


class _Exchange:
    def __init__(self, arrays, out_shapes, src, dst):
        self.arrays, self.out_shapes, self.src, self.dst = list(arrays), list(out_shapes), src, dst


_NO_EXCHANGE = _Exchange([], [], None, None)


def _exchange_sems(ex):
    n = len(ex.arrays)
    if n == 0:
        return []
    return [pltpu.SemaphoreType.DMA((n, N_DEV - 1)), pltpu.SemaphoreType.DMA((n, N_DEV - 1)),
            pltpu.SemaphoreType.DMA((n,))]


def _split_refs(rest, ex, n_own_outs):
    n_in, n_out = len(ex.arrays), len(ex.out_shapes)
    ex_in, rest = rest[:n_in], rest[n_in:]
    own, rest = rest[:n_own_outs], rest[n_own_outs:]
    return ex_in, own, rest[:n_out], rest[n_out:]


def _exchange_copies(ex, in_refs, out_refs, sems):
    send_sems, recv_sems, local_sems = sems
    me = _mesh_position()
    mine = 4 * me[0] + 2 * me[1] + me[2]
    copies = []
    for w in range(len(ex.arrays)):
        landing = ex.dst(w, out_refs, mine)
        copies.append(pltpu.make_async_copy(ex.src(w, in_refs, mine), landing, local_sems.at[w]))
        for k in range(1, N_DEV):
            peer, peer_idx = _peer(me, k)
            copies.append(pltpu.make_async_remote_copy(
                src_ref=ex.src(w, in_refs, peer_idx), dst_ref=landing, send_sem=send_sems.at[w, k - 1],
                recv_sem=recv_sems.at[w, k - 1], device_id=peer, device_id_type=pl.DeviceIdType.MESH))
    return copies


def _exchange_start(ex, in_refs, out_refs, sems):
    if ex.arrays:
        @pl.when(jnp.logical_and(pl.program_id(0) == 0, pl.program_id(1) == 0))
        def _():
            for cp in _exchange_copies(ex, in_refs, out_refs, sems):
                cp.start()


def _exchange_wait(ex, in_refs, out_refs, sems, n0, n1):
    if ex.arrays:
        @pl.when(jnp.logical_and(pl.program_id(0) == n0 - 1, pl.program_id(1) == n1 - 1))
        def _():
            for cp in _exchange_copies(ex, in_refs, out_refs, sems):
                cp.wait()


def _shard_block(ref, shard_shape, by_col, idx):
    r, c = shard_shape
    if by_col:
        return ref.at[:, pl.ds(pl.multiple_of(idx * c, LANES), c)]
    return ref.at[pl.ds(pl.multiple_of(idx * r, 16), r), :]


def _full_shape(shard_shape, by_col):
    r, c = shard_shape
    return (r, N_DEV * c) if by_col else (N_DEV * r, c)


def _gather_exchange(shards, col_sharded):
    shapes = [a.shape for a in shards]
    return _Exchange(
        shards, [jax.ShapeDtypeStruct(_full_shape(sh, bc), a.dtype) for a, sh, bc in zip(shards, shapes, col_sharded)],
        lambda w, refs, idx: refs[w],
        lambda w, refs, mine: _shard_block(refs[w], shapes[w], col_sharded[w], mine))


def _scatter_exchange(grads, col_sharded):
    shapes = []
    for g, by_col in zip(grads, col_sharded):
        r, c = g.shape
        shapes.append((r, c // N_DEV) if by_col else (r // N_DEV, c))
    return _Exchange(
        grads, [jax.ShapeDtypeStruct((N_DEV,) + sh, g.dtype) for g, sh in zip(grads, shapes)],
        lambda w, refs, idx: _shard_block(refs[w], shapes[w], col_sharded[w], idx),
        lambda w, refs, mine: refs[w].at[mine])


def _broadcast_exchange(arrays):
    return _Exchange(arrays, [jax.ShapeDtypeStruct((N_DEV,) + a.shape, a.dtype) for a in arrays],
                     lambda w, refs, idx: refs[w], lambda w, refs, mine: refs[w].at[mine])


def _join(*exs):
    arrays, shapes, owner = [], [], []
    for e in exs:
        for w in range(len(e.arrays)):
            owner.append((e, w, len(arrays), len(shapes)))
        arrays += e.arrays
        shapes += e.out_shapes

    def src(w, refs, idx):
        e, w0, i0, _ = owner[w]
        return e.src(w0, refs[i0:i0 + len(e.arrays)], idx)

    def dst(w, refs, mine):
        e, w0, _, o0 = owner[w]
        return e.dst(w0, refs[o0:o0 + len(e.out_shapes)], mine)

    return _Exchange(arrays, shapes, src, dst)


def _exchange_call(ex, name):
    n_in = len(ex.arrays)

    def body(*refs):
        in_refs, _, out_refs, sems = _split_refs(refs, ex, 0)
        copies = _exchange_copies(ex, in_refs, out_refs, sems)
        for cp in copies:
            cp.start()
        for cp in copies:
            cp.wait()

    return pl.pallas_call(
        body, name=name, in_specs=[ANY] * n_in, out_specs=[ANY] * len(ex.out_shapes), out_shape=ex.out_shapes,
        scratch_shapes=_exchange_sems(ex), compiler_params=pltpu.CompilerParams(vmem_limit_bytes=VMEM_LIMIT),
    )(*ex.arrays)


def _to_bf16(arrays):
    def body(*refs):
        for src, dst in zip(refs[:len(arrays)], refs[len(arrays):]):
            dst[...] = src[...].astype(BF16)

    vmem = pl.BlockSpec(memory_space=pltpu.VMEM)
    return pl.pallas_call(
        body, name="weights_to_bf16", in_specs=[vmem] * len(arrays), out_specs=[vmem] * len(arrays),
        out_shape=[jax.ShapeDtypeStruct(a.shape, BF16) for a in arrays],
        compiler_params=pltpu.CompilerParams(vmem_limit_bytes=VMEM_LIMIT),
    )(*arrays)


def _adamw(w, g, m, v):
    m = ADAM_B1 * m + (1.0 - ADAM_B1) * g
    v = ADAM_B2 * v + (1.0 - ADAM_B2) * jnp.square(g)
    m_hat = m / (1.0 - ADAM_B1 ** ADAM_STEP)
    v_hat = v / (1.0 - ADAM_B2 ** ADAM_STEP)
    delta = -ADAM_LR * (m_hat / (jnp.sqrt(v_hat) + ADAM_EPS) + ADAM_WD * w)
    return delta, m, v


def _sum_and_adamw(parts, w, m, v, name):
    r, c = w.shape
    tr = min(r, 256)

    def body(p_ref, w_ref, m_ref, v_ref, g_out, d_out, m_out, v_out):
        g = p_ref[0].astype(F32)
        for dev in range(1, N_DEV):
            g = g + p_ref[dev].astype(F32)
        g_out[...] = g
        d_out[...], m_out[...], v_out[...] = _adamw(w_ref[...], g, m_ref[...], v_ref[...])

    blk = pl.BlockSpec((tr, c), lambda i: (i, 0))
    return pl.pallas_call(
        body, name=name, grid=(r // tr,),
        in_specs=[pl.BlockSpec((N_DEV, tr, c), lambda i: (0, i, 0)), blk, blk, blk],
        out_specs=[blk] * 4, out_shape=[jax.ShapeDtypeStruct((r, c), F32)] * 4,
        compiler_params=_params(1),
    )(parts, w, m, v)


BIG = ("w_in", "w_attn_out", "w_conv_out", "w_o", "w_up", "w_down", "w_ple_gate", "w_ple_proj")
COL_SHARDED = {"w_in": True, "w_attn_out": True, "w_conv_out": True, "w_o": False, "w_up": True, "w_down": False,
               "w_ple_gate": False, "w_ple_proj": True}
SMALL = ("g_pre_mix", "b_gate", "g_post_mix", "g_pre_mlp", "g_post_mlp", "g_ple")


REST = BIG[1:]


def _local_grads(x, p, target, small, wconv, full, aw, cw, tm, t, gather_rest=None, scatter_rest=None):
    full = dict(full)
    qkv, conv, gate = _in_proj_fwd(x, small["g_pre_mix"], small["b_gate"], full["w_in"], aw, cw, tm)
    o, *rest = _attn_fwd(qkv, aw, t, gather_rest)
    full.update(zip(REST, rest))
    x1, mixed, mix_in, conv_in = _mix_fwd(x, o, conv, gate, wconv, small["g_post_mix"], full["w_attn_out"],
                                          full["w_conv_out"], full["w_o"], tm)
    (dx1, h2, du, a, df, h3, ds3, dpp, loss, dg_pre_mlp, dg_post_mlp, dg_ple) = _mlp_ple_loss(
        x1, p, target, small["g_pre_mlp"], small["g_post_mlp"], small["g_ple"], full["w_up"], full["w_down"],
        full["w_ple_gate"], full["w_ple_proj"], tm)
    big = {"w_up": _weight_grad(h2, du, "dw_up"), "w_down": _weight_grad(a, df, "dw_down"),
           "w_ple_gate": _weight_grad(h3, ds3, "dw_ple_gate"), "w_ple_proj": _weight_grad(p, dpp, "dw_ple_proj")}
    (dmixed, dattn, dconvout, do, dgate, dcb, dcv, dg_post_mix, db_gate) = _mix_bwd(
        dx1, mixed, o, conv, gate, wconv, small["g_post_mix"], full["w_attn_out"], full["w_conv_out"], full["w_o"],
        tm)
    big.update({"w_attn_out": _weight_grad(o, dattn, "dw_attn_out"),
                "w_conv_out": _weight_grad(conv_in, dconvout, "dw_conv_out"),
                "w_o": _weight_grad(mix_in, dmixed, "dw_o")})
    dq, dk, dv, *scattered = _attn_bwd(qkv, o, do, aw, t, scatter_rest and scatter_rest([big[n] for n in REST]))
    dx, dproj, h1, dg_pre_mix, dwconv = _in_proj_bwd(x, dx1, dq, dk, dv, dcb, dcv, conv, dgate, wconv,
                                                    small["g_pre_mix"], full["w_in"], tm)
    big["w_in"] = _weight_grad(h1, dproj, "dw_in")
    small_grads = {"g_pre_mix": dg_pre_mix, "b_gate": db_gate, "g_post_mix": dg_post_mix, "g_pre_mlp": dg_pre_mlp,
                   "g_post_mlp": dg_post_mlp, "g_ple": dg_ple, "w_conv": dwconv}
    return loss[0, 0], dx, big, small_grads, scattered


def _pack_small(vals, d):
    parts = []
    for a in vals:
        a = jnp.pad(a.reshape(-1), (0, -a.size % d)).reshape(-1, d)
        parts.append(jnp.pad(a, ((0, HALO - a.shape[0]), (0, 0))))
    return jnp.concatenate(parts, axis=0)


def _unpack_small(pack, shapes, d):
    out = []
    for i, shp in enumerate(shapes):
        n = 1
        for v in shp:
            n *= v
        rows = -(-n // d)
        out.append(pack[i * HALO:i * HALO + rows].reshape(-1)[:n].reshape(shp))
    return out


def kernel(x, p, g_pre_mix, w_in, b_gate, w_conv, w_attn_out, w_conv_out, w_o, g_post_mix, g_pre_mlp, w_up, w_down, g_post_mlp, g_ple, w_ple_gate, w_ple_proj, loss_target, m_g_pre_mix, m_w_in, m_b_gate, m_w_conv, m_w_attn_out, m_w_conv_out, m_w_o, m_g_post_mix, m_g_pre_mlp, m_w_up, m_w_down, m_g_post_mlp, m_g_ple, m_w_ple_gate, m_w_ple_proj, v_g_pre_mix, v_w_in, v_b_gate, v_w_conv, v_w_attn_out, v_w_conv_out, v_w_o, v_g_post_mix, v_g_pre_mlp, v_w_up, v_w_down, v_g_post_mlp, v_g_ple, v_w_ple_gate, v_w_ple_proj):
    given = dict(locals())
    order = ["g_pre_mix", "w_in", "b_gate", "w_conv", "w_attn_out", "w_conv_out", "w_o", "g_post_mix", "g_pre_mlp",
             "w_up", "w_down", "g_post_mlp", "g_ple", "w_ple_gate", "w_ple_proj"]
    d = x.shape[-1]
    me = 4 * lax.axis_index("x") + 2 * lax.axis_index("y") + lax.axis_index("c")

    col = [COL_SHARDED[n] for n in BIG]
    shards = _to_bf16([given[n][0] for n in BIG])
    cw_shard = w_conv.shape[-1]
    conv_tile = jnp.pad(w_conv[0], ((0, HALO - CONV_K), (0, LANES - cw_shard)))
    w_in_full, conv_g = _exchange_call(
        _join(_gather_exchange(shards[:1], col[:1]), _broadcast_exchange([conv_tile])), "gather_w_in")
    wconv = jnp.concatenate([conv_g[dev, :CONV_K, :cw_shard] for dev in range(N_DEV)], axis=1)

    small = {n: given[n] for n in SMALL}
    loss, dx, big_grads, small_grads, parts_rest = _local_grads(
        x[0], p[0, 0], loss_target[0], small, wconv, {"w_in": w_in_full}, w_attn_out.shape[1], w_conv_out.shape[1],
        ROW_BLOCK, ATTN_BLOCK,
        _gather_exchange(shards[1:], col[1:]), lambda grads: _scatter_exchange(grads, col[1:]))
    loss = lax.psum(loss, MESH_AXES)

    small_names = list(SMALL) + ["w_conv"]
    pack = _pack_small([small_grads[n] for n in small_names], d)
    part_in, packs = _exchange_call(
        _join(_scatter_exchange([big_grads["w_in"]], col[:1]), _broadcast_exchange([pack])), "scatter_dw_in")
    parts = [part_in] + list(parts_rest)

    grads, deltas, new_m, new_v = {}, {}, {}, {}
    for n, part in zip(BIG, parts):
        grads[n], deltas[n], new_m[n], new_v[n] = (
            a[None] for a in _sum_and_adamw(part, given[n][0], given["m_" + n][0], given["v_" + n][0], "adamw_" + n))

    full_conv = lambda a: lax.dynamic_update_slice(jnp.zeros((CONV_K, N_DEV * cw_shard), F32), a[0],
                                                   (jnp.int32(0), me * cw_shard))
    state = [_pack_small([given[pre + n] for n in SMALL] + [full_conv(given[pre + "w_conv"])], d)
             for pre in ("", "m_", "v_")]
    outs = _sum_and_adamw(packs, *state, "adamw_small")
    shapes = [given[n].shape for n in SMALL] + [(CONV_K, N_DEV * cw_shard)]
    for res, dst in zip(outs, (grads, deltas, new_m, new_v)):
        for n, a in zip(small_names, _unpack_small(res, shapes, d)):
            dst[n] = (lax.dynamic_slice(a, (jnp.int32(0), me * cw_shard), (CONV_K, cw_shard))[None]
                      if n == "w_conv" else a)

    return (loss, dx[None], *[grads[n] for n in order], *[deltas[n] for n in order],
            *[new_m[n] for n in order], *[new_v[n] for n in order])
```

```python
import jax
import jax.numpy as jnp
from jax import lax
from jax.experimental import pallas as pl
from jax.experimental.pallas import tpu as pltpu

F32 = jnp.float32
BF16 = jnp.bfloat16
RMS_EPS = 1e-6
N_DEV = 8
MESH_AXES = ("x", "y", "c")
LANES = 128
HEAD_DIM = 64
HEADS_PER_GROUP = LANES // HEAD_DIM
CONV_K = 3
HALO = 8
VMEM_LIMIT = 56 * 1024 * 1024
EXP_ZERO = -104.0

ADAM_LR = 0.001
ADAM_B1 = 0.9
ADAM_B2 = 0.999
ADAM_EPS = 1e-08
ADAM_WD = 0.01
ADAM_STEP = 10

ROW_BLOCK = 256
ATTN_BLOCK = 256
DW_TOKENS = 2048
DW_TILE = 1024
FF_CHUNK = 1024
PROJ_CHUNK = 512


def _dot(a, b):
    return lax.dot_general(a, b, (((1,), (0,)), ((), ())), preferred_element_type=F32)


def _dot_nt(a, b):
    return lax.dot_general(a, b, (((1,), (1,)), ((), ())), preferred_element_type=F32)


def _dot_tn(a, b):
    return lax.dot_general(a, b, (((0,), (0,)), ((), ())), preferred_element_type=F32)


def _sigmoid(z):
    return 1.0 / (1.0 + jnp.exp(-z))


def _rms_scale(x):
    return lax.rsqrt(jnp.mean(x * x, axis=-1, keepdims=True) + RMS_EPS)


def _rms_bwd(xhat, r, g, dy):
    gd = dy * g
    return r * (gd - xhat * jnp.mean(gd * xhat, axis=-1, keepdims=True))


def _params(n_axes, **kw):
    return pltpu.CompilerParams(dimension_semantics=("arbitrary",) * n_axes, vmem_limit_bytes=VMEM_LIMIT, **kw)


def _load_resident(pairs, sem):
    @pl.when(pl.program_id(0) == 0)
    def _():
        copies = [pltpu.make_async_copy(src, dst, sem.at[i]) for i, (src, dst) in enumerate(pairs)]
        for cp in copies:
            cp.start()
        for cp in copies:
            cp.wait()


def _row_spec(tm, width):
    return pl.BlockSpec((tm, width), lambda i: (i, 0))


def _col_spec(tm, width, col):
    return pl.BlockSpec((tm, width), lambda i: (i, col))


def _prev_halo_spec(tm, width, col=0):
    per = tm // HALO
    return pl.BlockSpec((HALO, width), lambda i: (jnp.maximum(i * per - 1, 0), col))


def _next_halo_spec(tm, width, n_rows):
    per = tm // HALO
    last = n_rows // HALO - 1
    return pl.BlockSpec((HALO, width), lambda i: (jnp.minimum((i + 1) * per, last), 0))


def _const_spec(shape):
    return pl.BlockSpec(shape, lambda i: (0,) * len(shape))


ANY = pl.BlockSpec(memory_space=pl.ANY)


def _shift_down(cur, prev, n):
    rows = lax.broadcasted_iota(jnp.int32, cur.shape, 0)
    out = pltpu.roll(cur, n, 0)
    for j in range(n):
        out = jnp.where(rows == j, prev[HALO - n + j:HALO - n + j + 1, :], out)
    return out


def _shift_up(cur, nxt, n):
    tm = cur.shape[0]
    rows = lax.broadcasted_iota(jnp.int32, cur.shape, 0)
    out = pltpu.roll(cur, tm - n, 0)
    for j in range(n):
        out = jnp.where(rows == tm - n + j, nxt[j:j + 1, :], out)
    return out


def _conv_taps(cm, cm_prev, wconv):
    cm1 = _shift_down(cm, cm_prev, 1)
    cm2 = _shift_down(cm, cm_prev, 2)
    cv = wconv[2:3, :] * cm + wconv[1:2, :] * cm1 + wconv[0:1, :] * cm2
    return cv, cm1, cm2


def _in_proj_fwd(x, g1, b_gate, w_in, aw, cw, tm):
    s, d = x.shape
    ni = w_in.shape[1]
    n_qkv, n_conv = 3 * aw, 3 * cw
    ch = PROJ_CHUNK

    def body(x_ref, g_ref, b_ref, w_hbm, qkv_ref, conv_ref, gate_ref, w_vmem, sem):
        _load_resident([(w_hbm, w_vmem)], sem)
        xv = x_ref[...]
        h = (xv * _rms_scale(xv) * g_ref[...]).astype(BF16)
        for c0 in range(0, ni, ch):
            pc = _dot(h, w_vmem[:, c0:c0 + ch])
            if c0 < n_qkv:
                qkv_ref[:, c0:c0 + ch] = pc.astype(BF16)
            elif c0 < n_qkv + n_conv:
                conv_ref[:, c0 - n_qkv:c0 - n_qkv + ch] = pc
            else:
                g0 = c0 - n_qkv - n_conv
                gate_ref[:, g0:g0 + ch] = _sigmoid(pc + b_ref[:, g0:g0 + ch])

    return pl.pallas_call(
        body, name="in_proj_fwd", grid=(s // tm,),
        in_specs=[_row_spec(tm, d), _const_spec((1, d)), _const_spec((1, 2 * d)), ANY],
        out_specs=[_row_spec(tm, n_qkv), _row_spec(tm, n_conv), _row_spec(tm, 2 * d)],
        out_shape=[jax.ShapeDtypeStruct((s, n_qkv), BF16), jax.ShapeDtypeStruct((s, n_conv), F32),
                   jax.ShapeDtypeStruct((s, 2 * d), F32)],
        scratch_shapes=[pltpu.VMEM((d, ni), BF16), pltpu.SemaphoreType.DMA((1,))],
        compiler_params=_params(1),
    )(x, g1, b_gate, w_in)


def _split_hi_lo(a):
    hi = a.astype(BF16)
    return hi, (a - hi.astype(F32)).astype(BF16)


def _log_gates(z):
    t = jnp.exp(-jnp.abs(z))
    u = 1.0 + t
    sp = jnp.log(u)
    return jnp.minimum(z, 0.0) - sp, jnp.minimum(-z, 0.0) - sp, t, u


def _attn_masks(t):
    row = lax.broadcasted_iota(jnp.int32, (t, t), 0)
    col = lax.broadcasted_iota(jnp.int32, (t, t), 1)
    causal = jnp.concatenate([col < row] * HEADS_PER_GROUP, axis=0)
    return causal, (row > col).astype(BF16), (row >= col).astype(BF16)


def _stack_heads(a):
    return jnp.concatenate([jnp.where(_head_lanes(h), a, jnp.zeros_like(a)) for h in range(HEADS_PER_GROUP)], axis=0)


def _unstack_heads(a, t):
    out = a[0:t]
    for h in range(1, HEADS_PER_GROUP):
        out = jnp.where(_head_lanes(h), a[h * t:(h + 1) * t], out)
    return out


def _while_weights_live(qi, block, carry):
    def cond(state):
        j, carry = state
        return jnp.logical_and(j < qi, jnp.max(carry[0]) >= EXP_ZERO)

    def step(state):
        j, carry = state
        return j + 1, block(qi - 1 - j, carry)

    return lax.while_loop(cond, step, (jnp.int32(0), carry))[1]


def _head_lanes(h):
    lane = lax.broadcasted_iota(jnp.int32, (1, LANES), 1)
    return (lane >= HEAD_DIM * h) & (lane < HEAD_DIM * (h + 1))


def _attn_fwd(qkv, aw, t, exchange=None):
    s = qkv.shape[0]
    groups = aw // LANES
    nq = s // t
    scale = HEAD_DIM ** -0.5
    ex = exchange or _NO_EXCHANGE

    def body(q_ref, k_ref, v_ref, *rest):
        ex_in, (o_ref,), ex_out, sems = _split_refs(rest, ex, 1)
        qi = pl.program_id(1)
        _exchange_start(ex, ex_in, ex_out, sems)
        causal, upper, _ = _attn_masks(t)
        qs = _stack_heads(q_ref[...] * scale)

        def block(kb, run, acc, diag):
            rows = pl.ds(pl.multiple_of(kb * t, t), t)
            z = _dot_nt(qs, k_ref[rows, :])
            log_b, log_keep, _, _ = _log_gates(z)
            if diag:
                log_keep = jnp.where(causal, log_keep, 0.0)
            hi, lo = _split_hi_lo(log_keep)
            between = _dot(hi, upper) + _dot(lo, upper) + run
            w = jnp.exp(log_b + between)
            if diag:
                w = jnp.where(causal, w, 0.0)
            acc = acc + _dot(w.astype(BF16), v_ref[rows, :])
            return run + jnp.sum(log_keep, axis=1, keepdims=True), acc

        rows_all = HEADS_PER_GROUP * t
        carry = block(qi, jnp.zeros((rows_all, 1), F32), jnp.zeros((rows_all, LANES), F32), True)
        _, acc = _while_weights_live(qi, lambda kb, carry: block(kb, *carry, False), carry)
        o_ref[...] = _unstack_heads(acc, t)
        _exchange_wait(ex, ex_in, ex_out, sems, groups, nq)

    return pl.pallas_call(
        body, name="attn_fwd", grid=(groups, nq),
        in_specs=[pl.BlockSpec((t, LANES), lambda g, i: (i, g)),
                  pl.BlockSpec((s, LANES), lambda g, i: (0, groups + g)),
                  pl.BlockSpec((s, LANES), lambda g, i: (0, 2 * groups + g))] + [ANY] * len(ex.arrays),
        out_specs=[pl.BlockSpec((t, LANES), lambda g, i: (i, g))] + [ANY] * len(ex.out_shapes),
        out_shape=[jax.ShapeDtypeStruct((s, aw), F32)] + ex.out_shapes,
        scratch_shapes=_exchange_sems(ex),
        compiler_params=_params(2),
    )(qkv, qkv, qkv, *ex.arrays)


def _attn_bwd(qkv, o, do, aw, t, exchange=None):
    s = qkv.shape[0]
    groups = aw // LANES
    nq = s // t
    scale = HEAD_DIM ** -0.5
    ex = exchange or _NO_EXCHANGE

    def body(q_ref, k_ref, v_ref, o_ref, do_ref, *rest):
        ex_in, (dq_ref, dk_ref, dv_ref), ex_out, (dk_acc, dv_acc, *sems) = _split_refs(rest, ex, 3)
        qi = pl.program_id(1)
        _exchange_start(ex, ex_in, ex_out, sems)

        @pl.when(qi == 0)
        def _():
            dk_acc[...] = jnp.zeros_like(dk_acc)
            dv_acc[...] = jnp.zeros_like(dv_acc)

        causal, upper, lower_incl = _attn_masks(t)
        qs = _stack_heads(q_ref[...] * scale)
        do_b = do_ref[...]
        dos = _stack_heads(do_b)
        e_total = jnp.sum(_stack_heads(do_b.astype(F32) * o_ref[...]), axis=1, keepdims=True)

        def block(kb, run, e_run, dq, diag):
            rows = pl.ds(pl.multiple_of(kb * t, t), t)
            k = k_ref[rows, :]
            v = v_ref[rows, :]
            z = _dot_nt(qs, k)
            log_b, log_keep, tt, u = _log_gates(z)
            r = 1.0 / u
            beta = jnp.where(z >= 0.0, r, tt * r)
            keep = jnp.where(z >= 0.0, tt * r, r)
            if diag:
                log_keep = jnp.where(causal, log_keep, 0.0)
            hi, lo = _split_hi_lo(log_keep)
            between = _dot(hi, upper) + _dot(lo, upper) + run
            w = jnp.exp(log_b + between)
            if diag:
                w = jnp.where(causal, w, 0.0)
            wb = w.astype(BF16)
            e = _dot_nt(dos, v) * wb.astype(F32)
            hi, lo = _split_hi_lo(e)
            e_suffix = _dot(hi, lower_incl) + _dot(lo, lower_incl) + e_run
            dz = e * keep - (e_total - e_suffix) * beta
            if diag:
                dz = jnp.where(causal, dz, 0.0)
            dzb = dz.astype(BF16)
            dk_acc[rows, :] += _dot_tn(dzb, qs)
            dv_acc[rows, :] += _dot_tn(wb, dos)
            return (run + jnp.sum(log_keep, axis=1, keepdims=True), e_suffix[:, 0:1], dq + _dot(dzb, k))

        rows_all = HEADS_PER_GROUP * t
        zero_col = jnp.zeros((rows_all, 1), F32)
        carry = block(qi, zero_col, zero_col, jnp.zeros((rows_all, LANES), F32), True)
        _, _, dq = _while_weights_live(qi, lambda kb, carry: block(kb, *carry, False), carry)
        dq_ref[...] = (_unstack_heads(dq, t) * scale).astype(BF16)

        @pl.when(qi == nq - 1)
        def _():
            dk_ref[...] = dk_acc[...].astype(BF16)
            dv_ref[...] = dv_acc[...].astype(BF16)

        _exchange_wait(ex, ex_in, ex_out, sems, groups, nq)

    blk = pl.BlockSpec((t, LANES), lambda g, i: (i, g))
    slab = pl.BlockSpec((s, LANES), lambda g, i: (0, g))
    return pl.pallas_call(
        body, name="attn_bwd", grid=(groups, nq),
        in_specs=[blk, pl.BlockSpec((s, LANES), lambda g, i: (0, groups + g)),
                  pl.BlockSpec((s, LANES), lambda g, i: (0, 2 * groups + g)), blk, blk] + [ANY] * len(ex.arrays),
        out_specs=[blk, slab, slab] + [ANY] * len(ex.out_shapes),
        out_shape=[jax.ShapeDtypeStruct((s, aw), BF16)] * 3 + ex.out_shapes,
        scratch_shapes=[pltpu.VMEM((s, LANES), F32), pltpu.VMEM((s, LANES), F32)] + _exchange_sems(ex),
        compiler_params=_params(2),
    )(qkv, qkv, qkv, o, do, *ex.arrays)


def _branches(o_b, conv, conv_prev, wconv, w_ao, w_co, cw, first):
    cb = conv[:, 0:cw]
    cm = conv[:, cw:2 * cw] * conv[:, 2 * cw:3 * cw]
    cm_prev = conv_prev[:, cw:2 * cw] * conv_prev[:, 2 * cw:3 * cw]
    cm_prev = jnp.where(first, 0.0, cm_prev)
    cv, cm1, cm2 = _conv_taps(cm, cm_prev, wconv)
    conv_in = (cb * cv).astype(BF16)
    return _dot(o_b, w_ao), _dot(conv_in, w_co), conv_in, cb, cv, cm, cm1, cm2


def _mix_fwd(x, o, conv, gate, wconv, g_post, w_ao, w_co, w_o, tm):
    s, d = x.shape
    aw, cw = w_ao.shape[0], w_co.shape[0]

    def body(x_ref, o_ref, conv_ref, prev_ref, gate_ref, wc_ref, g_ref, wao_hbm, wco_hbm, wo_hbm,
             x1_ref, mixed_ref, mixin_ref, convin_ref, wao, wco, wo, sem):
        _load_resident([(wao_hbm, wao), (wco_hbm, wco), (wo_hbm, wo)], sem)
        y_attn, y_conv, conv_in, *_ = _branches(
            o_ref[...].astype(BF16), conv_ref[...], prev_ref[...], wc_ref[...], wao[...], wco[...], cw,
            pl.program_id(0) == 0)
        mix_in = (gate_ref[:, 0:d] * y_attn + gate_ref[:, d:2 * d] * y_conv).astype(BF16)
        mixed = _dot(mix_in, wo[...])
        x1_ref[...] = x_ref[...] + mixed * _rms_scale(mixed) * g_ref[...]
        mixed_ref[...] = mixed
        mixin_ref[...] = mix_in
        convin_ref[...] = conv_in

    return pl.pallas_call(
        body, name="mix_fwd", grid=(s // tm,),
        in_specs=[_row_spec(tm, d), _row_spec(tm, aw), _row_spec(tm, 3 * cw), _prev_halo_spec(tm, 3 * cw),
                  _row_spec(tm, 2 * d), _const_spec((CONV_K, cw)), _const_spec((1, d)), ANY, ANY, ANY],
        out_specs=[_row_spec(tm, d), _row_spec(tm, d), _row_spec(tm, d), _row_spec(tm, cw)],
        out_shape=[jax.ShapeDtypeStruct((s, d), F32), jax.ShapeDtypeStruct((s, d), F32),
                   jax.ShapeDtypeStruct((s, d), BF16), jax.ShapeDtypeStruct((s, cw), BF16)],
        scratch_shapes=[pltpu.VMEM(w_ao.shape, BF16), pltpu.VMEM(w_co.shape, BF16), pltpu.VMEM(w_o.shape, BF16),
                        pltpu.SemaphoreType.DMA((3,))],
        compiler_params=_params(1),
    )(x, o, conv, conv, gate, wconv, g_post, w_ao, w_co, w_o)


def _mix_bwd(dx1, mixed, o, conv, gate, wconv, g_post, w_ao, w_co, w_o, tm):
    s, d = dx1.shape
    aw, cw = w_ao.shape[0], w_co.shape[0]

    def body(dx1_ref, mixed_ref, o_ref, conv_ref, prev_ref, gate_ref, wc_ref, g_ref, wao_hbm, wco_hbm, wo_hbm,
             dmixed_ref, dattn_ref, dconvout_ref, do_ref, dgate_ref, dcb_ref, dcv_ref, dg_ref, dbias_ref,
             wao, wco, wo, sem):
        i = pl.program_id(0)
        _load_resident([(wao_hbm, wao), (wco_hbm, wco), (wo_hbm, wo)], sem)

        @pl.when(i == 0)
        def _():
            dg_ref[...] = jnp.zeros_like(dg_ref)
            dbias_ref[...] = jnp.zeros_like(dbias_ref)

        mixed = mixed_ref[...]
        r = _rms_scale(mixed)
        mhat = mixed * r
        dn = dx1_ref[...]
        dg_ref[...] += jnp.sum(dn * mhat, axis=0, keepdims=True)
        dmixed = _rms_bwd(mhat, r, g_ref[...], dn).astype(BF16)
        dmixed_ref[...] = dmixed
        dmi = _dot_nt(dmixed, wo[...])

        y_attn, y_conv, _, cb, cv, *_ = _branches(
            o_ref[...].astype(BF16), conv_ref[...], prev_ref[...], wc_ref[...], wao[...], wco[...], cw, i == 0)
        ga = gate_ref[:, 0:d]
        gc = gate_ref[:, d:2 * d]
        dpre_a = dmi * y_attn * ga * (1.0 - ga)
        dpre_c = dmi * y_conv * gc * (1.0 - gc)
        dgate_ref[:, 0:d] = dpre_a.astype(BF16)
        dgate_ref[:, d:2 * d] = dpre_c.astype(BF16)
        dbias_ref[:, 0:d] += jnp.sum(dpre_a, axis=0, keepdims=True)
        dbias_ref[:, d:2 * d] += jnp.sum(dpre_c, axis=0, keepdims=True)

        dattn = (dmi * ga).astype(BF16)
        dattn_ref[...] = dattn
        do_ref[...] = _dot_nt(dattn, wao[...]).astype(BF16)
        dconvout = (dmi * gc).astype(BF16)
        dconvout_ref[...] = dconvout
        dconv_in = _dot_nt(dconvout, wco[...])
        dcb_ref[...] = (dconv_in * cv).astype(BF16)
        dcv_ref[...] = dconv_in * cb

    return pl.pallas_call(
        body, name="mix_bwd", grid=(s // tm,),
        in_specs=[_row_spec(tm, d), _row_spec(tm, d), _row_spec(tm, aw), _row_spec(tm, 3 * cw),
                  _prev_halo_spec(tm, 3 * cw), _row_spec(tm, 2 * d), _const_spec((CONV_K, cw)), _const_spec((1, d)),
                  ANY, ANY, ANY],
        out_specs=[_row_spec(tm, d), _row_spec(tm, d), _row_spec(tm, d), _row_spec(tm, aw), _row_spec(tm, 2 * d),
                   _row_spec(tm, cw), _row_spec(tm, cw), _const_spec((1, d)), _const_spec((1, 2 * d))],
        out_shape=[jax.ShapeDtypeStruct((s, d), BF16), jax.ShapeDtypeStruct((s, d), BF16),
                   jax.ShapeDtypeStruct((s, d), BF16), jax.ShapeDtypeStruct((s, aw), BF16),
                   jax.ShapeDtypeStruct((s, 2 * d), BF16), jax.ShapeDtypeStruct((s, cw), BF16),
                   jax.ShapeDtypeStruct((s, cw), F32), jax.ShapeDtypeStruct((1, d), F32),
                   jax.ShapeDtypeStruct((1, 2 * d), F32)],
        scratch_shapes=[pltpu.VMEM(w_ao.shape, BF16), pltpu.VMEM(w_co.shape, BF16), pltpu.VMEM(w_o.shape, BF16),
                        pltpu.SemaphoreType.DMA((3,))],
        compiler_params=_params(1),
    )(dx1, mixed, o, conv, conv, gate, wconv, g_post, w_ao, w_co, w_o)


def _mlp_ple_loss(x1, p, target, g_pre, g_post, g_ple, w_up, w_dn, w_pg, w_pp, tm):
    s, d = x1.shape
    ff = w_up.shape[1]
    pd = p.shape[1]
    fc = FF_CHUNK

    def body(x1_ref, p_ref, t_ref, gpre_ref, gpost_ref, gple_ref, wup_hbm, wdn_hbm, wpg_hbm, wpp_hbm,
             dx1_ref, h2_ref, du_ref, a_ref, df_ref, h3_ref, ds3_ref, dpp_ref, loss_ref, dgpre_ref, dgpost_ref,
             dgple_ref, wup, wdn, wpg, wpp, u_scr, sem):
        _load_resident([(wup_hbm, wup), (wdn_hbm, wdn), (wpg_hbm, wpg), (wpp_hbm, wpp)], sem)

        @pl.when(pl.program_id(0) == 0)
        def _():
            for ref in (loss_ref, dgpre_ref, dgpost_ref, dgple_ref):
                ref[...] = jnp.zeros_like(ref)

        x1v = x1_ref[...]
        r2 = _rms_scale(x1v)
        x1hat = x1v * r2
        h2 = (x1hat * gpre_ref[...]).astype(BF16)
        h2_ref[...] = h2
        f = jnp.zeros((tm, d), F32)
        for c0 in range(0, ff, fc):
            u = _dot(h2, wup[:, c0:c0 + fc])
            u_scr[:, c0:c0 + fc] = u
            a = jnp.square(jnp.maximum(u, 0.0)).astype(BF16)
            a_ref[:, c0:c0 + fc] = a
            f = f + _dot(a, wdn[c0:c0 + fc, :])
        rf = _rms_scale(f)
        fhat = f * rf
        x2 = x1v + fhat * gpost_ref[...]
        r3 = _rms_scale(x2)
        x2hat = x2 * r3
        h3 = (x2hat * gple_ref[...]).astype(BF16)
        h3_ref[...] = h3
        pg = _sigmoid(_dot(h3, wpg[...]))
        pp = _dot(p_ref[...].astype(BF16), wpp[...])
        diff = x2 + pg * pp - t_ref[...]
        loss_ref[...] += 0.5 * jnp.sum(jnp.mean(diff * diff, axis=-1, keepdims=True), axis=0, keepdims=True)

        dy = diff * (1.0 / d)
        dpp_ref[...] = (dy * pg).astype(BF16)
        ds3 = (dy * pp * pg * (1.0 - pg)).astype(BF16)
        ds3_ref[...] = ds3
        dh3 = _dot_nt(ds3, wpg[...])
        dgple_ref[...] += jnp.sum(dh3 * x2hat, axis=0, keepdims=True)
        dx2 = dy + _rms_bwd(x2hat, r3, gple_ref[...], dh3)
        dgpost_ref[...] += jnp.sum(dx2 * fhat, axis=0, keepdims=True)
        df = _rms_bwd(fhat, rf, gpost_ref[...], dx2).astype(BF16)
        df_ref[...] = df
        dh2 = jnp.zeros((tm, d), F32)
        for c0 in range(0, ff, fc):
            da = _dot_nt(df, wdn[c0:c0 + fc, :])
            du = (da * (2.0 * jnp.maximum(u_scr[:, c0:c0 + fc], 0.0))).astype(BF16)
            du_ref[:, c0:c0 + fc] = du
            dh2 = dh2 + _dot_nt(du, wup[:, c0:c0 + fc])
        dgpre_ref[...] += jnp.sum(dh2 * x1hat, axis=0, keepdims=True)
        dx1_ref[...] = dx2 + _rms_bwd(x1hat, r2, gpre_ref[...], dh2)

    vec = _const_spec((1, d))
    return pl.pallas_call(
        body, name="mlp_ple_loss", grid=(s // tm,),
        in_specs=[_row_spec(tm, d), _row_spec(tm, pd), _row_spec(tm, d), vec, vec, vec, ANY, ANY, ANY, ANY],
        out_specs=[_row_spec(tm, d), _row_spec(tm, d), _row_spec(tm, ff), _row_spec(tm, ff), _row_spec(tm, d),
                   _row_spec(tm, d), _row_spec(tm, d), _row_spec(tm, d), _const_spec((1, 1)), vec, vec, vec],
        out_shape=[jax.ShapeDtypeStruct((s, d), F32), jax.ShapeDtypeStruct((s, d), BF16),
                   jax.ShapeDtypeStruct((s, ff), BF16), jax.ShapeDtypeStruct((s, ff), BF16),
                   jax.ShapeDtypeStruct((s, d), BF16), jax.ShapeDtypeStruct((s, d), BF16),
                   jax.ShapeDtypeStruct((s, d), BF16), jax.ShapeDtypeStruct((s, d), BF16),
                   jax.ShapeDtypeStruct((1, 1), F32), jax.ShapeDtypeStruct((1, d), F32),
                   jax.ShapeDtypeStruct((1, d), F32), jax.ShapeDtypeStruct((1, d), F32)],
        scratch_shapes=[pltpu.VMEM(w_up.shape, BF16), pltpu.VMEM(w_dn.shape, BF16), pltpu.VMEM(w_pg.shape, BF16),
                        pltpu.VMEM(w_pp.shape, BF16), pltpu.VMEM((tm, ff), F32), pltpu.SemaphoreType.DMA((4,))],
        compiler_params=_params(1),
    )(x1, p, target, g_pre, g_post, g_ple, w_up, w_dn, w_pg, w_pp)


def _in_proj_bwd(x, dx1, dq, dk, dv, dcb, dcv, conv, dgate, wconv, g1, w_in, tm):
    s, d = x.shape
    aw, cw = dq.shape[1], dcb.shape[1]
    ni = w_in.shape[1]

    def body(x_ref, dx1_ref, dq_ref, dk_ref, dv_ref, dcb_ref, dcv_ref, dcvn_ref, cc_ref, cu_ref, ccp_ref, cup_ref,
             dgate_ref, wc_ref, g_ref, w_hbm, dx_ref, dproj_ref, h1_ref, dg_ref, dwc_ref, w_vmem, sem):
        i = pl.program_id(0)
        _load_resident([(w_hbm, w_vmem)], sem)

        @pl.when(i == 0)
        def _():
            dg_ref[...] = jnp.zeros_like(dg_ref)
            dwc_ref[...] = jnp.zeros_like(dwc_ref)

        wc = wc_ref[...]
        cc = cc_ref[...]
        cu = cu_ref[...]
        cm = cc * cu
        cm_prev = jnp.where(i == 0, 0.0, ccp_ref[...] * cup_ref[...])
        _, cm1, cm2 = _conv_taps(cm, cm_prev, wc)
        dcv_cur = dcv_ref[...]
        dcv_next = jnp.where(i == pl.num_programs(0) - 1, 0.0, dcvn_ref[...])
        dcm = (wc[2:3, :] * dcv_cur + wc[1:2, :] * _shift_up(dcv_cur, dcv_next, 1)
               + wc[0:1, :] * _shift_up(dcv_cur, dcv_next, 2))
        for tap, shifted in enumerate((cm2, cm1, cm)):
            dwc_ref[tap:tap + 1, :] += jnp.sum(dcv_cur * shifted, axis=0, keepdims=True)

        pieces = [(dq_ref[...], aw), (dk_ref[...], aw), (dv_ref[...], aw), (dcb_ref[...], cw),
                  ((dcm * cu).astype(BF16), cw), ((dcm * cc).astype(BF16), cw), (dgate_ref[...], 2 * d)]
        dh = jnp.zeros((tm, d), F32)
        c0 = 0
        for piece, width in pieces:
            dproj_ref[:, c0:c0 + width] = piece
            dh = dh + _dot_nt(piece, w_vmem[:, c0:c0 + width])
            c0 += width

        xv = x_ref[...]
        r = _rms_scale(xv)
        xhat = xv * r
        h1_ref[...] = (xhat * g_ref[...]).astype(BF16)
        dg_ref[...] += jnp.sum(dh * xhat, axis=0, keepdims=True)
        dx_ref[...] = dx1_ref[...] + _rms_bwd(xhat, r, g_ref[...], dh)

    return pl.pallas_call(
        body, name="in_proj_bwd", grid=(s // tm,),
        in_specs=[_row_spec(tm, d), _row_spec(tm, d), _row_spec(tm, aw), _row_spec(tm, aw), _row_spec(tm, aw),
                  _row_spec(tm, cw), _row_spec(tm, cw), _next_halo_spec(tm, cw, s), _col_spec(tm, cw, 1),
                  _col_spec(tm, cw, 2), _prev_halo_spec(tm, cw, 1), _prev_halo_spec(tm, cw, 2), _row_spec(tm, 2 * d),
                  _const_spec((CONV_K, cw)), _const_spec((1, d)), ANY],
        out_specs=[_row_spec(tm, d), _row_spec(tm, ni), _row_spec(tm, d), _const_spec((1, d)),
                   _const_spec((CONV_K, cw))],
        out_shape=[jax.ShapeDtypeStruct((s, d), F32), jax.ShapeDtypeStruct((s, ni), BF16),
                   jax.ShapeDtypeStruct((s, d), BF16), jax.ShapeDtypeStruct((1, d), F32),
                   jax.ShapeDtypeStruct((CONV_K, cw), F32)],
        scratch_shapes=[pltpu.VMEM((d, ni), BF16), pltpu.SemaphoreType.DMA((1,))],
        compiler_params=_params(1),
    )(x, dx1, dq, dk, dv, dcb, dcv, dcv, conv, conv, conv, conv, dgate, wconv, g1, w_in)


def _weight_grad(a, b, name):
    s, m = a.shape
    n = b.shape[1]
    tm, tn, tk = min(m, DW_TILE), min(n, DW_TILE), min(s, DW_TOKENS)
    nk = s // tk

    def body(a_ref, b_ref, o_ref, acc):
        k = pl.program_id(2)

        @pl.when(k == 0)
        def _():
            acc[...] = jnp.zeros_like(acc)

        acc[...] += _dot_tn(a_ref[...].astype(BF16), b_ref[...].astype(BF16))

        @pl.when(k == nk - 1)
        def _():
            o_ref[...] = acc[...].astype(BF16)

    return pl.pallas_call(
        body, name=name, grid=(m // tm, n // tn, nk),
        in_specs=[pl.BlockSpec((tk, tm), lambda i, j, k: (k, i)), pl.BlockSpec((tk, tn), lambda i, j, k: (k, j))],
        out_specs=pl.BlockSpec((tm, tn), lambda i, j, k: (i, j)),
        out_shape=jax.ShapeDtypeStruct((m, n), BF16),
        scratch_shapes=[pltpu.VMEM((tm, tn), F32)],
        compiler_params=_params(3),
    )(a, b)


def _mesh_position():
    return tuple(lax.axis_index(a) for a in MESH_AXES)


def _peer(me, k):
    bits = ((k >> 2) & 1, (k >> 1) & 1, k & 1)
    pos = tuple(1 - m if b else m for m, b in zip(me, bits))
    return pos, 4 * pos[0] + 2 * pos[1] + pos[2]


class _Exchange:
    def __init__(self, arrays, out_shapes, src, dst):
        self.arrays, self.out_shapes, self.src, self.dst = list(arrays), list(out_shapes), src, dst


_NO_EXCHANGE = _Exchange([], [], None, None)


def _exchange_sems(ex):
    n = len(ex.arrays)
    if n == 0:
        return []
    return [pltpu.SemaphoreType.DMA((n, N_DEV - 1)), pltpu.SemaphoreType.DMA((n, N_DEV - 1)),
            pltpu.SemaphoreType.DMA((n,))]


def _split_refs(rest, ex, n_own_outs):
    n_in, n_out = len(ex.arrays), len(ex.out_shapes)
    ex_in, rest = rest[:n_in], rest[n_in:]
    own, rest = rest[:n_own_outs], rest[n_own_outs:]
    return ex_in, own, rest[:n_out], rest[n_out:]


def _exchange_copies(ex, in_refs, out_refs, sems):
    send_sems, recv_sems, local_sems = sems
    me = _mesh_position()
    mine = 4 * me[0] + 2 * me[1] + me[2]
    copies = []
    for w in range(len(ex.arrays)):
        landing = ex.dst(w, out_refs, mine)
        copies.append(pltpu.make_async_copy(ex.src(w, in_refs, mine), landing, local_sems.at[w]))
        for k in range(1, N_DEV):
            peer, peer_idx = _peer(me, k)
            copies.append(pltpu.make_async_remote_copy(
                src_ref=ex.src(w, in_refs, peer_idx), dst_ref=landing, send_sem=send_sems.at[w, k - 1],
                recv_sem=recv_sems.at[w, k - 1], device_id=peer, device_id_type=pl.DeviceIdType.MESH))
    return copies


def _exchange_start(ex, in_refs, out_refs, sems):
    if ex.arrays:
        @pl.when(jnp.logical_and(pl.program_id(0) == 0, pl.program_id(1) == 0))
        def _():
            for cp in _exchange_copies(ex, in_refs, out_refs, sems):
                cp.start()


def _exchange_wait(ex, in_refs, out_refs, sems, n0, n1):
    if ex.arrays:
        @pl.when(jnp.logical_and(pl.program_id(0) == n0 - 1, pl.program_id(1) == n1 - 1))
        def _():
            for cp in _exchange_copies(ex, in_refs, out_refs, sems):
                cp.wait()


def _shard_block(ref, shard_shape, by_col, idx):
    r, c = shard_shape
    if by_col:
        return ref.at[:, pl.ds(pl.multiple_of(idx * c, LANES), c)]
    return ref.at[pl.ds(pl.multiple_of(idx * r, 16), r), :]


def _full_shape(shard_shape, by_col):
    r, c = shard_shape
    return (r, N_DEV * c) if by_col else (N_DEV * r, c)


def _gather_exchange(shards, col_sharded):
    shapes = [a.shape for a in shards]
    return _Exchange(
        shards, [jax.ShapeDtypeStruct(_full_shape(sh, bc), a.dtype) for a, sh, bc in zip(shards, shapes, col_sharded)],
        lambda w, refs, idx: refs[w],
        lambda w, refs, mine: _shard_block(refs[w], shapes[w], col_sharded[w], mine))


def _scatter_exchange(grads, col_sharded):
    shapes = []
    for g, by_col in zip(grads, col_sharded):
        r, c = g.shape
        shapes.append((r, c // N_DEV) if by_col else (r // N_DEV, c))
    return _Exchange(
        grads, [jax.ShapeDtypeStruct((N_DEV,) + sh, g.dtype) for g, sh in zip(grads, shapes)],
        lambda w, refs, idx: _shard_block(refs[w], shapes[w], col_sharded[w], idx),
        lambda w, refs, mine: refs[w].at[mine])


def _broadcast_exchange(arrays):
    return _Exchange(arrays, [jax.ShapeDtypeStruct((N_DEV,) + a.shape, a.dtype) for a in arrays],
                     lambda w, refs, idx: refs[w], lambda w, refs, mine: refs[w].at[mine])


def _join(*exs):
    arrays, shapes, owner = [], [], []
    for e in exs:
        for w in range(len(e.arrays)):
            owner.append((e, w, len(arrays), len(shapes)))
        arrays += e.arrays
        shapes += e.out_shapes

    def src(w, refs, idx):
        e, w0, i0, _ = owner[w]
        return e.src(w0, refs[i0:i0 + len(e.arrays)], idx)

    def dst(w, refs, mine):
        e, w0, _, o0 = owner[w]
        return e.dst(w0, refs[o0:o0 + len(e.out_shapes)], mine)

    return _Exchange(arrays, shapes, src, dst)


def _exchange_call(ex, name):
    n_in = len(ex.arrays)

    def body(*refs):
        in_refs, _, out_refs, sems = _split_refs(refs, ex, 0)
        copies = _exchange_copies(ex, in_refs, out_refs, sems)
        for cp in copies:
            cp.start()
        for cp in copies:
            cp.wait()

    return pl.pallas_call(
        body, name=name, in_specs=[ANY] * n_in, out_specs=[ANY] * len(ex.out_shapes), out_shape=ex.out_shapes,
        scratch_shapes=_exchange_sems(ex), compiler_params=pltpu.CompilerParams(vmem_limit_bytes=VMEM_LIMIT),
    )(*ex.arrays)


def _to_bf16(arrays):
    def body(*refs):
        for src, dst in zip(refs[:len(arrays)], refs[len(arrays):]):
            dst[...] = src[...].astype(BF16)

    vmem = pl.BlockSpec(memory_space=pltpu.VMEM)
    return pl.pallas_call(
        body, name="weights_to_bf16", in_specs=[vmem] * len(arrays), out_specs=[vmem] * len(arrays),
        out_shape=[jax.ShapeDtypeStruct(a.shape, BF16) for a in arrays],
        compiler_params=pltpu.CompilerParams(vmem_limit_bytes=VMEM_LIMIT),
    )(*arrays)


def _adamw(w, g, m, v):
    m = ADAM_B1 * m + (1.0 - ADAM_B1) * g
    v = ADAM_B2 * v + (1.0 - ADAM_B2) * jnp.square(g)
    m_hat = m / (1.0 - ADAM_B1 ** ADAM_STEP)
    v_hat = v / (1.0 - ADAM_B2 ** ADAM_STEP)
    delta = -ADAM_LR * (m_hat / (jnp.sqrt(v_hat) + ADAM_EPS) + ADAM_WD * w)
    return delta, m, v


def _sum_and_adamw(parts, w, m, v, name):
    r, c = w.shape
    tr = min(r, 256)

    def body(p_ref, w_ref, m_ref, v_ref, g_out, d_out, m_out, v_out):
        g = p_ref[0].astype(F32)
        for dev in range(1, N_DEV):
            g = g + p_ref[dev].astype(F32)
        g_out[...] = g
        d_out[...], m_out[...], v_out[...] = _adamw(w_ref[...], g, m_ref[...], v_ref[...])

    blk = pl.BlockSpec((tr, c), lambda i: (i, 0))
    return pl.pallas_call(
        body, name=name, grid=(r // tr,),
        in_specs=[pl.BlockSpec((N_DEV, tr, c), lambda i: (0, i, 0)), blk, blk, blk],
        out_specs=[blk] * 4, out_shape=[jax.ShapeDtypeStruct((r, c), F32)] * 4,
        compiler_params=_params(1),
    )(parts, w, m, v)


BIG = ("w_in", "w_attn_out", "w_conv_out", "w_o", "w_up", "w_down", "w_ple_gate", "w_ple_proj")
COL_SHARDED = {"w_in": True, "w_attn_out": True, "w_conv_out": True, "w_o": False, "w_up": True, "w_down": False,
               "w_ple_gate": False, "w_ple_proj": True}
SMALL = ("g_pre_mix", "b_gate", "g_post_mix", "g_pre_mlp", "g_post_mlp", "g_ple")


REST = BIG[1:]


def _local_grads(x, p, target, small, wconv, full, aw, cw, tm, t, gather_rest=None, scatter_rest=None):
    full = dict(full)
    qkv, conv, gate = _in_proj_fwd(x, small["g_pre_mix"], small["b_gate"], full["w_in"], aw, cw, tm)
    o, *rest = _attn_fwd(qkv, aw, t, gather_rest)
    full.update(zip(REST, rest))
    x1, mixed, mix_in, conv_in = _mix_fwd(x, o, conv, gate, wconv, small["g_post_mix"], full["w_attn_out"],
                                          full["w_conv_out"], full["w_o"], tm)
    (dx1, h2, du, a, df, h3, ds3, dpp, loss, dg_pre_mlp, dg_post_mlp, dg_ple) = _mlp_ple_loss(
        x1, p, target, small["g_pre_mlp"], small["g_post_mlp"], small["g_ple"], full["w_up"], full["w_down"],
        full["w_ple_gate"], full["w_ple_proj"], tm)
    big = {"w_up": _weight_grad(h2, du, "dw_up"), "w_down": _weight_grad(a, df, "dw_down"),
           "w_ple_gate": _weight_grad(h3, ds3, "dw_ple_gate"), "w_ple_proj": _weight_grad(p, dpp, "dw_ple_proj")}
    (dmixed, dattn, dconvout, do, dgate, dcb, dcv, dg_post_mix, db_gate) = _mix_bwd(
        dx1, mixed, o, conv, gate, wconv, small["g_post_mix"], full["w_attn_out"], full["w_conv_out"], full["w_o"],
        tm)
    big.update({"w_attn_out": _weight_grad(o, dattn, "dw_attn_out"),
                "w_conv_out": _weight_grad(conv_in, dconvout, "dw_conv_out"),
                "w_o": _weight_grad(mix_in, dmixed, "dw_o")})
    dq, dk, dv, *scattered = _attn_bwd(qkv, o, do, aw, t, scatter_rest and scatter_rest([big[n] for n in REST]))
    dx, dproj, h1, dg_pre_mix, dwconv = _in_proj_bwd(x, dx1, dq, dk, dv, dcb, dcv, conv, dgate, wconv,
                                                    small["g_pre_mix"], full["w_in"], tm)
    big["w_in"] = _weight_grad(h1, dproj, "dw_in")
    small_grads = {"g_pre_mix": dg_pre_mix, "b_gate": db_gate, "g_post_mix": dg_post_mix, "g_pre_mlp": dg_pre_mlp,
                   "g_post_mlp": dg_post_mlp, "g_ple": dg_ple, "w_conv": dwconv}
    return loss[0, 0], dx, big, small_grads, scattered


def _pack_small(vals, d):
    parts = []
    for a in vals:
        a = jnp.pad(a.reshape(-1), (0, -a.size % d)).reshape(-1, d)
        parts.append(jnp.pad(a, ((0, HALO - a.shape[0]), (0, 0))))
    return jnp.concatenate(parts, axis=0)


def _unpack_small(pack, shapes, d):
    out = []
    for i, shp in enumerate(shapes):
        n = 1
        for v in shp:
            n *= v
        rows = -(-n // d)
        out.append(pack[i * HALO:i * HALO + rows].reshape(-1)[:n].reshape(shp))
    return out


def kernel(x, p, g_pre_mix, w_in, b_gate, w_conv, w_attn_out, w_conv_out, w_o, g_post_mix, g_pre_mlp, w_up, w_down, g_post_mlp, g_ple, w_ple_gate, w_ple_proj, loss_target, m_g_pre_mix, m_w_in, m_b_gate, m_w_conv, m_w_attn_out, m_w_conv_out, m_w_o, m_g_post_mix, m_g_pre_mlp, m_w_up, m_w_down, m_g_post_mlp, m_g_ple, m_w_ple_gate, m_w_ple_proj, v_g_pre_mix, v_w_in, v_b_gate, v_w_conv, v_w_attn_out, v_w_conv_out, v_w_o, v_g_post_mix, v_g_pre_mlp, v_w_up, v_w_down, v_g_post_mlp, v_g_ple, v_w_ple_gate, v_w_ple_proj):
    given = dict(locals())
    order = ["g_pre_mix", "w_in", "b_gate", "w_conv", "w_attn_out", "w_conv_out", "w_o", "g_post_mix", "g_pre_mlp",
             "w_up", "w_down", "g_post_mlp", "g_ple", "w_ple_gate", "w_ple_proj"]
    d = x.shape[-1]
    me = 4 * lax.axis_index("x") + 2 * lax.axis_index("y") + lax.axis_index("c")

    col = [COL_SHARDED[n] for n in BIG]
    shards = _to_bf16([given[n][0] for n in BIG])
    cw_shard = w_conv.shape[-1]
    conv_tile = jnp.pad(w_conv[0], ((0, HALO - CONV_K), (0, LANES - cw_shard)))
    w_in_full, conv_g = _exchange_call(
        _join(_gather_exchange(shards[:1], col[:1]), _broadcast_exchange([conv_tile])), "gather_w_in")
    wconv = jnp.concatenate([conv_g[dev, :CONV_K, :cw_shard] for dev in range(N_DEV)], axis=1)

    small = {n: given[n] for n in SMALL}
    loss, dx, big_grads, small_grads, parts_rest = _local_grads(
        x[0], p[0, 0], loss_target[0], small, wconv, {"w_in": w_in_full}, w_attn_out.shape[1], w_conv_out.shape[1],
        ROW_BLOCK, ATTN_BLOCK,
        _gather_exchange(shards[1:], col[1:]), lambda grads: _scatter_exchange(grads, col[1:]))
    loss = lax.psum(loss, MESH_AXES)

    small_names = list(SMALL) + ["w_conv"]
    pack = _pack_small([small_grads[n] for n in small_names], d)
    part_in, packs = _exchange_call(
        _join(_scatter_exchange([big_grads["w_in"]], col[:1]), _broadcast_exchange([pack])), "scatter_dw_in")
    parts = [part_in] + list(parts_rest)

    grads, deltas, new_m, new_v = {}, {}, {}, {}
    for n, part in zip(BIG, parts):
        grads[n], deltas[n], new_m[n], new_v[n] = (
            a[None] for a in _sum_and_adamw(part, given[n][0], given["m_" + n][0], given["v_" + n][0], "adamw_" + n))

    full_conv = lambda a: lax.dynamic_update_slice(jnp.zeros((CONV_K, N_DEV * cw_shard), F32), a[0],
                                                   (jnp.int32(0), me * cw_shard))
    state = [_pack_small([given[pre + n] for n in SMALL] + [full_conv(given[pre + "w_conv"])], d)
             for pre in ("", "m_", "v_")]
    outs = _sum_and_adamw(packs, *state, "adamw_small")
    shapes = [given[n].shape for n in SMALL] + [(CONV_K, N_DEV * cw_shard)]
    for res, dst in zip(outs, (grads, deltas, new_m, new_v)):
        for n, a in zip(small_names, _unpack_small(res, shapes, d)):
            dst[n] = (lax.dynamic_slice(a, (jnp.int32(0), me * cw_shard), (CONV_K, cw_shard))[None]
                      if n == "w_conv" else a)

    return (loss, dx[None], *[grads[n] for n in order], *[deltas[n] for n in order],
            *[new_m[n] for n in order], *[new_v[n] for n in order])
```

```python
import jax
import jax.numpy as jnp
from jax import lax
from jax.experimental import pallas as pl
from jax.experimental.pallas import tpu as pltpu

F32 = jnp.float32
BF16 = jnp.bfloat16
RMS_EPS = 1e-6
N_DEV = 8
MESH_AXES = ("x", "y", "c")
LANES = 128
HEAD_DIM = 64
HEADS_PER_GROUP = LANES // HEAD_DIM
CONV_K = 3
HALO = 8
VMEM_LIMIT = 56 * 1024 * 1024
EXP2_ZERO = -150.0
LOG2_E = 1.4426950408889634

ADAM_LR = 0.001
ADAM_B1 = 0.9
ADAM_B2 = 0.999
ADAM_EPS = 1e-08
ADAM_WD = 0.01
ADAM_STEP = 10

ROW_BLOCK = 256
ATTN_BLOCK = 256
DW_TOKENS = 2048
DW_TILE = 1024
FF_CHUNK = 1024
PROJ_CHUNK = 512


def _dot(a, b):
    return lax.dot_general(a, b, (((1,), (0,)), ((), ())), preferred_element_type=F32)


def _dot_nt(a, b):
    return lax.dot_general(a, b, (((1,), (1,)), ((), ())), preferred_element_type=F32)


def _dot_tn(a, b):
    return lax.dot_general(a, b, (((0,), (0,)), ((), ())), preferred_element_type=F32)


def _sigmoid(z):
    return 1.0 / (1.0 + jnp.exp(-z))


def _rms_scale(x):
    return lax.rsqrt(jnp.mean(x * x, axis=-1, keepdims=True) + RMS_EPS)


def _rms_bwd(xhat, r, g, dy):
    gd = dy * g
    return r * (gd - xhat * jnp.mean(gd * xhat, axis=-1, keepdims=True))


def _params(n_axes, **kw):
    return pltpu.CompilerParams(dimension_semantics=("arbitrary",) * n_axes, vmem_limit_bytes=VMEM_LIMIT, **kw)


def _load_resident(pairs, sem):
    @pl.when(pl.program_id(0) == 0)
    def _():
        copies = [pltpu.make_async_copy(src, dst, sem.at[i]) for i, (src, dst) in enumerate(pairs)]
        for cp in copies:
            cp.start()
        for cp in copies:
            cp.wait()


def _row_spec(tm, width):
    return pl.BlockSpec((tm, width), lambda i: (i, 0))


def _col_spec(tm, width, col):
    return pl.BlockSpec((tm, width), lambda i: (i, col))


def _prev_halo_spec(tm, width, col=0):
    per = tm // HALO
    return pl.BlockSpec((HALO, width), lambda i: (jnp.maximum(i * per - 1, 0), col))


def _next_halo_spec(tm, width, n_rows):
    per = tm // HALO
    last = n_rows // HALO - 1
    return pl.BlockSpec((HALO, width), lambda i: (jnp.minimum((i + 1) * per, last), 0))


def _const_spec(shape):
    return pl.BlockSpec(shape, lambda i: (0,) * len(shape))


ANY = pl.BlockSpec(memory_space=pl.ANY)


def _shift_down(cur, prev, n):
    rows = lax.broadcasted_iota(jnp.int32, cur.shape, 0)
    out = pltpu.roll(cur, n, 0)
    for j in range(n):
        out = jnp.where(rows == j, prev[HALO - n + j:HALO - n + j + 1, :], out)
    return out


def _shift_up(cur, nxt, n):
    tm = cur.shape[0]
    rows = lax.broadcasted_iota(jnp.int32, cur.shape, 0)
    out = pltpu.roll(cur, tm - n, 0)
    for j in range(n):
        out = jnp.where(rows == tm - n + j, nxt[j:j + 1, :], out)
    return out


def _conv_taps(cm, cm_prev, wconv):
    cm1 = _shift_down(cm, cm_prev, 1)
    cm2 = _shift_down(cm, cm_prev, 2)
    cv = wconv[2:3, :] * cm + wconv[1:2, :] * cm1 + wconv[0:1, :] * cm2
    return cv, cm1, cm2


def _in_proj_fwd(x, g1, b_gate, w_in, aw, cw, tm):
    s, d = x.shape
    ni = w_in.shape[1]
    n_qkv, n_conv = 3 * aw, 3 * cw
    ch = PROJ_CHUNK

    def body(x_ref, g_ref, b_ref, w_hbm, qkv_ref, conv_ref, gate_ref, w_vmem, sem):
        _load_resident([(w_hbm, w_vmem)], sem)
        xv = x_ref[...]
        h = (xv * _rms_scale(xv) * g_ref[...]).astype(BF16)
        for c0 in range(0, ni, ch):
            pc = _dot(h, w_vmem[:, c0:c0 + ch])
            if c0 < n_qkv:
                qkv_ref[:, c0:c0 + ch] = pc.astype(BF16)
            elif c0 < n_qkv + n_conv:
                conv_ref[:, c0 - n_qkv:c0 - n_qkv + ch] = pc
            else:
                g0 = c0 - n_qkv - n_conv
                gate_ref[:, g0:g0 + ch] = _sigmoid(pc + b_ref[:, g0:g0 + ch])

    return pl.pallas_call(
        body, name="in_proj_fwd", grid=(s // tm,),
        in_specs=[_row_spec(tm, d), _const_spec((1, d)), _const_spec((1, 2 * d)), ANY],
        out_specs=[_row_spec(tm, n_qkv), _row_spec(tm, n_conv), _row_spec(tm, 2 * d)],
        out_shape=[jax.ShapeDtypeStruct((s, n_qkv), BF16), jax.ShapeDtypeStruct((s, n_conv), F32),
                   jax.ShapeDtypeStruct((s, 2 * d), F32)],
        scratch_shapes=[pltpu.VMEM((d, ni), BF16), pltpu.SemaphoreType.DMA((1,))],
        compiler_params=_params(1),
    )(x, g1, b_gate, w_in)


def _split_hi_lo(a):
    hi = a.astype(BF16)
    return hi, (a - hi.astype(F32)).astype(BF16)


def _log2_gates(z):
    z2 = z * LOG2_E
    nz2 = -z2
    log_keep = jnp.minimum(nz2, 0.0) - jnp.log2(1.0 + jnp.exp2(jnp.minimum(z2, nz2)))
    return log_keep + z2, log_keep


def _attn_masks(t):
    row = lax.broadcasted_iota(jnp.int32, (t, t), 0)
    col = lax.broadcasted_iota(jnp.int32, (t, t), 1)
    return col < row, (row > col).astype(BF16), (row >= col).astype(BF16)


def _per_head(a):
    return [jnp.where(_head_lanes(h), a, jnp.zeros_like(a)) for h in range(HEADS_PER_GROUP)]


def _merge_heads(parts):
    out = parts[0]
    for h in range(1, HEADS_PER_GROUP):
        out = jnp.where(_head_lanes(h), parts[h], out)
    return out


def _while_weights_live(qi, block, carry):
    def cond(state):
        j, carry = state
        live = jnp.max(carry[0][0])
        for run in carry[0][1:]:
            live = jnp.maximum(live, jnp.max(run))
        return jnp.logical_and(j < qi, live >= EXP2_ZERO)

    def step(state):
        j, carry = state
        return j + 1, block(qi - 1 - j, carry)

    return lax.while_loop(cond, step, (jnp.int32(0), carry))[1]


def _head_lanes(h):
    lane = lax.broadcasted_iota(jnp.int32, (1, LANES), 1)
    return (lane >= HEAD_DIM * h) & (lane < HEAD_DIM * (h + 1))


def _attn_fwd(qkv, aw, t, exchange=None):
    s = qkv.shape[0]
    groups = aw // LANES
    nq = s // t
    scale = HEAD_DIM ** -0.5
    ex = exchange or _NO_EXCHANGE

    def body(q_ref, k_ref, v_ref, *rest):
        ex_in, (o_ref,), ex_out, sems = _split_refs(rest, ex, 1)
        qi = pl.program_id(1)
        _exchange_start(ex, ex_in, ex_out, sems)
        causal, upper, _ = _attn_masks(t)
        qs = _per_head(q_ref[...] * scale)
        heads = range(HEADS_PER_GROUP)

        def block(kb, runs, accs, diag):
            rows = pl.ds(pl.multiple_of(kb * t, t), t)
            k = k_ref[rows, :]
            v = v_ref[rows, :]
            zs = [_dot_nt(qs[h], k) for h in heads]
            gates, sums = [], []
            for h in heads:
                log_b, log_keep = _log2_gates(zs[h])
                if diag:
                    log_keep = jnp.where(causal, log_keep, 0.0)
                hi, lo = _split_hi_lo(log_keep)
                gates.append((log_b, log_keep))
                sums.append(_dot(hi, upper) + _dot(lo, upper))
            new_runs, new_accs = [], []
            for h in heads:
                log_b, log_keep = gates[h]
                w = jnp.exp2(log_b + sums[h] + runs[h])
                if diag:
                    w = jnp.where(causal, w, 0.0)
                new_accs.append(accs[h] + _dot(w.astype(BF16), v))
                new_runs.append(runs[h] + jnp.sum(log_keep, axis=1, keepdims=True))
            return tuple(new_runs), tuple(new_accs)

        carry = block(qi, [jnp.zeros((t, 1), F32)] * len(heads), [jnp.zeros((t, LANES), F32)] * len(heads), True)
        _, accs = _while_weights_live(qi, lambda kb, carry: block(kb, *carry, False), carry)
        o_ref[...] = _merge_heads(accs)
        _exchange_wait(ex, ex_in, ex_out, sems, groups, nq)

    return pl.pallas_call(
        body, name="attn_fwd", grid=(groups, nq),
        in_specs=[pl.BlockSpec((t, LANES), lambda g, i: (i, g)),
                  pl.BlockSpec((s, LANES), lambda g, i: (0, groups + g)),
                  pl.BlockSpec((s, LANES), lambda g, i: (0, 2 * groups + g))] + [ANY] * len(ex.arrays),
        out_specs=[pl.BlockSpec((t, LANES), lambda g, i: (i, g))] + [ANY] * len(ex.out_shapes),
        out_shape=[jax.ShapeDtypeStruct((s, aw), F32)] + ex.out_shapes,
        scratch_shapes=_exchange_sems(ex),
        compiler_params=_params(2),
    )(qkv, qkv, qkv, *ex.arrays)


def _attn_bwd(qkv, o, do, aw, t, exchange=None):
    s = qkv.shape[0]
    groups = aw // LANES
    nq = s // t
    scale = HEAD_DIM ** -0.5
    ex = exchange or _NO_EXCHANGE

    def body(q_ref, k_ref, v_ref, o_ref, do_ref, *rest):
        ex_in, (dq_ref, dk_ref, dv_ref), ex_out, (dk_acc, dv_acc, *sems) = _split_refs(rest, ex, 3)
        qi = pl.program_id(1)
        _exchange_start(ex, ex_in, ex_out, sems)

        @pl.when(qi == 0)
        def _():
            dk_acc[...] = jnp.zeros_like(dk_acc)
            dv_acc[...] = jnp.zeros_like(dv_acc)

        causal, upper, lower_incl = _attn_masks(t)
        q = q_ref[...] * scale
        do_b = do_ref[...]
        qs = _per_head(q)
        dos = _per_head(do_b)
        qs_all = jnp.concatenate(qs, axis=0)
        dos_all = jnp.concatenate(dos, axis=0)
        e_totals = [jnp.sum(part, axis=1, keepdims=True) for part in _per_head(do_b.astype(F32) * o_ref[...])]
        heads = range(HEADS_PER_GROUP)

        def block(kb, runs, e_runs, dqs, diag):
            rows = pl.ds(pl.multiple_of(kb * t, t), t)
            k = k_ref[rows, :]
            v = v_ref[rows, :]
            zs = [_dot_nt(qs[h], k) for h in heads]
            dws = [_dot_nt(dos[h], v) for h in heads]
            gates, sums = [], []
            for h in heads:
                log_b, log_keep = _log2_gates(zs[h])
                beta = jnp.exp2(log_b)
                keep = jnp.exp2(log_keep)
                if diag:
                    log_keep = jnp.where(causal, log_keep, 0.0)
                hi, lo = _split_hi_lo(log_keep)
                gates.append((log_b, log_keep, beta, keep))
                sums.append(_dot(hi, upper) + _dot(lo, upper))
            es, wbs, e_suffixes = [], [], []
            for h in heads:
                w = jnp.exp2(gates[h][0] + sums[h] + runs[h])
                if diag:
                    w = jnp.where(causal, w, 0.0)
                wb = w.astype(BF16)
                e = dws[h] * wb.astype(F32)
                hi, lo = _split_hi_lo(e)
                es.append(e)
                wbs.append(wb)
                e_suffixes.append(_dot(hi, lower_incl) + _dot(lo, lower_incl) + e_runs[h])
            new_dqs, dzbs = [], []
            for h in heads:
                _, _, beta, keep = gates[h]
                dz = es[h] * keep - (e_totals[h] - e_suffixes[h]) * beta
                if diag:
                    dz = jnp.where(causal, dz, 0.0)
                dzb = dz.astype(BF16)
                dzbs.append(dzb)
                new_dqs.append(dqs[h] + _dot(dzb, k))
            dk_acc[rows, :] += _dot_tn(jnp.concatenate(dzbs, axis=0), qs_all)
            dv_acc[rows, :] += _dot_tn(jnp.concatenate(wbs, axis=0), dos_all)
            new_runs = tuple(runs[h] + jnp.sum(gates[h][1], axis=1, keepdims=True) for h in heads)
            return new_runs, tuple(e_suffixes[h][:, 0:1] for h in heads), tuple(new_dqs)

        zero_cols = [jnp.zeros((t, 1), F32)] * len(heads)
        carry = block(qi, zero_cols, zero_cols, [jnp.zeros((t, LANES), F32)] * len(heads), True)
        _, _, dqs = _while_weights_live(qi, lambda kb, carry: block(kb, *carry, False), carry)
        dq_ref[...] = (_merge_heads(dqs) * scale).astype(BF16)

        @pl.when(qi == nq - 1)
        def _():
            dk_ref[...] = dk_acc[...].astype(BF16)
            dv_ref[...] = dv_acc[...].astype(BF16)

        _exchange_wait(ex, ex_in, ex_out, sems, groups, nq)

    blk = pl.BlockSpec((t, LANES), lambda g, i: (i, g))
    slab = pl.BlockSpec((s, LANES), lambda g, i: (0, g))
    return pl.pallas_call(
        body, name="attn_bwd", grid=(groups, nq),
        in_specs=[blk, pl.BlockSpec((s, LANES), lambda g, i: (0, groups + g)),
                  pl.BlockSpec((s, LANES), lambda g, i: (0, 2 * groups + g)), blk, blk] + [ANY] * len(ex.arrays),
        out_specs=[blk, slab, slab] + [ANY] * len(ex.out_shapes),
        out_shape=[jax.ShapeDtypeStruct((s, aw), BF16)] * 3 + ex.out_shapes,
        scratch_shapes=[pltpu.VMEM((s, LANES), F32), pltpu.VMEM((s, LANES), F32)] + _exchange_sems(ex),
        compiler_params=_params(2),
    )(qkv, qkv, qkv, o, do, *ex.arrays)


def _branches(o_b, conv, conv_prev, wconv, w_ao, w_co, cw, first):
    cb = conv[:, 0:cw]
    cm = conv[:, cw:2 * cw] * conv[:, 2 * cw:3 * cw]
    cm_prev = conv_prev[:, cw:2 * cw] * conv_prev[:, 2 * cw:3 * cw]
    cm_prev = jnp.where(first, 0.0, cm_prev)
    cv, cm1, cm2 = _conv_taps(cm, cm_prev, wconv)
    conv_in = (cb * cv).astype(BF16)
    return _dot(o_b, w_ao), _dot(conv_in, w_co), conv_in, cb, cv, cm, cm1, cm2


def _mix_fwd(x, o, conv, gate, wconv, g_post, w_ao, w_co, w_o, tm):
    s, d = x.shape
    aw, cw = w_ao.shape[0], w_co.shape[0]

    def body(x_ref, o_ref, conv_ref, prev_ref, gate_ref, wc_ref, g_ref, wao_hbm, wco_hbm, wo_hbm,
             x1_ref, mixed_ref, mixin_ref, convin_ref, wao, wco, wo, sem):
        _load_resident([(wao_hbm, wao), (wco_hbm, wco), (wo_hbm, wo)], sem)
        y_attn, y_conv, conv_in, *_ = _branches(
            o_ref[...].astype(BF16), conv_ref[...], prev_ref[...], wc_ref[...], wao[...], wco[...], cw,
            pl.program_id(0) == 0)
        mix_in = (gate_ref[:, 0:d] * y_attn + gate_ref[:, d:2 * d] * y_conv).astype(BF16)
        mixed = _dot(mix_in, wo[...])
        x1_ref[...] = x_ref[...] + mixed * _rms_scale(mixed) * g_ref[...]
        mixed_ref[...] = mixed
        mixin_ref[...] = mix_in
        convin_ref[...] = conv_in

    return pl.pallas_call(
        body, name="mix_fwd", grid=(s // tm,),
        in_specs=[_row_spec(tm, d), _row_spec(tm, aw), _row_spec(tm, 3 * cw), _prev_halo_spec(tm, 3 * cw),
                  _row_spec(tm, 2 * d), _const_spec((CONV_K, cw)), _const_spec((1, d)), ANY, ANY, ANY],
        out_specs=[_row_spec(tm, d), _row_spec(tm, d), _row_spec(tm, d), _row_spec(tm, cw)],
        out_shape=[jax.ShapeDtypeStruct((s, d), F32), jax.ShapeDtypeStruct((s, d), F32),
                   jax.ShapeDtypeStruct((s, d), BF16), jax.ShapeDtypeStruct((s, cw), BF16)],
        scratch_shapes=[pltpu.VMEM(w_ao.shape, BF16), pltpu.VMEM(w_co.shape, BF16), pltpu.VMEM(w_o.shape, BF16),
                        pltpu.SemaphoreType.DMA((3,))],
        compiler_params=_params(1),
    )(x, o, conv, conv, gate, wconv, g_post, w_ao, w_co, w_o)


def _mix_bwd(dx1, mixed, o, conv, gate, wconv, g_post, w_ao, w_co, w_o, tm):
    s, d = dx1.shape
    aw, cw = w_ao.shape[0], w_co.shape[0]

    def body(dx1_ref, mixed_ref, o_ref, conv_ref, prev_ref, gate_ref, wc_ref, g_ref, wao_hbm, wco_hbm, wo_hbm,
             dmixed_ref, dattn_ref, dconvout_ref, do_ref, dgate_ref, dcb_ref, dcv_ref, dg_ref, dbias_ref,
             wao, wco, wo, sem):
        i = pl.program_id(0)
        _load_resident([(wao_hbm, wao), (wco_hbm, wco), (wo_hbm, wo)], sem)

        @pl.when(i == 0)
        def _():
            dg_ref[...] = jnp.zeros_like(dg_ref)
            dbias_ref[...] = jnp.zeros_like(dbias_ref)

        mixed = mixed_ref[...]
        r = _rms_scale(mixed)
        mhat = mixed * r
        dn = dx1_ref[...]
        dg_ref[...] += jnp.sum(dn * mhat, axis=0, keepdims=True)
        dmixed = _rms_bwd(mhat, r, g_ref[...], dn).astype(BF16)
        dmixed_ref[...] = dmixed
        dmi = _dot_nt(dmixed, wo[...])

        y_attn, y_conv, _, cb, cv, *_ = _branches(
            o_ref[...].astype(BF16), conv_ref[...], prev_ref[...], wc_ref[...], wao[...], wco[...], cw, i == 0)
        ga = gate_ref[:, 0:d]
        gc = gate_ref[:, d:2 * d]
        dpre_a = dmi * y_attn * ga * (1.0 - ga)
        dpre_c = dmi * y_conv * gc * (1.0 - gc)
        dgate_ref[:, 0:d] = dpre_a.astype(BF16)
        dgate_ref[:, d:2 * d] = dpre_c.astype(BF16)
        dbias_ref[:, 0:d] += jnp.sum(dpre_a, axis=0, keepdims=True)
        dbias_ref[:, d:2 * d] += jnp.sum(dpre_c, axis=0, keepdims=True)

        dattn = (dmi * ga).astype(BF16)
        dattn_ref[...] = dattn
        do_ref[...] = _dot_nt(dattn, wao[...]).astype(BF16)
        dconvout = (dmi * gc).astype(BF16)
        dconvout_ref[...] = dconvout
        dconv_in = _dot_nt(dconvout, wco[...])
        dcb_ref[...] = (dconv_in * cv).astype(BF16)
        dcv_ref[...] = dconv_in * cb

    return pl.pallas_call(
        body, name="mix_bwd", grid=(s // tm,),
        in_specs=[_row_spec(tm, d), _row_spec(tm, d), _row_spec(tm, aw), _row_spec(tm, 3 * cw),
                  _prev_halo_spec(tm, 3 * cw), _row_spec(tm, 2 * d), _const_spec((CONV_K, cw)), _const_spec((1, d)),
                  ANY, ANY, ANY],
        out_specs=[_row_spec(tm, d), _row_spec(tm, d), _row_spec(tm, d), _row_spec(tm, aw), _row_spec(tm, 2 * d),
                   _row_spec(tm, cw), _row_spec(tm, cw), _const_spec((1, d)), _const_spec((1, 2 * d))],
        out_shape=[jax.ShapeDtypeStruct((s, d), BF16), jax.ShapeDtypeStruct((s, d), BF16),
                   jax.ShapeDtypeStruct((s, d), BF16), jax.ShapeDtypeStruct((s, aw), BF16),
                   jax.ShapeDtypeStruct((s, 2 * d), BF16), jax.ShapeDtypeStruct((s, cw), BF16),
                   jax.ShapeDtypeStruct((s, cw), F32), jax.ShapeDtypeStruct((1, d), F32),
                   jax.ShapeDtypeStruct((1, 2 * d), F32)],
        scratch_shapes=[pltpu.VMEM(w_ao.shape, BF16), pltpu.VMEM(w_co.shape, BF16), pltpu.VMEM(w_o.shape, BF16),
                        pltpu.SemaphoreType.DMA((3,))],
        compiler_params=_params(1),
    )(dx1, mixed, o, conv, conv, gate, wconv, g_post, w_ao, w_co, w_o)


def _mlp_ple_loss(x1, p, target, g_pre, g_post, g_ple, w_up, w_dn, w_pg, w_pp, tm):
    s, d = x1.shape
    ff = w_up.shape[1]
    pd = p.shape[1]
    fc = FF_CHUNK

    def body(x1_ref, p_ref, t_ref, gpre_ref, gpost_ref, gple_ref, wup_hbm, wdn_hbm, wpg_hbm, wpp_hbm,
             dx1_ref, h2_ref, du_ref, a_ref, df_ref, h3_ref, ds3_ref, dpp_ref, loss_ref, dgpre_ref, dgpost_ref,
             dgple_ref, wup, wdn, wpg, wpp, u_scr, sem):
        _load_resident([(wup_hbm, wup), (wdn_hbm, wdn), (wpg_hbm, wpg), (wpp_hbm, wpp)], sem)

        @pl.when(pl.program_id(0) == 0)
        def _():
            for ref in (loss_ref, dgpre_ref, dgpost_ref, dgple_ref):
                ref[...] = jnp.zeros_like(ref)

        x1v = x1_ref[...]
        r2 = _rms_scale(x1v)
        x1hat = x1v * r2
        h2 = (x1hat * gpre_ref[...]).astype(BF16)
        h2_ref[...] = h2
        f = jnp.zeros((tm, d), F32)
        for c0 in range(0, ff, fc):
            u = _dot(h2, wup[:, c0:c0 + fc])
            u_scr[:, c0:c0 + fc] = u
            a = jnp.square(jnp.maximum(u, 0.0)).astype(BF16)
            a_ref[:, c0:c0 + fc] = a
            f = f + _dot(a, wdn[c0:c0 + fc, :])
        rf = _rms_scale(f)
        fhat = f * rf
        x2 = x1v + fhat * gpost_ref[...]
        r3 = _rms_scale(x2)
        x2hat = x2 * r3
        h3 = (x2hat * gple_ref[...]).astype(BF16)
        h3_ref[...] = h3
        pg = _sigmoid(_dot(h3, wpg[...]))
        pp = _dot(p_ref[...].astype(BF16), wpp[...])
        diff = x2 + pg * pp - t_ref[...]
        loss_ref[...] += 0.5 * jnp.sum(jnp.mean(diff * diff, axis=-1, keepdims=True), axis=0, keepdims=True)

        dy = diff * (1.0 / d)
        dpp_ref[...] = (dy * pg).astype(BF16)
        ds3 = (dy * pp * pg * (1.0 - pg)).astype(BF16)
        ds3_ref[...] = ds3
        dh3 = _dot_nt(ds3, wpg[...])
        dgple_ref[...] += jnp.sum(dh3 * x2hat, axis=0, keepdims=True)
        dx2 = dy + _rms_bwd(x2hat, r3, gple_ref[...], dh3)
        dgpost_ref[...] += jnp.sum(dx2 * fhat, axis=0, keepdims=True)
        df = _rms_bwd(fhat, rf, gpost_ref[...], dx2).astype(BF16)
        df_ref[...] = df
        dh2 = jnp.zeros((tm, d), F32)
        for c0 in range(0, ff, fc):
            da = _dot_nt(df, wdn[c0:c0 + fc, :])
            du = (da * (2.0 * jnp.maximum(u_scr[:, c0:c0 + fc], 0.0))).astype(BF16)
            du_ref[:, c0:c0 + fc] = du
            dh2 = dh2 + _dot_nt(du, wup[:, c0:c0 + fc])
        dgpre_ref[...] += jnp.sum(dh2 * x1hat, axis=0, keepdims=True)
        dx1_ref[...] = dx2 + _rms_bwd(x1hat, r2, gpre_ref[...], dh2)

    vec = _const_spec((1, d))
    return pl.pallas_call(
        body, name="mlp_ple_loss", grid=(s // tm,),
        in_specs=[_row_spec(tm, d), _row_spec(tm, pd), _row_spec(tm, d), vec, vec, vec, ANY, ANY, ANY, ANY],
        out_specs=[_row_spec(tm, d), _row_spec(tm, d), _row_spec(tm, ff), _row_spec(tm, ff), _row_spec(tm, d),
                   _row_spec(tm, d), _row_spec(tm, d), _row_spec(tm, d), _const_spec((1, 1)), vec, vec, vec],
        out_shape=[jax.ShapeDtypeStruct((s, d), F32), jax.ShapeDtypeStruct((s, d), BF16),
                   jax.ShapeDtypeStruct((s, ff), BF16), jax.ShapeDtypeStruct((s, ff), BF16),
                   jax.ShapeDtypeStruct((s, d), BF16), jax.ShapeDtypeStruct((s, d), BF16),
                   jax.ShapeDtypeStruct((s, d), BF16), jax.ShapeDtypeStruct((s, d), BF16),
                   jax.ShapeDtypeStruct((1, 1), F32), jax.ShapeDtypeStruct((1, d), F32),
                   jax.ShapeDtypeStruct((1, d), F32), jax.ShapeDtypeStruct((1, d), F32)],
        scratch_shapes=[pltpu.VMEM(w_up.shape, BF16), pltpu.VMEM(w_dn.shape, BF16), pltpu.VMEM(w_pg.shape, BF16),
                        pltpu.VMEM(w_pp.shape, BF16), pltpu.VMEM((tm, ff), F32), pltpu.SemaphoreType.DMA((4,))],
        compiler_params=_params(1),
    )(x1, p, target, g_pre, g_post, g_ple, w_up, w_dn, w_pg, w_pp)


def _in_proj_bwd(x, dx1, dq, dk, dv, dcb, dcv, conv, dgate, wconv, g1, w_in, tm):
    s, d = x.shape
    aw, cw = dq.shape[1], dcb.shape[1]
    ni = w_in.shape[1]

    def body(x_ref, dx1_ref, dq_ref, dk_ref, dv_ref, dcb_ref, dcv_ref, dcvn_ref, cc_ref, cu_ref, ccp_ref, cup_ref,
             dgate_ref, wc_ref, g_ref, w_hbm, dx_ref, dproj_ref, h1_ref, dg_ref, dwc_ref, w_vmem, sem):
        i = pl.program_id(0)
        _load_resident([(w_hbm, w_vmem)], sem)

        @pl.when(i == 0)
        def _():
            dg_ref[...] = jnp.zeros_like(dg_ref)
            dwc_ref[...] = jnp.zeros_like(dwc_ref)

        wc = wc_ref[...]
        cc = cc_ref[...]
        cu = cu_ref[...]
        cm = cc * cu
        cm_prev = jnp.where(i == 0, 0.0, ccp_ref[...] * cup_ref[...])
        _, cm1, cm2 = _conv_taps(cm, cm_prev, wc)
        dcv_cur = dcv_ref[...]
        dcv_next = jnp.where(i == pl.num_programs(0) - 1, 0.0, dcvn_ref[...])
        dcm = (wc[2:3, :] * dcv_cur + wc[1:2, :] * _shift_up(dcv_cur, dcv_next, 1)
               + wc[0:1, :] * _shift_up(dcv_cur, dcv_next, 2))
        for tap, shifted in enumerate((cm2, cm1, cm)):
            dwc_ref[tap:tap + 1, :] += jnp.sum(dcv_cur * shifted, axis=0, keepdims=True)

        pieces = [(dq_ref[...], aw), (dk_ref[...], aw), (dv_ref[...], aw), (dcb_ref[...], cw),
                  ((dcm * cu).astype(BF16), cw), ((dcm * cc).astype(BF16), cw), (dgate_ref[...], 2 * d)]
        dh = jnp.zeros((tm, d), F32)
        c0 = 0
        for piece, width in pieces:
            dproj_ref[:, c0:c0 + width] = piece
            dh = dh + _dot_nt(piece, w_vmem[:, c0:c0 + width])
            c0 += width

        xv = x_ref[...]
        r = _rms_scale(xv)
        xhat = xv * r
        h1_ref[...] = (xhat * g_ref[...]).astype(BF16)
        dg_ref[...] += jnp.sum(dh * xhat, axis=0, keepdims=True)
        dx_ref[...] = dx1_ref[...] + _rms_bwd(xhat, r, g_ref[...], dh)

    return pl.pallas_call(
        body, name="in_proj_bwd", grid=(s // tm,),
        in_specs=[_row_spec(tm, d), _row_spec(tm, d), _row_spec(tm, aw), _row_spec(tm, aw), _row_spec(tm, aw),
                  _row_spec(tm, cw), _row_spec(tm, cw), _next_halo_spec(tm, cw, s), _col_spec(tm, cw, 1),
                  _col_spec(tm, cw, 2), _prev_halo_spec(tm, cw, 1), _prev_halo_spec(tm, cw, 2), _row_spec(tm, 2 * d),
                  _const_spec((CONV_K, cw)), _const_spec((1, d)), ANY],
        out_specs=[_row_spec(tm, d), _row_spec(tm, ni), _row_spec(tm, d), _const_spec((1, d)),
                   _const_spec((CONV_K, cw))],
        out_shape=[jax.ShapeDtypeStruct((s, d), F32), jax.ShapeDtypeStruct((s, ni), BF16),
                   jax.ShapeDtypeStruct((s, d), BF16), jax.ShapeDtypeStruct((1, d), F32),
                   jax.ShapeDtypeStruct((CONV_K, cw), F32)],
        scratch_shapes=[pltpu.VMEM((d, ni), BF16), pltpu.SemaphoreType.DMA((1,))],
        compiler_params=_params(1),
    )(x, dx1, dq, dk, dv, dcb, dcv, dcv, conv, conv, conv, conv, dgate, wconv, g1, w_in)


def _weight_grad(a, b, name):
    s, m = a.shape
    n = b.shape[1]
    tm, tn, tk = min(m, DW_TILE), min(n, DW_TILE), min(s, DW_TOKENS)
    nk = s // tk

    def body(a_ref, b_ref, o_ref, acc):
        k = pl.program_id(2)

        @pl.when(k == 0)
        def _():
            acc[...] = jnp.zeros_like(acc)

        acc[...] += _dot_tn(a_ref[...].astype(BF16), b_ref[...].astype(BF16))

        @pl.when(k == nk - 1)
        def _():
            o_ref[...] = acc[...].astype(BF16)

    return pl.pallas_call(
        body, name=name, grid=(m // tm, n // tn, nk),
        in_specs=[pl.BlockSpec((tk, tm), lambda i, j, k: (k, i)), pl.BlockSpec((tk, tn), lambda i, j, k: (k, j))],
        out_specs=pl.BlockSpec((tm, tn), lambda i, j, k: (i, j)),
        out_shape=jax.ShapeDtypeStruct((m, n), BF16),
        scratch_shapes=[pltpu.VMEM((tm, tn), F32)],
        compiler_params=_params(3),
    )(a, b)


def _mesh_position():
    return tuple(lax.axis_index(a) for a in MESH_AXES)


def _peer(me, k):
    bits = ((k >> 2) & 1, (k >> 1) & 1, k & 1)
    pos = tuple(1 - m if b else m for m, b in zip(me, bits))
    return pos, 4 * pos[0] + 2 * pos[1] + pos[2]


class _Exchange:
    def __init__(self, arrays, out_shapes, src, dst):
        self.arrays, self.out_shapes, self.src, self.dst = list(arrays), list(out_shapes), src, dst


_NO_EXCHANGE = _Exchange([], [], None, None)


def _exchange_sems(ex):
    n = len(ex.arrays)
    if n == 0:
        return []
    return [pltpu.SemaphoreType.DMA((n, N_DEV - 1)), pltpu.SemaphoreType.DMA((n, N_DEV - 1)),
            pltpu.SemaphoreType.DMA((n,))]


def _split_refs(rest, ex, n_own_outs):
    n_in, n_out = len(ex.arrays), len(ex.out_shapes)
    ex_in, rest = rest[:n_in], rest[n_in:]
    own, rest = rest[:n_own_outs], rest[n_own_outs:]
    return ex_in, own, rest[:n_out], rest[n_out:]


def _exchange_copies(ex, in_refs, out_refs, sems):
    send_sems, recv_sems, local_sems = sems
    me = _mesh_position()
    mine = 4 * me[0] + 2 * me[1] + me[2]
    copies = []
    for w in range(len(ex.arrays)):
        landing = ex.dst(w, out_refs, mine)
        copies.append(pltpu.make_async_copy(ex.src(w, in_refs, mine), landing, local_sems.at[w]))
        for k in range(1, N_DEV):
            peer, peer_idx = _peer(me, k)
            copies.append(pltpu.make_async_remote_copy(
                src_ref=ex.src(w, in_refs, peer_idx), dst_ref=landing, send_sem=send_sems.at[w, k - 1],
                recv_sem=recv_sems.at[w, k - 1], device_id=peer, device_id_type=pl.DeviceIdType.MESH))
    return copies


def _exchange_start(ex, in_refs, out_refs, sems):
    if ex.arrays:
        @pl.when(jnp.logical_and(pl.program_id(0) == 0, pl.program_id(1) == 0))
        def _():
            for cp in _exchange_copies(ex, in_refs, out_refs, sems):
                cp.start()


def _exchange_wait(ex, in_refs, out_refs, sems, n0, n1):
    if ex.arrays:
        @pl.when(jnp.logical_and(pl.program_id(0) == n0 - 1, pl.program_id(1) == n1 - 1))
        def _():
            for cp in _exchange_copies(ex, in_refs, out_refs, sems):
                cp.wait()


def _shard_block(ref, shard_shape, by_col, idx):
    r, c = shard_shape
    if by_col:
        return ref.at[:, pl.ds(pl.multiple_of(idx * c, LANES), c)]
    return ref.at[pl.ds(pl.multiple_of(idx * r, 16), r), :]


def _full_shape(shard_shape, by_col):
    r, c = shard_shape
    return (r, N_DEV * c) if by_col else (N_DEV * r, c)


def _gather_exchange(shards, col_sharded):
    shapes = [a.shape for a in shards]
    return _Exchange(
        shards, [jax.ShapeDtypeStruct(_full_shape(sh, bc), a.dtype) for a, sh, bc in zip(shards, shapes, col_sharded)],
        lambda w, refs, idx: refs[w],
        lambda w, refs, mine: _shard_block(refs[w], shapes[w], col_sharded[w], mine))


def _scatter_exchange(grads, col_sharded):
    shapes = []
    for g, by_col in zip(grads, col_sharded):
        r, c = g.shape
        shapes.append((r, c // N_DEV) if by_col else (r // N_DEV, c))
    return _Exchange(
        grads, [jax.ShapeDtypeStruct((N_DEV,) + sh, g.dtype) for g, sh in zip(grads, shapes)],
        lambda w, refs, idx: _shard_block(refs[w], shapes[w], col_sharded[w], idx),
        lambda w, refs, mine: refs[w].at[mine])


def _broadcast_exchange(arrays):
    return _Exchange(arrays, [jax.ShapeDtypeStruct((N_DEV,) + a.shape, a.dtype) for a in arrays],
                     lambda w, refs, idx: refs[w], lambda w, refs, mine: refs[w].at[mine])


def _join(*exs):
    arrays, shapes, owner = [], [], []
    for e in exs:
        for w in range(len(e.arrays)):
            owner.append((e, w, len(arrays), len(shapes)))
        arrays += e.arrays
        shapes += e.out_shapes

    def src(w, refs, idx):
        e, w0, i0, _ = owner[w]
        return e.src(w0, refs[i0:i0 + len(e.arrays)], idx)

    def dst(w, refs, mine):
        e, w0, _, o0 = owner[w]
        return e.dst(w0, refs[o0:o0 + len(e.out_shapes)], mine)

    return _Exchange(arrays, shapes, src, dst)


def _exchange_call(ex, name):
    n_in = len(ex.arrays)

    def body(*refs):
        in_refs, _, out_refs, sems = _split_refs(refs, ex, 0)
        copies = _exchange_copies(ex, in_refs, out_refs, sems)
        for cp in copies:
            cp.start()
        for cp in copies:
            cp.wait()

    return pl.pallas_call(
        body, name=name, in_specs=[ANY] * n_in, out_specs=[ANY] * len(ex.out_shapes), out_shape=ex.out_shapes,
        scratch_shapes=_exchange_sems(ex), compiler_params=pltpu.CompilerParams(vmem_limit_bytes=VMEM_LIMIT),
    )(*ex.arrays)


def _to_bf16(arrays):
    def body(*refs):
        for src, dst in zip(refs[:len(arrays)], refs[len(arrays):]):
            dst[...] = src[...].astype(BF16)

    vmem = pl.BlockSpec(memory_space=pltpu.VMEM)
    return pl.pallas_call(
        body, name="weights_to_bf16", in_specs=[vmem] * len(arrays), out_specs=[vmem] * len(arrays),
        out_shape=[jax.ShapeDtypeStruct(a.shape, BF16) for a in arrays],
        compiler_params=pltpu.CompilerParams(vmem_limit_bytes=VMEM_LIMIT),
    )(*arrays)


def _adamw(w, g, m, v):
    m = ADAM_B1 * m + (1.0 - ADAM_B1) * g
    v = ADAM_B2 * v + (1.0 - ADAM_B2) * jnp.square(g)
    m_hat = m / (1.0 - ADAM_B1 ** ADAM_STEP)
    v_hat = v / (1.0 - ADAM_B2 ** ADAM_STEP)
    delta = -ADAM_LR * (m_hat / (jnp.sqrt(v_hat) + ADAM_EPS) + ADAM_WD * w)
    return delta, m, v


def _sum_and_adamw(parts, w, m, v, name):
    r, c = w.shape
    tr = min(r, 256)

    def body(p_ref, w_ref, m_ref, v_ref, g_out, d_out, m_out, v_out):
        g = p_ref[0].astype(F32)
        for dev in range(1, N_DEV):
            g = g + p_ref[dev].astype(F32)
        g_out[...] = g
        d_out[...], m_out[...], v_out[...] = _adamw(w_ref[...], g, m_ref[...], v_ref[...])

    blk = pl.BlockSpec((tr, c), lambda i: (i, 0))
    return pl.pallas_call(
        body, name=name, grid=(r // tr,),
        in_specs=[pl.BlockSpec((N_DEV, tr, c), lambda i: (0, i, 0)), blk, blk, blk],
        out_specs=[blk] * 4, out_shape=[jax.ShapeDtypeStruct((r, c), F32)] * 4,
        compiler_params=_params(1),
    )(parts, w, m, v)


BIG = ("w_in", "w_attn_out", "w_conv_out", "w_o", "w_up", "w_down", "w_ple_gate", "w_ple_proj")
COL_SHARDED = {"w_in": True, "w_attn_out": True, "w_conv_out": True, "w_o": False, "w_up": True, "w_down": False,
               "w_ple_gate": False, "w_ple_proj": True}
SMALL = ("g_pre_mix", "b_gate", "g_post_mix", "g_pre_mlp", "g_post_mlp", "g_ple")


REST = BIG[1:]


def _local_grads(x, p, target, small, wconv, full, aw, cw, tm, t, gather_rest=None, scatter_rest=None):
    full = dict(full)
    qkv, conv, gate = _in_proj_fwd(x, small["g_pre_mix"], small["b_gate"], full["w_in"], aw, cw, tm)
    o, *rest = _attn_fwd(qkv, aw, t, gather_rest)
    full.update(zip(REST, rest))
    x1, mixed, mix_in, conv_in = _mix_fwd(x, o, conv, gate, wconv, small["g_post_mix"], full["w_attn_out"],
                                          full["w_conv_out"], full["w_o"], tm)
    (dx1, h2, du, a, df, h3, ds3, dpp, loss, dg_pre_mlp, dg_post_mlp, dg_ple) = _mlp_ple_loss(
        x1, p, target, small["g_pre_mlp"], small["g_post_mlp"], small["g_ple"], full["w_up"], full["w_down"],
        full["w_ple_gate"], full["w_ple_proj"], tm)
    big = {"w_up": _weight_grad(h2, du, "dw_up"), "w_down": _weight_grad(a, df, "dw_down"),
           "w_ple_gate": _weight_grad(h3, ds3, "dw_ple_gate"), "w_ple_proj": _weight_grad(p, dpp, "dw_ple_proj")}
    (dmixed, dattn, dconvout, do, dgate, dcb, dcv, dg_post_mix, db_gate) = _mix_bwd(
        dx1, mixed, o, conv, gate, wconv, small["g_post_mix"], full["w_attn_out"], full["w_conv_out"], full["w_o"],
        tm)
    big.update({"w_attn_out": _weight_grad(o, dattn, "dw_attn_out"),
                "w_conv_out": _weight_grad(conv_in, dconvout, "dw_conv_out"),
                "w_o": _weight_grad(mix_in, dmixed, "dw_o")})
    dq, dk, dv, *scattered = _attn_bwd(qkv, o, do, aw, t, scatter_rest and scatter_rest([big[n] for n in REST]))
    dx, dproj, h1, dg_pre_mix, dwconv = _in_proj_bwd(x, dx1, dq, dk, dv, dcb, dcv, conv, dgate, wconv,
                                                    small["g_pre_mix"], full["w_in"], tm)
    big["w_in"] = _weight_grad(h1, dproj, "dw_in")
    small_grads = {"g_pre_mix": dg_pre_mix, "b_gate": db_gate, "g_post_mix": dg_post_mix, "g_pre_mlp": dg_pre_mlp,
                   "g_post_mlp": dg_post_mlp, "g_ple": dg_ple, "w_conv": dwconv}
    return loss[0, 0], dx, big, small_grads, scattered


def _pack_small(vals, d):
    parts = []
    for a in vals:
        a = jnp.pad(a.reshape(-1), (0, -a.size % d)).reshape(-1, d)
        parts.append(jnp.pad(a, ((0, HALO - a.shape[0]), (0, 0))))
    return jnp.concatenate(parts, axis=0)


def _unpack_small(pack, shapes, d):
    out = []
    for i, shp in enumerate(shapes):
        n = 1
        for v in shp:
            n *= v
        rows = -(-n // d)
        out.append(pack[i * HALO:i * HALO + rows].reshape(-1)[:n].reshape(shp))
    return out


def kernel(x, p, g_pre_mix, w_in, b_gate, w_conv, w_attn_out, w_conv_out, w_o, g_post_mix, g_pre_mlp, w_up, w_down, g_post_mlp, g_ple, w_ple_gate, w_ple_proj, loss_target, m_g_pre_mix, m_w_in, m_b_gate, m_w_conv, m_w_attn_out, m_w_conv_out, m_w_o, m_g_post_mix, m_g_pre_mlp, m_w_up, m_w_down, m_g_post_mlp, m_g_ple, m_w_ple_gate, m_w_ple_proj, v_g_pre_mix, v_w_in, v_b_gate, v_w_conv, v_w_attn_out, v_w_conv_out, v_w_o, v_g_post_mix, v_g_pre_mlp, v_w_up, v_w_down, v_g_post_mlp, v_g_ple, v_w_ple_gate, v_w_ple_proj):
    given = dict(locals())
    order = ["g_pre_mix", "w_in", "b_gate", "w_conv", "w_attn_out", "w_conv_out", "w_o", "g_post_mix", "g_pre_mlp",
             "w_up", "w_down", "g_post_mlp", "g_ple", "w_ple_gate", "w_ple_proj"]
    d = x.shape[-1]
    me = 4 * lax.axis_index("x") + 2 * lax.axis_index("y") + lax.axis_index("c")

    col = [COL_SHARDED[n] for n in BIG]
    shards = _to_bf16([given[n][0] for n in BIG])
    cw_shard = w_conv.shape[-1]
    conv_tile = jnp.pad(w_conv[0], ((0, HALO - CONV_K), (0, LANES - cw_shard)))
    w_in_full, conv_g = _exchange_call(
        _join(_gather_exchange(shards[:1], col[:1]), _broadcast_exchange([conv_tile])), "gather_w_in")
    wconv = jnp.concatenate([conv_g[dev, :CONV_K, :cw_shard] for dev in range(N_DEV)], axis=1)

    small = {n: given[n] for n in SMALL}
    loss, dx, big_grads, small_grads, parts_rest = _local_grads(
        x[0], p[0, 0], loss_target[0], small, wconv, {"w_in": w_in_full}, w_attn_out.shape[1], w_conv_out.shape[1],
        ROW_BLOCK, ATTN_BLOCK,
        _gather_exchange(shards[1:], col[1:]), lambda grads: _scatter_exchange(grads, col[1:]))
    loss = lax.psum(loss, MESH_AXES)

    small_names = list(SMALL) + ["w_conv"]
    pack = _pack_small([small_grads[n] for n in small_names], d)
    part_in, packs = _exchange_call(
        _join(_scatter_exchange([big_grads["w_in"]], col[:1]), _broadcast_exchange([pack])), "scatter_dw_in")
    parts = [part_in] + list(parts_rest)

    grads, deltas, new_m, new_v = {}, {}, {}, {}
    for n, part in zip(BIG, parts):
        grads[n], deltas[n], new_m[n], new_v[n] = (
            a[None] for a in _sum_and_adamw(part, given[n][0], given["m_" + n][0], given["v_" + n][0], "adamw_" + n))

    full_conv = lambda a: lax.dynamic_update_slice(jnp.zeros((CONV_K, N_DEV * cw_shard), F32), a[0],
                                                   (jnp.int32(0), me * cw_shard))
    state = [_pack_small([given[pre + n] for n in SMALL] + [full_conv(given[pre + "w_conv"])], d)
             for pre in ("", "m_", "v_")]
    outs = _sum_and_adamw(packs, *state, "adamw_small")
    shapes = [given[n].shape for n in SMALL] + [(CONV_K, N_DEV * cw_shard)]
    for res, dst in zip(outs, (grads, deltas, new_m, new_v)):
        for n, a in zip(small_names, _unpack_small(res, shapes, d)):
            dst[n] = (lax.dynamic_slice(a, (jnp.int32(0), me * cw_shard), (CONV_K, cw_shard))[None]
                      if n == "w_conv" else a)

    return (loss, dx[None], *[grads[n] for n in order], *[deltas[n] for n in order],
            *[new_m[n] for n in order], *[new_v[n] for n in order])
```

```python
import jax
import jax.numpy as jnp
from jax import lax
from jax.experimental import pallas as pl
from jax.experimental.pallas import tpu as pltpu

F32 = jnp.float32
BF16 = jnp.bfloat16
RMS_EPS = 1e-6
N_DEV = 8
MESH_AXES = ("x", "y", "c")
LANES = 128
HEAD_DIM = 64
HEADS_PER_GROUP = LANES // HEAD_DIM
CONV_K = 3
HALO = 8
VMEM_LIMIT = 56 * 1024 * 1024
EXP2_ZERO = -150.0
LOG2_E = 1.4426950408889634

ADAM_LR = 0.001
ADAM_B1 = 0.9
ADAM_B2 = 0.999
ADAM_EPS = 1e-08
ADAM_WD = 0.01
ADAM_STEP = 10

ROW_BLOCK = 256
ATTN_BLOCK = 256
DW_TOKENS = 2048
DW_TILE = 1024
DW_PIECE_TILE = 512
FF_CHUNK = 1024
PROJ_CHUNK = 512


def _dot(a, b):
    return lax.dot_general(a, b, (((1,), (0,)), ((), ())), preferred_element_type=F32)


def _dot_nt(a, b):
    return lax.dot_general(a, b, (((1,), (1,)), ((), ())), preferred_element_type=F32)


def _dot_tn(a, b):
    return lax.dot_general(a, b, (((0,), (0,)), ((), ())), preferred_element_type=F32)


def _sigmoid(z):
    return 1.0 / (1.0 + jnp.exp(-z))


def _rms_scale(x):
    return lax.rsqrt(jnp.mean(x * x, axis=-1, keepdims=True) + RMS_EPS)


def _rms_bwd(xhat, r, g, dy):
    gd = dy * g
    return r * (gd - xhat * jnp.mean(gd * xhat, axis=-1, keepdims=True))


def _params(n_axes, **kw):
    return pltpu.CompilerParams(dimension_semantics=("arbitrary",) * n_axes, vmem_limit_bytes=VMEM_LIMIT, **kw)


def _load_resident(pairs, sem):
    @pl.when(pl.program_id(0) == 0)
    def _():
        copies = [pltpu.make_async_copy(src, dst, sem.at[i]) for i, (src, dst) in enumerate(pairs)]
        for cp in copies:
            cp.start()
        for cp in copies:
            cp.wait()


def _row_spec(tm, width):
    return pl.BlockSpec((tm, width), lambda i: (i, 0))


def _prev_halo_spec(tm, width):
    per = tm // HALO
    return pl.BlockSpec((HALO, width), lambda i: (jnp.maximum(i * per - 1, 0), 0))


def _const_spec(shape):
    return pl.BlockSpec(shape, lambda i: (0,) * len(shape))


ANY = pl.BlockSpec(memory_space=pl.ANY)


def _shift_down(cur, prev, n):
    rows = lax.broadcasted_iota(jnp.int32, cur.shape, 0)
    out = pltpu.roll(cur, n, 0)
    for j in range(n):
        out = jnp.where(rows == j, prev[HALO - n + j:HALO - n + j + 1, :], out)
    return out


def _shift_up(cur, nxt, n):
    tm = cur.shape[0]
    rows = lax.broadcasted_iota(jnp.int32, cur.shape, 0)
    out = pltpu.roll(cur, tm - n, 0)
    for j in range(n):
        out = jnp.where(rows == tm - n + j, nxt[j:j + 1, :], out)
    return out


def _conv_taps(cm, cm_prev, wconv):
    cm1 = _shift_down(cm, cm_prev, 1)
    cm2 = _shift_down(cm, cm_prev, 2)
    cv = wconv[2:3, :] * cm + wconv[1:2, :] * cm1 + wconv[0:1, :] * cm2
    return cv, cm1, cm2


def _in_proj_fwd(x, g1, b_gate, w_in, aw, cw, tm):
    s, d = x.shape
    ni = w_in.shape[1]
    n_qkv, n_conv = 3 * aw, 3 * cw
    ch = PROJ_CHUNK

    def body(x_ref, g_ref, b_ref, w_hbm, qkv_ref, conv_ref, gate_ref, h_ref, w_vmem, sem):
        _load_resident([(w_hbm, w_vmem)], sem)
        xv = x_ref[...]
        h = (xv * _rms_scale(xv) * g_ref[...]).astype(BF16)
        h_ref[...] = h
        for c0 in range(0, ni, ch):
            pc = _dot(h, w_vmem[:, c0:c0 + ch])
            if c0 < n_qkv:
                qkv_ref[:, c0:c0 + ch] = pc.astype(BF16)
            elif c0 < n_qkv + n_conv:
                conv_ref[:, c0 - n_qkv:c0 - n_qkv + ch] = pc
            else:
                g0 = c0 - n_qkv - n_conv
                gate_ref[:, g0:g0 + ch] = _sigmoid(pc + b_ref[:, g0:g0 + ch])

    return pl.pallas_call(
        body, name="in_proj_fwd", grid=(s // tm,),
        in_specs=[_row_spec(tm, d), _const_spec((1, d)), _const_spec((1, 2 * d)), ANY],
        out_specs=[_row_spec(tm, n_qkv), _row_spec(tm, n_conv), _row_spec(tm, 2 * d), _row_spec(tm, d)],
        out_shape=[jax.ShapeDtypeStruct((s, n_qkv), BF16), jax.ShapeDtypeStruct((s, n_conv), F32),
                   jax.ShapeDtypeStruct((s, 2 * d), F32), jax.ShapeDtypeStruct((s, d), BF16)],
        scratch_shapes=[pltpu.VMEM((d, ni), BF16), pltpu.SemaphoreType.DMA((1,))],
        compiler_params=_params(1),
    )(x, g1, b_gate, w_in)


def _split_hi_lo(a):
    hi = a.astype(BF16)
    return hi, (a - hi.astype(F32)).astype(BF16)


def _log2_gates(z):
    z2 = z * LOG2_E
    nz2 = -z2
    log_keep = jnp.minimum(nz2, 0.0) - jnp.log2(1.0 + jnp.exp2(jnp.minimum(z2, nz2)))
    return log_keep + z2, log_keep


def _attn_masks(t):
    row = lax.broadcasted_iota(jnp.int32, (t, t), 0)
    col = lax.broadcasted_iota(jnp.int32, (t, t), 1)
    return col < row, (row > col).astype(BF16), (row >= col).astype(BF16)


def _per_head(a):
    return [jnp.where(_head_lanes(h), a, jnp.zeros_like(a)) for h in range(HEADS_PER_GROUP)]


def _merge_heads(parts):
    out = parts[0]
    for h in range(1, HEADS_PER_GROUP):
        out = jnp.where(_head_lanes(h), parts[h], out)
    return out


def _while_weights_live(qi, block, carry):
    def cond(state):
        j, carry = state
        live = jnp.max(carry[0][0])
        for run in carry[0][1:]:
            live = jnp.maximum(live, jnp.max(run))
        return jnp.logical_and(j < qi, live >= EXP2_ZERO)

    def step(state):
        j, carry = state
        return j + 1, block(qi - 1 - j, carry)

    return lax.while_loop(cond, step, (jnp.int32(0), carry))[1]


def _head_lanes(h):
    lane = lax.broadcasted_iota(jnp.int32, (1, LANES), 1)
    return (lane >= HEAD_DIM * h) & (lane < HEAD_DIM * (h + 1))


def _attn_fwd(qkv, aw, t, exchange=None):
    s = qkv.shape[0]
    groups = aw // LANES
    nq = s // t
    scale = HEAD_DIM ** -0.5
    ex = exchange or _NO_EXCHANGE

    def body(q_ref, k_ref, v_ref, *rest):
        ex_in, (o_ref,), ex_out, sems = _split_refs(rest, ex, 1)
        qi = pl.program_id(1)
        _exchange_start(ex, ex_in, ex_out, sems, (groups, nq))
        causal, upper, _ = _attn_masks(t)
        qs = _per_head(q_ref[...] * scale)
        heads = range(HEADS_PER_GROUP)

        def block(kb, runs, accs, diag):
            rows = pl.ds(pl.multiple_of(kb * t, t), t)
            k = k_ref[rows, :]
            v = v_ref[rows, :]
            zs = [_dot_nt(qs[h], k) for h in heads]
            gates, sums = [], []
            for h in heads:
                log_b, log_keep = _log2_gates(zs[h])
                if diag:
                    log_keep = jnp.where(causal, log_keep, 0.0)
                hi, lo = _split_hi_lo(log_keep)
                gates.append((log_b, log_keep))
                sums.append(_dot(hi, upper) + _dot(lo, upper))
            new_runs, new_accs = [], []
            for h in heads:
                log_b, log_keep = gates[h]
                w = jnp.exp2(log_b + sums[h] + runs[h])
                if diag:
                    w = jnp.where(causal, w, 0.0)
                new_accs.append(accs[h] + _dot(w.astype(BF16), v))
                new_runs.append(runs[h] + jnp.sum(log_keep, axis=1, keepdims=True))
            return tuple(new_runs), tuple(new_accs)

        carry = block(qi, [jnp.zeros((t, 1), F32)] * len(heads), [jnp.zeros((t, LANES), F32)] * len(heads), True)
        _, accs = _while_weights_live(qi, lambda kb, carry: block(kb, *carry, False), carry)
        o_ref[...] = _merge_heads(accs)
        _exchange_wait(ex, ex_in, ex_out, sems, (groups, nq))

    return pl.pallas_call(
        body, name="attn_fwd", grid=(groups, nq),
        in_specs=[pl.BlockSpec((t, LANES), lambda g, i: (i, g)),
                  pl.BlockSpec((s, LANES), lambda g, i: (0, groups + g)),
                  pl.BlockSpec((s, LANES), lambda g, i: (0, 2 * groups + g))] + [ANY] * len(ex.arrays),
        out_specs=[pl.BlockSpec((t, LANES), lambda g, i: (i, g))] + [ANY] * len(ex.out_shapes),
        out_shape=[jax.ShapeDtypeStruct((s, aw), F32)] + ex.out_shapes,
        scratch_shapes=_exchange_sems(ex),
        compiler_params=_params(2),
    )(qkv, qkv, qkv, *ex.arrays)


def _attn_bwd(qkv, o, do, aw, t, exchange=None):
    s = qkv.shape[0]
    groups = aw // LANES
    nq = s // t
    scale = HEAD_DIM ** -0.5
    ex = exchange or _NO_EXCHANGE

    def body(q_ref, k_ref, v_ref, o_ref, do_ref, *rest):
        ex_in, (dq_ref, dk_ref, dv_ref), ex_out, (dk_acc, dv_acc, *sems) = _split_refs(rest, ex, 3)
        qi = pl.program_id(1)
        _exchange_start(ex, ex_in, ex_out, sems, (groups, nq))

        @pl.when(qi == 0)
        def _():
            dk_acc[...] = jnp.zeros_like(dk_acc)
            dv_acc[...] = jnp.zeros_like(dv_acc)

        causal, upper, lower_incl = _attn_masks(t)
        q = q_ref[...] * scale
        do_b = do_ref[...]
        qs = _per_head(q)
        dos = _per_head(do_b)
        qs_all = jnp.concatenate(qs, axis=0)
        dos_all = jnp.concatenate(dos, axis=0)
        e_totals = [jnp.sum(part, axis=1, keepdims=True) for part in _per_head(do_b.astype(F32) * o_ref[...])]
        heads = range(HEADS_PER_GROUP)

        def block(kb, runs, e_runs, dqs, diag):
            rows = pl.ds(pl.multiple_of(kb * t, t), t)
            k = k_ref[rows, :]
            v = v_ref[rows, :]
            zs = [_dot_nt(qs[h], k) for h in heads]
            dws = [_dot_nt(dos[h], v) for h in heads]
            gates, sums = [], []
            for h in heads:
                log_b, log_keep = _log2_gates(zs[h])
                beta = jnp.exp2(log_b)
                keep = jnp.exp2(log_keep)
                if diag:
                    log_keep = jnp.where(causal, log_keep, 0.0)
                hi, lo = _split_hi_lo(log_keep)
                gates.append((log_b, log_keep, beta, keep))
                sums.append(_dot(hi, upper) + _dot(lo, upper))
            es, wbs, e_suffixes = [], [], []
            for h in heads:
                w = jnp.exp2(gates[h][0] + sums[h] + runs[h])
                if diag:
                    w = jnp.where(causal, w, 0.0)
                wb = w.astype(BF16)
                e = dws[h] * wb.astype(F32)
                hi, lo = _split_hi_lo(e)
                es.append(e)
                wbs.append(wb)
                e_suffixes.append(_dot(hi, lower_incl) + _dot(lo, lower_incl) + e_runs[h])
            new_dqs, dzbs = [], []
            for h in heads:
                _, _, beta, keep = gates[h]
                dz = es[h] * keep - (e_totals[h] - e_suffixes[h]) * beta
                if diag:
                    dz = jnp.where(causal, dz, 0.0)
                dzb = dz.astype(BF16)
                dzbs.append(dzb)
                new_dqs.append(dqs[h] + _dot(dzb, k))
            dk_acc[rows, :] += _dot_tn(jnp.concatenate(dzbs, axis=0), qs_all)
            dv_acc[rows, :] += _dot_tn(jnp.concatenate(wbs, axis=0), dos_all)
            new_runs = tuple(runs[h] + jnp.sum(gates[h][1], axis=1, keepdims=True) for h in heads)
            return new_runs, tuple(e_suffixes[h][:, 0:1] for h in heads), tuple(new_dqs)

        zero_cols = [jnp.zeros((t, 1), F32)] * len(heads)
        carry = block(qi, zero_cols, zero_cols, [jnp.zeros((t, LANES), F32)] * len(heads), True)
        _, _, dqs = _while_weights_live(qi, lambda kb, carry: block(kb, *carry, False), carry)
        dq_ref[...] = (_merge_heads(dqs) * scale).astype(BF16)

        @pl.when(qi == nq - 1)
        def _():
            dk_ref[...] = dk_acc[...].astype(BF16)
            dv_ref[...] = dv_acc[...].astype(BF16)

        _exchange_wait(ex, ex_in, ex_out, sems, (groups, nq))

    blk = pl.BlockSpec((t, LANES), lambda g, i: (i, g))
    slab = pl.BlockSpec((s, LANES), lambda g, i: (0, g))
    return pl.pallas_call(
        body, name="attn_bwd", grid=(groups, nq),
        in_specs=[blk, pl.BlockSpec((s, LANES), lambda g, i: (0, groups + g)),
                  pl.BlockSpec((s, LANES), lambda g, i: (0, 2 * groups + g)), blk, blk] + [ANY] * len(ex.arrays),
        out_specs=[blk, slab, slab] + [ANY] * len(ex.out_shapes),
        out_shape=[jax.ShapeDtypeStruct((s, aw), BF16)] * 3 + ex.out_shapes,
        scratch_shapes=[pltpu.VMEM((s, LANES), F32), pltpu.VMEM((s, LANES), F32)] + _exchange_sems(ex),
        compiler_params=_params(2),
    )(qkv, qkv, qkv, o, do, *ex.arrays)


def _branches(o_b, conv, conv_prev, wconv, w_ao, w_co, cw, first):
    cb = conv[:, 0:cw]
    cm = conv[:, cw:2 * cw] * conv[:, 2 * cw:3 * cw]
    cm_prev = conv_prev[:, cw:2 * cw] * conv_prev[:, 2 * cw:3 * cw]
    cm_prev = jnp.where(first, 0.0, cm_prev)
    cv, cm1, cm2 = _conv_taps(cm, cm_prev, wconv)
    conv_in = (cb * cv).astype(BF16)
    return _dot(o_b, w_ao), _dot(conv_in, w_co), conv_in, cb, cv, cm, cm1, cm2


def _mix_fwd(x, o, conv, gate, wconv, g_post, w_ao, w_co, w_o, tm):
    s, d = x.shape
    aw, cw = w_ao.shape[0], w_co.shape[0]

    def body(x_ref, o_ref, conv_ref, prev_ref, gate_ref, wc_ref, g_ref, wao_hbm, wco_hbm, wo_hbm,
             x1_ref, mixed_ref, mixin_ref, convin_ref, wao, wco, wo, sem):
        _load_resident([(wao_hbm, wao), (wco_hbm, wco), (wo_hbm, wo)], sem)
        y_attn, y_conv, conv_in, *_ = _branches(
            o_ref[...].astype(BF16), conv_ref[...], prev_ref[...], wc_ref[...], wao[...], wco[...], cw,
            pl.program_id(0) == 0)
        mix_in = (gate_ref[:, 0:d] * y_attn + gate_ref[:, d:2 * d] * y_conv).astype(BF16)
        mixed = _dot(mix_in, wo[...])
        x1_ref[...] = x_ref[...] + mixed * _rms_scale(mixed) * g_ref[...]
        mixed_ref[...] = mixed
        mixin_ref[...] = mix_in
        convin_ref[...] = conv_in

    return pl.pallas_call(
        body, name="mix_fwd", grid=(s // tm,),
        in_specs=[_row_spec(tm, d), _row_spec(tm, aw), _row_spec(tm, 3 * cw), _prev_halo_spec(tm, 3 * cw),
                  _row_spec(tm, 2 * d), _const_spec((CONV_K, cw)), _const_spec((1, d)), ANY, ANY, ANY],
        out_specs=[_row_spec(tm, d), _row_spec(tm, d), _row_spec(tm, d), _row_spec(tm, cw)],
        out_shape=[jax.ShapeDtypeStruct((s, d), F32), jax.ShapeDtypeStruct((s, d), F32),
                   jax.ShapeDtypeStruct((s, d), BF16), jax.ShapeDtypeStruct((s, cw), BF16)],
        scratch_shapes=[pltpu.VMEM(w_ao.shape, BF16), pltpu.VMEM(w_co.shape, BF16), pltpu.VMEM(w_o.shape, BF16),
                        pltpu.SemaphoreType.DMA((3,))],
        compiler_params=_params(1),
    )(x, o, conv, conv, gate, wconv, g_post, w_ao, w_co, w_o)


def _mix_bwd(dx1, mixed, o, conv, gate, wconv, g_post, w_ao, w_co, w_o, tm):
    s, d = dx1.shape
    aw, cw = w_ao.shape[0], w_co.shape[0]
    n = s // tm
    per = tm // HALO

    def body(dx1_ref, mixed_ref, o_ref, conv_ref, prev_ref, gate_ref, wc_ref, g_ref, wao_hbm, wco_hbm, wo_hbm,
             dmixed_ref, dattn_ref, dconvout_ref, do_ref, drest_ref, dg_ref, dbias_ref, dwc_ref,
             wao, wco, wo, dcv_next, sem):
        i = pl.program_id(0)
        _load_resident([(wao_hbm, wao), (wco_hbm, wco), (wo_hbm, wo)], sem)

        @pl.when(i == 0)
        def _():
            dg_ref[...] = jnp.zeros_like(dg_ref)
            dbias_ref[...] = jnp.zeros_like(dbias_ref)
            dwc_ref[...] = jnp.zeros_like(dwc_ref)
            dcv_next[...] = jnp.zeros_like(dcv_next)

        mixed = mixed_ref[...]
        r = _rms_scale(mixed)
        mhat = mixed * r
        dn = dx1_ref[...]
        dg_ref[...] += jnp.sum(dn * mhat, axis=0, keepdims=True)
        dmixed = _rms_bwd(mhat, r, g_ref[...], dn).astype(BF16)
        dmixed_ref[...] = dmixed
        dmi = _dot_nt(dmixed, wo[...])

        wc = wc_ref[...]
        conv = conv_ref[...]
        y_attn, y_conv, _, cb, cv, cm, cm1, cm2 = _branches(
            o_ref[...].astype(BF16), conv, prev_ref[...], wc, wao[...], wco[...], cw, i == n - 1)
        ga = gate_ref[:, 0:d]
        gc = gate_ref[:, d:2 * d]
        dpre_a = dmi * y_attn * ga * (1.0 - ga)
        dpre_c = dmi * y_conv * gc * (1.0 - gc)
        drest_ref[:, 3 * cw:3 * cw + d] = dpre_a.astype(BF16)
        drest_ref[:, 3 * cw + d:3 * cw + 2 * d] = dpre_c.astype(BF16)
        dbias_ref[:, 0:d] += jnp.sum(dpre_a, axis=0, keepdims=True)
        dbias_ref[:, d:2 * d] += jnp.sum(dpre_c, axis=0, keepdims=True)

        dattn = (dmi * ga).astype(BF16)
        dattn_ref[...] = dattn
        do_ref[...] = _dot_nt(dattn, wao[...]).astype(BF16)
        dconvout = (dmi * gc).astype(BF16)
        dconvout_ref[...] = dconvout
        dconv_in = _dot_nt(dconvout, wco[...])
        drest_ref[:, 0:cw] = (dconv_in * cv).astype(BF16)

        dcv = dconv_in * cb
        following = dcv_next[...]
        dcm = wc[2:3, :] * dcv + wc[1:2, :] * _shift_up(dcv, following, 1) + wc[0:1, :] * _shift_up(dcv, following, 2)
        drest_ref[:, cw:2 * cw] = (dcm * conv[:, 2 * cw:3 * cw]).astype(BF16)
        drest_ref[:, 2 * cw:3 * cw] = (dcm * conv[:, cw:2 * cw]).astype(BF16)
        for tap, shifted in enumerate((cm2, cm1, cm)):
            dwc_ref[tap:tap + 1, :] += jnp.sum(dcv * shifted, axis=0, keepdims=True)
        dcv_next[...] = dcv[0:HALO, :]

    def rows(width):
        return pl.BlockSpec((tm, width), lambda i: (n - 1 - i, 0))

    prev_halo = pl.BlockSpec((HALO, 3 * cw), lambda i: (jnp.maximum((n - 1 - i) * per - 1, 0), 0))
    n_rest = 3 * cw + 2 * d
    return pl.pallas_call(
        body, name="mix_bwd", grid=(n,),
        in_specs=[rows(d), rows(d), rows(aw), rows(3 * cw), prev_halo, rows(2 * d), _const_spec((CONV_K, cw)),
                  _const_spec((1, d)), ANY, ANY, ANY],
        out_specs=[rows(d), rows(d), rows(d), rows(aw), rows(n_rest), _const_spec((1, d)), _const_spec((1, 2 * d)),
                   _const_spec((CONV_K, cw))],
        out_shape=[jax.ShapeDtypeStruct((s, d), BF16), jax.ShapeDtypeStruct((s, d), BF16),
                   jax.ShapeDtypeStruct((s, d), BF16), jax.ShapeDtypeStruct((s, aw), BF16),
                   jax.ShapeDtypeStruct((s, n_rest), BF16), jax.ShapeDtypeStruct((1, d), F32),
                   jax.ShapeDtypeStruct((1, 2 * d), F32), jax.ShapeDtypeStruct((CONV_K, cw), F32)],
        scratch_shapes=[pltpu.VMEM(w_ao.shape, BF16), pltpu.VMEM(w_co.shape, BF16), pltpu.VMEM(w_o.shape, BF16),
                        pltpu.VMEM((HALO, cw), F32), pltpu.SemaphoreType.DMA((3,))],
        compiler_params=_params(1),
    )(dx1, mixed, o, conv, conv, gate, wconv, g_post, w_ao, w_co, w_o)


def _mlp_ple_loss(x1, p, target, g_pre, g_post, g_ple, w_up, w_dn, w_pg, w_pp, tm):
    s, d = x1.shape
    ff = w_up.shape[1]
    pd = p.shape[1]
    fc = FF_CHUNK

    def body(x1_ref, p_ref, t_ref, gpre_ref, gpost_ref, gple_ref, wup_hbm, wdn_hbm, wpg_hbm, wpp_hbm,
             dx1_ref, h2_ref, du_ref, a_ref, df_ref, h3_ref, ds3_ref, dpp_ref, loss_ref, dgpre_ref, dgpost_ref,
             dgple_ref, wup, wdn, wpg, wpp, u_scr, sem):
        _load_resident([(wup_hbm, wup), (wdn_hbm, wdn), (wpg_hbm, wpg), (wpp_hbm, wpp)], sem)

        @pl.when(pl.program_id(0) == 0)
        def _():
            for ref in (loss_ref, dgpre_ref, dgpost_ref, dgple_ref):
                ref[...] = jnp.zeros_like(ref)

        x1v = x1_ref[...]
        r2 = _rms_scale(x1v)
        x1hat = x1v * r2
        h2 = (x1hat * gpre_ref[...]).astype(BF16)
        h2_ref[...] = h2
        f = jnp.zeros((tm, d), F32)
        for c0 in range(0, ff, fc):
            u = _dot(h2, wup[:, c0:c0 + fc])
            u_scr[:, c0:c0 + fc] = u
            a = jnp.square(jnp.maximum(u, 0.0)).astype(BF16)
            a_ref[:, c0:c0 + fc] = a
            f = f + _dot(a, wdn[c0:c0 + fc, :])
        rf = _rms_scale(f)
        fhat = f * rf
        x2 = x1v + fhat * gpost_ref[...]
        r3 = _rms_scale(x2)
        x2hat = x2 * r3
        h3 = (x2hat * gple_ref[...]).astype(BF16)
        h3_ref[...] = h3
        pg = _sigmoid(_dot(h3, wpg[...]))
        pp = _dot(p_ref[...].astype(BF16), wpp[...])
        diff = x2 + pg * pp - t_ref[...]
        loss_ref[...] += 0.5 * jnp.sum(jnp.mean(diff * diff, axis=-1, keepdims=True), axis=0, keepdims=True)

        dy = diff * (1.0 / d)
        dpp_ref[...] = (dy * pg).astype(BF16)
        ds3 = (dy * pp * pg * (1.0 - pg)).astype(BF16)
        ds3_ref[...] = ds3
        dh3 = _dot_nt(ds3, wpg[...])
        dgple_ref[...] += jnp.sum(dh3 * x2hat, axis=0, keepdims=True)
        dx2 = dy + _rms_bwd(x2hat, r3, gple_ref[...], dh3)
        dgpost_ref[...] += jnp.sum(dx2 * fhat, axis=0, keepdims=True)
        df = _rms_bwd(fhat, rf, gpost_ref[...], dx2).astype(BF16)
        df_ref[...] = df
        dh2 = jnp.zeros((tm, d), F32)
        for c0 in range(0, ff, fc):
            da = _dot_nt(df, wdn[c0:c0 + fc, :])
            du = (da * (2.0 * jnp.maximum(u_scr[:, c0:c0 + fc], 0.0))).astype(BF16)
            du_ref[:, c0:c0 + fc] = du
            dh2 = dh2 + _dot_nt(du, wup[:, c0:c0 + fc])
        dgpre_ref[...] += jnp.sum(dh2 * x1hat, axis=0, keepdims=True)
        dx1_ref[...] = dx2 + _rms_bwd(x1hat, r2, gpre_ref[...], dh2)

    vec = _const_spec((1, d))
    return pl.pallas_call(
        body, name="mlp_ple_loss", grid=(s // tm,),
        in_specs=[_row_spec(tm, d), _row_spec(tm, pd), _row_spec(tm, d), vec, vec, vec, ANY, ANY, ANY, ANY],
        out_specs=[_row_spec(tm, d), _row_spec(tm, d), _row_spec(tm, ff), _row_spec(tm, ff), _row_spec(tm, d),
                   _row_spec(tm, d), _row_spec(tm, d), _row_spec(tm, d), _const_spec((1, 1)), vec, vec, vec],
        out_shape=[jax.ShapeDtypeStruct((s, d), F32), jax.ShapeDtypeStruct((s, d), BF16),
                   jax.ShapeDtypeStruct((s, ff), BF16), jax.ShapeDtypeStruct((s, ff), BF16),
                   jax.ShapeDtypeStruct((s, d), BF16), jax.ShapeDtypeStruct((s, d), BF16),
                   jax.ShapeDtypeStruct((s, d), BF16), jax.ShapeDtypeStruct((s, d), BF16),
                   jax.ShapeDtypeStruct((1, 1), F32), jax.ShapeDtypeStruct((1, d), F32),
                   jax.ShapeDtypeStruct((1, d), F32), jax.ShapeDtypeStruct((1, d), F32)],
        scratch_shapes=[pltpu.VMEM(w_up.shape, BF16), pltpu.VMEM(w_dn.shape, BF16), pltpu.VMEM(w_pg.shape, BF16),
                        pltpu.VMEM(w_pp.shape, BF16), pltpu.VMEM((tm, ff), F32), pltpu.SemaphoreType.DMA((4,))],
        compiler_params=_params(1),
    )(x1, p, target, g_pre, g_post, g_ple, w_up, w_dn, w_pg, w_pp)


def _in_proj_bwd(x, dx1, pieces, g1, w_in, tm, exchange=None):
    s, d = x.shape
    ni = w_in.shape[1]
    widths = [p.shape[1] for p in pieces]
    grid = (s // tm,)
    ex = exchange or _NO_EXCHANGE

    def body(x_ref, dx1_ref, *rest):
        piece_refs, rest = rest[:len(pieces)], rest[len(pieces):]
        g_ref, w_hbm = rest[0], rest[1]
        ex_in, (dx_ref, dg_ref), ex_out, (w_vmem, sem, *sems) = _split_refs(rest[2:], ex, 2)
        _exchange_start(ex, ex_in, ex_out, sems, grid)
        _load_resident([(w_hbm, w_vmem)], sem)

        @pl.when(pl.program_id(0) == 0)
        def _():
            dg_ref[...] = jnp.zeros_like(dg_ref)

        dh = jnp.zeros((tm, d), F32)
        c0 = 0
        for ref, width in zip(piece_refs, widths):
            dh = dh + _dot_nt(ref[...], w_vmem[:, c0:c0 + width])
            c0 += width
        xv = x_ref[...]
        r = _rms_scale(xv)
        xhat = xv * r
        dg_ref[...] += jnp.sum(dh * xhat, axis=0, keepdims=True)
        dx_ref[...] = dx1_ref[...] + _rms_bwd(xhat, r, g_ref[...], dh)
        _exchange_wait(ex, ex_in, ex_out, sems, grid)

    return pl.pallas_call(
        body, name="in_proj_bwd", grid=grid,
        in_specs=[_row_spec(tm, d), _row_spec(tm, d)] + [_row_spec(tm, w) for w in widths]
        + [_const_spec((1, d)), ANY] + [ANY] * len(ex.arrays),
        out_specs=[_row_spec(tm, d), _const_spec((1, d))] + [ANY] * len(ex.out_shapes),
        out_shape=[jax.ShapeDtypeStruct((s, d), F32), jax.ShapeDtypeStruct((1, d), F32)] + ex.out_shapes,
        scratch_shapes=[pltpu.VMEM((d, ni), BF16), pltpu.SemaphoreType.DMA((1,))] + _exchange_sems(ex),
        compiler_params=_params(1),
    )(x, dx1, *pieces, g1, w_in, *ex.arrays)


def _weight_grad(a, b, name, into=None, col0=0, n_total=None):
    s, m = a.shape
    n = b.shape[1]
    tm, tk = min(m, DW_TILE), min(s, DW_TOKENS)
    tn = min(n, DW_TILE) if n_total is None else DW_PIECE_TILE
    nk = s // tk
    j0 = col0 // tn
    assert m % tm == 0 and n % tn == 0 and col0 % tn == 0

    def body(a_ref, b_ref, *rest):
        o_ref, acc = rest[-2:]
        k = pl.program_id(2)

        @pl.when(k == 0)
        def _():
            acc[...] = jnp.zeros_like(acc)

        acc[...] += _dot_tn(a_ref[...].astype(BF16), b_ref[...].astype(BF16))

        @pl.when(k == nk - 1)
        def _():
            o_ref[...] = acc[...].astype(BF16)

    extra = [] if into is None else [into]
    return pl.pallas_call(
        body, name=name, grid=(m // tm, n // tn, nk),
        in_specs=[pl.BlockSpec((tk, tm), lambda i, j, k: (k, i)), pl.BlockSpec((tk, tn), lambda i, j, k: (k, j))]
        + [ANY] * len(extra),
        out_specs=pl.BlockSpec((tm, tn), lambda i, j, k: (i, j0 + j)),
        out_shape=jax.ShapeDtypeStruct((m, n_total or n), BF16),
        input_output_aliases={2: 0} if extra else {},
        scratch_shapes=[pltpu.VMEM((tm, tn), F32)],
        compiler_params=_params(3),
    )(a, b, *extra)


def _mesh_position():
    return tuple(lax.axis_index(a) for a in MESH_AXES)


def _peer(me, k):
    bits = ((k >> 2) & 1, (k >> 1) & 1, k & 1)
    pos = tuple(1 - m if b else m for m, b in zip(me, bits))
    return pos, 4 * pos[0] + 2 * pos[1] + pos[2]


class _Exchange:
    def __init__(self, arrays, out_shapes, src, dst):
        self.arrays, self.out_shapes, self.src, self.dst = list(arrays), list(out_shapes), src, dst


_NO_EXCHANGE = _Exchange([], [], None, None)


def _exchange_sems(ex):
    n = len(ex.arrays)
    if n == 0:
        return []
    return [pltpu.SemaphoreType.DMA((n, N_DEV - 1)), pltpu.SemaphoreType.DMA((n, N_DEV - 1)),
            pltpu.SemaphoreType.DMA((n,))]


def _split_refs(rest, ex, n_own_outs):
    n_in, n_out = len(ex.arrays), len(ex.out_shapes)
    ex_in, rest = rest[:n_in], rest[n_in:]
    own, rest = rest[:n_own_outs], rest[n_own_outs:]
    return ex_in, own, rest[:n_out], rest[n_out:]


def _exchange_copies(ex, in_refs, out_refs, sems):
    send_sems, recv_sems, local_sems = sems
    me = _mesh_position()
    mine = 4 * me[0] + 2 * me[1] + me[2]
    copies = []
    for w in range(len(ex.arrays)):
        landing = ex.dst(w, out_refs, mine)
        copies.append(pltpu.make_async_copy(ex.src(w, in_refs, mine), landing, local_sems.at[w]))
        for k in range(1, N_DEV):
            peer, peer_idx = _peer(me, k)
            copies.append(pltpu.make_async_remote_copy(
                src_ref=ex.src(w, in_refs, peer_idx), dst_ref=landing, send_sem=send_sems.at[w, k - 1],
                recv_sem=recv_sems.at[w, k - 1], device_id=peer, device_id_type=pl.DeviceIdType.MESH))
    return copies


def _at_grid_step(grid, last):
    hit = pl.program_id(0) == (grid[0] - 1 if last else 0)
    for axis in range(1, len(grid)):
        hit = jnp.logical_and(hit, pl.program_id(axis) == (grid[axis] - 1 if last else 0))
    return hit


def _exchange_start(ex, in_refs, out_refs, sems, grid):
    if ex.arrays:
        @pl.when(_at_grid_step(grid, False))
        def _():
            for cp in _exchange_copies(ex, in_refs, out_refs, sems):
                cp.start()


def _exchange_wait(ex, in_refs, out_refs, sems, grid):
    if ex.arrays:
        @pl.when(_at_grid_step(grid, True))
        def _():
            for cp in _exchange_copies(ex, in_refs, out_refs, sems):
                cp.wait()


def _shard_block(ref, shard_shape, by_col, idx):
    r, c = shard_shape
    if by_col:
        return ref.at[:, pl.ds(pl.multiple_of(idx * c, LANES), c)]
    return ref.at[pl.ds(pl.multiple_of(idx * r, 16), r), :]


def _full_shape(shard_shape, by_col):
    r, c = shard_shape
    return (r, N_DEV * c) if by_col else (N_DEV * r, c)


def _gather_exchange(shards, col_sharded):
    shapes = [a.shape for a in shards]
    return _Exchange(
        shards, [jax.ShapeDtypeStruct(_full_shape(sh, bc), a.dtype) for a, sh, bc in zip(shards, shapes, col_sharded)],
        lambda w, refs, idx: refs[w],
        lambda w, refs, mine: _shard_block(refs[w], shapes[w], col_sharded[w], mine))


def _scatter_exchange(grads, col_sharded):
    shapes = []
    for g, by_col in zip(grads, col_sharded):
        r, c = g.shape
        shapes.append((r, c // N_DEV) if by_col else (r // N_DEV, c))
    return _Exchange(
        grads, [jax.ShapeDtypeStruct((N_DEV,) + sh, g.dtype) for g, sh in zip(grads, shapes)],
        lambda w, refs, idx: _shard_block(refs[w], shapes[w], col_sharded[w], idx),
        lambda w, refs, mine: refs[w].at[mine])


def _broadcast_exchange(arrays):
    return _Exchange(arrays, [jax.ShapeDtypeStruct((N_DEV,) + a.shape, a.dtype) for a in arrays],
                     lambda w, refs, idx: refs[w], lambda w, refs, mine: refs[w].at[mine])


def _join(*exs):
    arrays, shapes, owner = [], [], []
    for e in exs:
        for w in range(len(e.arrays)):
            owner.append((e, w, len(arrays), len(shapes)))
        arrays += e.arrays
        shapes += e.out_shapes

    def src(w, refs, idx):
        e, w0, i0, _ = owner[w]
        return e.src(w0, refs[i0:i0 + len(e.arrays)], idx)

    def dst(w, refs, mine):
        e, w0, _, o0 = owner[w]
        return e.dst(w0, refs[o0:o0 + len(e.out_shapes)], mine)

    return _Exchange(arrays, shapes, src, dst)


def _exchange_call(ex, name):
    n_in = len(ex.arrays)

    def body(*refs):
        in_refs, _, out_refs, sems = _split_refs(refs, ex, 0)
        copies = _exchange_copies(ex, in_refs, out_refs, sems)
        for cp in copies:
            cp.start()
        for cp in copies:
            cp.wait()

    return pl.pallas_call(
        body, name=name, in_specs=[ANY] * n_in, out_specs=[ANY] * len(ex.out_shapes), out_shape=ex.out_shapes,
        scratch_shapes=_exchange_sems(ex), compiler_params=pltpu.CompilerParams(vmem_limit_bytes=VMEM_LIMIT),
    )(*ex.arrays)


def _to_bf16(arrays):
    def body(*refs):
        for src, dst in zip(refs[:len(arrays)], refs[len(arrays):]):
            dst[...] = src[...].astype(BF16)

    vmem = pl.BlockSpec(memory_space=pltpu.VMEM)
    return pl.pallas_call(
        body, name="weights_to_bf16", in_specs=[vmem] * len(arrays), out_specs=[vmem] * len(arrays),
        out_shape=[jax.ShapeDtypeStruct(a.shape, BF16) for a in arrays],
        compiler_params=pltpu.CompilerParams(vmem_limit_bytes=VMEM_LIMIT),
    )(*arrays)


def _adamw(w, g, m, v):
    m = ADAM_B1 * m + (1.0 - ADAM_B1) * g
    v = ADAM_B2 * v + (1.0 - ADAM_B2) * jnp.square(g)
    m_hat = m / (1.0 - ADAM_B1 ** ADAM_STEP)
    v_hat = v / (1.0 - ADAM_B2 ** ADAM_STEP)
    delta = -ADAM_LR * (m_hat / (jnp.sqrt(v_hat) + ADAM_EPS) + ADAM_WD * w)
    return delta, m, v


def _sum_and_adamw(parts, w, m, v, name):
    r, c = w.shape
    tr = min(r, 256)

    def body(p_ref, w_ref, m_ref, v_ref, g_out, d_out, m_out, v_out):
        g = p_ref[0].astype(F32)
        for dev in range(1, N_DEV):
            g = g + p_ref[dev].astype(F32)
        g_out[...] = g
        d_out[...], m_out[...], v_out[...] = _adamw(w_ref[...], g, m_ref[...], v_ref[...])

    blk = pl.BlockSpec((tr, c), lambda i: (i, 0))
    return pl.pallas_call(
        body, name=name, grid=(r // tr,),
        in_specs=[pl.BlockSpec((N_DEV, tr, c), lambda i: (0, i, 0)), blk, blk, blk],
        out_specs=[blk] * 4, out_shape=[jax.ShapeDtypeStruct((r, c), F32)] * 4,
        compiler_params=_params(1),
    )(parts, w, m, v)


BIG = ("w_in", "w_attn_out", "w_conv_out", "w_o", "w_up", "w_down", "w_ple_gate", "w_ple_proj")
COL_SHARDED = {"w_in": True, "w_attn_out": True, "w_conv_out": True, "w_o": False, "w_up": True, "w_down": False,
               "w_ple_gate": False, "w_ple_proj": True}
SMALL = ("g_pre_mix", "b_gate", "g_post_mix", "g_pre_mlp", "g_post_mlp", "g_ple")


REST = BIG[1:]


def _local_grads(x, p, target, small, wconv, full, aw, cw, tm, t, gather_rest=None, scatter_rest=None,
                 scatter_in=None):
    full = dict(full)
    qkv, conv, gate, h1 = _in_proj_fwd(x, small["g_pre_mix"], small["b_gate"], full["w_in"], aw, cw, tm)
    o, *rest = _attn_fwd(qkv, aw, t, gather_rest)
    full.update(zip(REST, rest))
    x1, mixed, mix_in, conv_in = _mix_fwd(x, o, conv, gate, wconv, small["g_post_mix"], full["w_attn_out"],
                                          full["w_conv_out"], full["w_o"], tm)
    (dx1, h2, du, a, df, h3, ds3, dpp, loss, dg_pre_mlp, dg_post_mlp, dg_ple) = _mlp_ple_loss(
        x1, p, target, small["g_pre_mlp"], small["g_post_mlp"], small["g_ple"], full["w_up"], full["w_down"],
        full["w_ple_gate"], full["w_ple_proj"], tm)
    big = {"w_up": _weight_grad(h2, du, "dw_up"), "w_down": _weight_grad(a, df, "dw_down"),
           "w_ple_gate": _weight_grad(h3, ds3, "dw_ple_gate"), "w_ple_proj": _weight_grad(p, dpp, "dw_ple_proj")}
    (dmixed, dattn, dconvout, do, drest, dg_post_mix, db_gate, dwconv) = _mix_bwd(
        dx1, mixed, o, conv, gate, wconv, small["g_post_mix"], full["w_attn_out"], full["w_conv_out"], full["w_o"],
        tm)
    big.update({"w_attn_out": _weight_grad(o, dattn, "dw_attn_out"),
                "w_conv_out": _weight_grad(conv_in, dconvout, "dw_conv_out"),
                "w_o": _weight_grad(mix_in, dmixed, "dw_o")})
    dq, dk, dv, *scattered = _attn_bwd(qkv, o, do, aw, t, scatter_rest and scatter_rest([big[n] for n in REST]))
    pieces = [dq, dk, dv, drest]
    dw_in, col0, ni = None, 0, full["w_in"].shape[1]
    for i, piece in enumerate(pieces):
        dw_in = _weight_grad(h1, piece, "dw_in_%d" % i, dw_in, col0, ni)
        col0 += piece.shape[1]
    big["w_in"] = dw_in
    dx, dg_pre_mix, *scattered_in = _in_proj_bwd(x, dx1, pieces, small["g_pre_mix"], full["w_in"], tm,
                                                scatter_in and scatter_in(dw_in))
    small_grads = {"g_pre_mix": dg_pre_mix, "b_gate": db_gate, "g_post_mix": dg_post_mix, "g_pre_mlp": dg_pre_mlp,
                   "g_post_mlp": dg_post_mlp, "g_ple": dg_ple, "w_conv": dwconv}
    return loss[0, 0], dx, big, small_grads, scattered_in + scattered


def _pack_small(vals, d):
    parts = []
    for a in vals:
        a = jnp.pad(a.reshape(-1), (0, -a.size % d)).reshape(-1, d)
        parts.append(jnp.pad(a, ((0, HALO - a.shape[0]), (0, 0))))
    return jnp.concatenate(parts, axis=0)


def _unpack_small(pack, shapes, d):
    out = []
    for i, shp in enumerate(shapes):
        n = 1
        for v in shp:
            n *= v
        rows = -(-n // d)
        out.append(pack[i * HALO:i * HALO + rows].reshape(-1)[:n].reshape(shp))
    return out


def kernel(x, p, g_pre_mix, w_in, b_gate, w_conv, w_attn_out, w_conv_out, w_o, g_post_mix, g_pre_mlp, w_up, w_down, g_post_mlp, g_ple, w_ple_gate, w_ple_proj, loss_target, m_g_pre_mix, m_w_in, m_b_gate, m_w_conv, m_w_attn_out, m_w_conv_out, m_w_o, m_g_post_mix, m_g_pre_mlp, m_w_up, m_w_down, m_g_post_mlp, m_g_ple, m_w_ple_gate, m_w_ple_proj, v_g_pre_mix, v_w_in, v_b_gate, v_w_conv, v_w_attn_out, v_w_conv_out, v_w_o, v_g_post_mix, v_g_pre_mlp, v_w_up, v_w_down, v_g_post_mlp, v_g_ple, v_w_ple_gate, v_w_ple_proj):
    given = dict(locals())
    order = ["g_pre_mix", "w_in", "b_gate", "w_conv", "w_attn_out", "w_conv_out", "w_o", "g_post_mix", "g_pre_mlp",
             "w_up", "w_down", "g_post_mlp", "g_ple", "w_ple_gate", "w_ple_proj"]
    d = x.shape[-1]
    me = 4 * lax.axis_index("x") + 2 * lax.axis_index("y") + lax.axis_index("c")

    col = [COL_SHARDED[n] for n in BIG]
    shards = _to_bf16([given[n][0] for n in BIG])
    cw_shard = w_conv.shape[-1]
    conv_tile = jnp.pad(w_conv[0], ((0, HALO - CONV_K), (0, LANES - cw_shard)))
    w_in_full, conv_g = _exchange_call(
        _join(_gather_exchange(shards[:1], col[:1]), _broadcast_exchange([conv_tile])), "gather_w_in")
    wconv = jnp.concatenate([conv_g[dev, :CONV_K, :cw_shard] for dev in range(N_DEV)], axis=1)

    small = {n: given[n] for n in SMALL}
    loss, dx, big_grads, small_grads, parts = _local_grads(
        x[0], p[0, 0], loss_target[0], small, wconv, {"w_in": w_in_full}, w_attn_out.shape[1], w_conv_out.shape[1],
        ROW_BLOCK, ATTN_BLOCK,
        _gather_exchange(shards[1:], col[1:]), lambda grads: _scatter_exchange(grads, col[1:]),
        lambda grad: _scatter_exchange([grad], col[:1]))
    loss = lax.psum(loss, MESH_AXES)

    small_names = list(SMALL) + ["w_conv"]
    pack = _pack_small([small_grads[n] for n in small_names], d)
    packs, = _exchange_call(_broadcast_exchange([pack]), "share_small_grads")

    grads, deltas, new_m, new_v = {}, {}, {}, {}
    for n, part in zip(BIG, parts):
        grads[n], deltas[n], new_m[n], new_v[n] = (
            a[None] for a in _sum_and_adamw(part, given[n][0], given["m_" + n][0], given["v_" + n][0], "adamw_" + n))

    full_conv = lambda a: lax.dynamic_update_slice(jnp.zeros((CONV_K, N_DEV * cw_shard), F32), a[0],
                                                   (jnp.int32(0), me * cw_shard))
    state = [_pack_small([given[pre + n] for n in SMALL] + [full_conv(given[pre + "w_conv"])], d)
             for pre in ("", "m_", "v_")]
    outs = _sum_and_adamw(packs, *state, "adamw_small")
    shapes = [given[n].shape for n in SMALL] + [(CONV_K, N_DEV * cw_shard)]
    for res, dst in zip(outs, (grads, deltas, new_m, new_v)):
        for n, a in zip(small_names, _unpack_small(res, shapes, d)):
            dst[n] = (lax.dynamic_slice(a, (jnp.int32(0), me * cw_shard), (CONV_K, cw_shard))[None]
                      if n == "w_conv" else a)

    return (loss, dx[None], *[grads[n] for n in order], *[deltas[n] for n in order],
            *[new_m[n] for n in order], *[new_v[n] for n in order])
```

```python
import jax
import jax.numpy as jnp
from jax import lax
from jax.experimental import pallas as pl
from jax.experimental.pallas import tpu as pltpu

F32 = jnp.float32
BF16 = jnp.bfloat16
RMS_EPS = 1e-6
N_DEV = 8
MESH_AXES = ("x", "y", "c")
LANES = 128
HEAD_DIM = 64
HEADS_PER_GROUP = LANES // HEAD_DIM
CONV_K = 3
HALO = 8
VMEM_LIMIT = 56 * 1024 * 1024
EXP2_ZERO = -150.0
LOG2_E = 1.4426950408889634

ADAM_LR = 0.001
ADAM_B1 = 0.9
ADAM_B2 = 0.999
ADAM_EPS = 1e-08
ADAM_WD = 0.01
ADAM_STEP = 10

ROW_BLOCK = 256
ATTN_BLOCK = 256
DW_TOKENS = 2048
DW_TILE = 1024
DW_PIECE_TILE = 512
FF_CHUNK = 1024
PROJ_CHUNK = 512


def _dot(a, b):
    return lax.dot_general(a, b, (((1,), (0,)), ((), ())), preferred_element_type=F32)


def _dot_nt(a, b):
    return lax.dot_general(a, b, (((1,), (1,)), ((), ())), preferred_element_type=F32)


def _dot_tn(a, b):
    return lax.dot_general(a, b, (((0,), (0,)), ((), ())), preferred_element_type=F32)


def _sigmoid(z):
    return 1.0 / (1.0 + jnp.exp(-z))


def _rms_scale(x):
    return lax.rsqrt(jnp.mean(x * x, axis=-1, keepdims=True) + RMS_EPS)


def _rms_bwd(xhat, r, g, dy):
    gd = dy * g
    return r * (gd - xhat * jnp.mean(gd * xhat, axis=-1, keepdims=True))


def _params(n_axes, **kw):
    return pltpu.CompilerParams(dimension_semantics=("arbitrary",) * n_axes, vmem_limit_bytes=VMEM_LIMIT, **kw)


def _load_resident(pairs, sem):
    @pl.when(pl.program_id(0) == 0)
    def _():
        copies = [pltpu.make_async_copy(src, dst, sem.at[i]) for i, (src, dst) in enumerate(pairs)]
        for cp in copies:
            cp.start()
        for cp in copies:
            cp.wait()


def _row_spec(tm, width):
    return pl.BlockSpec((tm, width), lambda i: (i, 0))


def _prev_halo_spec(tm, width):
    per = tm // HALO
    return pl.BlockSpec((HALO, width), lambda i: (jnp.maximum(i * per - 1, 0), 0))


def _const_spec(shape):
    return pl.BlockSpec(shape, lambda i: (0,) * len(shape))


ANY = pl.BlockSpec(memory_space=pl.ANY)


def _shift_down(cur, prev, n):
    rows = lax.broadcasted_iota(jnp.int32, cur.shape, 0)
    out = pltpu.roll(cur, n, 0)
    for j in range(n):
        out = jnp.where(rows == j, prev[HALO - n + j:HALO - n + j + 1, :], out)
    return out


def _shift_up(cur, nxt, n):
    tm = cur.shape[0]
    rows = lax.broadcasted_iota(jnp.int32, cur.shape, 0)
    out = pltpu.roll(cur, tm - n, 0)
    for j in range(n):
        out = jnp.where(rows == tm - n + j, nxt[j:j + 1, :], out)
    return out


def _conv_taps(cm, cm_prev, wconv):
    cm1 = _shift_down(cm, cm_prev, 1)
    cm2 = _shift_down(cm, cm_prev, 2)
    cv = wconv[2:3, :] * cm + wconv[1:2, :] * cm1 + wconv[0:1, :] * cm2
    return cv, cm1, cm2


def _in_proj_fwd(x, g1, b_gate, w_in, aw, cw, tm):
    s, d = x.shape
    ni = w_in.shape[1]
    n_qkv, n_conv = 3 * aw, 3 * cw
    ch = PROJ_CHUNK

    def body(x_ref, g_ref, b_ref, w_hbm, qkv_ref, conv_ref, gate_ref, h_ref, w_vmem, sem):
        _load_resident([(w_hbm, w_vmem)], sem)
        xv = x_ref[...]
        h = (xv * _rms_scale(xv) * g_ref[...]).astype(BF16)
        h_ref[...] = h
        for c0 in range(0, ni, ch):
            pc = _dot(h, w_vmem[:, c0:c0 + ch])
            if c0 < n_qkv:
                qkv_ref[:, c0:c0 + ch] = pc.astype(BF16)
            elif c0 < n_qkv + n_conv:
                conv_ref[:, c0 - n_qkv:c0 - n_qkv + ch] = pc
            else:
                g0 = c0 - n_qkv - n_conv
                gate_ref[:, g0:g0 + ch] = _sigmoid(pc + b_ref[:, g0:g0 + ch])

    return pl.pallas_call(
        body, name="in_proj_fwd", grid=(s // tm,),
        in_specs=[_row_spec(tm, d), _const_spec((1, d)), _const_spec((1, 2 * d)), ANY],
        out_specs=[_row_spec(tm, n_qkv), _row_spec(tm, n_conv), _row_spec(tm, 2 * d), _row_spec(tm, d)],
        out_shape=[jax.ShapeDtypeStruct((s, n_qkv), BF16), jax.ShapeDtypeStruct((s, n_conv), F32),
                   jax.ShapeDtypeStruct((s, 2 * d), F32), jax.ShapeDtypeStruct((s, d), BF16)],
        scratch_shapes=[pltpu.VMEM((d, ni), BF16), pltpu.SemaphoreType.DMA((1,))],
        compiler_params=_params(1),
    )(x, g1, b_gate, w_in)


def _split_hi_lo(a):
    hi = a.astype(BF16)
    return hi, (a - hi.astype(F32)).astype(BF16)


def _log2_gates(z):
    z2 = z * LOG2_E
    nz2 = -z2
    log_keep = jnp.minimum(nz2, 0.0) - jnp.log2(1.0 + jnp.exp2(jnp.minimum(z2, nz2)))
    return log_keep + z2, log_keep


def _attn_masks(t):
    row = lax.broadcasted_iota(jnp.int32, (t, t), 0)
    col = lax.broadcasted_iota(jnp.int32, (t, t), 1)
    return col < row, (row > col).astype(BF16), (row >= col).astype(BF16)


def _per_head(a):
    return [jnp.where(_head_lanes(h), a, jnp.zeros_like(a)) for h in range(HEADS_PER_GROUP)]


def _merge_heads(parts):
    out = parts[0]
    for h in range(1, HEADS_PER_GROUP):
        out = jnp.where(_head_lanes(h), parts[h], out)
    return out


def _while_weights_live(qi, block, carry):
    def cond(state):
        j, carry = state
        live = jnp.max(carry[0][0])
        for run in carry[0][1:]:
            live = jnp.maximum(live, jnp.max(run))
        return jnp.logical_and(j < qi, live >= EXP2_ZERO)

    def step(state):
        j, carry = state
        return j + 1, block(qi - 1 - j, carry)

    return lax.while_loop(cond, step, (jnp.int32(0), carry))[1]


def _head_lanes(h):
    lane = lax.broadcasted_iota(jnp.int32, (1, LANES), 1)
    return (lane >= HEAD_DIM * h) & (lane < HEAD_DIM * (h + 1))


def _attn_fwd(qkv, aw, t, exchange=None):
    s = qkv.shape[0]
    groups = aw // LANES
    nq = s // t
    scale = HEAD_DIM ** -0.5
    ex = exchange or _NO_EXCHANGE

    def body(q_ref, k_ref, v_ref, *rest):
        ex_in, (o_ref,), ex_out, sems = _split_refs(rest, ex, 1)
        qi = pl.program_id(1)
        _exchange_start(ex, ex_in, ex_out, sems, (groups, nq))
        causal, upper, _ = _attn_masks(t)
        qs = _per_head(q_ref[...] * scale)
        heads = range(HEADS_PER_GROUP)

        def block(kb, runs, accs, diag):
            rows = pl.ds(pl.multiple_of(kb * t, t), t)
            k = k_ref[rows, :]
            v = v_ref[rows, :]
            zs = [_dot_nt(qs[h], k) for h in heads]
            gates, sums = [], []
            for h in heads:
                log_b, log_keep = _log2_gates(zs[h])
                if diag:
                    log_keep = jnp.where(causal, log_keep, 0.0)
                hi, lo = _split_hi_lo(log_keep)
                gates.append((log_b, log_keep))
                sums.append(_dot(hi, upper) + _dot(lo, upper))
            new_runs, new_accs = [], []
            for h in heads:
                log_b, log_keep = gates[h]
                w = jnp.exp2(log_b + sums[h] + runs[h])
                if diag:
                    w = jnp.where(causal, w, 0.0)
                new_accs.append(accs[h] + _dot(w.astype(BF16), v))
                new_runs.append(runs[h] + jnp.sum(log_keep, axis=1, keepdims=True))
            return tuple(new_runs), tuple(new_accs)

        carry = block(qi, [jnp.zeros((t, 1), F32)] * len(heads), [jnp.zeros((t, LANES), F32)] * len(heads), True)
        _, accs = _while_weights_live(qi, lambda kb, carry: block(kb, *carry, False), carry)
        o_ref[...] = _merge_heads(accs)
        _exchange_wait(ex, ex_in, ex_out, sems, (groups, nq))

    return pl.pallas_call(
        body, name="attn_fwd", grid=(groups, nq),
        in_specs=[pl.BlockSpec((t, LANES), lambda g, i: (i, g)),
                  pl.BlockSpec((s, LANES), lambda g, i: (0, groups + g)),
                  pl.BlockSpec((s, LANES), lambda g, i: (0, 2 * groups + g))] + [ANY] * len(ex.arrays),
        out_specs=[pl.BlockSpec((t, LANES), lambda g, i: (i, g))] + [ANY] * len(ex.out_shapes),
        out_shape=[jax.ShapeDtypeStruct((s, aw), F32)] + ex.out_shapes,
        scratch_shapes=_exchange_sems(ex),
        compiler_params=_params(2),
    )(qkv, qkv, qkv, *ex.arrays)


def _attn_bwd(qkv, o, do, aw, t, exchange=None):
    s = qkv.shape[0]
    groups = aw // LANES
    nq = s // t
    scale = HEAD_DIM ** -0.5
    ex = exchange or _NO_EXCHANGE

    def body(q_ref, k_ref, v_ref, o_ref, do_ref, *rest):
        ex_in, (dq_ref, dk_ref, dv_ref), ex_out, (dk_acc, dv_acc, *sems) = _split_refs(rest, ex, 3)
        qi = pl.program_id(1)
        _exchange_start(ex, ex_in, ex_out, sems, (groups, nq))

        @pl.when(qi == 0)
        def _():
            dk_acc[...] = jnp.zeros_like(dk_acc)
            dv_acc[...] = jnp.zeros_like(dv_acc)

        causal, upper, lower_incl = _attn_masks(t)
        q = q_ref[...] * scale
        do_b = do_ref[...]
        qs = _per_head(q)
        dos = _per_head(do_b)
        qs_all = jnp.concatenate(qs, axis=0)
        dos_all = jnp.concatenate(dos, axis=0)
        e_totals = [jnp.sum(part, axis=1, keepdims=True) for part in _per_head(do_b.astype(F32) * o_ref[...])]
        heads = range(HEADS_PER_GROUP)

        def block(kb, runs, e_runs, dqs, diag):
            rows = pl.ds(pl.multiple_of(kb * t, t), t)
            k = k_ref[rows, :]
            v = v_ref[rows, :]
            zs = [_dot_nt(qs[h], k) for h in heads]
            dws = [_dot_nt(dos[h], v) for h in heads]
            gates, sums = [], []
            for h in heads:
                log_b, log_keep = _log2_gates(zs[h])
                beta = jnp.exp2(log_b)
                keep = jnp.exp2(log_keep)
                if diag:
                    log_keep = jnp.where(causal, log_keep, 0.0)
                hi, lo = _split_hi_lo(log_keep)
                gates.append((log_b, log_keep, beta, keep))
                sums.append(_dot(hi, upper) + _dot(lo, upper))
            es, wbs, e_suffixes = [], [], []
            for h in heads:
                w = jnp.exp2(gates[h][0] + sums[h] + runs[h])
                if diag:
                    w = jnp.where(causal, w, 0.0)
                wb = w.astype(BF16)
                e = dws[h] * wb.astype(F32)
                hi, lo = _split_hi_lo(e)
                es.append(e)
                wbs.append(wb)
                e_suffixes.append(_dot(hi, lower_incl) + _dot(lo, lower_incl) + e_runs[h])
            new_dqs, dzbs = [], []
            for h in heads:
                _, _, beta, keep = gates[h]
                dz = es[h] * keep - (e_totals[h] - e_suffixes[h]) * beta
                if diag:
                    dz = jnp.where(causal, dz, 0.0)
                dzb = dz.astype(BF16)
                dzbs.append(dzb)
                new_dqs.append(dqs[h] + _dot(dzb, k))
            dk_acc[rows, :] += _dot_tn(jnp.concatenate(dzbs, axis=0), qs_all)
            dv_acc[rows, :] += _dot_tn(jnp.concatenate(wbs, axis=0), dos_all)
            new_runs = tuple(runs[h] + jnp.sum(gates[h][1], axis=1, keepdims=True) for h in heads)
            return new_runs, tuple(e_suffixes[h][:, 0:1] for h in heads), tuple(new_dqs)

        zero_cols = [jnp.zeros((t, 1), F32)] * len(heads)
        carry = block(qi, zero_cols, zero_cols, [jnp.zeros((t, LANES), F32)] * len(heads), True)
        _, _, dqs = _while_weights_live(qi, lambda kb, carry: block(kb, *carry, False), carry)
        dq_ref[...] = (_merge_heads(dqs) * scale).astype(BF16)

        @pl.when(qi == nq - 1)
        def _():
            dk_ref[...] = dk_acc[...].astype(BF16)
            dv_ref[...] = dv_acc[...].astype(BF16)

        _exchange_wait(ex, ex_in, ex_out, sems, (groups, nq))

    blk = pl.BlockSpec((t, LANES), lambda g, i: (i, g))
    slab = pl.BlockSpec((s, LANES), lambda g, i: (0, g))
    return pl.pallas_call(
        body, name="attn_bwd", grid=(groups, nq),
        in_specs=[blk, pl.BlockSpec((s, LANES), lambda g, i: (0, groups + g)),
                  pl.BlockSpec((s, LANES), lambda g, i: (0, 2 * groups + g)), blk, blk] + [ANY] * len(ex.arrays),
        out_specs=[blk, slab, slab] + [ANY] * len(ex.out_shapes),
        out_shape=[jax.ShapeDtypeStruct((s, aw), BF16)] * 3 + ex.out_shapes,
        scratch_shapes=[pltpu.VMEM((s, LANES), F32), pltpu.VMEM((s, LANES), F32)] + _exchange_sems(ex),
        compiler_params=_params(2),
    )(qkv, qkv, qkv, o, do, *ex.arrays)


def _branches(o_b, conv, conv_prev, wconv, w_ao, w_co, cw, first):
    cb = conv[:, 0:cw]
    cm = conv[:, cw:2 * cw] * conv[:, 2 * cw:3 * cw]
    cm_prev = conv_prev[:, cw:2 * cw] * conv_prev[:, 2 * cw:3 * cw]
    cm_prev = jnp.where(first, 0.0, cm_prev)
    cv, cm1, cm2 = _conv_taps(cm, cm_prev, wconv)
    conv_in = (cb * cv).astype(BF16)
    return _dot(o_b, w_ao), _dot(conv_in, w_co), conv_in, cb, cv, cm, cm1, cm2


def _mix_fwd(x, o, conv, gate, wconv, g_post, w_ao, w_co, w_o, tm):
    s, d = x.shape
    aw, cw = w_ao.shape[0], w_co.shape[0]

    def body(x_ref, o_ref, conv_ref, prev_ref, gate_ref, wc_ref, g_ref, wao_hbm, wco_hbm, wo_hbm,
             x1_ref, mixed_ref, mixin_ref, convin_ref, wao, wco, wo, sem):
        _load_resident([(wao_hbm, wao), (wco_hbm, wco), (wo_hbm, wo)], sem)
        y_attn, y_conv, conv_in, *_ = _branches(
            o_ref[...].astype(BF16), conv_ref[...], prev_ref[...], wc_ref[...], wao[...], wco[...], cw,
            pl.program_id(0) == 0)
        mix_in = (gate_ref[:, 0:d] * y_attn + gate_ref[:, d:2 * d] * y_conv).astype(BF16)
        mixed = _dot(mix_in, wo[...])
        x1_ref[...] = x_ref[...] + mixed * _rms_scale(mixed) * g_ref[...]
        mixed_ref[...] = mixed
        mixin_ref[...] = mix_in
        convin_ref[...] = conv_in

    return pl.pallas_call(
        body, name="mix_fwd", grid=(s // tm,),
        in_specs=[_row_spec(tm, d), _row_spec(tm, aw), _row_spec(tm, 3 * cw), _prev_halo_spec(tm, 3 * cw),
                  _row_spec(tm, 2 * d), _const_spec((CONV_K, cw)), _const_spec((1, d)), ANY, ANY, ANY],
        out_specs=[_row_spec(tm, d), _row_spec(tm, d), _row_spec(tm, d), _row_spec(tm, cw)],
        out_shape=[jax.ShapeDtypeStruct((s, d), F32), jax.ShapeDtypeStruct((s, d), F32),
                   jax.ShapeDtypeStruct((s, d), BF16), jax.ShapeDtypeStruct((s, cw), BF16)],
        scratch_shapes=[pltpu.VMEM(w_ao.shape, BF16), pltpu.VMEM(w_co.shape, BF16), pltpu.VMEM(w_o.shape, BF16),
                        pltpu.SemaphoreType.DMA((3,))],
        compiler_params=_params(1),
    )(x, o, conv, conv, gate, wconv, g_post, w_ao, w_co, w_o)


def _mix_bwd(dx1, mixed, o, conv, gate, wconv, g_post, w_ao, w_co, w_o, tm):
    s, d = dx1.shape
    aw, cw = w_ao.shape[0], w_co.shape[0]
    n = s // tm
    per = tm // HALO

    def body(dx1_ref, mixed_ref, o_ref, conv_ref, prev_ref, gate_ref, wc_ref, g_ref, wao_hbm, wco_hbm, wo_hbm,
             dmixed_ref, dattn_ref, dconvout_ref, do_ref, drest_ref, dg_ref, dbias_ref, dwc_ref,
             wao, wco, wo, dcv_next, sem):
        i = pl.program_id(0)
        _load_resident([(wao_hbm, wao), (wco_hbm, wco), (wo_hbm, wo)], sem)

        @pl.when(i == 0)
        def _():
            dg_ref[...] = jnp.zeros_like(dg_ref)
            dbias_ref[...] = jnp.zeros_like(dbias_ref)
            dwc_ref[...] = jnp.zeros_like(dwc_ref)
            dcv_next[...] = jnp.zeros_like(dcv_next)

        mixed = mixed_ref[...]
        r = _rms_scale(mixed)
        mhat = mixed * r
        dn = dx1_ref[...]
        dg_ref[...] += jnp.sum(dn * mhat, axis=0, keepdims=True)
        dmixed = _rms_bwd(mhat, r, g_ref[...], dn).astype(BF16)
        dmixed_ref[...] = dmixed
        dmi = _dot_nt(dmixed, wo[...])

        wc = wc_ref[...]
        conv = conv_ref[...]
        y_attn, y_conv, _, cb, cv, cm, cm1, cm2 = _branches(
            o_ref[...].astype(BF16), conv, prev_ref[...], wc, wao[...], wco[...], cw, i == n - 1)
        ga = gate_ref[:, 0:d]
        gc = gate_ref[:, d:2 * d]
        dpre_a = dmi * y_attn * ga * (1.0 - ga)
        dpre_c = dmi * y_conv * gc * (1.0 - gc)
        drest_ref[:, 3 * cw:3 * cw + d] = dpre_a.astype(BF16)
        drest_ref[:, 3 * cw + d:3 * cw + 2 * d] = dpre_c.astype(BF16)
        dbias_ref[:, 0:d] += jnp.sum(dpre_a, axis=0, keepdims=True)
        dbias_ref[:, d:2 * d] += jnp.sum(dpre_c, axis=0, keepdims=True)

        dattn = (dmi * ga).astype(BF16)
        dattn_ref[...] = dattn
        do_ref[...] = _dot_nt(dattn, wao[...]).astype(BF16)
        dconvout = (dmi * gc).astype(BF16)
        dconvout_ref[...] = dconvout
        dconv_in = _dot_nt(dconvout, wco[...])
        drest_ref[:, 0:cw] = (dconv_in * cv).astype(BF16)

        dcv = dconv_in * cb
        following = dcv_next[...]
        dcm = wc[2:3, :] * dcv + wc[1:2, :] * _shift_up(dcv, following, 1) + wc[0:1, :] * _shift_up(dcv, following, 2)
        drest_ref[:, cw:2 * cw] = (dcm * conv[:, 2 * cw:3 * cw]).astype(BF16)
        drest_ref[:, 2 * cw:3 * cw] = (dcm * conv[:, cw:2 * cw]).astype(BF16)
        for tap, shifted in enumerate((cm2, cm1, cm)):
            dwc_ref[tap:tap + 1, :] += jnp.sum(dcv * shifted, axis=0, keepdims=True)
        dcv_next[...] = dcv[0:HALO, :]

    def rows(width):
        return pl.BlockSpec((tm, width), lambda i: (n - 1 - i, 0))

    prev_halo = pl.BlockSpec((HALO, 3 * cw), lambda i: (jnp.maximum((n - 1 - i) * per - 1, 0), 0))
    n_rest = 3 * cw + 2 * d
    return pl.pallas_call(
        body, name="mix_bwd", grid=(n,),
        in_specs=[rows(d), rows(d), rows(aw), rows(3 * cw), prev_halo, rows(2 * d), _const_spec((CONV_K, cw)),
                  _const_spec((1, d)), ANY, ANY, ANY],
        out_specs=[rows(d), rows(d), rows(d), rows(aw), rows(n_rest), _const_spec((1, d)), _const_spec((1, 2 * d)),
                   _const_spec((CONV_K, cw))],
        out_shape=[jax.ShapeDtypeStruct((s, d), BF16), jax.ShapeDtypeStruct((s, d), BF16),
                   jax.ShapeDtypeStruct((s, d), BF16), jax.ShapeDtypeStruct((s, aw), BF16),
                   jax.ShapeDtypeStruct((s, n_rest), BF16), jax.ShapeDtypeStruct((1, d), F32),
                   jax.ShapeDtypeStruct((1, 2 * d), F32), jax.ShapeDtypeStruct((CONV_K, cw), F32)],
        scratch_shapes=[pltpu.VMEM(w_ao.shape, BF16), pltpu.VMEM(w_co.shape, BF16), pltpu.VMEM(w_o.shape, BF16),
                        pltpu.VMEM((HALO, cw), F32), pltpu.SemaphoreType.DMA((3,))],
        compiler_params=_params(1),
    )(dx1, mixed, o, conv, conv, gate, wconv, g_post, w_ao, w_co, w_o)


def _mlp_ple_loss(x1, p, target, g_pre, g_post, g_ple, w_up, w_dn, w_pg, w_pp, tm):
    s, d = x1.shape
    ff = w_up.shape[1]
    pd = p.shape[1]
    fc = FF_CHUNK

    def body(x1_ref, p_ref, t_ref, gpre_ref, gpost_ref, gple_ref, wup_hbm, wdn_hbm, wpg_hbm, wpp_hbm,
             dx1_ref, h2_ref, du_ref, a_ref, df_ref, h3_ref, ds3_ref, dpp_ref, loss_ref, dgpre_ref, dgpost_ref,
             dgple_ref, wup, wdn, wpg, wpp, u_scr, sem):
        _load_resident([(wup_hbm, wup), (wdn_hbm, wdn), (wpg_hbm, wpg), (wpp_hbm, wpp)], sem)

        @pl.when(pl.program_id(0) == 0)
        def _():
            for ref in (loss_ref, dgpre_ref, dgpost_ref, dgple_ref):
                ref[...] = jnp.zeros_like(ref)

        x1v = x1_ref[...]
        r2 = _rms_scale(x1v)
        x1hat = x1v * r2
        h2 = (x1hat * gpre_ref[...]).astype(BF16)
        h2_ref[...] = h2
        f = jnp.zeros((tm, d), F32)
        for c0 in range(0, ff, fc):
            u = _dot(h2, wup[:, c0:c0 + fc])
            u_scr[:, c0:c0 + fc] = u
            a = jnp.square(jnp.maximum(u, 0.0)).astype(BF16)
            a_ref[:, c0:c0 + fc] = a
            f = f + _dot(a, wdn[c0:c0 + fc, :])
        rf = _rms_scale(f)
        fhat = f * rf
        x2 = x1v + fhat * gpost_ref[...]
        r3 = _rms_scale(x2)
        x2hat = x2 * r3
        h3 = (x2hat * gple_ref[...]).astype(BF16)
        h3_ref[...] = h3
        pg = _sigmoid(_dot(h3, wpg[...]))
        pp = _dot(p_ref[...].astype(BF16), wpp[...])
        diff = x2 + pg * pp - t_ref[...]
        loss_ref[...] += 0.5 * jnp.sum(jnp.mean(diff * diff, axis=-1, keepdims=True), axis=0, keepdims=True)

        dy = diff * (1.0 / d)
        dpp_ref[...] = (dy * pg).astype(BF16)
        ds3 = (dy * pp * pg * (1.0 - pg)).astype(BF16)
        ds3_ref[...] = ds3
        dh3 = _dot_nt(ds3, wpg[...])
        dgple_ref[...] += jnp.sum(dh3 * x2hat, axis=0, keepdims=True)
        dx2 = dy + _rms_bwd(x2hat, r3, gple_ref[...], dh3)
        dgpost_ref[...] += jnp.sum(dx2 * fhat, axis=0, keepdims=True)
        df = _rms_bwd(fhat, rf, gpost_ref[...], dx2).astype(BF16)
        df_ref[...] = df
        dh2 = jnp.zeros((tm, d), F32)
        for c0 in range(0, ff, fc):
            da = _dot_nt(df, wdn[c0:c0 + fc, :])
            du = (da * (2.0 * jnp.maximum(u_scr[:, c0:c0 + fc], 0.0))).astype(BF16)
            du_ref[:, c0:c0 + fc] = du
            dh2 = dh2 + _dot_nt(du, wup[:, c0:c0 + fc])
        dgpre_ref[...] += jnp.sum(dh2 * x1hat, axis=0, keepdims=True)
        dx1_ref[...] = dx2 + _rms_bwd(x1hat, r2, gpre_ref[...], dh2)

    vec = _const_spec((1, d))
    return pl.pallas_call(
        body, name="mlp_ple_loss", grid=(s // tm,),
        in_specs=[_row_spec(tm, d), _row_spec(tm, pd), _row_spec(tm, d), vec, vec, vec, ANY, ANY, ANY, ANY],
        out_specs=[_row_spec(tm, d), _row_spec(tm, d), _row_spec(tm, ff), _row_spec(tm, ff), _row_spec(tm, d),
                   _row_spec(tm, d), _row_spec(tm, d), _row_spec(tm, d), _const_spec((1, 1)), vec, vec, vec],
        out_shape=[jax.ShapeDtypeStruct((s, d), F32), jax.ShapeDtypeStruct((s, d), BF16),
                   jax.ShapeDtypeStruct((s, ff), BF16), jax.ShapeDtypeStruct((s, ff), BF16),
                   jax.ShapeDtypeStruct((s, d), BF16), jax.ShapeDtypeStruct((s, d), BF16),
                   jax.ShapeDtypeStruct((s, d), BF16), jax.ShapeDtypeStruct((s, d), BF16),
                   jax.ShapeDtypeStruct((1, 1), F32), jax.ShapeDtypeStruct((1, d), F32),
                   jax.ShapeDtypeStruct((1, d), F32), jax.ShapeDtypeStruct((1, d), F32)],
        scratch_shapes=[pltpu.VMEM(w_up.shape, BF16), pltpu.VMEM(w_dn.shape, BF16), pltpu.VMEM(w_pg.shape, BF16),
                        pltpu.VMEM(w_pp.shape, BF16), pltpu.VMEM((tm, ff), F32), pltpu.SemaphoreType.DMA((4,))],
        compiler_params=_params(1),
    )(x1, p, target, g_pre, g_post, g_ple, w_up, w_dn, w_pg, w_pp)


def _in_proj_bwd(x, dx1, pieces, g1, w_in, tm, exchange=None):
    s, d = x.shape
    ni = w_in.shape[1]
    widths = [p.shape[1] for p in pieces]
    grid = (s // tm,)
    ex = exchange or _NO_EXCHANGE

    def body(x_ref, dx1_ref, *rest):
        piece_refs, rest = rest[:len(pieces)], rest[len(pieces):]
        g_ref, w_hbm = rest[0], rest[1]
        ex_in, (dx_ref, dg_ref), ex_out, (w_vmem, sem, *sems) = _split_refs(rest[2:], ex, 2)
        _exchange_start(ex, ex_in, ex_out, sems, grid)
        _load_resident([(w_hbm, w_vmem)], sem)

        @pl.when(pl.program_id(0) == 0)
        def _():
            dg_ref[...] = jnp.zeros_like(dg_ref)

        dh = jnp.zeros((tm, d), F32)
        c0 = 0
        for ref, width in zip(piece_refs, widths):
            dh = dh + _dot_nt(ref[...], w_vmem[:, c0:c0 + width])
            c0 += width
        xv = x_ref[...]
        r = _rms_scale(xv)
        xhat = xv * r
        dg_ref[...] += jnp.sum(dh * xhat, axis=0, keepdims=True)
        dx_ref[...] = dx1_ref[...] + _rms_bwd(xhat, r, g_ref[...], dh)
        _exchange_wait(ex, ex_in, ex_out, sems, grid)

    return pl.pallas_call(
        body, name="in_proj_bwd", grid=grid,
        in_specs=[_row_spec(tm, d), _row_spec(tm, d)] + [_row_spec(tm, w) for w in widths]
        + [_const_spec((1, d)), ANY] + [ANY] * len(ex.arrays),
        out_specs=[_row_spec(tm, d), _const_spec((1, d))] + [ANY] * len(ex.out_shapes),
        out_shape=[jax.ShapeDtypeStruct((s, d), F32), jax.ShapeDtypeStruct((1, d), F32)] + ex.out_shapes,
        scratch_shapes=[pltpu.VMEM((d, ni), BF16), pltpu.SemaphoreType.DMA((1,))] + _exchange_sems(ex),
        compiler_params=_params(1),
    )(x, dx1, *pieces, g1, w_in, *ex.arrays)


def _weight_grad(a, b, name, into=None, col0=0, n_total=None):
    s, m = a.shape
    n = b.shape[1]
    tm, tk = min(m, DW_TILE), min(s, DW_TOKENS)
    tn = min(n, DW_TILE) if n_total is None else DW_PIECE_TILE
    nk = s // tk
    j0 = col0 // tn
    assert m % tm == 0 and n % tn == 0 and col0 % tn == 0

    def body(a_ref, b_ref, *rest):
        o_ref, acc = rest[-2:]
        k = pl.program_id(2)

        @pl.when(k == 0)
        def _():
            acc[...] = jnp.zeros_like(acc)

        acc[...] += _dot_tn(a_ref[...].astype(BF16), b_ref[...].astype(BF16))

        @pl.when(k == nk - 1)
        def _():
            o_ref[...] = acc[...].astype(BF16)

    extra = [] if into is None else [into]
    return pl.pallas_call(
        body, name=name, grid=(m // tm, n // tn, nk),
        in_specs=[pl.BlockSpec((tk, tm), lambda i, j, k: (k, i)), pl.BlockSpec((tk, tn), lambda i, j, k: (k, j))]
        + [ANY] * len(extra),
        out_specs=pl.BlockSpec((tm, tn), lambda i, j, k: (i, j0 + j)),
        out_shape=jax.ShapeDtypeStruct((m, n_total or n), BF16),
        input_output_aliases={2: 0} if extra else {},
        scratch_shapes=[pltpu.VMEM((tm, tn), F32)],
        compiler_params=_params(3),
    )(a, b, *extra)


def _mesh_position():
    return tuple(lax.axis_index(a) for a in MESH_AXES)


def _peer(me, k):
    bits = ((k >> 2) & 1, (k >> 1) & 1, k & 1)
    pos = tuple(1 - m if b else m for m, b in zip(me, bits))
    return pos, 4 * pos[0] + 2 * pos[1] + pos[2]


class _Exchange:
    def __init__(self, arrays, out_shapes, src, dst, relayed=None):
        self.arrays, self.out_shapes, self.src, self.dst = list(arrays), list(out_shapes), src, dst
        self.relayed = list(relayed) if relayed is not None else [False] * len(self.arrays)


_NO_EXCHANGE = _Exchange([], [], None, None)


def _exchange_sems(ex):
    n = len(ex.arrays)
    if n == 0:
        return []
    return [pltpu.SemaphoreType.DMA((n, N_DEV - 1)), pltpu.SemaphoreType.DMA((n, N_DEV - 1)),
            pltpu.SemaphoreType.DMA((n,))]


def _split_refs(rest, ex, n_own_outs):
    n_in, n_out = len(ex.arrays), len(ex.out_shapes)
    ex_in, rest = rest[:n_in], rest[n_in:]
    own, rest = rest[:n_own_outs], rest[n_own_outs:]
    return ex_in, own, rest[:n_out], rest[n_out:]


def _direct_steps(ex, w, in_refs, out_refs, sems):
    send_sems, recv_sems, local_sems = sems
    me = _mesh_position()
    mine = 4 * me[0] + 2 * me[1] + me[2]

    def copy(k):
        landing = ex.dst(w, out_refs, mine)
        if k == 0:
            return pltpu.make_async_copy(ex.src(w, in_refs, mine), landing, local_sems.at[w])
        peer, peer_idx = _peer(me, k)
        return pltpu.make_async_remote_copy(
            src_ref=ex.src(w, in_refs, peer_idx), dst_ref=landing, send_sem=send_sems.at[w, k - 1],
            recv_sem=recv_sems.at[w, k - 1], device_id=peer, device_id_type=pl.DeviceIdType.MESH)

    ks = range(N_DEV)
    return [lambda k=k: copy(k).start() for k in ks], [], [lambda k=k: copy(k).wait() for k in ks]


def _relayed_steps(ex, w, in_refs, out_refs, sems):
    send_sems, recv_sems, local_sems = sems
    x, y, c = _mesh_position()
    chips = [(1 - x, y), (x, 1 - y), (1 - x, 1 - y)]
    sibling = (x, y, 1 - c)
    js = range(len(chips))

    def block(px, py, pc):
        return ex.dst(w, out_refs, 4 * px + 2 * py + pc)

    def copy(k, dst, to, src=None):
        return pltpu.make_async_remote_copy(
            src_ref=ex.src(w, in_refs, None) if src is None else src, dst_ref=dst, send_sem=send_sems.at[w, k],
            recv_sem=recv_sems.at[w, k], device_id=to, device_id_type=pl.DeviceIdType.MESH)

    def local():
        return pltpu.make_async_copy(ex.src(w, in_refs, None), block(x, y, c), local_sems.at[w])

    def own(k):
        return copy(k, block(x, y, c), sibling if k == 0 else (*chips[k - 1], c))

    def came(j):
        return copy(1 + j, block(*chips[j], c), (*chips[j], c))

    def passed(j):
        return copy(4 + j, block(*chips[j], c), sibling, src=block(*chips[j], c))

    def from_sibling(k):
        return copy(k, block(x, y, 1 - c) if k == 0 else block(*chips[k - 4], 1 - c), sibling)

    start = [lambda: local().start()] + [lambda k=k: own(k).start() for k in range(4)]
    relay = [step for j in js for step in (lambda j=j: came(j).wait_recv(), lambda j=j: passed(j).start())]
    finish = ([lambda: local().wait()] + [lambda k=k: own(k).wait_send() for k in range(4)]
              + [lambda j=j: passed(j).wait_send() for j in js]
              + [lambda k=k: from_sibling(k).wait_recv() for k in (0, 4, 5, 6)])
    return start, relay, finish


def _exchange_steps(ex, in_refs, out_refs, sems):
    start, relay, finish = [], [], []
    for w in range(len(ex.arrays)):
        steps = (_relayed_steps if ex.relayed[w] else _direct_steps)(ex, w, in_refs, out_refs, sems)
        start += steps[0]
        relay += steps[1]
        finish += steps[2]
    return start, relay, finish


def _run(steps):
    for step in steps:
        step()


def _at_grid_step(grid, where):
    target = {"first": [0] * len(grid), "middle": [grid[0] // 2] + [0] * (len(grid) - 1),
              "last": [g - 1 for g in grid]}[where]
    hit = pl.program_id(0) == target[0]
    for axis in range(1, len(grid)):
        hit = jnp.logical_and(hit, pl.program_id(axis) == target[axis])
    return hit


def _exchange_start(ex, in_refs, out_refs, sems, grid):
    if ex.arrays:
        @pl.when(_at_grid_step(grid, "first"))
        def _():
            _run(_exchange_steps(ex, in_refs, out_refs, sems)[0])

        if any(ex.relayed):
            assert grid[0] >= 2

            @pl.when(_at_grid_step(grid, "middle"))
            def _():
                _run(_exchange_steps(ex, in_refs, out_refs, sems)[1])


def _exchange_wait(ex, in_refs, out_refs, sems, grid):
    if ex.arrays:
        @pl.when(_at_grid_step(grid, "last"))
        def _():
            _run(_exchange_steps(ex, in_refs, out_refs, sems)[2])


def _shard_block(ref, shard_shape, by_col, idx):
    r, c = shard_shape
    if by_col:
        return ref.at[:, pl.ds(pl.multiple_of(idx * c, LANES), c)]
    return ref.at[pl.ds(pl.multiple_of(idx * r, 16), r), :]


def _full_shape(shard_shape, by_col):
    r, c = shard_shape
    return (r, N_DEV * c) if by_col else (N_DEV * r, c)


def _gather_exchange(shards, col_sharded):
    shapes = [a.shape for a in shards]
    return _Exchange(
        shards, [jax.ShapeDtypeStruct(_full_shape(sh, bc), a.dtype) for a, sh, bc in zip(shards, shapes, col_sharded)],
        lambda w, refs, idx: refs[w],
        lambda w, refs, idx: _shard_block(refs[w], shapes[w], col_sharded[w], idx), [True] * len(shards))


def _scatter_exchange(grads, col_sharded):
    shapes = []
    for g, by_col in zip(grads, col_sharded):
        r, c = g.shape
        shapes.append((r, c // N_DEV) if by_col else (r // N_DEV, c))
    return _Exchange(
        grads, [jax.ShapeDtypeStruct((N_DEV,) + sh, g.dtype) for g, sh in zip(grads, shapes)],
        lambda w, refs, idx: _shard_block(refs[w], shapes[w], col_sharded[w], idx),
        lambda w, refs, mine: refs[w].at[mine])


def _broadcast_exchange(arrays):
    return _Exchange(arrays, [jax.ShapeDtypeStruct((N_DEV,) + a.shape, a.dtype) for a in arrays],
                     lambda w, refs, idx: refs[w], lambda w, refs, mine: refs[w].at[mine])


def _join(*exs):
    arrays, shapes, owner = [], [], []
    for e in exs:
        for w in range(len(e.arrays)):
            owner.append((e, w, len(arrays), len(shapes)))
        arrays += e.arrays
        shapes += e.out_shapes

    def src(w, refs, idx):
        e, w0, i0, _ = owner[w]
        return e.src(w0, refs[i0:i0 + len(e.arrays)], idx)

    def dst(w, refs, idx):
        e, w0, _, o0 = owner[w]
        return e.dst(w0, refs[o0:o0 + len(e.out_shapes)], idx)

    return _Exchange(arrays, shapes, src, dst, [flag for e in exs for flag in e.relayed])


def _exchange_call(ex, name):
    n_in = len(ex.arrays)

    def body(*refs):
        in_refs, _, out_refs, sems = _split_refs(refs, ex, 0)
        for steps in _exchange_steps(ex, in_refs, out_refs, sems):
            _run(steps)

    return pl.pallas_call(
        body, name=name, in_specs=[ANY] * n_in, out_specs=[ANY] * len(ex.out_shapes), out_shape=ex.out_shapes,
        scratch_shapes=_exchange_sems(ex), compiler_params=pltpu.CompilerParams(vmem_limit_bytes=VMEM_LIMIT),
    )(*ex.arrays)


def _to_bf16(arrays):
    def body(*refs):
        for src, dst in zip(refs[:len(arrays)], refs[len(arrays):]):
            dst[...] = src[...].astype(BF16)

    vmem = pl.BlockSpec(memory_space=pltpu.VMEM)
    return pl.pallas_call(
        body, name="weights_to_bf16", in_specs=[vmem] * len(arrays), out_specs=[vmem] * len(arrays),
        out_shape=[jax.ShapeDtypeStruct(a.shape, BF16) for a in arrays],
        compiler_params=pltpu.CompilerParams(vmem_limit_bytes=VMEM_LIMIT),
    )(*arrays)


def _adamw(w, g, m, v):
    m = ADAM_B1 * m + (1.0 - ADAM_B1) * g
    v = ADAM_B2 * v + (1.0 - ADAM_B2) * jnp.square(g)
    m_hat = m / (1.0 - ADAM_B1 ** ADAM_STEP)
    v_hat = v / (1.0 - ADAM_B2 ** ADAM_STEP)
    delta = -ADAM_LR * (m_hat / (jnp.sqrt(v_hat) + ADAM_EPS) + ADAM_WD * w)
    return delta, m, v


def _sum_and_adamw(parts, w, m, v, name):
    r, c = w.shape
    tr = min(r, 256)

    def body(p_ref, w_ref, m_ref, v_ref, g_out, d_out, m_out, v_out):
        g = p_ref[0].astype(F32)
        for dev in range(1, N_DEV):
            g = g + p_ref[dev].astype(F32)
        g_out[...] = g
        d_out[...], m_out[...], v_out[...] = _adamw(w_ref[...], g, m_ref[...], v_ref[...])

    blk = pl.BlockSpec((tr, c), lambda i: (i, 0))
    return pl.pallas_call(
        body, name=name, grid=(r // tr,),
        in_specs=[pl.BlockSpec((N_DEV, tr, c), lambda i: (0, i, 0)), blk, blk, blk],
        out_specs=[blk] * 4, out_shape=[jax.ShapeDtypeStruct((r, c), F32)] * 4,
        compiler_params=_params(1),
    )(parts, w, m, v)


BIG = ("w_in", "w_attn_out", "w_conv_out", "w_o", "w_up", "w_down", "w_ple_gate", "w_ple_proj")
COL_SHARDED = {"w_in": True, "w_attn_out": True, "w_conv_out": True, "w_o": False, "w_up": True, "w_down": False,
               "w_ple_gate": False, "w_ple_proj": True}
SMALL = ("g_pre_mix", "b_gate", "g_post_mix", "g_pre_mlp", "g_post_mlp", "g_ple")


REST = BIG[1:]


def _local_grads(x, p, target, small, wconv, full, aw, cw, tm, t, gather_rest=None, scatter_rest=None,
                 scatter_in=None):
    full = dict(full)
    qkv, conv, gate, h1 = _in_proj_fwd(x, small["g_pre_mix"], small["b_gate"], full["w_in"], aw, cw, tm)
    o, *rest = _attn_fwd(qkv, aw, t, gather_rest)
    full.update(zip(REST, rest))
    x1, mixed, mix_in, conv_in = _mix_fwd(x, o, conv, gate, wconv, small["g_post_mix"], full["w_attn_out"],
                                          full["w_conv_out"], full["w_o"], tm)
    (dx1, h2, du, a, df, h3, ds3, dpp, loss, dg_pre_mlp, dg_post_mlp, dg_ple) = _mlp_ple_loss(
        x1, p, target, small["g_pre_mlp"], small["g_post_mlp"], small["g_ple"], full["w_up"], full["w_down"],
        full["w_ple_gate"], full["w_ple_proj"], tm)
    big = {"w_up": _weight_grad(h2, du, "dw_up"), "w_down": _weight_grad(a, df, "dw_down"),
           "w_ple_gate": _weight_grad(h3, ds3, "dw_ple_gate"), "w_ple_proj": _weight_grad(p, dpp, "dw_ple_proj")}
    (dmixed, dattn, dconvout, do, drest, dg_post_mix, db_gate, dwconv) = _mix_bwd(
        dx1, mixed, o, conv, gate, wconv, small["g_post_mix"], full["w_attn_out"], full["w_conv_out"], full["w_o"],
        tm)
    big.update({"w_attn_out": _weight_grad(o, dattn, "dw_attn_out"),
                "w_conv_out": _weight_grad(conv_in, dconvout, "dw_conv_out"),
                "w_o": _weight_grad(mix_in, dmixed, "dw_o")})
    dq, dk, dv, *scattered = _attn_bwd(qkv, o, do, aw, t, scatter_rest and scatter_rest([big[n] for n in REST]))
    pieces = [dq, dk, dv, drest]
    dw_in, col0, ni = None, 0, full["w_in"].shape[1]
    for i, piece in enumerate(pieces):
        dw_in = _weight_grad(h1, piece, "dw_in_%d" % i, dw_in, col0, ni)
        col0 += piece.shape[1]
    big["w_in"] = dw_in
    dx, dg_pre_mix, *scattered_in = _in_proj_bwd(x, dx1, pieces, small["g_pre_mix"], full["w_in"], tm,
                                                scatter_in and scatter_in(dw_in))
    small_grads = {"g_pre_mix": dg_pre_mix, "b_gate": db_gate, "g_post_mix": dg_post_mix, "g_pre_mlp": dg_pre_mlp,
                   "g_post_mlp": dg_post_mlp, "g_ple": dg_ple, "w_conv": dwconv}
    return loss[0, 0], dx, big, small_grads, scattered_in + scattered


def _pack_small(vals, d):
    parts = []
    for a in vals:
        a = jnp.pad(a.reshape(-1), (0, -a.size % d)).reshape(-1, d)
        parts.append(jnp.pad(a, ((0, HALO - a.shape[0]), (0, 0))))
    return jnp.concatenate(parts, axis=0)


def _unpack_small(pack, shapes, d):
    out = []
    for i, shp in enumerate(shapes):
        n = 1
        for v in shp:
            n *= v
        rows = -(-n // d)
        out.append(pack[i * HALO:i * HALO + rows].reshape(-1)[:n].reshape(shp))
    return out


def kernel(x, p, g_pre_mix, w_in, b_gate, w_conv, w_attn_out, w_conv_out, w_o, g_post_mix, g_pre_mlp, w_up, w_down, g_post_mlp, g_ple, w_ple_gate, w_ple_proj, loss_target, m_g_pre_mix, m_w_in, m_b_gate, m_w_conv, m_w_attn_out, m_w_conv_out, m_w_o, m_g_post_mix, m_g_pre_mlp, m_w_up, m_w_down, m_g_post_mlp, m_g_ple, m_w_ple_gate, m_w_ple_proj, v_g_pre_mix, v_w_in, v_b_gate, v_w_conv, v_w_attn_out, v_w_conv_out, v_w_o, v_g_post_mix, v_g_pre_mlp, v_w_up, v_w_down, v_g_post_mlp, v_g_ple, v_w_ple_gate, v_w_ple_proj):
    given = dict(locals())
    order = ["g_pre_mix", "w_in", "b_gate", "w_conv", "w_attn_out", "w_conv_out", "w_o", "g_post_mix", "g_pre_mlp",
             "w_up", "w_down", "g_post_mlp", "g_ple", "w_ple_gate", "w_ple_proj"]
    d = x.shape[-1]
    me = 4 * lax.axis_index("x") + 2 * lax.axis_index("y") + lax.axis_index("c")

    col = [COL_SHARDED[n] for n in BIG]
    shards = _to_bf16([given[n][0] for n in BIG])
    cw_shard = w_conv.shape[-1]
    conv_tile = jnp.pad(w_conv[0], ((0, HALO - CONV_K), (0, LANES - cw_shard)))
    w_in_full, conv_g = _exchange_call(
        _join(_gather_exchange(shards[:1], col[:1]), _broadcast_exchange([conv_tile])), "gather_w_in")
    wconv = jnp.concatenate([conv_g[dev, :CONV_K, :cw_shard] for dev in range(N_DEV)], axis=1)

    small = {n: given[n] for n in SMALL}
    loss, dx, big_grads, small_grads, parts = _local_grads(
        x[0], p[0, 0], loss_target[0], small, wconv, {"w_in": w_in_full}, w_attn_out.shape[1], w_conv_out.shape[1],
        ROW_BLOCK, ATTN_BLOCK,
        _gather_exchange(shards[1:], col[1:]), lambda grads: _scatter_exchange(grads, col[1:]),
        lambda grad: _scatter_exchange([grad], col[:1]))
    loss = lax.psum(loss, MESH_AXES)

    small_names = list(SMALL) + ["w_conv"]
    pack = _pack_small([small_grads[n] for n in small_names], d)
    packs, = _exchange_call(_broadcast_exchange([pack]), "share_small_grads")

    grads, deltas, new_m, new_v = {}, {}, {}, {}
    for n, part in zip(BIG, parts):
        grads[n], deltas[n], new_m[n], new_v[n] = (
            a[None] for a in _sum_and_adamw(part, given[n][0], given["m_" + n][0], given["v_" + n][0], "adamw_" + n))

    full_conv = lambda a: lax.dynamic_update_slice(jnp.zeros((CONV_K, N_DEV * cw_shard), F32), a[0],
                                                   (jnp.int32(0), me * cw_shard))
    state = [_pack_small([given[pre + n] for n in SMALL] + [full_conv(given[pre + "w_conv"])], d)
             for pre in ("", "m_", "v_")]
    outs = _sum_and_adamw(packs, *state, "adamw_small")
    shapes = [given[n].shape for n in SMALL] + [(CONV_K, N_DEV * cw_shard)]
    for res, dst in zip(outs, (grads, deltas, new_m, new_v)):
        for n, a in zip(small_names, _unpack_small(res, shapes, d)):
            dst[n] = (lax.dynamic_slice(a, (jnp.int32(0), me * cw_shard), (CONV_K, cw_shard))[None]
                      if n == "w_conv" else a)

    return (loss, dx[None], *[grads[n] for n in order], *[deltas[n] for n in order],
            *[new_m[n] for n in order], *[new_v[n] for n in order])
```

```python
import jax
import jax.numpy as jnp
from jax import lax
from jax.experimental import pallas as pl
from jax.experimental.pallas import tpu as pltpu

F32 = jnp.float32
BF16 = jnp.bfloat16
RMS_EPS = 1e-6
N_DEV = 8
MESH_AXES = ("x", "y", "c")
LANES = 128
HEAD_DIM = 64
HEADS_PER_GROUP = LANES // HEAD_DIM
CONV_K = 3
HALO = 8
VMEM_LIMIT = 56 * 1024 * 1024
EXP2_ZERO = -150.0
LOG2_E = 1.4426950408889634

ADAM_LR = 0.001
ADAM_B1 = 0.9
ADAM_B2 = 0.999
ADAM_EPS = 1e-08
ADAM_WD = 0.01
ADAM_STEP = 10

ROW_BLOCK = 256
ATTN_BLOCK = 256
ATTN_ROW_SPLITS = 2
DW_TOKENS = 2048
DW_TILE = 1024
DW_PIECE_TILE = 512
FF_CHUNK = 1024
PROJ_CHUNK = 512


def _dot(a, b):
    return lax.dot_general(a, b, (((1,), (0,)), ((), ())), preferred_element_type=F32)


def _dot_nt(a, b):
    return lax.dot_general(a, b, (((1,), (1,)), ((), ())), preferred_element_type=F32)


def _dot_tn(a, b):
    return lax.dot_general(a, b, (((0,), (0,)), ((), ())), preferred_element_type=F32)


def _sigmoid(z):
    return 1.0 / (1.0 + jnp.exp(-z))


def _rms_scale(x):
    return lax.rsqrt(jnp.mean(x * x, axis=-1, keepdims=True) + RMS_EPS)


def _rms_bwd(xhat, r, g, dy):
    gd = dy * g
    return r * (gd - xhat * jnp.mean(gd * xhat, axis=-1, keepdims=True))


def _params(n_axes, **kw):
    return pltpu.CompilerParams(dimension_semantics=("arbitrary",) * n_axes, vmem_limit_bytes=VMEM_LIMIT, **kw)


def _load_resident(pairs, sem):
    @pl.when(pl.program_id(0) == 0)
    def _():
        copies = [pltpu.make_async_copy(src, dst, sem.at[i]) for i, (src, dst) in enumerate(pairs)]
        for cp in copies:
            cp.start()
        for cp in copies:
            cp.wait()


def _row_spec(tm, width):
    return pl.BlockSpec((tm, width), lambda i: (i, 0))


def _prev_halo_spec(tm, width):
    per = tm // HALO
    return pl.BlockSpec((HALO, width), lambda i: (jnp.maximum(i * per - 1, 0), 0))


def _const_spec(shape):
    return pl.BlockSpec(shape, lambda i: (0,) * len(shape))


ANY = pl.BlockSpec(memory_space=pl.ANY)


def _shift_down(cur, prev, n):
    rows = lax.broadcasted_iota(jnp.int32, cur.shape, 0)
    out = pltpu.roll(cur, n, 0)
    for j in range(n):
        out = jnp.where(rows == j, prev[HALO - n + j:HALO - n + j + 1, :], out)
    return out


def _shift_up(cur, nxt, n):
    tm = cur.shape[0]
    rows = lax.broadcasted_iota(jnp.int32, cur.shape, 0)
    out = pltpu.roll(cur, tm - n, 0)
    for j in range(n):
        out = jnp.where(rows == tm - n + j, nxt[j:j + 1, :], out)
    return out


def _conv_taps(cm, cm_prev, wconv):
    cm1 = _shift_down(cm, cm_prev, 1)
    cm2 = _shift_down(cm, cm_prev, 2)
    cv = wconv[2:3, :] * cm + wconv[1:2, :] * cm1 + wconv[0:1, :] * cm2
    return cv, cm1, cm2


def _in_proj_fwd(x, g1, b_gate, w_in, aw, cw, tm):
    s, d = x.shape
    ni = w_in.shape[1]
    n_qkv, n_conv = 3 * aw, 3 * cw
    ch = PROJ_CHUNK

    def body(x_ref, g_ref, b_ref, w_hbm, qkv_ref, conv_ref, gate_ref, h_ref, w_vmem, sem):
        _load_resident([(w_hbm, w_vmem)], sem)
        xv = x_ref[...]
        h = (xv * _rms_scale(xv) * g_ref[...]).astype(BF16)
        h_ref[...] = h
        for c0 in range(0, ni, ch):
            pc = _dot(h, w_vmem[:, c0:c0 + ch])
            if c0 < n_qkv:
                qkv_ref[:, c0:c0 + ch] = pc.astype(BF16)
            elif c0 < n_qkv + n_conv:
                conv_ref[:, c0 - n_qkv:c0 - n_qkv + ch] = pc
            else:
                g0 = c0 - n_qkv - n_conv
                gate_ref[:, g0:g0 + ch] = _sigmoid(pc + b_ref[:, g0:g0 + ch])

    return pl.pallas_call(
        body, name="in_proj_fwd", grid=(s // tm,),
        in_specs=[_row_spec(tm, d), _const_spec((1, d)), _const_spec((1, 2 * d)), ANY],
        out_specs=[_row_spec(tm, n_qkv), _row_spec(tm, n_conv), _row_spec(tm, 2 * d), _row_spec(tm, d)],
        out_shape=[jax.ShapeDtypeStruct((s, n_qkv), BF16), jax.ShapeDtypeStruct((s, n_conv), F32),
                   jax.ShapeDtypeStruct((s, 2 * d), F32), jax.ShapeDtypeStruct((s, d), BF16)],
        scratch_shapes=[pltpu.VMEM((d, ni), BF16), pltpu.SemaphoreType.DMA((1,))],
        compiler_params=_params(1),
    )(x, g1, b_gate, w_in)


def _split_hi_lo(a):
    hi = a.astype(BF16)
    return hi, (a - hi.astype(F32)).astype(BF16)


def _log2_gates(z):
    z2 = z * LOG2_E
    nz2 = -z2
    log_keep = jnp.minimum(nz2, 0.0) - jnp.log2(1.0 + jnp.exp2(jnp.minimum(z2, nz2)))
    return log_keep + z2, log_keep


def _attn_masks(t):
    row = lax.broadcasted_iota(jnp.int32, (t, t), 0)
    col = lax.broadcasted_iota(jnp.int32, (t, t), 1)
    return col < row, (row > col).astype(BF16), (row >= col).astype(BF16)


def _chains(a):
    tr = a.shape[0] // ATTN_ROW_SPLITS
    return [jnp.where(_head_lanes(h), a[r * tr:(r + 1) * tr], jnp.zeros((tr, LANES), a.dtype))
            for h in range(HEADS_PER_GROUP) for r in range(ATTN_ROW_SPLITS)]


def _merge_chains(parts):
    rows = []
    for r in range(ATTN_ROW_SPLITS):
        out = parts[r]
        for h in range(1, HEADS_PER_GROUP):
            out = jnp.where(_head_lanes(h), parts[h * ATTN_ROW_SPLITS + r], out)
        rows.append(out)
    return jnp.concatenate(rows, axis=0)


def _by_stage(n_chains, stages):
    for stage in stages:
        for c in range(n_chains):
            stage(c)


def _row_parts(a):
    tr = a.shape[0] // ATTN_ROW_SPLITS
    return [a[r * tr:(r + 1) * tr] for r in range(ATTN_ROW_SPLITS)]


def _while_weights_live(qi, block, carry):
    def cond(state):
        j, carry = state
        live = jnp.max(carry[0][0])
        for run in carry[0][1:]:
            live = jnp.maximum(live, jnp.max(run))
        return jnp.logical_and(j < qi, live >= EXP2_ZERO)

    def step(state):
        j, carry = state
        return j + 1, block(qi - 1 - j, carry)

    return lax.while_loop(cond, step, (jnp.int32(0), carry))[1]


def _head_lanes(h):
    lane = lax.broadcasted_iota(jnp.int32, (1, LANES), 1)
    return (lane >= HEAD_DIM * h) & (lane < HEAD_DIM * (h + 1))


def _attn_fwd(qkv, aw, t, exchange=None):
    s = qkv.shape[0]
    groups = aw // LANES
    nq = s // t
    scale = HEAD_DIM ** -0.5
    ex = exchange or _NO_EXCHANGE

    def body(q_ref, k_ref, v_ref, *rest):
        ex_in, (o_ref,), ex_out, sems = _split_refs(rest, ex, 1)
        qi = pl.program_id(1)
        _exchange_start(ex, ex_in, ex_out, sems, (groups, nq))
        causal, upper, _ = _attn_masks(t)
        causal = _row_parts(causal) * HEADS_PER_GROUP
        qs = _chains(q_ref[...] * scale)
        heads = range(len(qs))
        tr = t // ATTN_ROW_SPLITS

        def block(kb, runs, accs, diag):
            rows = pl.ds(pl.multiple_of(kb * t, t), t)
            k = k_ref[rows, :]
            v = v_ref[rows, :]
            ncs = [(h % ATTN_ROW_SPLITS + 1) * tr if diag else t for h in heads]
            live = [{} for _ in heads]
            new_runs, new_accs = [None] * len(heads), [None] * len(heads)

            def scores(h):
                live[h]["z"] = _dot_nt(qs[h], k[0:ncs[h]])

            def gates(h):
                nc = ncs[h]
                log_b, log_keep = _log2_gates(live[h].pop("z"))
                if diag:
                    log_keep = jnp.where(causal[h][:, 0:nc], log_keep, 0.0)
                hi, lo = _split_hi_lo(log_keep)
                live[h]["log_w"] = log_b + runs[h]
                live[h]["between"] = _dot(hi, upper[0:nc, 0:nc]) + _dot(lo, upper[0:nc, 0:nc])
                new_runs[h] = runs[h] + jnp.sum(log_keep, axis=1, keepdims=True)

            def weights(h):
                nc = ncs[h]
                w = jnp.exp2(live[h].pop("log_w") + live[h].pop("between"))
                if diag:
                    w = jnp.where(causal[h][:, 0:nc], w, 0.0)
                new_accs[h] = accs[h] + _dot(w.astype(BF16), v[0:nc])

            _by_stage(len(heads), [scores, gates, weights])
            return tuple(new_runs), tuple(new_accs)

        carry = block(qi, [jnp.zeros((tr, 1), F32)] * len(heads), [jnp.zeros((tr, LANES), F32)] * len(heads), True)
        _, accs = _while_weights_live(qi, lambda kb, carry: block(kb, *carry, False), carry)
        o_ref[...] = _merge_chains(accs)
        _exchange_wait(ex, ex_in, ex_out, sems, (groups, nq))

    return pl.pallas_call(
        body, name="attn_fwd", grid=(groups, nq),
        in_specs=[pl.BlockSpec((t, LANES), lambda g, i: (i, g)),
                  pl.BlockSpec((s, LANES), lambda g, i: (0, groups + g)),
                  pl.BlockSpec((s, LANES), lambda g, i: (0, 2 * groups + g))] + [ANY] * len(ex.arrays),
        out_specs=[pl.BlockSpec((t, LANES), lambda g, i: (i, g))] + [ANY] * len(ex.out_shapes),
        out_shape=[jax.ShapeDtypeStruct((s, aw), F32)] + ex.out_shapes,
        scratch_shapes=_exchange_sems(ex),
        compiler_params=_params(2),
    )(qkv, qkv, qkv, *ex.arrays)


def _attn_bwd(qkv, o, do, aw, t, exchange=None):
    s = qkv.shape[0]
    groups = aw // LANES
    nq = s // t
    scale = HEAD_DIM ** -0.5
    ex = exchange or _NO_EXCHANGE

    def body(q_ref, k_ref, v_ref, o_ref, do_ref, *rest):
        ex_in, (dq_ref, dk_ref, dv_ref), ex_out, (dk_acc, dv_acc, *sems) = _split_refs(rest, ex, 3)
        qi = pl.program_id(1)
        _exchange_start(ex, ex_in, ex_out, sems, (groups, nq))

        @pl.when(qi == 0)
        def _():
            dk_acc[...] = jnp.zeros_like(dk_acc)
            dv_acc[...] = jnp.zeros_like(dv_acc)

        causal, upper, lower_incl = _attn_masks(t)
        causal = _row_parts(causal) * HEADS_PER_GROUP
        q = q_ref[...] * scale
        do_b = do_ref[...]
        qs = _chains(q)
        dos = _chains(do_b)
        qs_all = jnp.concatenate(qs, axis=0)
        dos_all = jnp.concatenate(dos, axis=0)
        e_totals = [jnp.sum(part, axis=1, keepdims=True) for part in _chains(do_b.astype(F32) * o_ref[...])]
        heads = range(len(qs))
        tr = t // ATTN_ROW_SPLITS

        def block(kb, runs, e_runs, dqs, diag):
            rows = pl.ds(pl.multiple_of(kb * t, t), t)
            k = k_ref[rows, :]
            v = v_ref[rows, :]
            ncs = [(h % ATTN_ROW_SPLITS + 1) * tr if diag else t for h in heads]
            live = [{} for _ in heads]
            none = [None] * len(heads)
            new_runs, new_e_runs, new_dqs, dzbs, wbs = list(none), list(none), list(none), list(none), list(none)

            def scores(h):
                live[h]["z"] = _dot_nt(qs[h], k[0:ncs[h]])
                live[h]["dw"] = _dot_nt(dos[h], v[0:ncs[h]])

            def gates(h):
                nc = ncs[h]
                log_b, log_keep = _log2_gates(live[h].pop("z"))
                live[h]["beta"] = jnp.exp2(log_b)
                live[h]["keep"] = jnp.exp2(log_keep)
                if diag:
                    log_keep = jnp.where(causal[h][:, 0:nc], log_keep, 0.0)
                hi, lo = _split_hi_lo(log_keep)
                live[h]["log_w"] = log_b + runs[h]
                live[h]["between"] = _dot(hi, upper[0:nc, 0:nc]) + _dot(lo, upper[0:nc, 0:nc])
                new_runs[h] = runs[h] + jnp.sum(log_keep, axis=1, keepdims=True)

            def weights(h):
                nc = ncs[h]
                w = jnp.exp2(live[h].pop("log_w") + live[h].pop("between"))
                if diag:
                    w = jnp.where(causal[h][:, 0:nc], w, 0.0)
                wb = w.astype(BF16)
                e = live[h].pop("dw") * wb.astype(F32)
                hi, lo = _split_hi_lo(e)
                live[h]["e"] = e
                live[h]["e_suffix"] = _dot(hi, lower_incl[0:nc, 0:nc]) + _dot(lo, lower_incl[0:nc, 0:nc]) + e_runs[h]
                wbs[h] = wb

            def score_grads(h):
                nc = ncs[h]
                e_suffix = live[h].pop("e_suffix")
                dz = live[h].pop("e") * live[h].pop("keep") - (e_totals[h] - e_suffix) * live[h].pop("beta")
                if diag:
                    dz = jnp.where(causal[h][:, 0:nc], dz, 0.0)
                dzb = dz.astype(BF16)
                new_dqs[h] = dqs[h] + _dot(dzb, k[0:nc])
                new_e_runs[h] = e_suffix[:, 0:1]
                if nc < t:
                    unseen = jnp.zeros((tr, t - nc), BF16)
                    dzb = jnp.concatenate([dzb, unseen], axis=1)
                    wbs[h] = jnp.concatenate([wbs[h], unseen], axis=1)
                dzbs[h] = dzb

            _by_stage(len(heads), [scores, gates, weights, score_grads])
            dk_acc[rows, :] += _dot_tn(jnp.concatenate(dzbs, axis=0), qs_all)
            dv_acc[rows, :] += _dot_tn(jnp.concatenate(wbs, axis=0), dos_all)
            return tuple(new_runs), tuple(new_e_runs), tuple(new_dqs)

        zero_cols = [jnp.zeros((tr, 1), F32)] * len(heads)
        carry = block(qi, zero_cols, zero_cols, [jnp.zeros((tr, LANES), F32)] * len(heads), True)
        _, _, dqs = _while_weights_live(qi, lambda kb, carry: block(kb, *carry, False), carry)
        dq_ref[...] = (_merge_chains(dqs) * scale).astype(BF16)

        @pl.when(qi == nq - 1)
        def _():
            dk_ref[...] = dk_acc[...].astype(BF16)
            dv_ref[...] = dv_acc[...].astype(BF16)

        _exchange_wait(ex, ex_in, ex_out, sems, (groups, nq))

    blk = pl.BlockSpec((t, LANES), lambda g, i: (i, g))
    slab = pl.BlockSpec((s, LANES), lambda g, i: (0, g))
    return pl.pallas_call(
        body, name="attn_bwd", grid=(groups, nq),
        in_specs=[blk, pl.BlockSpec((s, LANES), lambda g, i: (0, groups + g)),
                  pl.BlockSpec((s, LANES), lambda g, i: (0, 2 * groups + g)), blk, blk] + [ANY] * len(ex.arrays),
        out_specs=[blk, slab, slab] + [ANY] * len(ex.out_shapes),
        out_shape=[jax.ShapeDtypeStruct((s, aw), BF16)] * 3 + ex.out_shapes,
        scratch_shapes=[pltpu.VMEM((s, LANES), F32), pltpu.VMEM((s, LANES), F32)] + _exchange_sems(ex),
        compiler_params=_params(2),
    )(qkv, qkv, qkv, o, do, *ex.arrays)


def _branches(o_b, conv, conv_prev, wconv, w_ao, w_co, cw, first):
    cb = conv[:, 0:cw]
    cm = conv[:, cw:2 * cw] * conv[:, 2 * cw:3 * cw]
    cm_prev = conv_prev[:, cw:2 * cw] * conv_prev[:, 2 * cw:3 * cw]
    cm_prev = jnp.where(first, 0.0, cm_prev)
    cv, cm1, cm2 = _conv_taps(cm, cm_prev, wconv)
    conv_in = (cb * cv).astype(BF16)
    return _dot(o_b, w_ao), _dot(conv_in, w_co), conv_in, cb, cv, cm, cm1, cm2


def _mix_fwd(x, o, conv, gate, wconv, g_post, w_ao, w_co, w_o, tm):
    s, d = x.shape
    aw, cw = w_ao.shape[0], w_co.shape[0]

    def body(x_ref, o_ref, conv_ref, prev_ref, gate_ref, wc_ref, g_ref, wao_hbm, wco_hbm, wo_hbm,
             x1_ref, mixed_ref, mixin_ref, convin_ref, wao, wco, wo, sem):
        _load_resident([(wao_hbm, wao), (wco_hbm, wco), (wo_hbm, wo)], sem)
        y_attn, y_conv, conv_in, *_ = _branches(
            o_ref[...].astype(BF16), conv_ref[...], prev_ref[...], wc_ref[...], wao[...], wco[...], cw,
            pl.program_id(0) == 0)
        mix_in = (gate_ref[:, 0:d] * y_attn + gate_ref[:, d:2 * d] * y_conv).astype(BF16)
        mixed = _dot(mix_in, wo[...])
        x1_ref[...] = x_ref[...] + mixed * _rms_scale(mixed) * g_ref[...]
        mixed_ref[...] = mixed
        mixin_ref[...] = mix_in
        convin_ref[...] = conv_in

    return pl.pallas_call(
        body, name="mix_fwd", grid=(s // tm,),
        in_specs=[_row_spec(tm, d), _row_spec(tm, aw), _row_spec(tm, 3 * cw), _prev_halo_spec(tm, 3 * cw),
                  _row_spec(tm, 2 * d), _const_spec((CONV_K, cw)), _const_spec((1, d)), ANY, ANY, ANY],
        out_specs=[_row_spec(tm, d), _row_spec(tm, d), _row_spec(tm, d), _row_spec(tm, cw)],
        out_shape=[jax.ShapeDtypeStruct((s, d), F32), jax.ShapeDtypeStruct((s, d), F32),
                   jax.ShapeDtypeStruct((s, d), BF16), jax.ShapeDtypeStruct((s, cw), BF16)],
        scratch_shapes=[pltpu.VMEM(w_ao.shape, BF16), pltpu.VMEM(w_co.shape, BF16), pltpu.VMEM(w_o.shape, BF16),
                        pltpu.SemaphoreType.DMA((3,))],
        compiler_params=_params(1),
    )(x, o, conv, conv, gate, wconv, g_post, w_ao, w_co, w_o)


def _mix_bwd(dx1, mixed, o, conv, gate, wconv, g_post, w_ao, w_co, w_o, tm):
    s, d = dx1.shape
    aw, cw = w_ao.shape[0], w_co.shape[0]
    n = s // tm
    per = tm // HALO

    def body(dx1_ref, mixed_ref, o_ref, conv_ref, prev_ref, gate_ref, wc_ref, g_ref, wao_hbm, wco_hbm, wo_hbm,
             dmixed_ref, dattn_ref, dconvout_ref, do_ref, drest_ref, dg_ref, dbias_ref, dwc_ref,
             wao, wco, wo, dcv_next, sem):
        i = pl.program_id(0)
        _load_resident([(wao_hbm, wao), (wco_hbm, wco), (wo_hbm, wo)], sem)

        @pl.when(i == 0)
        def _():
            dg_ref[...] = jnp.zeros_like(dg_ref)
            dbias_ref[...] = jnp.zeros_like(dbias_ref)
            dwc_ref[...] = jnp.zeros_like(dwc_ref)
            dcv_next[...] = jnp.zeros_like(dcv_next)

        mixed = mixed_ref[...]
        r = _rms_scale(mixed)
        mhat = mixed * r
        dn = dx1_ref[...]
        dg_ref[...] += jnp.sum(dn * mhat, axis=0, keepdims=True)
        dmixed = _rms_bwd(mhat, r, g_ref[...], dn).astype(BF16)
        dmixed_ref[...] = dmixed
        dmi = _dot_nt(dmixed, wo[...])

        wc = wc_ref[...]
        conv = conv_ref[...]
        y_attn, y_conv, _, cb, cv, cm, cm1, cm2 = _branches(
            o_ref[...].astype(BF16), conv, prev_ref[...], wc, wao[...], wco[...], cw, i == n - 1)
        ga = gate_ref[:, 0:d]
        gc = gate_ref[:, d:2 * d]
        dpre_a = dmi * y_attn * ga * (1.0 - ga)
        dpre_c = dmi * y_conv * gc * (1.0 - gc)
        drest_ref[:, 3 * cw:3 * cw + d] = dpre_a.astype(BF16)
        drest_ref[:, 3 * cw + d:3 * cw + 2 * d] = dpre_c.astype(BF16)
        dbias_ref[:, 0:d] += jnp.sum(dpre_a, axis=0, keepdims=True)
        dbias_ref[:, d:2 * d] += jnp.sum(dpre_c, axis=0, keepdims=True)

        dattn = (dmi * ga).astype(BF16)
        dattn_ref[...] = dattn
        do_ref[...] = _dot_nt(dattn, wao[...]).astype(BF16)
        dconvout = (dmi * gc).astype(BF16)
        dconvout_ref[...] = dconvout
        dconv_in = _dot_nt(dconvout, wco[...])
        drest_ref[:, 0:cw] = (dconv_in * cv).astype(BF16)

        dcv = dconv_in * cb
        following = dcv_next[...]
        dcm = wc[2:3, :] * dcv + wc[1:2, :] * _shift_up(dcv, following, 1) + wc[0:1, :] * _shift_up(dcv, following, 2)
        drest_ref[:, cw:2 * cw] = (dcm * conv[:, 2 * cw:3 * cw]).astype(BF16)
        drest_ref[:, 2 * cw:3 * cw] = (dcm * conv[:, cw:2 * cw]).astype(BF16)
        for tap, shifted in enumerate((cm2, cm1, cm)):
            dwc_ref[tap:tap + 1, :] += jnp.sum(dcv * shifted, axis=0, keepdims=True)
        dcv_next[...] = dcv[0:HALO, :]

    def rows(width):
        return pl.BlockSpec((tm, width), lambda i: (n - 1 - i, 0))

    prev_halo = pl.BlockSpec((HALO, 3 * cw), lambda i: (jnp.maximum((n - 1 - i) * per - 1, 0), 0))
    n_rest = 3 * cw + 2 * d
    return pl.pallas_call(
        body, name="mix_bwd", grid=(n,),
        in_specs=[rows(d), rows(d), rows(aw), rows(3 * cw), prev_halo, rows(2 * d), _const_spec((CONV_K, cw)),
                  _const_spec((1, d)), ANY, ANY, ANY],
        out_specs=[rows(d), rows(d), rows(d), rows(aw), rows(n_rest), _const_spec((1, d)), _const_spec((1, 2 * d)),
                   _const_spec((CONV_K, cw))],
        out_shape=[jax.ShapeDtypeStruct((s, d), BF16), jax.ShapeDtypeStruct((s, d), BF16),
                   jax.ShapeDtypeStruct((s, d), BF16), jax.ShapeDtypeStruct((s, aw), BF16),
                   jax.ShapeDtypeStruct((s, n_rest), BF16), jax.ShapeDtypeStruct((1, d), F32),
                   jax.ShapeDtypeStruct((1, 2 * d), F32), jax.ShapeDtypeStruct((CONV_K, cw), F32)],
        scratch_shapes=[pltpu.VMEM(w_ao.shape, BF16), pltpu.VMEM(w_co.shape, BF16), pltpu.VMEM(w_o.shape, BF16),
                        pltpu.VMEM((HALO, cw), F32), pltpu.SemaphoreType.DMA((3,))],
        compiler_params=_params(1),
    )(dx1, mixed, o, conv, conv, gate, wconv, g_post, w_ao, w_co, w_o)


def _mlp_ple_loss(x1, p, target, g_pre, g_post, g_ple, w_up, w_dn, w_pg, w_pp, tm):
    s, d = x1.shape
    ff = w_up.shape[1]
    pd = p.shape[1]
    fc = FF_CHUNK

    def body(x1_ref, p_ref, t_ref, gpre_ref, gpost_ref, gple_ref, wup_hbm, wdn_hbm, wpg_hbm, wpp_hbm,
             dx1_ref, h2_ref, du_ref, a_ref, df_ref, h3_ref, ds3_ref, dpp_ref, loss_ref, dgpre_ref, dgpost_ref,
             dgple_ref, wup, wdn, wpg, wpp, u_scr, sem):
        _load_resident([(wup_hbm, wup), (wdn_hbm, wdn), (wpg_hbm, wpg), (wpp_hbm, wpp)], sem)

        @pl.when(pl.program_id(0) == 0)
        def _():
            for ref in (loss_ref, dgpre_ref, dgpost_ref, dgple_ref):
                ref[...] = jnp.zeros_like(ref)

        x1v = x1_ref[...]
        r2 = _rms_scale(x1v)
        x1hat = x1v * r2
        h2 = (x1hat * gpre_ref[...]).astype(BF16)
        h2_ref[...] = h2
        f = jnp.zeros((tm, d), F32)
        for c0 in range(0, ff, fc):
            u = _dot(h2, wup[:, c0:c0 + fc])
            u_scr[:, c0:c0 + fc] = u
            a = jnp.square(jnp.maximum(u, 0.0)).astype(BF16)
            a_ref[:, c0:c0 + fc] = a
            f = f + _dot(a, wdn[c0:c0 + fc, :])
        rf = _rms_scale(f)
        fhat = f * rf
        x2 = x1v + fhat * gpost_ref[...]
        r3 = _rms_scale(x2)
        x2hat = x2 * r3
        h3 = (x2hat * gple_ref[...]).astype(BF16)
        h3_ref[...] = h3
        pg = _sigmoid(_dot(h3, wpg[...]))
        pp = _dot(p_ref[...].astype(BF16), wpp[...])
        diff = x2 + pg * pp - t_ref[...]
        loss_ref[...] += 0.5 * jnp.sum(jnp.mean(diff * diff, axis=-1, keepdims=True), axis=0, keepdims=True)

        dy = diff * (1.0 / d)
        dpp_ref[...] = (dy * pg).astype(BF16)
        ds3 = (dy * pp * pg * (1.0 - pg)).astype(BF16)
        ds3_ref[...] = ds3
        dh3 = _dot_nt(ds3, wpg[...])
        dgple_ref[...] += jnp.sum(dh3 * x2hat, axis=0, keepdims=True)
        dx2 = dy + _rms_bwd(x2hat, r3, gple_ref[...], dh3)
        dgpost_ref[...] += jnp.sum(dx2 * fhat, axis=0, keepdims=True)
        df = _rms_bwd(fhat, rf, gpost_ref[...], dx2).astype(BF16)
        df_ref[...] = df
        dh2 = jnp.zeros((tm, d), F32)
        for c0 in range(0, ff, fc):
            da = _dot_nt(df, wdn[c0:c0 + fc, :])
            du = (da * (2.0 * jnp.maximum(u_scr[:, c0:c0 + fc], 0.0))).astype(BF16)
            du_ref[:, c0:c0 + fc] = du
            dh2 = dh2 + _dot_nt(du, wup[:, c0:c0 + fc])
        dgpre_ref[...] += jnp.sum(dh2 * x1hat, axis=0, keepdims=True)
        dx1_ref[...] = dx2 + _rms_bwd(x1hat, r2, gpre_ref[...], dh2)

    vec = _const_spec((1, d))
    return pl.pallas_call(
        body, name="mlp_ple_loss", grid=(s // tm,),
        in_specs=[_row_spec(tm, d), _row_spec(tm, pd), _row_spec(tm, d), vec, vec, vec, ANY, ANY, ANY, ANY],
        out_specs=[_row_spec(tm, d), _row_spec(tm, d), _row_spec(tm, ff), _row_spec(tm, ff), _row_spec(tm, d),
                   _row_spec(tm, d), _row_spec(tm, d), _row_spec(tm, d), _const_spec((1, 1)), vec, vec, vec],
        out_shape=[jax.ShapeDtypeStruct((s, d), F32), jax.ShapeDtypeStruct((s, d), BF16),
                   jax.ShapeDtypeStruct((s, ff), BF16), jax.ShapeDtypeStruct((s, ff), BF16),
                   jax.ShapeDtypeStruct((s, d), BF16), jax.ShapeDtypeStruct((s, d), BF16),
                   jax.ShapeDtypeStruct((s, d), BF16), jax.ShapeDtypeStruct((s, d), BF16),
                   jax.ShapeDtypeStruct((1, 1), F32), jax.ShapeDtypeStruct((1, d), F32),
                   jax.ShapeDtypeStruct((1, d), F32), jax.ShapeDtypeStruct((1, d), F32)],
        scratch_shapes=[pltpu.VMEM(w_up.shape, BF16), pltpu.VMEM(w_dn.shape, BF16), pltpu.VMEM(w_pg.shape, BF16),
                        pltpu.VMEM(w_pp.shape, BF16), pltpu.VMEM((tm, ff), F32), pltpu.SemaphoreType.DMA((4,))],
        compiler_params=_params(1),
    )(x1, p, target, g_pre, g_post, g_ple, w_up, w_dn, w_pg, w_pp)


def _in_proj_bwd(x, dx1, pieces, g1, w_in, tm, exchange=None):
    s, d = x.shape
    ni = w_in.shape[1]
    widths = [p.shape[1] for p in pieces]
    grid = (s // tm,)
    ex = exchange or _NO_EXCHANGE

    def body(x_ref, dx1_ref, *rest):
        piece_refs, rest = rest[:len(pieces)], rest[len(pieces):]
        g_ref, w_hbm = rest[0], rest[1]
        ex_in, (dx_ref, dg_ref), ex_out, (w_vmem, sem, *sems) = _split_refs(rest[2:], ex, 2)
        _exchange_start(ex, ex_in, ex_out, sems, grid)
        _load_resident([(w_hbm, w_vmem)], sem)

        @pl.when(pl.program_id(0) == 0)
        def _():
            dg_ref[...] = jnp.zeros_like(dg_ref)

        dh = jnp.zeros((tm, d), F32)
        c0 = 0
        for ref, width in zip(piece_refs, widths):
            dh = dh + _dot_nt(ref[...], w_vmem[:, c0:c0 + width])
            c0 += width
        xv = x_ref[...]
        r = _rms_scale(xv)
        xhat = xv * r
        dg_ref[...] += jnp.sum(dh * xhat, axis=0, keepdims=True)
        dx_ref[...] = dx1_ref[...] + _rms_bwd(xhat, r, g_ref[...], dh)
        _exchange_wait(ex, ex_in, ex_out, sems, grid)

    return pl.pallas_call(
        body, name="in_proj_bwd", grid=grid,
        in_specs=[_row_spec(tm, d), _row_spec(tm, d)] + [_row_spec(tm, w) for w in widths]
        + [_const_spec((1, d)), ANY] + [ANY] * len(ex.arrays),
        out_specs=[_row_spec(tm, d), _const_spec((1, d))] + [ANY] * len(ex.out_shapes),
        out_shape=[jax.ShapeDtypeStruct((s, d), F32), jax.ShapeDtypeStruct((1, d), F32)] + ex.out_shapes,
        scratch_shapes=[pltpu.VMEM((d, ni), BF16), pltpu.SemaphoreType.DMA((1,))] + _exchange_sems(ex),
        compiler_params=_params(1),
    )(x, dx1, *pieces, g1, w_in, *ex.arrays)


def _weight_grad(a, b, name, into=None, col0=0, n_total=None):
    s, m = a.shape
    n = b.shape[1]
    tm, tk = min(m, DW_TILE), min(s, DW_TOKENS)
    tn = min(n, DW_TILE) if n_total is None else DW_PIECE_TILE
    nk = s // tk
    j0 = col0 // tn
    assert m % tm == 0 and n % tn == 0 and col0 % tn == 0

    def body(a_ref, b_ref, *rest):
        o_ref, acc = rest[-2:]
        k = pl.program_id(2)

        @pl.when(k == 0)
        def _():
            acc[...] = jnp.zeros_like(acc)

        acc[...] += _dot_tn(a_ref[...].astype(BF16), b_ref[...].astype(BF16))

        @pl.when(k == nk - 1)
        def _():
            o_ref[...] = acc[...].astype(BF16)

    extra = [] if into is None else [into]
    return pl.pallas_call(
        body, name=name, grid=(m // tm, n // tn, nk),
        in_specs=[pl.BlockSpec((tk, tm), lambda i, j, k: (k, i)), pl.BlockSpec((tk, tn), lambda i, j, k: (k, j))]
        + [ANY] * len(extra),
        out_specs=pl.BlockSpec((tm, tn), lambda i, j, k: (i, j0 + j)),
        out_shape=jax.ShapeDtypeStruct((m, n_total or n), BF16),
        input_output_aliases={2: 0} if extra else {},
        scratch_shapes=[pltpu.VMEM((tm, tn), F32)],
        compiler_params=_params(3),
    )(a, b, *extra)


def _mesh_position():
    return tuple(lax.axis_index(a) for a in MESH_AXES)


def _peer(me, k):
    bits = ((k >> 2) & 1, (k >> 1) & 1, k & 1)
    pos = tuple(1 - m if b else m for m, b in zip(me, bits))
    return pos, 4 * pos[0] + 2 * pos[1] + pos[2]


class _Exchange:
    def __init__(self, arrays, out_shapes, src, dst, relayed=None):
        self.arrays, self.out_shapes, self.src, self.dst = list(arrays), list(out_shapes), src, dst
        self.relayed = list(relayed) if relayed is not None else [False] * len(self.arrays)


_NO_EXCHANGE = _Exchange([], [], None, None)


def _exchange_sems(ex):
    n = len(ex.arrays)
    if n == 0:
        return []
    return [pltpu.SemaphoreType.DMA((n, N_DEV - 1)), pltpu.SemaphoreType.DMA((n, N_DEV - 1)),
            pltpu.SemaphoreType.DMA((n,))]


def _split_refs(rest, ex, n_own_outs):
    n_in, n_out = len(ex.arrays), len(ex.out_shapes)
    ex_in, rest = rest[:n_in], rest[n_in:]
    own, rest = rest[:n_own_outs], rest[n_own_outs:]
    return ex_in, own, rest[:n_out], rest[n_out:]


def _direct_steps(ex, w, in_refs, out_refs, sems):
    send_sems, recv_sems, local_sems = sems
    me = _mesh_position()
    mine = 4 * me[0] + 2 * me[1] + me[2]

    def copy(k):
        landing = ex.dst(w, out_refs, mine)
        if k == 0:
            return pltpu.make_async_copy(ex.src(w, in_refs, mine), landing, local_sems.at[w])
        peer, peer_idx = _peer(me, k)
        return pltpu.make_async_remote_copy(
            src_ref=ex.src(w, in_refs, peer_idx), dst_ref=landing, send_sem=send_sems.at[w, k - 1],
            recv_sem=recv_sems.at[w, k - 1], device_id=peer, device_id_type=pl.DeviceIdType.MESH)

    ks = range(N_DEV)
    return [lambda k=k: copy(k).start() for k in ks], [], [lambda k=k: copy(k).wait() for k in ks]


def _relayed_steps(ex, w, in_refs, out_refs, sems):
    send_sems, recv_sems, local_sems = sems
    x, y, c = _mesh_position()
    chips = [(1 - x, y), (x, 1 - y), (1 - x, 1 - y)]
    sibling = (x, y, 1 - c)
    js = range(len(chips))

    def block(px, py, pc):
        return ex.dst(w, out_refs, 4 * px + 2 * py + pc)

    def copy(k, dst, to, src=None):
        return pltpu.make_async_remote_copy(
            src_ref=ex.src(w, in_refs, None) if src is None else src, dst_ref=dst, send_sem=send_sems.at[w, k],
            recv_sem=recv_sems.at[w, k], device_id=to, device_id_type=pl.DeviceIdType.MESH)

    def local():
        return pltpu.make_async_copy(ex.src(w, in_refs, None), block(x, y, c), local_sems.at[w])

    def own(k):
        return copy(k, block(x, y, c), sibling if k == 0 else (*chips[k - 1], c))

    def came(j):
        return copy(1 + j, block(*chips[j], c), (*chips[j], c))

    def passed(j):
        return copy(4 + j, block(*chips[j], c), sibling, src=block(*chips[j], c))

    def from_sibling(k):
        return copy(k, block(x, y, 1 - c) if k == 0 else block(*chips[k - 4], 1 - c), sibling)

    start = [lambda: local().start()] + [lambda k=k: own(k).start() for k in range(4)]
    relay = [step for j in js for step in (lambda j=j: came(j).wait_recv(), lambda j=j: passed(j).start())]
    finish = ([lambda: local().wait()] + [lambda k=k: own(k).wait_send() for k in range(4)]
              + [lambda j=j: passed(j).wait_send() for j in js]
              + [lambda k=k: from_sibling(k).wait_recv() for k in (0, 4, 5, 6)])
    return start, relay, finish


def _exchange_steps(ex, in_refs, out_refs, sems):
    start, relay, finish = [], [], []
    for w in range(len(ex.arrays)):
        steps = (_relayed_steps if ex.relayed[w] else _direct_steps)(ex, w, in_refs, out_refs, sems)
        start += steps[0]
        relay += steps[1]
        finish += steps[2]
    return start, relay, finish


def _run(steps):
    for step in steps:
        step()


def _at_grid_step(grid, where):
    target = {"first": [0] * len(grid), "middle": [grid[0] // 2] + [0] * (len(grid) - 1),
              "last": [g - 1 for g in grid]}[where]
    hit = pl.program_id(0) == target[0]
    for axis in range(1, len(grid)):
        hit = jnp.logical_and(hit, pl.program_id(axis) == target[axis])
    return hit


def _exchange_start(ex, in_refs, out_refs, sems, grid):
    if ex.arrays:
        @pl.when(_at_grid_step(grid, "first"))
        def _():
            _run(_exchange_steps(ex, in_refs, out_refs, sems)[0])

        if any(ex.relayed):
            assert grid[0] >= 2

            @pl.when(_at_grid_step(grid, "middle"))
            def _():
                _run(_exchange_steps(ex, in_refs, out_refs, sems)[1])


def _exchange_wait(ex, in_refs, out_refs, sems, grid):
    if ex.arrays:
        @pl.when(_at_grid_step(grid, "last"))
        def _():
            _run(_exchange_steps(ex, in_refs, out_refs, sems)[2])


def _shard_block(ref, shard_shape, by_col, idx):
    r, c = shard_shape
    if by_col:
        return ref.at[:, pl.ds(pl.multiple_of(idx * c, LANES), c)]
    return ref.at[pl.ds(pl.multiple_of(idx * r, 16), r), :]


def _full_shape(shard_shape, by_col):
    r, c = shard_shape
    return (r, N_DEV * c) if by_col else (N_DEV * r, c)


def _gather_exchange(shards, col_sharded):
    shapes = [a.shape for a in shards]
    return _Exchange(
        shards, [jax.ShapeDtypeStruct(_full_shape(sh, bc), a.dtype) for a, sh, bc in zip(shards, shapes, col_sharded)],
        lambda w, refs, idx: refs[w],
        lambda w, refs, idx: _shard_block(refs[w], shapes[w], col_sharded[w], idx), [True] * len(shards))


def _scatter_exchange(grads, col_sharded):
    shapes = []
    for g, by_col in zip(grads, col_sharded):
        r, c = g.shape
        shapes.append((r, c // N_DEV) if by_col else (r // N_DEV, c))
    return _Exchange(
        grads, [jax.ShapeDtypeStruct((N_DEV,) + sh, g.dtype) for g, sh in zip(grads, shapes)],
        lambda w, refs, idx: _shard_block(refs[w], shapes[w], col_sharded[w], idx),
        lambda w, refs, mine: refs[w].at[mine])


def _broadcast_exchange(arrays):
    return _Exchange(arrays, [jax.ShapeDtypeStruct((N_DEV,) + a.shape, a.dtype) for a in arrays],
                     lambda w, refs, idx: refs[w], lambda w, refs, mine: refs[w].at[mine])


def _join(*exs):
    arrays, shapes, owner = [], [], []
    for e in exs:
        for w in range(len(e.arrays)):
            owner.append((e, w, len(arrays), len(shapes)))
        arrays += e.arrays
        shapes += e.out_shapes

    def src(w, refs, idx):
        e, w0, i0, _ = owner[w]
        return e.src(w0, refs[i0:i0 + len(e.arrays)], idx)

    def dst(w, refs, idx):
        e, w0, _, o0 = owner[w]
        return e.dst(w0, refs[o0:o0 + len(e.out_shapes)], idx)

    return _Exchange(arrays, shapes, src, dst, [flag for e in exs for flag in e.relayed])


def _exchange_call(ex, name):
    n_in = len(ex.arrays)

    def body(*refs):
        in_refs, _, out_refs, sems = _split_refs(refs, ex, 0)
        for steps in _exchange_steps(ex, in_refs, out_refs, sems):
            _run(steps)

    return pl.pallas_call(
        body, name=name, in_specs=[ANY] * n_in, out_specs=[ANY] * len(ex.out_shapes), out_shape=ex.out_shapes,
        scratch_shapes=_exchange_sems(ex), compiler_params=pltpu.CompilerParams(vmem_limit_bytes=VMEM_LIMIT),
    )(*ex.arrays)


def _to_bf16(arrays):
    def body(*refs):
        for src, dst in zip(refs[:len(arrays)], refs[len(arrays):]):
            dst[...] = src[...].astype(BF16)

    vmem = pl.BlockSpec(memory_space=pltpu.VMEM)
    return pl.pallas_call(
        body, name="weights_to_bf16", in_specs=[vmem] * len(arrays), out_specs=[vmem] * len(arrays),
        out_shape=[jax.ShapeDtypeStruct(a.shape, BF16) for a in arrays],
        compiler_params=pltpu.CompilerParams(vmem_limit_bytes=VMEM_LIMIT),
    )(*arrays)


def _adamw(w, g, m, v):
    m = ADAM_B1 * m + (1.0 - ADAM_B1) * g
    v = ADAM_B2 * v + (1.0 - ADAM_B2) * jnp.square(g)
    m_hat = m / (1.0 - ADAM_B1 ** ADAM_STEP)
    v_hat = v / (1.0 - ADAM_B2 ** ADAM_STEP)
    delta = -ADAM_LR * (m_hat / (jnp.sqrt(v_hat) + ADAM_EPS) + ADAM_WD * w)
    return delta, m, v


def _sum_and_adamw(parts, w, m, v, name):
    r, c = w.shape
    tr = min(r, 256)

    def body(p_ref, w_ref, m_ref, v_ref, g_out, d_out, m_out, v_out):
        g = p_ref[0].astype(F32)
        for dev in range(1, N_DEV):
            g = g + p_ref[dev].astype(F32)
        g_out[...] = g
        d_out[...], m_out[...], v_out[...] = _adamw(w_ref[...], g, m_ref[...], v_ref[...])

    blk = pl.BlockSpec((tr, c), lambda i: (i, 0))
    return pl.pallas_call(
        body, name=name, grid=(r // tr,),
        in_specs=[pl.BlockSpec((N_DEV, tr, c), lambda i: (0, i, 0)), blk, blk, blk],
        out_specs=[blk] * 4, out_shape=[jax.ShapeDtypeStruct((r, c), F32)] * 4,
        compiler_params=_params(1),
    )(parts, w, m, v)


BIG = ("w_in", "w_attn_out", "w_conv_out", "w_o", "w_up", "w_down", "w_ple_gate", "w_ple_proj")
COL_SHARDED = {"w_in": True, "w_attn_out": True, "w_conv_out": True, "w_o": False, "w_up": True, "w_down": False,
               "w_ple_gate": False, "w_ple_proj": True}
SMALL = ("g_pre_mix", "b_gate", "g_post_mix", "g_pre_mlp", "g_post_mlp", "g_ple")


REST = BIG[1:]


def _local_grads(x, p, target, small, wconv, full, aw, cw, tm, t, gather_rest=None, scatter_rest=None,
                 scatter_in=None):
    full = dict(full)
    qkv, conv, gate, h1 = _in_proj_fwd(x, small["g_pre_mix"], small["b_gate"], full["w_in"], aw, cw, tm)
    o, *rest = _attn_fwd(qkv, aw, t, gather_rest)
    full.update(zip(REST, rest))
    x1, mixed, mix_in, conv_in = _mix_fwd(x, o, conv, gate, wconv, small["g_post_mix"], full["w_attn_out"],
                                          full["w_conv_out"], full["w_o"], tm)
    (dx1, h2, du, a, df, h3, ds3, dpp, loss, dg_pre_mlp, dg_post_mlp, dg_ple) = _mlp_ple_loss(
        x1, p, target, small["g_pre_mlp"], small["g_post_mlp"], small["g_ple"], full["w_up"], full["w_down"],
        full["w_ple_gate"], full["w_ple_proj"], tm)
    big = {"w_up": _weight_grad(h2, du, "dw_up"), "w_down": _weight_grad(a, df, "dw_down"),
           "w_ple_gate": _weight_grad(h3, ds3, "dw_ple_gate"), "w_ple_proj": _weight_grad(p, dpp, "dw_ple_proj")}
    (dmixed, dattn, dconvout, do, drest, dg_post_mix, db_gate, dwconv) = _mix_bwd(
        dx1, mixed, o, conv, gate, wconv, small["g_post_mix"], full["w_attn_out"], full["w_conv_out"], full["w_o"],
        tm)
    big.update({"w_attn_out": _weight_grad(o, dattn, "dw_attn_out"),
                "w_conv_out": _weight_grad(conv_in, dconvout, "dw_conv_out"),
                "w_o": _weight_grad(mix_in, dmixed, "dw_o")})
    dq, dk, dv, *scattered = _attn_bwd(qkv, o, do, aw, t, scatter_rest and scatter_rest([big[n] for n in REST]))
    pieces = [dq, dk, dv, drest]
    dw_in, col0, ni = None, 0, full["w_in"].shape[1]
    for i, piece in enumerate(pieces):
        dw_in = _weight_grad(h1, piece, "dw_in_%d" % i, dw_in, col0, ni)
        col0 += piece.shape[1]
    big["w_in"] = dw_in
    dx, dg_pre_mix, *scattered_in = _in_proj_bwd(x, dx1, pieces, small["g_pre_mix"], full["w_in"], tm,
                                                scatter_in and scatter_in(dw_in))
    small_grads = {"g_pre_mix": dg_pre_mix, "b_gate": db_gate, "g_post_mix": dg_post_mix, "g_pre_mlp": dg_pre_mlp,
                   "g_post_mlp": dg_post_mlp, "g_ple": dg_ple, "w_conv": dwconv}
    return loss[0, 0], dx, big, small_grads, scattered_in + scattered


def _pack_small(vals, d):
    parts = []
    for a in vals:
        a = jnp.pad(a.reshape(-1), (0, -a.size % d)).reshape(-1, d)
        parts.append(jnp.pad(a, ((0, HALO - a.shape[0]), (0, 0))))
    return jnp.concatenate(parts, axis=0)


def _unpack_small(pack, shapes, d):
    out = []
    for i, shp in enumerate(shapes):
        n = 1
        for v in shp:
            n *= v
        rows = -(-n // d)
        out.append(pack[i * HALO:i * HALO + rows].reshape(-1)[:n].reshape(shp))
    return out


def kernel(x, p, g_pre_mix, w_in, b_gate, w_conv, w_attn_out, w_conv_out, w_o, g_post_mix, g_pre_mlp, w_up, w_down, g_post_mlp, g_ple, w_ple_gate, w_ple_proj, loss_target, m_g_pre_mix, m_w_in, m_b_gate, m_w_conv, m_w_attn_out, m_w_conv_out, m_w_o, m_g_post_mix, m_g_pre_mlp, m_w_up, m_w_down, m_g_post_mlp, m_g_ple, m_w_ple_gate, m_w_ple_proj, v_g_pre_mix, v_w_in, v_b_gate, v_w_conv, v_w_attn_out, v_w_conv_out, v_w_o, v_g_post_mix, v_g_pre_mlp, v_w_up, v_w_down, v_g_post_mlp, v_g_ple, v_w_ple_gate, v_w_ple_proj):
    given = dict(locals())
    order = ["g_pre_mix", "w_in", "b_gate", "w_conv", "w_attn_out", "w_conv_out", "w_o", "g_post_mix", "g_pre_mlp",
             "w_up", "w_down", "g_post_mlp", "g_ple", "w_ple_gate", "w_ple_proj"]
    d = x.shape[-1]
    me = 4 * lax.axis_index("x") + 2 * lax.axis_index("y") + lax.axis_index("c")

    col = [COL_SHARDED[n] for n in BIG]
    shards = _to_bf16([given[n][0] for n in BIG])
    cw_shard = w_conv.shape[-1]
    conv_tile = jnp.pad(w_conv[0], ((0, HALO - CONV_K), (0, LANES - cw_shard)))
    w_in_full, conv_g = _exchange_call(
        _join(_gather_exchange(shards[:1], col[:1]), _broadcast_exchange([conv_tile])), "gather_w_in")
    wconv = jnp.concatenate([conv_g[dev, :CONV_K, :cw_shard] for dev in range(N_DEV)], axis=1)

    small = {n: given[n] for n in SMALL}
    loss, dx, big_grads, small_grads, parts = _local_grads(
        x[0], p[0, 0], loss_target[0], small, wconv, {"w_in": w_in_full}, w_attn_out.shape[1], w_conv_out.shape[1],
        ROW_BLOCK, ATTN_BLOCK,
        _gather_exchange(shards[1:], col[1:]), lambda grads: _scatter_exchange(grads, col[1:]),
        lambda grad: _scatter_exchange([grad], col[:1]))
    loss = lax.psum(loss, MESH_AXES)

    small_names = list(SMALL) + ["w_conv"]
    pack = _pack_small([small_grads[n] for n in small_names], d)
    packs, = _exchange_call(_broadcast_exchange([pack]), "share_small_grads")

    grads, deltas, new_m, new_v = {}, {}, {}, {}
    for n, part in zip(BIG, parts):
        grads[n], deltas[n], new_m[n], new_v[n] = (
            a[None] for a in _sum_and_adamw(part, given[n][0], given["m_" + n][0], given["v_" + n][0], "adamw_" + n))

    full_conv = lambda a: lax.dynamic_update_slice(jnp.zeros((CONV_K, N_DEV * cw_shard), F32), a[0],
                                                   (jnp.int32(0), me * cw_shard))
    state = [_pack_small([given[pre + n] for n in SMALL] + [full_conv(given[pre + "w_conv"])], d)
             for pre in ("", "m_", "v_")]
    outs = _sum_and_adamw(packs, *state, "adamw_small")
    shapes = [given[n].shape for n in SMALL] + [(CONV_K, N_DEV * cw_shard)]
    for res, dst in zip(outs, (grads, deltas, new_m, new_v)):
        for n, a in zip(small_names, _unpack_small(res, shapes, d)):
            dst[n] = (lax.dynamic_slice(a, (jnp.int32(0), me * cw_shard), (CONV_K, cw_shard))[None]
                      if n == "w_conv" else a)

    return (loss, dx[None], *[grads[n] for n in order], *[deltas[n] for n in order],
            *[new_m[n] for n in order], *[new_v[n] for n in order])
```

```python
import jax
import jax.numpy as jnp
from jax import lax
from jax.experimental import pallas as pl
from jax.experimental.pallas import tpu as pltpu

F32 = jnp.float32
BF16 = jnp.bfloat16
RMS_EPS = 1e-6
N_DEV = 8
MESH_AXES = ("x", "y", "c")
LANES = 128
HEAD_DIM = 64
HEADS_PER_GROUP = LANES // HEAD_DIM
CONV_K = 3
HALO = 8
HALO_BF16 = 16
VMEM_LIMIT = 56 * 1024 * 1024
EXP2_ZERO = -150.0
LOG2_E = 1.4426950408889634

ADAM_LR = 0.001
ADAM_B1 = 0.9
ADAM_B2 = 0.999
ADAM_EPS = 1e-08
ADAM_WD = 0.01
ADAM_STEP = 10

ROW_BLOCK = 256
ATTN_BLOCK = 256
ATTN_ROW_SPLITS = 2
DW_TOKENS = 2048
DW_TILE = 1024
DW_PIECE_TILE = 512
FF_CHUNK = 1024
PROJ_CHUNK = 512


def _dot(a, b):
    return lax.dot_general(a, b, (((1,), (0,)), ((), ())), preferred_element_type=F32)


def _dot_nt(a, b):
    return lax.dot_general(a, b, (((1,), (1,)), ((), ())), preferred_element_type=F32)


def _dot_tn(a, b):
    return lax.dot_general(a, b, (((0,), (0,)), ((), ())), preferred_element_type=F32)


def _sigmoid(z):
    return 1.0 / (1.0 + jnp.exp(-z))


def _rms_scale(x):
    return lax.rsqrt(jnp.mean(x * x, axis=-1, keepdims=True) + RMS_EPS)


def _rms_bwd(xhat, r, g, dy):
    gd = dy * g
    return r * (gd - xhat * jnp.mean(gd * xhat, axis=-1, keepdims=True))


def _params(n_axes, **kw):
    return pltpu.CompilerParams(dimension_semantics=("arbitrary",) * n_axes, vmem_limit_bytes=VMEM_LIMIT, **kw)


def _load_resident(pairs, sem):
    @pl.when(pl.program_id(0) == 0)
    def _():
        copies = [pltpu.make_async_copy(src, dst, sem.at[i]) for i, (src, dst) in enumerate(pairs)]
        for cp in copies:
            cp.start()
        for cp in copies:
            cp.wait()


def _row_spec(tm, width):
    return pl.BlockSpec((tm, width), lambda i: (i, 0))


def _prev_halo_spec(tm, width, rows):
    per = tm // rows
    return pl.BlockSpec((rows, width), lambda i: (jnp.maximum(i * per - 1, 0), 0))


def _const_spec(shape):
    return pl.BlockSpec(shape, lambda i: (0,) * len(shape))


ANY = pl.BlockSpec(memory_space=pl.ANY)


def _shift_down(cur, prev, n):
    rows = lax.broadcasted_iota(jnp.int32, cur.shape, 0)
    out = pltpu.roll(cur, n, 0)
    for j in range(n):
        out = jnp.where(rows == j, prev[prev.shape[0] - n + j:prev.shape[0] - n + j + 1, :], out)
    return out


def _shift_up(cur, nxt, n):
    tm = cur.shape[0]
    rows = lax.broadcasted_iota(jnp.int32, cur.shape, 0)
    out = pltpu.roll(cur, tm - n, 0)
    for j in range(n):
        out = jnp.where(rows == tm - n + j, nxt[j:j + 1, :], out)
    return out


def _conv_taps(cm, cm_prev, wconv):
    cm1 = _shift_down(cm, cm_prev, 1)
    cm2 = _shift_down(cm, cm_prev, 2)
    cv = wconv[2:3, :] * cm + wconv[1:2, :] * cm1 + wconv[0:1, :] * cm2
    return cv, cm1, cm2


def _in_proj_fwd(x, g1, b_gate, w_in, aw, cw, tm):
    s, d = x.shape
    ni = w_in.shape[1]
    n_qkv, n_conv = 3 * aw, 3 * cw
    ch = PROJ_CHUNK

    def body(x_ref, g_ref, b_ref, w_hbm, qkv_ref, conv_ref, gate_ref, h_ref, w_vmem, sem):
        _load_resident([(w_hbm, w_vmem)], sem)
        xv = x_ref[...]
        h = (xv * _rms_scale(xv) * g_ref[...]).astype(BF16)
        h_ref[...] = h
        for c0 in range(0, ni, ch):
            pc = _dot(h, w_vmem[:, c0:c0 + ch])
            if c0 < n_qkv:
                qkv_ref[:, c0:c0 + ch] = pc.astype(BF16)
            elif c0 < n_qkv + n_conv:
                conv_ref[:, c0 - n_qkv:c0 - n_qkv + ch] = pc.astype(BF16)
            else:
                g0 = c0 - n_qkv - n_conv
                gate_ref[:, g0:g0 + ch] = _sigmoid(pc + b_ref[:, g0:g0 + ch]).astype(BF16)

    return pl.pallas_call(
        body, name="in_proj_fwd", grid=(s // tm,),
        in_specs=[_row_spec(tm, d), _const_spec((1, d)), _const_spec((1, 2 * d)), ANY],
        out_specs=[_row_spec(tm, n_qkv), _row_spec(tm, n_conv), _row_spec(tm, 2 * d), _row_spec(tm, d)],
        out_shape=[jax.ShapeDtypeStruct((s, n_qkv), BF16), jax.ShapeDtypeStruct((s, n_conv), BF16),
                   jax.ShapeDtypeStruct((s, 2 * d), BF16), jax.ShapeDtypeStruct((s, d), BF16)],
        scratch_shapes=[pltpu.VMEM((d, ni), BF16), pltpu.SemaphoreType.DMA((1,))],
        compiler_params=_params(1),
    )(x, g1, b_gate, w_in)


def _split_hi_lo(a):
    hi = a.astype(BF16)
    return hi, (a - hi.astype(F32)).astype(BF16)


def _log2_gates(z):
    z2 = z * LOG2_E
    nz2 = -z2
    log_keep = jnp.minimum(nz2, 0.0) - jnp.log2(1.0 + jnp.exp2(jnp.minimum(z2, nz2)))
    return log_keep + z2, log_keep


def _attn_masks(t):
    row = lax.broadcasted_iota(jnp.int32, (t, t), 0)
    col = lax.broadcasted_iota(jnp.int32, (t, t), 1)
    return col < row, (row > col).astype(BF16), (row >= col).astype(BF16)


def _chains(a):
    tr = a.shape[0] // ATTN_ROW_SPLITS
    return [jnp.where(_head_lanes(h), a[r * tr:(r + 1) * tr], jnp.zeros((tr, LANES), a.dtype))
            for h in range(HEADS_PER_GROUP) for r in range(ATTN_ROW_SPLITS)]


def _merge_chains(parts):
    rows = []
    for r in range(ATTN_ROW_SPLITS):
        out = parts[r]
        for h in range(1, HEADS_PER_GROUP):
            out = jnp.where(_head_lanes(h), parts[h * ATTN_ROW_SPLITS + r], out)
        rows.append(out)
    return jnp.concatenate(rows, axis=0)


def _by_stage(n_chains, stages):
    for stage in stages:
        for c in range(n_chains):
            stage(c)


def _row_parts(a):
    tr = a.shape[0] // ATTN_ROW_SPLITS
    return [a[r * tr:(r + 1) * tr] for r in range(ATTN_ROW_SPLITS)]


def _while_weights_live(qi, block, carry):
    def cond(state):
        j, carry = state
        live = jnp.max(carry[0][0])
        for run in carry[0][1:]:
            live = jnp.maximum(live, jnp.max(run))
        return jnp.logical_and(j < qi, live >= EXP2_ZERO)

    def step(state):
        j, carry = state
        return j + 1, block(qi - 1 - j, carry)

    return lax.while_loop(cond, step, (jnp.int32(0), carry))[1]


def _head_lanes(h):
    lane = lax.broadcasted_iota(jnp.int32, (1, LANES), 1)
    return (lane >= HEAD_DIM * h) & (lane < HEAD_DIM * (h + 1))


def _attn_fwd(qkv, aw, t, exchange=None):
    s = qkv.shape[0]
    groups = aw // LANES
    nq = s // t
    scale = HEAD_DIM ** -0.5
    ex = exchange or _NO_EXCHANGE

    def body(q_ref, k_ref, v_ref, *rest):
        ex_in, (o_ref,), ex_out, sems = _split_refs(rest, ex, 1)
        qi = pl.program_id(1)
        _exchange_start(ex, ex_in, ex_out, sems, (groups, nq))
        causal, upper, _ = _attn_masks(t)
        causal = _row_parts(causal) * HEADS_PER_GROUP
        qs = _chains(q_ref[...] * scale)
        heads = range(len(qs))
        tr = t // ATTN_ROW_SPLITS

        def block(kb, runs, accs, diag):
            rows = pl.ds(pl.multiple_of(kb * t, t), t)
            k = k_ref[rows, :]
            v = v_ref[rows, :]
            ncs = [(h % ATTN_ROW_SPLITS + 1) * tr if diag else t for h in heads]
            live = [{} for _ in heads]
            new_runs, new_accs = [None] * len(heads), [None] * len(heads)

            def scores(h):
                live[h]["z"] = _dot_nt(qs[h], k[0:ncs[h]])

            def gates(h):
                nc = ncs[h]
                log_b, log_keep = _log2_gates(live[h].pop("z"))
                if diag:
                    log_keep = jnp.where(causal[h][:, 0:nc], log_keep, 0.0)
                hi, lo = _split_hi_lo(log_keep)
                live[h]["log_w"] = log_b + runs[h]
                live[h]["between"] = _dot(hi, upper[0:nc, 0:nc]) + _dot(lo, upper[0:nc, 0:nc])
                new_runs[h] = runs[h] + jnp.sum(log_keep, axis=1, keepdims=True)

            def weights(h):
                nc = ncs[h]
                w = jnp.exp2(live[h].pop("log_w") + live[h].pop("between"))
                if diag:
                    w = jnp.where(causal[h][:, 0:nc], w, 0.0)
                new_accs[h] = accs[h] + _dot(w.astype(BF16), v[0:nc])

            _by_stage(len(heads), [scores, gates, weights])
            return tuple(new_runs), tuple(new_accs)

        carry = block(qi, [jnp.zeros((tr, 1), F32)] * len(heads), [jnp.zeros((tr, LANES), F32)] * len(heads), True)
        _, accs = _while_weights_live(qi, lambda kb, carry: block(kb, *carry, False), carry)
        o_ref[...] = _merge_chains(accs)
        _exchange_wait(ex, ex_in, ex_out, sems, (groups, nq))

    return pl.pallas_call(
        body, name="attn_fwd", grid=(groups, nq),
        in_specs=[pl.BlockSpec((t, LANES), lambda g, i: (i, g)),
                  pl.BlockSpec((s, LANES), lambda g, i: (0, groups + g)),
                  pl.BlockSpec((s, LANES), lambda g, i: (0, 2 * groups + g))] + [ANY] * len(ex.arrays),
        out_specs=[pl.BlockSpec((t, LANES), lambda g, i: (i, g))] + [ANY] * len(ex.out_shapes),
        out_shape=[jax.ShapeDtypeStruct((s, aw), F32)] + ex.out_shapes,
        scratch_shapes=_exchange_sems(ex),
        compiler_params=_params(2),
    )(qkv, qkv, qkv, *ex.arrays)


def _attn_bwd(qkv, o, do, aw, t, exchange=None):
    s = qkv.shape[0]
    groups = aw // LANES
    nq = s // t
    scale = HEAD_DIM ** -0.5
    ex = exchange or _NO_EXCHANGE

    def body(q_ref, k_ref, v_ref, o_ref, do_ref, *rest):
        ex_in, (dq_ref, dk_ref, dv_ref), ex_out, (dk_acc, dv_acc, *sems) = _split_refs(rest, ex, 3)
        qi = pl.program_id(1)
        _exchange_start(ex, ex_in, ex_out, sems, (groups, nq))

        @pl.when(qi == 0)
        def _():
            dk_acc[...] = jnp.zeros_like(dk_acc)
            dv_acc[...] = jnp.zeros_like(dv_acc)

        causal, upper, lower_incl = _attn_masks(t)
        causal = _row_parts(causal) * HEADS_PER_GROUP
        q = q_ref[...] * scale
        do_b = do_ref[...]
        qs = _chains(q)
        dos = _chains(do_b)
        qs_all = jnp.concatenate(qs, axis=0)
        dos_all = jnp.concatenate(dos, axis=0)
        e_totals = [jnp.sum(part, axis=1, keepdims=True) for part in _chains(do_b.astype(F32) * o_ref[...])]
        heads = range(len(qs))
        tr = t // ATTN_ROW_SPLITS

        def block(kb, runs, e_runs, dqs, diag):
            rows = pl.ds(pl.multiple_of(kb * t, t), t)
            k = k_ref[rows, :]
            v = v_ref[rows, :]
            ncs = [(h % ATTN_ROW_SPLITS + 1) * tr if diag else t for h in heads]
            live = [{} for _ in heads]
            none = [None] * len(heads)
            new_runs, new_e_runs, new_dqs, dzbs, wbs = list(none), list(none), list(none), list(none), list(none)

            def scores(h):
                live[h]["z"] = _dot_nt(qs[h], k[0:ncs[h]])
                live[h]["dw"] = _dot_nt(dos[h], v[0:ncs[h]])

            def gates(h):
                nc = ncs[h]
                log_b, log_keep = _log2_gates(live[h].pop("z"))
                live[h]["beta"] = jnp.exp2(log_b)
                live[h]["keep"] = jnp.exp2(log_keep)
                if diag:
                    log_keep = jnp.where(causal[h][:, 0:nc], log_keep, 0.0)
                hi, lo = _split_hi_lo(log_keep)
                live[h]["log_w"] = log_b + runs[h]
                live[h]["between"] = _dot(hi, upper[0:nc, 0:nc]) + _dot(lo, upper[0:nc, 0:nc])
                new_runs[h] = runs[h] + jnp.sum(log_keep, axis=1, keepdims=True)

            def weights(h):
                nc = ncs[h]
                w = jnp.exp2(live[h].pop("log_w") + live[h].pop("between"))
                if diag:
                    w = jnp.where(causal[h][:, 0:nc], w, 0.0)
                wb = w.astype(BF16)
                e = live[h].pop("dw") * wb.astype(F32)
                hi, lo = _split_hi_lo(e)
                live[h]["e"] = e
                live[h]["e_suffix"] = _dot(hi, lower_incl[0:nc, 0:nc]) + _dot(lo, lower_incl[0:nc, 0:nc]) + e_runs[h]
                wbs[h] = wb

            def score_grads(h):
                nc = ncs[h]
                e_suffix = live[h].pop("e_suffix")
                dz = live[h].pop("e") * live[h].pop("keep") - (e_totals[h] - e_suffix) * live[h].pop("beta")
                if diag:
                    dz = jnp.where(causal[h][:, 0:nc], dz, 0.0)
                dzb = dz.astype(BF16)
                new_dqs[h] = dqs[h] + _dot(dzb, k[0:nc])
                new_e_runs[h] = e_suffix[:, 0:1]
                if nc < t:
                    unseen = jnp.zeros((tr, t - nc), BF16)
                    dzb = jnp.concatenate([dzb, unseen], axis=1)
                    wbs[h] = jnp.concatenate([wbs[h], unseen], axis=1)
                dzbs[h] = dzb

            _by_stage(len(heads), [scores, gates, weights, score_grads])
            dk_acc[rows, :] += _dot_tn(jnp.concatenate(dzbs, axis=0), qs_all)
            dv_acc[rows, :] += _dot_tn(jnp.concatenate(wbs, axis=0), dos_all)
            return tuple(new_runs), tuple(new_e_runs), tuple(new_dqs)

        zero_cols = [jnp.zeros((tr, 1), F32)] * len(heads)
        carry = block(qi, zero_cols, zero_cols, [jnp.zeros((tr, LANES), F32)] * len(heads), True)
        _, _, dqs = _while_weights_live(qi, lambda kb, carry: block(kb, *carry, False), carry)
        dq_ref[...] = (_merge_chains(dqs) * scale).astype(BF16)

        @pl.when(qi == nq - 1)
        def _():
            dk_ref[...] = dk_acc[...].astype(BF16)
            dv_ref[...] = dv_acc[...].astype(BF16)

        _exchange_wait(ex, ex_in, ex_out, sems, (groups, nq))

    blk = pl.BlockSpec((t, LANES), lambda g, i: (i, g))
    slab = pl.BlockSpec((s, LANES), lambda g, i: (0, g))
    return pl.pallas_call(
        body, name="attn_bwd", grid=(groups, nq),
        in_specs=[blk, pl.BlockSpec((s, LANES), lambda g, i: (0, groups + g)),
                  pl.BlockSpec((s, LANES), lambda g, i: (0, 2 * groups + g)), blk, blk] + [ANY] * len(ex.arrays),
        out_specs=[blk, slab, slab] + [ANY] * len(ex.out_shapes),
        out_shape=[jax.ShapeDtypeStruct((s, aw), BF16)] * 3 + ex.out_shapes,
        scratch_shapes=[pltpu.VMEM((s, LANES), F32), pltpu.VMEM((s, LANES), F32)] + _exchange_sems(ex),
        compiler_params=_params(2),
    )(qkv, qkv, qkv, o, do, *ex.arrays)


def _branches(o_b, conv, conv_prev, wconv, w_ao, w_co, cw, first):
    conv = conv.astype(F32)
    conv_prev = conv_prev.astype(F32)
    cb = conv[:, 0:cw]
    cm = conv[:, cw:2 * cw] * conv[:, 2 * cw:3 * cw]
    cm_prev = conv_prev[:, cw:2 * cw] * conv_prev[:, 2 * cw:3 * cw]
    cm_prev = jnp.where(first, 0.0, cm_prev)
    cv, cm1, cm2 = _conv_taps(cm, cm_prev, wconv)
    conv_in = (cb * cv).astype(BF16)
    return _dot(o_b, w_ao), _dot(conv_in, w_co), conv_in, cb, cv, cm, cm1, cm2


def _mix_fwd(x, o, conv, gate, wconv, g_post, w_ao, w_co, w_o, tm):
    s, d = x.shape
    aw, cw = w_ao.shape[0], w_co.shape[0]

    def body(x_ref, o_ref, conv_ref, prev_ref, gate_ref, wc_ref, g_ref, wao_hbm, wco_hbm, wo_hbm,
             x1_ref, mixed_ref, mixin_ref, convin_ref, wao, wco, wo, sem):
        _load_resident([(wao_hbm, wao), (wco_hbm, wco), (wo_hbm, wo)], sem)
        y_attn, y_conv, conv_in, *_ = _branches(
            o_ref[...].astype(BF16), conv_ref[...], prev_ref[...], wc_ref[...], wao[...], wco[...], cw,
            pl.program_id(0) == 0)
        mix_in = (gate_ref[:, 0:d].astype(F32) * y_attn + gate_ref[:, d:2 * d].astype(F32) * y_conv).astype(BF16)
        mixed = _dot(mix_in, wo[...])
        x1_ref[...] = x_ref[...] + mixed * _rms_scale(mixed) * g_ref[...]
        mixed_ref[...] = mixed
        mixin_ref[...] = mix_in
        convin_ref[...] = conv_in

    return pl.pallas_call(
        body, name="mix_fwd", grid=(s // tm,),
        in_specs=[_row_spec(tm, d), _row_spec(tm, aw), _row_spec(tm, 3 * cw), _prev_halo_spec(tm, 3 * cw, HALO_BF16),
                  _row_spec(tm, 2 * d), _const_spec((CONV_K, cw)), _const_spec((1, d)), ANY, ANY, ANY],
        out_specs=[_row_spec(tm, d), _row_spec(tm, d), _row_spec(tm, d), _row_spec(tm, cw)],
        out_shape=[jax.ShapeDtypeStruct((s, d), F32), jax.ShapeDtypeStruct((s, d), F32),
                   jax.ShapeDtypeStruct((s, d), BF16), jax.ShapeDtypeStruct((s, cw), BF16)],
        scratch_shapes=[pltpu.VMEM(w_ao.shape, BF16), pltpu.VMEM(w_co.shape, BF16), pltpu.VMEM(w_o.shape, BF16),
                        pltpu.SemaphoreType.DMA((3,))],
        compiler_params=_params(1),
    )(x, o, conv, conv, gate, wconv, g_post, w_ao, w_co, w_o)


def _mix_bwd(dx1, mixed, o, conv, gate, wconv, g_post, w_ao, w_co, w_o, tm):
    s, d = dx1.shape
    aw, cw = w_ao.shape[0], w_co.shape[0]
    n = s // tm
    per = tm // HALO_BF16

    def body(dx1_ref, mixed_ref, o_ref, conv_ref, prev_ref, gate_ref, wc_ref, g_ref, wao_hbm, wco_hbm, wo_hbm,
             dmixed_ref, dattn_ref, dconvout_ref, do_ref, drest_ref, dg_ref, dbias_ref, dwc_ref,
             wao, wco, wo, dcv_next, sem):
        i = pl.program_id(0)
        _load_resident([(wao_hbm, wao), (wco_hbm, wco), (wo_hbm, wo)], sem)

        @pl.when(i == 0)
        def _():
            dg_ref[...] = jnp.zeros_like(dg_ref)
            dbias_ref[...] = jnp.zeros_like(dbias_ref)
            dwc_ref[...] = jnp.zeros_like(dwc_ref)
            dcv_next[...] = jnp.zeros_like(dcv_next)

        mixed = mixed_ref[...]
        r = _rms_scale(mixed)
        mhat = mixed * r
        dn = dx1_ref[...]
        dg_ref[...] += jnp.sum(dn * mhat, axis=0, keepdims=True)
        dmixed = _rms_bwd(mhat, r, g_ref[...], dn).astype(BF16)
        dmixed_ref[...] = dmixed
        dmi = _dot_nt(dmixed, wo[...])

        wc = wc_ref[...]
        conv = conv_ref[...].astype(F32)
        y_attn, y_conv, _, cb, cv, cm, cm1, cm2 = _branches(
            o_ref[...].astype(BF16), conv, prev_ref[...], wc, wao[...], wco[...], cw, i == n - 1)
        ga = gate_ref[:, 0:d].astype(F32)
        gc = gate_ref[:, d:2 * d].astype(F32)
        dpre_a = dmi * y_attn * ga * (1.0 - ga)
        dpre_c = dmi * y_conv * gc * (1.0 - gc)
        drest_ref[:, 3 * cw:3 * cw + d] = dpre_a.astype(BF16)
        drest_ref[:, 3 * cw + d:3 * cw + 2 * d] = dpre_c.astype(BF16)
        dbias_ref[:, 0:d] += jnp.sum(dpre_a, axis=0, keepdims=True)
        dbias_ref[:, d:2 * d] += jnp.sum(dpre_c, axis=0, keepdims=True)

        dattn = (dmi * ga).astype(BF16)
        dattn_ref[...] = dattn
        do_ref[...] = _dot_nt(dattn, wao[...]).astype(BF16)
        dconvout = (dmi * gc).astype(BF16)
        dconvout_ref[...] = dconvout
        dconv_in = _dot_nt(dconvout, wco[...])
        drest_ref[:, 0:cw] = (dconv_in * cv).astype(BF16)

        dcv = dconv_in * cb
        following = dcv_next[...]
        dcm = wc[2:3, :] * dcv + wc[1:2, :] * _shift_up(dcv, following, 1) + wc[0:1, :] * _shift_up(dcv, following, 2)
        drest_ref[:, cw:2 * cw] = (dcm * conv[:, 2 * cw:3 * cw]).astype(BF16)
        drest_ref[:, 2 * cw:3 * cw] = (dcm * conv[:, cw:2 * cw]).astype(BF16)
        for tap, shifted in enumerate((cm2, cm1, cm)):
            dwc_ref[tap:tap + 1, :] += jnp.sum(dcv * shifted, axis=0, keepdims=True)
        dcv_next[...] = dcv[0:HALO, :]

    def rows(width):
        return pl.BlockSpec((tm, width), lambda i: (n - 1 - i, 0))

    prev_halo = pl.BlockSpec((HALO_BF16, 3 * cw), lambda i: (jnp.maximum((n - 1 - i) * per - 1, 0), 0))
    n_rest = 3 * cw + 2 * d
    return pl.pallas_call(
        body, name="mix_bwd", grid=(n,),
        in_specs=[rows(d), rows(d), rows(aw), rows(3 * cw), prev_halo, rows(2 * d), _const_spec((CONV_K, cw)),
                  _const_spec((1, d)), ANY, ANY, ANY],
        out_specs=[rows(d), rows(d), rows(d), rows(aw), rows(n_rest), _const_spec((1, d)), _const_spec((1, 2 * d)),
                   _const_spec((CONV_K, cw))],
        out_shape=[jax.ShapeDtypeStruct((s, d), BF16), jax.ShapeDtypeStruct((s, d), BF16),
                   jax.ShapeDtypeStruct((s, d), BF16), jax.ShapeDtypeStruct((s, aw), BF16),
                   jax.ShapeDtypeStruct((s, n_rest), BF16), jax.ShapeDtypeStruct((1, d), F32),
                   jax.ShapeDtypeStruct((1, 2 * d), F32), jax.ShapeDtypeStruct((CONV_K, cw), F32)],
        scratch_shapes=[pltpu.VMEM(w_ao.shape, BF16), pltpu.VMEM(w_co.shape, BF16), pltpu.VMEM(w_o.shape, BF16),
                        pltpu.VMEM((HALO, cw), F32), pltpu.SemaphoreType.DMA((3,))],
        compiler_params=_params(1),
    )(dx1, mixed, o, conv, conv, gate, wconv, g_post, w_ao, w_co, w_o)


def _mlp_ple_loss(x1, p, target, g_pre, g_post, g_ple, w_up, w_dn, w_pg, w_pp, tm):
    s, d = x1.shape
    ff = w_up.shape[1]
    pd = p.shape[1]
    fc = FF_CHUNK

    def body(x1_ref, p_ref, t_ref, gpre_ref, gpost_ref, gple_ref, wup_hbm, wdn_hbm, wpg_hbm, wpp_hbm,
             dx1_ref, h2_ref, du_ref, a_ref, df_ref, h3_ref, ds3_ref, dpp_ref, loss_ref, dgpre_ref, dgpost_ref,
             dgple_ref, wup, wdn, wpg, wpp, u_scr, sem):
        _load_resident([(wup_hbm, wup), (wdn_hbm, wdn), (wpg_hbm, wpg), (wpp_hbm, wpp)], sem)

        @pl.when(pl.program_id(0) == 0)
        def _():
            for ref in (loss_ref, dgpre_ref, dgpost_ref, dgple_ref):
                ref[...] = jnp.zeros_like(ref)

        x1v = x1_ref[...]
        r2 = _rms_scale(x1v)
        x1hat = x1v * r2
        h2 = (x1hat * gpre_ref[...]).astype(BF16)
        h2_ref[...] = h2
        f = jnp.zeros((tm, d), F32)
        for c0 in range(0, ff, fc):
            u = _dot(h2, wup[:, c0:c0 + fc])
            u_scr[:, c0:c0 + fc] = u
            a = jnp.square(jnp.maximum(u, 0.0)).astype(BF16)
            a_ref[:, c0:c0 + fc] = a
            f = f + _dot(a, wdn[c0:c0 + fc, :])
        rf = _rms_scale(f)
        fhat = f * rf
        x2 = x1v + fhat * gpost_ref[...]
        r3 = _rms_scale(x2)
        x2hat = x2 * r3
        h3 = (x2hat * gple_ref[...]).astype(BF16)
        h3_ref[...] = h3
        pg = _sigmoid(_dot(h3, wpg[...]))
        pp = _dot(p_ref[...].astype(BF16), wpp[...])
        diff = x2 + pg * pp - t_ref[...]
        loss_ref[...] += 0.5 * jnp.sum(jnp.mean(diff * diff, axis=-1, keepdims=True), axis=0, keepdims=True)

        dy = diff * (1.0 / d)
        dpp_ref[...] = (dy * pg).astype(BF16)
        ds3 = (dy * pp * pg * (1.0 - pg)).astype(BF16)
        ds3_ref[...] = ds3
        dh3 = _dot_nt(ds3, wpg[...])
        dgple_ref[...] += jnp.sum(dh3 * x2hat, axis=0, keepdims=True)
        dx2 = dy + _rms_bwd(x2hat, r3, gple_ref[...], dh3)
        dgpost_ref[...] += jnp.sum(dx2 * fhat, axis=0, keepdims=True)
        df = _rms_bwd(fhat, rf, gpost_ref[...], dx2).astype(BF16)
        df_ref[...] = df
        dh2 = jnp.zeros((tm, d), F32)
        for c0 in range(0, ff, fc):
            da = _dot_nt(df, wdn[c0:c0 + fc, :])
            du = (da * (2.0 * jnp.maximum(u_scr[:, c0:c0 + fc], 0.0))).astype(BF16)
            du_ref[:, c0:c0 + fc] = du
            dh2 = dh2 + _dot_nt(du, wup[:, c0:c0 + fc])
        dgpre_ref[...] += jnp.sum(dh2 * x1hat, axis=0, keepdims=True)
        dx1_ref[...] = dx2 + _rms_bwd(x1hat, r2, gpre_ref[...], dh2)

    vec = _const_spec((1, d))
    return pl.pallas_call(
        body, name="mlp_ple_loss", grid=(s // tm,),
        in_specs=[_row_spec(tm, d), _row_spec(tm, pd), _row_spec(tm, d), vec, vec, vec, ANY, ANY, ANY, ANY],
        out_specs=[_row_spec(tm, d), _row_spec(tm, d), _row_spec(tm, ff), _row_spec(tm, ff), _row_spec(tm, d),
                   _row_spec(tm, d), _row_spec(tm, d), _row_spec(tm, d), _const_spec((1, 1)), vec, vec, vec],
        out_shape=[jax.ShapeDtypeStruct((s, d), F32), jax.ShapeDtypeStruct((s, d), BF16),
                   jax.ShapeDtypeStruct((s, ff), BF16), jax.ShapeDtypeStruct((s, ff), BF16),
                   jax.ShapeDtypeStruct((s, d), BF16), jax.ShapeDtypeStruct((s, d), BF16),
                   jax.ShapeDtypeStruct((s, d), BF16), jax.ShapeDtypeStruct((s, d), BF16),
                   jax.ShapeDtypeStruct((1, 1), F32), jax.ShapeDtypeStruct((1, d), F32),
                   jax.ShapeDtypeStruct((1, d), F32), jax.ShapeDtypeStruct((1, d), F32)],
        scratch_shapes=[pltpu.VMEM(w_up.shape, BF16), pltpu.VMEM(w_dn.shape, BF16), pltpu.VMEM(w_pg.shape, BF16),
                        pltpu.VMEM(w_pp.shape, BF16), pltpu.VMEM((tm, ff), F32), pltpu.SemaphoreType.DMA((4,))],
        compiler_params=_params(1),
    )(x1, p, target, g_pre, g_post, g_ple, w_up, w_dn, w_pg, w_pp)


def _in_proj_bwd(x, dx1, pieces, g1, w_in, tm, exchange=None):
    s, d = x.shape
    ni = w_in.shape[1]
    widths = [p.shape[1] for p in pieces]
    grid = (s // tm,)
    ex = exchange or _NO_EXCHANGE

    def body(x_ref, dx1_ref, *rest):
        piece_refs, rest = rest[:len(pieces)], rest[len(pieces):]
        g_ref, w_hbm = rest[0], rest[1]
        ex_in, (dx_ref, dg_ref), ex_out, (w_vmem, sem, *sems) = _split_refs(rest[2:], ex, 2)
        _exchange_start(ex, ex_in, ex_out, sems, grid)
        _load_resident([(w_hbm, w_vmem)], sem)

        @pl.when(pl.program_id(0) == 0)
        def _():
            dg_ref[...] = jnp.zeros_like(dg_ref)

        dh = jnp.zeros((tm, d), F32)
        c0 = 0
        for ref, width in zip(piece_refs, widths):
            dh = dh + _dot_nt(ref[...], w_vmem[:, c0:c0 + width])
            c0 += width
        xv = x_ref[...]
        r = _rms_scale(xv)
        xhat = xv * r
        dg_ref[...] += jnp.sum(dh * xhat, axis=0, keepdims=True)
        dx_ref[...] = dx1_ref[...] + _rms_bwd(xhat, r, g_ref[...], dh)
        _exchange_wait(ex, ex_in, ex_out, sems, grid)

    return pl.pallas_call(
        body, name="in_proj_bwd", grid=grid,
        in_specs=[_row_spec(tm, d), _row_spec(tm, d)] + [_row_spec(tm, w) for w in widths]
        + [_const_spec((1, d)), ANY] + [ANY] * len(ex.arrays),
        out_specs=[_row_spec(tm, d), _const_spec((1, d))] + [ANY] * len(ex.out_shapes),
        out_shape=[jax.ShapeDtypeStruct((s, d), F32), jax.ShapeDtypeStruct((1, d), F32)] + ex.out_shapes,
        scratch_shapes=[pltpu.VMEM((d, ni), BF16), pltpu.SemaphoreType.DMA((1,))] + _exchange_sems(ex),
        compiler_params=_params(1),
    )(x, dx1, *pieces, g1, w_in, *ex.arrays)


def _weight_grad(a, b, name, into=None, col0=0, n_total=None):
    s, m = a.shape
    n = b.shape[1]
    tm, tk = min(m, DW_TILE), min(s, DW_TOKENS)
    tn = min(n, DW_TILE) if n_total is None else DW_PIECE_TILE
    nk = s // tk
    j0 = col0 // tn
    assert m % tm == 0 and n % tn == 0 and col0 % tn == 0

    def body(a_ref, b_ref, *rest):
        o_ref, acc = rest[-2:]
        k = pl.program_id(2)

        @pl.when(k == 0)
        def _():
            acc[...] = jnp.zeros_like(acc)

        acc[...] += _dot_tn(a_ref[...].astype(BF16), b_ref[...].astype(BF16))

        @pl.when(k == nk - 1)
        def _():
            o_ref[...] = acc[...].astype(BF16)

    extra = [] if into is None else [into]
    return pl.pallas_call(
        body, name=name, grid=(m // tm, n // tn, nk),
        in_specs=[pl.BlockSpec((tk, tm), lambda i, j, k: (k, i)), pl.BlockSpec((tk, tn), lambda i, j, k: (k, j))]
        + [ANY] * len(extra),
        out_specs=pl.BlockSpec((tm, tn), lambda i, j, k: (i, j0 + j)),
        out_shape=jax.ShapeDtypeStruct((m, n_total or n), BF16),
        input_output_aliases={2: 0} if extra else {},
        scratch_shapes=[pltpu.VMEM((tm, tn), F32)],
        compiler_params=_params(3),
    )(a, b, *extra)


def _mesh_position():
    return tuple(lax.axis_index(a) for a in MESH_AXES)


def _peer(me, k):
    bits = ((k >> 2) & 1, (k >> 1) & 1, k & 1)
    pos = tuple(1 - m if b else m for m, b in zip(me, bits))
    return pos, 4 * pos[0] + 2 * pos[1] + pos[2]


class _Exchange:
    def __init__(self, arrays, out_shapes, src, dst, relayed=None):
        self.arrays, self.out_shapes, self.src, self.dst = list(arrays), list(out_shapes), src, dst
        self.relayed = list(relayed) if relayed is not None else [False] * len(self.arrays)


_NO_EXCHANGE = _Exchange([], [], None, None)


def _exchange_sems(ex):
    n = len(ex.arrays)
    if n == 0:
        return []
    return [pltpu.SemaphoreType.DMA((n, N_DEV - 1)), pltpu.SemaphoreType.DMA((n, N_DEV - 1)),
            pltpu.SemaphoreType.DMA((n,))]


def _split_refs(rest, ex, n_own_outs):
    n_in, n_out = len(ex.arrays), len(ex.out_shapes)
    ex_in, rest = rest[:n_in], rest[n_in:]
    own, rest = rest[:n_own_outs], rest[n_own_outs:]
    return ex_in, own, rest[:n_out], rest[n_out:]


def _direct_steps(ex, w, in_refs, out_refs, sems):
    send_sems, recv_sems, local_sems = sems
    me = _mesh_position()
    mine = 4 * me[0] + 2 * me[1] + me[2]

    def copy(k):
        landing = ex.dst(w, out_refs, mine)
        if k == 0:
            return pltpu.make_async_copy(ex.src(w, in_refs, mine), landing, local_sems.at[w])
        peer, peer_idx = _peer(me, k)
        return pltpu.make_async_remote_copy(
            src_ref=ex.src(w, in_refs, peer_idx), dst_ref=landing, send_sem=send_sems.at[w, k - 1],
            recv_sem=recv_sems.at[w, k - 1], device_id=peer, device_id_type=pl.DeviceIdType.MESH)

    ks = range(N_DEV)
    return [lambda k=k: copy(k).start() for k in ks], [], [lambda k=k: copy(k).wait() for k in ks]


def _relayed_steps(ex, w, in_refs, out_refs, sems):
    send_sems, recv_sems, local_sems = sems
    x, y, c = _mesh_position()
    chips = [(1 - x, y), (x, 1 - y), (1 - x, 1 - y)]
    sibling = (x, y, 1 - c)
    js = range(len(chips))

    def block(px, py, pc):
        return ex.dst(w, out_refs, 4 * px + 2 * py + pc)

    def copy(k, dst, to, src=None):
        return pltpu.make_async_remote_copy(
            src_ref=ex.src(w, in_refs, None) if src is None else src, dst_ref=dst, send_sem=send_sems.at[w, k],
            recv_sem=recv_sems.at[w, k], device_id=to, device_id_type=pl.DeviceIdType.MESH)

    def local():
        return pltpu.make_async_copy(ex.src(w, in_refs, None), block(x, y, c), local_sems.at[w])

    def own(k):
        return copy(k, block(x, y, c), sibling if k == 0 else (*chips[k - 1], c))

    def came(j):
        return copy(1 + j, block(*chips[j], c), (*chips[j], c))

    def passed(j):
        return copy(4 + j, block(*chips[j], c), sibling, src=block(*chips[j], c))

    def from_sibling(k):
        return copy(k, block(x, y, 1 - c) if k == 0 else block(*chips[k - 4], 1 - c), sibling)

    start = [lambda: local().start()] + [lambda k=k: own(k).start() for k in range(4)]
    relay = [step for j in js for step in (lambda j=j: came(j).wait_recv(), lambda j=j: passed(j).start())]
    finish = ([lambda: local().wait()] + [lambda k=k: own(k).wait_send() for k in range(4)]
              + [lambda j=j: passed(j).wait_send() for j in js]
              + [lambda k=k: from_sibling(k).wait_recv() for k in (0, 4, 5, 6)])
    return start, relay, finish


def _exchange_steps(ex, in_refs, out_refs, sems):
    start, relay, finish = [], [], []
    for w in range(len(ex.arrays)):
        steps = (_relayed_steps if ex.relayed[w] else _direct_steps)(ex, w, in_refs, out_refs, sems)
        start += steps[0]
        relay += steps[1]
        finish += steps[2]
    return start, relay, finish


def _run(steps):
    for step in steps:
        step()


def _at_grid_step(grid, where):
    target = {"first": [0] * len(grid), "middle": [grid[0] // 2] + [0] * (len(grid) - 1),
              "last": [g - 1 for g in grid]}[where]
    hit = pl.program_id(0) == target[0]
    for axis in range(1, len(grid)):
        hit = jnp.logical_and(hit, pl.program_id(axis) == target[axis])
    return hit


def _exchange_start(ex, in_refs, out_refs, sems, grid):
    if ex.arrays:
        @pl.when(_at_grid_step(grid, "first"))
        def _():
            _run(_exchange_steps(ex, in_refs, out_refs, sems)[0])

        if any(ex.relayed):
            assert grid[0] >= 2

            @pl.when(_at_grid_step(grid, "middle"))
            def _():
                _run(_exchange_steps(ex, in_refs, out_refs, sems)[1])


def _exchange_wait(ex, in_refs, out_refs, sems, grid):
    if ex.arrays:
        @pl.when(_at_grid_step(grid, "last"))
        def _():
            _run(_exchange_steps(ex, in_refs, out_refs, sems)[2])


def _shard_block(ref, shard_shape, by_col, idx):
    r, c = shard_shape
    if by_col:
        return ref.at[:, pl.ds(pl.multiple_of(idx * c, LANES), c)]
    return ref.at[pl.ds(pl.multiple_of(idx * r, 16), r), :]


def _full_shape(shard_shape, by_col):
    r, c = shard_shape
    return (r, N_DEV * c) if by_col else (N_DEV * r, c)


def _gather_exchange(shards, col_sharded):
    shapes = [a.shape for a in shards]
    return _Exchange(
        shards, [jax.ShapeDtypeStruct(_full_shape(sh, bc), a.dtype) for a, sh, bc in zip(shards, shapes, col_sharded)],
        lambda w, refs, idx: refs[w],
        lambda w, refs, idx: _shard_block(refs[w], shapes[w], col_sharded[w], idx), [True] * len(shards))


def _scatter_exchange(grads, col_sharded):
    shapes = []
    for g, by_col in zip(grads, col_sharded):
        r, c = g.shape
        shapes.append((r, c // N_DEV) if by_col else (r // N_DEV, c))
    return _Exchange(
        grads, [jax.ShapeDtypeStruct((N_DEV,) + sh, g.dtype) for g, sh in zip(grads, shapes)],
        lambda w, refs, idx: _shard_block(refs[w], shapes[w], col_sharded[w], idx),
        lambda w, refs, mine: refs[w].at[mine])


def _broadcast_exchange(arrays):
    return _Exchange(arrays, [jax.ShapeDtypeStruct((N_DEV,) + a.shape, a.dtype) for a in arrays],
                     lambda w, refs, idx: refs[w], lambda w, refs, mine: refs[w].at[mine])


def _join(*exs):
    arrays, shapes, owner = [], [], []
    for e in exs:
        for w in range(len(e.arrays)):
            owner.append((e, w, len(arrays), len(shapes)))
        arrays += e.arrays
        shapes += e.out_shapes

    def src(w, refs, idx):
        e, w0, i0, _ = owner[w]
        return e.src(w0, refs[i0:i0 + len(e.arrays)], idx)

    def dst(w, refs, idx):
        e, w0, _, o0 = owner[w]
        return e.dst(w0, refs[o0:o0 + len(e.out_shapes)], idx)

    return _Exchange(arrays, shapes, src, dst, [flag for e in exs for flag in e.relayed])


def _exchange_call(ex, name):
    n_in = len(ex.arrays)

    def body(*refs):
        in_refs, _, out_refs, sems = _split_refs(refs, ex, 0)
        for steps in _exchange_steps(ex, in_refs, out_refs, sems):
            _run(steps)

    return pl.pallas_call(
        body, name=name, in_specs=[ANY] * n_in, out_specs=[ANY] * len(ex.out_shapes), out_shape=ex.out_shapes,
        scratch_shapes=_exchange_sems(ex), compiler_params=pltpu.CompilerParams(vmem_limit_bytes=VMEM_LIMIT),
    )(*ex.arrays)


def _to_bf16(arrays):
    def body(*refs):
        for src, dst in zip(refs[:len(arrays)], refs[len(arrays):]):
            dst[...] = src[...].astype(BF16)

    vmem = pl.BlockSpec(memory_space=pltpu.VMEM)
    return pl.pallas_call(
        body, name="weights_to_bf16", in_specs=[vmem] * len(arrays), out_specs=[vmem] * len(arrays),
        out_shape=[jax.ShapeDtypeStruct(a.shape, BF16) for a in arrays],
        compiler_params=pltpu.CompilerParams(vmem_limit_bytes=VMEM_LIMIT),
    )(*arrays)


def _adamw(w, g, m, v):
    m = ADAM_B1 * m + (1.0 - ADAM_B1) * g
    v = ADAM_B2 * v + (1.0 - ADAM_B2) * jnp.square(g)
    m_hat = m / (1.0 - ADAM_B1 ** ADAM_STEP)
    v_hat = v / (1.0 - ADAM_B2 ** ADAM_STEP)
    delta = -ADAM_LR * (m_hat / (jnp.sqrt(v_hat) + ADAM_EPS) + ADAM_WD * w)
    return delta, m, v


def _sum_and_adamw(parts, w, m, v, name):
    r, c = w.shape
    tr = min(r, 256)

    def body(p_ref, w_ref, m_ref, v_ref, g_out, d_out, m_out, v_out):
        g = p_ref[0].astype(F32)
        for dev in range(1, N_DEV):
            g = g + p_ref[dev].astype(F32)
        g_out[...] = g
        d_out[...], m_out[...], v_out[...] = _adamw(w_ref[...], g, m_ref[...], v_ref[...])

    blk = pl.BlockSpec((tr, c), lambda i: (i, 0))
    return pl.pallas_call(
        body, name=name, grid=(r // tr,),
        in_specs=[pl.BlockSpec((N_DEV, tr, c), lambda i: (0, i, 0)), blk, blk, blk],
        out_specs=[blk] * 4, out_shape=[jax.ShapeDtypeStruct((r, c), F32)] * 4,
        compiler_params=_params(1),
    )(parts, w, m, v)


BIG = ("w_in", "w_attn_out", "w_conv_out", "w_o", "w_up", "w_down", "w_ple_gate", "w_ple_proj")
COL_SHARDED = {"w_in": True, "w_attn_out": True, "w_conv_out": True, "w_o": False, "w_up": True, "w_down": False,
               "w_ple_gate": False, "w_ple_proj": True}
SMALL = ("g_pre_mix", "b_gate", "g_post_mix", "g_pre_mlp", "g_post_mlp", "g_ple")


REST = BIG[1:]


def _local_grads(x, p, target, small, wconv, full, aw, cw, tm, t, gather_rest=None, scatter_rest=None,
                 scatter_in=None):
    full = dict(full)
    qkv, conv, gate, h1 = _in_proj_fwd(x, small["g_pre_mix"], small["b_gate"], full["w_in"], aw, cw, tm)
    o, *rest = _attn_fwd(qkv, aw, t, gather_rest)
    full.update(zip(REST, rest))
    x1, mixed, mix_in, conv_in = _mix_fwd(x, o, conv, gate, wconv, small["g_post_mix"], full["w_attn_out"],
                                          full["w_conv_out"], full["w_o"], tm)
    (dx1, h2, du, a, df, h3, ds3, dpp, loss, dg_pre_mlp, dg_post_mlp, dg_ple) = _mlp_ple_loss(
        x1, p, target, small["g_pre_mlp"], small["g_post_mlp"], small["g_ple"], full["w_up"], full["w_down"],
        full["w_ple_gate"], full["w_ple_proj"], tm)
    big = {"w_up": _weight_grad(h2, du, "dw_up"), "w_down": _weight_grad(a, df, "dw_down"),
           "w_ple_gate": _weight_grad(h3, ds3, "dw_ple_gate"), "w_ple_proj": _weight_grad(p, dpp, "dw_ple_proj")}
    (dmixed, dattn, dconvout, do, drest, dg_post_mix, db_gate, dwconv) = _mix_bwd(
        dx1, mixed, o, conv, gate, wconv, small["g_post_mix"], full["w_attn_out"], full["w_conv_out"], full["w_o"],
        tm)
    big.update({"w_attn_out": _weight_grad(o, dattn, "dw_attn_out"),
                "w_conv_out": _weight_grad(conv_in, dconvout, "dw_conv_out"),
                "w_o": _weight_grad(mix_in, dmixed, "dw_o")})
    dq, dk, dv, *scattered = _attn_bwd(qkv, o, do, aw, t, scatter_rest and scatter_rest([big[n] for n in REST]))
    pieces = [dq, dk, dv, drest]
    dw_in, col0, ni = None, 0, full["w_in"].shape[1]
    for i, piece in enumerate(pieces):
        dw_in = _weight_grad(h1, piece, "dw_in_%d" % i, dw_in, col0, ni)
        col0 += piece.shape[1]
    big["w_in"] = dw_in
    dx, dg_pre_mix, *scattered_in = _in_proj_bwd(x, dx1, pieces, small["g_pre_mix"], full["w_in"], tm,
                                                scatter_in and scatter_in(dw_in))
    small_grads = {"g_pre_mix": dg_pre_mix, "b_gate": db_gate, "g_post_mix": dg_post_mix, "g_pre_mlp": dg_pre_mlp,
                   "g_post_mlp": dg_post_mlp, "g_ple": dg_ple, "w_conv": dwconv}
    return loss[0, 0], dx, big, small_grads, scattered_in + scattered


def _pack_small(vals, d):
    parts = []
    for a in vals:
        a = jnp.pad(a.reshape(-1), (0, -a.size % d)).reshape(-1, d)
        parts.append(jnp.pad(a, ((0, HALO - a.shape[0]), (0, 0))))
    return jnp.concatenate(parts, axis=0)


def _unpack_small(pack, shapes, d):
    out = []
    for i, shp in enumerate(shapes):
        n = 1
        for v in shp:
            n *= v
        rows = -(-n // d)
        out.append(pack[i * HALO:i * HALO + rows].reshape(-1)[:n].reshape(shp))
    return out


def kernel(x, p, g_pre_mix, w_in, b_gate, w_conv, w_attn_out, w_conv_out, w_o, g_post_mix, g_pre_mlp, w_up, w_down, g_post_mlp, g_ple, w_ple_gate, w_ple_proj, loss_target, m_g_pre_mix, m_w_in, m_b_gate, m_w_conv, m_w_attn_out, m_w_conv_out, m_w_o, m_g_post_mix, m_g_pre_mlp, m_w_up, m_w_down, m_g_post_mlp, m_g_ple, m_w_ple_gate, m_w_ple_proj, v_g_pre_mix, v_w_in, v_b_gate, v_w_conv, v_w_attn_out, v_w_conv_out, v_w_o, v_g_post_mix, v_g_pre_mlp, v_w_up, v_w_down, v_g_post_mlp, v_g_ple, v_w_ple_gate, v_w_ple_proj):
    given = dict(locals())
    order = ["g_pre_mix", "w_in", "b_gate", "w_conv", "w_attn_out", "w_conv_out", "w_o", "g_post_mix", "g_pre_mlp",
             "w_up", "w_down", "g_post_mlp", "g_ple", "w_ple_gate", "w_ple_proj"]
    d = x.shape[-1]
    me = 4 * lax.axis_index("x") + 2 * lax.axis_index("y") + lax.axis_index("c")

    col = [COL_SHARDED[n] for n in BIG]
    shards = _to_bf16([given[n][0] for n in BIG])
    cw_shard = w_conv.shape[-1]
    conv_tile = jnp.pad(w_conv[0], ((0, HALO - CONV_K), (0, LANES - cw_shard)))
    w_in_full, conv_g = _exchange_call(
        _join(_gather_exchange(shards[:1], col[:1]), _broadcast_exchange([conv_tile])), "gather_w_in")
    wconv = jnp.concatenate([conv_g[dev, :CONV_K, :cw_shard] for dev in range(N_DEV)], axis=1)

    small = {n: given[n] for n in SMALL}
    loss, dx, big_grads, small_grads, parts = _local_grads(
        x[0], p[0, 0], loss_target[0], small, wconv, {"w_in": w_in_full}, w_attn_out.shape[1], w_conv_out.shape[1],
        ROW_BLOCK, ATTN_BLOCK,
        _gather_exchange(shards[1:], col[1:]), lambda grads: _scatter_exchange(grads, col[1:]),
        lambda grad: _scatter_exchange([grad], col[:1]))
    loss = lax.psum(loss, MESH_AXES)

    small_names = list(SMALL) + ["w_conv"]
    pack = _pack_small([small_grads[n] for n in small_names], d)
    packs, = _exchange_call(_broadcast_exchange([pack]), "share_small_grads")

    grads, deltas, new_m, new_v = {}, {}, {}, {}
    for n, part in zip(BIG, parts):
        grads[n], deltas[n], new_m[n], new_v[n] = (
            a[None] for a in _sum_and_adamw(part, given[n][0], given["m_" + n][0], given["v_" + n][0], "adamw_" + n))

    full_conv = lambda a: lax.dynamic_update_slice(jnp.zeros((CONV_K, N_DEV * cw_shard), F32), a[0],
                                                   (jnp.int32(0), me * cw_shard))
    state = [_pack_small([given[pre + n] for n in SMALL] + [full_conv(given[pre + "w_conv"])], d)
             for pre in ("", "m_", "v_")]
    outs = _sum_and_adamw(packs, *state, "adamw_small")
    shapes = [given[n].shape for n in SMALL] + [(CONV_K, N_DEV * cw_shard)]
    for res, dst in zip(outs, (grads, deltas, new_m, new_v)):
        for n, a in zip(small_names, _unpack_small(res, shapes, d)):
            dst[n] = (lax.dynamic_slice(a, (jnp.int32(0), me * cw_shard), (CONV_K, cw_shard))[None]
                      if n == "w_conv" else a)

    return (loss, dx[None], *[grads[n] for n in order], *[deltas[n] for n in order],
            *[new_m[n] for n in order], *[new_v[n] for n in order])
```

```python
import jax
import jax.numpy as jnp
from jax import lax
from jax.experimental import pallas as pl
from jax.experimental.pallas import tpu as pltpu

F32 = jnp.float32
BF16 = jnp.bfloat16
RMS_EPS = 1e-6
N_DEV = 8
MESH_AXES = ("x", "y", "c")
LANES = 128
HEAD_DIM = 64
HEADS_PER_GROUP = LANES // HEAD_DIM
CONV_K = 3
HALO = 8
HALO_BF16 = 16
VMEM_LIMIT = 56 * 1024 * 1024
EXP2_ZERO = -150.0
LOG2_E = 1.4426950408889634

ADAM_LR = 0.001
ADAM_B1 = 0.9
ADAM_B2 = 0.999
ADAM_EPS = 1e-08
ADAM_WD = 0.01
ADAM_STEP = 10

ROW_BLOCK = 256
ATTN_BLOCK = 256
ATTN_ROW_SPLITS = 2
DW_TOKENS = 2048
DW_TILE = 1024
DW_PIECE_TILE = 512
FF_CHUNK = 1024
PROJ_CHUNK = 512


def _dot(a, b):
    return lax.dot_general(a, b, (((1,), (0,)), ((), ())), preferred_element_type=F32)


def _dot_nt(a, b):
    return lax.dot_general(a, b, (((1,), (1,)), ((), ())), preferred_element_type=F32)


def _dot_tn(a, b):
    return lax.dot_general(a, b, (((0,), (0,)), ((), ())), preferred_element_type=F32)


def _sigmoid(z):
    return 1.0 / (1.0 + jnp.exp(-z))


def _rms_scale(x):
    return lax.rsqrt(jnp.mean(x * x, axis=-1, keepdims=True) + RMS_EPS)


def _rms_bwd(xhat, r, g, dy):
    gd = dy * g
    return r * (gd - xhat * jnp.mean(gd * xhat, axis=-1, keepdims=True))


def _params(n_axes, **kw):
    return pltpu.CompilerParams(dimension_semantics=("arbitrary",) * n_axes, vmem_limit_bytes=VMEM_LIMIT, **kw)


def _load_resident(pairs, sem):
    @pl.when(pl.program_id(0) == 0)
    def _():
        copies = [pltpu.make_async_copy(src, dst, sem.at[i]) for i, (src, dst) in enumerate(pairs)]
        for cp in copies:
            cp.start()
        for cp in copies:
            cp.wait()


def _row_spec(tm, width):
    return pl.BlockSpec((tm, width), lambda i: (i, 0))


def _prev_halo_spec(tm, width, rows):
    per = tm // rows
    return pl.BlockSpec((rows, width), lambda i: (jnp.maximum(i * per - 1, 0), 0))


def _const_spec(shape):
    return pl.BlockSpec(shape, lambda i: (0,) * len(shape))


ANY = pl.BlockSpec(memory_space=pl.ANY)


def _shift_down(cur, prev, n):
    rows = lax.broadcasted_iota(jnp.int32, cur.shape, 0)
    out = pltpu.roll(cur, n, 0)
    for j in range(n):
        out = jnp.where(rows == j, prev[prev.shape[0] - n + j:prev.shape[0] - n + j + 1, :], out)
    return out


def _shift_up(cur, nxt, n):
    tm = cur.shape[0]
    rows = lax.broadcasted_iota(jnp.int32, cur.shape, 0)
    out = pltpu.roll(cur, tm - n, 0)
    for j in range(n):
        out = jnp.where(rows == tm - n + j, nxt[j:j + 1, :], out)
    return out


def _conv_taps(cm, cm_prev, wconv):
    cm1 = _shift_down(cm, cm_prev, 1)
    cm2 = _shift_down(cm, cm_prev, 2)
    cv = wconv[2:3, :] * cm + wconv[1:2, :] * cm1 + wconv[0:1, :] * cm2
    return cv, cm1, cm2


def _in_proj_fwd(x, g1, b_gate, w_in, aw, cw, tm):
    s, d = x.shape
    ni = w_in.shape[1]
    n_qkv, n_conv = 3 * aw, 3 * cw
    ch = PROJ_CHUNK

    def body(x_ref, g_ref, b_ref, w_hbm, qkv_ref, conv_ref, gate_ref, h_ref, w_vmem, sem):
        _load_resident([(w_hbm, w_vmem)], sem)
        xv = x_ref[...]
        h = (xv * _rms_scale(xv) * g_ref[...]).astype(BF16)
        h_ref[...] = h
        for c0 in range(0, ni, ch):
            pc = _dot(h, w_vmem[:, c0:c0 + ch])
            if c0 < n_qkv:
                qkv_ref[:, c0:c0 + ch] = pc.astype(BF16)
            elif c0 < n_qkv + n_conv:
                conv_ref[:, c0 - n_qkv:c0 - n_qkv + ch] = pc.astype(BF16)
            else:
                g0 = c0 - n_qkv - n_conv
                gate_ref[:, g0:g0 + ch] = _sigmoid(pc + b_ref[:, g0:g0 + ch]).astype(BF16)

    return pl.pallas_call(
        body, name="in_proj_fwd", grid=(s // tm,),
        in_specs=[_row_spec(tm, d), _const_spec((1, d)), _const_spec((1, 2 * d)), ANY],
        out_specs=[_row_spec(tm, n_qkv), _row_spec(tm, n_conv), _row_spec(tm, 2 * d), _row_spec(tm, d)],
        out_shape=[jax.ShapeDtypeStruct((s, n_qkv), BF16), jax.ShapeDtypeStruct((s, n_conv), BF16),
                   jax.ShapeDtypeStruct((s, 2 * d), BF16), jax.ShapeDtypeStruct((s, d), BF16)],
        scratch_shapes=[pltpu.VMEM((d, ni), BF16), pltpu.SemaphoreType.DMA((1,))],
        compiler_params=_params(1),
    )(x, g1, b_gate, w_in)


def _split_hi_lo(a):
    hi = a.astype(BF16)
    return hi, (a - hi.astype(F32)).astype(BF16)


def _log2_gates(z):
    z2 = z * LOG2_E
    nz2 = -z2
    log_keep = jnp.minimum(nz2, 0.0) - jnp.log2(1.0 + jnp.exp2(jnp.minimum(z2, nz2)))
    return log_keep + z2, log_keep


def _attn_masks(t):
    row = lax.broadcasted_iota(jnp.int32, (t, t), 0)
    col = lax.broadcasted_iota(jnp.int32, (t, t), 1)
    return (col < row).astype(F32), (row > col).astype(BF16), (row >= col).astype(BF16)


def _chains(a):
    tr = a.shape[0] // ATTN_ROW_SPLITS
    return [jnp.where(_head_lanes(h), a[r * tr:(r + 1) * tr], jnp.zeros((tr, LANES), a.dtype))
            for h in range(HEADS_PER_GROUP) for r in range(ATTN_ROW_SPLITS)]


def _merge_chains(parts):
    rows = []
    for r in range(ATTN_ROW_SPLITS):
        out = parts[r]
        for h in range(1, HEADS_PER_GROUP):
            out = jnp.where(_head_lanes(h), parts[h * ATTN_ROW_SPLITS + r], out)
        rows.append(out)
    return jnp.concatenate(rows, axis=0)


def _by_stage(n_chains, stages):
    for stage in stages:
        for c in range(n_chains):
            stage(c)


def _row_parts(a):
    tr = a.shape[0] // ATTN_ROW_SPLITS
    return [a[r * tr:(r + 1) * tr] for r in range(ATTN_ROW_SPLITS)]


def _while_weights_live(qi, block, carry):
    def cond(state):
        j, carry = state
        live = jnp.max(carry[0][0])
        for run in carry[0][1:]:
            live = jnp.maximum(live, jnp.max(run))
        return jnp.logical_and(j < qi, live >= EXP2_ZERO)

    def step(state):
        j, carry = state
        return j + 1, block(qi - 1 - j, carry)

    return lax.while_loop(cond, step, (jnp.int32(0), carry))[1]


def _head_lanes(h):
    lane = lax.broadcasted_iota(jnp.int32, (1, LANES), 1)
    return (lane >= HEAD_DIM * h) & (lane < HEAD_DIM * (h + 1))


def _attn_fwd(qkv, aw, t, exchange=None):
    s = qkv.shape[0]
    groups = aw // LANES
    nq = s // t
    scale = HEAD_DIM ** -0.5
    ex = exchange or _NO_EXCHANGE
    causal, upper, _ = _attn_masks(t)
    mask_spec = pl.BlockSpec((t, t), lambda g, i: (0, 0))

    def body(q_ref, k_ref, v_ref, causal_ref, upper_ref, *rest):
        ex_in, (o_ref,), ex_out, sems = _split_refs(rest, ex, 1)
        qi = pl.program_id(1)
        _exchange_start(ex, ex_in, ex_out, sems, (groups, nq))
        upper = upper_ref[...]
        causal = _row_parts(causal_ref[...] > 0.5) * HEADS_PER_GROUP
        qs = _chains(q_ref[...] * scale)
        heads = range(len(qs))
        tr = t // ATTN_ROW_SPLITS

        def block(kb, runs, accs, diag):
            rows = pl.ds(pl.multiple_of(kb * t, t), t)
            k = k_ref[rows, :]
            v = v_ref[rows, :]
            ncs = [(h % ATTN_ROW_SPLITS + 1) * tr if diag else t for h in heads]
            live = [{} for _ in heads]
            new_runs, new_accs = [None] * len(heads), [None] * len(heads)

            def scores(h):
                live[h]["z"] = _dot_nt(qs[h], k[0:ncs[h]])

            def gates(h):
                nc = ncs[h]
                log_b, log_keep = _log2_gates(live[h].pop("z"))
                if diag:
                    log_keep = jnp.where(causal[h][:, 0:nc], log_keep, 0.0)
                hi, lo = _split_hi_lo(log_keep)
                live[h]["log_w"] = log_b + runs[h]
                live[h]["between"] = _dot(hi, upper[0:nc, 0:nc]) + _dot(lo, upper[0:nc, 0:nc])
                new_runs[h] = runs[h] + jnp.sum(log_keep, axis=1, keepdims=True)

            def weights(h):
                nc = ncs[h]
                w = jnp.exp2(live[h].pop("log_w") + live[h].pop("between"))
                if diag:
                    w = jnp.where(causal[h][:, 0:nc], w, 0.0)
                new_accs[h] = accs[h] + _dot(w.astype(BF16), v[0:nc])

            _by_stage(len(heads), [scores, gates, weights])
            return tuple(new_runs), tuple(new_accs)

        carry = block(qi, [jnp.zeros((tr, 1), F32)] * len(heads), [jnp.zeros((tr, LANES), F32)] * len(heads), True)
        _, accs = _while_weights_live(qi, lambda kb, carry: block(kb, *carry, False), carry)
        o_ref[...] = _merge_chains(accs)
        _exchange_wait(ex, ex_in, ex_out, sems, (groups, nq))

    return pl.pallas_call(
        body, name="attn_fwd", grid=(groups, nq),
        in_specs=[pl.BlockSpec((t, LANES), lambda g, i: (i, g)),
                  pl.BlockSpec((s, LANES), lambda g, i: (0, groups + g)),
                  pl.BlockSpec((s, LANES), lambda g, i: (0, 2 * groups + g)), mask_spec, mask_spec]
        + [ANY] * len(ex.arrays),
        out_specs=[pl.BlockSpec((t, LANES), lambda g, i: (i, g))] + [ANY] * len(ex.out_shapes),
        out_shape=[jax.ShapeDtypeStruct((s, aw), F32)] + ex.out_shapes,
        scratch_shapes=_exchange_sems(ex),
        compiler_params=_params(2),
    )(qkv, qkv, qkv, causal, upper, *ex.arrays)


def _attn_bwd(qkv, o, do, aw, t, exchange=None):
    s = qkv.shape[0]
    groups = aw // LANES
    nq = s // t
    scale = HEAD_DIM ** -0.5
    ex = exchange or _NO_EXCHANGE

    def body(q_ref, k_ref, v_ref, o_ref, do_ref, causal_ref, upper_ref, lower_ref, *rest):
        ex_in, (dq_ref, dk_ref, dv_ref), ex_out, (dk_acc, dv_acc, *sems) = _split_refs(rest, ex, 3)
        qi = pl.program_id(1)
        _exchange_start(ex, ex_in, ex_out, sems, (groups, nq))

        @pl.when(qi == 0)
        def _():
            dk_acc[...] = jnp.zeros_like(dk_acc)
            dv_acc[...] = jnp.zeros_like(dv_acc)

        upper = upper_ref[...]
        lower_incl = lower_ref[...]
        causal = _row_parts(causal_ref[...] > 0.5) * HEADS_PER_GROUP
        q = q_ref[...] * scale
        do_b = do_ref[...]
        qs = _chains(q)
        dos = _chains(do_b)
        qs_all = jnp.concatenate(qs, axis=0)
        dos_all = jnp.concatenate(dos, axis=0)
        e_totals = [jnp.sum(part, axis=1, keepdims=True) for part in _chains(do_b.astype(F32) * o_ref[...])]
        heads = range(len(qs))
        tr = t // ATTN_ROW_SPLITS

        def block(kb, runs, e_runs, dqs, diag):
            rows = pl.ds(pl.multiple_of(kb * t, t), t)
            k = k_ref[rows, :]
            v = v_ref[rows, :]
            ncs = [(h % ATTN_ROW_SPLITS + 1) * tr if diag else t for h in heads]
            live = [{} for _ in heads]
            none = [None] * len(heads)
            new_runs, new_e_runs, new_dqs, dzbs, wbs = list(none), list(none), list(none), list(none), list(none)

            def scores(h):
                live[h]["z"] = _dot_nt(qs[h], k[0:ncs[h]])
                live[h]["dw"] = _dot_nt(dos[h], v[0:ncs[h]])

            def gates(h):
                nc = ncs[h]
                log_b, log_keep = _log2_gates(live[h].pop("z"))
                live[h]["beta"] = jnp.exp2(log_b)
                live[h]["keep"] = jnp.exp2(log_keep)
                if diag:
                    log_keep = jnp.where(causal[h][:, 0:nc], log_keep, 0.0)
                hi, lo = _split_hi_lo(log_keep)
                live[h]["log_w"] = log_b + runs[h]
                live[h]["between"] = _dot(hi, upper[0:nc, 0:nc]) + _dot(lo, upper[0:nc, 0:nc])
                new_runs[h] = runs[h] + jnp.sum(log_keep, axis=1, keepdims=True)

            def weights(h):
                nc = ncs[h]
                w = jnp.exp2(live[h].pop("log_w") + live[h].pop("between"))
                if diag:
                    w = jnp.where(causal[h][:, 0:nc], w, 0.0)
                wb = w.astype(BF16)
                e = live[h].pop("dw") * wb.astype(F32)
                hi, lo = _split_hi_lo(e)
                live[h]["e"] = e
                live[h]["e_suffix"] = _dot(hi, lower_incl[0:nc, 0:nc]) + _dot(lo, lower_incl[0:nc, 0:nc]) + e_runs[h]
                wbs[h] = wb

            def score_grads(h):
                nc = ncs[h]
                e_suffix = live[h].pop("e_suffix")
                dz = live[h].pop("e") * live[h].pop("keep") - (e_totals[h] - e_suffix) * live[h].pop("beta")
                if diag:
                    dz = jnp.where(causal[h][:, 0:nc], dz, 0.0)
                dzb = dz.astype(BF16)
                new_dqs[h] = dqs[h] + _dot(dzb, k[0:nc])
                new_e_runs[h] = e_suffix[:, 0:1]
                if nc < t:
                    unseen = jnp.zeros((tr, t - nc), BF16)
                    dzb = jnp.concatenate([dzb, unseen], axis=1)
                    wbs[h] = jnp.concatenate([wbs[h], unseen], axis=1)
                dzbs[h] = dzb

            _by_stage(len(heads), [scores, gates, weights, score_grads])
            dk_acc[rows, :] += _dot_tn(jnp.concatenate(dzbs, axis=0), qs_all)
            dv_acc[rows, :] += _dot_tn(jnp.concatenate(wbs, axis=0), dos_all)
            return tuple(new_runs), tuple(new_e_runs), tuple(new_dqs)

        zero_cols = [jnp.zeros((tr, 1), F32)] * len(heads)
        carry = block(qi, zero_cols, zero_cols, [jnp.zeros((tr, LANES), F32)] * len(heads), True)
        _, _, dqs = _while_weights_live(qi, lambda kb, carry: block(kb, *carry, False), carry)
        dq_ref[...] = (_merge_chains(dqs) * scale).astype(BF16)

        @pl.when(qi == nq - 1)
        def _():
            dk_ref[...] = dk_acc[...].astype(BF16)
            dv_ref[...] = dv_acc[...].astype(BF16)

        _exchange_wait(ex, ex_in, ex_out, sems, (groups, nq))

    blk = pl.BlockSpec((t, LANES), lambda g, i: (i, g))
    slab = pl.BlockSpec((s, LANES), lambda g, i: (0, g))
    mask_spec = pl.BlockSpec((t, t), lambda g, i: (0, 0))
    return pl.pallas_call(
        body, name="attn_bwd", grid=(groups, nq),
        in_specs=[blk, pl.BlockSpec((s, LANES), lambda g, i: (0, groups + g)),
                  pl.BlockSpec((s, LANES), lambda g, i: (0, 2 * groups + g)), blk, blk, mask_spec, mask_spec, mask_spec]
        + [ANY] * len(ex.arrays),
        out_specs=[blk, slab, slab] + [ANY] * len(ex.out_shapes),
        out_shape=[jax.ShapeDtypeStruct((s, aw), BF16)] * 3 + ex.out_shapes,
        scratch_shapes=[pltpu.VMEM((s, LANES), F32), pltpu.VMEM((s, LANES), F32)] + _exchange_sems(ex),
        compiler_params=_params(2),
    )(qkv, qkv, qkv, o, do, *_attn_masks(t), *ex.arrays)


def _branches(o_b, conv, conv_prev, wconv, w_ao, w_co, cw, first):
    conv = conv.astype(F32)
    conv_prev = conv_prev.astype(F32)
    cb = conv[:, 0:cw]
    cm = conv[:, cw:2 * cw] * conv[:, 2 * cw:3 * cw]
    cm_prev = conv_prev[:, cw:2 * cw] * conv_prev[:, 2 * cw:3 * cw]
    cm_prev = jnp.where(first, 0.0, cm_prev)
    cv, cm1, cm2 = _conv_taps(cm, cm_prev, wconv)
    conv_in = (cb * cv).astype(BF16)
    return _dot(o_b, w_ao), _dot(conv_in, w_co), conv_in, cb, cv, cm, cm1, cm2


def _mix_fwd(x, o, conv, gate, wconv, g_post, w_ao, w_co, w_o, tm):
    s, d = x.shape
    aw, cw = w_ao.shape[0], w_co.shape[0]

    def body(x_ref, o_ref, conv_ref, prev_ref, gate_ref, wc_ref, g_ref, wao_hbm, wco_hbm, wo_hbm,
             x1_ref, mixed_ref, mixin_ref, convin_ref, wao, wco, wo, sem):
        _load_resident([(wao_hbm, wao), (wco_hbm, wco), (wo_hbm, wo)], sem)
        y_attn, y_conv, conv_in, *_ = _branches(
            o_ref[...].astype(BF16), conv_ref[...], prev_ref[...], wc_ref[...], wao[...], wco[...], cw,
            pl.program_id(0) == 0)
        mix_in = (gate_ref[:, 0:d].astype(F32) * y_attn + gate_ref[:, d:2 * d].astype(F32) * y_conv).astype(BF16)
        mixed = _dot(mix_in, wo[...])
        x1_ref[...] = x_ref[...] + mixed * _rms_scale(mixed) * g_ref[...]
        mixed_ref[...] = mixed
        mixin_ref[...] = mix_in
        convin_ref[...] = conv_in

    return pl.pallas_call(
        body, name="mix_fwd", grid=(s // tm,),
        in_specs=[_row_spec(tm, d), _row_spec(tm, aw), _row_spec(tm, 3 * cw), _prev_halo_spec(tm, 3 * cw, HALO_BF16),
                  _row_spec(tm, 2 * d), _const_spec((CONV_K, cw)), _const_spec((1, d)), ANY, ANY, ANY],
        out_specs=[_row_spec(tm, d), _row_spec(tm, d), _row_spec(tm, d), _row_spec(tm, cw)],
        out_shape=[jax.ShapeDtypeStruct((s, d), F32), jax.ShapeDtypeStruct((s, d), F32),
                   jax.ShapeDtypeStruct((s, d), BF16), jax.ShapeDtypeStruct((s, cw), BF16)],
        scratch_shapes=[pltpu.VMEM(w_ao.shape, BF16), pltpu.VMEM(w_co.shape, BF16), pltpu.VMEM(w_o.shape, BF16),
                        pltpu.SemaphoreType.DMA((3,))],
        compiler_params=_params(1),
    )(x, o, conv, conv, gate, wconv, g_post, w_ao, w_co, w_o)


def _mix_bwd(dx1, mixed, o, conv, gate, wconv, g_post, w_ao, w_co, w_o, tm):
    s, d = dx1.shape
    aw, cw = w_ao.shape[0], w_co.shape[0]
    n = s // tm
    per = tm // HALO_BF16

    def body(dx1_ref, mixed_ref, o_ref, conv_ref, prev_ref, gate_ref, wc_ref, g_ref, wao_hbm, wco_hbm, wo_hbm,
             dmixed_ref, dattn_ref, dconvout_ref, do_ref, drest_ref, dg_ref, dbias_ref, dwc_ref,
             wao, wco, wo, dcv_next, sem):
        i = pl.program_id(0)
        _load_resident([(wao_hbm, wao), (wco_hbm, wco), (wo_hbm, wo)], sem)

        @pl.when(i == 0)
        def _():
            dg_ref[...] = jnp.zeros_like(dg_ref)
            dbias_ref[...] = jnp.zeros_like(dbias_ref)
            dwc_ref[...] = jnp.zeros_like(dwc_ref)
            dcv_next[...] = jnp.zeros_like(dcv_next)

        mixed = mixed_ref[...]
        r = _rms_scale(mixed)
        mhat = mixed * r
        dn = dx1_ref[...]
        dg_ref[...] += jnp.sum(dn * mhat, axis=0, keepdims=True)
        dmixed = _rms_bwd(mhat, r, g_ref[...], dn).astype(BF16)
        dmixed_ref[...] = dmixed
        dmi = _dot_nt(dmixed, wo[...])

        wc = wc_ref[...]
        conv = conv_ref[...].astype(F32)
        y_attn, y_conv, _, cb, cv, cm, cm1, cm2 = _branches(
            o_ref[...].astype(BF16), conv, prev_ref[...], wc, wao[...], wco[...], cw, i == n - 1)
        ga = gate_ref[:, 0:d].astype(F32)
        gc = gate_ref[:, d:2 * d].astype(F32)
        dpre_a = dmi * y_attn * ga * (1.0 - ga)
        dpre_c = dmi * y_conv * gc * (1.0 - gc)
        drest_ref[:, 3 * cw:3 * cw + d] = dpre_a.astype(BF16)
        drest_ref[:, 3 * cw + d:3 * cw + 2 * d] = dpre_c.astype(BF16)
        dbias_ref[:, 0:d] += jnp.sum(dpre_a, axis=0, keepdims=True)
        dbias_ref[:, d:2 * d] += jnp.sum(dpre_c, axis=0, keepdims=True)

        dattn = (dmi * ga).astype(BF16)
        dattn_ref[...] = dattn
        do_ref[...] = _dot_nt(dattn, wao[...]).astype(BF16)
        dconvout = (dmi * gc).astype(BF16)
        dconvout_ref[...] = dconvout
        dconv_in = _dot_nt(dconvout, wco[...])
        drest_ref[:, 0:cw] = (dconv_in * cv).astype(BF16)

        dcv = dconv_in * cb
        following = dcv_next[...]
        dcm = wc[2:3, :] * dcv + wc[1:2, :] * _shift_up(dcv, following, 1) + wc[0:1, :] * _shift_up(dcv, following, 2)
        drest_ref[:, cw:2 * cw] = (dcm * conv[:, 2 * cw:3 * cw]).astype(BF16)
        drest_ref[:, 2 * cw:3 * cw] = (dcm * conv[:, cw:2 * cw]).astype(BF16)
        for tap, shifted in enumerate((cm2, cm1, cm)):
            dwc_ref[tap:tap + 1, :] += jnp.sum(dcv * shifted, axis=0, keepdims=True)
        dcv_next[...] = dcv[0:HALO, :]

    def rows(width):
        return pl.BlockSpec((tm, width), lambda i: (n - 1 - i, 0))

    prev_halo = pl.BlockSpec((HALO_BF16, 3 * cw), lambda i: (jnp.maximum((n - 1 - i) * per - 1, 0), 0))
    n_rest = 3 * cw + 2 * d
    return pl.pallas_call(
        body, name="mix_bwd", grid=(n,),
        in_specs=[rows(d), rows(d), rows(aw), rows(3 * cw), prev_halo, rows(2 * d), _const_spec((CONV_K, cw)),
                  _const_spec((1, d)), ANY, ANY, ANY],
        out_specs=[rows(d), rows(d), rows(d), rows(aw), rows(n_rest), _const_spec((1, d)), _const_spec((1, 2 * d)),
                   _const_spec((CONV_K, cw))],
        out_shape=[jax.ShapeDtypeStruct((s, d), BF16), jax.ShapeDtypeStruct((s, d), BF16),
                   jax.ShapeDtypeStruct((s, d), BF16), jax.ShapeDtypeStruct((s, aw), BF16),
                   jax.ShapeDtypeStruct((s, n_rest), BF16), jax.ShapeDtypeStruct((1, d), F32),
                   jax.ShapeDtypeStruct((1, 2 * d), F32), jax.ShapeDtypeStruct((CONV_K, cw), F32)],
        scratch_shapes=[pltpu.VMEM(w_ao.shape, BF16), pltpu.VMEM(w_co.shape, BF16), pltpu.VMEM(w_o.shape, BF16),
                        pltpu.VMEM((HALO, cw), F32), pltpu.SemaphoreType.DMA((3,))],
        compiler_params=_params(1),
    )(dx1, mixed, o, conv, conv, gate, wconv, g_post, w_ao, w_co, w_o)


def _mlp_ple_loss(x1, p, target, g_pre, g_post, g_ple, w_up, w_dn, w_pg, w_pp, tm):
    s, d = x1.shape
    ff = w_up.shape[1]
    pd = p.shape[1]
    fc = FF_CHUNK

    def body(x1_ref, p_ref, t_ref, gpre_ref, gpost_ref, gple_ref, wup_hbm, wdn_hbm, wpg_hbm, wpp_hbm,
             dx1_ref, h2_ref, du_ref, a_ref, df_ref, h3_ref, ds3_ref, dpp_ref, loss_ref, dgpre_ref, dgpost_ref,
             dgple_ref, wup, wdn, wpg, wpp, u_scr, sem):
        _load_resident([(wup_hbm, wup), (wdn_hbm, wdn), (wpg_hbm, wpg), (wpp_hbm, wpp)], sem)

        @pl.when(pl.program_id(0) == 0)
        def _():
            for ref in (loss_ref, dgpre_ref, dgpost_ref, dgple_ref):
                ref[...] = jnp.zeros_like(ref)

        x1v = x1_ref[...]
        r2 = _rms_scale(x1v)
        x1hat = x1v * r2
        h2 = (x1hat * gpre_ref[...]).astype(BF16)
        h2_ref[...] = h2
        f = jnp.zeros((tm, d), F32)
        for c0 in range(0, ff, fc):
            u = _dot(h2, wup[:, c0:c0 + fc])
            u_scr[:, c0:c0 + fc] = u
            a = jnp.square(jnp.maximum(u, 0.0)).astype(BF16)
            a_ref[:, c0:c0 + fc] = a
            f = f + _dot(a, wdn[c0:c0 + fc, :])
        rf = _rms_scale(f)
        fhat = f * rf
        x2 = x1v + fhat * gpost_ref[...]
        r3 = _rms_scale(x2)
        x2hat = x2 * r3
        h3 = (x2hat * gple_ref[...]).astype(BF16)
        h3_ref[...] = h3
        pg = _sigmoid(_dot(h3, wpg[...]))
        pp = _dot(p_ref[...].astype(BF16), wpp[...])
        diff = x2 + pg * pp - t_ref[...]
        loss_ref[...] += 0.5 * jnp.sum(jnp.mean(diff * diff, axis=-1, keepdims=True), axis=0, keepdims=True)

        dy = diff * (1.0 / d)
        dpp_ref[...] = (dy * pg).astype(BF16)
        ds3 = (dy * pp * pg * (1.0 - pg)).astype(BF16)
        ds3_ref[...] = ds3
        dh3 = _dot_nt(ds3, wpg[...])
        dgple_ref[...] += jnp.sum(dh3 * x2hat, axis=0, keepdims=True)
        dx2 = dy + _rms_bwd(x2hat, r3, gple_ref[...], dh3)
        dgpost_ref[...] += jnp.sum(dx2 * fhat, axis=0, keepdims=True)
        df = _rms_bwd(fhat, rf, gpost_ref[...], dx2).astype(BF16)
        df_ref[...] = df
        dh2 = jnp.zeros((tm, d), F32)
        for c0 in range(0, ff, fc):
            da = _dot_nt(df, wdn[c0:c0 + fc, :])
            du = (da * (2.0 * jnp.maximum(u_scr[:, c0:c0 + fc], 0.0))).astype(BF16)
            du_ref[:, c0:c0 + fc] = du
            dh2 = dh2 + _dot_nt(du, wup[:, c0:c0 + fc])
        dgpre_ref[...] += jnp.sum(dh2 * x1hat, axis=0, keepdims=True)
        dx1_ref[...] = dx2 + _rms_bwd(x1hat, r2, gpre_ref[...], dh2)

    vec = _const_spec((1, d))
    return pl.pallas_call(
        body, name="mlp_ple_loss", grid=(s // tm,),
        in_specs=[_row_spec(tm, d), _row_spec(tm, pd), _row_spec(tm, d), vec, vec, vec, ANY, ANY, ANY, ANY],
        out_specs=[_row_spec(tm, d), _row_spec(tm, d), _row_spec(tm, ff), _row_spec(tm, ff), _row_spec(tm, d),
                   _row_spec(tm, d), _row_spec(tm, d), _row_spec(tm, d), _const_spec((1, 1)), vec, vec, vec],
        out_shape=[jax.ShapeDtypeStruct((s, d), F32), jax.ShapeDtypeStruct((s, d), BF16),
                   jax.ShapeDtypeStruct((s, ff), BF16), jax.ShapeDtypeStruct((s, ff), BF16),
                   jax.ShapeDtypeStruct((s, d), BF16), jax.ShapeDtypeStruct((s, d), BF16),
                   jax.ShapeDtypeStruct((s, d), BF16), jax.ShapeDtypeStruct((s, d), BF16),
                   jax.ShapeDtypeStruct((1, 1), F32), jax.ShapeDtypeStruct((1, d), F32),
                   jax.ShapeDtypeStruct((1, d), F32), jax.ShapeDtypeStruct((1, d), F32)],
        scratch_shapes=[pltpu.VMEM(w_up.shape, BF16), pltpu.VMEM(w_dn.shape, BF16), pltpu.VMEM(w_pg.shape, BF16),
                        pltpu.VMEM(w_pp.shape, BF16), pltpu.VMEM((tm, ff), F32), pltpu.SemaphoreType.DMA((4,))],
        compiler_params=_params(1),
    )(x1, p, target, g_pre, g_post, g_ple, w_up, w_dn, w_pg, w_pp)


def _in_proj_bwd(x, dx1, pieces, g1, w_in, tm, exchange=None):
    s, d = x.shape
    ni = w_in.shape[1]
    widths = [p.shape[1] for p in pieces]
    grid = (s // tm,)
    ex = exchange or _NO_EXCHANGE

    def body(x_ref, dx1_ref, *rest):
        piece_refs, rest = rest[:len(pieces)], rest[len(pieces):]
        g_ref, w_hbm = rest[0], rest[1]
        ex_in, (dx_ref, dg_ref), ex_out, (w_vmem, sem, *sems) = _split_refs(rest[2:], ex, 2)
        _exchange_start(ex, ex_in, ex_out, sems, grid)
        _load_resident([(w_hbm, w_vmem)], sem)

        @pl.when(pl.program_id(0) == 0)
        def _():
            dg_ref[...] = jnp.zeros_like(dg_ref)

        dh = jnp.zeros((tm, d), F32)
        c0 = 0
        for ref, width in zip(piece_refs, widths):
            dh = dh + _dot_nt(ref[...], w_vmem[:, c0:c0 + width])
            c0 += width
        xv = x_ref[...]
        r = _rms_scale(xv)
        xhat = xv * r
        dg_ref[...] += jnp.sum(dh * xhat, axis=0, keepdims=True)
        dx_ref[...] = dx1_ref[...] + _rms_bwd(xhat, r, g_ref[...], dh)
        _exchange_wait(ex, ex_in, ex_out, sems, grid)

    return pl.pallas_call(
        body, name="in_proj_bwd", grid=grid,
        in_specs=[_row_spec(tm, d), _row_spec(tm, d)] + [_row_spec(tm, w) for w in widths]
        + [_const_spec((1, d)), ANY] + [ANY] * len(ex.arrays),
        out_specs=[_row_spec(tm, d), _const_spec((1, d))] + [ANY] * len(ex.out_shapes),
        out_shape=[jax.ShapeDtypeStruct((s, d), F32), jax.ShapeDtypeStruct((1, d), F32)] + ex.out_shapes,
        scratch_shapes=[pltpu.VMEM((d, ni), BF16), pltpu.SemaphoreType.DMA((1,))] + _exchange_sems(ex),
        compiler_params=_params(1),
    )(x, dx1, *pieces, g1, w_in, *ex.arrays)


def _weight_grad(a, b, name, into=None, col0=0, n_total=None):
    s, m = a.shape
    n = b.shape[1]
    tm, tk = min(m, DW_TILE), min(s, DW_TOKENS)
    tn = min(n, DW_TILE) if n_total is None else DW_PIECE_TILE
    nk = s // tk
    j0 = col0 // tn
    assert m % tm == 0 and n % tn == 0 and col0 % tn == 0

    def body(a_ref, b_ref, *rest):
        o_ref, acc = rest[-2:]
        k = pl.program_id(2)

        @pl.when(k == 0)
        def _():
            acc[...] = jnp.zeros_like(acc)

        acc[...] += _dot_tn(a_ref[...].astype(BF16), b_ref[...].astype(BF16))

        @pl.when(k == nk - 1)
        def _():
            o_ref[...] = acc[...].astype(BF16)

    extra = [] if into is None else [into]
    return pl.pallas_call(
        body, name=name, grid=(m // tm, n // tn, nk),
        in_specs=[pl.BlockSpec((tk, tm), lambda i, j, k: (k, i)), pl.BlockSpec((tk, tn), lambda i, j, k: (k, j))]
        + [ANY] * len(extra),
        out_specs=pl.BlockSpec((tm, tn), lambda i, j, k: (i, j0 + j)),
        out_shape=jax.ShapeDtypeStruct((m, n_total or n), BF16),
        input_output_aliases={2: 0} if extra else {},
        scratch_shapes=[pltpu.VMEM((tm, tn), F32)],
        compiler_params=_params(3),
    )(a, b, *extra)


def _mesh_position():
    return tuple(lax.axis_index(a) for a in MESH_AXES)


def _peer(me, k):
    bits = ((k >> 2) & 1, (k >> 1) & 1, k & 1)
    pos = tuple(1 - m if b else m for m, b in zip(me, bits))
    return pos, 4 * pos[0] + 2 * pos[1] + pos[2]


class _Exchange:
    def __init__(self, arrays, out_shapes, src, dst, relayed=None):
        self.arrays, self.out_shapes, self.src, self.dst = list(arrays), list(out_shapes), src, dst
        self.relayed = list(relayed) if relayed is not None else [False] * len(self.arrays)


_NO_EXCHANGE = _Exchange([], [], None, None)


def _exchange_sems(ex):
    n = len(ex.arrays)
    if n == 0:
        return []
    return [pltpu.SemaphoreType.DMA((n, N_DEV - 1)), pltpu.SemaphoreType.DMA((n, N_DEV - 1)),
            pltpu.SemaphoreType.DMA((n,))]


def _split_refs(rest, ex, n_own_outs):
    n_in, n_out = len(ex.arrays), len(ex.out_shapes)
    ex_in, rest = rest[:n_in], rest[n_in:]
    own, rest = rest[:n_own_outs], rest[n_own_outs:]
    return ex_in, own, rest[:n_out], rest[n_out:]


def _direct_steps(ex, w, in_refs, out_refs, sems):
    send_sems, recv_sems, local_sems = sems
    me = _mesh_position()
    mine = 4 * me[0] + 2 * me[1] + me[2]

    def copy(k):
        landing = ex.dst(w, out_refs, mine)
        if k == 0:
            return pltpu.make_async_copy(ex.src(w, in_refs, mine), landing, local_sems.at[w])
        peer, peer_idx = _peer(me, k)
        return pltpu.make_async_remote_copy(
            src_ref=ex.src(w, in_refs, peer_idx), dst_ref=landing, send_sem=send_sems.at[w, k - 1],
            recv_sem=recv_sems.at[w, k - 1], device_id=peer, device_id_type=pl.DeviceIdType.MESH)

    ks = range(N_DEV)
    return [lambda k=k: copy(k).start() for k in ks], [], [lambda k=k: copy(k).wait() for k in ks]


def _relayed_steps(ex, w, in_refs, out_refs, sems):
    send_sems, recv_sems, local_sems = sems
    x, y, c = _mesh_position()
    chips = [(1 - x, y), (x, 1 - y), (1 - x, 1 - y)]
    sibling = (x, y, 1 - c)
    js = range(len(chips))

    def block(px, py, pc):
        return ex.dst(w, out_refs, 4 * px + 2 * py + pc)

    def copy(k, dst, to, src=None):
        return pltpu.make_async_remote_copy(
            src_ref=ex.src(w, in_refs, None) if src is None else src, dst_ref=dst, send_sem=send_sems.at[w, k],
            recv_sem=recv_sems.at[w, k], device_id=to, device_id_type=pl.DeviceIdType.MESH)

    def local():
        return pltpu.make_async_copy(ex.src(w, in_refs, None), block(x, y, c), local_sems.at[w])

    def own(k):
        return copy(k, block(x, y, c), sibling if k == 0 else (*chips[k - 1], c))

    def came(j):
        return copy(1 + j, block(*chips[j], c), (*chips[j], c))

    def passed(j):
        return copy(4 + j, block(*chips[j], c), sibling, src=block(*chips[j], c))

    def from_sibling(k):
        return copy(k, block(x, y, 1 - c) if k == 0 else block(*chips[k - 4], 1 - c), sibling)

    start = [lambda: local().start()] + [lambda k=k: own(k).start() for k in range(4)]
    relay = [step for j in js for step in (lambda j=j: came(j).wait_recv(), lambda j=j: passed(j).start())]
    finish = ([lambda: local().wait()] + [lambda k=k: own(k).wait_send() for k in range(4)]
              + [lambda j=j: passed(j).wait_send() for j in js]
              + [lambda k=k: from_sibling(k).wait_recv() for k in (0, 4, 5, 6)])
    return start, relay, finish


def _exchange_steps(ex, in_refs, out_refs, sems):
    start, relay, finish = [], [], []
    for w in range(len(ex.arrays)):
        steps = (_relayed_steps if ex.relayed[w] else _direct_steps)(ex, w, in_refs, out_refs, sems)
        start += steps[0]
        relay += steps[1]
        finish += steps[2]
    return start, relay, finish


def _run(steps):
    for step in steps:
        step()


def _at_grid_step(grid, where):
    target = {"first": [0] * len(grid), "middle": [grid[0] // 2] + [0] * (len(grid) - 1),
              "last": [g - 1 for g in grid]}[where]
    hit = pl.program_id(0) == target[0]
    for axis in range(1, len(grid)):
        hit = jnp.logical_and(hit, pl.program_id(axis) == target[axis])
    return hit


def _exchange_start(ex, in_refs, out_refs, sems, grid):
    if ex.arrays:
        @pl.when(_at_grid_step(grid, "first"))
        def _():
            _run(_exchange_steps(ex, in_refs, out_refs, sems)[0])

        if any(ex.relayed):
            assert grid[0] >= 2

            @pl.when(_at_grid_step(grid, "middle"))
            def _():
                _run(_exchange_steps(ex, in_refs, out_refs, sems)[1])


def _exchange_wait(ex, in_refs, out_refs, sems, grid):
    if ex.arrays:
        @pl.when(_at_grid_step(grid, "last"))
        def _():
            _run(_exchange_steps(ex, in_refs, out_refs, sems)[2])


def _shard_block(ref, shard_shape, by_col, idx):
    r, c = shard_shape
    if by_col:
        return ref.at[:, pl.ds(pl.multiple_of(idx * c, LANES), c)]
    return ref.at[pl.ds(pl.multiple_of(idx * r, 16), r), :]


def _full_shape(shard_shape, by_col):
    r, c = shard_shape
    return (r, N_DEV * c) if by_col else (N_DEV * r, c)


def _gather_exchange(shards, col_sharded):
    shapes = [a.shape for a in shards]
    return _Exchange(
        shards, [jax.ShapeDtypeStruct(_full_shape(sh, bc), a.dtype) for a, sh, bc in zip(shards, shapes, col_sharded)],
        lambda w, refs, idx: refs[w],
        lambda w, refs, idx: _shard_block(refs[w], shapes[w], col_sharded[w], idx), [True] * len(shards))


def _scatter_exchange(grads, col_sharded):
    shapes = []
    for g, by_col in zip(grads, col_sharded):
        r, c = g.shape
        shapes.append((r, c // N_DEV) if by_col else (r // N_DEV, c))
    return _Exchange(
        grads, [jax.ShapeDtypeStruct((N_DEV,) + sh, g.dtype) for g, sh in zip(grads, shapes)],
        lambda w, refs, idx: _shard_block(refs[w], shapes[w], col_sharded[w], idx),
        lambda w, refs, mine: refs[w].at[mine])


def _broadcast_exchange(arrays):
    return _Exchange(arrays, [jax.ShapeDtypeStruct((N_DEV,) + a.shape, a.dtype) for a in arrays],
                     lambda w, refs, idx: refs[w], lambda w, refs, mine: refs[w].at[mine])


def _join(*exs):
    arrays, shapes, owner = [], [], []
    for e in exs:
        for w in range(len(e.arrays)):
            owner.append((e, w, len(arrays), len(shapes)))
        arrays += e.arrays
        shapes += e.out_shapes

    def src(w, refs, idx):
        e, w0, i0, _ = owner[w]
        return e.src(w0, refs[i0:i0 + len(e.arrays)], idx)

    def dst(w, refs, idx):
        e, w0, _, o0 = owner[w]
        return e.dst(w0, refs[o0:o0 + len(e.out_shapes)], idx)

    return _Exchange(arrays, shapes, src, dst, [flag for e in exs for flag in e.relayed])


def _exchange_call(ex, name):
    n_in = len(ex.arrays)

    def body(*refs):
        in_refs, _, out_refs, sems = _split_refs(refs, ex, 0)
        for steps in _exchange_steps(ex, in_refs, out_refs, sems):
            _run(steps)

    return pl.pallas_call(
        body, name=name, in_specs=[ANY] * n_in, out_specs=[ANY] * len(ex.out_shapes), out_shape=ex.out_shapes,
        scratch_shapes=_exchange_sems(ex), compiler_params=pltpu.CompilerParams(vmem_limit_bytes=VMEM_LIMIT),
    )(*ex.arrays)


def _to_bf16(arrays):
    def body(*refs):
        for src, dst in zip(refs[:len(arrays)], refs[len(arrays):]):
            dst[...] = src[...].astype(BF16)

    vmem = pl.BlockSpec(memory_space=pltpu.VMEM)
    return pl.pallas_call(
        body, name="weights_to_bf16", in_specs=[vmem] * len(arrays), out_specs=[vmem] * len(arrays),
        out_shape=[jax.ShapeDtypeStruct(a.shape, BF16) for a in arrays],
        compiler_params=pltpu.CompilerParams(vmem_limit_bytes=VMEM_LIMIT),
    )(*arrays)


def _adamw(w, g, m, v):
    m = ADAM_B1 * m + (1.0 - ADAM_B1) * g
    v = ADAM_B2 * v + (1.0 - ADAM_B2) * jnp.square(g)
    m_hat = m / (1.0 - ADAM_B1 ** ADAM_STEP)
    v_hat = v / (1.0 - ADAM_B2 ** ADAM_STEP)
    delta = -ADAM_LR * (m_hat / (jnp.sqrt(v_hat) + ADAM_EPS) + ADAM_WD * w)
    return delta, m, v


def _sum_and_adamw(parts, w, m, v, name):
    r, c = w.shape
    tr = min(r, 256)

    def body(p_ref, w_ref, m_ref, v_ref, g_out, d_out, m_out, v_out):
        g = p_ref[0].astype(F32)
        for dev in range(1, N_DEV):
            g = g + p_ref[dev].astype(F32)
        g_out[...] = g
        d_out[...], m_out[...], v_out[...] = _adamw(w_ref[...], g, m_ref[...], v_ref[...])

    blk = pl.BlockSpec((tr, c), lambda i: (i, 0))
    return pl.pallas_call(
        body, name=name, grid=(r // tr,),
        in_specs=[pl.BlockSpec((N_DEV, tr, c), lambda i: (0, i, 0)), blk, blk, blk],
        out_specs=[blk] * 4, out_shape=[jax.ShapeDtypeStruct((r, c), F32)] * 4,
        compiler_params=_params(1),
    )(parts, w, m, v)


BIG = ("w_in", "w_attn_out", "w_conv_out", "w_o", "w_up", "w_down", "w_ple_gate", "w_ple_proj")
COL_SHARDED = {"w_in": True, "w_attn_out": True, "w_conv_out": True, "w_o": False, "w_up": True, "w_down": False,
               "w_ple_gate": False, "w_ple_proj": True}
SMALL = ("g_pre_mix", "b_gate", "g_post_mix", "g_pre_mlp", "g_post_mlp", "g_ple")


REST = BIG[1:]


def _local_grads(x, p, target, small, wconv, full, aw, cw, tm, t, gather_rest=None, scatter_rest=None,
                 scatter_in=None):
    full = dict(full)
    qkv, conv, gate, h1 = _in_proj_fwd(x, small["g_pre_mix"], small["b_gate"], full["w_in"], aw, cw, tm)
    o, *rest = _attn_fwd(qkv, aw, t, gather_rest)
    full.update(zip(REST, rest))
    x1, mixed, mix_in, conv_in = _mix_fwd(x, o, conv, gate, wconv, small["g_post_mix"], full["w_attn_out"],
                                          full["w_conv_out"], full["w_o"], tm)
    (dx1, h2, du, a, df, h3, ds3, dpp, loss, dg_pre_mlp, dg_post_mlp, dg_ple) = _mlp_ple_loss(
        x1, p, target, small["g_pre_mlp"], small["g_post_mlp"], small["g_ple"], full["w_up"], full["w_down"],
        full["w_ple_gate"], full["w_ple_proj"], tm)
    big = {"w_up": _weight_grad(h2, du, "dw_up"), "w_down": _weight_grad(a, df, "dw_down"),
           "w_ple_gate": _weight_grad(h3, ds3, "dw_ple_gate"), "w_ple_proj": _weight_grad(p, dpp, "dw_ple_proj")}
    (dmixed, dattn, dconvout, do, drest, dg_post_mix, db_gate, dwconv) = _mix_bwd(
        dx1, mixed, o, conv, gate, wconv, small["g_post_mix"], full["w_attn_out"], full["w_conv_out"], full["w_o"],
        tm)
    big.update({"w_attn_out": _weight_grad(o, dattn, "dw_attn_out"),
                "w_conv_out": _weight_grad(conv_in, dconvout, "dw_conv_out"),
                "w_o": _weight_grad(mix_in, dmixed, "dw_o")})
    dq, dk, dv, *scattered = _attn_bwd(qkv, o, do, aw, t, scatter_rest and scatter_rest([big[n] for n in REST]))
    pieces = [dq, dk, dv, drest]
    dw_in, col0, ni = None, 0, full["w_in"].shape[1]
    for i, piece in enumerate(pieces):
        dw_in = _weight_grad(h1, piece, "dw_in_%d" % i, dw_in, col0, ni)
        col0 += piece.shape[1]
    big["w_in"] = dw_in
    dx, dg_pre_mix, *scattered_in = _in_proj_bwd(x, dx1, pieces, small["g_pre_mix"], full["w_in"], tm,
                                                scatter_in and scatter_in(dw_in))
    small_grads = {"g_pre_mix": dg_pre_mix, "b_gate": db_gate, "g_post_mix": dg_post_mix, "g_pre_mlp": dg_pre_mlp,
                   "g_post_mlp": dg_post_mlp, "g_ple": dg_ple, "w_conv": dwconv}
    return loss[0, 0], dx, big, small_grads, scattered_in + scattered


PACK_ROWS = 16


def _pack_layout(shapes, d):
    slots, at = [], 0
    for i, (r, c) in enumerate(shapes):
        assert d % c == 0
        for row in range(r):
            slots.append((i, row, at // d, at % d))
            at += c
        at = -(-at // d) * d
    assert at <= PACK_ROWS * d
    return slots


def _pack_small(groups, d):
    shapes = [a.shape for a in groups[0]]
    slots = _pack_layout(shapes, d)
    n = len(shapes)

    def body(*refs):
        ins, outs = refs[:n * len(groups)], refs[n * len(groups):]
        for g, out in enumerate(outs):
            out[...] = jnp.zeros_like(out)
            for i, row, pr, pc in slots:
                src = ins[g * n + i]
                out[pr:pr + 1, pc:pc + shapes[i][1]] = src[row:row + 1, :]

    vmem = pl.BlockSpec(memory_space=pltpu.VMEM)
    return pl.pallas_call(
        body, name="pack_small", in_specs=[vmem] * (n * len(groups)), out_specs=[vmem] * len(groups),
        out_shape=[jax.ShapeDtypeStruct((PACK_ROWS, d), F32)] * len(groups),
    )(*[a for group in groups for a in group])


def _unpack_small(pack, shapes, d):
    slots = _pack_layout(shapes, d)
    return [jnp.stack([pack[pr, pc:pc + shapes[i][1]] for j, row, pr, pc in slots if j == i])
            for i in range(len(shapes))]


def kernel(x, p, g_pre_mix, w_in, b_gate, w_conv, w_attn_out, w_conv_out, w_o, g_post_mix, g_pre_mlp, w_up, w_down, g_post_mlp, g_ple, w_ple_gate, w_ple_proj, loss_target, m_g_pre_mix, m_w_in, m_b_gate, m_w_conv, m_w_attn_out, m_w_conv_out, m_w_o, m_g_post_mix, m_g_pre_mlp, m_w_up, m_w_down, m_g_post_mlp, m_g_ple, m_w_ple_gate, m_w_ple_proj, v_g_pre_mix, v_w_in, v_b_gate, v_w_conv, v_w_attn_out, v_w_conv_out, v_w_o, v_g_post_mix, v_g_pre_mlp, v_w_up, v_w_down, v_g_post_mlp, v_g_ple, v_w_ple_gate, v_w_ple_proj):
    given = dict(locals())
    order = ["g_pre_mix", "w_in", "b_gate", "w_conv", "w_attn_out", "w_conv_out", "w_o", "g_post_mix", "g_pre_mlp",
             "w_up", "w_down", "g_post_mlp", "g_ple", "w_ple_gate", "w_ple_proj"]
    d = x.shape[-1]
    me = 4 * lax.axis_index("x") + 2 * lax.axis_index("y") + lax.axis_index("c")

    col = [COL_SHARDED[n] for n in BIG]
    shards = _to_bf16([given[n][0] for n in BIG])
    cw_shard = w_conv.shape[-1]
    conv_tile = jnp.pad(w_conv[0], ((0, HALO - CONV_K), (0, LANES - cw_shard)))
    w_in_full, conv_g = _exchange_call(
        _join(_gather_exchange(shards[:1], col[:1]), _broadcast_exchange([conv_tile])), "gather_w_in")
    wconv = jnp.concatenate([conv_g[dev, :CONV_K, :cw_shard] for dev in range(N_DEV)], axis=1)

    small = {n: given[n] for n in SMALL}
    loss, dx, big_grads, small_grads, parts = _local_grads(
        x[0], p[0, 0], loss_target[0], small, wconv, {"w_in": w_in_full}, w_attn_out.shape[1], w_conv_out.shape[1],
        ROW_BLOCK, ATTN_BLOCK,
        _gather_exchange(shards[1:], col[1:]), lambda grads: _scatter_exchange(grads, col[1:]),
        lambda grad: _scatter_exchange([grad], col[:1]))
    small_names = list(SMALL) + ["w_conv"]
    two_d = lambda a: a.reshape(-1, d) if a.shape[-1] > d else a.reshape(-1, a.shape[-1])
    full_conv = lambda a: lax.dynamic_update_slice(jnp.zeros((CONV_K, N_DEV * cw_shard), F32), a[0],
                                                   (jnp.int32(0), me * cw_shard))
    groups = [[two_d(small_grads[n]) for n in small_names] + [loss.reshape(1, 1)]]
    for pre in ("", "m_", "v_"):
        groups.append([two_d(given[pre + n]) for n in SMALL] + [full_conv(given[pre + "w_conv"]), jnp.zeros((1, 1), F32)])
    pack, *state = _pack_small(groups, d)
    packs, = _exchange_call(_broadcast_exchange([pack]), "share_small_grads")

    grads, deltas, new_m, new_v = {}, {}, {}, {}
    for n, part in zip(BIG, parts):
        grads[n], deltas[n], new_m[n], new_v[n] = (
            a[None] for a in _sum_and_adamw(part, given[n][0], given["m_" + n][0], given["v_" + n][0], "adamw_" + n))

    outs = _sum_and_adamw(packs, *state, "adamw_small")
    shapes = [a.shape for a in groups[0]]
    for res, dst in zip(outs, (grads, deltas, new_m, new_v)):
        for n, a in zip(small_names + ["loss"], _unpack_small(res, shapes, d)):
            if n == "w_conv":
                a = lax.dynamic_slice(a, (jnp.int32(0), me * cw_shard), (CONV_K, cw_shard))[None]
            dst[n] = a.reshape(given[n].shape) if n in SMALL else a
    loss = grads["loss"][0, 0]

    return (loss, dx[None], *[grads[n] for n in order], *[deltas[n] for n in order],
            *[new_m[n] for n in order], *[new_v[n] for n in order])
```

```python
import jax
import jax.numpy as jnp
from jax import lax
from jax.experimental import pallas as pl
from jax.experimental.pallas import tpu as pltpu

F32 = jnp.float32
BF16 = jnp.bfloat16
RMS_EPS = 1e-6
N_DEV = 8
MESH_AXES = ("x", "y", "c")
LANES = 128
HEAD_DIM = 64
HEADS_PER_GROUP = LANES // HEAD_DIM
CONV_K = 3
HALO = 8
HALO_BF16 = 16
VMEM_LIMIT = 56 * 1024 * 1024
EXP2_ZERO = -150.0
LOG2_E = 1.4426950408889634

ADAM_LR = 0.001
ADAM_B1 = 0.9
ADAM_B2 = 0.999
ADAM_EPS = 1e-08
ADAM_WD = 0.01
ADAM_STEP = 10

ROW_BLOCK = 256
ATTN_BLOCK = 256
ATTN_ROW_SPLITS = 2
DW_TOKENS = 2048
DW_TILE = 1024
DW_PIECE_TILE = 512
FF_CHUNK = 1024


def _dot(a, b):
    return lax.dot_general(a, b, (((1,), (0,)), ((), ())), preferred_element_type=F32)


def _dot_nt(a, b):
    return lax.dot_general(a, b, (((1,), (1,)), ((), ())), preferred_element_type=F32)


def _dot_tn(a, b):
    return lax.dot_general(a, b, (((0,), (0,)), ((), ())), preferred_element_type=F32)


def _sigmoid(z):
    return 1.0 / (1.0 + jnp.exp(-z))


def _rms_scale(x):
    return lax.rsqrt(jnp.mean(x * x, axis=-1, keepdims=True) + RMS_EPS)


def _rms_bwd(xhat, r, g, dy):
    gd = dy * g
    return r * (gd - xhat * jnp.mean(gd * xhat, axis=-1, keepdims=True))


def _params(n_axes, **kw):
    return pltpu.CompilerParams(dimension_semantics=("arbitrary",) * n_axes, vmem_limit_bytes=VMEM_LIMIT, **kw)


def _load_resident(pairs, sem):
    @pl.when(pl.program_id(0) == 0)
    def _():
        copies = [pltpu.make_async_copy(src, dst, sem.at[i]) for i, (src, dst) in enumerate(pairs)]
        for cp in copies:
            cp.start()
        for cp in copies:
            cp.wait()


def _row_spec(tm, width):
    return pl.BlockSpec((tm, width), lambda i: (i, 0))


def _prev_halo_spec(tm, width, rows):
    per = tm // rows
    return pl.BlockSpec((rows, width), lambda i: (jnp.maximum(i * per - 1, 0), 0))


def _const_spec(shape):
    return pl.BlockSpec(shape, lambda i: (0,) * len(shape))


ANY = pl.BlockSpec(memory_space=pl.ANY)


def _shift_down(cur, prev, n):
    rows = lax.broadcasted_iota(jnp.int32, cur.shape, 0)
    out = pltpu.roll(cur, n, 0)
    for j in range(n):
        out = jnp.where(rows == j, prev[prev.shape[0] - n + j:prev.shape[0] - n + j + 1, :], out)
    return out


def _shift_up(cur, nxt, n):
    tm = cur.shape[0]
    rows = lax.broadcasted_iota(jnp.int32, cur.shape, 0)
    out = pltpu.roll(cur, tm - n, 0)
    for j in range(n):
        out = jnp.where(rows == tm - n + j, nxt[j:j + 1, :], out)
    return out


def _conv_taps(cm, cm_prev, wconv):
    cm1 = _shift_down(cm, cm_prev, 1)
    cm2 = _shift_down(cm, cm_prev, 2)
    cv = wconv[2:3, :] * cm + wconv[1:2, :] * cm1 + wconv[0:1, :] * cm2
    return cv, cm1, cm2


def _in_proj_fwd(x, g1, bias, w_in, order, n_plain, tm, gathered=False):
    s, d = x.shape
    cb = w_in.shape[1] if gathered else w_in.shape[1] // N_DEV
    ni = cb * N_DEV
    n_rows = s // tm
    ex = _gather_exchange([w_in], [True]) if gathered else _NO_EXCHANGE

    def body(order_ref, x_ref, g_ref, bias_ref, w_ref, proj_ref, h_ref, *rest):
        if gathered:
            w_full, h_all, w_blk, blk_sem, *sems = rest
            local, own, came, passed, from_sibling = _relayed_copies(ex, 0, [w_ref], [w_full], sems)
        else:
            w_full, (h_all, w_blk, blk_sem) = w_ref, rest
        j = pl.program_id(0)
        i = pl.program_id(1)
        blk = order_ref[j]

        @pl.when(i == 0)
        def _():
            if gathered:
                @pl.when(j == 0)
                def _():
                    local().start()
                    for k in range(4):
                        own(k).start()
                    local().wait()

                @pl.when(j == 1)
                def _():
                    from_sibling(0).wait_recv()

                for c in range(N_CHIPS_AWAY):
                    @pl.when(j == 2 + c)
                    def _(c=c):
                        came(c).wait_recv()
                        passed(c).start()

                    @pl.when(j == 2 + N_CHIPS_AWAY + c)
                    def _(c=c):
                        from_sibling(4 + c).wait_recv()

            cp = pltpu.make_async_copy(w_full.at[:, pl.ds(pl.multiple_of(blk * cb, LANES), cb)], w_blk, blk_sem.at[0])
            cp.start()
            cp.wait()

        rows = pl.ds(pl.multiple_of(i * tm, tm), tm)

        @pl.when(j == 0)
        def _():
            xv = x_ref[...]
            h = (xv * _rms_scale(xv) * g_ref[...]).astype(BF16)
            h_all[rows, :] = h
            h_ref[...] = h

        pc = _dot(h_all[rows, :], w_blk[...])
        col = blk * cb + lax.broadcasted_iota(jnp.int32, (1, cb), 1)
        proj_ref[...] = jnp.where(col >= n_plain, _sigmoid(pc + bias_ref[...]), pc).astype(BF16)

        if gathered:
            @pl.when(jnp.logical_and(j == N_DEV - 1, i == n_rows - 1))
            def _():
                for k in range(4):
                    own(k).wait_send()
                for c in range(N_CHIPS_AWAY):
                    passed(c).wait_send()

    grid_spec = pltpu.PrefetchScalarGridSpec(
        num_scalar_prefetch=1, grid=(N_DEV, n_rows),
        in_specs=[pl.BlockSpec((tm, d), lambda j, i, o: (jnp.where(j == 0, i, 0), 0)),
                  pl.BlockSpec((1, d), lambda j, i, o: (0, 0)),
                  pl.BlockSpec((1, cb), lambda j, i, o: (0, o[j])), ANY],
        out_specs=[pl.BlockSpec((tm, cb), lambda j, i, o: (i, o[j])),
                   pl.BlockSpec((tm, d), lambda j, i, o: (jnp.where(j == 0, i, n_rows - 1), 0))]
        + [ANY] * len(ex.out_shapes),
        scratch_shapes=[pltpu.VMEM((s, d), BF16), pltpu.VMEM((d, cb), BF16), pltpu.SemaphoreType.DMA((1,))]
        + _exchange_sems(ex))
    return pl.pallas_call(
        body, name="in_proj_fwd", grid_spec=grid_spec,
        out_shape=[jax.ShapeDtypeStruct((s, ni), BF16), jax.ShapeDtypeStruct((s, d), BF16)] + ex.out_shapes,
        compiler_params=_params(2),
    )(order, x, g1, bias, w_in)


def _split_hi_lo(a):
    hi = a.astype(BF16)
    return hi, (a - hi.astype(F32)).astype(BF16)


def _log2_gates(z):
    z2 = z * LOG2_E
    nz2 = -z2
    log_keep = jnp.minimum(nz2, 0.0) - jnp.log2(1.0 + jnp.exp2(jnp.minimum(z2, nz2)))
    return log_keep + z2, log_keep


def _attn_masks(t):
    row = lax.broadcasted_iota(jnp.int32, (t, t), 0)
    col = lax.broadcasted_iota(jnp.int32, (t, t), 1)
    return (col < row).astype(F32), (row > col).astype(BF16), (row >= col).astype(BF16)


def _chains(a):
    tr = a.shape[0] // ATTN_ROW_SPLITS
    return [jnp.where(_head_lanes(h), a[r * tr:(r + 1) * tr], jnp.zeros((tr, LANES), a.dtype))
            for h in range(HEADS_PER_GROUP) for r in range(ATTN_ROW_SPLITS)]


def _merge_chains(parts):
    rows = []
    for r in range(ATTN_ROW_SPLITS):
        out = parts[r]
        for h in range(1, HEADS_PER_GROUP):
            out = jnp.where(_head_lanes(h), parts[h * ATTN_ROW_SPLITS + r], out)
        rows.append(out)
    return jnp.concatenate(rows, axis=0)


def _by_stage(n_chains, stages):
    for stage in stages:
        for c in range(n_chains):
            stage(c)


def _row_parts(a):
    tr = a.shape[0] // ATTN_ROW_SPLITS
    return [a[r * tr:(r + 1) * tr] for r in range(ATTN_ROW_SPLITS)]


def _while_weights_live(qi, block, carry):
    def cond(state):
        j, carry = state
        live = jnp.max(carry[0][0])
        for run in carry[0][1:]:
            live = jnp.maximum(live, jnp.max(run))
        return jnp.logical_and(j < qi, live >= EXP2_ZERO)

    def step(state):
        j, carry = state
        return j + 1, block(qi - 1 - j, carry)

    return lax.while_loop(cond, step, (jnp.int32(0), carry))[1]


def _head_lanes(h):
    lane = lax.broadcasted_iota(jnp.int32, (1, LANES), 1)
    return (lane >= HEAD_DIM * h) & (lane < HEAD_DIM * (h + 1))


def _attn_fwd(qkv, aw, t, exchange=None):
    s = qkv.shape[0]
    groups = aw // LANES
    nq = s // t
    scale = HEAD_DIM ** -0.5
    ex = exchange or _NO_EXCHANGE
    causal, upper, _ = _attn_masks(t)
    mask_spec = pl.BlockSpec((t, t), lambda g, i: (0, 0))

    def body(q_ref, k_ref, v_ref, causal_ref, upper_ref, *rest):
        ex_in, (o_ref,), ex_out, sems = _split_refs(rest, ex, 1)
        qi = pl.program_id(1)
        _exchange_start(ex, ex_in, ex_out, sems, (groups, nq))
        upper = upper_ref[...]
        causal = _row_parts(causal_ref[...] > 0.5) * HEADS_PER_GROUP
        qs = _chains(q_ref[...] * scale)
        heads = range(len(qs))
        tr = t // ATTN_ROW_SPLITS

        def block(kb, runs, accs, diag):
            rows = pl.ds(pl.multiple_of(kb * t, t), t)
            k = k_ref[rows, :]
            v = v_ref[rows, :]
            ncs = [(h % ATTN_ROW_SPLITS + 1) * tr if diag else t for h in heads]
            live = [{} for _ in heads]
            new_runs, new_accs = [None] * len(heads), [None] * len(heads)

            def scores(h):
                live[h]["z"] = _dot_nt(qs[h], k[0:ncs[h]])

            def gates(h):
                nc = ncs[h]
                log_b, log_keep = _log2_gates(live[h].pop("z"))
                if diag:
                    log_keep = jnp.where(causal[h][:, 0:nc], log_keep, 0.0)
                hi, lo = _split_hi_lo(log_keep)
                live[h]["log_w"] = log_b + runs[h]
                live[h]["between"] = _dot(hi, upper[0:nc, 0:nc]) + _dot(lo, upper[0:nc, 0:nc])
                new_runs[h] = runs[h] + jnp.sum(log_keep, axis=1, keepdims=True)

            def weights(h):
                nc = ncs[h]
                w = jnp.exp2(live[h].pop("log_w") + live[h].pop("between"))
                if diag:
                    w = jnp.where(causal[h][:, 0:nc], w, 0.0)
                new_accs[h] = accs[h] + _dot(w.astype(BF16), v[0:nc])

            _by_stage(len(heads), [scores, gates, weights])
            return tuple(new_runs), tuple(new_accs)

        carry = block(qi, [jnp.zeros((tr, 1), F32)] * len(heads), [jnp.zeros((tr, LANES), F32)] * len(heads), True)
        _, accs = _while_weights_live(qi, lambda kb, carry: block(kb, *carry, False), carry)
        o_ref[...] = _merge_chains(accs)
        _exchange_wait(ex, ex_in, ex_out, sems, (groups, nq))

    return pl.pallas_call(
        body, name="attn_fwd", grid=(groups, nq),
        in_specs=[pl.BlockSpec((t, LANES), lambda g, i: (i, g)),
                  pl.BlockSpec((s, LANES), lambda g, i: (0, groups + g)),
                  pl.BlockSpec((s, LANES), lambda g, i: (0, 2 * groups + g)), mask_spec, mask_spec]
        + [ANY] * len(ex.arrays),
        out_specs=[pl.BlockSpec((t, LANES), lambda g, i: (i, g))] + [ANY] * len(ex.out_shapes),
        out_shape=[jax.ShapeDtypeStruct((s, aw), F32)] + ex.out_shapes,
        scratch_shapes=_exchange_sems(ex),
        compiler_params=_params(2),
    )(qkv, qkv, qkv, causal, upper, *ex.arrays)


def _attn_bwd(qkv, o, do, aw, t, exchange=None):
    s = qkv.shape[0]
    groups = aw // LANES
    nq = s // t
    scale = HEAD_DIM ** -0.5
    ex = exchange or _NO_EXCHANGE

    def body(q_ref, k_ref, v_ref, o_ref, do_ref, causal_ref, upper_ref, lower_ref, *rest):
        ex_in, (dq_ref, dk_ref, dv_ref), ex_out, (dk_acc, dv_acc, *sems) = _split_refs(rest, ex, 3)
        qi = pl.program_id(1)
        _exchange_start(ex, ex_in, ex_out, sems, (groups, nq))

        @pl.when(qi == 0)
        def _():
            dk_acc[...] = jnp.zeros_like(dk_acc)
            dv_acc[...] = jnp.zeros_like(dv_acc)

        upper = upper_ref[...]
        lower_incl = lower_ref[...]
        causal = _row_parts(causal_ref[...] > 0.5) * HEADS_PER_GROUP
        q = q_ref[...] * scale
        do_b = do_ref[...]
        qs = _chains(q)
        dos = _chains(do_b)
        qs_all = jnp.concatenate(qs, axis=0)
        dos_all = jnp.concatenate(dos, axis=0)
        e_totals = [jnp.sum(part, axis=1, keepdims=True) for part in _chains(do_b.astype(F32) * o_ref[...])]
        heads = range(len(qs))
        tr = t // ATTN_ROW_SPLITS

        def block(kb, runs, e_runs, dqs, diag):
            rows = pl.ds(pl.multiple_of(kb * t, t), t)
            k = k_ref[rows, :]
            v = v_ref[rows, :]
            ncs = [(h % ATTN_ROW_SPLITS + 1) * tr if diag else t for h in heads]
            live = [{} for _ in heads]
            none = [None] * len(heads)
            new_runs, new_e_runs, new_dqs, dzbs, wbs = list(none), list(none), list(none), list(none), list(none)

            def scores(h):
                live[h]["z"] = _dot_nt(qs[h], k[0:ncs[h]])
                live[h]["dw"] = _dot_nt(dos[h], v[0:ncs[h]])

            def gates(h):
                nc = ncs[h]
                log_b, log_keep = _log2_gates(live[h].pop("z"))
                live[h]["beta"] = jnp.exp2(log_b)
                live[h]["keep"] = jnp.exp2(log_keep)
                if diag:
                    log_keep = jnp.where(causal[h][:, 0:nc], log_keep, 0.0)
                hi, lo = _split_hi_lo(log_keep)
                live[h]["log_w"] = log_b + runs[h]
                live[h]["between"] = _dot(hi, upper[0:nc, 0:nc]) + _dot(lo, upper[0:nc, 0:nc])
                new_runs[h] = runs[h] + jnp.sum(log_keep, axis=1, keepdims=True)

            def weights(h):
                nc = ncs[h]
                w = jnp.exp2(live[h].pop("log_w") + live[h].pop("between"))
                if diag:
                    w = jnp.where(causal[h][:, 0:nc], w, 0.0)
                wb = w.astype(BF16)
                e = live[h].pop("dw") * wb.astype(F32)
                hi, lo = _split_hi_lo(e)
                live[h]["e"] = e
                live[h]["e_suffix"] = _dot(hi, lower_incl[0:nc, 0:nc]) + _dot(lo, lower_incl[0:nc, 0:nc]) + e_runs[h]
                wbs[h] = wb

            def score_grads(h):
                nc = ncs[h]
                e_suffix = live[h].pop("e_suffix")
                dz = live[h].pop("e") * live[h].pop("keep") - (e_totals[h] - e_suffix) * live[h].pop("beta")
                if diag:
                    dz = jnp.where(causal[h][:, 0:nc], dz, 0.0)
                dzb = dz.astype(BF16)
                new_dqs[h] = dqs[h] + _dot(dzb, k[0:nc])
                new_e_runs[h] = e_suffix[:, 0:1]
                if nc < t:
                    unseen = jnp.zeros((tr, t - nc), BF16)
                    dzb = jnp.concatenate([dzb, unseen], axis=1)
                    wbs[h] = jnp.concatenate([wbs[h], unseen], axis=1)
                dzbs[h] = dzb

            _by_stage(len(heads), [scores, gates, weights, score_grads])
            dk_acc[rows, :] += _dot_tn(jnp.concatenate(dzbs, axis=0), qs_all)
            dv_acc[rows, :] += _dot_tn(jnp.concatenate(wbs, axis=0), dos_all)
            return tuple(new_runs), tuple(new_e_runs), tuple(new_dqs)

        zero_cols = [jnp.zeros((tr, 1), F32)] * len(heads)
        carry = block(qi, zero_cols, zero_cols, [jnp.zeros((tr, LANES), F32)] * len(heads), True)
        _, _, dqs = _while_weights_live(qi, lambda kb, carry: block(kb, *carry, False), carry)
        dq_ref[...] = (_merge_chains(dqs) * scale).astype(BF16)

        @pl.when(qi == nq - 1)
        def _():
            dk_ref[...] = dk_acc[...].astype(BF16)
            dv_ref[...] = dv_acc[...].astype(BF16)

        _exchange_wait(ex, ex_in, ex_out, sems, (groups, nq))

    blk = pl.BlockSpec((t, LANES), lambda g, i: (i, g))
    slab = pl.BlockSpec((s, LANES), lambda g, i: (0, g))
    mask_spec = pl.BlockSpec((t, t), lambda g, i: (0, 0))
    return pl.pallas_call(
        body, name="attn_bwd", grid=(groups, nq),
        in_specs=[blk, pl.BlockSpec((s, LANES), lambda g, i: (0, groups + g)),
                  pl.BlockSpec((s, LANES), lambda g, i: (0, 2 * groups + g)), blk, blk, mask_spec, mask_spec, mask_spec]
        + [ANY] * len(ex.arrays),
        out_specs=[blk, slab, slab] + [ANY] * len(ex.out_shapes),
        out_shape=[jax.ShapeDtypeStruct((s, aw), BF16)] * 3 + ex.out_shapes,
        scratch_shapes=[pltpu.VMEM((s, LANES), F32), pltpu.VMEM((s, LANES), F32)] + _exchange_sems(ex),
        compiler_params=_params(2),
    )(qkv, qkv, qkv, o, do, *_attn_masks(t), *ex.arrays)


def _branches(o_b, conv, conv_prev, wconv, w_ao, w_co, cw, first):
    conv = conv.astype(F32)
    conv_prev = conv_prev.astype(F32)
    cb = conv[:, 0:cw]
    cm = conv[:, cw:2 * cw] * conv[:, 2 * cw:3 * cw]
    cm_prev = conv_prev[:, cw:2 * cw] * conv_prev[:, 2 * cw:3 * cw]
    cm_prev = jnp.where(first, 0.0, cm_prev)
    cv, cm1, cm2 = _conv_taps(cm, cm_prev, wconv)
    conv_in = (cb * cv).astype(BF16)
    return _dot(o_b, w_ao), _dot(conv_in, w_co), conv_in, cb, cv, cm, cm1, cm2


def _proj_specs(tm, aw, cw, d, row):
    assert (3 * aw) % (3 * cw) == 0 and (3 * aw + 3 * cw) % d == 0
    conv_col, gate_col, per = 3 * aw // (3 * cw), (3 * aw + 3 * cw) // d, tm // HALO_BF16
    return [pl.BlockSpec((tm, 3 * cw), lambda i: (row(i), conv_col)),
            pl.BlockSpec((HALO_BF16, 3 * cw), lambda i: (jnp.maximum(row(i) * per - 1, 0), conv_col)),
            pl.BlockSpec((tm, d), lambda i: (row(i), gate_col)), pl.BlockSpec((tm, d), lambda i: (row(i), gate_col + 1))]


def _mix_fwd(x, o, proj, wconv, g_post, w_ao, w_co, w_o, tm):
    s, d = x.shape
    aw, cw = w_ao.shape[0], w_co.shape[0]

    def body(x_ref, o_ref, conv_ref, prev_ref, ga_ref, gc_ref, wc_ref, g_ref, wao_hbm, wco_hbm, wo_hbm,
             x1_ref, mixed_ref, mixin_ref, convin_ref, wao, wco, wo, sem):
        _load_resident([(wao_hbm, wao), (wco_hbm, wco), (wo_hbm, wo)], sem)
        y_attn, y_conv, conv_in, *_ = _branches(
            o_ref[...].astype(BF16), conv_ref[...], prev_ref[...], wc_ref[...], wao[...], wco[...], cw,
            pl.program_id(0) == 0)
        mix_in = (ga_ref[...].astype(F32) * y_attn + gc_ref[...].astype(F32) * y_conv).astype(BF16)
        mixed = _dot(mix_in, wo[...])
        x1_ref[...] = x_ref[...] + mixed * _rms_scale(mixed) * g_ref[...]
        mixed_ref[...] = mixed
        mixin_ref[...] = mix_in
        convin_ref[...] = conv_in

    return pl.pallas_call(
        body, name="mix_fwd", grid=(s // tm,),
        in_specs=[_row_spec(tm, d), _row_spec(tm, aw)] + _proj_specs(tm, aw, cw, d, lambda i: i)
        + [_const_spec((CONV_K, cw)), _const_spec((1, d)), ANY, ANY, ANY],
        out_specs=[_row_spec(tm, d), _row_spec(tm, d), _row_spec(tm, d), _row_spec(tm, cw)],
        out_shape=[jax.ShapeDtypeStruct((s, d), F32), jax.ShapeDtypeStruct((s, d), F32),
                   jax.ShapeDtypeStruct((s, d), BF16), jax.ShapeDtypeStruct((s, cw), BF16)],
        scratch_shapes=[pltpu.VMEM(w_ao.shape, BF16), pltpu.VMEM(w_co.shape, BF16), pltpu.VMEM(w_o.shape, BF16),
                        pltpu.SemaphoreType.DMA((3,))],
        compiler_params=_params(1),
    )(x, o, proj, proj, proj, proj, wconv, g_post, w_ao, w_co, w_o)


def _mix_bwd(dx1, mixed, o, proj, wconv, g_post, w_ao, w_co, w_o, tm):
    s, d = dx1.shape
    aw, cw = w_ao.shape[0], w_co.shape[0]
    n = s // tm

    def body(dx1_ref, mixed_ref, o_ref, conv_ref, prev_ref, ga_ref, gc_ref, wc_ref, g_ref, wao_hbm, wco_hbm, wo_hbm,
             dmixed_ref, dattn_ref, dconvout_ref, do_ref, drest_ref, dg_ref, dbias_ref, dwc_ref,
             wao, wco, wo, dcv_next, sem):
        i = pl.program_id(0)
        _load_resident([(wao_hbm, wao), (wco_hbm, wco), (wo_hbm, wo)], sem)

        @pl.when(i == 0)
        def _():
            dg_ref[...] = jnp.zeros_like(dg_ref)
            dbias_ref[...] = jnp.zeros_like(dbias_ref)
            dwc_ref[...] = jnp.zeros_like(dwc_ref)
            dcv_next[...] = jnp.zeros_like(dcv_next)

        mixed = mixed_ref[...]
        r = _rms_scale(mixed)
        mhat = mixed * r
        dn = dx1_ref[...]
        dg_ref[...] += jnp.sum(dn * mhat, axis=0, keepdims=True)
        dmixed = _rms_bwd(mhat, r, g_ref[...], dn).astype(BF16)
        dmixed_ref[...] = dmixed
        dmi = _dot_nt(dmixed, wo[...])

        wc = wc_ref[...]
        conv = conv_ref[...].astype(F32)
        y_attn, y_conv, _, cb, cv, cm, cm1, cm2 = _branches(
            o_ref[...].astype(BF16), conv, prev_ref[...], wc, wao[...], wco[...], cw, i == n - 1)
        ga = ga_ref[...].astype(F32)
        gc = gc_ref[...].astype(F32)
        dpre_a = dmi * y_attn * ga * (1.0 - ga)
        dpre_c = dmi * y_conv * gc * (1.0 - gc)
        drest_ref[:, 3 * cw:3 * cw + d] = dpre_a.astype(BF16)
        drest_ref[:, 3 * cw + d:3 * cw + 2 * d] = dpre_c.astype(BF16)
        dbias_ref[:, 0:d] += jnp.sum(dpre_a, axis=0, keepdims=True)
        dbias_ref[:, d:2 * d] += jnp.sum(dpre_c, axis=0, keepdims=True)

        dattn = (dmi * ga).astype(BF16)
        dattn_ref[...] = dattn
        do_ref[...] = _dot_nt(dattn, wao[...]).astype(BF16)
        dconvout = (dmi * gc).astype(BF16)
        dconvout_ref[...] = dconvout
        dconv_in = _dot_nt(dconvout, wco[...])
        drest_ref[:, 0:cw] = (dconv_in * cv).astype(BF16)

        dcv = dconv_in * cb
        following = dcv_next[...]
        dcm = wc[2:3, :] * dcv + wc[1:2, :] * _shift_up(dcv, following, 1) + wc[0:1, :] * _shift_up(dcv, following, 2)
        drest_ref[:, cw:2 * cw] = (dcm * conv[:, 2 * cw:3 * cw]).astype(BF16)
        drest_ref[:, 2 * cw:3 * cw] = (dcm * conv[:, cw:2 * cw]).astype(BF16)
        for tap, shifted in enumerate((cm2, cm1, cm)):
            dwc_ref[tap:tap + 1, :] += jnp.sum(dcv * shifted, axis=0, keepdims=True)
        dcv_next[...] = dcv[0:HALO, :]

    def rows(width):
        return pl.BlockSpec((tm, width), lambda i: (n - 1 - i, 0))

    n_rest = 3 * cw + 2 * d
    return pl.pallas_call(
        body, name="mix_bwd", grid=(n,),
        in_specs=[rows(d), rows(d), rows(aw)] + _proj_specs(tm, aw, cw, d, lambda i: n - 1 - i)
        + [_const_spec((CONV_K, cw)), _const_spec((1, d)), ANY, ANY, ANY],
        out_specs=[rows(d), rows(d), rows(d), rows(aw), rows(n_rest), _const_spec((1, d)), _const_spec((1, 2 * d)),
                   _const_spec((CONV_K, cw))],
        out_shape=[jax.ShapeDtypeStruct((s, d), BF16), jax.ShapeDtypeStruct((s, d), BF16),
                   jax.ShapeDtypeStruct((s, d), BF16), jax.ShapeDtypeStruct((s, aw), BF16),
                   jax.ShapeDtypeStruct((s, n_rest), BF16), jax.ShapeDtypeStruct((1, d), F32),
                   jax.ShapeDtypeStruct((1, 2 * d), F32), jax.ShapeDtypeStruct((CONV_K, cw), F32)],
        scratch_shapes=[pltpu.VMEM(w_ao.shape, BF16), pltpu.VMEM(w_co.shape, BF16), pltpu.VMEM(w_o.shape, BF16),
                        pltpu.VMEM((HALO, cw), F32), pltpu.SemaphoreType.DMA((3,))],
        compiler_params=_params(1),
    )(dx1, mixed, o, proj, proj, proj, proj, wconv, g_post, w_ao, w_co, w_o)


def _mlp_ple_loss(x1, p, target, g_pre, g_post, g_ple, w_up, w_dn, w_pg, w_pp, tm):
    s, d = x1.shape
    ff = w_up.shape[1]
    pd = p.shape[1]
    fc = FF_CHUNK

    def body(x1_ref, p_ref, t_ref, gpre_ref, gpost_ref, gple_ref, wup_hbm, wdn_hbm, wpg_hbm, wpp_hbm,
             dx1_ref, h2_ref, du_ref, a_ref, df_ref, h3_ref, ds3_ref, dpp_ref, loss_ref, dgpre_ref, dgpost_ref,
             dgple_ref, wup, wdn, wpg, wpp, u_scr, sem):
        _load_resident([(wup_hbm, wup), (wdn_hbm, wdn), (wpg_hbm, wpg), (wpp_hbm, wpp)], sem)

        @pl.when(pl.program_id(0) == 0)
        def _():
            for ref in (loss_ref, dgpre_ref, dgpost_ref, dgple_ref):
                ref[...] = jnp.zeros_like(ref)

        x1v = x1_ref[...]
        r2 = _rms_scale(x1v)
        x1hat = x1v * r2
        h2 = (x1hat * gpre_ref[...]).astype(BF16)
        h2_ref[...] = h2
        f = jnp.zeros((tm, d), F32)
        for c0 in range(0, ff, fc):
            u = _dot(h2, wup[:, c0:c0 + fc])
            u_scr[:, c0:c0 + fc] = u
            a = jnp.square(jnp.maximum(u, 0.0)).astype(BF16)
            a_ref[:, c0:c0 + fc] = a
            f = f + _dot(a, wdn[c0:c0 + fc, :])
        rf = _rms_scale(f)
        fhat = f * rf
        x2 = x1v + fhat * gpost_ref[...]
        r3 = _rms_scale(x2)
        x2hat = x2 * r3
        h3 = (x2hat * gple_ref[...]).astype(BF16)
        h3_ref[...] = h3
        pg = _sigmoid(_dot(h3, wpg[...]))
        pp = _dot(p_ref[...].astype(BF16), wpp[...])
        diff = x2 + pg * pp - t_ref[...]
        loss_ref[...] += 0.5 * jnp.sum(jnp.mean(diff * diff, axis=-1, keepdims=True), axis=0, keepdims=True)

        dy = diff * (1.0 / d)
        dpp_ref[...] = (dy * pg).astype(BF16)
        ds3 = (dy * pp * pg * (1.0 - pg)).astype(BF16)
        ds3_ref[...] = ds3
        dh3 = _dot_nt(ds3, wpg[...])
        dgple_ref[...] += jnp.sum(dh3 * x2hat, axis=0, keepdims=True)
        dx2 = dy + _rms_bwd(x2hat, r3, gple_ref[...], dh3)
        dgpost_ref[...] += jnp.sum(dx2 * fhat, axis=0, keepdims=True)
        df = _rms_bwd(fhat, rf, gpost_ref[...], dx2).astype(BF16)
        df_ref[...] = df
        dh2 = jnp.zeros((tm, d), F32)
        for c0 in range(0, ff, fc):
            da = _dot_nt(df, wdn[c0:c0 + fc, :])
            du = (da * (2.0 * jnp.maximum(u_scr[:, c0:c0 + fc], 0.0))).astype(BF16)
            du_ref[:, c0:c0 + fc] = du
            dh2 = dh2 + _dot_nt(du, wup[:, c0:c0 + fc])
        dgpre_ref[...] += jnp.sum(dh2 * x1hat, axis=0, keepdims=True)
        dx1_ref[...] = dx2 + _rms_bwd(x1hat, r2, gpre_ref[...], dh2)

    vec = _const_spec((1, d))
    return pl.pallas_call(
        body, name="mlp_ple_loss", grid=(s // tm,),
        in_specs=[_row_spec(tm, d), _row_spec(tm, pd), _row_spec(tm, d), vec, vec, vec, ANY, ANY, ANY, ANY],
        out_specs=[_row_spec(tm, d), _row_spec(tm, d), _row_spec(tm, ff), _row_spec(tm, ff), _row_spec(tm, d),
                   _row_spec(tm, d), _row_spec(tm, d), _row_spec(tm, d), _const_spec((1, 1)), vec, vec, vec],
        out_shape=[jax.ShapeDtypeStruct((s, d), F32), jax.ShapeDtypeStruct((s, d), BF16),
                   jax.ShapeDtypeStruct((s, ff), BF16), jax.ShapeDtypeStruct((s, ff), BF16),
                   jax.ShapeDtypeStruct((s, d), BF16), jax.ShapeDtypeStruct((s, d), BF16),
                   jax.ShapeDtypeStruct((s, d), BF16), jax.ShapeDtypeStruct((s, d), BF16),
                   jax.ShapeDtypeStruct((1, 1), F32), jax.ShapeDtypeStruct((1, d), F32),
                   jax.ShapeDtypeStruct((1, d), F32), jax.ShapeDtypeStruct((1, d), F32)],
        scratch_shapes=[pltpu.VMEM(w_up.shape, BF16), pltpu.VMEM(w_dn.shape, BF16), pltpu.VMEM(w_pg.shape, BF16),
                        pltpu.VMEM(w_pp.shape, BF16), pltpu.VMEM((tm, ff), F32), pltpu.SemaphoreType.DMA((4,))],
        compiler_params=_params(1),
    )(x1, p, target, g_pre, g_post, g_ple, w_up, w_dn, w_pg, w_pp)


def _in_proj_bwd(x, dx1, pieces, g1, w_in, tm, exchange=None):
    s, d = x.shape
    ni = w_in.shape[1]
    widths = [p.shape[1] for p in pieces]
    grid = (s // tm,)
    ex = exchange or _NO_EXCHANGE

    def body(x_ref, dx1_ref, *rest):
        piece_refs, rest = rest[:len(pieces)], rest[len(pieces):]
        g_ref, w_hbm = rest[0], rest[1]
        ex_in, (dx_ref, dg_ref), ex_out, (w_vmem, sem, *sems) = _split_refs(rest[2:], ex, 2)
        _exchange_start(ex, ex_in, ex_out, sems, grid)
        _load_resident([(w_hbm, w_vmem)], sem)

        @pl.when(pl.program_id(0) == 0)
        def _():
            dg_ref[...] = jnp.zeros_like(dg_ref)

        dh = jnp.zeros((tm, d), F32)
        c0 = 0
        for ref, width in zip(piece_refs, widths):
            dh = dh + _dot_nt(ref[...], w_vmem[:, c0:c0 + width])
            c0 += width
        xv = x_ref[...]
        r = _rms_scale(xv)
        xhat = xv * r
        dg_ref[...] += jnp.sum(dh * xhat, axis=0, keepdims=True)
        dx_ref[...] = dx1_ref[...] + _rms_bwd(xhat, r, g_ref[...], dh)
        _exchange_wait(ex, ex_in, ex_out, sems, grid)

    return pl.pallas_call(
        body, name="in_proj_bwd", grid=grid,
        in_specs=[_row_spec(tm, d), _row_spec(tm, d)] + [_row_spec(tm, w) for w in widths]
        + [_const_spec((1, d)), ANY] + [ANY] * len(ex.arrays),
        out_specs=[_row_spec(tm, d), _const_spec((1, d))] + [ANY] * len(ex.out_shapes),
        out_shape=[jax.ShapeDtypeStruct((s, d), F32), jax.ShapeDtypeStruct((1, d), F32)] + ex.out_shapes,
        scratch_shapes=[pltpu.VMEM((d, ni), BF16), pltpu.SemaphoreType.DMA((1,))] + _exchange_sems(ex),
        compiler_params=_params(1),
    )(x, dx1, *pieces, g1, w_in, *ex.arrays)


def _weight_grad(a, b, name, into=None, col0=0, n_total=None):
    s, m = a.shape
    n = b.shape[1]
    tm, tk = min(m, DW_TILE), min(s, DW_TOKENS)
    tn = min(n, DW_TILE) if n_total is None else DW_PIECE_TILE
    nk = s // tk
    j0 = col0 // tn
    assert m % tm == 0 and n % tn == 0 and col0 % tn == 0

    def body(a_ref, b_ref, *rest):
        o_ref, acc = rest[-2:]
        k = pl.program_id(2)

        @pl.when(k == 0)
        def _():
            acc[...] = jnp.zeros_like(acc)

        acc[...] += _dot_tn(a_ref[...].astype(BF16), b_ref[...].astype(BF16))

        @pl.when(k == nk - 1)
        def _():
            o_ref[...] = acc[...].astype(BF16)

    extra = [] if into is None else [into]
    return pl.pallas_call(
        body, name=name, grid=(m // tm, n // tn, nk),
        in_specs=[pl.BlockSpec((tk, tm), lambda i, j, k: (k, i)), pl.BlockSpec((tk, tn), lambda i, j, k: (k, j))]
        + [ANY] * len(extra),
        out_specs=pl.BlockSpec((tm, tn), lambda i, j, k: (i, j0 + j)),
        out_shape=jax.ShapeDtypeStruct((m, n_total or n), BF16),
        input_output_aliases={2: 0} if extra else {},
        scratch_shapes=[pltpu.VMEM((tm, tn), F32)],
        compiler_params=_params(3),
    )(a, b, *extra)


def _mesh_position():
    return tuple(lax.axis_index(a) for a in MESH_AXES)


def _peer(me, k):
    bits = ((k >> 2) & 1, (k >> 1) & 1, k & 1)
    pos = tuple(1 - m if b else m for m, b in zip(me, bits))
    return pos, 4 * pos[0] + 2 * pos[1] + pos[2]


class _Exchange:
    def __init__(self, arrays, out_shapes, src, dst, relayed=None):
        self.arrays, self.out_shapes, self.src, self.dst = list(arrays), list(out_shapes), src, dst
        self.relayed = list(relayed) if relayed is not None else [False] * len(self.arrays)


_NO_EXCHANGE = _Exchange([], [], None, None)


def _exchange_sems(ex):
    n = len(ex.arrays)
    if n == 0:
        return []
    return [pltpu.SemaphoreType.DMA((n, N_DEV - 1)), pltpu.SemaphoreType.DMA((n, N_DEV - 1)),
            pltpu.SemaphoreType.DMA((n,))]


def _split_refs(rest, ex, n_own_outs):
    n_in, n_out = len(ex.arrays), len(ex.out_shapes)
    ex_in, rest = rest[:n_in], rest[n_in:]
    own, rest = rest[:n_own_outs], rest[n_own_outs:]
    return ex_in, own, rest[:n_out], rest[n_out:]


def _direct_steps(ex, w, in_refs, out_refs, sems):
    send_sems, recv_sems, local_sems = sems
    me = _mesh_position()
    mine = 4 * me[0] + 2 * me[1] + me[2]

    def copy(k):
        landing = ex.dst(w, out_refs, mine)
        if k == 0:
            return pltpu.make_async_copy(ex.src(w, in_refs, mine), landing, local_sems.at[w])
        peer, peer_idx = _peer(me, k)
        return pltpu.make_async_remote_copy(
            src_ref=ex.src(w, in_refs, peer_idx), dst_ref=landing, send_sem=send_sems.at[w, k - 1],
            recv_sem=recv_sems.at[w, k - 1], device_id=peer, device_id_type=pl.DeviceIdType.MESH)

    ks = range(N_DEV)
    return [lambda k=k: copy(k).start() for k in ks], [], [lambda k=k: copy(k).wait() for k in ks]


def _relayed_copies(ex, w, in_refs, out_refs, sems):
    send_sems, recv_sems, local_sems = sems
    x, y, c = _mesh_position()
    chips = [(1 - x, y), (x, 1 - y), (1 - x, 1 - y)]
    sibling = (x, y, 1 - c)

    def block(px, py, pc):
        return ex.dst(w, out_refs, 4 * px + 2 * py + pc)

    def copy(k, dst, to, src=None):
        return pltpu.make_async_remote_copy(
            src_ref=ex.src(w, in_refs, None) if src is None else src, dst_ref=dst, send_sem=send_sems.at[w, k],
            recv_sem=recv_sems.at[w, k], device_id=to, device_id_type=pl.DeviceIdType.MESH)

    def local():
        return pltpu.make_async_copy(ex.src(w, in_refs, None), block(x, y, c), local_sems.at[w])

    def own(k):
        return copy(k, block(x, y, c), sibling if k == 0 else (*chips[k - 1], c))

    def came(j):
        return copy(1 + j, block(*chips[j], c), (*chips[j], c))

    def passed(j):
        return copy(4 + j, block(*chips[j], c), sibling, src=block(*chips[j], c))

    def from_sibling(k):
        return copy(k, block(x, y, 1 - c) if k == 0 else block(*chips[k - 4], 1 - c), sibling)

    return local, own, came, passed, from_sibling


N_CHIPS_AWAY = 3


def _relayed_steps(ex, w, in_refs, out_refs, sems):
    local, own, came, passed, from_sibling = _relayed_copies(ex, w, in_refs, out_refs, sems)
    js = range(N_CHIPS_AWAY)
    start = [lambda: local().start()] + [lambda k=k: own(k).start() for k in range(4)]
    relay = [step for j in js for step in (lambda j=j: came(j).wait_recv(), lambda j=j: passed(j).start())]
    finish = ([lambda: local().wait()] + [lambda k=k: own(k).wait_send() for k in range(4)]
              + [lambda j=j: passed(j).wait_send() for j in js]
              + [lambda k=k: from_sibling(k).wait_recv() for k in (0, 4, 5, 6)])
    return start, relay, finish


def _arrival_order():
    x, y, c = _mesh_position()
    chips = [(1 - x, y), (x, 1 - y), (1 - x, 1 - y)]
    order = [(x, y, c), (x, y, 1 - c)] + [(*chip, c) for chip in chips] + [(*chip, 1 - c) for chip in chips]
    return jnp.stack([4 * px + 2 * py + pc for px, py, pc in order]).astype(jnp.int32)


def _exchange_steps(ex, in_refs, out_refs, sems):
    start, relay, finish = [], [], []
    for w in range(len(ex.arrays)):
        steps = (_relayed_steps if ex.relayed[w] else _direct_steps)(ex, w, in_refs, out_refs, sems)
        start += steps[0]
        relay += steps[1]
        finish += steps[2]
    return start, relay, finish


def _run(steps):
    for step in steps:
        step()


def _at_grid_step(grid, where):
    target = {"first": [0] * len(grid), "middle": [grid[0] // 2] + [0] * (len(grid) - 1),
              "last": [g - 1 for g in grid]}[where]
    hit = pl.program_id(0) == target[0]
    for axis in range(1, len(grid)):
        hit = jnp.logical_and(hit, pl.program_id(axis) == target[axis])
    return hit


def _exchange_start(ex, in_refs, out_refs, sems, grid):
    if ex.arrays:
        @pl.when(_at_grid_step(grid, "first"))
        def _():
            _run(_exchange_steps(ex, in_refs, out_refs, sems)[0])

        if any(ex.relayed):
            assert grid[0] >= 2

            @pl.when(_at_grid_step(grid, "middle"))
            def _():
                _run(_exchange_steps(ex, in_refs, out_refs, sems)[1])


def _exchange_wait(ex, in_refs, out_refs, sems, grid):
    if ex.arrays:
        @pl.when(_at_grid_step(grid, "last"))
        def _():
            _run(_exchange_steps(ex, in_refs, out_refs, sems)[2])


def _shard_block(ref, shard_shape, by_col, idx):
    r, c = shard_shape
    if by_col:
        return ref.at[:, pl.ds(pl.multiple_of(idx * c, LANES), c)]
    return ref.at[pl.ds(pl.multiple_of(idx * r, 16), r), :]


def _full_shape(shard_shape, by_col):
    r, c = shard_shape
    return (r, N_DEV * c) if by_col else (N_DEV * r, c)


def _gather_exchange(shards, col_sharded):
    shapes = [a.shape for a in shards]
    return _Exchange(
        shards, [jax.ShapeDtypeStruct(_full_shape(sh, bc), a.dtype) for a, sh, bc in zip(shards, shapes, col_sharded)],
        lambda w, refs, idx: refs[w],
        lambda w, refs, idx: _shard_block(refs[w], shapes[w], col_sharded[w], idx), [True] * len(shards))


def _scatter_exchange(grads, col_sharded):
    shapes = []
    for g, by_col in zip(grads, col_sharded):
        r, c = g.shape
        shapes.append((r, c // N_DEV) if by_col else (r // N_DEV, c))
    return _Exchange(
        grads, [jax.ShapeDtypeStruct((N_DEV,) + sh, g.dtype) for g, sh in zip(grads, shapes)],
        lambda w, refs, idx: _shard_block(refs[w], shapes[w], col_sharded[w], idx),
        lambda w, refs, mine: refs[w].at[mine])


def _broadcast_exchange(arrays):
    return _Exchange(arrays, [jax.ShapeDtypeStruct((N_DEV,) + a.shape, a.dtype) for a in arrays],
                     lambda w, refs, idx: refs[w], lambda w, refs, mine: refs[w].at[mine])


def _join(*exs):
    arrays, shapes, owner = [], [], []
    for e in exs:
        for w in range(len(e.arrays)):
            owner.append((e, w, len(arrays), len(shapes)))
        arrays += e.arrays
        shapes += e.out_shapes

    def src(w, refs, idx):
        e, w0, i0, _ = owner[w]
        return e.src(w0, refs[i0:i0 + len(e.arrays)], idx)

    def dst(w, refs, idx):
        e, w0, _, o0 = owner[w]
        return e.dst(w0, refs[o0:o0 + len(e.out_shapes)], idx)

    return _Exchange(arrays, shapes, src, dst, [flag for e in exs for flag in e.relayed])


def _exchange_call(ex, name):
    n_in = len(ex.arrays)

    def body(*refs):
        in_refs, _, out_refs, sems = _split_refs(refs, ex, 0)
        for steps in _exchange_steps(ex, in_refs, out_refs, sems):
            _run(steps)

    return pl.pallas_call(
        body, name=name, in_specs=[ANY] * n_in, out_specs=[ANY] * len(ex.out_shapes), out_shape=ex.out_shapes,
        scratch_shapes=_exchange_sems(ex), compiler_params=pltpu.CompilerParams(vmem_limit_bytes=VMEM_LIMIT),
    )(*ex.arrays)


def _to_bf16(arrays):
    def body(*refs):
        for src, dst in zip(refs[:len(arrays)], refs[len(arrays):]):
            dst[...] = src[...].astype(BF16)

    vmem = pl.BlockSpec(memory_space=pltpu.VMEM)
    return pl.pallas_call(
        body, name="weights_to_bf16", in_specs=[vmem] * len(arrays), out_specs=[vmem] * len(arrays),
        out_shape=[jax.ShapeDtypeStruct(a.shape, BF16) for a in arrays],
        compiler_params=pltpu.CompilerParams(vmem_limit_bytes=VMEM_LIMIT),
    )(*arrays)


def _adamw(w, g, m, v):
    m = ADAM_B1 * m + (1.0 - ADAM_B1) * g
    v = ADAM_B2 * v + (1.0 - ADAM_B2) * jnp.square(g)
    m_hat = m / (1.0 - ADAM_B1 ** ADAM_STEP)
    v_hat = v / (1.0 - ADAM_B2 ** ADAM_STEP)
    delta = -ADAM_LR * (m_hat / (jnp.sqrt(v_hat) + ADAM_EPS) + ADAM_WD * w)
    return delta, m, v


def _sum_and_adamw(parts, w, m, v, name):
    r, c = w.shape
    tr = min(r, 256)

    def body(p_ref, w_ref, m_ref, v_ref, g_out, d_out, m_out, v_out):
        g = p_ref[0].astype(F32)
        for dev in range(1, N_DEV):
            g = g + p_ref[dev].astype(F32)
        g_out[...] = g
        d_out[...], m_out[...], v_out[...] = _adamw(w_ref[...], g, m_ref[...], v_ref[...])

    blk = pl.BlockSpec((tr, c), lambda i: (i, 0))
    return pl.pallas_call(
        body, name=name, grid=(r // tr,),
        in_specs=[pl.BlockSpec((N_DEV, tr, c), lambda i: (0, i, 0)), blk, blk, blk],
        out_specs=[blk] * 4, out_shape=[jax.ShapeDtypeStruct((r, c), F32)] * 4,
        compiler_params=_params(1),
    )(parts, w, m, v)


BIG = ("w_in", "w_attn_out", "w_conv_out", "w_o", "w_up", "w_down", "w_ple_gate", "w_ple_proj")
COL_SHARDED = {"w_in": True, "w_attn_out": True, "w_conv_out": True, "w_o": False, "w_up": True, "w_down": False,
               "w_ple_gate": False, "w_ple_proj": True}
SMALL = ("g_pre_mix", "b_gate", "g_post_mix", "g_pre_mlp", "g_post_mlp", "g_ple")


REST = BIG[1:]


def _local_grads(x, p, target, small, wconv, full, aw, cw, tm, t, w_in_shard=None, gather_rest=None,
                 scatter_rest=None, scatter_in=None):
    full = dict(full)
    d = x.shape[1]
    n_plain = 3 * aw + 3 * cw
    bias = jnp.pad(small["b_gate"], ((0, 0), (n_plain, 0)))
    if w_in_shard is None:
        proj, h1 = _in_proj_fwd(x, small["g_pre_mix"], bias, full["w_in"], jnp.arange(N_DEV, dtype=jnp.int32),
                                n_plain, tm)
    else:
        proj, h1, full["w_in"] = _in_proj_fwd(x, small["g_pre_mix"], bias, w_in_shard, _arrival_order(), n_plain, tm,
                                              gathered=True)
    o, *rest = _attn_fwd(proj, aw, t, gather_rest)
    if gather_rest is not None:
        wconv = wconv(rest.pop())
    full.update(zip(REST, rest))
    x1, mixed, mix_in, conv_in = _mix_fwd(x, o, proj, wconv, small["g_post_mix"], full["w_attn_out"],
                                          full["w_conv_out"], full["w_o"], tm)
    (dx1, h2, du, a, df, h3, ds3, dpp, loss, dg_pre_mlp, dg_post_mlp, dg_ple) = _mlp_ple_loss(
        x1, p, target, small["g_pre_mlp"], small["g_post_mlp"], small["g_ple"], full["w_up"], full["w_down"],
        full["w_ple_gate"], full["w_ple_proj"], tm)
    big = {"w_up": _weight_grad(h2, du, "dw_up"), "w_down": _weight_grad(a, df, "dw_down"),
           "w_ple_gate": _weight_grad(h3, ds3, "dw_ple_gate"), "w_ple_proj": _weight_grad(p, dpp, "dw_ple_proj")}
    (dmixed, dattn, dconvout, do, drest, dg_post_mix, db_gate, dwconv) = _mix_bwd(
        dx1, mixed, o, proj, wconv, small["g_post_mix"], full["w_attn_out"], full["w_conv_out"], full["w_o"], tm)
    big.update({"w_attn_out": _weight_grad(o, dattn, "dw_attn_out"),
                "w_conv_out": _weight_grad(conv_in, dconvout, "dw_conv_out"),
                "w_o": _weight_grad(mix_in, dmixed, "dw_o")})
    dq, dk, dv, *scattered = _attn_bwd(proj, o, do, aw, t, scatter_rest and scatter_rest([big[n] for n in REST]))
    pieces = [dq, dk, dv, drest]
    dw_in, col0, ni = None, 0, full["w_in"].shape[1]
    for i, piece in enumerate(pieces):
        dw_in = _weight_grad(h1, piece, "dw_in_%d" % i, dw_in, col0, ni)
        col0 += piece.shape[1]
    big["w_in"] = dw_in
    dx, dg_pre_mix, *scattered_in = _in_proj_bwd(x, dx1, pieces, small["g_pre_mix"], full["w_in"], tm,
                                                scatter_in and scatter_in(dw_in))
    small_grads = {"g_pre_mix": dg_pre_mix, "b_gate": db_gate, "g_post_mix": dg_post_mix, "g_pre_mlp": dg_pre_mlp,
                   "g_post_mlp": dg_post_mlp, "g_ple": dg_ple, "w_conv": dwconv}
    return loss[0, 0], dx, big, small_grads, scattered_in + scattered


PACK_ROWS = 16


def _pack_layout(shapes, d):
    slots, at = [], 0
    for i, (r, c) in enumerate(shapes):
        assert d % c == 0
        for row in range(r):
            slots.append((i, row, at // d, at % d))
            at += c
        at = -(-at // d) * d
    assert at <= PACK_ROWS * d
    return slots


def _pack_small(groups, d):
    shapes = [a.shape for a in groups[0]]
    slots = _pack_layout(shapes, d)
    n = len(shapes)

    def body(*refs):
        ins, outs = refs[:n * len(groups)], refs[n * len(groups):]
        for g, out in enumerate(outs):
            out[...] = jnp.zeros_like(out)
            for i, row, pr, pc in slots:
                src = ins[g * n + i]
                out[pr:pr + 1, pc:pc + shapes[i][1]] = src[row:row + 1, :]

    vmem = pl.BlockSpec(memory_space=pltpu.VMEM)
    return pl.pallas_call(
        body, name="pack_small", in_specs=[vmem] * (n * len(groups)), out_specs=[vmem] * len(groups),
        out_shape=[jax.ShapeDtypeStruct((PACK_ROWS, d), F32)] * len(groups),
    )(*[a for group in groups for a in group])


def _unpack_small(pack, shapes, d):
    slots = _pack_layout(shapes, d)
    return [jnp.stack([pack[pr, pc:pc + shapes[i][1]] for j, row, pr, pc in slots if j == i])
            for i in range(len(shapes))]


def kernel(x, p, g_pre_mix, w_in, b_gate, w_conv, w_attn_out, w_conv_out, w_o, g_post_mix, g_pre_mlp, w_up, w_down, g_post_mlp, g_ple, w_ple_gate, w_ple_proj, loss_target, m_g_pre_mix, m_w_in, m_b_gate, m_w_conv, m_w_attn_out, m_w_conv_out, m_w_o, m_g_post_mix, m_g_pre_mlp, m_w_up, m_w_down, m_g_post_mlp, m_g_ple, m_w_ple_gate, m_w_ple_proj, v_g_pre_mix, v_w_in, v_b_gate, v_w_conv, v_w_attn_out, v_w_conv_out, v_w_o, v_g_post_mix, v_g_pre_mlp, v_w_up, v_w_down, v_g_post_mlp, v_g_ple, v_w_ple_gate, v_w_ple_proj):
    given = dict(locals())
    order = ["g_pre_mix", "w_in", "b_gate", "w_conv", "w_attn_out", "w_conv_out", "w_o", "g_post_mix", "g_pre_mlp",
             "w_up", "w_down", "g_post_mlp", "g_ple", "w_ple_gate", "w_ple_proj"]
    d = x.shape[-1]
    me = 4 * lax.axis_index("x") + 2 * lax.axis_index("y") + lax.axis_index("c")

    col = [COL_SHARDED[n] for n in BIG]
    shards = _to_bf16([given[n][0] for n in BIG])
    cw_shard = w_conv.shape[-1]
    conv_tile = jnp.pad(w_conv[0], ((0, HALO - CONV_K), (0, LANES - cw_shard)))
    wconv_of = lambda tiles: jnp.concatenate([tiles[dev, :CONV_K, :cw_shard] for dev in range(N_DEV)], axis=1)

    small = {n: given[n] for n in SMALL}
    loss, dx, big_grads, small_grads, parts = _local_grads(
        x[0], p[0, 0], loss_target[0], small, wconv_of, {}, w_attn_out.shape[1], w_conv_out.shape[1],
        ROW_BLOCK, ATTN_BLOCK, shards[0],
        _join(_gather_exchange(shards[1:], col[1:]), _broadcast_exchange([conv_tile])),
        lambda grads: _scatter_exchange(grads, col[1:]), lambda grad: _scatter_exchange([grad], col[:1]))
    small_names = list(SMALL) + ["w_conv"]
    two_d = lambda a: a.reshape(-1, d) if a.shape[-1] > d else a.reshape(-1, a.shape[-1])
    full_conv = lambda a: lax.dynamic_update_slice(jnp.zeros((CONV_K, N_DEV * cw_shard), F32), a[0],
                                                   (jnp.int32(0), me * cw_shard))
    groups = [[two_d(small_grads[n]) for n in small_names] + [loss.reshape(1, 1)]]
    for pre in ("", "m_", "v_"):
        groups.append([two_d(given[pre + n]) for n in SMALL] + [full_conv(given[pre + "w_conv"]), jnp.zeros((1, 1), F32)])
    pack, *state = _pack_small(groups, d)
    packs, = _exchange_call(_broadcast_exchange([pack]), "share_small_grads")

    grads, deltas, new_m, new_v = {}, {}, {}, {}
    for n, part in zip(BIG, parts):
        grads[n], deltas[n], new_m[n], new_v[n] = (
            a[None] for a in _sum_and_adamw(part, given[n][0], given["m_" + n][0], given["v_" + n][0], "adamw_" + n))

    outs = _sum_and_adamw(packs, *state, "adamw_small")
    shapes = [a.shape for a in groups[0]]
    for res, dst in zip(outs, (grads, deltas, new_m, new_v)):
        for n, a in zip(small_names + ["loss"], _unpack_small(res, shapes, d)):
            if n == "w_conv":
                a = lax.dynamic_slice(a, (jnp.int32(0), me * cw_shard), (CONV_K, cw_shard))[None]
            dst[n] = a.reshape(given[n].shape) if n in SMALL else a
    loss = grads["loss"][0, 0]

    return (loss, dx[None], *[grads[n] for n in order], *[deltas[n] for n in order],
            *[new_m[n] for n in order], *[new_v[n] for n in order])
```

```python
import jax
import jax.numpy as jnp
from jax import lax
from jax.experimental import pallas as pl
from jax.experimental.pallas import tpu as pltpu

F32 = jnp.float32
BF16 = jnp.bfloat16
RMS_EPS = 1e-6
N_DEV = 8
MESH_AXES = ("x", "y", "c")
LANES = 128
HEAD_DIM = 64
HEADS_PER_GROUP = LANES // HEAD_DIM
CONV_K = 3
HALO = 8
HALO_BF16 = 16
VMEM_LIMIT = 56 * 1024 * 1024
EXP2_ZERO = -150.0
LOG2_E = 1.4426950408889634

ADAM_LR = 0.001
ADAM_B1 = 0.9
ADAM_B2 = 0.999
ADAM_EPS = 1e-08
ADAM_WD = 0.01
ADAM_STEP = 10

ROW_BLOCK = 256
IN_PROJ_BLOCK = 1024
ATTN_BLOCK = 256
ATTN_ROW_SPLITS = 2
DW_TOKENS = 2048
DW_TILE = 1024
DW_PIECE_TILE = 512
FF_CHUNK = 1024


def _dot(a, b):
    return lax.dot_general(a, b, (((1,), (0,)), ((), ())), preferred_element_type=F32)


def _dot_nt(a, b):
    return lax.dot_general(a, b, (((1,), (1,)), ((), ())), preferred_element_type=F32)


def _dot_tn(a, b):
    return lax.dot_general(a, b, (((0,), (0,)), ((), ())), preferred_element_type=F32)


def _sigmoid(z):
    return 1.0 / (1.0 + jnp.exp(-z))


def _rms_scale(x):
    return lax.rsqrt(jnp.mean(x * x, axis=-1, keepdims=True) + RMS_EPS)


def _rms_bwd(xhat, r, g, dy):
    gd = dy * g
    return r * (gd - xhat * jnp.mean(gd * xhat, axis=-1, keepdims=True))


def _params(n_axes, **kw):
    return pltpu.CompilerParams(dimension_semantics=("arbitrary",) * n_axes, vmem_limit_bytes=VMEM_LIMIT, **kw)


def _load_resident(pairs, sem):
    @pl.when(pl.program_id(0) == 0)
    def _():
        copies = [pltpu.make_async_copy(src, dst, sem.at[i]) for i, (src, dst) in enumerate(pairs)]
        for cp in copies:
            cp.start()
        for cp in copies:
            cp.wait()


def _row_spec(tm, width):
    return pl.BlockSpec((tm, width), lambda i: (i, 0))


def _prev_halo_spec(tm, width, rows):
    per = tm // rows
    return pl.BlockSpec((rows, width), lambda i: (jnp.maximum(i * per - 1, 0), 0))


def _const_spec(shape):
    return pl.BlockSpec(shape, lambda i: (0,) * len(shape))


ANY = pl.BlockSpec(memory_space=pl.ANY)


def _shift_down(cur, prev, n):
    rows = lax.broadcasted_iota(jnp.int32, cur.shape, 0)
    out = pltpu.roll(cur, n, 0)
    for j in range(n):
        out = jnp.where(rows == j, prev[prev.shape[0] - n + j:prev.shape[0] - n + j + 1, :], out)
    return out


def _shift_up(cur, nxt, n):
    tm = cur.shape[0]
    rows = lax.broadcasted_iota(jnp.int32, cur.shape, 0)
    out = pltpu.roll(cur, tm - n, 0)
    for j in range(n):
        out = jnp.where(rows == tm - n + j, nxt[j:j + 1, :], out)
    return out


def _conv_taps(cm, cm_prev, wconv):
    cm1 = _shift_down(cm, cm_prev, 1)
    cm2 = _shift_down(cm, cm_prev, 2)
    cv = wconv[2:3, :] * cm + wconv[1:2, :] * cm1 + wconv[0:1, :] * cm2
    return cv, cm1, cm2


def _in_proj_fwd(x, g1, bias, w_in, order, n_plain, tm, gathered=False):
    s, d = x.shape
    cb = w_in.shape[1] if gathered else w_in.shape[1] // N_DEV
    ni = cb * N_DEV
    n_rows = s // tm
    ex = _gather_exchange([w_in], [True]) if gathered else _NO_EXCHANGE

    def body(order_ref, x_ref, g_ref, bias_ref, w_ref, proj_ref, h_ref, *rest):
        if gathered:
            w_full, h_all, w_blk, blk_sem, *sems = rest
            local, own, came, passed, from_sibling = _relayed_copies(ex, 0, [w_ref], [w_full], sems)
        else:
            w_full, (h_all, w_blk, blk_sem) = w_ref, rest
        j = pl.program_id(0)
        i = pl.program_id(1)
        blk = order_ref[j]

        @pl.when(i == 0)
        def _():
            if gathered:
                @pl.when(j == 0)
                def _():
                    local().start()
                    for k in range(4):
                        own(k).start()
                    local().wait()

                @pl.when(j == 1)
                def _():
                    from_sibling(0).wait_recv()

                for c in range(N_CHIPS_AWAY):
                    @pl.when(j == 2 + c)
                    def _(c=c):
                        came(c).wait_recv()
                        passed(c).start()

                    @pl.when(j == 2 + N_CHIPS_AWAY + c)
                    def _(c=c):
                        from_sibling(4 + c).wait_recv()

            cp = pltpu.make_async_copy(w_full.at[:, pl.ds(pl.multiple_of(blk * cb, LANES), cb)], w_blk, blk_sem.at[0])
            cp.start()
            cp.wait()

        rows = pl.ds(pl.multiple_of(i * tm, tm), tm)

        @pl.when(j == 0)
        def _():
            xv = x_ref[...]
            h = (xv * _rms_scale(xv) * g_ref[...]).astype(BF16)
            h_all[rows, :] = h
            h_ref[...] = h

        pc = _dot(h_all[rows, :], w_blk[...])
        col = blk * cb + lax.broadcasted_iota(jnp.int32, (1, cb), 1)
        proj_ref[...] = jnp.where(col >= n_plain, _sigmoid(pc + bias_ref[...]), pc).astype(BF16)

        if gathered:
            @pl.when(jnp.logical_and(j == N_DEV - 1, i == n_rows - 1))
            def _():
                for k in range(4):
                    own(k).wait_send()
                for c in range(N_CHIPS_AWAY):
                    passed(c).wait_send()

    grid_spec = pltpu.PrefetchScalarGridSpec(
        num_scalar_prefetch=1, grid=(N_DEV, n_rows),
        in_specs=[pl.BlockSpec((tm, d), lambda j, i, o: (jnp.where(j == 0, i, 0), 0)),
                  pl.BlockSpec((1, d), lambda j, i, o: (0, 0)),
                  pl.BlockSpec((1, cb), lambda j, i, o: (0, o[j])), ANY],
        out_specs=[pl.BlockSpec((tm, cb), lambda j, i, o: (i, o[j])),
                   pl.BlockSpec((tm, d), lambda j, i, o: (jnp.where(j == 0, i, n_rows - 1), 0))]
        + [ANY] * len(ex.out_shapes),
        scratch_shapes=[pltpu.VMEM((s, d), BF16), pltpu.VMEM((d, cb), BF16), pltpu.SemaphoreType.DMA((1,))]
        + _exchange_sems(ex))
    return pl.pallas_call(
        body, name="in_proj_fwd", grid_spec=grid_spec,
        out_shape=[jax.ShapeDtypeStruct((s, ni), BF16), jax.ShapeDtypeStruct((s, d), BF16)] + ex.out_shapes,
        compiler_params=_params(2),
    )(order, x, g1, bias, w_in)


def _split_hi_lo(a):
    hi = a.astype(BF16)
    return hi, (a - hi.astype(F32)).astype(BF16)


def _log2_gates(z):
    z2 = z * LOG2_E
    nz2 = -z2
    log_keep = jnp.minimum(nz2, 0.0) - jnp.log2(1.0 + jnp.exp2(jnp.minimum(z2, nz2)))
    return log_keep + z2, log_keep


def _attn_masks(t):
    row = lax.broadcasted_iota(jnp.int32, (t, t), 0)
    col = lax.broadcasted_iota(jnp.int32, (t, t), 1)
    return (col < row).astype(F32), (row > col).astype(BF16), (row >= col).astype(BF16)


def _chains(a):
    tr = a.shape[0] // ATTN_ROW_SPLITS
    return [jnp.where(_head_lanes(h), a[r * tr:(r + 1) * tr], jnp.zeros((tr, LANES), a.dtype))
            for h in range(HEADS_PER_GROUP) for r in range(ATTN_ROW_SPLITS)]


def _merge_chains(parts):
    rows = []
    for r in range(ATTN_ROW_SPLITS):
        out = parts[r]
        for h in range(1, HEADS_PER_GROUP):
            out = jnp.where(_head_lanes(h), parts[h * ATTN_ROW_SPLITS + r], out)
        rows.append(out)
    return jnp.concatenate(rows, axis=0)


def _by_stage(n_chains, stages):
    for stage in stages:
        for c in range(n_chains):
            stage(c)


def _row_parts(a):
    tr = a.shape[0] // ATTN_ROW_SPLITS
    return [a[r * tr:(r + 1) * tr] for r in range(ATTN_ROW_SPLITS)]


def _while_weights_live(qi, block, carry):
    def cond(state):
        j, carry = state
        live = jnp.max(carry[0][0])
        for run in carry[0][1:]:
            live = jnp.maximum(live, jnp.max(run))
        return jnp.logical_and(j < qi, live >= EXP2_ZERO)

    def step(state):
        j, carry = state
        return j + 1, block(qi - 1 - j, carry)

    return lax.while_loop(cond, step, (jnp.int32(0), carry))[1]


def _head_lanes(h):
    lane = lax.broadcasted_iota(jnp.int32, (1, LANES), 1)
    return (lane >= HEAD_DIM * h) & (lane < HEAD_DIM * (h + 1))


def _attn_fwd(qkv, aw, t, exchange=None):
    s = qkv.shape[0]
    groups = aw // LANES
    nq = s // t
    scale = HEAD_DIM ** -0.5
    ex = exchange or _NO_EXCHANGE
    causal, upper, _ = _attn_masks(t)
    mask_spec = pl.BlockSpec((t, t), lambda g, i: (0, 0))

    def body(q_ref, k_ref, v_ref, causal_ref, upper_ref, *rest):
        ex_in, (o_ref,), ex_out, sems = _split_refs(rest, ex, 1)
        qi = pl.program_id(1)
        _exchange_start(ex, ex_in, ex_out, sems, (groups, nq))
        upper = upper_ref[...]
        causal = _row_parts(causal_ref[...] > 0.5) * HEADS_PER_GROUP
        qs = _chains(q_ref[...] * scale)
        heads = range(len(qs))
        tr = t // ATTN_ROW_SPLITS

        def block(kb, runs, accs, diag):
            rows = pl.ds(pl.multiple_of(kb * t, t), t)
            k = k_ref[rows, :]
            v = v_ref[rows, :]
            ncs = [(h % ATTN_ROW_SPLITS + 1) * tr if diag else t for h in heads]
            live = [{} for _ in heads]
            new_runs, new_accs = [None] * len(heads), [None] * len(heads)

            def scores(h):
                live[h]["z"] = _dot_nt(qs[h], k[0:ncs[h]])

            def gates(h):
                nc = ncs[h]
                log_b, log_keep = _log2_gates(live[h].pop("z"))
                if diag:
                    log_keep = jnp.where(causal[h][:, 0:nc], log_keep, 0.0)
                hi, lo = _split_hi_lo(log_keep)
                live[h]["log_w"] = log_b + runs[h]
                live[h]["between"] = _dot(hi, upper[0:nc, 0:nc]) + _dot(lo, upper[0:nc, 0:nc])
                new_runs[h] = runs[h] + jnp.sum(log_keep, axis=1, keepdims=True)

            def weights(h):
                nc = ncs[h]
                w = jnp.exp2(live[h].pop("log_w") + live[h].pop("between"))
                if diag:
                    w = jnp.where(causal[h][:, 0:nc], w, 0.0)
                new_accs[h] = accs[h] + _dot(w.astype(BF16), v[0:nc])

            _by_stage(len(heads), [scores, gates, weights])
            return tuple(new_runs), tuple(new_accs)

        carry = block(qi, [jnp.zeros((tr, 1), F32)] * len(heads), [jnp.zeros((tr, LANES), F32)] * len(heads), True)
        _, accs = _while_weights_live(qi, lambda kb, carry: block(kb, *carry, False), carry)
        o_ref[...] = _merge_chains(accs)
        _exchange_wait(ex, ex_in, ex_out, sems, (groups, nq))

    return pl.pallas_call(
        body, name="attn_fwd", grid=(groups, nq),
        in_specs=[pl.BlockSpec((t, LANES), lambda g, i: (i, g)),
                  pl.BlockSpec((s, LANES), lambda g, i: (0, groups + g)),
                  pl.BlockSpec((s, LANES), lambda g, i: (0, 2 * groups + g)), mask_spec, mask_spec]
        + [ANY] * len(ex.arrays),
        out_specs=[pl.BlockSpec((t, LANES), lambda g, i: (i, g))] + [ANY] * len(ex.out_shapes),
        out_shape=[jax.ShapeDtypeStruct((s, aw), F32)] + ex.out_shapes,
        scratch_shapes=_exchange_sems(ex),
        compiler_params=_params(2),
    )(qkv, qkv, qkv, causal, upper, *ex.arrays)


def _attn_bwd(qkv, o, do, aw, t, exchange=None):
    s = qkv.shape[0]
    groups = aw // LANES
    nq = s // t
    scale = HEAD_DIM ** -0.5
    ex = exchange or _NO_EXCHANGE

    def body(q_ref, k_ref, v_ref, o_ref, do_ref, causal_ref, upper_ref, lower_ref, *rest):
        ex_in, (dq_ref, dk_ref, dv_ref), ex_out, (dk_acc, dv_acc, *sems) = _split_refs(rest, ex, 3)
        qi = pl.program_id(1)
        _exchange_start(ex, ex_in, ex_out, sems, (groups, nq))

        @pl.when(qi == 0)
        def _():
            dk_acc[...] = jnp.zeros_like(dk_acc)
            dv_acc[...] = jnp.zeros_like(dv_acc)

        upper = upper_ref[...]
        lower_incl = lower_ref[...]
        causal = _row_parts(causal_ref[...] > 0.5) * HEADS_PER_GROUP
        q = q_ref[...] * scale
        do_b = do_ref[...]
        qs = _chains(q)
        dos = _chains(do_b)
        qs_all = jnp.concatenate(qs, axis=0)
        dos_all = jnp.concatenate(dos, axis=0)
        e_totals = [jnp.sum(part, axis=1, keepdims=True) for part in _chains(do_b.astype(F32) * o_ref[...])]
        heads = range(len(qs))
        tr = t // ATTN_ROW_SPLITS

        def block(kb, runs, e_runs, dqs, diag):
            rows = pl.ds(pl.multiple_of(kb * t, t), t)
            k = k_ref[rows, :]
            v = v_ref[rows, :]
            ncs = [(h % ATTN_ROW_SPLITS + 1) * tr if diag else t for h in heads]
            live = [{} for _ in heads]
            none = [None] * len(heads)
            new_runs, new_e_runs, new_dqs, dzbs, wbs = list(none), list(none), list(none), list(none), list(none)

            def scores(h):
                live[h]["z"] = _dot_nt(qs[h], k[0:ncs[h]])
                live[h]["dw"] = _dot_nt(dos[h], v[0:ncs[h]])

            def gates(h):
                nc = ncs[h]
                log_b, log_keep = _log2_gates(live[h].pop("z"))
                live[h]["beta"] = jnp.exp2(log_b)
                live[h]["keep"] = jnp.exp2(log_keep)
                if diag:
                    log_keep = jnp.where(causal[h][:, 0:nc], log_keep, 0.0)
                hi, lo = _split_hi_lo(log_keep)
                live[h]["log_w"] = log_b + runs[h]
                live[h]["between"] = _dot(hi, upper[0:nc, 0:nc]) + _dot(lo, upper[0:nc, 0:nc])
                new_runs[h] = runs[h] + jnp.sum(log_keep, axis=1, keepdims=True)

            def weights(h):
                nc = ncs[h]
                w = jnp.exp2(live[h].pop("log_w") + live[h].pop("between"))
                if diag:
                    w = jnp.where(causal[h][:, 0:nc], w, 0.0)
                wb = w.astype(BF16)
                e = live[h].pop("dw") * wb.astype(F32)
                hi, lo = _split_hi_lo(e)
                live[h]["e"] = e
                live[h]["e_suffix"] = _dot(hi, lower_incl[0:nc, 0:nc]) + _dot(lo, lower_incl[0:nc, 0:nc]) + e_runs[h]
                wbs[h] = wb

            def score_grads(h):
                nc = ncs[h]
                e_suffix = live[h].pop("e_suffix")
                dz = live[h].pop("e") * live[h].pop("keep") - (e_totals[h] - e_suffix) * live[h].pop("beta")
                if diag:
                    dz = jnp.where(causal[h][:, 0:nc], dz, 0.0)
                dzb = dz.astype(BF16)
                new_dqs[h] = dqs[h] + _dot(dzb, k[0:nc])
                new_e_runs[h] = e_suffix[:, 0:1]
                if nc < t:
                    unseen = jnp.zeros((tr, t - nc), BF16)
                    dzb = jnp.concatenate([dzb, unseen], axis=1)
                    wbs[h] = jnp.concatenate([wbs[h], unseen], axis=1)
                dzbs[h] = dzb

            _by_stage(len(heads), [scores, gates, weights, score_grads])
            dk_acc[rows, :] += _dot_tn(jnp.concatenate(dzbs, axis=0), qs_all)
            dv_acc[rows, :] += _dot_tn(jnp.concatenate(wbs, axis=0), dos_all)
            return tuple(new_runs), tuple(new_e_runs), tuple(new_dqs)

        zero_cols = [jnp.zeros((tr, 1), F32)] * len(heads)
        carry = block(qi, zero_cols, zero_cols, [jnp.zeros((tr, LANES), F32)] * len(heads), True)
        _, _, dqs = _while_weights_live(qi, lambda kb, carry: block(kb, *carry, False), carry)
        dq_ref[...] = (_merge_chains(dqs) * scale).astype(BF16)

        @pl.when(qi == nq - 1)
        def _():
            dk_ref[...] = dk_acc[...].astype(BF16)
            dv_ref[...] = dv_acc[...].astype(BF16)

        _exchange_wait(ex, ex_in, ex_out, sems, (groups, nq))

    blk = pl.BlockSpec((t, LANES), lambda g, i: (i, g))
    slab = pl.BlockSpec((s, LANES), lambda g, i: (0, g))
    mask_spec = pl.BlockSpec((t, t), lambda g, i: (0, 0))
    return pl.pallas_call(
        body, name="attn_bwd", grid=(groups, nq),
        in_specs=[blk, pl.BlockSpec((s, LANES), lambda g, i: (0, groups + g)),
                  pl.BlockSpec((s, LANES), lambda g, i: (0, 2 * groups + g)), blk, blk, mask_spec, mask_spec, mask_spec]
        + [ANY] * len(ex.arrays),
        out_specs=[blk, slab, slab] + [ANY] * len(ex.out_shapes),
        out_shape=[jax.ShapeDtypeStruct((s, aw), BF16)] * 3 + ex.out_shapes,
        scratch_shapes=[pltpu.VMEM((s, LANES), F32), pltpu.VMEM((s, LANES), F32)] + _exchange_sems(ex),
        compiler_params=_params(2),
    )(qkv, qkv, qkv, o, do, *_attn_masks(t), *ex.arrays)


def _branches(o_b, conv, conv_prev, wconv, w_ao, w_co, cw, first):
    conv = conv.astype(F32)
    conv_prev = conv_prev.astype(F32)
    cb = conv[:, 0:cw]
    cm = conv[:, cw:2 * cw] * conv[:, 2 * cw:3 * cw]
    cm_prev = conv_prev[:, cw:2 * cw] * conv_prev[:, 2 * cw:3 * cw]
    cm_prev = jnp.where(first, 0.0, cm_prev)
    cv, cm1, cm2 = _conv_taps(cm, cm_prev, wconv)
    conv_in = (cb * cv).astype(BF16)
    return _dot(o_b, w_ao), _dot(conv_in, w_co), conv_in, cb, cv, cm, cm1, cm2


def _proj_specs(tm, aw, cw, d, row):
    assert (3 * aw) % (3 * cw) == 0 and (3 * aw + 3 * cw) % d == 0
    conv_col, gate_col, per = 3 * aw // (3 * cw), (3 * aw + 3 * cw) // d, tm // HALO_BF16
    return [pl.BlockSpec((tm, 3 * cw), lambda i: (row(i), conv_col)),
            pl.BlockSpec((HALO_BF16, 3 * cw), lambda i: (jnp.maximum(row(i) * per - 1, 0), conv_col)),
            pl.BlockSpec((tm, d), lambda i: (row(i), gate_col)), pl.BlockSpec((tm, d), lambda i: (row(i), gate_col + 1))]


def _mix_fwd(x, o, proj, wconv, g_post, w_ao, w_co, w_o, tm):
    s, d = x.shape
    aw, cw = w_ao.shape[0], w_co.shape[0]

    def body(x_ref, o_ref, conv_ref, prev_ref, ga_ref, gc_ref, wc_ref, g_ref, wao_hbm, wco_hbm, wo_hbm,
             x1_ref, mixed_ref, mixin_ref, convin_ref, wao, wco, wo, sem):
        _load_resident([(wao_hbm, wao), (wco_hbm, wco), (wo_hbm, wo)], sem)
        y_attn, y_conv, conv_in, *_ = _branches(
            o_ref[...].astype(BF16), conv_ref[...], prev_ref[...], wc_ref[...], wao[...], wco[...], cw,
            pl.program_id(0) == 0)
        mix_in = (ga_ref[...].astype(F32) * y_attn + gc_ref[...].astype(F32) * y_conv).astype(BF16)
        mixed = _dot(mix_in, wo[...])
        x1_ref[...] = x_ref[...] + mixed * _rms_scale(mixed) * g_ref[...]
        mixed_ref[...] = mixed
        mixin_ref[...] = mix_in
        convin_ref[...] = conv_in

    return pl.pallas_call(
        body, name="mix_fwd", grid=(s // tm,),
        in_specs=[_row_spec(tm, d), _row_spec(tm, aw)] + _proj_specs(tm, aw, cw, d, lambda i: i)
        + [_const_spec((CONV_K, cw)), _const_spec((1, d)), ANY, ANY, ANY],
        out_specs=[_row_spec(tm, d), _row_spec(tm, d), _row_spec(tm, d), _row_spec(tm, cw)],
        out_shape=[jax.ShapeDtypeStruct((s, d), F32), jax.ShapeDtypeStruct((s, d), F32),
                   jax.ShapeDtypeStruct((s, d), BF16), jax.ShapeDtypeStruct((s, cw), BF16)],
        scratch_shapes=[pltpu.VMEM(w_ao.shape, BF16), pltpu.VMEM(w_co.shape, BF16), pltpu.VMEM(w_o.shape, BF16),
                        pltpu.SemaphoreType.DMA((3,))],
        compiler_params=_params(1),
    )(x, o, proj, proj, proj, proj, wconv, g_post, w_ao, w_co, w_o)


def _mix_bwd(dx1, mixed, o, proj, wconv, g_post, w_ao, w_co, w_o, tm):
    s, d = dx1.shape
    aw, cw = w_ao.shape[0], w_co.shape[0]
    n = s // tm

    def body(dx1_ref, mixed_ref, o_ref, conv_ref, prev_ref, ga_ref, gc_ref, wc_ref, g_ref, wao_hbm, wco_hbm, wo_hbm,
             dmixed_ref, dattn_ref, dconvout_ref, do_ref, drest_ref, dg_ref, dbias_ref, dwc_ref,
             wao, wco, wo, dcv_next, sem):
        i = pl.program_id(0)
        _load_resident([(wao_hbm, wao), (wco_hbm, wco), (wo_hbm, wo)], sem)

        @pl.when(i == 0)
        def _():
            dg_ref[...] = jnp.zeros_like(dg_ref)
            dbias_ref[...] = jnp.zeros_like(dbias_ref)
            dwc_ref[...] = jnp.zeros_like(dwc_ref)
            dcv_next[...] = jnp.zeros_like(dcv_next)

        mixed = mixed_ref[...]
        r = _rms_scale(mixed)
        mhat = mixed * r
        dn = dx1_ref[...]
        dg_ref[...] += jnp.sum(dn * mhat, axis=0, keepdims=True)
        dmixed = _rms_bwd(mhat, r, g_ref[...], dn).astype(BF16)
        dmixed_ref[...] = dmixed
        dmi = _dot_nt(dmixed, wo[...])

        wc = wc_ref[...]
        conv = conv_ref[...].astype(F32)
        y_attn, y_conv, _, cb, cv, cm, cm1, cm2 = _branches(
            o_ref[...].astype(BF16), conv, prev_ref[...], wc, wao[...], wco[...], cw, i == n - 1)
        ga = ga_ref[...].astype(F32)
        gc = gc_ref[...].astype(F32)
        dpre_a = dmi * y_attn * ga * (1.0 - ga)
        dpre_c = dmi * y_conv * gc * (1.0 - gc)
        drest_ref[:, 3 * cw:3 * cw + d] = dpre_a.astype(BF16)
        drest_ref[:, 3 * cw + d:3 * cw + 2 * d] = dpre_c.astype(BF16)
        dbias_ref[:, 0:d] += jnp.sum(dpre_a, axis=0, keepdims=True)
        dbias_ref[:, d:2 * d] += jnp.sum(dpre_c, axis=0, keepdims=True)

        dattn = (dmi * ga).astype(BF16)
        dattn_ref[...] = dattn
        do_ref[...] = _dot_nt(dattn, wao[...]).astype(BF16)
        dconvout = (dmi * gc).astype(BF16)
        dconvout_ref[...] = dconvout
        dconv_in = _dot_nt(dconvout, wco[...])
        drest_ref[:, 0:cw] = (dconv_in * cv).astype(BF16)

        dcv = dconv_in * cb
        following = dcv_next[...]
        dcm = wc[2:3, :] * dcv + wc[1:2, :] * _shift_up(dcv, following, 1) + wc[0:1, :] * _shift_up(dcv, following, 2)
        drest_ref[:, cw:2 * cw] = (dcm * conv[:, 2 * cw:3 * cw]).astype(BF16)
        drest_ref[:, 2 * cw:3 * cw] = (dcm * conv[:, cw:2 * cw]).astype(BF16)
        for tap, shifted in enumerate((cm2, cm1, cm)):
            dwc_ref[tap:tap + 1, :] += jnp.sum(dcv * shifted, axis=0, keepdims=True)
        dcv_next[...] = dcv[0:HALO, :]

    def rows(width):
        return pl.BlockSpec((tm, width), lambda i: (n - 1 - i, 0))

    n_rest = 3 * cw + 2 * d
    return pl.pallas_call(
        body, name="mix_bwd", grid=(n,),
        in_specs=[rows(d), rows(d), rows(aw)] + _proj_specs(tm, aw, cw, d, lambda i: n - 1 - i)
        + [_const_spec((CONV_K, cw)), _const_spec((1, d)), ANY, ANY, ANY],
        out_specs=[rows(d), rows(d), rows(d), rows(aw), rows(n_rest), _const_spec((1, d)), _const_spec((1, 2 * d)),
                   _const_spec((CONV_K, cw))],
        out_shape=[jax.ShapeDtypeStruct((s, d), BF16), jax.ShapeDtypeStruct((s, d), BF16),
                   jax.ShapeDtypeStruct((s, d), BF16), jax.ShapeDtypeStruct((s, aw), BF16),
                   jax.ShapeDtypeStruct((s, n_rest), BF16), jax.ShapeDtypeStruct((1, d), F32),
                   jax.ShapeDtypeStruct((1, 2 * d), F32), jax.ShapeDtypeStruct((CONV_K, cw), F32)],
        scratch_shapes=[pltpu.VMEM(w_ao.shape, BF16), pltpu.VMEM(w_co.shape, BF16), pltpu.VMEM(w_o.shape, BF16),
                        pltpu.VMEM((HALO, cw), F32), pltpu.SemaphoreType.DMA((3,))],
        compiler_params=_params(1),
    )(dx1, mixed, o, proj, proj, proj, proj, wconv, g_post, w_ao, w_co, w_o)


def _mlp_ple_loss(x1, p, target, g_pre, g_post, g_ple, w_up, w_dn, w_pg, w_pp, tm):
    s, d = x1.shape
    ff = w_up.shape[1]
    pd = p.shape[1]
    fc = FF_CHUNK

    def body(x1_ref, p_ref, t_ref, gpre_ref, gpost_ref, gple_ref, wup_hbm, wdn_hbm, wpg_hbm, wpp_hbm,
             dx1_ref, h2_ref, du_ref, a_ref, df_ref, h3_ref, ds3_ref, dpp_ref, loss_ref, dgpre_ref, dgpost_ref,
             dgple_ref, wup, wdn, wpg, wpp, u_scr, sem):
        _load_resident([(wup_hbm, wup), (wdn_hbm, wdn), (wpg_hbm, wpg), (wpp_hbm, wpp)], sem)

        @pl.when(pl.program_id(0) == 0)
        def _():
            for ref in (loss_ref, dgpre_ref, dgpost_ref, dgple_ref):
                ref[...] = jnp.zeros_like(ref)

        x1v = x1_ref[...]
        r2 = _rms_scale(x1v)
        x1hat = x1v * r2
        h2 = (x1hat * gpre_ref[...]).astype(BF16)
        h2_ref[...] = h2
        f = jnp.zeros((tm, d), F32)
        for c0 in range(0, ff, fc):
            u = _dot(h2, wup[:, c0:c0 + fc])
            u_scr[:, c0:c0 + fc] = u
            a = jnp.square(jnp.maximum(u, 0.0)).astype(BF16)
            a_ref[:, c0:c0 + fc] = a
            f = f + _dot(a, wdn[c0:c0 + fc, :])
        rf = _rms_scale(f)
        fhat = f * rf
        x2 = x1v + fhat * gpost_ref[...]
        r3 = _rms_scale(x2)
        x2hat = x2 * r3
        h3 = (x2hat * gple_ref[...]).astype(BF16)
        h3_ref[...] = h3
        pg = _sigmoid(_dot(h3, wpg[...]))
        pp = _dot(p_ref[...].astype(BF16), wpp[...])
        diff = x2 + pg * pp - t_ref[...]
        loss_ref[...] += 0.5 * jnp.sum(jnp.mean(diff * diff, axis=-1, keepdims=True), axis=0, keepdims=True)

        dy = diff * (1.0 / d)
        dpp_ref[...] = (dy * pg).astype(BF16)
        ds3 = (dy * pp * pg * (1.0 - pg)).astype(BF16)
        ds3_ref[...] = ds3
        dh3 = _dot_nt(ds3, wpg[...])
        dgple_ref[...] += jnp.sum(dh3 * x2hat, axis=0, keepdims=True)
        dx2 = dy + _rms_bwd(x2hat, r3, gple_ref[...], dh3)
        dgpost_ref[...] += jnp.sum(dx2 * fhat, axis=0, keepdims=True)
        df = _rms_bwd(fhat, rf, gpost_ref[...], dx2).astype(BF16)
        df_ref[...] = df
        dh2 = jnp.zeros((tm, d), F32)
        for c0 in range(0, ff, fc):
            da = _dot_nt(df, wdn[c0:c0 + fc, :])
            du = (da * (2.0 * jnp.maximum(u_scr[:, c0:c0 + fc], 0.0))).astype(BF16)
            du_ref[:, c0:c0 + fc] = du
            dh2 = dh2 + _dot_nt(du, wup[:, c0:c0 + fc])
        dgpre_ref[...] += jnp.sum(dh2 * x1hat, axis=0, keepdims=True)
        dx1_ref[...] = dx2 + _rms_bwd(x1hat, r2, gpre_ref[...], dh2)

    vec = _const_spec((1, d))
    return pl.pallas_call(
        body, name="mlp_ple_loss", grid=(s // tm,),
        in_specs=[_row_spec(tm, d), _row_spec(tm, pd), _row_spec(tm, d), vec, vec, vec, ANY, ANY, ANY, ANY],
        out_specs=[_row_spec(tm, d), _row_spec(tm, d), _row_spec(tm, ff), _row_spec(tm, ff), _row_spec(tm, d),
                   _row_spec(tm, d), _row_spec(tm, d), _row_spec(tm, d), _const_spec((1, 1)), vec, vec, vec],
        out_shape=[jax.ShapeDtypeStruct((s, d), F32), jax.ShapeDtypeStruct((s, d), BF16),
                   jax.ShapeDtypeStruct((s, ff), BF16), jax.ShapeDtypeStruct((s, ff), BF16),
                   jax.ShapeDtypeStruct((s, d), BF16), jax.ShapeDtypeStruct((s, d), BF16),
                   jax.ShapeDtypeStruct((s, d), BF16), jax.ShapeDtypeStruct((s, d), BF16),
                   jax.ShapeDtypeStruct((1, 1), F32), jax.ShapeDtypeStruct((1, d), F32),
                   jax.ShapeDtypeStruct((1, d), F32), jax.ShapeDtypeStruct((1, d), F32)],
        scratch_shapes=[pltpu.VMEM(w_up.shape, BF16), pltpu.VMEM(w_dn.shape, BF16), pltpu.VMEM(w_pg.shape, BF16),
                        pltpu.VMEM(w_pp.shape, BF16), pltpu.VMEM((tm, ff), F32), pltpu.SemaphoreType.DMA((4,))],
        compiler_params=_params(1),
    )(x1, p, target, g_pre, g_post, g_ple, w_up, w_dn, w_pg, w_pp)


def _in_proj_bwd(x, dx1, pieces, g1, w_in, tm, exchange=None):
    s, d = x.shape
    ni = w_in.shape[1]
    widths = [p.shape[1] for p in pieces]
    grid = (s // tm,)
    ex = exchange or _NO_EXCHANGE

    def body(x_ref, dx1_ref, *rest):
        piece_refs, rest = rest[:len(pieces)], rest[len(pieces):]
        g_ref, w_hbm = rest[0], rest[1]
        ex_in, (dx_ref, dg_ref), ex_out, (w_vmem, sem, *sems) = _split_refs(rest[2:], ex, 2)
        _exchange_start(ex, ex_in, ex_out, sems, grid)
        _load_resident([(w_hbm, w_vmem)], sem)

        @pl.when(pl.program_id(0) == 0)
        def _():
            dg_ref[...] = jnp.zeros_like(dg_ref)

        dh = jnp.zeros((tm, d), F32)
        c0 = 0
        for ref, width in zip(piece_refs, widths):
            dh = dh + _dot_nt(ref[...], w_vmem[:, c0:c0 + width])
            c0 += width
        xv = x_ref[...]
        r = _rms_scale(xv)
        xhat = xv * r
        dg_ref[...] += jnp.sum(dh * xhat, axis=0, keepdims=True)
        dx_ref[...] = dx1_ref[...] + _rms_bwd(xhat, r, g_ref[...], dh)
        _exchange_wait(ex, ex_in, ex_out, sems, grid)

    return pl.pallas_call(
        body, name="in_proj_bwd", grid=grid,
        in_specs=[_row_spec(tm, d), _row_spec(tm, d)] + [_row_spec(tm, w) for w in widths]
        + [_const_spec((1, d)), ANY] + [ANY] * len(ex.arrays),
        out_specs=[_row_spec(tm, d), _const_spec((1, d))] + [ANY] * len(ex.out_shapes),
        out_shape=[jax.ShapeDtypeStruct((s, d), F32), jax.ShapeDtypeStruct((1, d), F32)] + ex.out_shapes,
        scratch_shapes=[pltpu.VMEM((d, ni), BF16), pltpu.SemaphoreType.DMA((1,))] + _exchange_sems(ex),
        compiler_params=_params(1),
    )(x, dx1, *pieces, g1, w_in, *ex.arrays)


def _weight_grad(a, b, name, into=None, col0=0, n_total=None):
    s, m = a.shape
    n = b.shape[1]
    tm, tk = min(m, DW_TILE), min(s, DW_TOKENS)
    tn = min(n, DW_TILE) if n_total is None else DW_PIECE_TILE
    nk = s // tk
    j0 = col0 // tn
    assert m % tm == 0 and n % tn == 0 and col0 % tn == 0

    def body(a_ref, b_ref, *rest):
        o_ref, acc = rest[-2:]
        k = pl.program_id(2)

        @pl.when(k == 0)
        def _():
            acc[...] = jnp.zeros_like(acc)

        acc[...] += _dot_tn(a_ref[...].astype(BF16), b_ref[...].astype(BF16))

        @pl.when(k == nk - 1)
        def _():
            o_ref[...] = acc[...].astype(BF16)

    extra = [] if into is None else [into]
    return pl.pallas_call(
        body, name=name, grid=(m // tm, n // tn, nk),
        in_specs=[pl.BlockSpec((tk, tm), lambda i, j, k: (k, i)), pl.BlockSpec((tk, tn), lambda i, j, k: (k, j))]
        + [ANY] * len(extra),
        out_specs=pl.BlockSpec((tm, tn), lambda i, j, k: (i, j0 + j)),
        out_shape=jax.ShapeDtypeStruct((m, n_total or n), BF16),
        input_output_aliases={2: 0} if extra else {},
        scratch_shapes=[pltpu.VMEM((tm, tn), F32)],
        compiler_params=_params(3),
    )(a, b, *extra)


def _mesh_position():
    return tuple(lax.axis_index(a) for a in MESH_AXES)


def _peer(me, k):
    bits = ((k >> 2) & 1, (k >> 1) & 1, k & 1)
    pos = tuple(1 - m if b else m for m, b in zip(me, bits))
    return pos, 4 * pos[0] + 2 * pos[1] + pos[2]


class _Exchange:
    def __init__(self, arrays, out_shapes, src, dst, relayed=None):
        self.arrays, self.out_shapes, self.src, self.dst = list(arrays), list(out_shapes), src, dst
        self.relayed = list(relayed) if relayed is not None else [False] * len(self.arrays)


_NO_EXCHANGE = _Exchange([], [], None, None)


def _exchange_sems(ex):
    n = len(ex.arrays)
    if n == 0:
        return []
    return [pltpu.SemaphoreType.DMA((n, N_DEV - 1)), pltpu.SemaphoreType.DMA((n, N_DEV - 1)),
            pltpu.SemaphoreType.DMA((n,))]


def _split_refs(rest, ex, n_own_outs):
    n_in, n_out = len(ex.arrays), len(ex.out_shapes)
    ex_in, rest = rest[:n_in], rest[n_in:]
    own, rest = rest[:n_own_outs], rest[n_own_outs:]
    return ex_in, own, rest[:n_out], rest[n_out:]


def _direct_steps(ex, w, in_refs, out_refs, sems):
    send_sems, recv_sems, local_sems = sems
    me = _mesh_position()
    mine = 4 * me[0] + 2 * me[1] + me[2]

    def copy(k):
        landing = ex.dst(w, out_refs, mine)
        if k == 0:
            return pltpu.make_async_copy(ex.src(w, in_refs, mine), landing, local_sems.at[w])
        peer, peer_idx = _peer(me, k)
        return pltpu.make_async_remote_copy(
            src_ref=ex.src(w, in_refs, peer_idx), dst_ref=landing, send_sem=send_sems.at[w, k - 1],
            recv_sem=recv_sems.at[w, k - 1], device_id=peer, device_id_type=pl.DeviceIdType.MESH)

    ks = range(N_DEV)
    return [lambda k=k: copy(k).start() for k in ks], [], [lambda k=k: copy(k).wait() for k in ks]


def _relayed_copies(ex, w, in_refs, out_refs, sems):
    send_sems, recv_sems, local_sems = sems
    x, y, c = _mesh_position()
    chips = [(1 - x, y), (x, 1 - y), (1 - x, 1 - y)]
    sibling = (x, y, 1 - c)

    def block(px, py, pc):
        return ex.dst(w, out_refs, 4 * px + 2 * py + pc)

    def copy(k, dst, to, src=None):
        return pltpu.make_async_remote_copy(
            src_ref=ex.src(w, in_refs, None) if src is None else src, dst_ref=dst, send_sem=send_sems.at[w, k],
            recv_sem=recv_sems.at[w, k], device_id=to, device_id_type=pl.DeviceIdType.MESH)

    def local():
        return pltpu.make_async_copy(ex.src(w, in_refs, None), block(x, y, c), local_sems.at[w])

    def own(k):
        return copy(k, block(x, y, c), sibling if k == 0 else (*chips[k - 1], c))

    def came(j):
        return copy(1 + j, block(*chips[j], c), (*chips[j], c))

    def passed(j):
        return copy(4 + j, block(*chips[j], c), sibling, src=block(*chips[j], c))

    def from_sibling(k):
        return copy(k, block(x, y, 1 - c) if k == 0 else block(*chips[k - 4], 1 - c), sibling)

    return local, own, came, passed, from_sibling


N_CHIPS_AWAY = 3


def _relayed_steps(ex, w, in_refs, out_refs, sems):
    local, own, came, passed, from_sibling = _relayed_copies(ex, w, in_refs, out_refs, sems)
    js = range(N_CHIPS_AWAY)
    start = [lambda: local().start()] + [lambda k=k: own(k).start() for k in range(4)]
    relay = [step for j in js for step in (lambda j=j: came(j).wait_recv(), lambda j=j: passed(j).start())]
    finish = ([lambda: local().wait()] + [lambda k=k: own(k).wait_send() for k in range(4)]
              + [lambda j=j: passed(j).wait_send() for j in js]
              + [lambda k=k: from_sibling(k).wait_recv() for k in (0, 4, 5, 6)])
    return start, relay, finish


def _arrival_order():
    x, y, c = _mesh_position()
    chips = [(1 - x, y), (x, 1 - y), (1 - x, 1 - y)]
    order = [(x, y, c), (x, y, 1 - c)] + [(*chip, c) for chip in chips] + [(*chip, 1 - c) for chip in chips]
    return jnp.stack([4 * px + 2 * py + pc for px, py, pc in order]).astype(jnp.int32)


def _exchange_steps(ex, in_refs, out_refs, sems):
    start, relay, finish = [], [], []
    for w in range(len(ex.arrays)):
        steps = (_relayed_steps if ex.relayed[w] else _direct_steps)(ex, w, in_refs, out_refs, sems)
        start += steps[0]
        relay += steps[1]
        finish += steps[2]
    return start, relay, finish


def _run(steps):
    for step in steps:
        step()


def _at_grid_step(grid, where):
    target = {"first": [0] * len(grid), "middle": [grid[0] // 2] + [0] * (len(grid) - 1),
              "last": [g - 1 for g in grid]}[where]
    hit = pl.program_id(0) == target[0]
    for axis in range(1, len(grid)):
        hit = jnp.logical_and(hit, pl.program_id(axis) == target[axis])
    return hit


def _exchange_start(ex, in_refs, out_refs, sems, grid):
    if ex.arrays:
        @pl.when(_at_grid_step(grid, "first"))
        def _():
            _run(_exchange_steps(ex, in_refs, out_refs, sems)[0])

        if any(ex.relayed):
            assert grid[0] >= 2

            @pl.when(_at_grid_step(grid, "middle"))
            def _():
                _run(_exchange_steps(ex, in_refs, out_refs, sems)[1])


def _exchange_wait(ex, in_refs, out_refs, sems, grid):
    if ex.arrays:
        @pl.when(_at_grid_step(grid, "last"))
        def _():
            _run(_exchange_steps(ex, in_refs, out_refs, sems)[2])


def _shard_block(ref, shard_shape, by_col, idx):
    r, c = shard_shape
    if by_col:
        return ref.at[:, pl.ds(pl.multiple_of(idx * c, LANES), c)]
    return ref.at[pl.ds(pl.multiple_of(idx * r, 16), r), :]


def _full_shape(shard_shape, by_col):
    r, c = shard_shape
    return (r, N_DEV * c) if by_col else (N_DEV * r, c)


def _gather_exchange(shards, col_sharded):
    shapes = [a.shape for a in shards]
    return _Exchange(
        shards, [jax.ShapeDtypeStruct(_full_shape(sh, bc), a.dtype) for a, sh, bc in zip(shards, shapes, col_sharded)],
        lambda w, refs, idx: refs[w],
        lambda w, refs, idx: _shard_block(refs[w], shapes[w], col_sharded[w], idx), [True] * len(shards))


def _scatter_exchange(grads, col_sharded):
    shapes = []
    for g, by_col in zip(grads, col_sharded):
        r, c = g.shape
        shapes.append((r, c // N_DEV) if by_col else (r // N_DEV, c))
    return _Exchange(
        grads, [jax.ShapeDtypeStruct((N_DEV,) + sh, g.dtype) for g, sh in zip(grads, shapes)],
        lambda w, refs, idx: _shard_block(refs[w], shapes[w], col_sharded[w], idx),
        lambda w, refs, mine: refs[w].at[mine])


def _broadcast_exchange(arrays):
    return _Exchange(arrays, [jax.ShapeDtypeStruct((N_DEV,) + a.shape, a.dtype) for a in arrays],
                     lambda w, refs, idx: refs[w], lambda w, refs, mine: refs[w].at[mine])


def _join(*exs):
    arrays, shapes, owner = [], [], []
    for e in exs:
        for w in range(len(e.arrays)):
            owner.append((e, w, len(arrays), len(shapes)))
        arrays += e.arrays
        shapes += e.out_shapes

    def src(w, refs, idx):
        e, w0, i0, _ = owner[w]
        return e.src(w0, refs[i0:i0 + len(e.arrays)], idx)

    def dst(w, refs, idx):
        e, w0, _, o0 = owner[w]
        return e.dst(w0, refs[o0:o0 + len(e.out_shapes)], idx)

    return _Exchange(arrays, shapes, src, dst, [flag for e in exs for flag in e.relayed])


def _exchange_call(ex, name):
    n_in = len(ex.arrays)

    def body(*refs):
        in_refs, _, out_refs, sems = _split_refs(refs, ex, 0)
        for steps in _exchange_steps(ex, in_refs, out_refs, sems):
            _run(steps)

    return pl.pallas_call(
        body, name=name, in_specs=[ANY] * n_in, out_specs=[ANY] * len(ex.out_shapes), out_shape=ex.out_shapes,
        scratch_shapes=_exchange_sems(ex), compiler_params=pltpu.CompilerParams(vmem_limit_bytes=VMEM_LIMIT),
    )(*ex.arrays)


def _to_bf16(arrays):
    def body(*refs):
        for src, dst in zip(refs[:len(arrays)], refs[len(arrays):]):
            dst[...] = src[...].astype(BF16)

    vmem = pl.BlockSpec(memory_space=pltpu.VMEM)
    return pl.pallas_call(
        body, name="weights_to_bf16", in_specs=[vmem] * len(arrays), out_specs=[vmem] * len(arrays),
        out_shape=[jax.ShapeDtypeStruct(a.shape, BF16) for a in arrays],
        compiler_params=pltpu.CompilerParams(vmem_limit_bytes=VMEM_LIMIT),
    )(*arrays)


def _adamw(w, g, m, v):
    m = ADAM_B1 * m + (1.0 - ADAM_B1) * g
    v = ADAM_B2 * v + (1.0 - ADAM_B2) * jnp.square(g)
    m_hat = m / (1.0 - ADAM_B1 ** ADAM_STEP)
    v_hat = v / (1.0 - ADAM_B2 ** ADAM_STEP)
    delta = -ADAM_LR * (m_hat / (jnp.sqrt(v_hat) + ADAM_EPS) + ADAM_WD * w)
    return delta, m, v


def _sum_and_adamw(parts, w, m, v, name):
    r, c = w.shape
    tr = min(r, 256)

    def body(p_ref, w_ref, m_ref, v_ref, g_out, d_out, m_out, v_out):
        g = p_ref[0].astype(F32)
        for dev in range(1, N_DEV):
            g = g + p_ref[dev].astype(F32)
        g_out[...] = g
        d_out[...], m_out[...], v_out[...] = _adamw(w_ref[...], g, m_ref[...], v_ref[...])

    blk = pl.BlockSpec((tr, c), lambda i: (i, 0))
    return pl.pallas_call(
        body, name=name, grid=(r // tr,),
        in_specs=[pl.BlockSpec((N_DEV, tr, c), lambda i: (0, i, 0)), blk, blk, blk],
        out_specs=[blk] * 4, out_shape=[jax.ShapeDtypeStruct((r, c), F32)] * 4,
        compiler_params=_params(1),
    )(parts, w, m, v)


BIG = ("w_in", "w_attn_out", "w_conv_out", "w_o", "w_up", "w_down", "w_ple_gate", "w_ple_proj")
COL_SHARDED = {"w_in": True, "w_attn_out": True, "w_conv_out": True, "w_o": False, "w_up": True, "w_down": False,
               "w_ple_gate": False, "w_ple_proj": True}
SMALL = ("g_pre_mix", "b_gate", "g_post_mix", "g_pre_mlp", "g_post_mlp", "g_ple")


REST = BIG[1:]


def _local_grads(x, p, target, small, wconv, full, aw, cw, tm, t, w_in_shard=None, gather_rest=None,
                 scatter_rest=None, scatter_in=None):
    full = dict(full)
    n_plain = 3 * aw + 3 * cw
    bias = jnp.pad(small["b_gate"], ((0, 0), (n_plain, 0)))
    tm_proj = min(IN_PROJ_BLOCK, x.shape[0])
    if w_in_shard is None:
        proj, h1 = _in_proj_fwd(x, small["g_pre_mix"], bias, full["w_in"], jnp.arange(N_DEV, dtype=jnp.int32),
                                n_plain, tm_proj)
    else:
        proj, h1, full["w_in"] = _in_proj_fwd(x, small["g_pre_mix"], bias, w_in_shard, _arrival_order(), n_plain,
                                              tm_proj, gathered=True)
    o, *rest = _attn_fwd(proj, aw, t, gather_rest)
    if gather_rest is not None:
        wconv = wconv(rest.pop())
    full.update(zip(REST, rest))
    x1, mixed, mix_in, conv_in = _mix_fwd(x, o, proj, wconv, small["g_post_mix"], full["w_attn_out"],
                                          full["w_conv_out"], full["w_o"], tm)
    (dx1, h2, du, a, df, h3, ds3, dpp, loss, dg_pre_mlp, dg_post_mlp, dg_ple) = _mlp_ple_loss(
        x1, p, target, small["g_pre_mlp"], small["g_post_mlp"], small["g_ple"], full["w_up"], full["w_down"],
        full["w_ple_gate"], full["w_ple_proj"], tm)
    big = {"w_up": _weight_grad(h2, du, "dw_up"), "w_down": _weight_grad(a, df, "dw_down"),
           "w_ple_gate": _weight_grad(h3, ds3, "dw_ple_gate"), "w_ple_proj": _weight_grad(p, dpp, "dw_ple_proj")}
    (dmixed, dattn, dconvout, do, drest, dg_post_mix, db_gate, dwconv) = _mix_bwd(
        dx1, mixed, o, proj, wconv, small["g_post_mix"], full["w_attn_out"], full["w_conv_out"], full["w_o"], tm)
    big.update({"w_attn_out": _weight_grad(o, dattn, "dw_attn_out"),
                "w_conv_out": _weight_grad(conv_in, dconvout, "dw_conv_out"),
                "w_o": _weight_grad(mix_in, dmixed, "dw_o")})
    dq, dk, dv, *scattered = _attn_bwd(proj, o, do, aw, t, scatter_rest and scatter_rest([big[n] for n in REST]))
    pieces = [dq, dk, dv, drest]
    dw_in, col0, ni = None, 0, full["w_in"].shape[1]
    for i, piece in enumerate(pieces):
        dw_in = _weight_grad(h1, piece, "dw_in_%d" % i, dw_in, col0, ni)
        col0 += piece.shape[1]
    big["w_in"] = dw_in
    dx, dg_pre_mix, *scattered_in = _in_proj_bwd(x, dx1, pieces, small["g_pre_mix"], full["w_in"], tm,
                                                scatter_in and scatter_in(dw_in))
    small_grads = {"g_pre_mix": dg_pre_mix, "b_gate": db_gate, "g_post_mix": dg_post_mix, "g_pre_mlp": dg_pre_mlp,
                   "g_post_mlp": dg_post_mlp, "g_ple": dg_ple, "w_conv": dwconv}
    return loss[0, 0], dx, big, small_grads, scattered_in + scattered


PACK_ROWS = 16


def _pack_layout(shapes, d):
    slots, at = [], 0
    for i, (r, c) in enumerate(shapes):
        assert d % c == 0
        for row in range(r):
            slots.append((i, row, at // d, at % d))
            at += c
        at = -(-at // d) * d
    assert at <= PACK_ROWS * d
    return slots


def _pack_small(groups, d):
    shapes = [a.shape for a in groups[0]]
    slots = _pack_layout(shapes, d)
    n = len(shapes)

    def body(*refs):
        ins, outs = refs[:n * len(groups)], refs[n * len(groups):]
        for g, out in enumerate(outs):
            out[...] = jnp.zeros_like(out)
            for i, row, pr, pc in slots:
                src = ins[g * n + i]
                out[pr:pr + 1, pc:pc + shapes[i][1]] = src[row:row + 1, :]

    vmem = pl.BlockSpec(memory_space=pltpu.VMEM)
    return pl.pallas_call(
        body, name="pack_small", in_specs=[vmem] * (n * len(groups)), out_specs=[vmem] * len(groups),
        out_shape=[jax.ShapeDtypeStruct((PACK_ROWS, d), F32)] * len(groups),
    )(*[a for group in groups for a in group])


def _unpack_small(pack, shapes, d):
    slots = _pack_layout(shapes, d)
    return [jnp.stack([pack[pr, pc:pc + shapes[i][1]] for j, row, pr, pc in slots if j == i])
            for i in range(len(shapes))]


def kernel(x, p, g_pre_mix, w_in, b_gate, w_conv, w_attn_out, w_conv_out, w_o, g_post_mix, g_pre_mlp, w_up, w_down, g_post_mlp, g_ple, w_ple_gate, w_ple_proj, loss_target, m_g_pre_mix, m_w_in, m_b_gate, m_w_conv, m_w_attn_out, m_w_conv_out, m_w_o, m_g_post_mix, m_g_pre_mlp, m_w_up, m_w_down, m_g_post_mlp, m_g_ple, m_w_ple_gate, m_w_ple_proj, v_g_pre_mix, v_w_in, v_b_gate, v_w_conv, v_w_attn_out, v_w_conv_out, v_w_o, v_g_post_mix, v_g_pre_mlp, v_w_up, v_w_down, v_g_post_mlp, v_g_ple, v_w_ple_gate, v_w_ple_proj):
    given = dict(locals())
    order = ["g_pre_mix", "w_in", "b_gate", "w_conv", "w_attn_out", "w_conv_out", "w_o", "g_post_mix", "g_pre_mlp",
             "w_up", "w_down", "g_post_mlp", "g_ple", "w_ple_gate", "w_ple_proj"]
    d = x.shape[-1]
    me = 4 * lax.axis_index("x") + 2 * lax.axis_index("y") + lax.axis_index("c")

    col = [COL_SHARDED[n] for n in BIG]
    shards = _to_bf16([given[n][0] for n in BIG])
    cw_shard = w_conv.shape[-1]
    conv_tile = jnp.pad(w_conv[0], ((0, HALO - CONV_K), (0, LANES - cw_shard)))
    wconv_of = lambda tiles: jnp.concatenate([tiles[dev, :CONV_K, :cw_shard] for dev in range(N_DEV)], axis=1)

    small = {n: given[n] for n in SMALL}
    loss, dx, big_grads, small_grads, parts = _local_grads(
        x[0], p[0, 0], loss_target[0], small, wconv_of, {}, w_attn_out.shape[1], w_conv_out.shape[1],
        ROW_BLOCK, ATTN_BLOCK, shards[0],
        _join(_gather_exchange(shards[1:], col[1:]), _broadcast_exchange([conv_tile])),
        lambda grads: _scatter_exchange(grads, col[1:]), lambda grad: _scatter_exchange([grad], col[:1]))
    small_names = list(SMALL) + ["w_conv"]
    two_d = lambda a: a.reshape(-1, d) if a.shape[-1] > d else a.reshape(-1, a.shape[-1])
    full_conv = lambda a: lax.dynamic_update_slice(jnp.zeros((CONV_K, N_DEV * cw_shard), F32), a[0],
                                                   (jnp.int32(0), me * cw_shard))
    groups = [[two_d(small_grads[n]) for n in small_names] + [loss.reshape(1, 1)]]
    for pre in ("", "m_", "v_"):
        groups.append([two_d(given[pre + n]) for n in SMALL] + [full_conv(given[pre + "w_conv"]), jnp.zeros((1, 1), F32)])
    pack, *state = _pack_small(groups, d)
    packs, = _exchange_call(_broadcast_exchange([pack]), "share_small_grads")

    grads, deltas, new_m, new_v = {}, {}, {}, {}
    for n, part in zip(BIG, parts):
        grads[n], deltas[n], new_m[n], new_v[n] = (
            a[None] for a in _sum_and_adamw(part, given[n][0], given["m_" + n][0], given["v_" + n][0], "adamw_" + n))

    outs = _sum_and_adamw(packs, *state, "adamw_small")
    shapes = [a.shape for a in groups[0]]
    for res, dst in zip(outs, (grads, deltas, new_m, new_v)):
        for n, a in zip(small_names + ["loss"], _unpack_small(res, shapes, d)):
            if n == "w_conv":
                a = lax.dynamic_slice(a, (jnp.int32(0), me * cw_shard), (CONV_K, cw_shard))[None]
            dst[n] = a.reshape(given[n].shape) if n in SMALL else a
    loss = grads["loss"][0, 0]

    return (loss, dx[None], *[grads[n] for n in order], *[deltas[n] for n in order],
            *[new_m[n] for n in order], *[new_v[n] for n in order])
```

```python
import jax
import jax.numpy as jnp
from jax import lax
from jax.experimental import pallas as pl
from jax.experimental.pallas import tpu as pltpu

F32 = jnp.float32
BF16 = jnp.bfloat16
RMS_EPS = 1e-6
N_DEV = 8
MESH_AXES = ("x", "y", "c")
LANES = 128
HEAD_DIM = 64
HEADS_PER_GROUP = LANES // HEAD_DIM
CONV_K = 3
HALO = 8
HALO_BF16 = 16
VMEM_LIMIT = 56 * 1024 * 1024
EXP2_ZERO = -150.0
LOG2_E = 1.4426950408889634

ADAM_LR = 0.001
ADAM_B1 = 0.9
ADAM_B2 = 0.999
ADAM_EPS = 1e-08
ADAM_WD = 0.01
ADAM_STEP = 10

ROW_BLOCK = 256
MIX_BLOCKS = 2
ATTN_BLOCK = 256
ATTN_ROW_SPLITS = 2
DW_TOKENS = 2048
DW_TILE = 1024
DW_PIECE_TILE = 512
FF_CHUNK = 1024
PROJ_CHUNK = 512


def _dot(a, b):
    return lax.dot_general(a, b, (((1,), (0,)), ((), ())), preferred_element_type=F32)


def _dot_nt(a, b):
    return lax.dot_general(a, b, (((1,), (1,)), ((), ())), preferred_element_type=F32)


def _dot_tn(a, b):
    return lax.dot_general(a, b, (((0,), (0,)), ((), ())), preferred_element_type=F32)


def _sigmoid(z):
    return 1.0 / (1.0 + jnp.exp(-z))


def _rms_scale(x):
    return lax.rsqrt(jnp.mean(x * x, axis=-1, keepdims=True) + RMS_EPS)


def _rms_bwd(xhat, r, g, dy):
    gd = dy * g
    return r * (gd - xhat * jnp.mean(gd * xhat, axis=-1, keepdims=True))


def _params(n_axes, **kw):
    return pltpu.CompilerParams(dimension_semantics=("arbitrary",) * n_axes, vmem_limit_bytes=VMEM_LIMIT, **kw)


def _load_resident(pairs, sem):
    @pl.when(pl.program_id(0) == 0)
    def _():
        copies = [pltpu.make_async_copy(src, dst, sem.at[i]) for i, (src, dst) in enumerate(pairs)]
        for cp in copies:
            cp.start()
        for cp in copies:
            cp.wait()


def _row_spec(tm, width):
    return pl.BlockSpec((tm, width), lambda i: (i, 0))


def _prev_halo_spec(tm, width, rows):
    per = tm // rows
    return pl.BlockSpec((rows, width), lambda i: (jnp.maximum(i * per - 1, 0), 0))


def _const_spec(shape):
    return pl.BlockSpec(shape, lambda i: (0,) * len(shape))


ANY = pl.BlockSpec(memory_space=pl.ANY)


def _shift_down(cur, prev, n):
    rows = lax.broadcasted_iota(jnp.int32, cur.shape, 0)
    out = pltpu.roll(cur, n, 0)
    for j in range(n):
        out = jnp.where(rows == j, prev[prev.shape[0] - n + j:prev.shape[0] - n + j + 1, :], out)
    return out


def _shift_up(cur, nxt, n):
    tm = cur.shape[0]
    rows = lax.broadcasted_iota(jnp.int32, cur.shape, 0)
    out = pltpu.roll(cur, tm - n, 0)
    for j in range(n):
        out = jnp.where(rows == tm - n + j, nxt[j:j + 1, :], out)
    return out


def _conv_taps(cm, cm_prev, wconv):
    cm1 = _shift_down(cm, cm_prev, 1)
    cm2 = _shift_down(cm, cm_prev, 2)
    cv = wconv[2:3, :] * cm + wconv[1:2, :] * cm1 + wconv[0:1, :] * cm2
    return cv, cm1, cm2


def _in_proj_fwd(x, g1, b_gate, w_in, aw, cw, tm):
    s, d = x.shape
    ni = w_in.shape[1]
    n_qkv, n_conv = 3 * aw, 3 * cw
    ch = PROJ_CHUNK

    def body(x_ref, g_ref, b_ref, w_hbm, qkv_ref, conv_ref, gate_ref, h_ref, w_vmem, sem):
        _load_resident([(w_hbm, w_vmem)], sem)
        xv = x_ref[...]
        h = (xv * _rms_scale(xv) * g_ref[...]).astype(BF16)
        h_ref[...] = h
        for c0 in range(0, ni, ch):
            pc = _dot(h, w_vmem[:, c0:c0 + ch])
            if c0 < n_qkv:
                qkv_ref[:, c0:c0 + ch] = pc.astype(BF16)
            elif c0 < n_qkv + n_conv:
                conv_ref[:, c0 - n_qkv:c0 - n_qkv + ch] = pc.astype(BF16)
            else:
                g0 = c0 - n_qkv - n_conv
                gate_ref[:, g0:g0 + ch] = _sigmoid(pc + b_ref[:, g0:g0 + ch]).astype(BF16)

    return pl.pallas_call(
        body, name="in_proj_fwd", grid=(s // tm,),
        in_specs=[_row_spec(tm, d), _const_spec((1, d)), _const_spec((1, 2 * d)), ANY],
        out_specs=[_row_spec(tm, n_qkv), _row_spec(tm, n_conv), _row_spec(tm, 2 * d), _row_spec(tm, d)],
        out_shape=[jax.ShapeDtypeStruct((s, n_qkv), BF16), jax.ShapeDtypeStruct((s, n_conv), BF16),
                   jax.ShapeDtypeStruct((s, 2 * d), BF16), jax.ShapeDtypeStruct((s, d), BF16)],
        scratch_shapes=[pltpu.VMEM((d, ni), BF16), pltpu.SemaphoreType.DMA((1,))],
        compiler_params=_params(1),
    )(x, g1, b_gate, w_in)


def _split_hi_lo(a):
    hi = a.astype(BF16)
    return hi, (a - hi.astype(F32)).astype(BF16)


def _log2_gates(z):
    z2 = z * LOG2_E
    nz2 = -z2
    log_keep = jnp.minimum(nz2, 0.0) - jnp.log2(1.0 + jnp.exp2(jnp.minimum(z2, nz2)))
    return log_keep + z2, log_keep


def _attn_masks(t):
    row = lax.broadcasted_iota(jnp.int32, (t, t), 0)
    col = lax.broadcasted_iota(jnp.int32, (t, t), 1)
    return (col < row).astype(F32), (row > col).astype(BF16), (row >= col).astype(BF16)


def _chains(a):
    tr = a.shape[0] // ATTN_ROW_SPLITS
    return [jnp.where(_head_lanes(h), a[r * tr:(r + 1) * tr], jnp.zeros((tr, LANES), a.dtype))
            for h in range(HEADS_PER_GROUP) for r in range(ATTN_ROW_SPLITS)]


def _merge_chains(parts):
    rows = []
    for r in range(ATTN_ROW_SPLITS):
        out = parts[r]
        for h in range(1, HEADS_PER_GROUP):
            out = jnp.where(_head_lanes(h), parts[h * ATTN_ROW_SPLITS + r], out)
        rows.append(out)
    return jnp.concatenate(rows, axis=0)


def _by_stage(n_chains, stages):
    for stage in stages:
        for c in range(n_chains):
            stage(c)


def _row_parts(a):
    tr = a.shape[0] // ATTN_ROW_SPLITS
    return [a[r * tr:(r + 1) * tr] for r in range(ATTN_ROW_SPLITS)]


def _while_weights_live(qi, block, carry):
    def cond(state):
        j, carry = state
        live = jnp.max(carry[0][0])
        for run in carry[0][1:]:
            live = jnp.maximum(live, jnp.max(run))
        return jnp.logical_and(j < qi, live >= EXP2_ZERO)

    def step(state):
        j, carry = state
        return j + 1, block(qi - 1 - j, carry)

    return lax.while_loop(cond, step, (jnp.int32(0), carry))[1]


def _head_lanes(h):
    lane = lax.broadcasted_iota(jnp.int32, (1, LANES), 1)
    return (lane >= HEAD_DIM * h) & (lane < HEAD_DIM * (h + 1))


def _attn_fwd(qkv, aw, t, exchange=None):
    s = qkv.shape[0]
    groups = aw // LANES
    nq = s // t
    scale = HEAD_DIM ** -0.5
    ex = exchange or _NO_EXCHANGE
    causal, upper, _ = _attn_masks(t)
    mask_spec = pl.BlockSpec((t, t), lambda g, i: (0, 0))

    def body(q_ref, k_ref, v_ref, causal_ref, upper_ref, *rest):
        ex_in, (o_ref,), ex_out, sems = _split_refs(rest, ex, 1)
        qi = pl.program_id(1)
        _exchange_start(ex, ex_in, ex_out, sems, (groups, nq))
        upper = upper_ref[...]
        causal = _row_parts(causal_ref[...] > 0.5) * HEADS_PER_GROUP
        qs = _chains(q_ref[...] * scale)
        heads = range(len(qs))
        tr = t // ATTN_ROW_SPLITS

        def block(kb, runs, accs, diag):
            rows = pl.ds(pl.multiple_of(kb * t, t), t)
            k = k_ref[rows, :]
            v = v_ref[rows, :]
            ncs = [(h % ATTN_ROW_SPLITS + 1) * tr if diag else t for h in heads]
            live = [{} for _ in heads]
            new_runs, new_accs = [None] * len(heads), [None] * len(heads)

            def scores(h):
                live[h]["z"] = _dot_nt(qs[h], k[0:ncs[h]])

            def gates(h):
                nc = ncs[h]
                log_b, log_keep = _log2_gates(live[h].pop("z"))
                if diag:
                    log_keep = jnp.where(causal[h][:, 0:nc], log_keep, 0.0)
                hi, lo = _split_hi_lo(log_keep)
                live[h]["log_w"] = log_b + runs[h]
                live[h]["between"] = _dot(hi, upper[0:nc, 0:nc]) + _dot(lo, upper[0:nc, 0:nc])
                new_runs[h] = runs[h] + jnp.sum(log_keep, axis=1, keepdims=True)

            def weights(h):
                nc = ncs[h]
                w = jnp.exp2(live[h].pop("log_w") + live[h].pop("between"))
                if diag:
                    w = jnp.where(causal[h][:, 0:nc], w, 0.0)
                new_accs[h] = accs[h] + _dot(w.astype(BF16), v[0:nc])

            _by_stage(len(heads), [scores, gates, weights])
            return tuple(new_runs), tuple(new_accs)

        carry = block(qi, [jnp.zeros((tr, 1), F32)] * len(heads), [jnp.zeros((tr, LANES), F32)] * len(heads), True)
        _, accs = _while_weights_live(qi, lambda kb, carry: block(kb, *carry, False), carry)
        o_ref[...] = _merge_chains(accs)
        _exchange_wait(ex, ex_in, ex_out, sems, (groups, nq))

    return pl.pallas_call(
        body, name="attn_fwd", grid=(groups, nq),
        in_specs=[pl.BlockSpec((t, LANES), lambda g, i: (i, g)),
                  pl.BlockSpec((s, LANES), lambda g, i: (0, groups + g)),
                  pl.BlockSpec((s, LANES), lambda g, i: (0, 2 * groups + g)), mask_spec, mask_spec]
        + [ANY] * len(ex.arrays),
        out_specs=[pl.BlockSpec((t, LANES), lambda g, i: (i, g))] + [ANY] * len(ex.out_shapes),
        out_shape=[jax.ShapeDtypeStruct((s, aw), F32)] + ex.out_shapes,
        scratch_shapes=_exchange_sems(ex),
        compiler_params=_params(2),
    )(qkv, qkv, qkv, causal, upper, *ex.arrays)


def _attn_bwd(qkv, o, do, aw, t, exchange=None):
    s = qkv.shape[0]
    groups = aw // LANES
    nq = s // t
    scale = HEAD_DIM ** -0.5
    ex = exchange or _NO_EXCHANGE

    def body(q_ref, k_ref, v_ref, o_ref, do_ref, causal_ref, upper_ref, lower_ref, *rest):
        ex_in, (dq_ref, dk_ref, dv_ref), ex_out, (dk_acc, dv_acc, *sems) = _split_refs(rest, ex, 3)
        qi = pl.program_id(1)
        _exchange_start(ex, ex_in, ex_out, sems, (groups, nq))

        @pl.when(qi == 0)
        def _():
            dk_acc[...] = jnp.zeros_like(dk_acc)
            dv_acc[...] = jnp.zeros_like(dv_acc)

        upper = upper_ref[...]
        lower_incl = lower_ref[...]
        causal = _row_parts(causal_ref[...] > 0.5) * HEADS_PER_GROUP
        q = q_ref[...] * scale
        do_b = do_ref[...]
        qs = _chains(q)
        dos = _chains(do_b)
        qs_all = jnp.concatenate(qs, axis=0)
        dos_all = jnp.concatenate(dos, axis=0)
        e_totals = [jnp.sum(part, axis=1, keepdims=True) for part in _chains(do_b.astype(F32) * o_ref[...])]
        heads = range(len(qs))
        tr = t // ATTN_ROW_SPLITS

        def block(kb, runs, e_runs, dqs, diag):
            rows = pl.ds(pl.multiple_of(kb * t, t), t)
            k = k_ref[rows, :]
            v = v_ref[rows, :]
            ncs = [(h % ATTN_ROW_SPLITS + 1) * tr if diag else t for h in heads]
            live = [{} for _ in heads]
            none = [None] * len(heads)
            new_runs, new_e_runs, new_dqs, dzbs, wbs = list(none), list(none), list(none), list(none), list(none)

            def scores(h):
                live[h]["z"] = _dot_nt(qs[h], k[0:ncs[h]])
                live[h]["dw"] = _dot_nt(dos[h], v[0:ncs[h]])

            def gates(h):
                nc = ncs[h]
                log_b, log_keep = _log2_gates(live[h].pop("z"))
                live[h]["beta"] = jnp.exp2(log_b)
                live[h]["keep"] = jnp.exp2(log_keep)
                if diag:
                    log_keep = jnp.where(causal[h][:, 0:nc], log_keep, 0.0)
                hi, lo = _split_hi_lo(log_keep)
                live[h]["log_w"] = log_b + runs[h]
                live[h]["between"] = _dot(hi, upper[0:nc, 0:nc]) + _dot(lo, upper[0:nc, 0:nc])
                new_runs[h] = runs[h] + jnp.sum(log_keep, axis=1, keepdims=True)

            def weights(h):
                nc = ncs[h]
                w = jnp.exp2(live[h].pop("log_w") + live[h].pop("between"))
                if diag:
                    w = jnp.where(causal[h][:, 0:nc], w, 0.0)
                wb = w.astype(BF16)
                e = live[h].pop("dw") * wb.astype(F32)
                hi, lo = _split_hi_lo(e)
                live[h]["e"] = e
                live[h]["e_suffix"] = _dot(hi, lower_incl[0:nc, 0:nc]) + _dot(lo, lower_incl[0:nc, 0:nc]) + e_runs[h]
                wbs[h] = wb

            def score_grads(h):
                nc = ncs[h]
                e_suffix = live[h].pop("e_suffix")
                dz = live[h].pop("e") * live[h].pop("keep") - (e_totals[h] - e_suffix) * live[h].pop("beta")
                if diag:
                    dz = jnp.where(causal[h][:, 0:nc], dz, 0.0)
                dzb = dz.astype(BF16)
                new_dqs[h] = dqs[h] + _dot(dzb, k[0:nc])
                new_e_runs[h] = e_suffix[:, 0:1]
                if nc < t:
                    unseen = jnp.zeros((tr, t - nc), BF16)
                    dzb = jnp.concatenate([dzb, unseen], axis=1)
                    wbs[h] = jnp.concatenate([wbs[h], unseen], axis=1)
                dzbs[h] = dzb

            _by_stage(len(heads), [scores, gates, weights, score_grads])
            dk_acc[rows, :] += _dot_tn(jnp.concatenate(dzbs, axis=0), qs_all)
            dv_acc[rows, :] += _dot_tn(jnp.concatenate(wbs, axis=0), dos_all)
            return tuple(new_runs), tuple(new_e_runs), tuple(new_dqs)

        zero_cols = [jnp.zeros((tr, 1), F32)] * len(heads)
        carry = block(qi, zero_cols, zero_cols, [jnp.zeros((tr, LANES), F32)] * len(heads), True)
        _, _, dqs = _while_weights_live(qi, lambda kb, carry: block(kb, *carry, False), carry)
        dq_ref[...] = (_merge_chains(dqs) * scale).astype(BF16)

        @pl.when(qi == nq - 1)
        def _():
            dk_ref[...] = dk_acc[...].astype(BF16)
            dv_ref[...] = dv_acc[...].astype(BF16)

        _exchange_wait(ex, ex_in, ex_out, sems, (groups, nq))

    blk = pl.BlockSpec((t, LANES), lambda g, i: (i, g))
    slab = pl.BlockSpec((s, LANES), lambda g, i: (0, g))
    mask_spec = pl.BlockSpec((t, t), lambda g, i: (0, 0))
    return pl.pallas_call(
        body, name="attn_bwd", grid=(groups, nq),
        in_specs=[blk, pl.BlockSpec((s, LANES), lambda g, i: (0, groups + g)),
                  pl.BlockSpec((s, LANES), lambda g, i: (0, 2 * groups + g)), blk, blk, mask_spec, mask_spec, mask_spec]
        + [ANY] * len(ex.arrays),
        out_specs=[blk, slab, slab] + [ANY] * len(ex.out_shapes),
        out_shape=[jax.ShapeDtypeStruct((s, aw), BF16)] * 3 + ex.out_shapes,
        scratch_shapes=[pltpu.VMEM((s, LANES), F32), pltpu.VMEM((s, LANES), F32)] + _exchange_sems(ex),
        compiler_params=_params(2),
    )(qkv, qkv, qkv, o, do, *_attn_masks(t), *ex.arrays)


def _branches(o_b, conv, conv_prev, wconv, w_ao, w_co, cw, first):
    conv = conv.astype(F32)
    conv_prev = conv_prev.astype(F32)
    cb = conv[:, 0:cw]
    cm = conv[:, cw:2 * cw] * conv[:, 2 * cw:3 * cw]
    cm_prev = conv_prev[:, cw:2 * cw] * conv_prev[:, 2 * cw:3 * cw]
    cm_prev = jnp.where(first, 0.0, cm_prev)
    cv, cm1, cm2 = _conv_taps(cm, cm_prev, wconv)
    conv_in = (cb * cv).astype(BF16)
    return _dot(o_b, w_ao), _dot(conv_in, w_co), conv_in, cb, cv, cm, cm1, cm2


def _mix_fwd(x, o, conv, gate, wconv, g_post, w_ao, w_co, w_o, tm):
    s, d = x.shape
    aw, cw = w_ao.shape[0], w_co.shape[0]

    def body(x_ref, o_ref, conv_ref, prev_ref, gate_ref, wc_ref, g_ref, wao_hbm, wco_hbm, wo_hbm,
             x1_ref, mixed_ref, mixin_ref, convin_ref, wao, wco, wo, sem):
        _load_resident([(wao_hbm, wao), (wco_hbm, wco), (wo_hbm, wo)], sem)
        y_attn, y_conv, conv_in, *_ = _branches(
            o_ref[...].astype(BF16), conv_ref[...], prev_ref[...], wc_ref[...], wao[...], wco[...], cw,
            pl.program_id(0) == 0)
        mix_in = (gate_ref[:, 0:d].astype(F32) * y_attn + gate_ref[:, d:2 * d].astype(F32) * y_conv).astype(BF16)
        mixed = _dot(mix_in, wo[...])
        x1_ref[...] = x_ref[...] + mixed * _rms_scale(mixed) * g_ref[...]
        mixed_ref[...] = mixed
        mixin_ref[...] = mix_in
        convin_ref[...] = conv_in

    return pl.pallas_call(
        body, name="mix_fwd", grid=(s // tm,),
        in_specs=[_row_spec(tm, d), _row_spec(tm, aw), _row_spec(tm, 3 * cw), _prev_halo_spec(tm, 3 * cw, HALO_BF16),
                  _row_spec(tm, 2 * d), _const_spec((CONV_K, cw)), _const_spec((1, d)), ANY, ANY, ANY],
        out_specs=[_row_spec(tm, d), _row_spec(tm, d), _row_spec(tm, d), _row_spec(tm, cw)],
        out_shape=[jax.ShapeDtypeStruct((s, d), F32), jax.ShapeDtypeStruct((s, d), F32),
                   jax.ShapeDtypeStruct((s, d), BF16), jax.ShapeDtypeStruct((s, cw), BF16)],
        scratch_shapes=[pltpu.VMEM(w_ao.shape, BF16), pltpu.VMEM(w_co.shape, BF16), pltpu.VMEM(w_o.shape, BF16),
                        pltpu.SemaphoreType.DMA((3,))],
        compiler_params=_params(1),
    )(x, o, conv, conv, gate, wconv, g_post, w_ao, w_co, w_o)


def _mix_bwd(dx1, mixed, o, conv, gate, wconv, g_post, w_ao, w_co, w_o, tm):
    s, d = dx1.shape
    aw, cw = w_ao.shape[0], w_co.shape[0]
    n = s // tm
    per = tm // HALO_BF16

    def body(dx1_ref, mixed_ref, o_ref, conv_ref, prev_ref, gate_ref, wc_ref, g_ref, wao_hbm, wco_hbm, wo_hbm,
             dmixed_ref, dattn_ref, dconvout_ref, do_ref, drest_ref, dg_ref, dbias_ref, dwc_ref,
             wao, wco, wo, dcv_next, sem):
        i = pl.program_id(0)
        _load_resident([(wao_hbm, wao), (wco_hbm, wco), (wo_hbm, wo)], sem)

        @pl.when(i == 0)
        def _():
            dg_ref[...] = jnp.zeros_like(dg_ref)
            dbias_ref[...] = jnp.zeros_like(dbias_ref)
            dwc_ref[...] = jnp.zeros_like(dwc_ref)
            dcv_next[...] = jnp.zeros_like(dcv_next)

        mixed = mixed_ref[...]
        r = _rms_scale(mixed)
        mhat = mixed * r
        dn = dx1_ref[...]
        dg_ref[...] += jnp.sum(dn * mhat, axis=0, keepdims=True)
        dmixed = _rms_bwd(mhat, r, g_ref[...], dn).astype(BF16)
        dmixed_ref[...] = dmixed
        dmi = _dot_nt(dmixed, wo[...])

        wc = wc_ref[...]
        conv = conv_ref[...].astype(F32)
        y_attn, y_conv, _, cb, cv, cm, cm1, cm2 = _branches(
            o_ref[...].astype(BF16), conv, prev_ref[...], wc, wao[...], wco[...], cw, i == n - 1)
        ga = gate_ref[:, 0:d].astype(F32)
        gc = gate_ref[:, d:2 * d].astype(F32)
        dpre_a = dmi * y_attn * ga * (1.0 - ga)
        dpre_c = dmi * y_conv * gc * (1.0 - gc)
        drest_ref[:, 3 * cw:3 * cw + d] = dpre_a.astype(BF16)
        drest_ref[:, 3 * cw + d:3 * cw + 2 * d] = dpre_c.astype(BF16)
        dbias_ref[:, 0:d] += jnp.sum(dpre_a, axis=0, keepdims=True)
        dbias_ref[:, d:2 * d] += jnp.sum(dpre_c, axis=0, keepdims=True)

        dattn = (dmi * ga).astype(BF16)
        dattn_ref[...] = dattn
        do_ref[...] = _dot_nt(dattn, wao[...]).astype(BF16)
        dconvout = (dmi * gc).astype(BF16)
        dconvout_ref[...] = dconvout
        dconv_in = _dot_nt(dconvout, wco[...])
        drest_ref[:, 0:cw] = (dconv_in * cv).astype(BF16)

        dcv = dconv_in * cb
        following = dcv_next[...]
        dcm = wc[2:3, :] * dcv + wc[1:2, :] * _shift_up(dcv, following, 1) + wc[0:1, :] * _shift_up(dcv, following, 2)
        drest_ref[:, cw:2 * cw] = (dcm * conv[:, 2 * cw:3 * cw]).astype(BF16)
        drest_ref[:, 2 * cw:3 * cw] = (dcm * conv[:, cw:2 * cw]).astype(BF16)
        for tap, shifted in enumerate((cm2, cm1, cm)):
            dwc_ref[tap:tap + 1, :] += jnp.sum(dcv * shifted, axis=0, keepdims=True)
        dcv_next[...] = dcv[0:HALO, :]

    def rows(width):
        return pl.BlockSpec((tm, width), lambda i: (n - 1 - i, 0))

    prev_halo = pl.BlockSpec((HALO_BF16, 3 * cw), lambda i: (jnp.maximum((n - 1 - i) * per - 1, 0), 0))
    n_rest = 3 * cw + 2 * d
    return pl.pallas_call(
        body, name="mix_bwd", grid=(n,),
        in_specs=[rows(d), rows(d), rows(aw), rows(3 * cw), prev_halo, rows(2 * d), _const_spec((CONV_K, cw)),
                  _const_spec((1, d)), ANY, ANY, ANY],
        out_specs=[rows(d), rows(d), rows(d), rows(aw), rows(n_rest), _const_spec((1, d)), _const_spec((1, 2 * d)),
                   _const_spec((CONV_K, cw))],
        out_shape=[jax.ShapeDtypeStruct((s, d), BF16), jax.ShapeDtypeStruct((s, d), BF16),
                   jax.ShapeDtypeStruct((s, d), BF16), jax.ShapeDtypeStruct((s, aw), BF16),
                   jax.ShapeDtypeStruct((s, n_rest), BF16), jax.ShapeDtypeStruct((1, d), F32),
                   jax.ShapeDtypeStruct((1, 2 * d), F32), jax.ShapeDtypeStruct((CONV_K, cw), F32)],
        scratch_shapes=[pltpu.VMEM(w_ao.shape, BF16), pltpu.VMEM(w_co.shape, BF16), pltpu.VMEM(w_o.shape, BF16),
                        pltpu.VMEM((HALO, cw), F32), pltpu.SemaphoreType.DMA((3,))],
        compiler_params=_params(1),
    )(dx1, mixed, o, conv, conv, gate, wconv, g_post, w_ao, w_co, w_o)


def _mlp_ple_loss(x1, p, target, g_pre, g_post, g_ple, w_up, w_dn, w_pg, w_pp, tm):
    s, d = x1.shape
    ff = w_up.shape[1]
    pd = p.shape[1]
    fc = FF_CHUNK

    def body(x1_ref, p_ref, t_ref, gpre_ref, gpost_ref, gple_ref, wup_hbm, wdn_hbm, wpg_hbm, wpp_hbm,
             dx1_ref, h2_ref, du_ref, a_ref, df_ref, h3_ref, ds3_ref, dpp_ref, loss_ref, dgpre_ref, dgpost_ref,
             dgple_ref, wup, wdn, wpg, wpp, u_scr, sem):
        _load_resident([(wup_hbm, wup), (wdn_hbm, wdn), (wpg_hbm, wpg), (wpp_hbm, wpp)], sem)

        @pl.when(pl.program_id(0) == 0)
        def _():
            for ref in (loss_ref, dgpre_ref, dgpost_ref, dgple_ref):
                ref[...] = jnp.zeros_like(ref)

        x1v = x1_ref[...]
        r2 = _rms_scale(x1v)
        x1hat = x1v * r2
        h2 = (x1hat * gpre_ref[...]).astype(BF16)
        h2_ref[...] = h2
        f = jnp.zeros((tm, d), F32)
        for c0 in range(0, ff, fc):
            u = _dot(h2, wup[:, c0:c0 + fc])
            u_scr[:, c0:c0 + fc] = u
            a = jnp.square(jnp.maximum(u, 0.0)).astype(BF16)
            a_ref[:, c0:c0 + fc] = a
            f = f + _dot(a, wdn[c0:c0 + fc, :])
        rf = _rms_scale(f)
        fhat = f * rf
        x2 = x1v + fhat * gpost_ref[...]
        r3 = _rms_scale(x2)
        x2hat = x2 * r3
        h3 = (x2hat * gple_ref[...]).astype(BF16)
        h3_ref[...] = h3
        pg = _sigmoid(_dot(h3, wpg[...]))
        pp = _dot(p_ref[...].astype(BF16), wpp[...])
        diff = x2 + pg * pp - t_ref[...]
        loss_ref[...] += 0.5 * jnp.sum(jnp.mean(diff * diff, axis=-1, keepdims=True), axis=0, keepdims=True)

        dy = diff * (1.0 / d)
        dpp_ref[...] = (dy * pg).astype(BF16)
        ds3 = (dy * pp * pg * (1.0 - pg)).astype(BF16)
        ds3_ref[...] = ds3
        dh3 = _dot_nt(ds3, wpg[...])
        dgple_ref[...] += jnp.sum(dh3 * x2hat, axis=0, keepdims=True)
        dx2 = dy + _rms_bwd(x2hat, r3, gple_ref[...], dh3)
        dgpost_ref[...] += jnp.sum(dx2 * fhat, axis=0, keepdims=True)
        df = _rms_bwd(fhat, rf, gpost_ref[...], dx2).astype(BF16)
        df_ref[...] = df
        dh2 = jnp.zeros((tm, d), F32)
        for c0 in range(0, ff, fc):
            da = _dot_nt(df, wdn[c0:c0 + fc, :])
            du = (da * (2.0 * jnp.maximum(u_scr[:, c0:c0 + fc], 0.0))).astype(BF16)
            du_ref[:, c0:c0 + fc] = du
            dh2 = dh2 + _dot_nt(du, wup[:, c0:c0 + fc])
        dgpre_ref[...] += jnp.sum(dh2 * x1hat, axis=0, keepdims=True)
        dx1_ref[...] = dx2 + _rms_bwd(x1hat, r2, gpre_ref[...], dh2)

    vec = _const_spec((1, d))
    return pl.pallas_call(
        body, name="mlp_ple_loss", grid=(s // tm,),
        in_specs=[_row_spec(tm, d), _row_spec(tm, pd), _row_spec(tm, d), vec, vec, vec, ANY, ANY, ANY, ANY],
        out_specs=[_row_spec(tm, d), _row_spec(tm, d), _row_spec(tm, ff), _row_spec(tm, ff), _row_spec(tm, d),
                   _row_spec(tm, d), _row_spec(tm, d), _row_spec(tm, d), _const_spec((1, 1)), vec, vec, vec],
        out_shape=[jax.ShapeDtypeStruct((s, d), F32), jax.ShapeDtypeStruct((s, d), BF16),
                   jax.ShapeDtypeStruct((s, ff), BF16), jax.ShapeDtypeStruct((s, ff), BF16),
                   jax.ShapeDtypeStruct((s, d), BF16), jax.ShapeDtypeStruct((s, d), BF16),
                   jax.ShapeDtypeStruct((s, d), BF16), jax.ShapeDtypeStruct((s, d), BF16),
                   jax.ShapeDtypeStruct((1, 1), F32), jax.ShapeDtypeStruct((1, d), F32),
                   jax.ShapeDtypeStruct((1, d), F32), jax.ShapeDtypeStruct((1, d), F32)],
        scratch_shapes=[pltpu.VMEM(w_up.shape, BF16), pltpu.VMEM(w_dn.shape, BF16), pltpu.VMEM(w_pg.shape, BF16),
                        pltpu.VMEM(w_pp.shape, BF16), pltpu.VMEM((tm, ff), F32), pltpu.SemaphoreType.DMA((4,))],
        compiler_params=_params(1),
    )(x1, p, target, g_pre, g_post, g_ple, w_up, w_dn, w_pg, w_pp)


def _in_proj_bwd(x, dx1, pieces, g1, w_in, tm, exchange=None):
    s, d = x.shape
    ni = w_in.shape[1]
    widths = [p.shape[1] for p in pieces]
    grid = (s // tm,)
    ex = exchange or _NO_EXCHANGE

    def body(x_ref, dx1_ref, *rest):
        piece_refs, rest = rest[:len(pieces)], rest[len(pieces):]
        g_ref, w_hbm = rest[0], rest[1]
        ex_in, (dx_ref, dg_ref), ex_out, (w_vmem, sem, *sems) = _split_refs(rest[2:], ex, 2)
        _exchange_start(ex, ex_in, ex_out, sems, grid)
        _load_resident([(w_hbm, w_vmem)], sem)

        @pl.when(pl.program_id(0) == 0)
        def _():
            dg_ref[...] = jnp.zeros_like(dg_ref)

        dh = jnp.zeros((tm, d), F32)
        c0 = 0
        for ref, width in zip(piece_refs, widths):
            dh = dh + _dot_nt(ref[...], w_vmem[:, c0:c0 + width])
            c0 += width
        xv = x_ref[...]
        r = _rms_scale(xv)
        xhat = xv * r
        dg_ref[...] += jnp.sum(dh * xhat, axis=0, keepdims=True)
        dx_ref[...] = dx1_ref[...] + _rms_bwd(xhat, r, g_ref[...], dh)
        _exchange_wait(ex, ex_in, ex_out, sems, grid)

    return pl.pallas_call(
        body, name="in_proj_bwd", grid=grid,
        in_specs=[_row_spec(tm, d), _row_spec(tm, d)] + [_row_spec(tm, w) for w in widths]
        + [_const_spec((1, d)), ANY] + [ANY] * len(ex.arrays),
        out_specs=[_row_spec(tm, d), _const_spec((1, d))] + [ANY] * len(ex.out_shapes),
        out_shape=[jax.ShapeDtypeStruct((s, d), F32), jax.ShapeDtypeStruct((1, d), F32)] + ex.out_shapes,
        scratch_shapes=[pltpu.VMEM((d, ni), BF16), pltpu.SemaphoreType.DMA((1,))] + _exchange_sems(ex),
        compiler_params=_params(1),
    )(x, dx1, *pieces, g1, w_in, *ex.arrays)


def _weight_grad(a, b, name, into=None, col0=0, n_total=None):
    s, m = a.shape
    n = b.shape[1]
    tm, tk = min(m, DW_TILE), min(s, DW_TOKENS)
    tn = min(n, DW_TILE) if n_total is None else DW_PIECE_TILE
    nk = s // tk
    j0 = col0 // tn
    assert m % tm == 0 and n % tn == 0 and col0 % tn == 0

    def body(a_ref, b_ref, *rest):
        o_ref, acc = rest[-2:]
        k = pl.program_id(2)

        @pl.when(k == 0)
        def _():
            acc[...] = jnp.zeros_like(acc)

        acc[...] += _dot_tn(a_ref[...].astype(BF16), b_ref[...].astype(BF16))

        @pl.when(k == nk - 1)
        def _():
            o_ref[...] = acc[...].astype(BF16)

    extra = [] if into is None else [into]
    return pl.pallas_call(
        body, name=name, grid=(m // tm, n // tn, nk),
        in_specs=[pl.BlockSpec((tk, tm), lambda i, j, k: (k, i)), pl.BlockSpec((tk, tn), lambda i, j, k: (k, j))]
        + [ANY] * len(extra),
        out_specs=pl.BlockSpec((tm, tn), lambda i, j, k: (i, j0 + j)),
        out_shape=jax.ShapeDtypeStruct((m, n_total or n), BF16),
        input_output_aliases={2: 0} if extra else {},
        scratch_shapes=[pltpu.VMEM((tm, tn), F32)],
        compiler_params=_params(3),
    )(a, b, *extra)


def _mesh_position():
    return tuple(lax.axis_index(a) for a in MESH_AXES)


def _peer(me, k):
    bits = ((k >> 2) & 1, (k >> 1) & 1, k & 1)
    pos = tuple(1 - m if b else m for m, b in zip(me, bits))
    return pos, 4 * pos[0] + 2 * pos[1] + pos[2]


class _Exchange:
    def __init__(self, arrays, out_shapes, src, dst, relayed=None):
        self.arrays, self.out_shapes, self.src, self.dst = list(arrays), list(out_shapes), src, dst
        self.relayed = list(relayed) if relayed is not None else [False] * len(self.arrays)


_NO_EXCHANGE = _Exchange([], [], None, None)


def _exchange_sems(ex):
    n = len(ex.arrays)
    if n == 0:
        return []
    return [pltpu.SemaphoreType.DMA((n, N_DEV - 1)), pltpu.SemaphoreType.DMA((n, N_DEV - 1)),
            pltpu.SemaphoreType.DMA((n,))]


def _split_refs(rest, ex, n_own_outs):
    n_in, n_out = len(ex.arrays), len(ex.out_shapes)
    ex_in, rest = rest[:n_in], rest[n_in:]
    own, rest = rest[:n_own_outs], rest[n_own_outs:]
    return ex_in, own, rest[:n_out], rest[n_out:]


def _direct_steps(ex, w, in_refs, out_refs, sems):
    send_sems, recv_sems, local_sems = sems
    me = _mesh_position()
    mine = 4 * me[0] + 2 * me[1] + me[2]

    def copy(k):
        landing = ex.dst(w, out_refs, mine)
        if k == 0:
            return pltpu.make_async_copy(ex.src(w, in_refs, mine), landing, local_sems.at[w])
        peer, peer_idx = _peer(me, k)
        return pltpu.make_async_remote_copy(
            src_ref=ex.src(w, in_refs, peer_idx), dst_ref=landing, send_sem=send_sems.at[w, k - 1],
            recv_sem=recv_sems.at[w, k - 1], device_id=peer, device_id_type=pl.DeviceIdType.MESH)

    ks = range(N_DEV)
    return [lambda k=k: copy(k).start() for k in ks], [], [lambda k=k: copy(k).wait() for k in ks]


def _relayed_steps(ex, w, in_refs, out_refs, sems):
    send_sems, recv_sems, local_sems = sems
    x, y, c = _mesh_position()
    chips = [(1 - x, y), (x, 1 - y), (1 - x, 1 - y)]
    sibling = (x, y, 1 - c)
    js = range(len(chips))

    def block(px, py, pc):
        return ex.dst(w, out_refs, 4 * px + 2 * py + pc)

    def copy(k, dst, to, src=None):
        return pltpu.make_async_remote_copy(
            src_ref=ex.src(w, in_refs, None) if src is None else src, dst_ref=dst, send_sem=send_sems.at[w, k],
            recv_sem=recv_sems.at[w, k], device_id=to, device_id_type=pl.DeviceIdType.MESH)

    def local():
        return pltpu.make_async_copy(ex.src(w, in_refs, None), block(x, y, c), local_sems.at[w])

    def own(k):
        return copy(k, block(x, y, c), sibling if k == 0 else (*chips[k - 1], c))

    def came(j):
        return copy(1 + j, block(*chips[j], c), (*chips[j], c))

    def passed(j):
        return copy(4 + j, block(*chips[j], c), sibling, src=block(*chips[j], c))

    def from_sibling(k):
        return copy(k, block(x, y, 1 - c) if k == 0 else block(*chips[k - 4], 1 - c), sibling)

    start = [lambda: local().start()] + [lambda k=k: own(k).start() for k in range(4)]
    relay = [step for j in js for step in (lambda j=j: came(j).wait_recv(), lambda j=j: passed(j).start())]
    finish = ([lambda: local().wait()] + [lambda k=k: own(k).wait_send() for k in range(4)]
              + [lambda j=j: passed(j).wait_send() for j in js]
              + [lambda k=k: from_sibling(k).wait_recv() for k in (0, 4, 5, 6)])
    return start, relay, finish


def _exchange_steps(ex, in_refs, out_refs, sems):
    start, relay, finish = [], [], []
    for w in range(len(ex.arrays)):
        steps = (_relayed_steps if ex.relayed[w] else _direct_steps)(ex, w, in_refs, out_refs, sems)
        start += steps[0]
        relay += steps[1]
        finish += steps[2]
    return start, relay, finish


def _run(steps):
    for step in steps:
        step()


def _at_grid_step(grid, where):
    target = {"first": [0] * len(grid), "middle": [grid[0] // 2] + [0] * (len(grid) - 1),
              "last": [g - 1 for g in grid]}[where]
    hit = pl.program_id(0) == target[0]
    for axis in range(1, len(grid)):
        hit = jnp.logical_and(hit, pl.program_id(axis) == target[axis])
    return hit


def _exchange_start(ex, in_refs, out_refs, sems, grid):
    if ex.arrays:
        @pl.when(_at_grid_step(grid, "first"))
        def _():
            _run(_exchange_steps(ex, in_refs, out_refs, sems)[0])

        if any(ex.relayed):
            assert grid[0] >= 2

            @pl.when(_at_grid_step(grid, "middle"))
            def _():
                _run(_exchange_steps(ex, in_refs, out_refs, sems)[1])


def _exchange_wait(ex, in_refs, out_refs, sems, grid):
    if ex.arrays:
        @pl.when(_at_grid_step(grid, "last"))
        def _():
            _run(_exchange_steps(ex, in_refs, out_refs, sems)[2])


def _shard_block(ref, shard_shape, by_col, idx):
    r, c = shard_shape
    if by_col:
        return ref.at[:, pl.ds(pl.multiple_of(idx * c, LANES), c)]
    return ref.at[pl.ds(pl.multiple_of(idx * r, 16), r), :]


def _full_shape(shard_shape, by_col):
    r, c = shard_shape
    return (r, N_DEV * c) if by_col else (N_DEV * r, c)


def _gather_exchange(shards, col_sharded):
    shapes = [a.shape for a in shards]
    return _Exchange(
        shards, [jax.ShapeDtypeStruct(_full_shape(sh, bc), a.dtype) for a, sh, bc in zip(shards, shapes, col_sharded)],
        lambda w, refs, idx: refs[w],
        lambda w, refs, idx: _shard_block(refs[w], shapes[w], col_sharded[w], idx), [True] * len(shards))


def _scatter_exchange(grads, col_sharded):
    shapes = []
    for g, by_col in zip(grads, col_sharded):
        r, c = g.shape
        shapes.append((r, c // N_DEV) if by_col else (r // N_DEV, c))
    return _Exchange(
        grads, [jax.ShapeDtypeStruct((N_DEV,) + sh, g.dtype) for g, sh in zip(grads, shapes)],
        lambda w, refs, idx: _shard_block(refs[w], shapes[w], col_sharded[w], idx),
        lambda w, refs, mine: refs[w].at[mine])


def _broadcast_exchange(arrays):
    return _Exchange(arrays, [jax.ShapeDtypeStruct((N_DEV,) + a.shape, a.dtype) for a in arrays],
                     lambda w, refs, idx: refs[w], lambda w, refs, mine: refs[w].at[mine])


def _join(*exs):
    arrays, shapes, owner = [], [], []
    for e in exs:
        for w in range(len(e.arrays)):
            owner.append((e, w, len(arrays), len(shapes)))
        arrays += e.arrays
        shapes += e.out_shapes

    def src(w, refs, idx):
        e, w0, i0, _ = owner[w]
        return e.src(w0, refs[i0:i0 + len(e.arrays)], idx)

    def dst(w, refs, idx):
        e, w0, _, o0 = owner[w]
        return e.dst(w0, refs[o0:o0 + len(e.out_shapes)], idx)

    return _Exchange(arrays, shapes, src, dst, [flag for e in exs for flag in e.relayed])


def _exchange_call(ex, name):
    n_in = len(ex.arrays)

    def body(*refs):
        in_refs, _, out_refs, sems = _split_refs(refs, ex, 0)
        for steps in _exchange_steps(ex, in_refs, out_refs, sems):
            _run(steps)

    return pl.pallas_call(
        body, name=name, in_specs=[ANY] * n_in, out_specs=[ANY] * len(ex.out_shapes), out_shape=ex.out_shapes,
        scratch_shapes=_exchange_sems(ex), compiler_params=pltpu.CompilerParams(vmem_limit_bytes=VMEM_LIMIT),
    )(*ex.arrays)


def _to_bf16(arrays):
    def body(*refs):
        for src, dst in zip(refs[:len(arrays)], refs[len(arrays):]):
            dst[...] = src[...].astype(BF16)

    vmem = pl.BlockSpec(memory_space=pltpu.VMEM)
    return pl.pallas_call(
        body, name="weights_to_bf16", in_specs=[vmem] * len(arrays), out_specs=[vmem] * len(arrays),
        out_shape=[jax.ShapeDtypeStruct(a.shape, BF16) for a in arrays],
        compiler_params=pltpu.CompilerParams(vmem_limit_bytes=VMEM_LIMIT),
    )(*arrays)


def _adamw(w, g, m, v):
    m = ADAM_B1 * m + (1.0 - ADAM_B1) * g
    v = ADAM_B2 * v + (1.0 - ADAM_B2) * jnp.square(g)
    m_hat = m / (1.0 - ADAM_B1 ** ADAM_STEP)
    v_hat = v / (1.0 - ADAM_B2 ** ADAM_STEP)
    delta = -ADAM_LR * (m_hat / (jnp.sqrt(v_hat) + ADAM_EPS) + ADAM_WD * w)
    return delta, m, v


def _sum_and_adamw(parts, w, m, v, name):
    r, c = w.shape
    tr = min(r, 256)

    def body(p_ref, w_ref, m_ref, v_ref, g_out, d_out, m_out, v_out):
        g = p_ref[0].astype(F32)
        for dev in range(1, N_DEV):
            g = g + p_ref[dev].astype(F32)
        g_out[...] = g
        d_out[...], m_out[...], v_out[...] = _adamw(w_ref[...], g, m_ref[...], v_ref[...])

    blk = pl.BlockSpec((tr, c), lambda i: (i, 0))
    return pl.pallas_call(
        body, name=name, grid=(r // tr,),
        in_specs=[pl.BlockSpec((N_DEV, tr, c), lambda i: (0, i, 0)), blk, blk, blk],
        out_specs=[blk] * 4, out_shape=[jax.ShapeDtypeStruct((r, c), F32)] * 4,
        compiler_params=_params(1),
    )(parts, w, m, v)


BIG = ("w_in", "w_attn_out", "w_conv_out", "w_o", "w_up", "w_down", "w_ple_gate", "w_ple_proj")
COL_SHARDED = {"w_in": True, "w_attn_out": True, "w_conv_out": True, "w_o": False, "w_up": True, "w_down": False,
               "w_ple_gate": False, "w_ple_proj": True}
SMALL = ("g_pre_mix", "b_gate", "g_post_mix", "g_pre_mlp", "g_post_mlp", "g_ple")


REST = BIG[1:]


def _local_grads(x, p, target, small, wconv, full, aw, cw, tm, t, gather_rest=None, scatter_rest=None,
                 scatter_in=None):
    full = dict(full)
    qkv, conv, gate, h1 = _in_proj_fwd(x, small["g_pre_mix"], small["b_gate"], full["w_in"], aw, cw, tm)
    o, *rest = _attn_fwd(qkv, aw, t, gather_rest)
    full.update(zip(REST, rest))
    tm_mix = min(MIX_BLOCKS * tm, x.shape[0])
    x1, mixed, mix_in, conv_in = _mix_fwd(x, o, conv, gate, wconv, small["g_post_mix"], full["w_attn_out"],
                                          full["w_conv_out"], full["w_o"], tm_mix)
    (dx1, h2, du, a, df, h3, ds3, dpp, loss, dg_pre_mlp, dg_post_mlp, dg_ple) = _mlp_ple_loss(
        x1, p, target, small["g_pre_mlp"], small["g_post_mlp"], small["g_ple"], full["w_up"], full["w_down"],
        full["w_ple_gate"], full["w_ple_proj"], tm)
    big = {"w_up": _weight_grad(h2, du, "dw_up"), "w_down": _weight_grad(a, df, "dw_down"),
           "w_ple_gate": _weight_grad(h3, ds3, "dw_ple_gate"), "w_ple_proj": _weight_grad(p, dpp, "dw_ple_proj")}
    (dmixed, dattn, dconvout, do, drest, dg_post_mix, db_gate, dwconv) = _mix_bwd(
        dx1, mixed, o, conv, gate, wconv, small["g_post_mix"], full["w_attn_out"], full["w_conv_out"], full["w_o"],
        tm_mix)
    big.update({"w_attn_out": _weight_grad(o, dattn, "dw_attn_out"),
                "w_conv_out": _weight_grad(conv_in, dconvout, "dw_conv_out"),
                "w_o": _weight_grad(mix_in, dmixed, "dw_o")})
    dq, dk, dv, *scattered = _attn_bwd(qkv, o, do, aw, t, scatter_rest and scatter_rest([big[n] for n in REST]))
    pieces = [dq, dk, dv, drest]
    dw_in, col0, ni = None, 0, full["w_in"].shape[1]
    for i, piece in enumerate(pieces):
        dw_in = _weight_grad(h1, piece, "dw_in_%d" % i, dw_in, col0, ni)
        col0 += piece.shape[1]
    big["w_in"] = dw_in
    dx, dg_pre_mix, *scattered_in = _in_proj_bwd(x, dx1, pieces, small["g_pre_mix"], full["w_in"], tm,
                                                scatter_in and scatter_in(dw_in))
    small_grads = {"g_pre_mix": dg_pre_mix, "b_gate": db_gate, "g_post_mix": dg_post_mix, "g_pre_mlp": dg_pre_mlp,
                   "g_post_mlp": dg_post_mlp, "g_ple": dg_ple, "w_conv": dwconv}
    return loss[0, 0], dx, big, small_grads, scattered_in + scattered


PACK_ROWS = 16


def _pack_layout(shapes, d):
    slots, at = [], 0
    for i, (r, c) in enumerate(shapes):
        assert d % c == 0
        for row in range(r):
            slots.append((i, row, at // d, at % d))
            at += c
        at = -(-at // d) * d
    assert at <= PACK_ROWS * d
    return slots


def _pack_small(groups, d):
    shapes = [a.shape for a in groups[0]]
    slots = _pack_layout(shapes, d)
    n = len(shapes)

    def body(*refs):
        ins, outs = refs[:n * len(groups)], refs[n * len(groups):]
        for g, out in enumerate(outs):
            out[...] = jnp.zeros_like(out)
            for i, row, pr, pc in slots:
                src = ins[g * n + i]
                out[pr:pr + 1, pc:pc + shapes[i][1]] = src[row:row + 1, :]

    vmem = pl.BlockSpec(memory_space=pltpu.VMEM)
    return pl.pallas_call(
        body, name="pack_small", in_specs=[vmem] * (n * len(groups)), out_specs=[vmem] * len(groups),
        out_shape=[jax.ShapeDtypeStruct((PACK_ROWS, d), F32)] * len(groups),
    )(*[a for group in groups for a in group])


def _unpack_small(pack, shapes, d):
    slots = _pack_layout(shapes, d)
    return [jnp.stack([pack[pr, pc:pc + shapes[i][1]] for j, row, pr, pc in slots if j == i])
            for i in range(len(shapes))]


def kernel(x, p, g_pre_mix, w_in, b_gate, w_conv, w_attn_out, w_conv_out, w_o, g_post_mix, g_pre_mlp, w_up, w_down, g_post_mlp, g_ple, w_ple_gate, w_ple_proj, loss_target, m_g_pre_mix, m_w_in, m_b_gate, m_w_conv, m_w_attn_out, m_w_conv_out, m_w_o, m_g_post_mix, m_g_pre_mlp, m_w_up, m_w_down, m_g_post_mlp, m_g_ple, m_w_ple_gate, m_w_ple_proj, v_g_pre_mix, v_w_in, v_b_gate, v_w_conv, v_w_attn_out, v_w_conv_out, v_w_o, v_g_post_mix, v_g_pre_mlp, v_w_up, v_w_down, v_g_post_mlp, v_g_ple, v_w_ple_gate, v_w_ple_proj):
    given = dict(locals())
    order = ["g_pre_mix", "w_in", "b_gate", "w_conv", "w_attn_out", "w_conv_out", "w_o", "g_post_mix", "g_pre_mlp",
             "w_up", "w_down", "g_post_mlp", "g_ple", "w_ple_gate", "w_ple_proj"]
    d = x.shape[-1]
    me = 4 * lax.axis_index("x") + 2 * lax.axis_index("y") + lax.axis_index("c")

    col = [COL_SHARDED[n] for n in BIG]
    shards = _to_bf16([given[n][0] for n in BIG])
    cw_shard = w_conv.shape[-1]
    conv_tile = jnp.pad(w_conv[0], ((0, HALO - CONV_K), (0, LANES - cw_shard)))
    w_in_full, conv_g = _exchange_call(
        _join(_gather_exchange(shards[:1], col[:1]), _broadcast_exchange([conv_tile])), "gather_w_in")
    wconv = jnp.concatenate([conv_g[dev, :CONV_K, :cw_shard] for dev in range(N_DEV)], axis=1)

    small = {n: given[n] for n in SMALL}
    loss, dx, big_grads, small_grads, parts = _local_grads(
        x[0], p[0, 0], loss_target[0], small, wconv, {"w_in": w_in_full}, w_attn_out.shape[1], w_conv_out.shape[1],
        ROW_BLOCK, ATTN_BLOCK,
        _gather_exchange(shards[1:], col[1:]), lambda grads: _scatter_exchange(grads, col[1:]),
        lambda grad: _scatter_exchange([grad], col[:1]))
    small_names = list(SMALL) + ["w_conv"]
    two_d = lambda a: a.reshape(-1, d) if a.shape[-1] > d else a.reshape(-1, a.shape[-1])
    full_conv = lambda a: lax.dynamic_update_slice(jnp.zeros((CONV_K, N_DEV * cw_shard), F32), a[0],
                                                   (jnp.int32(0), me * cw_shard))
    groups = [[two_d(small_grads[n]) for n in small_names] + [loss.reshape(1, 1)]]
    for pre in ("", "m_", "v_"):
        groups.append([two_d(given[pre + n]) for n in SMALL] + [full_conv(given[pre + "w_conv"]), jnp.zeros((1, 1), F32)])
    pack, *state = _pack_small(groups, d)
    packs, = _exchange_call(_broadcast_exchange([pack]), "share_small_grads")

    grads, deltas, new_m, new_v = {}, {}, {}, {}
    for n, part in zip(BIG, parts):
        grads[n], deltas[n], new_m[n], new_v[n] = (
            a[None] for a in _sum_and_adamw(part, given[n][0], given["m_" + n][0], given["v_" + n][0], "adamw_" + n))

    outs = _sum_and_adamw(packs, *state, "adamw_small")
    shapes = [a.shape for a in groups[0]]
    for res, dst in zip(outs, (grads, deltas, new_m, new_v)):
        for n, a in zip(small_names + ["loss"], _unpack_small(res, shapes, d)):
            if n == "w_conv":
                a = lax.dynamic_slice(a, (jnp.int32(0), me * cw_shard), (CONV_K, cw_shard))[None]
            dst[n] = a.reshape(given[n].shape) if n in SMALL else a
    loss = grads["loss"][0, 0]

    return (loss, dx[None], *[grads[n] for n in order], *[deltas[n] for n in order],
            *[new_m[n] for n in order], *[new_v[n] for n in order])
```

```python
import jax
import jax.numpy as jnp
from jax import lax
from jax.experimental import pallas as pl
from jax.experimental.pallas import tpu as pltpu

F32 = jnp.float32
BF16 = jnp.bfloat16
RMS_EPS = 1e-6
N_DEV = 8
MESH_AXES = ("x", "y", "c")
LANES = 128
HEAD_DIM = 64
HEADS_PER_GROUP = LANES // HEAD_DIM
CONV_K = 3
HALO = 8
HALO_BF16 = 16
VMEM_LIMIT = 56 * 1024 * 1024
EXP2_ZERO = -150.0
LOG2_E = 1.4426950408889634

ADAM_LR = 0.001
ADAM_B1 = 0.9
ADAM_B2 = 0.999
ADAM_EPS = 1e-08
ADAM_WD = 0.01
ADAM_STEP = 10

ROW_BLOCK = 256
WIDE_BLOCKS = 2
ATTN_BLOCK = 256
ATTN_ROW_SPLITS = 2
DW_TOKENS = 2048
DW_TILE = 1024
DW_PIECE_TILE = 512
FF_CHUNK = 1024
PROJ_CHUNK = 512


def _dot(a, b):
    return lax.dot_general(a, b, (((1,), (0,)), ((), ())), preferred_element_type=F32)


def _dot_nt(a, b):
    return lax.dot_general(a, b, (((1,), (1,)), ((), ())), preferred_element_type=F32)


def _dot_tn(a, b):
    return lax.dot_general(a, b, (((0,), (0,)), ((), ())), preferred_element_type=F32)


def _sigmoid(z):
    return 1.0 / (1.0 + jnp.exp(-z))


def _rms_scale(x):
    return lax.rsqrt(jnp.mean(x * x, axis=-1, keepdims=True) + RMS_EPS)


def _rms_bwd(xhat, r, g, dy):
    gd = dy * g
    return r * (gd - xhat * jnp.mean(gd * xhat, axis=-1, keepdims=True))


def _params(n_axes, **kw):
    return pltpu.CompilerParams(dimension_semantics=("arbitrary",) * n_axes, vmem_limit_bytes=VMEM_LIMIT, **kw)


def _load_resident(pairs, sem):
    @pl.when(pl.program_id(0) == 0)
    def _():
        copies = [pltpu.make_async_copy(src, dst, sem.at[i]) for i, (src, dst) in enumerate(pairs)]
        for cp in copies:
            cp.start()
        for cp in copies:
            cp.wait()


def _row_spec(tm, width):
    return pl.BlockSpec((tm, width), lambda i: (i, 0))


def _prev_halo_spec(tm, width, rows):
    per = tm // rows
    return pl.BlockSpec((rows, width), lambda i: (jnp.maximum(i * per - 1, 0), 0))


def _const_spec(shape):
    return pl.BlockSpec(shape, lambda i: (0,) * len(shape))


ANY = pl.BlockSpec(memory_space=pl.ANY)


def _shift_down(cur, prev, n):
    rows = lax.broadcasted_iota(jnp.int32, cur.shape, 0)
    out = pltpu.roll(cur, n, 0)
    for j in range(n):
        out = jnp.where(rows == j, prev[prev.shape[0] - n + j:prev.shape[0] - n + j + 1, :], out)
    return out


def _shift_up(cur, nxt, n):
    tm = cur.shape[0]
    rows = lax.broadcasted_iota(jnp.int32, cur.shape, 0)
    out = pltpu.roll(cur, tm - n, 0)
    for j in range(n):
        out = jnp.where(rows == tm - n + j, nxt[j:j + 1, :], out)
    return out


def _conv_taps(cm, cm_prev, wconv):
    cm1 = _shift_down(cm, cm_prev, 1)
    cm2 = _shift_down(cm, cm_prev, 2)
    cv = wconv[2:3, :] * cm + wconv[1:2, :] * cm1 + wconv[0:1, :] * cm2
    return cv, cm1, cm2


def _in_proj_fwd(x, g1, b_gate, w_in, aw, cw, tm):
    s, d = x.shape
    ni = w_in.shape[1]
    n_qkv, n_conv = 3 * aw, 3 * cw
    ch = PROJ_CHUNK

    def body(x_ref, g_ref, b_ref, w_hbm, qkv_ref, conv_ref, gate_ref, h_ref, w_vmem, sem):
        _load_resident([(w_hbm, w_vmem)], sem)
        xv = x_ref[...]
        h = (xv * _rms_scale(xv) * g_ref[...]).astype(BF16)
        h_ref[...] = h
        for c0 in range(0, ni, ch):
            pc = _dot(h, w_vmem[:, c0:c0 + ch])
            if c0 < n_qkv:
                qkv_ref[:, c0:c0 + ch] = pc.astype(BF16)
            elif c0 < n_qkv + n_conv:
                conv_ref[:, c0 - n_qkv:c0 - n_qkv + ch] = pc.astype(BF16)
            else:
                g0 = c0 - n_qkv - n_conv
                gate_ref[:, g0:g0 + ch] = _sigmoid(pc + b_ref[:, g0:g0 + ch]).astype(BF16)

    return pl.pallas_call(
        body, name="in_proj_fwd", grid=(s // tm,),
        in_specs=[_row_spec(tm, d), _const_spec((1, d)), _const_spec((1, 2 * d)), ANY],
        out_specs=[_row_spec(tm, n_qkv), _row_spec(tm, n_conv), _row_spec(tm, 2 * d), _row_spec(tm, d)],
        out_shape=[jax.ShapeDtypeStruct((s, n_qkv), BF16), jax.ShapeDtypeStruct((s, n_conv), BF16),
                   jax.ShapeDtypeStruct((s, 2 * d), BF16), jax.ShapeDtypeStruct((s, d), BF16)],
        scratch_shapes=[pltpu.VMEM((d, ni), BF16), pltpu.SemaphoreType.DMA((1,))],
        compiler_params=_params(1),
    )(x, g1, b_gate, w_in)


def _split_hi_lo(a):
    hi = a.astype(BF16)
    return hi, (a - hi.astype(F32)).astype(BF16)


def _log2_gates(z):
    z2 = z * LOG2_E
    nz2 = -z2
    log_keep = jnp.minimum(nz2, 0.0) - jnp.log2(1.0 + jnp.exp2(jnp.minimum(z2, nz2)))
    return log_keep + z2, log_keep


def _attn_masks(t):
    row = lax.broadcasted_iota(jnp.int32, (t, t), 0)
    col = lax.broadcasted_iota(jnp.int32, (t, t), 1)
    return (col < row).astype(F32), (row > col).astype(BF16), (row >= col).astype(BF16)


def _chains(a):
    tr = a.shape[0] // ATTN_ROW_SPLITS
    return [jnp.where(_head_lanes(h), a[r * tr:(r + 1) * tr], jnp.zeros((tr, LANES), a.dtype))
            for h in range(HEADS_PER_GROUP) for r in range(ATTN_ROW_SPLITS)]


def _merge_chains(parts):
    rows = []
    for r in range(ATTN_ROW_SPLITS):
        out = parts[r]
        for h in range(1, HEADS_PER_GROUP):
            out = jnp.where(_head_lanes(h), parts[h * ATTN_ROW_SPLITS + r], out)
        rows.append(out)
    return jnp.concatenate(rows, axis=0)


def _by_stage(n_chains, stages):
    for stage in stages:
        for c in range(n_chains):
            stage(c)


def _row_parts(a):
    tr = a.shape[0] // ATTN_ROW_SPLITS
    return [a[r * tr:(r + 1) * tr] for r in range(ATTN_ROW_SPLITS)]


def _while_weights_live(qi, block, carry):
    def cond(state):
        j, carry = state
        live = jnp.max(carry[0][0])
        for run in carry[0][1:]:
            live = jnp.maximum(live, jnp.max(run))
        return jnp.logical_and(j < qi, live >= EXP2_ZERO)

    def step(state):
        j, carry = state
        return j + 1, block(qi - 1 - j, carry)

    return lax.while_loop(cond, step, (jnp.int32(0), carry))[1]


def _head_lanes(h):
    lane = lax.broadcasted_iota(jnp.int32, (1, LANES), 1)
    return (lane >= HEAD_DIM * h) & (lane < HEAD_DIM * (h + 1))


def _attn_fwd(qkv, aw, t, exchange=None):
    s = qkv.shape[0]
    groups = aw // LANES
    nq = s // t
    scale = HEAD_DIM ** -0.5
    ex = exchange or _NO_EXCHANGE
    causal, upper, _ = _attn_masks(t)
    mask_spec = pl.BlockSpec((t, t), lambda g, i: (0, 0))

    def body(q_ref, k_ref, v_ref, causal_ref, upper_ref, *rest):
        ex_in, (o_ref,), ex_out, sems = _split_refs(rest, ex, 1)
        qi = pl.program_id(1)
        _exchange_start(ex, ex_in, ex_out, sems, (groups, nq))
        upper = upper_ref[...]
        causal = _row_parts(causal_ref[...] > 0.5) * HEADS_PER_GROUP
        qs = _chains(q_ref[...] * scale)
        heads = range(len(qs))
        tr = t // ATTN_ROW_SPLITS

        def block(kb, runs, accs, diag):
            rows = pl.ds(pl.multiple_of(kb * t, t), t)
            k = k_ref[rows, :]
            v = v_ref[rows, :]
            ncs = [(h % ATTN_ROW_SPLITS + 1) * tr if diag else t for h in heads]
            live = [{} for _ in heads]
            new_runs, new_accs = [None] * len(heads), [None] * len(heads)

            def scores(h):
                live[h]["z"] = _dot_nt(qs[h], k[0:ncs[h]])

            def gates(h):
                nc = ncs[h]
                log_b, log_keep = _log2_gates(live[h].pop("z"))
                if diag:
                    log_keep = jnp.where(causal[h][:, 0:nc], log_keep, 0.0)
                hi, lo = _split_hi_lo(log_keep)
                live[h]["log_w"] = log_b + runs[h]
                live[h]["between"] = _dot(hi, upper[0:nc, 0:nc]) + _dot(lo, upper[0:nc, 0:nc])
                new_runs[h] = runs[h] + jnp.sum(log_keep, axis=1, keepdims=True)

            def weights(h):
                nc = ncs[h]
                w = jnp.exp2(live[h].pop("log_w") + live[h].pop("between"))
                if diag:
                    w = jnp.where(causal[h][:, 0:nc], w, 0.0)
                new_accs[h] = accs[h] + _dot(w.astype(BF16), v[0:nc])

            _by_stage(len(heads), [scores, gates, weights])
            return tuple(new_runs), tuple(new_accs)

        carry = block(qi, [jnp.zeros((tr, 1), F32)] * len(heads), [jnp.zeros((tr, LANES), F32)] * len(heads), True)
        _, accs = _while_weights_live(qi, lambda kb, carry: block(kb, *carry, False), carry)
        o_ref[...] = _merge_chains(accs)
        _exchange_wait(ex, ex_in, ex_out, sems, (groups, nq))

    return pl.pallas_call(
        body, name="attn_fwd", grid=(groups, nq),
        in_specs=[pl.BlockSpec((t, LANES), lambda g, i: (i, g)),
                  pl.BlockSpec((s, LANES), lambda g, i: (0, groups + g)),
                  pl.BlockSpec((s, LANES), lambda g, i: (0, 2 * groups + g)), mask_spec, mask_spec]
        + [ANY] * len(ex.arrays),
        out_specs=[pl.BlockSpec((t, LANES), lambda g, i: (i, g))] + [ANY] * len(ex.out_shapes),
        out_shape=[jax.ShapeDtypeStruct((s, aw), F32)] + ex.out_shapes,
        scratch_shapes=_exchange_sems(ex),
        compiler_params=_params(2),
    )(qkv, qkv, qkv, causal, upper, *ex.arrays)


def _attn_bwd(qkv, o, do, aw, t, exchange=None):
    s = qkv.shape[0]
    groups = aw // LANES
    nq = s // t
    scale = HEAD_DIM ** -0.5
    ex = exchange or _NO_EXCHANGE

    def body(q_ref, k_ref, v_ref, o_ref, do_ref, causal_ref, upper_ref, lower_ref, *rest):
        ex_in, (dq_ref, dk_ref, dv_ref), ex_out, (dk_acc, dv_acc, *sems) = _split_refs(rest, ex, 3)
        qi = pl.program_id(1)
        _exchange_start(ex, ex_in, ex_out, sems, (groups, nq))

        @pl.when(qi == 0)
        def _():
            dk_acc[...] = jnp.zeros_like(dk_acc)
            dv_acc[...] = jnp.zeros_like(dv_acc)

        upper = upper_ref[...]
        lower_incl = lower_ref[...]
        causal = _row_parts(causal_ref[...] > 0.5) * HEADS_PER_GROUP
        q = q_ref[...] * scale
        do_b = do_ref[...]
        qs = _chains(q)
        dos = _chains(do_b)
        qs_all = jnp.concatenate(qs, axis=0)
        dos_all = jnp.concatenate(dos, axis=0)
        e_totals = [jnp.sum(part, axis=1, keepdims=True) for part in _chains(do_b.astype(F32) * o_ref[...])]
        heads = range(len(qs))
        tr = t // ATTN_ROW_SPLITS

        def block(kb, runs, e_runs, dqs, diag):
            rows = pl.ds(pl.multiple_of(kb * t, t), t)
            k = k_ref[rows, :]
            v = v_ref[rows, :]
            ncs = [(h % ATTN_ROW_SPLITS + 1) * tr if diag else t for h in heads]
            live = [{} for _ in heads]
            none = [None] * len(heads)
            new_runs, new_e_runs, new_dqs, dzbs, wbs = list(none), list(none), list(none), list(none), list(none)

            def scores(h):
                live[h]["z"] = _dot_nt(qs[h], k[0:ncs[h]])
                live[h]["dw"] = _dot_nt(dos[h], v[0:ncs[h]])

            def gates(h):
                nc = ncs[h]
                log_b, log_keep = _log2_gates(live[h].pop("z"))
                live[h]["beta"] = jnp.exp2(log_b)
                live[h]["keep"] = jnp.exp2(log_keep)
                if diag:
                    log_keep = jnp.where(causal[h][:, 0:nc], log_keep, 0.0)
                hi, lo = _split_hi_lo(log_keep)
                live[h]["log_w"] = log_b + runs[h]
                live[h]["between"] = _dot(hi, upper[0:nc, 0:nc]) + _dot(lo, upper[0:nc, 0:nc])
                new_runs[h] = runs[h] + jnp.sum(log_keep, axis=1, keepdims=True)

            def weights(h):
                nc = ncs[h]
                w = jnp.exp2(live[h].pop("log_w") + live[h].pop("between"))
                if diag:
                    w = jnp.where(causal[h][:, 0:nc], w, 0.0)
                wb = w.astype(BF16)
                e = live[h].pop("dw") * wb.astype(F32)
                hi, lo = _split_hi_lo(e)
                live[h]["e"] = e
                live[h]["e_suffix"] = _dot(hi, lower_incl[0:nc, 0:nc]) + _dot(lo, lower_incl[0:nc, 0:nc]) + e_runs[h]
                wbs[h] = wb

            def score_grads(h):
                nc = ncs[h]
                e_suffix = live[h].pop("e_suffix")
                dz = live[h].pop("e") * live[h].pop("keep") - (e_totals[h] - e_suffix) * live[h].pop("beta")
                if diag:
                    dz = jnp.where(causal[h][:, 0:nc], dz, 0.0)
                dzb = dz.astype(BF16)
                new_dqs[h] = dqs[h] + _dot(dzb, k[0:nc])
                new_e_runs[h] = e_suffix[:, 0:1]
                if nc < t:
                    unseen = jnp.zeros((tr, t - nc), BF16)
                    dzb = jnp.concatenate([dzb, unseen], axis=1)
                    wbs[h] = jnp.concatenate([wbs[h], unseen], axis=1)
                dzbs[h] = dzb

            _by_stage(len(heads), [scores, gates, weights, score_grads])
            dk_acc[rows, :] += _dot_tn(jnp.concatenate(dzbs, axis=0), qs_all)
            dv_acc[rows, :] += _dot_tn(jnp.concatenate(wbs, axis=0), dos_all)
            return tuple(new_runs), tuple(new_e_runs), tuple(new_dqs)

        zero_cols = [jnp.zeros((tr, 1), F32)] * len(heads)
        carry = block(qi, zero_cols, zero_cols, [jnp.zeros((tr, LANES), F32)] * len(heads), True)
        _, _, dqs = _while_weights_live(qi, lambda kb, carry: block(kb, *carry, False), carry)
        dq_ref[...] = (_merge_chains(dqs) * scale).astype(BF16)

        @pl.when(qi == nq - 1)
        def _():
            dk_ref[...] = dk_acc[...].astype(BF16)
            dv_ref[...] = dv_acc[...].astype(BF16)

        _exchange_wait(ex, ex_in, ex_out, sems, (groups, nq))

    blk = pl.BlockSpec((t, LANES), lambda g, i: (i, g))
    slab = pl.BlockSpec((s, LANES), lambda g, i: (0, g))
    mask_spec = pl.BlockSpec((t, t), lambda g, i: (0, 0))
    return pl.pallas_call(
        body, name="attn_bwd", grid=(groups, nq),
        in_specs=[blk, pl.BlockSpec((s, LANES), lambda g, i: (0, groups + g)),
                  pl.BlockSpec((s, LANES), lambda g, i: (0, 2 * groups + g)), blk, blk, mask_spec, mask_spec, mask_spec]
        + [ANY] * len(ex.arrays),
        out_specs=[blk, slab, slab] + [ANY] * len(ex.out_shapes),
        out_shape=[jax.ShapeDtypeStruct((s, aw), BF16)] * 3 + ex.out_shapes,
        scratch_shapes=[pltpu.VMEM((s, LANES), F32), pltpu.VMEM((s, LANES), F32)] + _exchange_sems(ex),
        compiler_params=_params(2),
    )(qkv, qkv, qkv, o, do, *_attn_masks(t), *ex.arrays)


def _branches(o_b, conv, conv_prev, wconv, w_ao, w_co, cw, first):
    conv = conv.astype(F32)
    conv_prev = conv_prev.astype(F32)
    cb = conv[:, 0:cw]
    cm = conv[:, cw:2 * cw] * conv[:, 2 * cw:3 * cw]
    cm_prev = conv_prev[:, cw:2 * cw] * conv_prev[:, 2 * cw:3 * cw]
    cm_prev = jnp.where(first, 0.0, cm_prev)
    cv, cm1, cm2 = _conv_taps(cm, cm_prev, wconv)
    conv_in = (cb * cv).astype(BF16)
    return _dot(o_b, w_ao), _dot(conv_in, w_co), conv_in, cb, cv, cm, cm1, cm2


def _mix_fwd(x, o, conv, gate, wconv, g_post, w_ao, w_co, w_o, tm):
    s, d = x.shape
    aw, cw = w_ao.shape[0], w_co.shape[0]

    def body(x_ref, o_ref, conv_ref, prev_ref, gate_ref, wc_ref, g_ref, wao_hbm, wco_hbm, wo_hbm,
             x1_ref, mixed_ref, mixin_ref, convin_ref, wao, wco, wo, sem):
        _load_resident([(wao_hbm, wao), (wco_hbm, wco), (wo_hbm, wo)], sem)
        y_attn, y_conv, conv_in, *_ = _branches(
            o_ref[...].astype(BF16), conv_ref[...], prev_ref[...], wc_ref[...], wao[...], wco[...], cw,
            pl.program_id(0) == 0)
        mix_in = (gate_ref[:, 0:d].astype(F32) * y_attn + gate_ref[:, d:2 * d].astype(F32) * y_conv).astype(BF16)
        mixed = _dot(mix_in, wo[...])
        x1_ref[...] = x_ref[...] + mixed * _rms_scale(mixed) * g_ref[...]
        mixed_ref[...] = mixed
        mixin_ref[...] = mix_in
        convin_ref[...] = conv_in

    return pl.pallas_call(
        body, name="mix_fwd", grid=(s // tm,),
        in_specs=[_row_spec(tm, d), _row_spec(tm, aw), _row_spec(tm, 3 * cw), _prev_halo_spec(tm, 3 * cw, HALO_BF16),
                  _row_spec(tm, 2 * d), _const_spec((CONV_K, cw)), _const_spec((1, d)), ANY, ANY, ANY],
        out_specs=[_row_spec(tm, d), _row_spec(tm, d), _row_spec(tm, d), _row_spec(tm, cw)],
        out_shape=[jax.ShapeDtypeStruct((s, d), F32), jax.ShapeDtypeStruct((s, d), F32),
                   jax.ShapeDtypeStruct((s, d), BF16), jax.ShapeDtypeStruct((s, cw), BF16)],
        scratch_shapes=[pltpu.VMEM(w_ao.shape, BF16), pltpu.VMEM(w_co.shape, BF16), pltpu.VMEM(w_o.shape, BF16),
                        pltpu.SemaphoreType.DMA((3,))],
        compiler_params=_params(1),
    )(x, o, conv, conv, gate, wconv, g_post, w_ao, w_co, w_o)


def _mix_bwd(dx1, mixed, o, conv, gate, wconv, g_post, w_ao, w_co, w_o, tm):
    s, d = dx1.shape
    aw, cw = w_ao.shape[0], w_co.shape[0]
    n = s // tm
    per = tm // HALO_BF16

    def body(dx1_ref, mixed_ref, o_ref, conv_ref, prev_ref, gate_ref, wc_ref, g_ref, wao_hbm, wco_hbm, wo_hbm,
             dmixed_ref, dattn_ref, dconvout_ref, do_ref, drest_ref, dg_ref, dbias_ref, dwc_ref,
             wao, wco, wo, dcv_next, sem):
        i = pl.program_id(0)
        _load_resident([(wao_hbm, wao), (wco_hbm, wco), (wo_hbm, wo)], sem)

        @pl.when(i == 0)
        def _():
            dg_ref[...] = jnp.zeros_like(dg_ref)
            dbias_ref[...] = jnp.zeros_like(dbias_ref)
            dwc_ref[...] = jnp.zeros_like(dwc_ref)
            dcv_next[...] = jnp.zeros_like(dcv_next)

        mixed = mixed_ref[...]
        r = _rms_scale(mixed)
        mhat = mixed * r
        dn = dx1_ref[...]
        dg_ref[...] += jnp.sum(dn * mhat, axis=0, keepdims=True)
        dmixed = _rms_bwd(mhat, r, g_ref[...], dn).astype(BF16)
        dmixed_ref[...] = dmixed
        dmi = _dot_nt(dmixed, wo[...])

        wc = wc_ref[...]
        conv = conv_ref[...].astype(F32)
        y_attn, y_conv, _, cb, cv, cm, cm1, cm2 = _branches(
            o_ref[...].astype(BF16), conv, prev_ref[...], wc, wao[...], wco[...], cw, i == n - 1)
        ga = gate_ref[:, 0:d].astype(F32)
        gc = gate_ref[:, d:2 * d].astype(F32)
        dpre_a = dmi * y_attn * ga * (1.0 - ga)
        dpre_c = dmi * y_conv * gc * (1.0 - gc)
        drest_ref[:, 3 * cw:3 * cw + d] = dpre_a.astype(BF16)
        drest_ref[:, 3 * cw + d:3 * cw + 2 * d] = dpre_c.astype(BF16)
        dbias_ref[:, 0:d] += jnp.sum(dpre_a, axis=0, keepdims=True)
        dbias_ref[:, d:2 * d] += jnp.sum(dpre_c, axis=0, keepdims=True)

        dattn = (dmi * ga).astype(BF16)
        dattn_ref[...] = dattn
        do_ref[...] = _dot_nt(dattn, wao[...]).astype(BF16)
        dconvout = (dmi * gc).astype(BF16)
        dconvout_ref[...] = dconvout
        dconv_in = _dot_nt(dconvout, wco[...])
        drest_ref[:, 0:cw] = (dconv_in * cv).astype(BF16)

        dcv = dconv_in * cb
        following = dcv_next[...]
        dcm = wc[2:3, :] * dcv + wc[1:2, :] * _shift_up(dcv, following, 1) + wc[0:1, :] * _shift_up(dcv, following, 2)
        drest_ref[:, cw:2 * cw] = (dcm * conv[:, 2 * cw:3 * cw]).astype(BF16)
        drest_ref[:, 2 * cw:3 * cw] = (dcm * conv[:, cw:2 * cw]).astype(BF16)
        for tap, shifted in enumerate((cm2, cm1, cm)):
            dwc_ref[tap:tap + 1, :] += jnp.sum(dcv * shifted, axis=0, keepdims=True)
        dcv_next[...] = dcv[0:HALO, :]

    def rows(width):
        return pl.BlockSpec((tm, width), lambda i: (n - 1 - i, 0))

    prev_halo = pl.BlockSpec((HALO_BF16, 3 * cw), lambda i: (jnp.maximum((n - 1 - i) * per - 1, 0), 0))
    n_rest = 3 * cw + 2 * d
    return pl.pallas_call(
        body, name="mix_bwd", grid=(n,),
        in_specs=[rows(d), rows(d), rows(aw), rows(3 * cw), prev_halo, rows(2 * d), _const_spec((CONV_K, cw)),
                  _const_spec((1, d)), ANY, ANY, ANY],
        out_specs=[rows(d), rows(d), rows(d), rows(aw), rows(n_rest), _const_spec((1, d)), _const_spec((1, 2 * d)),
                   _const_spec((CONV_K, cw))],
        out_shape=[jax.ShapeDtypeStruct((s, d), BF16), jax.ShapeDtypeStruct((s, d), BF16),
                   jax.ShapeDtypeStruct((s, d), BF16), jax.ShapeDtypeStruct((s, aw), BF16),
                   jax.ShapeDtypeStruct((s, n_rest), BF16), jax.ShapeDtypeStruct((1, d), F32),
                   jax.ShapeDtypeStruct((1, 2 * d), F32), jax.ShapeDtypeStruct((CONV_K, cw), F32)],
        scratch_shapes=[pltpu.VMEM(w_ao.shape, BF16), pltpu.VMEM(w_co.shape, BF16), pltpu.VMEM(w_o.shape, BF16),
                        pltpu.VMEM((HALO, cw), F32), pltpu.SemaphoreType.DMA((3,))],
        compiler_params=_params(1),
    )(dx1, mixed, o, conv, conv, gate, wconv, g_post, w_ao, w_co, w_o)


def _mlp_ple_loss(x1, p, target, g_pre, g_post, g_ple, w_up, w_dn, w_pg, w_pp, tm):
    s, d = x1.shape
    ff = w_up.shape[1]
    pd = p.shape[1]
    fc = FF_CHUNK

    def body(x1_ref, p_ref, t_ref, gpre_ref, gpost_ref, gple_ref, wup_hbm, wdn_hbm, wpg_hbm, wpp_hbm,
             dx1_ref, h2_ref, du_ref, a_ref, df_ref, h3_ref, ds3_ref, dpp_ref, loss_ref, dgpre_ref, dgpost_ref,
             dgple_ref, wup, wdn, wpg, wpp, u_scr, sem):
        _load_resident([(wup_hbm, wup), (wdn_hbm, wdn), (wpg_hbm, wpg), (wpp_hbm, wpp)], sem)

        @pl.when(pl.program_id(0) == 0)
        def _():
            for ref in (loss_ref, dgpre_ref, dgpost_ref, dgple_ref):
                ref[...] = jnp.zeros_like(ref)

        x1v = x1_ref[...]
        r2 = _rms_scale(x1v)
        x1hat = x1v * r2
        h2 = (x1hat * gpre_ref[...]).astype(BF16)
        h2_ref[...] = h2
        f = jnp.zeros((tm, d), F32)
        for c0 in range(0, ff, fc):
            u = _dot(h2, wup[:, c0:c0 + fc])
            u_scr[:, c0:c0 + fc] = u
            a = jnp.square(jnp.maximum(u, 0.0)).astype(BF16)
            a_ref[:, c0:c0 + fc] = a
            f = f + _dot(a, wdn[c0:c0 + fc, :])
        rf = _rms_scale(f)
        fhat = f * rf
        x2 = x1v + fhat * gpost_ref[...]
        r3 = _rms_scale(x2)
        x2hat = x2 * r3
        h3 = (x2hat * gple_ref[...]).astype(BF16)
        h3_ref[...] = h3
        pg = _sigmoid(_dot(h3, wpg[...]))
        pp = _dot(p_ref[...].astype(BF16), wpp[...])
        diff = x2 + pg * pp - t_ref[...]
        loss_ref[...] += 0.5 * jnp.sum(jnp.mean(diff * diff, axis=-1, keepdims=True), axis=0, keepdims=True)

        dy = diff * (1.0 / d)
        dpp_ref[...] = (dy * pg).astype(BF16)
        ds3 = (dy * pp * pg * (1.0 - pg)).astype(BF16)
        ds3_ref[...] = ds3
        dh3 = _dot_nt(ds3, wpg[...])
        dgple_ref[...] += jnp.sum(dh3 * x2hat, axis=0, keepdims=True)
        dx2 = dy + _rms_bwd(x2hat, r3, gple_ref[...], dh3)
        dgpost_ref[...] += jnp.sum(dx2 * fhat, axis=0, keepdims=True)
        df = _rms_bwd(fhat, rf, gpost_ref[...], dx2).astype(BF16)
        df_ref[...] = df
        dh2 = jnp.zeros((tm, d), F32)
        for c0 in range(0, ff, fc):
            da = _dot_nt(df, wdn[c0:c0 + fc, :])
            du = (da * (2.0 * jnp.maximum(u_scr[:, c0:c0 + fc], 0.0))).astype(BF16)
            du_ref[:, c0:c0 + fc] = du
            dh2 = dh2 + _dot_nt(du, wup[:, c0:c0 + fc])
        dgpre_ref[...] += jnp.sum(dh2 * x1hat, axis=0, keepdims=True)
        dx1_ref[...] = dx2 + _rms_bwd(x1hat, r2, gpre_ref[...], dh2)

    vec = _const_spec((1, d))
    return pl.pallas_call(
        body, name="mlp_ple_loss", grid=(s // tm,),
        in_specs=[_row_spec(tm, d), _row_spec(tm, pd), _row_spec(tm, d), vec, vec, vec, ANY, ANY, ANY, ANY],
        out_specs=[_row_spec(tm, d), _row_spec(tm, d), _row_spec(tm, ff), _row_spec(tm, ff), _row_spec(tm, d),
                   _row_spec(tm, d), _row_spec(tm, d), _row_spec(tm, d), _const_spec((1, 1)), vec, vec, vec],
        out_shape=[jax.ShapeDtypeStruct((s, d), F32), jax.ShapeDtypeStruct((s, d), BF16),
                   jax.ShapeDtypeStruct((s, ff), BF16), jax.ShapeDtypeStruct((s, ff), BF16),
                   jax.ShapeDtypeStruct((s, d), BF16), jax.ShapeDtypeStruct((s, d), BF16),
                   jax.ShapeDtypeStruct((s, d), BF16), jax.ShapeDtypeStruct((s, d), BF16),
                   jax.ShapeDtypeStruct((1, 1), F32), jax.ShapeDtypeStruct((1, d), F32),
                   jax.ShapeDtypeStruct((1, d), F32), jax.ShapeDtypeStruct((1, d), F32)],
        scratch_shapes=[pltpu.VMEM(w_up.shape, BF16), pltpu.VMEM(w_dn.shape, BF16), pltpu.VMEM(w_pg.shape, BF16),
                        pltpu.VMEM(w_pp.shape, BF16), pltpu.VMEM((tm, ff), F32), pltpu.SemaphoreType.DMA((4,))],
        compiler_params=_params(1),
    )(x1, p, target, g_pre, g_post, g_ple, w_up, w_dn, w_pg, w_pp)


def _in_proj_bwd(x, dx1, pieces, g1, w_in, tm, exchange=None):
    s, d = x.shape
    ni = w_in.shape[1]
    widths = [p.shape[1] for p in pieces]
    grid = (s // tm,)
    ex = exchange or _NO_EXCHANGE

    def body(x_ref, dx1_ref, *rest):
        piece_refs, rest = rest[:len(pieces)], rest[len(pieces):]
        g_ref, w_hbm = rest[0], rest[1]
        ex_in, (dx_ref, dg_ref), ex_out, (w_vmem, sem, *sems) = _split_refs(rest[2:], ex, 2)
        _exchange_start(ex, ex_in, ex_out, sems, grid)
        _load_resident([(w_hbm, w_vmem)], sem)

        @pl.when(pl.program_id(0) == 0)
        def _():
            dg_ref[...] = jnp.zeros_like(dg_ref)

        dh = jnp.zeros((tm, d), F32)
        c0 = 0
        for ref, width in zip(piece_refs, widths):
            dh = dh + _dot_nt(ref[...], w_vmem[:, c0:c0 + width])
            c0 += width
        xv = x_ref[...]
        r = _rms_scale(xv)
        xhat = xv * r
        dg_ref[...] += jnp.sum(dh * xhat, axis=0, keepdims=True)
        dx_ref[...] = dx1_ref[...] + _rms_bwd(xhat, r, g_ref[...], dh)
        _exchange_wait(ex, ex_in, ex_out, sems, grid)

    return pl.pallas_call(
        body, name="in_proj_bwd", grid=grid,
        in_specs=[_row_spec(tm, d), _row_spec(tm, d)] + [_row_spec(tm, w) for w in widths]
        + [_const_spec((1, d)), ANY] + [ANY] * len(ex.arrays),
        out_specs=[_row_spec(tm, d), _const_spec((1, d))] + [ANY] * len(ex.out_shapes),
        out_shape=[jax.ShapeDtypeStruct((s, d), F32), jax.ShapeDtypeStruct((1, d), F32)] + ex.out_shapes,
        scratch_shapes=[pltpu.VMEM((d, ni), BF16), pltpu.SemaphoreType.DMA((1,))] + _exchange_sems(ex),
        compiler_params=_params(1),
    )(x, dx1, *pieces, g1, w_in, *ex.arrays)


def _weight_grad(a, b, name, into=None, col0=0, n_total=None):
    s, m = a.shape
    n = b.shape[1]
    tm, tk = min(m, DW_TILE), min(s, DW_TOKENS)
    tn = min(n, DW_TILE) if n_total is None else DW_PIECE_TILE
    nk = s // tk
    j0 = col0 // tn
    assert m % tm == 0 and n % tn == 0 and col0 % tn == 0

    def body(a_ref, b_ref, *rest):
        o_ref, acc = rest[-2:]
        k = pl.program_id(2)

        @pl.when(k == 0)
        def _():
            acc[...] = jnp.zeros_like(acc)

        acc[...] += _dot_tn(a_ref[...].astype(BF16), b_ref[...].astype(BF16))

        @pl.when(k == nk - 1)
        def _():
            o_ref[...] = acc[...].astype(BF16)

    extra = [] if into is None else [into]
    return pl.pallas_call(
        body, name=name, grid=(m // tm, n // tn, nk),
        in_specs=[pl.BlockSpec((tk, tm), lambda i, j, k: (k, i)), pl.BlockSpec((tk, tn), lambda i, j, k: (k, j))]
        + [ANY] * len(extra),
        out_specs=pl.BlockSpec((tm, tn), lambda i, j, k: (i, j0 + j)),
        out_shape=jax.ShapeDtypeStruct((m, n_total or n), BF16),
        input_output_aliases={2: 0} if extra else {},
        scratch_shapes=[pltpu.VMEM((tm, tn), F32)],
        compiler_params=_params(3),
    )(a, b, *extra)


def _mesh_position():
    return tuple(lax.axis_index(a) for a in MESH_AXES)


def _peer(me, k):
    bits = ((k >> 2) & 1, (k >> 1) & 1, k & 1)
    pos = tuple(1 - m if b else m for m, b in zip(me, bits))
    return pos, 4 * pos[0] + 2 * pos[1] + pos[2]


class _Exchange:
    def __init__(self, arrays, out_shapes, src, dst, relayed=None):
        self.arrays, self.out_shapes, self.src, self.dst = list(arrays), list(out_shapes), src, dst
        self.relayed = list(relayed) if relayed is not None else [False] * len(self.arrays)


_NO_EXCHANGE = _Exchange([], [], None, None)


def _exchange_sems(ex):
    n = len(ex.arrays)
    if n == 0:
        return []
    return [pltpu.SemaphoreType.DMA((n, N_DEV - 1)), pltpu.SemaphoreType.DMA((n, N_DEV - 1)),
            pltpu.SemaphoreType.DMA((n,))]


def _split_refs(rest, ex, n_own_outs):
    n_in, n_out = len(ex.arrays), len(ex.out_shapes)
    ex_in, rest = rest[:n_in], rest[n_in:]
    own, rest = rest[:n_own_outs], rest[n_own_outs:]
    return ex_in, own, rest[:n_out], rest[n_out:]


def _direct_steps(ex, w, in_refs, out_refs, sems):
    send_sems, recv_sems, local_sems = sems
    me = _mesh_position()
    mine = 4 * me[0] + 2 * me[1] + me[2]

    def copy(k):
        landing = ex.dst(w, out_refs, mine)
        if k == 0:
            return pltpu.make_async_copy(ex.src(w, in_refs, mine), landing, local_sems.at[w])
        peer, peer_idx = _peer(me, k)
        return pltpu.make_async_remote_copy(
            src_ref=ex.src(w, in_refs, peer_idx), dst_ref=landing, send_sem=send_sems.at[w, k - 1],
            recv_sem=recv_sems.at[w, k - 1], device_id=peer, device_id_type=pl.DeviceIdType.MESH)

    ks = range(N_DEV)
    return [lambda k=k: copy(k).start() for k in ks], [], [lambda k=k: copy(k).wait() for k in ks]


def _relayed_steps(ex, w, in_refs, out_refs, sems):
    send_sems, recv_sems, local_sems = sems
    x, y, c = _mesh_position()
    chips = [(1 - x, y), (x, 1 - y), (1 - x, 1 - y)]
    sibling = (x, y, 1 - c)
    js = range(len(chips))

    def block(px, py, pc):
        return ex.dst(w, out_refs, 4 * px + 2 * py + pc)

    def copy(k, dst, to, src=None):
        return pltpu.make_async_remote_copy(
            src_ref=ex.src(w, in_refs, None) if src is None else src, dst_ref=dst, send_sem=send_sems.at[w, k],
            recv_sem=recv_sems.at[w, k], device_id=to, device_id_type=pl.DeviceIdType.MESH)

    def local():
        return pltpu.make_async_copy(ex.src(w, in_refs, None), block(x, y, c), local_sems.at[w])

    def own(k):
        return copy(k, block(x, y, c), sibling if k == 0 else (*chips[k - 1], c))

    def came(j):
        return copy(1 + j, block(*chips[j], c), (*chips[j], c))

    def passed(j):
        return copy(4 + j, block(*chips[j], c), sibling, src=block(*chips[j], c))

    def from_sibling(k):
        return copy(k, block(x, y, 1 - c) if k == 0 else block(*chips[k - 4], 1 - c), sibling)

    start = [lambda: local().start()] + [lambda k=k: own(k).start() for k in range(4)]
    relay = [step for j in js for step in (lambda j=j: came(j).wait_recv(), lambda j=j: passed(j).start())]
    finish = ([lambda: local().wait()] + [lambda k=k: own(k).wait_send() for k in range(4)]
              + [lambda j=j: passed(j).wait_send() for j in js]
              + [lambda k=k: from_sibling(k).wait_recv() for k in (0, 4, 5, 6)])
    return start, relay, finish


def _exchange_steps(ex, in_refs, out_refs, sems):
    start, relay, finish = [], [], []
    for w in range(len(ex.arrays)):
        steps = (_relayed_steps if ex.relayed[w] else _direct_steps)(ex, w, in_refs, out_refs, sems)
        start += steps[0]
        relay += steps[1]
        finish += steps[2]
    return start, relay, finish


def _run(steps):
    for step in steps:
        step()


def _at_grid_step(grid, where):
    target = {"first": [0] * len(grid), "middle": [grid[0] // 2] + [0] * (len(grid) - 1),
              "last": [g - 1 for g in grid]}[where]
    hit = pl.program_id(0) == target[0]
    for axis in range(1, len(grid)):
        hit = jnp.logical_and(hit, pl.program_id(axis) == target[axis])
    return hit


def _exchange_start(ex, in_refs, out_refs, sems, grid):
    if ex.arrays:
        @pl.when(_at_grid_step(grid, "first"))
        def _():
            _run(_exchange_steps(ex, in_refs, out_refs, sems)[0])

        if any(ex.relayed):
            assert grid[0] >= 2

            @pl.when(_at_grid_step(grid, "middle"))
            def _():
                _run(_exchange_steps(ex, in_refs, out_refs, sems)[1])


def _exchange_wait(ex, in_refs, out_refs, sems, grid):
    if ex.arrays:
        @pl.when(_at_grid_step(grid, "last"))
        def _():
            _run(_exchange_steps(ex, in_refs, out_refs, sems)[2])


def _shard_block(ref, shard_shape, by_col, idx):
    r, c = shard_shape
    if by_col:
        return ref.at[:, pl.ds(pl.multiple_of(idx * c, LANES), c)]
    return ref.at[pl.ds(pl.multiple_of(idx * r, 16), r), :]


def _full_shape(shard_shape, by_col):
    r, c = shard_shape
    return (r, N_DEV * c) if by_col else (N_DEV * r, c)


def _gather_exchange(shards, col_sharded):
    shapes = [a.shape for a in shards]
    return _Exchange(
        shards, [jax.ShapeDtypeStruct(_full_shape(sh, bc), a.dtype) for a, sh, bc in zip(shards, shapes, col_sharded)],
        lambda w, refs, idx: refs[w],
        lambda w, refs, idx: _shard_block(refs[w], shapes[w], col_sharded[w], idx), [True] * len(shards))


def _scatter_exchange(grads, col_sharded):
    shapes = []
    for g, by_col in zip(grads, col_sharded):
        r, c = g.shape
        shapes.append((r, c // N_DEV) if by_col else (r // N_DEV, c))
    return _Exchange(
        grads, [jax.ShapeDtypeStruct((N_DEV,) + sh, g.dtype) for g, sh in zip(grads, shapes)],
        lambda w, refs, idx: _shard_block(refs[w], shapes[w], col_sharded[w], idx),
        lambda w, refs, mine: refs[w].at[mine])


def _broadcast_exchange(arrays):
    return _Exchange(arrays, [jax.ShapeDtypeStruct((N_DEV,) + a.shape, a.dtype) for a in arrays],
                     lambda w, refs, idx: refs[w], lambda w, refs, mine: refs[w].at[mine])


def _join(*exs):
    arrays, shapes, owner = [], [], []
    for e in exs:
        for w in range(len(e.arrays)):
            owner.append((e, w, len(arrays), len(shapes)))
        arrays += e.arrays
        shapes += e.out_shapes

    def src(w, refs, idx):
        e, w0, i0, _ = owner[w]
        return e.src(w0, refs[i0:i0 + len(e.arrays)], idx)

    def dst(w, refs, idx):
        e, w0, _, o0 = owner[w]
        return e.dst(w0, refs[o0:o0 + len(e.out_shapes)], idx)

    return _Exchange(arrays, shapes, src, dst, [flag for e in exs for flag in e.relayed])


def _exchange_call(ex, name):
    n_in = len(ex.arrays)

    def body(*refs):
        in_refs, _, out_refs, sems = _split_refs(refs, ex, 0)
        for steps in _exchange_steps(ex, in_refs, out_refs, sems):
            _run(steps)

    return pl.pallas_call(
        body, name=name, in_specs=[ANY] * n_in, out_specs=[ANY] * len(ex.out_shapes), out_shape=ex.out_shapes,
        scratch_shapes=_exchange_sems(ex), compiler_params=pltpu.CompilerParams(vmem_limit_bytes=VMEM_LIMIT),
    )(*ex.arrays)


def _to_bf16(arrays):
    def body(*refs):
        for src, dst in zip(refs[:len(arrays)], refs[len(arrays):]):
            dst[...] = src[...].astype(BF16)

    vmem = pl.BlockSpec(memory_space=pltpu.VMEM)
    return pl.pallas_call(
        body, name="weights_to_bf16", in_specs=[vmem] * len(arrays), out_specs=[vmem] * len(arrays),
        out_shape=[jax.ShapeDtypeStruct(a.shape, BF16) for a in arrays],
        compiler_params=pltpu.CompilerParams(vmem_limit_bytes=VMEM_LIMIT),
    )(*arrays)


def _adamw(w, g, m, v):
    m = ADAM_B1 * m + (1.0 - ADAM_B1) * g
    v = ADAM_B2 * v + (1.0 - ADAM_B2) * jnp.square(g)
    m_hat = m / (1.0 - ADAM_B1 ** ADAM_STEP)
    v_hat = v / (1.0 - ADAM_B2 ** ADAM_STEP)
    delta = -ADAM_LR * (m_hat / (jnp.sqrt(v_hat) + ADAM_EPS) + ADAM_WD * w)
    return delta, m, v


def _sum_and_adamw(parts, w, m, v, name):
    r, c = w.shape
    tr = min(r, 256)

    def body(p_ref, w_ref, m_ref, v_ref, g_out, d_out, m_out, v_out):
        g = p_ref[0].astype(F32)
        for dev in range(1, N_DEV):
            g = g + p_ref[dev].astype(F32)
        g_out[...] = g
        d_out[...], m_out[...], v_out[...] = _adamw(w_ref[...], g, m_ref[...], v_ref[...])

    blk = pl.BlockSpec((tr, c), lambda i: (i, 0))
    return pl.pallas_call(
        body, name=name, grid=(r // tr,),
        in_specs=[pl.BlockSpec((N_DEV, tr, c), lambda i: (0, i, 0)), blk, blk, blk],
        out_specs=[blk] * 4, out_shape=[jax.ShapeDtypeStruct((r, c), F32)] * 4,
        compiler_params=_params(1),
    )(parts, w, m, v)


BIG = ("w_in", "w_attn_out", "w_conv_out", "w_o", "w_up", "w_down", "w_ple_gate", "w_ple_proj")
COL_SHARDED = {"w_in": True, "w_attn_out": True, "w_conv_out": True, "w_o": False, "w_up": True, "w_down": False,
               "w_ple_gate": False, "w_ple_proj": True}
SMALL = ("g_pre_mix", "b_gate", "g_post_mix", "g_pre_mlp", "g_post_mlp", "g_ple")


REST = BIG[1:]


def _local_grads(x, p, target, small, wconv, full, aw, cw, tm, t, gather_rest=None, scatter_rest=None,
                 scatter_in=None):
    full = dict(full)
    tm_wide = min(WIDE_BLOCKS * tm, x.shape[0])
    qkv, conv, gate, h1 = _in_proj_fwd(x, small["g_pre_mix"], small["b_gate"], full["w_in"], aw, cw, tm_wide)
    o, *rest = _attn_fwd(qkv, aw, t, gather_rest)
    full.update(zip(REST, rest))
    x1, mixed, mix_in, conv_in = _mix_fwd(x, o, conv, gate, wconv, small["g_post_mix"], full["w_attn_out"],
                                          full["w_conv_out"], full["w_o"], tm_wide)
    (dx1, h2, du, a, df, h3, ds3, dpp, loss, dg_pre_mlp, dg_post_mlp, dg_ple) = _mlp_ple_loss(
        x1, p, target, small["g_pre_mlp"], small["g_post_mlp"], small["g_ple"], full["w_up"], full["w_down"],
        full["w_ple_gate"], full["w_ple_proj"], tm)
    big = {"w_up": _weight_grad(h2, du, "dw_up"), "w_down": _weight_grad(a, df, "dw_down"),
           "w_ple_gate": _weight_grad(h3, ds3, "dw_ple_gate"), "w_ple_proj": _weight_grad(p, dpp, "dw_ple_proj")}
    (dmixed, dattn, dconvout, do, drest, dg_post_mix, db_gate, dwconv) = _mix_bwd(
        dx1, mixed, o, conv, gate, wconv, small["g_post_mix"], full["w_attn_out"], full["w_conv_out"], full["w_o"],
        tm_wide)
    big.update({"w_attn_out": _weight_grad(o, dattn, "dw_attn_out"),
                "w_conv_out": _weight_grad(conv_in, dconvout, "dw_conv_out"),
                "w_o": _weight_grad(mix_in, dmixed, "dw_o")})
    dq, dk, dv, *scattered = _attn_bwd(qkv, o, do, aw, t, scatter_rest and scatter_rest([big[n] for n in REST]))
    pieces = [dq, dk, dv, drest]
    dw_in, col0, ni = None, 0, full["w_in"].shape[1]
    for i, piece in enumerate(pieces):
        dw_in = _weight_grad(h1, piece, "dw_in_%d" % i, dw_in, col0, ni)
        col0 += piece.shape[1]
    big["w_in"] = dw_in
    dx, dg_pre_mix, *scattered_in = _in_proj_bwd(x, dx1, pieces, small["g_pre_mix"], full["w_in"], tm_wide,
                                                scatter_in and scatter_in(dw_in))
    small_grads = {"g_pre_mix": dg_pre_mix, "b_gate": db_gate, "g_post_mix": dg_post_mix, "g_pre_mlp": dg_pre_mlp,
                   "g_post_mlp": dg_post_mlp, "g_ple": dg_ple, "w_conv": dwconv}
    return loss[0, 0], dx, big, small_grads, scattered_in + scattered


PACK_ROWS = 16


def _pack_layout(shapes, d):
    slots, at = [], 0
    for i, (r, c) in enumerate(shapes):
        assert d % c == 0
        for row in range(r):
            slots.append((i, row, at // d, at % d))
            at += c
        at = -(-at // d) * d
    assert at <= PACK_ROWS * d
    return slots


def _pack_small(groups, d):
    shapes = [a.shape for a in groups[0]]
    slots = _pack_layout(shapes, d)
    n = len(shapes)

    def body(*refs):
        ins, outs = refs[:n * len(groups)], refs[n * len(groups):]
        for g, out in enumerate(outs):
            out[...] = jnp.zeros_like(out)
            for i, row, pr, pc in slots:
                src = ins[g * n + i]
                out[pr:pr + 1, pc:pc + shapes[i][1]] = src[row:row + 1, :]

    vmem = pl.BlockSpec(memory_space=pltpu.VMEM)
    return pl.pallas_call(
        body, name="pack_small", in_specs=[vmem] * (n * len(groups)), out_specs=[vmem] * len(groups),
        out_shape=[jax.ShapeDtypeStruct((PACK_ROWS, d), F32)] * len(groups),
    )(*[a for group in groups for a in group])


def _unpack_small(pack, shapes, d):
    slots = _pack_layout(shapes, d)
    return [jnp.stack([pack[pr, pc:pc + shapes[i][1]] for j, row, pr, pc in slots if j == i])
            for i in range(len(shapes))]


def kernel(x, p, g_pre_mix, w_in, b_gate, w_conv, w_attn_out, w_conv_out, w_o, g_post_mix, g_pre_mlp, w_up, w_down, g_post_mlp, g_ple, w_ple_gate, w_ple_proj, loss_target, m_g_pre_mix, m_w_in, m_b_gate, m_w_conv, m_w_attn_out, m_w_conv_out, m_w_o, m_g_post_mix, m_g_pre_mlp, m_w_up, m_w_down, m_g_post_mlp, m_g_ple, m_w_ple_gate, m_w_ple_proj, v_g_pre_mix, v_w_in, v_b_gate, v_w_conv, v_w_attn_out, v_w_conv_out, v_w_o, v_g_post_mix, v_g_pre_mlp, v_w_up, v_w_down, v_g_post_mlp, v_g_ple, v_w_ple_gate, v_w_ple_proj):
    given = dict(locals())
    order = ["g_pre_mix", "w_in", "b_gate", "w_conv", "w_attn_out", "w_conv_out", "w_o", "g_post_mix", "g_pre_mlp",
             "w_up", "w_down", "g_post_mlp", "g_ple", "w_ple_gate", "w_ple_proj"]
    d = x.shape[-1]
    me = 4 * lax.axis_index("x") + 2 * lax.axis_index("y") + lax.axis_index("c")

    col = [COL_SHARDED[n] for n in BIG]
    shards = _to_bf16([given[n][0] for n in BIG])
    cw_shard = w_conv.shape[-1]
    conv_tile = jnp.pad(w_conv[0], ((0, HALO - CONV_K), (0, LANES - cw_shard)))
    w_in_full, conv_g = _exchange_call(
        _join(_gather_exchange(shards[:1], col[:1]), _broadcast_exchange([conv_tile])), "gather_w_in")
    wconv = jnp.concatenate([conv_g[dev, :CONV_K, :cw_shard] for dev in range(N_DEV)], axis=1)

    small = {n: given[n] for n in SMALL}
    loss, dx, big_grads, small_grads, parts = _local_grads(
        x[0], p[0, 0], loss_target[0], small, wconv, {"w_in": w_in_full}, w_attn_out.shape[1], w_conv_out.shape[1],
        ROW_BLOCK, ATTN_BLOCK,
        _gather_exchange(shards[1:], col[1:]), lambda grads: _scatter_exchange(grads, col[1:]),
        lambda grad: _scatter_exchange([grad], col[:1]))
    small_names = list(SMALL) + ["w_conv"]
    two_d = lambda a: a.reshape(-1, d) if a.shape[-1] > d else a.reshape(-1, a.shape[-1])
    full_conv = lambda a: lax.dynamic_update_slice(jnp.zeros((CONV_K, N_DEV * cw_shard), F32), a[0],
                                                   (jnp.int32(0), me * cw_shard))
    groups = [[two_d(small_grads[n]) for n in small_names] + [loss.reshape(1, 1)]]
    for pre in ("", "m_", "v_"):
        groups.append([two_d(given[pre + n]) for n in SMALL] + [full_conv(given[pre + "w_conv"]), jnp.zeros((1, 1), F32)])
    pack, *state = _pack_small(groups, d)
    packs, = _exchange_call(_broadcast_exchange([pack]), "share_small_grads")

    grads, deltas, new_m, new_v = {}, {}, {}, {}
    for n, part in zip(BIG, parts):
        grads[n], deltas[n], new_m[n], new_v[n] = (
            a[None] for a in _sum_and_adamw(part, given[n][0], given["m_" + n][0], given["v_" + n][0], "adamw_" + n))

    outs = _sum_and_adamw(packs, *state, "adamw_small")
    shapes = [a.shape for a in groups[0]]
    for res, dst in zip(outs, (grads, deltas, new_m, new_v)):
        for n, a in zip(small_names + ["loss"], _unpack_small(res, shapes, d)):
            if n == "w_conv":
                a = lax.dynamic_slice(a, (jnp.int32(0), me * cw_shard), (CONV_K, cw_shard))[None]
            dst[n] = a.reshape(given[n].shape) if n in SMALL else a
    loss = grads["loss"][0, 0]

    return (loss, dx[None], *[grads[n] for n in order], *[deltas[n] for n in order],
            *[new_m[n] for n in order], *[new_v[n] for n in order])
```

```python
import jax
import jax.numpy as jnp
from jax import lax
from jax.experimental import pallas as pl
from jax.experimental.pallas import tpu as pltpu

F32 = jnp.float32
BF16 = jnp.bfloat16
RMS_EPS = 1e-6
N_DEV = 8
MESH_AXES = ("x", "y", "c")
LANES = 128
HEAD_DIM = 64
HEADS_PER_GROUP = LANES // HEAD_DIM
CONV_K = 3
HALO = 8
HALO_BF16 = 16
VMEM_LIMIT = 56 * 1024 * 1024
EXP2_ZERO = -150.0
LOG2_E = 1.4426950408889634

ADAM_LR = 0.001
ADAM_B1 = 0.9
ADAM_B2 = 0.999
ADAM_EPS = 1e-08
ADAM_WD = 0.01
ADAM_STEP = 10

ROW_BLOCK = 256
WIDE_BLOCKS = 2
ATTN_BLOCK = 256
ATTN_ROW_SPLITS = 2
DW_TOKENS = 4096
DW_TILE = 1024
DW_PIECE_TILE = 512
FF_CHUNK = 1024
PROJ_CHUNK = 512


def _dot(a, b):
    return lax.dot_general(a, b, (((1,), (0,)), ((), ())), preferred_element_type=F32)


def _dot_nt(a, b):
    return lax.dot_general(a, b, (((1,), (1,)), ((), ())), preferred_element_type=F32)


def _dot_tn(a, b):
    return lax.dot_general(a, b, (((0,), (0,)), ((), ())), preferred_element_type=F32)


def _sigmoid(z):
    return 1.0 / (1.0 + jnp.exp(-z))


def _rms_scale(x):
    return lax.rsqrt(jnp.mean(x * x, axis=-1, keepdims=True) + RMS_EPS)


def _rms_bwd(xhat, r, g, dy):
    gd = dy * g
    return r * (gd - xhat * jnp.mean(gd * xhat, axis=-1, keepdims=True))


def _params(n_axes, **kw):
    return pltpu.CompilerParams(dimension_semantics=("arbitrary",) * n_axes, vmem_limit_bytes=VMEM_LIMIT, **kw)


def _load_resident(pairs, sem):
    @pl.when(pl.program_id(0) == 0)
    def _():
        copies = [pltpu.make_async_copy(src, dst, sem.at[i]) for i, (src, dst) in enumerate(pairs)]
        for cp in copies:
            cp.start()
        for cp in copies:
            cp.wait()


def _row_spec(tm, width):
    return pl.BlockSpec((tm, width), lambda i: (i, 0))


def _prev_halo_spec(tm, width, rows):
    per = tm // rows
    return pl.BlockSpec((rows, width), lambda i: (jnp.maximum(i * per - 1, 0), 0))


def _const_spec(shape):
    return pl.BlockSpec(shape, lambda i: (0,) * len(shape))


ANY = pl.BlockSpec(memory_space=pl.ANY)


def _shift_down(cur, prev, n):
    rows = lax.broadcasted_iota(jnp.int32, cur.shape, 0)
    out = pltpu.roll(cur, n, 0)
    for j in range(n):
        out = jnp.where(rows == j, prev[prev.shape[0] - n + j:prev.shape[0] - n + j + 1, :], out)
    return out


def _shift_up(cur, nxt, n):
    tm = cur.shape[0]
    rows = lax.broadcasted_iota(jnp.int32, cur.shape, 0)
    out = pltpu.roll(cur, tm - n, 0)
    for j in range(n):
        out = jnp.where(rows == tm - n + j, nxt[j:j + 1, :], out)
    return out


def _conv_taps(cm, cm_prev, wconv):
    cm1 = _shift_down(cm, cm_prev, 1)
    cm2 = _shift_down(cm, cm_prev, 2)
    cv = wconv[2:3, :] * cm + wconv[1:2, :] * cm1 + wconv[0:1, :] * cm2
    return cv, cm1, cm2


def _in_proj_fwd(x, g1, b_gate, w_in, aw, cw, tm):
    s, d = x.shape
    ni = w_in.shape[1]
    n_qkv, n_conv = 3 * aw, 3 * cw
    ch = PROJ_CHUNK

    def body(x_ref, g_ref, b_ref, w_hbm, qkv_ref, conv_ref, gate_ref, h_ref, w_vmem, sem):
        _load_resident([(w_hbm, w_vmem)], sem)
        xv = x_ref[...]
        h = (xv * _rms_scale(xv) * g_ref[...]).astype(BF16)
        h_ref[...] = h
        for c0 in range(0, ni, ch):
            pc = _dot(h, w_vmem[:, c0:c0 + ch])
            if c0 < n_qkv:
                qkv_ref[:, c0:c0 + ch] = pc.astype(BF16)
            elif c0 < n_qkv + n_conv:
                conv_ref[:, c0 - n_qkv:c0 - n_qkv + ch] = pc.astype(BF16)
            else:
                g0 = c0 - n_qkv - n_conv
                gate_ref[:, g0:g0 + ch] = _sigmoid(pc + b_ref[:, g0:g0 + ch]).astype(BF16)

    return pl.pallas_call(
        body, name="in_proj_fwd", grid=(s // tm,),
        in_specs=[_row_spec(tm, d), _const_spec((1, d)), _const_spec((1, 2 * d)), ANY],
        out_specs=[_row_spec(tm, n_qkv), _row_spec(tm, n_conv), _row_spec(tm, 2 * d), _row_spec(tm, d)],
        out_shape=[jax.ShapeDtypeStruct((s, n_qkv), BF16), jax.ShapeDtypeStruct((s, n_conv), BF16),
                   jax.ShapeDtypeStruct((s, 2 * d), BF16), jax.ShapeDtypeStruct((s, d), BF16)],
        scratch_shapes=[pltpu.VMEM((d, ni), BF16), pltpu.SemaphoreType.DMA((1,))],
        compiler_params=_params(1),
    )(x, g1, b_gate, w_in)


def _split_hi_lo(a):
    hi = a.astype(BF16)
    return hi, (a - hi.astype(F32)).astype(BF16)


def _log2_gates(z):
    z2 = z * LOG2_E
    nz2 = -z2
    log_keep = jnp.minimum(nz2, 0.0) - jnp.log2(1.0 + jnp.exp2(jnp.minimum(z2, nz2)))
    return log_keep + z2, log_keep


def _attn_masks(t):
    row = lax.broadcasted_iota(jnp.int32, (t, t), 0)
    col = lax.broadcasted_iota(jnp.int32, (t, t), 1)
    return (col < row).astype(F32), (row > col).astype(BF16), (row >= col).astype(BF16)


def _chains(a):
    tr = a.shape[0] // ATTN_ROW_SPLITS
    return [jnp.where(_head_lanes(h), a[r * tr:(r + 1) * tr], jnp.zeros((tr, LANES), a.dtype))
            for h in range(HEADS_PER_GROUP) for r in range(ATTN_ROW_SPLITS)]


def _merge_chains(parts):
    rows = []
    for r in range(ATTN_ROW_SPLITS):
        out = parts[r]
        for h in range(1, HEADS_PER_GROUP):
            out = jnp.where(_head_lanes(h), parts[h * ATTN_ROW_SPLITS + r], out)
        rows.append(out)
    return jnp.concatenate(rows, axis=0)


def _by_stage(n_chains, stages):
    for stage in stages:
        for c in range(n_chains):
            stage(c)


def _row_parts(a):
    tr = a.shape[0] // ATTN_ROW_SPLITS
    return [a[r * tr:(r + 1) * tr] for r in range(ATTN_ROW_SPLITS)]


def _while_weights_live(qi, block, carry):
    def cond(state):
        j, carry = state
        live = jnp.max(carry[0][0])
        for run in carry[0][1:]:
            live = jnp.maximum(live, jnp.max(run))
        return jnp.logical_and(j < qi, live >= EXP2_ZERO)

    def step(state):
        j, carry = state
        return j + 1, block(qi - 1 - j, carry)

    return lax.while_loop(cond, step, (jnp.int32(0), carry))[1]


def _head_lanes(h):
    lane = lax.broadcasted_iota(jnp.int32, (1, LANES), 1)
    return (lane >= HEAD_DIM * h) & (lane < HEAD_DIM * (h + 1))


def _attn_fwd(qkv, aw, t, exchange=None):
    s = qkv.shape[0]
    groups = aw // LANES
    nq = s // t
    scale = HEAD_DIM ** -0.5
    ex = exchange or _NO_EXCHANGE
    causal, upper, _ = _attn_masks(t)
    mask_spec = pl.BlockSpec((t, t), lambda g, i: (0, 0))

    def body(q_ref, k_ref, v_ref, causal_ref, upper_ref, *rest):
        ex_in, (o_ref,), ex_out, sems = _split_refs(rest, ex, 1)
        qi = pl.program_id(1)
        _exchange_start(ex, ex_in, ex_out, sems, (groups, nq))
        upper = upper_ref[...]
        causal = _row_parts(causal_ref[...] > 0.5) * HEADS_PER_GROUP
        qs = _chains(q_ref[...] * scale)
        heads = range(len(qs))
        tr = t // ATTN_ROW_SPLITS

        def block(kb, runs, accs, diag):
            rows = pl.ds(pl.multiple_of(kb * t, t), t)
            k = k_ref[rows, :]
            v = v_ref[rows, :]
            ncs = [(h % ATTN_ROW_SPLITS + 1) * tr if diag else t for h in heads]
            live = [{} for _ in heads]
            new_runs, new_accs = [None] * len(heads), [None] * len(heads)

            def scores(h):
                live[h]["z"] = _dot_nt(qs[h], k[0:ncs[h]])

            def gates(h):
                nc = ncs[h]
                log_b, log_keep = _log2_gates(live[h].pop("z"))
                if diag:
                    log_keep = jnp.where(causal[h][:, 0:nc], log_keep, 0.0)
                hi, lo = _split_hi_lo(log_keep)
                live[h]["log_w"] = log_b + runs[h]
                live[h]["between"] = _dot(hi, upper[0:nc, 0:nc]) + _dot(lo, upper[0:nc, 0:nc])
                new_runs[h] = runs[h] + jnp.sum(log_keep, axis=1, keepdims=True)

            def weights(h):
                nc = ncs[h]
                w = jnp.exp2(live[h].pop("log_w") + live[h].pop("between"))
                if diag:
                    w = jnp.where(causal[h][:, 0:nc], w, 0.0)
                new_accs[h] = accs[h] + _dot(w.astype(BF16), v[0:nc])

            _by_stage(len(heads), [scores, gates, weights])
            return tuple(new_runs), tuple(new_accs)

        carry = block(qi, [jnp.zeros((tr, 1), F32)] * len(heads), [jnp.zeros((tr, LANES), F32)] * len(heads), True)
        _, accs = _while_weights_live(qi, lambda kb, carry: block(kb, *carry, False), carry)
        o_ref[...] = _merge_chains(accs)
        _exchange_wait(ex, ex_in, ex_out, sems, (groups, nq))

    return pl.pallas_call(
        body, name="attn_fwd", grid=(groups, nq),
        in_specs=[pl.BlockSpec((t, LANES), lambda g, i: (i, g)),
                  pl.BlockSpec((s, LANES), lambda g, i: (0, groups + g)),
                  pl.BlockSpec((s, LANES), lambda g, i: (0, 2 * groups + g)), mask_spec, mask_spec]
        + [ANY] * len(ex.arrays),
        out_specs=[pl.BlockSpec((t, LANES), lambda g, i: (i, g))] + [ANY] * len(ex.out_shapes),
        out_shape=[jax.ShapeDtypeStruct((s, aw), F32)] + ex.out_shapes,
        scratch_shapes=_exchange_sems(ex),
        compiler_params=_params(2),
    )(qkv, qkv, qkv, causal, upper, *ex.arrays)


def _attn_bwd(qkv, o, do, aw, t, exchange=None):
    s = qkv.shape[0]
    groups = aw // LANES
    nq = s // t
    scale = HEAD_DIM ** -0.5
    ex = exchange or _NO_EXCHANGE

    def body(q_ref, k_ref, v_ref, o_ref, do_ref, causal_ref, upper_ref, lower_ref, *rest):
        ex_in, (dq_ref, dk_ref, dv_ref), ex_out, (dk_acc, dv_acc, *sems) = _split_refs(rest, ex, 3)
        qi = pl.program_id(1)
        _exchange_start(ex, ex_in, ex_out, sems, (groups, nq))

        @pl.when(qi == 0)
        def _():
            dk_acc[...] = jnp.zeros_like(dk_acc)
            dv_acc[...] = jnp.zeros_like(dv_acc)

        upper = upper_ref[...]
        lower_incl = lower_ref[...]
        causal = _row_parts(causal_ref[...] > 0.5) * HEADS_PER_GROUP
        q = q_ref[...] * scale
        do_b = do_ref[...]
        qs = _chains(q)
        dos = _chains(do_b)
        qs_all = jnp.concatenate(qs, axis=0)
        dos_all = jnp.concatenate(dos, axis=0)
        e_totals = [jnp.sum(part, axis=1, keepdims=True) for part in _chains(do_b.astype(F32) * o_ref[...])]
        heads = range(len(qs))
        tr = t // ATTN_ROW_SPLITS

        def block(kb, runs, e_runs, dqs, diag):
            rows = pl.ds(pl.multiple_of(kb * t, t), t)
            k = k_ref[rows, :]
            v = v_ref[rows, :]
            ncs = [(h % ATTN_ROW_SPLITS + 1) * tr if diag else t for h in heads]
            live = [{} for _ in heads]
            none = [None] * len(heads)
            new_runs, new_e_runs, new_dqs, dzbs, wbs = list(none), list(none), list(none), list(none), list(none)

            def scores(h):
                live[h]["z"] = _dot_nt(qs[h], k[0:ncs[h]])
                live[h]["dw"] = _dot_nt(dos[h], v[0:ncs[h]])

            def gates(h):
                nc = ncs[h]
                log_b, log_keep = _log2_gates(live[h].pop("z"))
                live[h]["beta"] = jnp.exp2(log_b)
                live[h]["keep"] = jnp.exp2(log_keep)
                if diag:
                    log_keep = jnp.where(causal[h][:, 0:nc], log_keep, 0.0)
                hi, lo = _split_hi_lo(log_keep)
                live[h]["log_w"] = log_b + runs[h]
                live[h]["between"] = _dot(hi, upper[0:nc, 0:nc]) + _dot(lo, upper[0:nc, 0:nc])
                new_runs[h] = runs[h] + jnp.sum(log_keep, axis=1, keepdims=True)

            def weights(h):
                nc = ncs[h]
                w = jnp.exp2(live[h].pop("log_w") + live[h].pop("between"))
                if diag:
                    w = jnp.where(causal[h][:, 0:nc], w, 0.0)
                wb = w.astype(BF16)
                e = live[h].pop("dw") * wb.astype(F32)
                hi, lo = _split_hi_lo(e)
                live[h]["e"] = e
                live[h]["e_suffix"] = _dot(hi, lower_incl[0:nc, 0:nc]) + _dot(lo, lower_incl[0:nc, 0:nc]) + e_runs[h]
                wbs[h] = wb

            def score_grads(h):
                nc = ncs[h]
                e_suffix = live[h].pop("e_suffix")
                dz = live[h].pop("e") * live[h].pop("keep") - (e_totals[h] - e_suffix) * live[h].pop("beta")
                if diag:
                    dz = jnp.where(causal[h][:, 0:nc], dz, 0.0)
                dzb = dz.astype(BF16)
                new_dqs[h] = dqs[h] + _dot(dzb, k[0:nc])
                new_e_runs[h] = e_suffix[:, 0:1]
                if nc < t:
                    unseen = jnp.zeros((tr, t - nc), BF16)
                    dzb = jnp.concatenate([dzb, unseen], axis=1)
                    wbs[h] = jnp.concatenate([wbs[h], unseen], axis=1)
                dzbs[h] = dzb

            _by_stage(len(heads), [scores, gates, weights, score_grads])
            dk_acc[rows, :] += _dot_tn(jnp.concatenate(dzbs, axis=0), qs_all)
            dv_acc[rows, :] += _dot_tn(jnp.concatenate(wbs, axis=0), dos_all)
            return tuple(new_runs), tuple(new_e_runs), tuple(new_dqs)

        zero_cols = [jnp.zeros((tr, 1), F32)] * len(heads)
        carry = block(qi, zero_cols, zero_cols, [jnp.zeros((tr, LANES), F32)] * len(heads), True)
        _, _, dqs = _while_weights_live(qi, lambda kb, carry: block(kb, *carry, False), carry)
        dq_ref[...] = (_merge_chains(dqs) * scale).astype(BF16)

        @pl.when(qi == nq - 1)
        def _():
            dk_ref[...] = dk_acc[...].astype(BF16)
            dv_ref[...] = dv_acc[...].astype(BF16)

        _exchange_wait(ex, ex_in, ex_out, sems, (groups, nq))

    blk = pl.BlockSpec((t, LANES), lambda g, i: (i, g))
    slab = pl.BlockSpec((s, LANES), lambda g, i: (0, g))
    mask_spec = pl.BlockSpec((t, t), lambda g, i: (0, 0))
    return pl.pallas_call(
        body, name="attn_bwd", grid=(groups, nq),
        in_specs=[blk, pl.BlockSpec((s, LANES), lambda g, i: (0, groups + g)),
                  pl.BlockSpec((s, LANES), lambda g, i: (0, 2 * groups + g)), blk, blk, mask_spec, mask_spec, mask_spec]
        + [ANY] * len(ex.arrays),
        out_specs=[blk, slab, slab] + [ANY] * len(ex.out_shapes),
        out_shape=[jax.ShapeDtypeStruct((s, aw), BF16)] * 3 + ex.out_shapes,
        scratch_shapes=[pltpu.VMEM((s, LANES), F32), pltpu.VMEM((s, LANES), F32)] + _exchange_sems(ex),
        compiler_params=_params(2),
    )(qkv, qkv, qkv, o, do, *_attn_masks(t), *ex.arrays)


def _branches(o_b, conv, conv_prev, wconv, w_ao, w_co, cw, first):
    conv = conv.astype(F32)
    conv_prev = conv_prev.astype(F32)
    cb = conv[:, 0:cw]
    cm = conv[:, cw:2 * cw] * conv[:, 2 * cw:3 * cw]
    cm_prev = conv_prev[:, cw:2 * cw] * conv_prev[:, 2 * cw:3 * cw]
    cm_prev = jnp.where(first, 0.0, cm_prev)
    cv, cm1, cm2 = _conv_taps(cm, cm_prev, wconv)
    conv_in = (cb * cv).astype(BF16)
    return _dot(o_b, w_ao), _dot(conv_in, w_co), conv_in, cb, cv, cm, cm1, cm2


def _mix_fwd(x, o, conv, gate, wconv, g_post, w_ao, w_co, w_o, tm):
    s, d = x.shape
    aw, cw = w_ao.shape[0], w_co.shape[0]

    def body(x_ref, o_ref, conv_ref, prev_ref, gate_ref, wc_ref, g_ref, wao_hbm, wco_hbm, wo_hbm,
             x1_ref, mixed_ref, mixin_ref, convin_ref, wao, wco, wo, sem):
        _load_resident([(wao_hbm, wao), (wco_hbm, wco), (wo_hbm, wo)], sem)
        y_attn, y_conv, conv_in, *_ = _branches(
            o_ref[...].astype(BF16), conv_ref[...], prev_ref[...], wc_ref[...], wao[...], wco[...], cw,
            pl.program_id(0) == 0)
        mix_in = (gate_ref[:, 0:d].astype(F32) * y_attn + gate_ref[:, d:2 * d].astype(F32) * y_conv).astype(BF16)
        mixed = _dot(mix_in, wo[...])
        x1_ref[...] = x_ref[...] + mixed * _rms_scale(mixed) * g_ref[...]
        mixed_ref[...] = mixed
        mixin_ref[...] = mix_in
        convin_ref[...] = conv_in

    return pl.pallas_call(
        body, name="mix_fwd", grid=(s // tm,),
        in_specs=[_row_spec(tm, d), _row_spec(tm, aw), _row_spec(tm, 3 * cw), _prev_halo_spec(tm, 3 * cw, HALO_BF16),
                  _row_spec(tm, 2 * d), _const_spec((CONV_K, cw)), _const_spec((1, d)), ANY, ANY, ANY],
        out_specs=[_row_spec(tm, d), _row_spec(tm, d), _row_spec(tm, d), _row_spec(tm, cw)],
        out_shape=[jax.ShapeDtypeStruct((s, d), F32), jax.ShapeDtypeStruct((s, d), F32),
                   jax.ShapeDtypeStruct((s, d), BF16), jax.ShapeDtypeStruct((s, cw), BF16)],
        scratch_shapes=[pltpu.VMEM(w_ao.shape, BF16), pltpu.VMEM(w_co.shape, BF16), pltpu.VMEM(w_o.shape, BF16),
                        pltpu.SemaphoreType.DMA((3,))],
        compiler_params=_params(1),
    )(x, o, conv, conv, gate, wconv, g_post, w_ao, w_co, w_o)


def _mix_bwd(dx1, mixed, o, conv, gate, wconv, g_post, w_ao, w_co, w_o, tm):
    s, d = dx1.shape
    aw, cw = w_ao.shape[0], w_co.shape[0]
    n = s // tm
    per = tm // HALO_BF16

    def body(dx1_ref, mixed_ref, o_ref, conv_ref, prev_ref, gate_ref, wc_ref, g_ref, wao_hbm, wco_hbm, wo_hbm,
             dmixed_ref, dattn_ref, dconvout_ref, do_ref, drest_ref, dg_ref, dbias_ref, dwc_ref,
             wao, wco, wo, dcv_next, sem):
        i = pl.program_id(0)
        _load_resident([(wao_hbm, wao), (wco_hbm, wco), (wo_hbm, wo)], sem)

        @pl.when(i == 0)
        def _():
            dg_ref[...] = jnp.zeros_like(dg_ref)
            dbias_ref[...] = jnp.zeros_like(dbias_ref)
            dwc_ref[...] = jnp.zeros_like(dwc_ref)
            dcv_next[...] = jnp.zeros_like(dcv_next)

        mixed = mixed_ref[...]
        r = _rms_scale(mixed)
        mhat = mixed * r
        dn = dx1_ref[...]
        dg_ref[...] += jnp.sum(dn * mhat, axis=0, keepdims=True)
        dmixed = _rms_bwd(mhat, r, g_ref[...], dn).astype(BF16)
        dmixed_ref[...] = dmixed
        dmi = _dot_nt(dmixed, wo[...])

        wc = wc_ref[...]
        conv = conv_ref[...].astype(F32)
        y_attn, y_conv, _, cb, cv, cm, cm1, cm2 = _branches(
            o_ref[...].astype(BF16), conv, prev_ref[...], wc, wao[...], wco[...], cw, i == n - 1)
        ga = gate_ref[:, 0:d].astype(F32)
        gc = gate_ref[:, d:2 * d].astype(F32)
        dpre_a = dmi * y_attn * ga * (1.0 - ga)
        dpre_c = dmi * y_conv * gc * (1.0 - gc)
        drest_ref[:, 3 * cw:3 * cw + d] = dpre_a.astype(BF16)
        drest_ref[:, 3 * cw + d:3 * cw + 2 * d] = dpre_c.astype(BF16)
        dbias_ref[:, 0:d] += jnp.sum(dpre_a, axis=0, keepdims=True)
        dbias_ref[:, d:2 * d] += jnp.sum(dpre_c, axis=0, keepdims=True)

        dattn = (dmi * ga).astype(BF16)
        dattn_ref[...] = dattn
        do_ref[...] = _dot_nt(dattn, wao[...]).astype(BF16)
        dconvout = (dmi * gc).astype(BF16)
        dconvout_ref[...] = dconvout
        dconv_in = _dot_nt(dconvout, wco[...])
        drest_ref[:, 0:cw] = (dconv_in * cv).astype(BF16)

        dcv = dconv_in * cb
        following = dcv_next[...]
        dcm = wc[2:3, :] * dcv + wc[1:2, :] * _shift_up(dcv, following, 1) + wc[0:1, :] * _shift_up(dcv, following, 2)
        drest_ref[:, cw:2 * cw] = (dcm * conv[:, 2 * cw:3 * cw]).astype(BF16)
        drest_ref[:, 2 * cw:3 * cw] = (dcm * conv[:, cw:2 * cw]).astype(BF16)
        for tap, shifted in enumerate((cm2, cm1, cm)):
            dwc_ref[tap:tap + 1, :] += jnp.sum(dcv * shifted, axis=0, keepdims=True)
        dcv_next[...] = dcv[0:HALO, :]

    def rows(width):
        return pl.BlockSpec((tm, width), lambda i: (n - 1 - i, 0))

    prev_halo = pl.BlockSpec((HALO_BF16, 3 * cw), lambda i: (jnp.maximum((n - 1 - i) * per - 1, 0), 0))
    n_rest = 3 * cw + 2 * d
    return pl.pallas_call(
        body, name="mix_bwd", grid=(n,),
        in_specs=[rows(d), rows(d), rows(aw), rows(3 * cw), prev_halo, rows(2 * d), _const_spec((CONV_K, cw)),
                  _const_spec((1, d)), ANY, ANY, ANY],
        out_specs=[rows(d), rows(d), rows(d), rows(aw), rows(n_rest), _const_spec((1, d)), _const_spec((1, 2 * d)),
                   _const_spec((CONV_K, cw))],
        out_shape=[jax.ShapeDtypeStruct((s, d), BF16), jax.ShapeDtypeStruct((s, d), BF16),
                   jax.ShapeDtypeStruct((s, d), BF16), jax.ShapeDtypeStruct((s, aw), BF16),
                   jax.ShapeDtypeStruct((s, n_rest), BF16), jax.ShapeDtypeStruct((1, d), F32),
                   jax.ShapeDtypeStruct((1, 2 * d), F32), jax.ShapeDtypeStruct((CONV_K, cw), F32)],
        scratch_shapes=[pltpu.VMEM(w_ao.shape, BF16), pltpu.VMEM(w_co.shape, BF16), pltpu.VMEM(w_o.shape, BF16),
                        pltpu.VMEM((HALO, cw), F32), pltpu.SemaphoreType.DMA((3,))],
        compiler_params=_params(1),
    )(dx1, mixed, o, conv, conv, gate, wconv, g_post, w_ao, w_co, w_o)


def _mlp_ple_loss(x1, p, target, g_pre, g_post, g_ple, w_up, w_dn, w_pg, w_pp, tm):
    s, d = x1.shape
    ff = w_up.shape[1]
    pd = p.shape[1]
    fc = FF_CHUNK

    def body(x1_ref, p_ref, t_ref, gpre_ref, gpost_ref, gple_ref, wup_hbm, wdn_hbm, wpg_hbm, wpp_hbm,
             dx1_ref, h2_ref, du_ref, a_ref, df_ref, h3_ref, ds3_ref, dpp_ref, loss_ref, dgpre_ref, dgpost_ref,
             dgple_ref, wup, wdn, wpg, wpp, u_scr, sem):
        _load_resident([(wup_hbm, wup), (wdn_hbm, wdn), (wpg_hbm, wpg), (wpp_hbm, wpp)], sem)

        @pl.when(pl.program_id(0) == 0)
        def _():
            for ref in (loss_ref, dgpre_ref, dgpost_ref, dgple_ref):
                ref[...] = jnp.zeros_like(ref)

        x1v = x1_ref[...]
        r2 = _rms_scale(x1v)
        x1hat = x1v * r2
        h2 = (x1hat * gpre_ref[...]).astype(BF16)
        h2_ref[...] = h2
        f = jnp.zeros((tm, d), F32)
        for c0 in range(0, ff, fc):
            u = _dot(h2, wup[:, c0:c0 + fc])
            u_scr[:, c0:c0 + fc] = u
            a = jnp.square(jnp.maximum(u, 0.0)).astype(BF16)
            a_ref[:, c0:c0 + fc] = a
            f = f + _dot(a, wdn[c0:c0 + fc, :])
        rf = _rms_scale(f)
        fhat = f * rf
        x2 = x1v + fhat * gpost_ref[...]
        r3 = _rms_scale(x2)
        x2hat = x2 * r3
        h3 = (x2hat * gple_ref[...]).astype(BF16)
        h3_ref[...] = h3
        pg = _sigmoid(_dot(h3, wpg[...]))
        pp = _dot(p_ref[...].astype(BF16), wpp[...])
        diff = x2 + pg * pp - t_ref[...]
        loss_ref[...] += 0.5 * jnp.sum(jnp.mean(diff * diff, axis=-1, keepdims=True), axis=0, keepdims=True)

        dy = diff * (1.0 / d)
        dpp_ref[...] = (dy * pg).astype(BF16)
        ds3 = (dy * pp * pg * (1.0 - pg)).astype(BF16)
        ds3_ref[...] = ds3
        dh3 = _dot_nt(ds3, wpg[...])
        dgple_ref[...] += jnp.sum(dh3 * x2hat, axis=0, keepdims=True)
        dx2 = dy + _rms_bwd(x2hat, r3, gple_ref[...], dh3)
        dgpost_ref[...] += jnp.sum(dx2 * fhat, axis=0, keepdims=True)
        df = _rms_bwd(fhat, rf, gpost_ref[...], dx2).astype(BF16)
        df_ref[...] = df
        dh2 = jnp.zeros((tm, d), F32)
        for c0 in range(0, ff, fc):
            da = _dot_nt(df, wdn[c0:c0 + fc, :])
            du = (da * (2.0 * jnp.maximum(u_scr[:, c0:c0 + fc], 0.0))).astype(BF16)
            du_ref[:, c0:c0 + fc] = du
            dh2 = dh2 + _dot_nt(du, wup[:, c0:c0 + fc])
        dgpre_ref[...] += jnp.sum(dh2 * x1hat, axis=0, keepdims=True)
        dx1_ref[...] = dx2 + _rms_bwd(x1hat, r2, gpre_ref[...], dh2)

    vec = _const_spec((1, d))
    return pl.pallas_call(
        body, name="mlp_ple_loss", grid=(s // tm,),
        in_specs=[_row_spec(tm, d), _row_spec(tm, pd), _row_spec(tm, d), vec, vec, vec, ANY, ANY, ANY, ANY],
        out_specs=[_row_spec(tm, d), _row_spec(tm, d), _row_spec(tm, ff), _row_spec(tm, ff), _row_spec(tm, d),
                   _row_spec(tm, d), _row_spec(tm, d), _row_spec(tm, d), _const_spec((1, 1)), vec, vec, vec],
        out_shape=[jax.ShapeDtypeStruct((s, d), F32), jax.ShapeDtypeStruct((s, d), BF16),
                   jax.ShapeDtypeStruct((s, ff), BF16), jax.ShapeDtypeStruct((s, ff), BF16),
                   jax.ShapeDtypeStruct((s, d), BF16), jax.ShapeDtypeStruct((s, d), BF16),
                   jax.ShapeDtypeStruct((s, d), BF16), jax.ShapeDtypeStruct((s, d), BF16),
                   jax.ShapeDtypeStruct((1, 1), F32), jax.ShapeDtypeStruct((1, d), F32),
                   jax.ShapeDtypeStruct((1, d), F32), jax.ShapeDtypeStruct((1, d), F32)],
        scratch_shapes=[pltpu.VMEM(w_up.shape, BF16), pltpu.VMEM(w_dn.shape, BF16), pltpu.VMEM(w_pg.shape, BF16),
                        pltpu.VMEM(w_pp.shape, BF16), pltpu.VMEM((tm, ff), F32), pltpu.SemaphoreType.DMA((4,))],
        compiler_params=_params(1),
    )(x1, p, target, g_pre, g_post, g_ple, w_up, w_dn, w_pg, w_pp)


def _in_proj_bwd(x, dx1, pieces, g1, w_in, tm, exchange=None):
    s, d = x.shape
    ni = w_in.shape[1]
    widths = [p.shape[1] for p in pieces]
    grid = (s // tm,)
    ex = exchange or _NO_EXCHANGE

    def body(x_ref, dx1_ref, *rest):
        piece_refs, rest = rest[:len(pieces)], rest[len(pieces):]
        g_ref, w_hbm = rest[0], rest[1]
        ex_in, (dx_ref, dg_ref), ex_out, (w_vmem, sem, *sems) = _split_refs(rest[2:], ex, 2)
        _exchange_start(ex, ex_in, ex_out, sems, grid)
        _load_resident([(w_hbm, w_vmem)], sem)

        @pl.when(pl.program_id(0) == 0)
        def _():
            dg_ref[...] = jnp.zeros_like(dg_ref)

        dh = jnp.zeros((tm, d), F32)
        c0 = 0
        for ref, width in zip(piece_refs, widths):
            dh = dh + _dot_nt(ref[...], w_vmem[:, c0:c0 + width])
            c0 += width
        xv = x_ref[...]
        r = _rms_scale(xv)
        xhat = xv * r
        dg_ref[...] += jnp.sum(dh * xhat, axis=0, keepdims=True)
        dx_ref[...] = dx1_ref[...] + _rms_bwd(xhat, r, g_ref[...], dh)
        _exchange_wait(ex, ex_in, ex_out, sems, grid)

    return pl.pallas_call(
        body, name="in_proj_bwd", grid=grid,
        in_specs=[_row_spec(tm, d), _row_spec(tm, d)] + [_row_spec(tm, w) for w in widths]
        + [_const_spec((1, d)), ANY] + [ANY] * len(ex.arrays),
        out_specs=[_row_spec(tm, d), _const_spec((1, d))] + [ANY] * len(ex.out_shapes),
        out_shape=[jax.ShapeDtypeStruct((s, d), F32), jax.ShapeDtypeStruct((1, d), F32)] + ex.out_shapes,
        scratch_shapes=[pltpu.VMEM((d, ni), BF16), pltpu.SemaphoreType.DMA((1,))] + _exchange_sems(ex),
        compiler_params=_params(1),
    )(x, dx1, *pieces, g1, w_in, *ex.arrays)


def _weight_grad(a, b, name, into=None, col0=0, n_total=None):
    s, m = a.shape
    n = b.shape[1]
    tm, tk = min(m, DW_TILE), min(s, DW_TOKENS)
    tn = min(n, DW_TILE) if n_total is None else DW_PIECE_TILE
    nk = s // tk
    j0 = col0 // tn
    assert m % tm == 0 and n % tn == 0 and col0 % tn == 0

    def body(a_ref, b_ref, *rest):
        o_ref, acc = rest[-2:]
        k = pl.program_id(2)

        @pl.when(k == 0)
        def _():
            acc[...] = jnp.zeros_like(acc)

        acc[...] += _dot_tn(a_ref[...].astype(BF16), b_ref[...].astype(BF16))

        @pl.when(k == nk - 1)
        def _():
            o_ref[...] = acc[...].astype(BF16)

    extra = [] if into is None else [into]
    return pl.pallas_call(
        body, name=name, grid=(m // tm, n // tn, nk),
        in_specs=[pl.BlockSpec((tk, tm), lambda i, j, k: (k, i)), pl.BlockSpec((tk, tn), lambda i, j, k: (k, j))]
        + [ANY] * len(extra),
        out_specs=pl.BlockSpec((tm, tn), lambda i, j, k: (i, j0 + j)),
        out_shape=jax.ShapeDtypeStruct((m, n_total or n), BF16),
        input_output_aliases={2: 0} if extra else {},
        scratch_shapes=[pltpu.VMEM((tm, tn), F32)],
        compiler_params=_params(3),
    )(a, b, *extra)


def _mesh_position():
    return tuple(lax.axis_index(a) for a in MESH_AXES)


def _peer(me, k):
    bits = ((k >> 2) & 1, (k >> 1) & 1, k & 1)
    pos = tuple(1 - m if b else m for m, b in zip(me, bits))
    return pos, 4 * pos[0] + 2 * pos[1] + pos[2]


class _Exchange:
    def __init__(self, arrays, out_shapes, src, dst, relayed=None):
        self.arrays, self.out_shapes, self.src, self.dst = list(arrays), list(out_shapes), src, dst
        self.relayed = list(relayed) if relayed is not None else [False] * len(self.arrays)


_NO_EXCHANGE = _Exchange([], [], None, None)


def _exchange_sems(ex):
    n = len(ex.arrays)
    if n == 0:
        return []
    return [pltpu.SemaphoreType.DMA((n, N_DEV - 1)), pltpu.SemaphoreType.DMA((n, N_DEV - 1)),
            pltpu.SemaphoreType.DMA((n,))]


def _split_refs(rest, ex, n_own_outs):
    n_in, n_out = len(ex.arrays), len(ex.out_shapes)
    ex_in, rest = rest[:n_in], rest[n_in:]
    own, rest = rest[:n_own_outs], rest[n_own_outs:]
    return ex_in, own, rest[:n_out], rest[n_out:]


def _direct_steps(ex, w, in_refs, out_refs, sems):
    send_sems, recv_sems, local_sems = sems
    me = _mesh_position()
    mine = 4 * me[0] + 2 * me[1] + me[2]

    def copy(k):
        landing = ex.dst(w, out_refs, mine)
        if k == 0:
            return pltpu.make_async_copy(ex.src(w, in_refs, mine), landing, local_sems.at[w])
        peer, peer_idx = _peer(me, k)
        return pltpu.make_async_remote_copy(
            src_ref=ex.src(w, in_refs, peer_idx), dst_ref=landing, send_sem=send_sems.at[w, k - 1],
            recv_sem=recv_sems.at[w, k - 1], device_id=peer, device_id_type=pl.DeviceIdType.MESH)

    ks = range(N_DEV)
    return [lambda k=k: copy(k).start() for k in ks], [], [lambda k=k: copy(k).wait() for k in ks]


def _relayed_steps(ex, w, in_refs, out_refs, sems):
    send_sems, recv_sems, local_sems = sems
    x, y, c = _mesh_position()
    chips = [(1 - x, y), (x, 1 - y), (1 - x, 1 - y)]
    sibling = (x, y, 1 - c)
    js = range(len(chips))

    def block(px, py, pc):
        return ex.dst(w, out_refs, 4 * px + 2 * py + pc)

    def copy(k, dst, to, src=None):
        return pltpu.make_async_remote_copy(
            src_ref=ex.src(w, in_refs, None) if src is None else src, dst_ref=dst, send_sem=send_sems.at[w, k],
            recv_sem=recv_sems.at[w, k], device_id=to, device_id_type=pl.DeviceIdType.MESH)

    def local():
        return pltpu.make_async_copy(ex.src(w, in_refs, None), block(x, y, c), local_sems.at[w])

    def own(k):
        return copy(k, block(x, y, c), sibling if k == 0 else (*chips[k - 1], c))

    def came(j):
        return copy(1 + j, block(*chips[j], c), (*chips[j], c))

    def passed(j):
        return copy(4 + j, block(*chips[j], c), sibling, src=block(*chips[j], c))

    def from_sibling(k):
        return copy(k, block(x, y, 1 - c) if k == 0 else block(*chips[k - 4], 1 - c), sibling)

    start = [lambda: local().start()] + [lambda k=k: own(k).start() for k in range(4)]
    relay = [step for j in js for step in (lambda j=j: came(j).wait_recv(), lambda j=j: passed(j).start())]
    finish = ([lambda: local().wait()] + [lambda k=k: own(k).wait_send() for k in range(4)]
              + [lambda j=j: passed(j).wait_send() for j in js]
              + [lambda k=k: from_sibling(k).wait_recv() for k in (0, 4, 5, 6)])
    return start, relay, finish


def _exchange_steps(ex, in_refs, out_refs, sems):
    start, relay, finish = [], [], []
    for w in range(len(ex.arrays)):
        steps = (_relayed_steps if ex.relayed[w] else _direct_steps)(ex, w, in_refs, out_refs, sems)
        start += steps[0]
        relay += steps[1]
        finish += steps[2]
    return start, relay, finish


def _run(steps):
    for step in steps:
        step()


def _at_grid_step(grid, where):
    target = {"first": [0] * len(grid), "middle": [grid[0] // 2] + [0] * (len(grid) - 1),
              "last": [g - 1 for g in grid]}[where]
    hit = pl.program_id(0) == target[0]
    for axis in range(1, len(grid)):
        hit = jnp.logical_and(hit, pl.program_id(axis) == target[axis])
    return hit


def _exchange_start(ex, in_refs, out_refs, sems, grid):
    if ex.arrays:
        @pl.when(_at_grid_step(grid, "first"))
        def _():
            _run(_exchange_steps(ex, in_refs, out_refs, sems)[0])

        if any(ex.relayed):
            assert grid[0] >= 2

            @pl.when(_at_grid_step(grid, "middle"))
            def _():
                _run(_exchange_steps(ex, in_refs, out_refs, sems)[1])


def _exchange_wait(ex, in_refs, out_refs, sems, grid):
    if ex.arrays:
        @pl.when(_at_grid_step(grid, "last"))
        def _():
            _run(_exchange_steps(ex, in_refs, out_refs, sems)[2])


def _shard_block(ref, shard_shape, by_col, idx):
    r, c = shard_shape
    if by_col:
        return ref.at[:, pl.ds(pl.multiple_of(idx * c, LANES), c)]
    return ref.at[pl.ds(pl.multiple_of(idx * r, 16), r), :]


def _full_shape(shard_shape, by_col):
    r, c = shard_shape
    return (r, N_DEV * c) if by_col else (N_DEV * r, c)


def _gather_exchange(shards, col_sharded):
    shapes = [a.shape for a in shards]
    return _Exchange(
        shards, [jax.ShapeDtypeStruct(_full_shape(sh, bc), a.dtype) for a, sh, bc in zip(shards, shapes, col_sharded)],
        lambda w, refs, idx: refs[w],
        lambda w, refs, idx: _shard_block(refs[w], shapes[w], col_sharded[w], idx), [True] * len(shards))


def _scatter_exchange(grads, col_sharded):
    shapes = []
    for g, by_col in zip(grads, col_sharded):
        r, c = g.shape
        shapes.append((r, c // N_DEV) if by_col else (r // N_DEV, c))
    return _Exchange(
        grads, [jax.ShapeDtypeStruct((N_DEV,) + sh, g.dtype) for g, sh in zip(grads, shapes)],
        lambda w, refs, idx: _shard_block(refs[w], shapes[w], col_sharded[w], idx),
        lambda w, refs, mine: refs[w].at[mine])


def _broadcast_exchange(arrays):
    return _Exchange(arrays, [jax.ShapeDtypeStruct((N_DEV,) + a.shape, a.dtype) for a in arrays],
                     lambda w, refs, idx: refs[w], lambda w, refs, mine: refs[w].at[mine])


def _join(*exs):
    arrays, shapes, owner = [], [], []
    for e in exs:
        for w in range(len(e.arrays)):
            owner.append((e, w, len(arrays), len(shapes)))
        arrays += e.arrays
        shapes += e.out_shapes

    def src(w, refs, idx):
        e, w0, i0, _ = owner[w]
        return e.src(w0, refs[i0:i0 + len(e.arrays)], idx)

    def dst(w, refs, idx):
        e, w0, _, o0 = owner[w]
        return e.dst(w0, refs[o0:o0 + len(e.out_shapes)], idx)

    return _Exchange(arrays, shapes, src, dst, [flag for e in exs for flag in e.relayed])


def _exchange_call(ex, name):
    n_in = len(ex.arrays)

    def body(*refs):
        in_refs, _, out_refs, sems = _split_refs(refs, ex, 0)
        for steps in _exchange_steps(ex, in_refs, out_refs, sems):
            _run(steps)

    return pl.pallas_call(
        body, name=name, in_specs=[ANY] * n_in, out_specs=[ANY] * len(ex.out_shapes), out_shape=ex.out_shapes,
        scratch_shapes=_exchange_sems(ex), compiler_params=pltpu.CompilerParams(vmem_limit_bytes=VMEM_LIMIT),
    )(*ex.arrays)


def _to_bf16(arrays):
    def body(*refs):
        for src, dst in zip(refs[:len(arrays)], refs[len(arrays):]):
            dst[...] = src[...].astype(BF16)

    vmem = pl.BlockSpec(memory_space=pltpu.VMEM)
    return pl.pallas_call(
        body, name="weights_to_bf16", in_specs=[vmem] * len(arrays), out_specs=[vmem] * len(arrays),
        out_shape=[jax.ShapeDtypeStruct(a.shape, BF16) for a in arrays],
        compiler_params=pltpu.CompilerParams(vmem_limit_bytes=VMEM_LIMIT),
    )(*arrays)


def _adamw(w, g, m, v):
    m = ADAM_B1 * m + (1.0 - ADAM_B1) * g
    v = ADAM_B2 * v + (1.0 - ADAM_B2) * jnp.square(g)
    m_hat = m / (1.0 - ADAM_B1 ** ADAM_STEP)
    v_hat = v / (1.0 - ADAM_B2 ** ADAM_STEP)
    delta = -ADAM_LR * (m_hat / (jnp.sqrt(v_hat) + ADAM_EPS) + ADAM_WD * w)
    return delta, m, v


def _sum_and_adamw(parts, w, m, v, name):
    r, c = w.shape
    tr = min(r, 256)

    def body(p_ref, w_ref, m_ref, v_ref, g_out, d_out, m_out, v_out):
        g = p_ref[0].astype(F32)
        for dev in range(1, N_DEV):
            g = g + p_ref[dev].astype(F32)
        g_out[...] = g
        d_out[...], m_out[...], v_out[...] = _adamw(w_ref[...], g, m_ref[...], v_ref[...])

    blk = pl.BlockSpec((tr, c), lambda i: (i, 0))
    return pl.pallas_call(
        body, name=name, grid=(r // tr,),
        in_specs=[pl.BlockSpec((N_DEV, tr, c), lambda i: (0, i, 0)), blk, blk, blk],
        out_specs=[blk] * 4, out_shape=[jax.ShapeDtypeStruct((r, c), F32)] * 4,
        compiler_params=_params(1),
    )(parts, w, m, v)


BIG = ("w_in", "w_attn_out", "w_conv_out", "w_o", "w_up", "w_down", "w_ple_gate", "w_ple_proj")
COL_SHARDED = {"w_in": True, "w_attn_out": True, "w_conv_out": True, "w_o": False, "w_up": True, "w_down": False,
               "w_ple_gate": False, "w_ple_proj": True}
SMALL = ("g_pre_mix", "b_gate", "g_post_mix", "g_pre_mlp", "g_post_mlp", "g_ple")


REST = BIG[1:]


def _local_grads(x, p, target, small, wconv, full, aw, cw, tm, t, gather_rest=None, scatter_rest=None,
                 scatter_in=None):
    full = dict(full)
    tm_wide = min(WIDE_BLOCKS * tm, x.shape[0])
    qkv, conv, gate, h1 = _in_proj_fwd(x, small["g_pre_mix"], small["b_gate"], full["w_in"], aw, cw, tm_wide)
    o, *rest = _attn_fwd(qkv, aw, t, gather_rest)
    full.update(zip(REST, rest))
    x1, mixed, mix_in, conv_in = _mix_fwd(x, o, conv, gate, wconv, small["g_post_mix"], full["w_attn_out"],
                                          full["w_conv_out"], full["w_o"], tm_wide)
    (dx1, h2, du, a, df, h3, ds3, dpp, loss, dg_pre_mlp, dg_post_mlp, dg_ple) = _mlp_ple_loss(
        x1, p, target, small["g_pre_mlp"], small["g_post_mlp"], small["g_ple"], full["w_up"], full["w_down"],
        full["w_ple_gate"], full["w_ple_proj"], tm)
    big = {"w_up": _weight_grad(h2, du, "dw_up"), "w_down": _weight_grad(a, df, "dw_down"),
           "w_ple_gate": _weight_grad(h3, ds3, "dw_ple_gate"), "w_ple_proj": _weight_grad(p, dpp, "dw_ple_proj")}
    (dmixed, dattn, dconvout, do, drest, dg_post_mix, db_gate, dwconv) = _mix_bwd(
        dx1, mixed, o, conv, gate, wconv, small["g_post_mix"], full["w_attn_out"], full["w_conv_out"], full["w_o"],
        tm_wide)
    big.update({"w_attn_out": _weight_grad(o, dattn, "dw_attn_out"),
                "w_conv_out": _weight_grad(conv_in, dconvout, "dw_conv_out"),
                "w_o": _weight_grad(mix_in, dmixed, "dw_o")})
    dq, dk, dv, *scattered = _attn_bwd(qkv, o, do, aw, t, scatter_rest and scatter_rest([big[n] for n in REST]))
    pieces = [dq, dk, dv, drest]
    dw_in, col0, ni = None, 0, full["w_in"].shape[1]
    for i, piece in enumerate(pieces):
        dw_in = _weight_grad(h1, piece, "dw_in_%d" % i, dw_in, col0, ni)
        col0 += piece.shape[1]
    big["w_in"] = dw_in
    dx, dg_pre_mix, *scattered_in = _in_proj_bwd(x, dx1, pieces, small["g_pre_mix"], full["w_in"], tm_wide,
                                                scatter_in and scatter_in(dw_in))
    small_grads = {"g_pre_mix": dg_pre_mix, "b_gate": db_gate, "g_post_mix": dg_post_mix, "g_pre_mlp": dg_pre_mlp,
                   "g_post_mlp": dg_post_mlp, "g_ple": dg_ple, "w_conv": dwconv}
    return loss[0, 0], dx, big, small_grads, scattered_in + scattered


PACK_ROWS = 16


def _pack_layout(shapes, d):
    slots, at = [], 0
    for i, (r, c) in enumerate(shapes):
        assert d % c == 0
        for row in range(r):
            slots.append((i, row, at // d, at % d))
            at += c
        at = -(-at // d) * d
    assert at <= PACK_ROWS * d
    return slots


def _pack_small(groups, d):
    shapes = [a.shape for a in groups[0]]
    slots = _pack_layout(shapes, d)
    n = len(shapes)

    def body(*refs):
        ins, outs = refs[:n * len(groups)], refs[n * len(groups):]
        for g, out in enumerate(outs):
            out[...] = jnp.zeros_like(out)
            for i, row, pr, pc in slots:
                src = ins[g * n + i]
                out[pr:pr + 1, pc:pc + shapes[i][1]] = src[row:row + 1, :]

    vmem = pl.BlockSpec(memory_space=pltpu.VMEM)
    return pl.pallas_call(
        body, name="pack_small", in_specs=[vmem] * (n * len(groups)), out_specs=[vmem] * len(groups),
        out_shape=[jax.ShapeDtypeStruct((PACK_ROWS, d), F32)] * len(groups),
    )(*[a for group in groups for a in group])


def _unpack_small(pack, shapes, d):
    slots = _pack_layout(shapes, d)
    return [jnp.stack([pack[pr, pc:pc + shapes[i][1]] for j, row, pr, pc in slots if j == i])
            for i in range(len(shapes))]


def kernel(x, p, g_pre_mix, w_in, b_gate, w_conv, w_attn_out, w_conv_out, w_o, g_post_mix, g_pre_mlp, w_up, w_down, g_post_mlp, g_ple, w_ple_gate, w_ple_proj, loss_target, m_g_pre_mix, m_w_in, m_b_gate, m_w_conv, m_w_attn_out, m_w_conv_out, m_w_o, m_g_post_mix, m_g_pre_mlp, m_w_up, m_w_down, m_g_post_mlp, m_g_ple, m_w_ple_gate, m_w_ple_proj, v_g_pre_mix, v_w_in, v_b_gate, v_w_conv, v_w_attn_out, v_w_conv_out, v_w_o, v_g_post_mix, v_g_pre_mlp, v_w_up, v_w_down, v_g_post_mlp, v_g_ple, v_w_ple_gate, v_w_ple_proj):
    given = dict(locals())
    order = ["g_pre_mix", "w_in", "b_gate", "w_conv", "w_attn_out", "w_conv_out", "w_o", "g_post_mix", "g_pre_mlp",
             "w_up", "w_down", "g_post_mlp", "g_ple", "w_ple_gate", "w_ple_proj"]
    d = x.shape[-1]
    me = 4 * lax.axis_index("x") + 2 * lax.axis_index("y") + lax.axis_index("c")

    col = [COL_SHARDED[n] for n in BIG]
    shards = _to_bf16([given[n][0] for n in BIG])
    cw_shard = w_conv.shape[-1]
    conv_tile = jnp.pad(w_conv[0], ((0, HALO - CONV_K), (0, LANES - cw_shard)))
    w_in_full, conv_g = _exchange_call(
        _join(_gather_exchange(shards[:1], col[:1]), _broadcast_exchange([conv_tile])), "gather_w_in")
    wconv = jnp.concatenate([conv_g[dev, :CONV_K, :cw_shard] for dev in range(N_DEV)], axis=1)

    small = {n: given[n] for n in SMALL}
    loss, dx, big_grads, small_grads, parts = _local_grads(
        x[0], p[0, 0], loss_target[0], small, wconv, {"w_in": w_in_full}, w_attn_out.shape[1], w_conv_out.shape[1],
        ROW_BLOCK, ATTN_BLOCK,
        _gather_exchange(shards[1:], col[1:]), lambda grads: _scatter_exchange(grads, col[1:]),
        lambda grad: _scatter_exchange([grad], col[:1]))
    small_names = list(SMALL) + ["w_conv"]
    two_d = lambda a: a.reshape(-1, d) if a.shape[-1] > d else a.reshape(-1, a.shape[-1])
    full_conv = lambda a: lax.dynamic_update_slice(jnp.zeros((CONV_K, N_DEV * cw_shard), F32), a[0],
                                                   (jnp.int32(0), me * cw_shard))
    groups = [[two_d(small_grads[n]) for n in small_names] + [loss.reshape(1, 1)]]
    for pre in ("", "m_", "v_"):
        groups.append([two_d(given[pre + n]) for n in SMALL] + [full_conv(given[pre + "w_conv"]), jnp.zeros((1, 1), F32)])
    pack, *state = _pack_small(groups, d)
    packs, = _exchange_call(_broadcast_exchange([pack]), "share_small_grads")

    grads, deltas, new_m, new_v = {}, {}, {}, {}
    for n, part in zip(BIG, parts):
        grads[n], deltas[n], new_m[n], new_v[n] = (
            a[None] for a in _sum_and_adamw(part, given[n][0], given["m_" + n][0], given["v_" + n][0], "adamw_" + n))

    outs = _sum_and_adamw(packs, *state, "adamw_small")
    shapes = [a.shape for a in groups[0]]
    for res, dst in zip(outs, (grads, deltas, new_m, new_v)):
        for n, a in zip(small_names + ["loss"], _unpack_small(res, shapes, d)):
            if n == "w_conv":
                a = lax.dynamic_slice(a, (jnp.int32(0), me * cw_shard), (CONV_K, cw_shard))[None]
            dst[n] = a.reshape(given[n].shape) if n in SMALL else a
    loss = grads["loss"][0, 0]

    return (loss, dx[None], *[grads[n] for n in order], *[deltas[n] for n in order],
            *[new_m[n] for n in order], *[new_v[n] for n in order])
```

```python
import jax
import jax.numpy as jnp
from jax import lax
from jax.experimental import pallas as pl
from jax.experimental.pallas import tpu as pltpu

F32 = jnp.float32
BF16 = jnp.bfloat16
RMS_EPS = 1e-6
N_DEV = 8
MESH_AXES = ("x", "y", "c")
LANES = 128
HEAD_DIM = 64
HEADS_PER_GROUP = LANES // HEAD_DIM
CONV_K = 3
HALO = 8
HALO_BF16 = 16
VMEM_LIMIT = 56 * 1024 * 1024
EXP2_ZERO = -150.0
LOG2_E = 1.4426950408889634

ADAM_LR = 0.001
ADAM_B1 = 0.9
ADAM_B2 = 0.999
ADAM_EPS = 1e-08
ADAM_WD = 0.01
ADAM_STEP = 10

ROW_BLOCK = 256
WIDE_BLOCKS = 2
ATTN_BLOCK = 256
ATTN_ROW_SPLITS = 2
DW_TOKENS = 2048
DW_TILE = 1024
DW_MANY_TILES = 6
DW_PIECE_TILE = 512
FF_CHUNK = 1024
PROJ_CHUNK = 512


def _dot(a, b):
    return lax.dot_general(a, b, (((1,), (0,)), ((), ())), preferred_element_type=F32)


def _dot_nt(a, b):
    return lax.dot_general(a, b, (((1,), (1,)), ((), ())), preferred_element_type=F32)


def _dot_tn(a, b):
    return lax.dot_general(a, b, (((0,), (0,)), ((), ())), preferred_element_type=F32)


def _sigmoid(z):
    return 1.0 / (1.0 + jnp.exp(-z))


def _rms_scale(x):
    return lax.rsqrt(jnp.mean(x * x, axis=-1, keepdims=True) + RMS_EPS)


def _rms_bwd(xhat, r, g, dy):
    gd = dy * g
    return r * (gd - xhat * jnp.mean(gd * xhat, axis=-1, keepdims=True))


def _params(n_axes, **kw):
    return pltpu.CompilerParams(dimension_semantics=("arbitrary",) * n_axes, vmem_limit_bytes=VMEM_LIMIT, **kw)


def _load_resident(pairs, sem):
    @pl.when(pl.program_id(0) == 0)
    def _():
        copies = [pltpu.make_async_copy(src, dst, sem.at[i]) for i, (src, dst) in enumerate(pairs)]
        for cp in copies:
            cp.start()
        for cp in copies:
            cp.wait()


def _row_spec(tm, width):
    return pl.BlockSpec((tm, width), lambda i: (i, 0))


def _prev_halo_spec(tm, width, rows):
    per = tm // rows
    return pl.BlockSpec((rows, width), lambda i: (jnp.maximum(i * per - 1, 0), 0))


def _const_spec(shape):
    return pl.BlockSpec(shape, lambda i: (0,) * len(shape))


ANY = pl.BlockSpec(memory_space=pl.ANY)


def _shift_down(cur, prev, n):
    rows = lax.broadcasted_iota(jnp.int32, cur.shape, 0)
    out = pltpu.roll(cur, n, 0)
    for j in range(n):
        out = jnp.where(rows == j, prev[prev.shape[0] - n + j:prev.shape[0] - n + j + 1, :], out)
    return out


def _shift_up(cur, nxt, n):
    tm = cur.shape[0]
    rows = lax.broadcasted_iota(jnp.int32, cur.shape, 0)
    out = pltpu.roll(cur, tm - n, 0)
    for j in range(n):
        out = jnp.where(rows == tm - n + j, nxt[j:j + 1, :], out)
    return out


def _conv_taps(cm, cm_prev, wconv):
    cm1 = _shift_down(cm, cm_prev, 1)
    cm2 = _shift_down(cm, cm_prev, 2)
    cv = wconv[2:3, :] * cm + wconv[1:2, :] * cm1 + wconv[0:1, :] * cm2
    return cv, cm1, cm2


def _in_proj_fwd(x, g1, b_gate, w_in, aw, cw, tm):
    s, d = x.shape
    ni = w_in.shape[1]
    n_qkv, n_conv = 3 * aw, 3 * cw
    ch = PROJ_CHUNK

    def body(x_ref, g_ref, b_ref, w_hbm, qkv_ref, conv_ref, gate_ref, h_ref, w_vmem, sem):
        _load_resident([(w_hbm, w_vmem)], sem)
        xv = x_ref[...]
        h = (xv * _rms_scale(xv) * g_ref[...]).astype(BF16)
        h_ref[...] = h
        for c0 in range(0, ni, ch):
            pc = _dot(h, w_vmem[:, c0:c0 + ch])
            if c0 < n_qkv:
                qkv_ref[:, c0:c0 + ch] = pc.astype(BF16)
            elif c0 < n_qkv + n_conv:
                conv_ref[:, c0 - n_qkv:c0 - n_qkv + ch] = pc.astype(BF16)
            else:
                g0 = c0 - n_qkv - n_conv
                gate_ref[:, g0:g0 + ch] = _sigmoid(pc + b_ref[:, g0:g0 + ch]).astype(BF16)

    return pl.pallas_call(
        body, name="in_proj_fwd", grid=(s // tm,),
        in_specs=[_row_spec(tm, d), _const_spec((1, d)), _const_spec((1, 2 * d)), ANY],
        out_specs=[_row_spec(tm, n_qkv), _row_spec(tm, n_conv), _row_spec(tm, 2 * d), _row_spec(tm, d)],
        out_shape=[jax.ShapeDtypeStruct((s, n_qkv), BF16), jax.ShapeDtypeStruct((s, n_conv), BF16),
                   jax.ShapeDtypeStruct((s, 2 * d), BF16), jax.ShapeDtypeStruct((s, d), BF16)],
        scratch_shapes=[pltpu.VMEM((d, ni), BF16), pltpu.SemaphoreType.DMA((1,))],
        compiler_params=_params(1),
    )(x, g1, b_gate, w_in)


def _split_hi_lo(a):
    hi = a.astype(BF16)
    return hi, (a - hi.astype(F32)).astype(BF16)


def _log2_gates(z):
    z2 = z * LOG2_E
    nz2 = -z2
    log_keep = jnp.minimum(nz2, 0.0) - jnp.log2(1.0 + jnp.exp2(jnp.minimum(z2, nz2)))
    return log_keep + z2, log_keep


def _attn_masks(t):
    row = lax.broadcasted_iota(jnp.int32, (t, t), 0)
    col = lax.broadcasted_iota(jnp.int32, (t, t), 1)
    return (col < row).astype(F32), (row > col).astype(BF16), (row >= col).astype(BF16)


def _chains(a):
    tr = a.shape[0] // ATTN_ROW_SPLITS
    return [jnp.where(_head_lanes(h), a[r * tr:(r + 1) * tr], jnp.zeros((tr, LANES), a.dtype))
            for h in range(HEADS_PER_GROUP) for r in range(ATTN_ROW_SPLITS)]


def _merge_chains(parts):
    rows = []
    for r in range(ATTN_ROW_SPLITS):
        out = parts[r]
        for h in range(1, HEADS_PER_GROUP):
            out = jnp.where(_head_lanes(h), parts[h * ATTN_ROW_SPLITS + r], out)
        rows.append(out)
    return jnp.concatenate(rows, axis=0)


def _by_stage(n_chains, stages):
    for stage in stages:
        for c in range(n_chains):
            stage(c)


def _row_parts(a):
    tr = a.shape[0] // ATTN_ROW_SPLITS
    return [a[r * tr:(r + 1) * tr] for r in range(ATTN_ROW_SPLITS)]


def _while_weights_live(qi, block, carry):
    def cond(state):
        j, carry = state
        live = jnp.max(carry[0][0])
        for run in carry[0][1:]:
            live = jnp.maximum(live, jnp.max(run))
        return jnp.logical_and(j < qi, live >= EXP2_ZERO)

    def step(state):
        j, carry = state
        return j + 1, block(qi - 1 - j, carry)

    return lax.while_loop(cond, step, (jnp.int32(0), carry))[1]


def _head_lanes(h):
    lane = lax.broadcasted_iota(jnp.int32, (1, LANES), 1)
    return (lane >= HEAD_DIM * h) & (lane < HEAD_DIM * (h + 1))


def _attn_fwd(qkv, aw, t, exchange=None):
    s = qkv.shape[0]
    groups = aw // LANES
    nq = s // t
    scale = HEAD_DIM ** -0.5
    ex = exchange or _NO_EXCHANGE
    causal, upper, _ = _attn_masks(t)
    mask_spec = pl.BlockSpec((t, t), lambda g, i: (0, 0))

    def body(q_ref, k_ref, v_ref, causal_ref, upper_ref, *rest):
        ex_in, (o_ref, ob_ref), ex_out, sems = _split_refs(rest, ex, 2)
        qi = pl.program_id(1)
        _exchange_start(ex, ex_in, ex_out, sems, (groups, nq))
        upper = upper_ref[...]
        causal = _row_parts(causal_ref[...] > 0.5) * HEADS_PER_GROUP
        qs = _chains(q_ref[...] * scale)
        heads = range(len(qs))
        tr = t // ATTN_ROW_SPLITS

        def block(kb, runs, accs, diag):
            rows = pl.ds(pl.multiple_of(kb * t, t), t)
            k = k_ref[rows, :]
            v = v_ref[rows, :]
            ncs = [(h % ATTN_ROW_SPLITS + 1) * tr if diag else t for h in heads]
            live = [{} for _ in heads]
            new_runs, new_accs = [None] * len(heads), [None] * len(heads)

            def scores(h):
                live[h]["z"] = _dot_nt(qs[h], k[0:ncs[h]])

            def gates(h):
                nc = ncs[h]
                log_b, log_keep = _log2_gates(live[h].pop("z"))
                if diag:
                    log_keep = jnp.where(causal[h][:, 0:nc], log_keep, 0.0)
                hi, lo = _split_hi_lo(log_keep)
                live[h]["log_w"] = log_b + runs[h]
                live[h]["between"] = _dot(hi, upper[0:nc, 0:nc]) + _dot(lo, upper[0:nc, 0:nc])
                new_runs[h] = runs[h] + jnp.sum(log_keep, axis=1, keepdims=True)

            def weights(h):
                nc = ncs[h]
                w = jnp.exp2(live[h].pop("log_w") + live[h].pop("between"))
                if diag:
                    w = jnp.where(causal[h][:, 0:nc], w, 0.0)
                new_accs[h] = accs[h] + _dot(w.astype(BF16), v[0:nc])

            _by_stage(len(heads), [scores, gates, weights])
            return tuple(new_runs), tuple(new_accs)

        carry = block(qi, [jnp.zeros((tr, 1), F32)] * len(heads), [jnp.zeros((tr, LANES), F32)] * len(heads), True)
        _, accs = _while_weights_live(qi, lambda kb, carry: block(kb, *carry, False), carry)
        o = _merge_chains(accs)
        o_ref[...] = o
        ob_ref[...] = o.astype(BF16)
        _exchange_wait(ex, ex_in, ex_out, sems, (groups, nq))

    return pl.pallas_call(
        body, name="attn_fwd", grid=(groups, nq),
        in_specs=[pl.BlockSpec((t, LANES), lambda g, i: (i, g)),
                  pl.BlockSpec((s, LANES), lambda g, i: (0, groups + g)),
                  pl.BlockSpec((s, LANES), lambda g, i: (0, 2 * groups + g)), mask_spec, mask_spec]
        + [ANY] * len(ex.arrays),
        out_specs=[pl.BlockSpec((t, LANES), lambda g, i: (i, g))] * 2 + [ANY] * len(ex.out_shapes),
        out_shape=[jax.ShapeDtypeStruct((s, aw), F32), jax.ShapeDtypeStruct((s, aw), BF16)] + ex.out_shapes,
        scratch_shapes=_exchange_sems(ex),
        compiler_params=_params(2),
    )(qkv, qkv, qkv, causal, upper, *ex.arrays)


def _attn_bwd(qkv, o, do, aw, t, exchange=None):
    s = qkv.shape[0]
    groups = aw // LANES
    nq = s // t
    scale = HEAD_DIM ** -0.5
    ex = exchange or _NO_EXCHANGE

    def body(q_ref, k_ref, v_ref, o_ref, do_ref, causal_ref, upper_ref, lower_ref, *rest):
        ex_in, (dq_ref, dk_ref, dv_ref), ex_out, (dk_acc, dv_acc, *sems) = _split_refs(rest, ex, 3)
        qi = pl.program_id(1)
        _exchange_start(ex, ex_in, ex_out, sems, (groups, nq))

        @pl.when(qi == 0)
        def _():
            dk_acc[...] = jnp.zeros_like(dk_acc)
            dv_acc[...] = jnp.zeros_like(dv_acc)

        upper = upper_ref[...]
        lower_incl = lower_ref[...]
        causal = _row_parts(causal_ref[...] > 0.5) * HEADS_PER_GROUP
        q = q_ref[...] * scale
        do_b = do_ref[...]
        qs = _chains(q)
        dos = _chains(do_b)
        qs_all = jnp.concatenate(qs, axis=0)
        dos_all = jnp.concatenate(dos, axis=0)
        e_totals = [jnp.sum(part, axis=1, keepdims=True) for part in _chains(do_b.astype(F32) * o_ref[...])]
        heads = range(len(qs))
        tr = t // ATTN_ROW_SPLITS

        def block(kb, runs, e_runs, dqs, diag):
            rows = pl.ds(pl.multiple_of(kb * t, t), t)
            k = k_ref[rows, :]
            v = v_ref[rows, :]
            ncs = [(h % ATTN_ROW_SPLITS + 1) * tr if diag else t for h in heads]
            live = [{} for _ in heads]
            none = [None] * len(heads)
            new_runs, new_e_runs, new_dqs, dzbs, wbs = list(none), list(none), list(none), list(none), list(none)

            def scores(h):
                live[h]["z"] = _dot_nt(qs[h], k[0:ncs[h]])
                live[h]["dw"] = _dot_nt(dos[h], v[0:ncs[h]])

            def gates(h):
                nc = ncs[h]
                log_b, log_keep = _log2_gates(live[h].pop("z"))
                live[h]["beta"] = jnp.exp2(log_b)
                live[h]["keep"] = jnp.exp2(log_keep)
                if diag:
                    log_keep = jnp.where(causal[h][:, 0:nc], log_keep, 0.0)
                hi, lo = _split_hi_lo(log_keep)
                live[h]["log_w"] = log_b + runs[h]
                live[h]["between"] = _dot(hi, upper[0:nc, 0:nc]) + _dot(lo, upper[0:nc, 0:nc])
                new_runs[h] = runs[h] + jnp.sum(log_keep, axis=1, keepdims=True)

            def weights(h):
                nc = ncs[h]
                w = jnp.exp2(live[h].pop("log_w") + live[h].pop("between"))
                if diag:
                    w = jnp.where(causal[h][:, 0:nc], w, 0.0)
                wb = w.astype(BF16)
                e = live[h].pop("dw") * wb.astype(F32)
                hi, lo = _split_hi_lo(e)
                live[h]["e"] = e
                live[h]["e_suffix"] = _dot(hi, lower_incl[0:nc, 0:nc]) + _dot(lo, lower_incl[0:nc, 0:nc]) + e_runs[h]
                wbs[h] = wb

            def score_grads(h):
                nc = ncs[h]
                e_suffix = live[h].pop("e_suffix")
                dz = live[h].pop("e") * live[h].pop("keep") - (e_totals[h] - e_suffix) * live[h].pop("beta")
                if diag:
                    dz = jnp.where(causal[h][:, 0:nc], dz, 0.0)
                dzb = dz.astype(BF16)
                new_dqs[h] = dqs[h] + _dot(dzb, k[0:nc])
                new_e_runs[h] = e_suffix[:, 0:1]
                if nc < t:
                    unseen = jnp.zeros((tr, t - nc), BF16)
                    dzb = jnp.concatenate([dzb, unseen], axis=1)
                    wbs[h] = jnp.concatenate([wbs[h], unseen], axis=1)
                dzbs[h] = dzb

            _by_stage(len(heads), [scores, gates, weights, score_grads])
            dk_acc[rows, :] += _dot_tn(jnp.concatenate(dzbs, axis=0), qs_all)
            dv_acc[rows, :] += _dot_tn(jnp.concatenate(wbs, axis=0), dos_all)
            return tuple(new_runs), tuple(new_e_runs), tuple(new_dqs)

        zero_cols = [jnp.zeros((tr, 1), F32)] * len(heads)
        carry = block(qi, zero_cols, zero_cols, [jnp.zeros((tr, LANES), F32)] * len(heads), True)
        _, _, dqs = _while_weights_live(qi, lambda kb, carry: block(kb, *carry, False), carry)
        dq_ref[...] = (_merge_chains(dqs) * scale).astype(BF16)

        @pl.when(qi == nq - 1)
        def _():
            dk_ref[...] = dk_acc[...].astype(BF16)
            dv_ref[...] = dv_acc[...].astype(BF16)

        _exchange_wait(ex, ex_in, ex_out, sems, (groups, nq))

    blk = pl.BlockSpec((t, LANES), lambda g, i: (i, g))
    slab = pl.BlockSpec((s, LANES), lambda g, i: (0, g))
    mask_spec = pl.BlockSpec((t, t), lambda g, i: (0, 0))
    return pl.pallas_call(
        body, name="attn_bwd", grid=(groups, nq),
        in_specs=[blk, pl.BlockSpec((s, LANES), lambda g, i: (0, groups + g)),
                  pl.BlockSpec((s, LANES), lambda g, i: (0, 2 * groups + g)), blk, blk, mask_spec, mask_spec, mask_spec]
        + [ANY] * len(ex.arrays),
        out_specs=[blk, slab, slab] + [ANY] * len(ex.out_shapes),
        out_shape=[jax.ShapeDtypeStruct((s, aw), BF16)] * 3 + ex.out_shapes,
        scratch_shapes=[pltpu.VMEM((s, LANES), F32), pltpu.VMEM((s, LANES), F32)] + _exchange_sems(ex),
        compiler_params=_params(2),
    )(qkv, qkv, qkv, o, do, *_attn_masks(t), *ex.arrays)


def _branches(o_b, conv, conv_prev, wconv, w_ao, w_co, cw, first):
    conv = conv.astype(F32)
    conv_prev = conv_prev.astype(F32)
    cb = conv[:, 0:cw]
    cm = conv[:, cw:2 * cw] * conv[:, 2 * cw:3 * cw]
    cm_prev = conv_prev[:, cw:2 * cw] * conv_prev[:, 2 * cw:3 * cw]
    cm_prev = jnp.where(first, 0.0, cm_prev)
    cv, cm1, cm2 = _conv_taps(cm, cm_prev, wconv)
    conv_in = (cb * cv).astype(BF16)
    return _dot(o_b, w_ao), _dot(conv_in, w_co), conv_in, cb, cv, cm, cm1, cm2


def _mix_fwd(x, o, conv, gate, wconv, g_post, w_ao, w_co, w_o, tm):
    s, d = x.shape
    aw, cw = w_ao.shape[0], w_co.shape[0]

    def body(x_ref, o_ref, conv_ref, prev_ref, gate_ref, wc_ref, g_ref, wao_hbm, wco_hbm, wo_hbm,
             x1_ref, mixed_ref, mixin_ref, convin_ref, wao, wco, wo, sem):
        _load_resident([(wao_hbm, wao), (wco_hbm, wco), (wo_hbm, wo)], sem)
        y_attn, y_conv, conv_in, *_ = _branches(
            o_ref[...].astype(BF16), conv_ref[...], prev_ref[...], wc_ref[...], wao[...], wco[...], cw,
            pl.program_id(0) == 0)
        mix_in = (gate_ref[:, 0:d].astype(F32) * y_attn + gate_ref[:, d:2 * d].astype(F32) * y_conv).astype(BF16)
        mixed = _dot(mix_in, wo[...])
        x1_ref[...] = x_ref[...] + mixed * _rms_scale(mixed) * g_ref[...]
        mixed_ref[...] = mixed
        mixin_ref[...] = mix_in
        convin_ref[...] = conv_in

    return pl.pallas_call(
        body, name="mix_fwd", grid=(s // tm,),
        in_specs=[_row_spec(tm, d), _row_spec(tm, aw), _row_spec(tm, 3 * cw), _prev_halo_spec(tm, 3 * cw, HALO_BF16),
                  _row_spec(tm, 2 * d), _const_spec((CONV_K, cw)), _const_spec((1, d)), ANY, ANY, ANY],
        out_specs=[_row_spec(tm, d), _row_spec(tm, d), _row_spec(tm, d), _row_spec(tm, cw)],
        out_shape=[jax.ShapeDtypeStruct((s, d), F32), jax.ShapeDtypeStruct((s, d), F32),
                   jax.ShapeDtypeStruct((s, d), BF16), jax.ShapeDtypeStruct((s, cw), BF16)],
        scratch_shapes=[pltpu.VMEM(w_ao.shape, BF16), pltpu.VMEM(w_co.shape, BF16), pltpu.VMEM(w_o.shape, BF16),
                        pltpu.SemaphoreType.DMA((3,))],
        compiler_params=_params(1),
    )(x, o, conv, conv, gate, wconv, g_post, w_ao, w_co, w_o)


def _mix_bwd(dx1, mixed, o, conv, gate, wconv, g_post, w_ao, w_co, w_o, tm):
    s, d = dx1.shape
    aw, cw = w_ao.shape[0], w_co.shape[0]
    n = s // tm
    per = tm // HALO_BF16

    def body(dx1_ref, mixed_ref, o_ref, conv_ref, prev_ref, gate_ref, wc_ref, g_ref, wao_hbm, wco_hbm, wo_hbm,
             dmixed_ref, dattn_ref, dconvout_ref, do_ref, drest_ref, dg_ref, dbias_ref, dwc_ref,
             wao, wco, wo, dcv_next, sem):
        i = pl.program_id(0)
        _load_resident([(wao_hbm, wao), (wco_hbm, wco), (wo_hbm, wo)], sem)

        @pl.when(i == 0)
        def _():
            dg_ref[...] = jnp.zeros_like(dg_ref)
            dbias_ref[...] = jnp.zeros_like(dbias_ref)
            dwc_ref[...] = jnp.zeros_like(dwc_ref)
            dcv_next[...] = jnp.zeros_like(dcv_next)

        mixed = mixed_ref[...]
        r = _rms_scale(mixed)
        mhat = mixed * r
        dn = dx1_ref[...]
        dg_ref[...] += jnp.sum(dn * mhat, axis=0, keepdims=True)
        dmixed = _rms_bwd(mhat, r, g_ref[...], dn).astype(BF16)
        dmixed_ref[...] = dmixed
        dmi = _dot_nt(dmixed, wo[...])

        wc = wc_ref[...]
        conv = conv_ref[...].astype(F32)
        y_attn, y_conv, _, cb, cv, cm, cm1, cm2 = _branches(
            o_ref[...].astype(BF16), conv, prev_ref[...], wc, wao[...], wco[...], cw, i == n - 1)
        ga = gate_ref[:, 0:d].astype(F32)
        gc = gate_ref[:, d:2 * d].astype(F32)
        dpre_a = dmi * y_attn * ga * (1.0 - ga)
        dpre_c = dmi * y_conv * gc * (1.0 - gc)
        drest_ref[:, 3 * cw:3 * cw + d] = dpre_a.astype(BF16)
        drest_ref[:, 3 * cw + d:3 * cw + 2 * d] = dpre_c.astype(BF16)
        dbias_ref[:, 0:d] += jnp.sum(dpre_a, axis=0, keepdims=True)
        dbias_ref[:, d:2 * d] += jnp.sum(dpre_c, axis=0, keepdims=True)

        dattn = (dmi * ga).astype(BF16)
        dattn_ref[...] = dattn
        do_ref[...] = _dot_nt(dattn, wao[...]).astype(BF16)
        dconvout = (dmi * gc).astype(BF16)
        dconvout_ref[...] = dconvout
        dconv_in = _dot_nt(dconvout, wco[...])
        drest_ref[:, 0:cw] = (dconv_in * cv).astype(BF16)

        dcv = dconv_in * cb
        following = dcv_next[...]
        dcm = wc[2:3, :] * dcv + wc[1:2, :] * _shift_up(dcv, following, 1) + wc[0:1, :] * _shift_up(dcv, following, 2)
        drest_ref[:, cw:2 * cw] = (dcm * conv[:, 2 * cw:3 * cw]).astype(BF16)
        drest_ref[:, 2 * cw:3 * cw] = (dcm * conv[:, cw:2 * cw]).astype(BF16)
        for tap, shifted in enumerate((cm2, cm1, cm)):
            dwc_ref[tap:tap + 1, :] += jnp.sum(dcv * shifted, axis=0, keepdims=True)
        dcv_next[...] = dcv[0:HALO, :]

    def rows(width):
        return pl.BlockSpec((tm, width), lambda i: (n - 1 - i, 0))

    prev_halo = pl.BlockSpec((HALO_BF16, 3 * cw), lambda i: (jnp.maximum((n - 1 - i) * per - 1, 0), 0))
    n_rest = 3 * cw + 2 * d
    return pl.pallas_call(
        body, name="mix_bwd", grid=(n,),
        in_specs=[rows(d), rows(d), rows(aw), rows(3 * cw), prev_halo, rows(2 * d), _const_spec((CONV_K, cw)),
                  _const_spec((1, d)), ANY, ANY, ANY],
        out_specs=[rows(d), rows(d), rows(d), rows(aw), rows(n_rest), _const_spec((1, d)), _const_spec((1, 2 * d)),
                   _const_spec((CONV_K, cw))],
        out_shape=[jax.ShapeDtypeStruct((s, d), BF16), jax.ShapeDtypeStruct((s, d), BF16),
                   jax.ShapeDtypeStruct((s, d), BF16), jax.ShapeDtypeStruct((s, aw), BF16),
                   jax.ShapeDtypeStruct((s, n_rest), BF16), jax.ShapeDtypeStruct((1, d), F32),
                   jax.ShapeDtypeStruct((1, 2 * d), F32), jax.ShapeDtypeStruct((CONV_K, cw), F32)],
        scratch_shapes=[pltpu.VMEM(w_ao.shape, BF16), pltpu.VMEM(w_co.shape, BF16), pltpu.VMEM(w_o.shape, BF16),
                        pltpu.VMEM((HALO, cw), F32), pltpu.SemaphoreType.DMA((3,))],
        compiler_params=_params(1),
    )(dx1, mixed, o, conv, conv, gate, wconv, g_post, w_ao, w_co, w_o)


def _mlp_ple_loss(x1, p, target, g_pre, g_post, g_ple, w_up, w_dn, w_pg, w_pp, tm):
    s, d = x1.shape
    ff = w_up.shape[1]
    pd = p.shape[1]
    fc = FF_CHUNK

    def body(x1_ref, p_ref, t_ref, gpre_ref, gpost_ref, gple_ref, wup_hbm, wdn_hbm, wpg_hbm, wpp_hbm,
             dx1_ref, h2_ref, du_ref, a_ref, df_ref, h3_ref, ds3_ref, dpp_ref, loss_ref, dgpre_ref, dgpost_ref,
             dgple_ref, wup, wdn, wpg, wpp, u_scr, sem):
        _load_resident([(wup_hbm, wup), (wdn_hbm, wdn), (wpg_hbm, wpg), (wpp_hbm, wpp)], sem)

        @pl.when(pl.program_id(0) == 0)
        def _():
            for ref in (loss_ref, dgpre_ref, dgpost_ref, dgple_ref):
                ref[...] = jnp.zeros_like(ref)

        x1v = x1_ref[...]
        r2 = _rms_scale(x1v)
        x1hat = x1v * r2
        h2 = (x1hat * gpre_ref[...]).astype(BF16)
        h2_ref[...] = h2
        f = jnp.zeros((tm, d), F32)
        for c0 in range(0, ff, fc):
            u = _dot(h2, wup[:, c0:c0 + fc])
            u_scr[:, c0:c0 + fc] = u
            a = jnp.square(jnp.maximum(u, 0.0)).astype(BF16)
            a_ref[:, c0:c0 + fc] = a
            f = f + _dot(a, wdn[c0:c0 + fc, :])
        rf = _rms_scale(f)
        fhat = f * rf
        x2 = x1v + fhat * gpost_ref[...]
        r3 = _rms_scale(x2)
        x2hat = x2 * r3
        h3 = (x2hat * gple_ref[...]).astype(BF16)
        h3_ref[...] = h3
        pg = _sigmoid(_dot(h3, wpg[...]))
        pp = _dot(p_ref[...].astype(BF16), wpp[...])
        diff = x2 + pg * pp - t_ref[...]
        loss_ref[...] += 0.5 * jnp.sum(jnp.mean(diff * diff, axis=-1, keepdims=True), axis=0, keepdims=True)

        dy = diff * (1.0 / d)
        dpp_ref[...] = (dy * pg).astype(BF16)
        ds3 = (dy * pp * pg * (1.0 - pg)).astype(BF16)
        ds3_ref[...] = ds3
        dh3 = _dot_nt(ds3, wpg[...])
        dgple_ref[...] += jnp.sum(dh3 * x2hat, axis=0, keepdims=True)
        dx2 = dy + _rms_bwd(x2hat, r3, gple_ref[...], dh3)
        dgpost_ref[...] += jnp.sum(dx2 * fhat, axis=0, keepdims=True)
        df = _rms_bwd(fhat, rf, gpost_ref[...], dx2).astype(BF16)
        df_ref[...] = df
        dh2 = jnp.zeros((tm, d), F32)
        for c0 in range(0, ff, fc):
            da = _dot_nt(df, wdn[c0:c0 + fc, :])
            du = (da * (2.0 * jnp.maximum(u_scr[:, c0:c0 + fc], 0.0))).astype(BF16)
            du_ref[:, c0:c0 + fc] = du
            dh2 = dh2 + _dot_nt(du, wup[:, c0:c0 + fc])
        dgpre_ref[...] += jnp.sum(dh2 * x1hat, axis=0, keepdims=True)
        dx1_ref[...] = dx2 + _rms_bwd(x1hat, r2, gpre_ref[...], dh2)

    vec = _const_spec((1, d))
    return pl.pallas_call(
        body, name="mlp_ple_loss", grid=(s // tm,),
        in_specs=[_row_spec(tm, d), _row_spec(tm, pd), _row_spec(tm, d), vec, vec, vec, ANY, ANY, ANY, ANY],
        out_specs=[_row_spec(tm, d), _row_spec(tm, d), _row_spec(tm, ff), _row_spec(tm, ff), _row_spec(tm, d),
                   _row_spec(tm, d), _row_spec(tm, d), _row_spec(tm, d), _const_spec((1, 1)), vec, vec, vec],
        out_shape=[jax.ShapeDtypeStruct((s, d), F32), jax.ShapeDtypeStruct((s, d), BF16),
                   jax.ShapeDtypeStruct((s, ff), BF16), jax.ShapeDtypeStruct((s, ff), BF16),
                   jax.ShapeDtypeStruct((s, d), BF16), jax.ShapeDtypeStruct((s, d), BF16),
                   jax.ShapeDtypeStruct((s, d), BF16), jax.ShapeDtypeStruct((s, d), BF16),
                   jax.ShapeDtypeStruct((1, 1), F32), jax.ShapeDtypeStruct((1, d), F32),
                   jax.ShapeDtypeStruct((1, d), F32), jax.ShapeDtypeStruct((1, d), F32)],
        scratch_shapes=[pltpu.VMEM(w_up.shape, BF16), pltpu.VMEM(w_dn.shape, BF16), pltpu.VMEM(w_pg.shape, BF16),
                        pltpu.VMEM(w_pp.shape, BF16), pltpu.VMEM((tm, ff), F32), pltpu.SemaphoreType.DMA((4,))],
        compiler_params=_params(1),
    )(x1, p, target, g_pre, g_post, g_ple, w_up, w_dn, w_pg, w_pp)


def _in_proj_bwd(x, dx1, pieces, g1, w_in, tm, exchange=None):
    s, d = x.shape
    ni = w_in.shape[1]
    widths = [p.shape[1] for p in pieces]
    grid = (s // tm,)
    ex = exchange or _NO_EXCHANGE

    def body(x_ref, dx1_ref, *rest):
        piece_refs, rest = rest[:len(pieces)], rest[len(pieces):]
        g_ref, w_hbm = rest[0], rest[1]
        ex_in, (dx_ref, dg_ref), ex_out, (w_vmem, sem, *sems) = _split_refs(rest[2:], ex, 2)
        _exchange_start(ex, ex_in, ex_out, sems, grid)
        _load_resident([(w_hbm, w_vmem)], sem)

        @pl.when(pl.program_id(0) == 0)
        def _():
            dg_ref[...] = jnp.zeros_like(dg_ref)

        dh = jnp.zeros((tm, d), F32)
        c0 = 0
        for ref, width in zip(piece_refs, widths):
            dh = dh + _dot_nt(ref[...], w_vmem[:, c0:c0 + width])
            c0 += width
        xv = x_ref[...]
        r = _rms_scale(xv)
        xhat = xv * r
        dg_ref[...] += jnp.sum(dh * xhat, axis=0, keepdims=True)
        dx_ref[...] = dx1_ref[...] + _rms_bwd(xhat, r, g_ref[...], dh)
        _exchange_wait(ex, ex_in, ex_out, sems, grid)

    return pl.pallas_call(
        body, name="in_proj_bwd", grid=grid,
        in_specs=[_row_spec(tm, d), _row_spec(tm, d)] + [_row_spec(tm, w) for w in widths]
        + [_const_spec((1, d)), ANY] + [ANY] * len(ex.arrays),
        out_specs=[_row_spec(tm, d), _const_spec((1, d))] + [ANY] * len(ex.out_shapes),
        out_shape=[jax.ShapeDtypeStruct((s, d), F32), jax.ShapeDtypeStruct((1, d), F32)] + ex.out_shapes,
        scratch_shapes=[pltpu.VMEM((d, ni), BF16), pltpu.SemaphoreType.DMA((1,))] + _exchange_sems(ex),
        compiler_params=_params(1),
    )(x, dx1, *pieces, g1, w_in, *ex.arrays)


def _weight_grad(a, b, name, into=None, col0=0, n_total=None):
    s, m = a.shape
    n = b.shape[1]
    tm = min(m, DW_TILE)
    tn = min(n, DW_TILE) if n_total is None else DW_PIECE_TILE
    tk = min(s, DW_TOKENS * (2 if (m // tm) * (n // tn) >= DW_MANY_TILES else 1))
    nk = s // tk
    j0 = col0 // tn
    assert m % tm == 0 and n % tn == 0 and col0 % tn == 0

    def body(a_ref, b_ref, *rest):
        o_ref, acc = rest[-2:]
        k = pl.program_id(2)

        @pl.when(k == 0)
        def _():
            acc[...] = jnp.zeros_like(acc)

        acc[...] += _dot_tn(a_ref[...].astype(BF16), b_ref[...].astype(BF16))

        @pl.when(k == nk - 1)
        def _():
            o_ref[...] = acc[...].astype(BF16)

    extra = [] if into is None else [into]
    return pl.pallas_call(
        body, name=name, grid=(m // tm, n // tn, nk),
        in_specs=[pl.BlockSpec((tk, tm), lambda i, j, k: (k, i)), pl.BlockSpec((tk, tn), lambda i, j, k: (k, j))]
        + [ANY] * len(extra),
        out_specs=pl.BlockSpec((tm, tn), lambda i, j, k: (i, j0 + j)),
        out_shape=jax.ShapeDtypeStruct((m, n_total or n), BF16),
        input_output_aliases={2: 0} if extra else {},
        scratch_shapes=[pltpu.VMEM((tm, tn), F32)],
        compiler_params=_params(3),
    )(a, b, *extra)


def _mesh_position():
    return tuple(lax.axis_index(a) for a in MESH_AXES)


def _peer(me, k):
    bits = ((k >> 2) & 1, (k >> 1) & 1, k & 1)
    pos = tuple(1 - m if b else m for m, b in zip(me, bits))
    return pos, 4 * pos[0] + 2 * pos[1] + pos[2]


class _Exchange:
    def __init__(self, arrays, out_shapes, src, dst, relayed=None):
        self.arrays, self.out_shapes, self.src, self.dst = list(arrays), list(out_shapes), src, dst
        self.relayed = list(relayed) if relayed is not None else [False] * len(self.arrays)


_NO_EXCHANGE = _Exchange([], [], None, None)


def _exchange_sems(ex):
    n = len(ex.arrays)
    if n == 0:
        return []
    return [pltpu.SemaphoreType.DMA((n, N_DEV - 1)), pltpu.SemaphoreType.DMA((n, N_DEV - 1)),
            pltpu.SemaphoreType.DMA((n,))]


def _split_refs(rest, ex, n_own_outs):
    n_in, n_out = len(ex.arrays), len(ex.out_shapes)
    ex_in, rest = rest[:n_in], rest[n_in:]
    own, rest = rest[:n_own_outs], rest[n_own_outs:]
    return ex_in, own, rest[:n_out], rest[n_out:]


def _direct_steps(ex, w, in_refs, out_refs, sems):
    send_sems, recv_sems, local_sems = sems
    me = _mesh_position()
    mine = 4 * me[0] + 2 * me[1] + me[2]

    def copy(k):
        landing = ex.dst(w, out_refs, mine)
        if k == 0:
            return pltpu.make_async_copy(ex.src(w, in_refs, mine), landing, local_sems.at[w])
        peer, peer_idx = _peer(me, k)
        return pltpu.make_async_remote_copy(
            src_ref=ex.src(w, in_refs, peer_idx), dst_ref=landing, send_sem=send_sems.at[w, k - 1],
            recv_sem=recv_sems.at[w, k - 1], device_id=peer, device_id_type=pl.DeviceIdType.MESH)

    ks = range(N_DEV)
    return [lambda k=k: copy(k).start() for k in ks], [], [lambda k=k: copy(k).wait() for k in ks]


def _relayed_steps(ex, w, in_refs, out_refs, sems):
    send_sems, recv_sems, local_sems = sems
    x, y, c = _mesh_position()
    chips = [(1 - x, y), (x, 1 - y), (1 - x, 1 - y)]
    sibling = (x, y, 1 - c)
    js = range(len(chips))

    def block(px, py, pc):
        return ex.dst(w, out_refs, 4 * px + 2 * py + pc)

    def copy(k, dst, to, src=None):
        return pltpu.make_async_remote_copy(
            src_ref=ex.src(w, in_refs, None) if src is None else src, dst_ref=dst, send_sem=send_sems.at[w, k],
            recv_sem=recv_sems.at[w, k], device_id=to, device_id_type=pl.DeviceIdType.MESH)

    def local():
        return pltpu.make_async_copy(ex.src(w, in_refs, None), block(x, y, c), local_sems.at[w])

    def own(k):
        return copy(k, block(x, y, c), sibling if k == 0 else (*chips[k - 1], c))

    def came(j):
        return copy(1 + j, block(*chips[j], c), (*chips[j], c))

    def passed(j):
        return copy(4 + j, block(*chips[j], c), sibling, src=block(*chips[j], c))

    def from_sibling(k):
        return copy(k, block(x, y, 1 - c) if k == 0 else block(*chips[k - 4], 1 - c), sibling)

    start = [lambda: local().start()] + [lambda k=k: own(k).start() for k in range(4)]
    relay = [step for j in js for step in (lambda j=j: came(j).wait_recv(), lambda j=j: passed(j).start())]
    finish = ([lambda: local().wait()] + [lambda k=k: own(k).wait_send() for k in range(4)]
              + [lambda j=j: passed(j).wait_send() for j in js]
              + [lambda k=k: from_sibling(k).wait_recv() for k in (0, 4, 5, 6)])
    return start, relay, finish


def _exchange_steps(ex, in_refs, out_refs, sems):
    start, relay, finish = [], [], []
    for w in range(len(ex.arrays)):
        steps = (_relayed_steps if ex.relayed[w] else _direct_steps)(ex, w, in_refs, out_refs, sems)
        start += steps[0]
        relay += steps[1]
        finish += steps[2]
    return start, relay, finish


def _run(steps):
    for step in steps:
        step()


def _at_grid_step(grid, where):
    target = {"first": [0] * len(grid), "middle": [grid[0] // 2] + [0] * (len(grid) - 1),
              "last": [g - 1 for g in grid]}[where]
    hit = pl.program_id(0) == target[0]
    for axis in range(1, len(grid)):
        hit = jnp.logical_and(hit, pl.program_id(axis) == target[axis])
    return hit


def _exchange_start(ex, in_refs, out_refs, sems, grid):
    if ex.arrays:
        @pl.when(_at_grid_step(grid, "first"))
        def _():
            _run(_exchange_steps(ex, in_refs, out_refs, sems)[0])

        if any(ex.relayed):
            assert grid[0] >= 2

            @pl.when(_at_grid_step(grid, "middle"))
            def _():
                _run(_exchange_steps(ex, in_refs, out_refs, sems)[1])


def _exchange_wait(ex, in_refs, out_refs, sems, grid):
    if ex.arrays:
        @pl.when(_at_grid_step(grid, "last"))
        def _():
            _run(_exchange_steps(ex, in_refs, out_refs, sems)[2])


def _shard_block(ref, shard_shape, by_col, idx):
    r, c = shard_shape
    if by_col:
        return ref.at[:, pl.ds(pl.multiple_of(idx * c, LANES), c)]
    return ref.at[pl.ds(pl.multiple_of(idx * r, 16), r), :]


def _full_shape(shard_shape, by_col):
    r, c = shard_shape
    return (r, N_DEV * c) if by_col else (N_DEV * r, c)


def _gather_exchange(shards, col_sharded):
    shapes = [a.shape for a in shards]
    return _Exchange(
        shards, [jax.ShapeDtypeStruct(_full_shape(sh, bc), a.dtype) for a, sh, bc in zip(shards, shapes, col_sharded)],
        lambda w, refs, idx: refs[w],
        lambda w, refs, idx: _shard_block(refs[w], shapes[w], col_sharded[w], idx), [True] * len(shards))


def _scatter_exchange(grads, col_sharded):
    shapes = []
    for g, by_col in zip(grads, col_sharded):
        r, c = g.shape
        shapes.append((r, c // N_DEV) if by_col else (r // N_DEV, c))
    return _Exchange(
        grads, [jax.ShapeDtypeStruct((N_DEV,) + sh, g.dtype) for g, sh in zip(grads, shapes)],
        lambda w, refs, idx: _shard_block(refs[w], shapes[w], col_sharded[w], idx),
        lambda w, refs, mine: refs[w].at[mine])


def _broadcast_exchange(arrays):
    return _Exchange(arrays, [jax.ShapeDtypeStruct((N_DEV,) + a.shape, a.dtype) for a in arrays],
                     lambda w, refs, idx: refs[w], lambda w, refs, mine: refs[w].at[mine])


def _join(*exs):
    arrays, shapes, owner = [], [], []
    for e in exs:
        for w in range(len(e.arrays)):
            owner.append((e, w, len(arrays), len(shapes)))
        arrays += e.arrays
        shapes += e.out_shapes

    def src(w, refs, idx):
        e, w0, i0, _ = owner[w]
        return e.src(w0, refs[i0:i0 + len(e.arrays)], idx)

    def dst(w, refs, idx):
        e, w0, _, o0 = owner[w]
        return e.dst(w0, refs[o0:o0 + len(e.out_shapes)], idx)

    return _Exchange(arrays, shapes, src, dst, [flag for e in exs for flag in e.relayed])


def _exchange_call(ex, name):
    n_in = len(ex.arrays)

    def body(*refs):
        in_refs, _, out_refs, sems = _split_refs(refs, ex, 0)
        for steps in _exchange_steps(ex, in_refs, out_refs, sems):
            _run(steps)

    return pl.pallas_call(
        body, name=name, in_specs=[ANY] * n_in, out_specs=[ANY] * len(ex.out_shapes), out_shape=ex.out_shapes,
        scratch_shapes=_exchange_sems(ex), compiler_params=pltpu.CompilerParams(vmem_limit_bytes=VMEM_LIMIT),
    )(*ex.arrays)


def _to_bf16(arrays):
    def body(*refs):
        for src, dst in zip(refs[:len(arrays)], refs[len(arrays):]):
            dst[...] = src[...].astype(BF16)

    vmem = pl.BlockSpec(memory_space=pltpu.VMEM)
    return pl.pallas_call(
        body, name="weights_to_bf16", in_specs=[vmem] * len(arrays), out_specs=[vmem] * len(arrays),
        out_shape=[jax.ShapeDtypeStruct(a.shape, BF16) for a in arrays],
        compiler_params=pltpu.CompilerParams(vmem_limit_bytes=VMEM_LIMIT),
    )(*arrays)


def _adamw(w, g, m, v):
    m = ADAM_B1 * m + (1.0 - ADAM_B1) * g
    v = ADAM_B2 * v + (1.0 - ADAM_B2) * jnp.square(g)
    m_hat = m / (1.0 - ADAM_B1 ** ADAM_STEP)
    v_hat = v / (1.0 - ADAM_B2 ** ADAM_STEP)
    delta = -ADAM_LR * (m_hat / (jnp.sqrt(v_hat) + ADAM_EPS) + ADAM_WD * w)
    return delta, m, v


def _sum_and_adamw(parts, w, m, v, name):
    r, c = w.shape
    tr = min(r, 256)

    def body(p_ref, w_ref, m_ref, v_ref, g_out, d_out, m_out, v_out):
        g = p_ref[0].astype(F32)
        for dev in range(1, N_DEV):
            g = g + p_ref[dev].astype(F32)
        g_out[...] = g
        d_out[...], m_out[...], v_out[...] = _adamw(w_ref[...], g, m_ref[...], v_ref[...])

    blk = pl.BlockSpec((tr, c), lambda i: (i, 0))
    return pl.pallas_call(
        body, name=name, grid=(r // tr,),
        in_specs=[pl.BlockSpec((N_DEV, tr, c), lambda i: (0, i, 0)), blk, blk, blk],
        out_specs=[blk] * 4, out_shape=[jax.ShapeDtypeStruct((r, c), F32)] * 4,
        compiler_params=_params(1),
    )(parts, w, m, v)


BIG = ("w_in", "w_attn_out", "w_conv_out", "w_o", "w_up", "w_down", "w_ple_gate", "w_ple_proj")
COL_SHARDED = {"w_in": True, "w_attn_out": True, "w_conv_out": True, "w_o": False, "w_up": True, "w_down": False,
               "w_ple_gate": False, "w_ple_proj": True}
SMALL = ("g_pre_mix", "b_gate", "g_post_mix", "g_pre_mlp", "g_post_mlp", "g_ple")


REST = BIG[1:]


def _local_grads(x, p, target, small, wconv, full, aw, cw, tm, t, gather_rest=None, scatter_rest=None,
                 scatter_in=None):
    full = dict(full)
    tm_wide = min(WIDE_BLOCKS * tm, x.shape[0])
    qkv, conv, gate, h1 = _in_proj_fwd(x, small["g_pre_mix"], small["b_gate"], full["w_in"], aw, cw, tm_wide)
    o, o_b, *rest = _attn_fwd(qkv, aw, t, gather_rest)
    full.update(zip(REST, rest))
    x1, mixed, mix_in, conv_in = _mix_fwd(x, o_b, conv, gate, wconv, small["g_post_mix"], full["w_attn_out"],
                                          full["w_conv_out"], full["w_o"], tm_wide)
    (dx1, h2, du, a, df, h3, ds3, dpp, loss, dg_pre_mlp, dg_post_mlp, dg_ple) = _mlp_ple_loss(
        x1, p, target, small["g_pre_mlp"], small["g_post_mlp"], small["g_ple"], full["w_up"], full["w_down"],
        full["w_ple_gate"], full["w_ple_proj"], tm)
    big = {"w_up": _weight_grad(h2, du, "dw_up"), "w_down": _weight_grad(a, df, "dw_down"),
           "w_ple_gate": _weight_grad(h3, ds3, "dw_ple_gate"), "w_ple_proj": _weight_grad(p, dpp, "dw_ple_proj")}
    (dmixed, dattn, dconvout, do, drest, dg_post_mix, db_gate, dwconv) = _mix_bwd(
        dx1, mixed, o_b, conv, gate, wconv, small["g_post_mix"], full["w_attn_out"], full["w_conv_out"], full["w_o"],
        tm_wide)
    big.update({"w_attn_out": _weight_grad(o_b, dattn, "dw_attn_out"),
                "w_conv_out": _weight_grad(conv_in, dconvout, "dw_conv_out"),
                "w_o": _weight_grad(mix_in, dmixed, "dw_o")})
    dq, dk, dv, *scattered = _attn_bwd(qkv, o, do, aw, t, scatter_rest and scatter_rest([big[n] for n in REST]))
    pieces = [dq, dk, dv, drest]
    dw_in, col0, ni = None, 0, full["w_in"].shape[1]
    for i, piece in enumerate(pieces):
        dw_in = _weight_grad(h1, piece, "dw_in_%d" % i, dw_in, col0, ni)
        col0 += piece.shape[1]
    big["w_in"] = dw_in
    dx, dg_pre_mix, *scattered_in = _in_proj_bwd(x, dx1, pieces, small["g_pre_mix"], full["w_in"], tm_wide,
                                                scatter_in and scatter_in(dw_in))
    small_grads = {"g_pre_mix": dg_pre_mix, "b_gate": db_gate, "g_post_mix": dg_post_mix, "g_pre_mlp": dg_pre_mlp,
                   "g_post_mlp": dg_post_mlp, "g_ple": dg_ple, "w_conv": dwconv}
    return loss[0, 0], dx, big, small_grads, scattered_in + scattered


PACK_ROWS = 16


def _pack_layout(shapes, d):
    slots, at = [], 0
    for i, (r, c) in enumerate(shapes):
        assert d % c == 0
        for row in range(r):
            slots.append((i, row, at // d, at % d))
            at += c
        at = -(-at // d) * d
    assert at <= PACK_ROWS * d
    return slots


def _pack_small(groups, d):
    shapes = [a.shape for a in groups[0]]
    slots = _pack_layout(shapes, d)
    n = len(shapes)

    def body(*refs):
        ins, outs = refs[:n * len(groups)], refs[n * len(groups):]
        for g, out in enumerate(outs):
            out[...] = jnp.zeros_like(out)
            for i, row, pr, pc in slots:
                src = ins[g * n + i]
                out[pr:pr + 1, pc:pc + shapes[i][1]] = src[row:row + 1, :]

    vmem = pl.BlockSpec(memory_space=pltpu.VMEM)
    return pl.pallas_call(
        body, name="pack_small", in_specs=[vmem] * (n * len(groups)), out_specs=[vmem] * len(groups),
        out_shape=[jax.ShapeDtypeStruct((PACK_ROWS, d), F32)] * len(groups),
    )(*[a for group in groups for a in group])


def _unpack_small(pack, shapes, d):
    slots = _pack_layout(shapes, d)
    return [jnp.stack([pack[pr, pc:pc + shapes[i][1]] for j, row, pr, pc in slots if j == i])
            for i in range(len(shapes))]


def kernel(x, p, g_pre_mix, w_in, b_gate, w_conv, w_attn_out, w_conv_out, w_o, g_post_mix, g_pre_mlp, w_up, w_down, g_post_mlp, g_ple, w_ple_gate, w_ple_proj, loss_target, m_g_pre_mix, m_w_in, m_b_gate, m_w_conv, m_w_attn_out, m_w_conv_out, m_w_o, m_g_post_mix, m_g_pre_mlp, m_w_up, m_w_down, m_g_post_mlp, m_g_ple, m_w_ple_gate, m_w_ple_proj, v_g_pre_mix, v_w_in, v_b_gate, v_w_conv, v_w_attn_out, v_w_conv_out, v_w_o, v_g_post_mix, v_g_pre_mlp, v_w_up, v_w_down, v_g_post_mlp, v_g_ple, v_w_ple_gate, v_w_ple_proj):
    given = dict(locals())
    order = ["g_pre_mix", "w_in", "b_gate", "w_conv", "w_attn_out", "w_conv_out", "w_o", "g_post_mix", "g_pre_mlp",
             "w_up", "w_down", "g_post_mlp", "g_ple", "w_ple_gate", "w_ple_proj"]
    d = x.shape[-1]
    me = 4 * lax.axis_index("x") + 2 * lax.axis_index("y") + lax.axis_index("c")

    col = [COL_SHARDED[n] for n in BIG]
    shards = _to_bf16([given[n][0] for n in BIG])
    cw_shard = w_conv.shape[-1]
    conv_tile = jnp.pad(w_conv[0], ((0, HALO - CONV_K), (0, LANES - cw_shard)))
    w_in_full, conv_g = _exchange_call(
        _join(_gather_exchange(shards[:1], col[:1]), _broadcast_exchange([conv_tile])), "gather_w_in")
    wconv = jnp.concatenate([conv_g[dev, :CONV_K, :cw_shard] for dev in range(N_DEV)], axis=1)

    small = {n: given[n] for n in SMALL}
    loss, dx, big_grads, small_grads, parts = _local_grads(
        x[0], p[0, 0], loss_target[0], small, wconv, {"w_in": w_in_full}, w_attn_out.shape[1], w_conv_out.shape[1],
        ROW_BLOCK, ATTN_BLOCK,
        _gather_exchange(shards[1:], col[1:]), lambda grads: _scatter_exchange(grads, col[1:]),
        lambda grad: _scatter_exchange([grad], col[:1]))
    small_names = list(SMALL) + ["w_conv"]
    two_d = lambda a: a.reshape(-1, d) if a.shape[-1] > d else a.reshape(-1, a.shape[-1])
    full_conv = lambda a: lax.dynamic_update_slice(jnp.zeros((CONV_K, N_DEV * cw_shard), F32), a[0],
                                                   (jnp.int32(0), me * cw_shard))
    groups = [[two_d(small_grads[n]) for n in small_names] + [loss.reshape(1, 1)]]
    for pre in ("", "m_", "v_"):
        groups.append([two_d(given[pre + n]) for n in SMALL] + [full_conv(given[pre + "w_conv"]), jnp.zeros((1, 1), F32)])
    pack, *state = _pack_small(groups, d)
    packs, = _exchange_call(_broadcast_exchange([pack]), "share_small_grads")

    grads, deltas, new_m, new_v = {}, {}, {}, {}
    for n, part in zip(BIG, parts):
        grads[n], deltas[n], new_m[n], new_v[n] = (
            a[None] for a in _sum_and_adamw(part, given[n][0], given["m_" + n][0], given["v_" + n][0], "adamw_" + n))

    outs = _sum_and_adamw(packs, *state, "adamw_small")
    shapes = [a.shape for a in groups[0]]
    for res, dst in zip(outs, (grads, deltas, new_m, new_v)):
        for n, a in zip(small_names + ["loss"], _unpack_small(res, shapes, d)):
            if n == "w_conv":
                a = lax.dynamic_slice(a, (jnp.int32(0), me * cw_shard), (CONV_K, cw_shard))[None]
            dst[n] = a.reshape(given[n].shape) if n in SMALL else a
    loss = grads["loss"][0, 0]

    return (loss, dx[None], *[grads[n] for n in order], *[deltas[n] for n in order],
            *[new_m[n] for n in order], *[new_v[n] for n in order])
```

```python
import jax
import jax.numpy as jnp
from jax import lax
from jax.experimental import pallas as pl
from jax.experimental.pallas import tpu as pltpu

F32 = jnp.float32
BF16 = jnp.bfloat16
RMS_EPS = 1e-6
N_DEV = 8
MESH_AXES = ("x", "y", "c")
LANES = 128
HEAD_DIM = 64
HEADS_PER_GROUP = LANES // HEAD_DIM
CONV_K = 3
HALO = 8
HALO_BF16 = 16
VMEM_LIMIT = 56 * 1024 * 1024
EXP2_ZERO = -150.0
LOG2_E = 1.4426950408889634

ADAM_LR = 0.001
ADAM_B1 = 0.9
ADAM_B2 = 0.999
ADAM_EPS = 1e-08
ADAM_WD = 0.01
ADAM_STEP = 10

ROW_BLOCK = 256
WIDE_BLOCKS = 2
ATTN_BLOCK = 256
ATTN_ROW_SPLITS = 2
DW_TOKENS = 2048
DW_TILE = 1024
DW_MANY_TILES = 6
DW_PIECE_TILE = 512
FF_CHUNK = 1024
PROJ_CHUNK = 512


def _dot(a, b):
    return lax.dot_general(a, b, (((1,), (0,)), ((), ())), preferred_element_type=F32)


def _dot_nt(a, b):
    return lax.dot_general(a, b, (((1,), (1,)), ((), ())), preferred_element_type=F32)


def _dot_tn(a, b):
    return lax.dot_general(a, b, (((0,), (0,)), ((), ())), preferred_element_type=F32)


def _sigmoid(z):
    return 1.0 / (1.0 + jnp.exp(-z))


def _rms_scale(x):
    return lax.rsqrt(jnp.mean(x * x, axis=-1, keepdims=True) + RMS_EPS)


def _rms_bwd(xhat, r, g, dy):
    gd = dy * g
    return r * (gd - xhat * jnp.mean(gd * xhat, axis=-1, keepdims=True))


def _params(n_axes, **kw):
    return pltpu.CompilerParams(dimension_semantics=("arbitrary",) * n_axes, vmem_limit_bytes=VMEM_LIMIT, **kw)


def _load_resident(pairs, sem):
    @pl.when(pl.program_id(0) == 0)
    def _():
        copies = [pltpu.make_async_copy(src, dst, sem.at[i]) for i, (src, dst) in enumerate(pairs)]
        for cp in copies:
            cp.start()
        for cp in copies:
            cp.wait()


def _row_spec(tm, width):
    return pl.BlockSpec((tm, width), lambda i: (i, 0))


def _prev_halo_spec(tm, width, rows):
    per = tm // rows
    return pl.BlockSpec((rows, width), lambda i: (jnp.maximum(i * per - 1, 0), 0))


def _const_spec(shape):
    return pl.BlockSpec(shape, lambda i: (0,) * len(shape))


ANY = pl.BlockSpec(memory_space=pl.ANY)


def _shift_down(cur, prev, n):
    rows = lax.broadcasted_iota(jnp.int32, cur.shape, 0)
    out = pltpu.roll(cur, n, 0)
    for j in range(n):
        out = jnp.where(rows == j, prev[prev.shape[0] - n + j:prev.shape[0] - n + j + 1, :], out)
    return out


def _shift_up(cur, nxt, n):
    tm = cur.shape[0]
    rows = lax.broadcasted_iota(jnp.int32, cur.shape, 0)
    out = pltpu.roll(cur, tm - n, 0)
    for j in range(n):
        out = jnp.where(rows == tm - n + j, nxt[j:j + 1, :], out)
    return out


def _conv_taps(cm, cm_prev, wconv):
    cm1 = _shift_down(cm, cm_prev, 1)
    cm2 = _shift_down(cm, cm_prev, 2)
    cv = wconv[2:3, :] * cm + wconv[1:2, :] * cm1 + wconv[0:1, :] * cm2
    return cv, cm1, cm2


def _in_proj_fwd(x, g1, b_gate, w_in, aw, cw, tm):
    s, d = x.shape
    ni = w_in.shape[1]
    n_qkv, n_conv = 3 * aw, 3 * cw
    ch = PROJ_CHUNK

    def body(x_ref, g_ref, b_ref, w_hbm, qkv_ref, conv_ref, gate_ref, h_ref, w_vmem, sem):
        _load_resident([(w_hbm, w_vmem)], sem)
        xv = x_ref[...]
        h = (xv * _rms_scale(xv) * g_ref[...]).astype(BF16)
        h_ref[...] = h
        for c0 in range(0, ni, ch):
            pc = _dot(h, w_vmem[:, c0:c0 + ch])
            if c0 < n_qkv:
                qkv_ref[:, c0:c0 + ch] = pc.astype(BF16)
            elif c0 < n_qkv + n_conv:
                conv_ref[:, c0 - n_qkv:c0 - n_qkv + ch] = pc.astype(BF16)
            else:
                g0 = c0 - n_qkv - n_conv
                gate_ref[:, g0:g0 + ch] = _sigmoid(pc + b_ref[:, g0:g0 + ch]).astype(BF16)

    return pl.pallas_call(
        body, name="in_proj_fwd", grid=(s // tm,),
        in_specs=[_row_spec(tm, d), _const_spec((1, d)), _const_spec((1, 2 * d)), ANY],
        out_specs=[_row_spec(tm, n_qkv), _row_spec(tm, n_conv), _row_spec(tm, 2 * d), _row_spec(tm, d)],
        out_shape=[jax.ShapeDtypeStruct((s, n_qkv), BF16), jax.ShapeDtypeStruct((s, n_conv), BF16),
                   jax.ShapeDtypeStruct((s, 2 * d), BF16), jax.ShapeDtypeStruct((s, d), BF16)],
        scratch_shapes=[pltpu.VMEM((d, ni), BF16), pltpu.SemaphoreType.DMA((1,))],
        compiler_params=_params(1),
    )(x, g1, b_gate, w_in)


def _split_hi_lo(a):
    hi = a.astype(BF16)
    return hi, (a - hi.astype(F32)).astype(BF16)


def _log2_gates(z):
    z2 = z * LOG2_E
    nz2 = -z2
    log_keep = jnp.minimum(nz2, 0.0) - jnp.log2(1.0 + jnp.exp2(jnp.minimum(z2, nz2)))
    return log_keep + z2, log_keep


def _attn_masks(t):
    row = lax.broadcasted_iota(jnp.int32, (t, t), 0)
    col = lax.broadcasted_iota(jnp.int32, (t, t), 1)
    return (col < row).astype(F32), (row > col).astype(BF16), (row >= col).astype(BF16)


def _chains(a):
    tr = a.shape[0] // ATTN_ROW_SPLITS
    return [jnp.where(_head_lanes(h), a[r * tr:(r + 1) * tr], jnp.zeros((tr, LANES), a.dtype))
            for h in range(HEADS_PER_GROUP) for r in range(ATTN_ROW_SPLITS)]


def _merge_chains(parts):
    rows = []
    for r in range(ATTN_ROW_SPLITS):
        out = parts[r]
        for h in range(1, HEADS_PER_GROUP):
            out = jnp.where(_head_lanes(h), parts[h * ATTN_ROW_SPLITS + r], out)
        rows.append(out)
    return jnp.concatenate(rows, axis=0)


def _by_stage(n_chains, stages):
    for stage in stages:
        for c in range(n_chains):
            stage(c)


def _row_parts(a):
    tr = a.shape[0] // ATTN_ROW_SPLITS
    return [a[r * tr:(r + 1) * tr] for r in range(ATTN_ROW_SPLITS)]


def _while_weights_live(qi, block, carry):
    def cond(state):
        j, carry = state
        live = jnp.max(carry[0][0])
        for run in carry[0][1:]:
            live = jnp.maximum(live, jnp.max(run))
        return jnp.logical_and(j < qi, live >= EXP2_ZERO)

    def step(state):
        j, carry = state
        return j + 1, block(qi - 1 - j, carry)

    return lax.while_loop(cond, step, (jnp.int32(0), carry))[1]


def _head_lanes(h):
    lane = lax.broadcasted_iota(jnp.int32, (1, LANES), 1)
    return (lane >= HEAD_DIM * h) & (lane < HEAD_DIM * (h + 1))


def _attn_fwd(qkv, aw, t, exchange=None):
    s = qkv.shape[0]
    groups = aw // LANES
    nq = s // t
    scale = HEAD_DIM ** -0.5
    ex = exchange or _NO_EXCHANGE
    causal, upper, _ = _attn_masks(t)
    mask_spec = pl.BlockSpec((t, t), lambda g, i: (0, 0))

    def body(q_ref, k_ref, v_ref, causal_ref, upper_ref, *rest):
        ex_in, (o_ref, ob_ref), ex_out, sems = _split_refs(rest, ex, 2)
        qi = pl.program_id(1)
        _exchange_start(ex, ex_in, ex_out, sems, (groups, nq))
        upper = upper_ref[...]
        causal = _row_parts(causal_ref[...] > 0.5) * HEADS_PER_GROUP
        qs = _chains(q_ref[...] * scale)
        heads = range(len(qs))
        tr = t // ATTN_ROW_SPLITS

        def block(kb, runs, accs, diag):
            rows = pl.ds(pl.multiple_of(kb * t, t), t)
            k = k_ref[rows, :]
            v = v_ref[rows, :]
            ncs = [(h % ATTN_ROW_SPLITS + 1) * tr if diag else t for h in heads]
            live = [{} for _ in heads]
            new_runs, new_accs = [None] * len(heads), [None] * len(heads)

            def scores(h):
                live[h]["z"] = _dot_nt(qs[h], k[0:ncs[h]])

            def gates(h):
                nc = ncs[h]
                log_b, log_keep = _log2_gates(live[h].pop("z"))
                if diag:
                    log_keep = jnp.where(causal[h][:, 0:nc], log_keep, 0.0)
                hi, lo = _split_hi_lo(log_keep)
                live[h]["log_w"] = log_b + runs[h]
                live[h]["between"] = _dot(hi, upper[0:nc, 0:nc]) + _dot(lo, upper[0:nc, 0:nc])
                new_runs[h] = runs[h] + jnp.sum(log_keep, axis=1, keepdims=True)

            def weights(h):
                nc = ncs[h]
                w = jnp.exp2(live[h].pop("log_w") + live[h].pop("between"))
                if diag:
                    w = jnp.where(causal[h][:, 0:nc], w, 0.0)
                new_accs[h] = accs[h] + _dot(w.astype(BF16), v[0:nc])

            _by_stage(len(heads), [scores, gates, weights])
            return tuple(new_runs), tuple(new_accs)

        carry = block(qi, [jnp.zeros((tr, 1), F32)] * len(heads), [jnp.zeros((tr, LANES), F32)] * len(heads), True)
        _, accs = _while_weights_live(qi, lambda kb, carry: block(kb, *carry, False), carry)
        o = _merge_chains(accs)
        o_ref[...] = o
        ob_ref[...] = o.astype(BF16)
        _exchange_wait(ex, ex_in, ex_out, sems, (groups, nq))

    return pl.pallas_call(
        body, name="attn_fwd", grid=(groups, nq),
        in_specs=[pl.BlockSpec((t, LANES), lambda g, i: (i, g)),
                  pl.BlockSpec((s, LANES), lambda g, i: (0, groups + g)),
                  pl.BlockSpec((s, LANES), lambda g, i: (0, 2 * groups + g)), mask_spec, mask_spec]
        + [ANY] * len(ex.arrays),
        out_specs=[pl.BlockSpec((t, LANES), lambda g, i: (i, g))] * 2 + [ANY] * len(ex.out_shapes),
        out_shape=[jax.ShapeDtypeStruct((s, aw), F32), jax.ShapeDtypeStruct((s, aw), BF16)] + ex.out_shapes,
        scratch_shapes=_exchange_sems(ex),
        compiler_params=_params(2),
    )(qkv, qkv, qkv, causal, upper, *ex.arrays)


def _attn_bwd(qkv, o, do, aw, t, exchange=None):
    s = qkv.shape[0]
    groups = aw // LANES
    nq = s // t
    scale = HEAD_DIM ** -0.5
    ex = exchange or _NO_EXCHANGE

    def body(q_ref, k_ref, v_ref, o_ref, do_ref, causal_ref, upper_ref, lower_ref, *rest):
        ex_in, (dq_ref, dk_ref, dv_ref), ex_out, (dk_acc, dv_acc, *sems) = _split_refs(rest, ex, 3)
        qi = pl.program_id(1)
        _exchange_start(ex, ex_in, ex_out, sems, (groups, nq))

        @pl.when(qi == 0)
        def _():
            dk_acc[...] = jnp.zeros_like(dk_acc)
            dv_acc[...] = jnp.zeros_like(dv_acc)

        upper = upper_ref[...]
        lower_incl = lower_ref[...]
        causal = _row_parts(causal_ref[...] > 0.5) * HEADS_PER_GROUP
        q = q_ref[...] * scale
        do_b = do_ref[...]
        qs = _chains(q)
        dos = _chains(do_b)
        qs_all = jnp.concatenate(qs, axis=0)
        dos_all = jnp.concatenate(dos, axis=0)
        e_totals = [jnp.sum(part, axis=1, keepdims=True) for part in _chains(do_b.astype(F32) * o_ref[...])]
        heads = range(len(qs))
        tr = t // ATTN_ROW_SPLITS

        def block(kb, runs, e_runs, dqs, diag):
            rows = pl.ds(pl.multiple_of(kb * t, t), t)
            k = k_ref[rows, :]
            v = v_ref[rows, :]
            ncs = [(h % ATTN_ROW_SPLITS + 1) * tr if diag else t for h in heads]
            live = [{} for _ in heads]
            none = [None] * len(heads)
            new_runs, new_e_runs, new_dqs, dzbs, wbs = list(none), list(none), list(none), list(none), list(none)

            def scores(h):
                live[h]["z"] = _dot_nt(qs[h], k[0:ncs[h]])
                live[h]["dw"] = _dot_nt(dos[h], v[0:ncs[h]])

            def gates(h):
                nc = ncs[h]
                log_b, log_keep = _log2_gates(live[h].pop("z"))
                live[h]["beta"] = jnp.exp2(log_b)
                live[h]["keep"] = jnp.exp2(log_keep)
                if diag:
                    log_keep = jnp.where(causal[h][:, 0:nc], log_keep, 0.0)
                hi, lo = _split_hi_lo(log_keep)
                live[h]["log_w"] = log_b + runs[h]
                live[h]["between"] = _dot(hi, upper[0:nc, 0:nc]) + _dot(lo, upper[0:nc, 0:nc])
                new_runs[h] = runs[h] + jnp.sum(log_keep, axis=1, keepdims=True)

            def weights(h):
                nc = ncs[h]
                w = jnp.exp2(live[h].pop("log_w") + live[h].pop("between"))
                if diag:
                    w = jnp.where(causal[h][:, 0:nc], w, 0.0)
                wb = w.astype(BF16)
                e = live[h].pop("dw") * wb.astype(F32)
                hi, lo = _split_hi_lo(e)
                live[h]["e"] = e
                live[h]["e_suffix"] = _dot(hi, lower_incl[0:nc, 0:nc]) + _dot(lo, lower_incl[0:nc, 0:nc]) + e_runs[h]
                wbs[h] = wb

            def score_grads(h):
                nc = ncs[h]
                e_suffix = live[h].pop("e_suffix")
                dz = live[h].pop("e") * live[h].pop("keep") - (e_totals[h] - e_suffix) * live[h].pop("beta")
                if diag:
                    dz = jnp.where(causal[h][:, 0:nc], dz, 0.0)
                dzb = dz.astype(BF16)
                new_dqs[h] = dqs[h] + _dot(dzb, k[0:nc])
                new_e_runs[h] = e_suffix[:, 0:1]
                if nc < t:
                    unseen = jnp.zeros((tr, t - nc), BF16)
                    dzb = jnp.concatenate([dzb, unseen], axis=1)
                    wbs[h] = jnp.concatenate([wbs[h], unseen], axis=1)
                dzbs[h] = dzb

            _by_stage(len(heads), [scores, gates, weights, score_grads])
            dk_acc[rows, :] += _dot_tn(jnp.concatenate(dzbs, axis=0), qs_all)
            dv_acc[rows, :] += _dot_tn(jnp.concatenate(wbs, axis=0), dos_all)
            return tuple(new_runs), tuple(new_e_runs), tuple(new_dqs)

        zero_cols = [jnp.zeros((tr, 1), F32)] * len(heads)
        carry = block(qi, zero_cols, zero_cols, [jnp.zeros((tr, LANES), F32)] * len(heads), True)
        _, _, dqs = _while_weights_live(qi, lambda kb, carry: block(kb, *carry, False), carry)
        dq_ref[...] = (_merge_chains(dqs) * scale).astype(BF16)

        @pl.when(qi == nq - 1)
        def _():
            dk_ref[...] = dk_acc[...].astype(BF16)
            dv_ref[...] = dv_acc[...].astype(BF16)

        _exchange_wait(ex, ex_in, ex_out, sems, (groups, nq))

    blk = pl.BlockSpec((t, LANES), lambda g, i: (i, g))
    slab = pl.BlockSpec((s, LANES), lambda g, i: (0, g))
    mask_spec = pl.BlockSpec((t, t), lambda g, i: (0, 0))
    return pl.pallas_call(
        body, name="attn_bwd", grid=(groups, nq),
        in_specs=[blk, pl.BlockSpec((s, LANES), lambda g, i: (0, groups + g)),
                  pl.BlockSpec((s, LANES), lambda g, i: (0, 2 * groups + g)), blk, blk, mask_spec, mask_spec, mask_spec]
        + [ANY] * len(ex.arrays),
        out_specs=[blk, slab, slab] + [ANY] * len(ex.out_shapes),
        out_shape=[jax.ShapeDtypeStruct((s, aw), BF16)] * 3 + ex.out_shapes,
        scratch_shapes=[pltpu.VMEM((s, LANES), F32), pltpu.VMEM((s, LANES), F32)] + _exchange_sems(ex),
        compiler_params=_params(2),
    )(qkv, qkv, qkv, o, do, *_attn_masks(t), *ex.arrays)


def _branches(o_b, conv, conv_prev, wconv, w_ao, w_co, cw, first):
    conv = conv.astype(F32)
    conv_prev = conv_prev.astype(F32)
    cb = conv[:, 0:cw]
    cm = conv[:, cw:2 * cw] * conv[:, 2 * cw:3 * cw]
    cm_prev = conv_prev[:, cw:2 * cw] * conv_prev[:, 2 * cw:3 * cw]
    cm_prev = jnp.where(first, 0.0, cm_prev)
    cv, cm1, cm2 = _conv_taps(cm, cm_prev, wconv)
    conv_in = (cb * cv).astype(BF16)
    return _dot(o_b, w_ao), _dot(conv_in, w_co), conv_in, cb, cv, cm, cm1, cm2


def _mix_fwd(x, o, conv, gate, wconv, g_post, w_ao, w_co, w_o, tm):
    s, d = x.shape
    aw, cw = w_ao.shape[0], w_co.shape[0]

    def body(x_ref, o_ref, conv_ref, prev_ref, gate_ref, wc_ref, g_ref, wao_hbm, wco_hbm, wo_hbm,
             x1_ref, mixed_ref, mixin_ref, convin_ref, wao, wco, wo, sem):
        _load_resident([(wao_hbm, wao), (wco_hbm, wco), (wo_hbm, wo)], sem)
        y_attn, y_conv, conv_in, *_ = _branches(
            o_ref[...].astype(BF16), conv_ref[...], prev_ref[...], wc_ref[...], wao[...], wco[...], cw,
            pl.program_id(0) == 0)
        mix_in = (gate_ref[:, 0:d].astype(F32) * y_attn + gate_ref[:, d:2 * d].astype(F32) * y_conv).astype(BF16)
        mixed = _dot(mix_in, wo[...])
        x1_ref[...] = x_ref[...] + mixed * _rms_scale(mixed) * g_ref[...]
        mixed_ref[...] = mixed
        mixin_ref[...] = mix_in
        convin_ref[...] = conv_in

    return pl.pallas_call(
        body, name="mix_fwd", grid=(s // tm,),
        in_specs=[_row_spec(tm, d), _row_spec(tm, aw), _row_spec(tm, 3 * cw), _prev_halo_spec(tm, 3 * cw, HALO_BF16),
                  _row_spec(tm, 2 * d), _const_spec((CONV_K, cw)), _const_spec((1, d)), ANY, ANY, ANY],
        out_specs=[_row_spec(tm, d), _row_spec(tm, d), _row_spec(tm, d), _row_spec(tm, cw)],
        out_shape=[jax.ShapeDtypeStruct((s, d), F32), jax.ShapeDtypeStruct((s, d), F32),
                   jax.ShapeDtypeStruct((s, d), BF16), jax.ShapeDtypeStruct((s, cw), BF16)],
        scratch_shapes=[pltpu.VMEM(w_ao.shape, BF16), pltpu.VMEM(w_co.shape, BF16), pltpu.VMEM(w_o.shape, BF16),
                        pltpu.SemaphoreType.DMA((3,))],
        compiler_params=_params(1),
    )(x, o, conv, conv, gate, wconv, g_post, w_ao, w_co, w_o)


def _mix_bwd(dx1, mixed, o, conv, gate, wconv, g_post, w_ao, w_co, w_o, tm):
    s, d = dx1.shape
    aw, cw = w_ao.shape[0], w_co.shape[0]
    n = s // tm
    per = tm // HALO_BF16

    def body(dx1_ref, mixed_ref, o_ref, conv_ref, prev_ref, gate_ref, wc_ref, g_ref, wao_hbm, wco_hbm, wo_hbm,
             dmixed_ref, dattn_ref, dconvout_ref, do_ref, drest_ref, dg_ref, dbias_ref, dwc_ref,
             wao, wco, wo, dcv_next, sem):
        i = pl.program_id(0)
        _load_resident([(wao_hbm, wao), (wco_hbm, wco), (wo_hbm, wo)], sem)

        @pl.when(i == 0)
        def _():
            dg_ref[...] = jnp.zeros_like(dg_ref)
            dbias_ref[...] = jnp.zeros_like(dbias_ref)
            dwc_ref[...] = jnp.zeros_like(dwc_ref)
            dcv_next[...] = jnp.zeros_like(dcv_next)

        mixed = mixed_ref[...]
        r = _rms_scale(mixed)
        mhat = mixed * r
        dn = dx1_ref[...]
        dg_ref[...] += jnp.sum(dn * mhat, axis=0, keepdims=True)
        dmixed = _rms_bwd(mhat, r, g_ref[...], dn).astype(BF16)
        dmixed_ref[...] = dmixed
        dmi = _dot_nt(dmixed, wo[...])

        wc = wc_ref[...]
        conv = conv_ref[...].astype(F32)
        y_attn, y_conv, _, cb, cv, cm, cm1, cm2 = _branches(
            o_ref[...].astype(BF16), conv, prev_ref[...], wc, wao[...], wco[...], cw, i == n - 1)
        ga = gate_ref[:, 0:d].astype(F32)
        gc = gate_ref[:, d:2 * d].astype(F32)
        dpre_a = dmi * y_attn * ga * (1.0 - ga)
        dpre_c = dmi * y_conv * gc * (1.0 - gc)
        drest_ref[:, 3 * cw:3 * cw + d] = dpre_a.astype(BF16)
        drest_ref[:, 3 * cw + d:3 * cw + 2 * d] = dpre_c.astype(BF16)
        dbias_ref[:, 0:d] += jnp.sum(dpre_a, axis=0, keepdims=True)
        dbias_ref[:, d:2 * d] += jnp.sum(dpre_c, axis=0, keepdims=True)

        dattn = (dmi * ga).astype(BF16)
        dattn_ref[...] = dattn
        do_ref[...] = _dot_nt(dattn, wao[...]).astype(BF16)
        dconvout = (dmi * gc).astype(BF16)
        dconvout_ref[...] = dconvout
        dconv_in = _dot_nt(dconvout, wco[...])
        drest_ref[:, 0:cw] = (dconv_in * cv).astype(BF16)

        dcv = dconv_in * cb
        following = dcv_next[...]
        dcm = wc[2:3, :] * dcv + wc[1:2, :] * _shift_up(dcv, following, 1) + wc[0:1, :] * _shift_up(dcv, following, 2)
        drest_ref[:, cw:2 * cw] = (dcm * conv[:, 2 * cw:3 * cw]).astype(BF16)
        drest_ref[:, 2 * cw:3 * cw] = (dcm * conv[:, cw:2 * cw]).astype(BF16)
        for tap, shifted in enumerate((cm2, cm1, cm)):
            dwc_ref[tap:tap + 1, :] += jnp.sum(dcv * shifted, axis=0, keepdims=True)
        dcv_next[...] = dcv[0:HALO, :]

    def rows(width):
        return pl.BlockSpec((tm, width), lambda i: (n - 1 - i, 0))

    prev_halo = pl.BlockSpec((HALO_BF16, 3 * cw), lambda i: (jnp.maximum((n - 1 - i) * per - 1, 0), 0))
    n_rest = 3 * cw + 2 * d
    return pl.pallas_call(
        body, name="mix_bwd", grid=(n,),
        in_specs=[rows(d), rows(d), rows(aw), rows(3 * cw), prev_halo, rows(2 * d), _const_spec((CONV_K, cw)),
                  _const_spec((1, d)), ANY, ANY, ANY],
        out_specs=[rows(d), rows(d), rows(d), rows(aw), rows(n_rest), _const_spec((1, d)), _const_spec((1, 2 * d)),
                   _const_spec((CONV_K, cw))],
        out_shape=[jax.ShapeDtypeStruct((s, d), BF16), jax.ShapeDtypeStruct((s, d), BF16),
                   jax.ShapeDtypeStruct((s, d), BF16), jax.ShapeDtypeStruct((s, aw), BF16),
                   jax.ShapeDtypeStruct((s, n_rest), BF16), jax.ShapeDtypeStruct((1, d), F32),
                   jax.ShapeDtypeStruct((1, 2 * d), F32), jax.ShapeDtypeStruct((CONV_K, cw), F32)],
        scratch_shapes=[pltpu.VMEM(w_ao.shape, BF16), pltpu.VMEM(w_co.shape, BF16), pltpu.VMEM(w_o.shape, BF16),
                        pltpu.VMEM((HALO, cw), F32), pltpu.SemaphoreType.DMA((3,))],
        compiler_params=_params(1),
    )(dx1, mixed, o, conv, conv, gate, wconv, g_post, w_ao, w_co, w_o)


def _mlp_ple_loss(x1, p, target, g_pre, g_post, g_ple, w_up, w_dn, w_pg, w_pp, tm):
    s, d = x1.shape
    ff = w_up.shape[1]
    pd = p.shape[1]
    fc = FF_CHUNK

    def body(x1_ref, p_ref, t_ref, gpre_ref, gpost_ref, gple_ref, wup_hbm, wdn_hbm, wpg_hbm, wpp_hbm,
             dx1_ref, h2_ref, du_ref, a_ref, df_ref, h3_ref, ds3_ref, dpp_ref, loss_ref, dgpre_ref, dgpost_ref,
             dgple_ref, wup, wdn, wpg, wpp, u_scr, sem):
        _load_resident([(wup_hbm, wup), (wdn_hbm, wdn), (wpg_hbm, wpg), (wpp_hbm, wpp)], sem)

        @pl.when(pl.program_id(0) == 0)
        def _():
            for ref in (loss_ref, dgpre_ref, dgpost_ref, dgple_ref):
                ref[...] = jnp.zeros_like(ref)

        x1v = x1_ref[...]
        r2 = _rms_scale(x1v)
        x1hat = x1v * r2
        h2 = (x1hat * gpre_ref[...]).astype(BF16)
        h2_ref[...] = h2
        f = jnp.zeros((tm, d), F32)
        for c0 in range(0, ff, fc):
            u = _dot(h2, wup[:, c0:c0 + fc])
            u_scr[:, c0:c0 + fc] = u
            a = jnp.square(jnp.maximum(u, 0.0)).astype(BF16)
            a_ref[:, c0:c0 + fc] = a
            f = f + _dot(a, wdn[c0:c0 + fc, :])
        rf = _rms_scale(f)
        fhat = f * rf
        x2 = x1v + fhat * gpost_ref[...]
        r3 = _rms_scale(x2)
        x2hat = x2 * r3
        h3 = (x2hat * gple_ref[...]).astype(BF16)
        h3_ref[...] = h3
        pg = _sigmoid(_dot(h3, wpg[...]))
        pp = _dot(p_ref[...].astype(BF16), wpp[...])
        diff = x2 + pg * pp - t_ref[...]
        loss_ref[...] += 0.5 * jnp.sum(jnp.mean(diff * diff, axis=-1, keepdims=True), axis=0, keepdims=True)

        dy = diff * (1.0 / d)
        dpp_ref[...] = (dy * pg).astype(BF16)
        ds3 = (dy * pp * pg * (1.0 - pg)).astype(BF16)
        ds3_ref[...] = ds3
        dh3 = _dot_nt(ds3, wpg[...])
        dgple_ref[...] += jnp.sum(dh3 * x2hat, axis=0, keepdims=True)
        dx2 = dy + _rms_bwd(x2hat, r3, gple_ref[...], dh3)
        dgpost_ref[...] += jnp.sum(dx2 * fhat, axis=0, keepdims=True)
        df = _rms_bwd(fhat, rf, gpost_ref[...], dx2).astype(BF16)
        df_ref[...] = df
        dh2 = jnp.zeros((tm, d), F32)
        for c0 in range(0, ff, fc):
            da = _dot_nt(df, wdn[c0:c0 + fc, :])
            du = (da * (2.0 * jnp.maximum(u_scr[:, c0:c0 + fc], 0.0))).astype(BF16)
            du_ref[:, c0:c0 + fc] = du
            dh2 = dh2 + _dot_nt(du, wup[:, c0:c0 + fc])
        dgpre_ref[...] += jnp.sum(dh2 * x1hat, axis=0, keepdims=True)
        dx1_ref[...] = dx2 + _rms_bwd(x1hat, r2, gpre_ref[...], dh2)

    vec = _const_spec((1, d))
    return pl.pallas_call(
        body, name="mlp_ple_loss", grid=(s // tm,),
        in_specs=[_row_spec(tm, d), _row_spec(tm, pd), _row_spec(tm, d), vec, vec, vec, ANY, ANY, ANY, ANY],
        out_specs=[_row_spec(tm, d), _row_spec(tm, d), _row_spec(tm, ff), _row_spec(tm, ff), _row_spec(tm, d),
                   _row_spec(tm, d), _row_spec(tm, d), _row_spec(tm, d), _const_spec((1, 1)), vec, vec, vec],
        out_shape=[jax.ShapeDtypeStruct((s, d), F32), jax.ShapeDtypeStruct((s, d), BF16),
                   jax.ShapeDtypeStruct((s, ff), BF16), jax.ShapeDtypeStruct((s, ff), BF16),
                   jax.ShapeDtypeStruct((s, d), BF16), jax.ShapeDtypeStruct((s, d), BF16),
                   jax.ShapeDtypeStruct((s, d), BF16), jax.ShapeDtypeStruct((s, d), BF16),
                   jax.ShapeDtypeStruct((1, 1), F32), jax.ShapeDtypeStruct((1, d), F32),
                   jax.ShapeDtypeStruct((1, d), F32), jax.ShapeDtypeStruct((1, d), F32)],
        scratch_shapes=[pltpu.VMEM(w_up.shape, BF16), pltpu.VMEM(w_dn.shape, BF16), pltpu.VMEM(w_pg.shape, BF16),
                        pltpu.VMEM(w_pp.shape, BF16), pltpu.VMEM((tm, ff), F32), pltpu.SemaphoreType.DMA((4,))],
        compiler_params=_params(1),
    )(x1, p, target, g_pre, g_post, g_ple, w_up, w_dn, w_pg, w_pp)


def _in_proj_bwd(x, dx1, pieces, g1, w_in, tm, exchange=None):
    s, d = x.shape
    ni = w_in.shape[1]
    widths = [p.shape[1] for p in pieces]
    grid = (s // tm,)
    ex = exchange or _NO_EXCHANGE
    continued = [(j, a) for j, a in enumerate(ex.into) if a is not None]

    def body(x_ref, dx1_ref, *rest):
        piece_refs, rest = rest[:len(pieces)], rest[len(pieces):]
        g_ref, w_hbm = rest[0], rest[1]
        ex_in, rest = rest[2:2 + len(ex.arrays)], rest[2 + len(ex.arrays) + len(continued):]
        _, (dx_ref, dg_ref), ex_out, (w_vmem, sem, *sems) = _split_refs(rest, _NO_EXCHANGE_INPUTS(ex), 2)
        _exchange_start(ex, ex_in, ex_out, sems, grid)
        _load_resident([(w_hbm, w_vmem)], sem)

        @pl.when(pl.program_id(0) == 0)
        def _():
            dg_ref[...] = jnp.zeros_like(dg_ref)

        dh = jnp.zeros((tm, d), F32)
        c0 = 0
        for ref, width in zip(piece_refs, widths):
            dh = dh + _dot_nt(ref[...], w_vmem[:, c0:c0 + width])
            c0 += width
        xv = x_ref[...]
        r = _rms_scale(xv)
        xhat = xv * r
        dg_ref[...] += jnp.sum(dh * xhat, axis=0, keepdims=True)
        dx_ref[...] = dx1_ref[...] + _rms_bwd(xhat, r, g_ref[...], dh)
        _exchange_wait(ex, ex_in, ex_out, sems, grid)

    return pl.pallas_call(
        body, name="in_proj_bwd", grid=grid,
        in_specs=[_row_spec(tm, d), _row_spec(tm, d)] + [_row_spec(tm, w) for w in widths]
        + [_const_spec((1, d)), ANY] + [ANY] * (len(ex.arrays) + len(continued)),
        out_specs=[_row_spec(tm, d), _const_spec((1, d))] + [ANY] * len(ex.out_shapes),
        out_shape=[jax.ShapeDtypeStruct((s, d), F32), jax.ShapeDtypeStruct((1, d), F32)] + ex.out_shapes,
        input_output_aliases={4 + len(pieces) + len(ex.arrays) + i: 2 + j for i, (j, _) in enumerate(continued)},
        scratch_shapes=[pltpu.VMEM((d, ni), BF16), pltpu.SemaphoreType.DMA((1,))] + _exchange_sems(ex),
        compiler_params=_params(1),
    )(x, dx1, *pieces, g1, w_in, *ex.arrays, *[a for _, a in continued])


def _weight_grad(a, b, name, into=None, col0=0, n_total=None):
    s, m = a.shape
    n = b.shape[1]
    tm = min(m, DW_TILE)
    tn = min(n, DW_TILE) if n_total is None else DW_PIECE_TILE
    tk = min(s, DW_TOKENS * (2 if (m // tm) * (n // tn) >= DW_MANY_TILES else 1))
    nk = s // tk
    j0 = col0 // tn
    assert m % tm == 0 and n % tn == 0 and col0 % tn == 0

    def body(a_ref, b_ref, *rest):
        o_ref, acc = rest[-2:]
        k = pl.program_id(2)

        @pl.when(k == 0)
        def _():
            acc[...] = jnp.zeros_like(acc)

        acc[...] += _dot_tn(a_ref[...].astype(BF16), b_ref[...].astype(BF16))

        @pl.when(k == nk - 1)
        def _():
            o_ref[...] = acc[...].astype(BF16)

    extra = [] if into is None else [into]
    return pl.pallas_call(
        body, name=name, grid=(m // tm, n // tn, nk),
        in_specs=[pl.BlockSpec((tk, tm), lambda i, j, k: (k, i)), pl.BlockSpec((tk, tn), lambda i, j, k: (k, j))]
        + [ANY] * len(extra),
        out_specs=pl.BlockSpec((tm, tn), lambda i, j, k: (i, j0 + j)),
        out_shape=jax.ShapeDtypeStruct((m, n_total or n), BF16),
        input_output_aliases={2: 0} if extra else {},
        scratch_shapes=[pltpu.VMEM((tm, tn), F32)],
        compiler_params=_params(3),
    )(a, b, *extra)


def _mesh_position():
    return tuple(lax.axis_index(a) for a in MESH_AXES)


def _peer(me, k):
    bits = ((k >> 2) & 1, (k >> 1) & 1, k & 1)
    pos = tuple(1 - m if b else m for m, b in zip(me, bits))
    return pos, 4 * pos[0] + 2 * pos[1] + pos[2]


class _Exchange:
    def __init__(self, arrays, out_shapes, src, dst, relayed=None, blocks=None, into=None):
        self.arrays, self.out_shapes, self.src, self.dst = list(arrays), list(out_shapes), src, dst
        self.relayed = list(relayed) if relayed is not None else [False] * len(self.arrays)
        self.blocks = list(blocks) if blocks is not None else [None] * len(self.arrays)
        self.into = list(into) if into is not None else [None] * len(self.out_shapes)


_NO_EXCHANGE = _Exchange([], [], None, None)


def _NO_EXCHANGE_INPUTS(ex):
    return _Exchange([], ex.out_shapes, None, None)


def _exchange_sems(ex):
    n = len(ex.arrays)
    if n == 0:
        return []
    return [pltpu.SemaphoreType.DMA((n, N_DEV - 1)), pltpu.SemaphoreType.DMA((n, N_DEV - 1)),
            pltpu.SemaphoreType.DMA((n,))]


def _split_refs(rest, ex, n_own_outs):
    n_in, n_out = len(ex.arrays), len(ex.out_shapes)
    ex_in, rest = rest[:n_in], rest[n_in:]
    own, rest = rest[:n_own_outs], rest[n_own_outs:]
    return ex_in, own, rest[:n_out], rest[n_out:]


def _direct_steps(ex, w, in_refs, out_refs, sems):
    send_sems, recv_sems, local_sems = sems
    me = _mesh_position()
    mine = 4 * me[0] + 2 * me[1] + me[2]

    def copy(k):
        landing = ex.dst(w, out_refs, mine)
        if k == 0:
            return pltpu.make_async_copy(ex.src(w, in_refs, mine), landing, local_sems.at[w])
        peer, peer_idx = _peer(me, k)
        return pltpu.make_async_remote_copy(
            src_ref=ex.src(w, in_refs, peer_idx), dst_ref=landing, send_sem=send_sems.at[w, k - 1],
            recv_sem=recv_sems.at[w, k - 1], device_id=peer, device_id_type=pl.DeviceIdType.MESH)

    ks = range(N_DEV)
    served = ex.blocks[w]
    if served is None:
        return [lambda k=k: copy(k).start() for k in ks], [], [lambda k=k: copy(k).wait() for k in ks]

    def is_served(idx):
        hit = idx == served[0]
        for b in served[1:]:
            hit = jnp.logical_or(hit, idx == b)
        return hit

    def target(k):
        return mine if k == 0 else _peer(me, k)[1]

    def start(k):
        pl.when(is_served(target(k)))(lambda: copy(k).start())

    def finish(k):
        if k == 0:
            pl.when(is_served(mine))(lambda: copy(k).wait())
        else:
            pl.when(is_served(target(k)))(lambda: copy(k).wait_send())
            pl.when(is_served(mine))(lambda: copy(k).wait_recv())

    return [lambda k=k: start(k) for k in ks], [], [lambda k=k: finish(k) for k in ks]


def _relayed_steps(ex, w, in_refs, out_refs, sems):
    send_sems, recv_sems, local_sems = sems
    x, y, c = _mesh_position()
    chips = [(1 - x, y), (x, 1 - y), (1 - x, 1 - y)]
    sibling = (x, y, 1 - c)
    js = range(len(chips))

    def block(px, py, pc):
        return ex.dst(w, out_refs, 4 * px + 2 * py + pc)

    def copy(k, dst, to, src=None):
        return pltpu.make_async_remote_copy(
            src_ref=ex.src(w, in_refs, None) if src is None else src, dst_ref=dst, send_sem=send_sems.at[w, k],
            recv_sem=recv_sems.at[w, k], device_id=to, device_id_type=pl.DeviceIdType.MESH)

    def local():
        return pltpu.make_async_copy(ex.src(w, in_refs, None), block(x, y, c), local_sems.at[w])

    def own(k):
        return copy(k, block(x, y, c), sibling if k == 0 else (*chips[k - 1], c))

    def came(j):
        return copy(1 + j, block(*chips[j], c), (*chips[j], c))

    def passed(j):
        return copy(4 + j, block(*chips[j], c), sibling, src=block(*chips[j], c))

    def from_sibling(k):
        return copy(k, block(x, y, 1 - c) if k == 0 else block(*chips[k - 4], 1 - c), sibling)

    start = [lambda: local().start()] + [lambda k=k: own(k).start() for k in range(4)]
    relay = [step for j in js for step in (lambda j=j: came(j).wait_recv(), lambda j=j: passed(j).start())]
    finish = ([lambda: local().wait()] + [lambda k=k: own(k).wait_send() for k in range(4)]
              + [lambda j=j: passed(j).wait_send() for j in js]
              + [lambda k=k: from_sibling(k).wait_recv() for k in (0, 4, 5, 6)])
    return start, relay, finish


def _exchange_steps(ex, in_refs, out_refs, sems):
    start, relay, finish = [], [], []
    for w in range(len(ex.arrays)):
        steps = (_relayed_steps if ex.relayed[w] else _direct_steps)(ex, w, in_refs, out_refs, sems)
        start += steps[0]
        relay += steps[1]
        finish += steps[2]
    return start, relay, finish


def _run(steps):
    for step in steps:
        step()


def _at_grid_step(grid, where):
    target = {"first": [0] * len(grid), "middle": [grid[0] // 2] + [0] * (len(grid) - 1),
              "last": [g - 1 for g in grid]}[where]
    hit = pl.program_id(0) == target[0]
    for axis in range(1, len(grid)):
        hit = jnp.logical_and(hit, pl.program_id(axis) == target[axis])
    return hit


def _exchange_start(ex, in_refs, out_refs, sems, grid):
    if ex.arrays:
        @pl.when(_at_grid_step(grid, "first"))
        def _():
            _run(_exchange_steps(ex, in_refs, out_refs, sems)[0])

        if any(ex.relayed):
            assert grid[0] >= 2

            @pl.when(_at_grid_step(grid, "middle"))
            def _():
                _run(_exchange_steps(ex, in_refs, out_refs, sems)[1])


def _exchange_wait(ex, in_refs, out_refs, sems, grid):
    if ex.arrays:
        @pl.when(_at_grid_step(grid, "last"))
        def _():
            _run(_exchange_steps(ex, in_refs, out_refs, sems)[2])


def _shard_block(ref, shard_shape, by_col, idx):
    r, c = shard_shape
    if by_col:
        return ref.at[:, pl.ds(pl.multiple_of(idx * c, LANES), c)]
    return ref.at[pl.ds(pl.multiple_of(idx * r, 16), r), :]


def _full_shape(shard_shape, by_col):
    r, c = shard_shape
    return (r, N_DEV * c) if by_col else (N_DEV * r, c)


def _gather_exchange(shards, col_sharded):
    shapes = [a.shape for a in shards]
    return _Exchange(
        shards, [jax.ShapeDtypeStruct(_full_shape(sh, bc), a.dtype) for a, sh, bc in zip(shards, shapes, col_sharded)],
        lambda w, refs, idx: refs[w],
        lambda w, refs, idx: _shard_block(refs[w], shapes[w], col_sharded[w], idx), [True] * len(shards))


def _scatter_exchange(grads, col_sharded, blocks=None, into=None):
    shapes = []
    for g, by_col in zip(grads, col_sharded):
        r, c = g.shape
        shapes.append((r, c // N_DEV) if by_col else (r // N_DEV, c))
    return _Exchange(
        grads, [jax.ShapeDtypeStruct((N_DEV,) + sh, g.dtype) for g, sh in zip(grads, shapes)],
        lambda w, refs, idx: _shard_block(refs[w], shapes[w], col_sharded[w], idx),
        lambda w, refs, mine: refs[w].at[mine], blocks=blocks and [blocks] * len(grads), into=into)


def _broadcast_exchange(arrays):
    return _Exchange(arrays, [jax.ShapeDtypeStruct((N_DEV,) + a.shape, a.dtype) for a in arrays],
                     lambda w, refs, idx: refs[w], lambda w, refs, mine: refs[w].at[mine])


def _join(*exs):
    arrays, shapes, owner = [], [], []
    for e in exs:
        for w in range(len(e.arrays)):
            owner.append((e, w, len(arrays), len(shapes)))
        arrays += e.arrays
        shapes += e.out_shapes

    def src(w, refs, idx):
        e, w0, i0, _ = owner[w]
        return e.src(w0, refs[i0:i0 + len(e.arrays)], idx)

    def dst(w, refs, idx):
        e, w0, _, o0 = owner[w]
        return e.dst(w0, refs[o0:o0 + len(e.out_shapes)], idx)

    return _Exchange(arrays, shapes, src, dst, [flag for e in exs for flag in e.relayed],
                     [b for e in exs for b in e.blocks], [a for e in exs for a in e.into])


def _exchange_call(ex, name):
    n_in = len(ex.arrays)

    def body(*refs):
        in_refs, _, out_refs, sems = _split_refs(refs, ex, 0)
        for steps in _exchange_steps(ex, in_refs, out_refs, sems):
            _run(steps)

    return pl.pallas_call(
        body, name=name, in_specs=[ANY] * n_in, out_specs=[ANY] * len(ex.out_shapes), out_shape=ex.out_shapes,
        scratch_shapes=_exchange_sems(ex), compiler_params=pltpu.CompilerParams(vmem_limit_bytes=VMEM_LIMIT),
    )(*ex.arrays)


def _to_bf16(arrays):
    def body(*refs):
        for src, dst in zip(refs[:len(arrays)], refs[len(arrays):]):
            dst[...] = src[...].astype(BF16)

    vmem = pl.BlockSpec(memory_space=pltpu.VMEM)
    return pl.pallas_call(
        body, name="weights_to_bf16", in_specs=[vmem] * len(arrays), out_specs=[vmem] * len(arrays),
        out_shape=[jax.ShapeDtypeStruct(a.shape, BF16) for a in arrays],
        compiler_params=pltpu.CompilerParams(vmem_limit_bytes=VMEM_LIMIT),
    )(*arrays)


def _adamw(w, g, m, v):
    m = ADAM_B1 * m + (1.0 - ADAM_B1) * g
    v = ADAM_B2 * v + (1.0 - ADAM_B2) * jnp.square(g)
    m_hat = m / (1.0 - ADAM_B1 ** ADAM_STEP)
    v_hat = v / (1.0 - ADAM_B2 ** ADAM_STEP)
    delta = -ADAM_LR * (m_hat / (jnp.sqrt(v_hat) + ADAM_EPS) + ADAM_WD * w)
    return delta, m, v


def _sum_and_adamw(parts, w, m, v, name):
    r, c = w.shape
    tr = min(r, 256)

    def body(p_ref, w_ref, m_ref, v_ref, g_out, d_out, m_out, v_out):
        g = p_ref[0].astype(F32)
        for dev in range(1, N_DEV):
            g = g + p_ref[dev].astype(F32)
        g_out[...] = g
        d_out[...], m_out[...], v_out[...] = _adamw(w_ref[...], g, m_ref[...], v_ref[...])

    blk = pl.BlockSpec((tr, c), lambda i: (i, 0))
    return pl.pallas_call(
        body, name=name, grid=(r // tr,),
        in_specs=[pl.BlockSpec((N_DEV, tr, c), lambda i: (0, i, 0)), blk, blk, blk],
        out_specs=[blk] * 4, out_shape=[jax.ShapeDtypeStruct((r, c), F32)] * 4,
        compiler_params=_params(1),
    )(parts, w, m, v)


BIG = ("w_in", "w_attn_out", "w_conv_out", "w_o", "w_up", "w_down", "w_ple_gate", "w_ple_proj")
COL_SHARDED = {"w_in": True, "w_attn_out": True, "w_conv_out": True, "w_o": False, "w_up": True, "w_down": False,
               "w_ple_gate": False, "w_ple_proj": True}
SMALL = ("g_pre_mix", "b_gate", "g_post_mix", "g_pre_mlp", "g_post_mlp", "g_ple")


REST = BIG[1:]


def _local_grads(x, p, target, small, wconv, full, aw, cw, tm, t, gather_rest=None, scatter_rest=None,
                 scatter_in=None):
    full = dict(full)
    tm_wide = min(WIDE_BLOCKS * tm, x.shape[0])
    qkv, conv, gate, h1 = _in_proj_fwd(x, small["g_pre_mix"], small["b_gate"], full["w_in"], aw, cw, tm_wide)
    o, o_b, *rest = _attn_fwd(qkv, aw, t, gather_rest)
    full.update(zip(REST, rest))
    x1, mixed, mix_in, conv_in = _mix_fwd(x, o_b, conv, gate, wconv, small["g_post_mix"], full["w_attn_out"],
                                          full["w_conv_out"], full["w_o"], tm_wide)
    (dx1, h2, du, a, df, h3, ds3, dpp, loss, dg_pre_mlp, dg_post_mlp, dg_ple) = _mlp_ple_loss(
        x1, p, target, small["g_pre_mlp"], small["g_post_mlp"], small["g_ple"], full["w_up"], full["w_down"],
        full["w_ple_gate"], full["w_ple_proj"], tm)
    big = {"w_up": _weight_grad(h2, du, "dw_up"), "w_down": _weight_grad(a, df, "dw_down"),
           "w_ple_gate": _weight_grad(h3, ds3, "dw_ple_gate"), "w_ple_proj": _weight_grad(p, dpp, "dw_ple_proj")}
    (dmixed, dattn, dconvout, do, drest, dg_post_mix, db_gate, dwconv) = _mix_bwd(
        dx1, mixed, o_b, conv, gate, wconv, small["g_post_mix"], full["w_attn_out"], full["w_conv_out"], full["w_o"],
        tm_wide)
    big.update({"w_attn_out": _weight_grad(o_b, dattn, "dw_attn_out"),
                "w_conv_out": _weight_grad(conv_in, dconvout, "dw_conv_out"),
                "w_o": _weight_grad(mix_in, dmixed, "dw_o")})
    ni = full["w_in"].shape[1]
    dw_in = _weight_grad(h1, drest, "dw_in_rest", None, 3 * aw, ni)
    early = scatter_rest and _join(scatter_rest([big[n] for n in REST]), scatter_in(dw_in, True, None))
    dq, dk, dv, *scattered = _attn_bwd(qkv, o, do, aw, t, early)
    col0 = 0
    for name, piece in (("dw_in_q", dq), ("dw_in_k", dk), ("dw_in_v", dv)):
        dw_in = _weight_grad(h1, piece, name, dw_in, col0, ni)
        col0 += piece.shape[1]
    big["w_in"] = dw_in
    late = scatter_in and scatter_in(dw_in, False, scattered.pop())
    dx, dg_pre_mix, *scattered_in = _in_proj_bwd(x, dx1, [dq, dk, dv, drest], small["g_pre_mix"], full["w_in"],
                                                tm_wide, late)
    small_grads = {"g_pre_mix": dg_pre_mix, "b_gate": db_gate, "g_post_mix": dg_post_mix, "g_pre_mlp": dg_pre_mlp,
                   "g_post_mlp": dg_post_mlp, "g_ple": dg_ple, "w_conv": dwconv}
    return loss[0, 0], dx, big, small_grads, scattered_in + scattered


PACK_ROWS = 16


def _pack_layout(shapes, d):
    slots, at = [], 0
    for i, (r, c) in enumerate(shapes):
        assert d % c == 0
        for row in range(r):
            slots.append((i, row, at // d, at % d))
            at += c
        at = -(-at // d) * d
    assert at <= PACK_ROWS * d
    return slots


def _early_blocks(width, n_late_cols):
    return tuple(b for b in range(N_DEV) if b * width >= n_late_cols)


def _late_blocks(width, n_late_cols):
    return tuple(b for b in range(N_DEV) if b * width < n_late_cols)


def _pack_small(groups, d):
    shapes = [a.shape for a in groups[0]]
    slots = _pack_layout(shapes, d)
    n = len(shapes)

    def body(*refs):
        ins, outs = refs[:n * len(groups)], refs[n * len(groups):]
        for g, out in enumerate(outs):
            out[...] = jnp.zeros_like(out)
            for i, row, pr, pc in slots:
                src = ins[g * n + i]
                out[pr:pr + 1, pc:pc + shapes[i][1]] = src[row:row + 1, :]

    vmem = pl.BlockSpec(memory_space=pltpu.VMEM)
    return pl.pallas_call(
        body, name="pack_small", in_specs=[vmem] * (n * len(groups)), out_specs=[vmem] * len(groups),
        out_shape=[jax.ShapeDtypeStruct((PACK_ROWS, d), F32)] * len(groups),
    )(*[a for group in groups for a in group])


def _unpack_small(pack, shapes, d):
    slots = _pack_layout(shapes, d)
    return [jnp.stack([pack[pr, pc:pc + shapes[i][1]] for j, row, pr, pc in slots if j == i])
            for i in range(len(shapes))]


def kernel(x, p, g_pre_mix, w_in, b_gate, w_conv, w_attn_out, w_conv_out, w_o, g_post_mix, g_pre_mlp, w_up, w_down, g_post_mlp, g_ple, w_ple_gate, w_ple_proj, loss_target, m_g_pre_mix, m_w_in, m_b_gate, m_w_conv, m_w_attn_out, m_w_conv_out, m_w_o, m_g_post_mix, m_g_pre_mlp, m_w_up, m_w_down, m_g_post_mlp, m_g_ple, m_w_ple_gate, m_w_ple_proj, v_g_pre_mix, v_w_in, v_b_gate, v_w_conv, v_w_attn_out, v_w_conv_out, v_w_o, v_g_post_mix, v_g_pre_mlp, v_w_up, v_w_down, v_g_post_mlp, v_g_ple, v_w_ple_gate, v_w_ple_proj):
    given = dict(locals())
    order = ["g_pre_mix", "w_in", "b_gate", "w_conv", "w_attn_out", "w_conv_out", "w_o", "g_post_mix", "g_pre_mlp",
             "w_up", "w_down", "g_post_mlp", "g_ple", "w_ple_gate", "w_ple_proj"]
    d = x.shape[-1]
    me = 4 * lax.axis_index("x") + 2 * lax.axis_index("y") + lax.axis_index("c")

    col = [COL_SHARDED[n] for n in BIG]
    shards = _to_bf16([given[n][0] for n in BIG])
    cw_shard = w_conv.shape[-1]
    conv_tile = jnp.pad(w_conv[0], ((0, HALO - CONV_K), (0, LANES - cw_shard)))
    w_in_full, conv_g = _exchange_call(
        _join(_gather_exchange(shards[:1], col[:1]), _broadcast_exchange([conv_tile])), "gather_w_in")
    wconv = jnp.concatenate([conv_g[dev, :CONV_K, :cw_shard] for dev in range(N_DEV)], axis=1)

    ni_shard = w_in.shape[-1]
    small = {n: given[n] for n in SMALL}
    loss, dx, big_grads, small_grads, parts = _local_grads(
        x[0], p[0, 0], loss_target[0], small, wconv, {"w_in": w_in_full}, w_attn_out.shape[1], w_conv_out.shape[1],
        ROW_BLOCK, ATTN_BLOCK,
        _gather_exchange(shards[1:], col[1:]), lambda grads: _scatter_exchange(grads, col[1:]),
        lambda grad, early, into: _scatter_exchange(
            [grad], col[:1], _early_blocks(ni_shard, 3 * w_attn_out.shape[1]) if early
            else _late_blocks(ni_shard, 3 * w_attn_out.shape[1]), None if into is None else [into]))
    small_names = list(SMALL) + ["w_conv"]
    two_d = lambda a: a.reshape(-1, d) if a.shape[-1] > d else a.reshape(-1, a.shape[-1])
    full_conv = lambda a: lax.dynamic_update_slice(jnp.zeros((CONV_K, N_DEV * cw_shard), F32), a[0],
                                                   (jnp.int32(0), me * cw_shard))
    groups = [[two_d(small_grads[n]) for n in small_names] + [loss.reshape(1, 1)]]
    for pre in ("", "m_", "v_"):
        groups.append([two_d(given[pre + n]) for n in SMALL] + [full_conv(given[pre + "w_conv"]), jnp.zeros((1, 1), F32)])
    pack, *state = _pack_small(groups, d)
    packs, = _exchange_call(_broadcast_exchange([pack]), "share_small_grads")

    grads, deltas, new_m, new_v = {}, {}, {}, {}
    for n, part in zip(BIG, parts):
        grads[n], deltas[n], new_m[n], new_v[n] = (
            a[None] for a in _sum_and_adamw(part, given[n][0], given["m_" + n][0], given["v_" + n][0], "adamw_" + n))

    outs = _sum_and_adamw(packs, *state, "adamw_small")
    shapes = [a.shape for a in groups[0]]
    for res, dst in zip(outs, (grads, deltas, new_m, new_v)):
        for n, a in zip(small_names + ["loss"], _unpack_small(res, shapes, d)):
            if n == "w_conv":
                a = lax.dynamic_slice(a, (jnp.int32(0), me * cw_shard), (CONV_K, cw_shard))[None]
            dst[n] = a.reshape(given[n].shape) if n in SMALL else a
    loss = grads["loss"][0, 0]

    return (loss, dx[None], *[grads[n] for n in order], *[deltas[n] for n in order],
            *[new_m[n] for n in order], *[new_v[n] for n in order])
```

```python
import jax
import jax.numpy as jnp
from jax import lax
from jax.experimental import pallas as pl
from jax.experimental.pallas import tpu as pltpu

F32 = jnp.float32
BF16 = jnp.bfloat16
RMS_EPS = 1e-6
N_DEV = 8
MESH_AXES = ("x", "y", "c")
LANES = 128
HEAD_DIM = 64
HEADS_PER_GROUP = LANES // HEAD_DIM
CONV_K = 3
HALO = 8
HALO_BF16 = 16
VMEM_LIMIT = 56 * 1024 * 1024
EXP2_ZERO = -150.0
LOG2_E = 1.4426950408889634

ADAM_LR = 0.001
ADAM_B1 = 0.9
ADAM_B2 = 0.999
ADAM_EPS = 1e-08
ADAM_WD = 0.01
ADAM_STEP = 10

ROW_BLOCK = 256
WIDE_BLOCKS = 2
ATTN_BLOCK = 256
ATTN_ROW_SPLITS = 2
DW_TOKENS = 2048
DW_TILE = 1024
DW_MANY_TILES = 6
DW_PIECE_TILE = 512
FF_CHUNK = 1024
PROJ_CHUNK = 512


def _dot(a, b):
    return lax.dot_general(a, b, (((1,), (0,)), ((), ())), preferred_element_type=F32)


def _dot_nt(a, b):
    return lax.dot_general(a, b, (((1,), (1,)), ((), ())), preferred_element_type=F32)


def _dot_tn(a, b):
    return lax.dot_general(a, b, (((0,), (0,)), ((), ())), preferred_element_type=F32)


def _sigmoid(z):
    return 1.0 / (1.0 + jnp.exp(-z))


def _rms_scale(x):
    return lax.rsqrt(jnp.mean(x * x, axis=-1, keepdims=True) + RMS_EPS)


def _rms_bwd(xhat, r, g, dy):
    gd = dy * g
    return r * (gd - xhat * jnp.mean(gd * xhat, axis=-1, keepdims=True))


def _params(n_axes, **kw):
    return pltpu.CompilerParams(dimension_semantics=("arbitrary",) * n_axes, vmem_limit_bytes=VMEM_LIMIT, **kw)


def _load_resident(pairs, sem):
    @pl.when(pl.program_id(0) == 0)
    def _():
        copies = [pltpu.make_async_copy(src, dst, sem.at[i]) for i, (src, dst) in enumerate(pairs)]
        for cp in copies:
            cp.start()
        for cp in copies:
            cp.wait()


def _row_spec(tm, width):
    return pl.BlockSpec((tm, width), lambda i: (i, 0))


def _prev_halo_spec(tm, width, rows):
    per = tm // rows
    return pl.BlockSpec((rows, width), lambda i: (jnp.maximum(i * per - 1, 0), 0))


def _const_spec(shape):
    return pl.BlockSpec(shape, lambda i: (0,) * len(shape))


ANY = pl.BlockSpec(memory_space=pl.ANY)


def _shift_down(cur, prev, n):
    rows = lax.broadcasted_iota(jnp.int32, cur.shape, 0)
    out = pltpu.roll(cur, n, 0)
    for j in range(n):
        out = jnp.where(rows == j, prev[prev.shape[0] - n + j:prev.shape[0] - n + j + 1, :], out)
    return out


def _shift_up(cur, nxt, n):
    tm = cur.shape[0]
    rows = lax.broadcasted_iota(jnp.int32, cur.shape, 0)
    out = pltpu.roll(cur, tm - n, 0)
    for j in range(n):
        out = jnp.where(rows == tm - n + j, nxt[j:j + 1, :], out)
    return out


def _conv_taps(cm, cm_prev, wconv):
    cm1 = _shift_down(cm, cm_prev, 1)
    cm2 = _shift_down(cm, cm_prev, 2)
    cv = wconv[2:3, :] * cm + wconv[1:2, :] * cm1 + wconv[0:1, :] * cm2
    return cv, cm1, cm2


def _in_proj_fwd(x, g1, b_gate, w_in, aw, cw, tm):
    s, d = x.shape
    ni = w_in.shape[1]
    n_qkv, n_conv = 3 * aw, 3 * cw
    ch = PROJ_CHUNK

    def body(x_ref, g_ref, b_ref, w_hbm, qkv_ref, conv_ref, gate_ref, h_ref, w_vmem, sem):
        _load_resident([(w_hbm, w_vmem)], sem)
        xv = x_ref[...]
        h = (xv * _rms_scale(xv) * g_ref[...]).astype(BF16)
        h_ref[...] = h
        for c0 in range(0, ni, ch):
            pc = _dot(h, w_vmem[:, c0:c0 + ch])
            if c0 < n_qkv:
                qkv_ref[:, c0:c0 + ch] = pc.astype(BF16)
            elif c0 < n_qkv + n_conv:
                conv_ref[:, c0 - n_qkv:c0 - n_qkv + ch] = pc.astype(BF16)
            else:
                g0 = c0 - n_qkv - n_conv
                gate_ref[:, g0:g0 + ch] = _sigmoid(pc + b_ref[:, g0:g0 + ch]).astype(BF16)

    return pl.pallas_call(
        body, name="in_proj_fwd", grid=(s // tm,),
        in_specs=[_row_spec(tm, d), _const_spec((1, d)), _const_spec((1, 2 * d)), ANY],
        out_specs=[_row_spec(tm, n_qkv), _row_spec(tm, n_conv), _row_spec(tm, 2 * d), _row_spec(tm, d)],
        out_shape=[jax.ShapeDtypeStruct((s, n_qkv), BF16), jax.ShapeDtypeStruct((s, n_conv), BF16),
                   jax.ShapeDtypeStruct((s, 2 * d), BF16), jax.ShapeDtypeStruct((s, d), BF16)],
        scratch_shapes=[pltpu.VMEM((d, ni), BF16), pltpu.SemaphoreType.DMA((1,))],
        compiler_params=_params(1),
    )(x, g1, b_gate, w_in)


def _split_hi_lo(a):
    hi = a.astype(BF16)
    return hi, (a - hi.astype(F32)).astype(BF16)


def _log2_gates(z):
    z2 = z * LOG2_E
    nz2 = -z2
    log_keep = jnp.minimum(nz2, 0.0) - jnp.log2(1.0 + jnp.exp2(jnp.minimum(z2, nz2)))
    return log_keep + z2, log_keep


def _attn_masks(t):
    row = lax.broadcasted_iota(jnp.int32, (t, t), 0)
    col = lax.broadcasted_iota(jnp.int32, (t, t), 1)
    return (col < row).astype(F32), (row > col).astype(BF16), (row >= col).astype(BF16)


def _chains(a):
    tr = a.shape[0] // ATTN_ROW_SPLITS
    return [jnp.where(_head_lanes(h), a[r * tr:(r + 1) * tr], jnp.zeros((tr, LANES), a.dtype))
            for h in range(HEADS_PER_GROUP) for r in range(ATTN_ROW_SPLITS)]


def _merge_chains(parts):
    rows = []
    for r in range(ATTN_ROW_SPLITS):
        out = parts[r]
        for h in range(1, HEADS_PER_GROUP):
            out = jnp.where(_head_lanes(h), parts[h * ATTN_ROW_SPLITS + r], out)
        rows.append(out)
    return jnp.concatenate(rows, axis=0)


def _by_stage(n_chains, stages):
    for stage in stages:
        for c in range(n_chains):
            stage(c)


def _row_parts(a):
    tr = a.shape[0] // ATTN_ROW_SPLITS
    return [a[r * tr:(r + 1) * tr] for r in range(ATTN_ROW_SPLITS)]


def _while_weights_live(qi, block, carry):
    def cond(state):
        j, carry = state
        live = jnp.max(carry[0][0])
        for run in carry[0][1:]:
            live = jnp.maximum(live, jnp.max(run))
        return jnp.logical_and(j < qi, live >= EXP2_ZERO)

    def step(state):
        j, carry = state
        return j + 1, block(qi - 1 - j, carry)

    return lax.while_loop(cond, step, (jnp.int32(0), carry))[1]


def _head_lanes(h):
    lane = lax.broadcasted_iota(jnp.int32, (1, LANES), 1)
    return (lane >= HEAD_DIM * h) & (lane < HEAD_DIM * (h + 1))


def _attn_fwd(qkv, aw, t, exchange=None):
    s = qkv.shape[0]
    groups = aw // LANES
    nq = s // t
    scale = HEAD_DIM ** -0.5
    ex = exchange or _NO_EXCHANGE
    causal, upper, _ = _attn_masks(t)
    mask_spec = pl.BlockSpec((t, t), lambda g, i: (0, 0))

    def body(q_ref, k_ref, v_ref, causal_ref, upper_ref, *rest):
        ex_in, (o_ref, ob_ref), ex_out, sems = _split_refs(rest, ex, 2)
        qi = pl.program_id(1)
        _exchange_start(ex, ex_in, ex_out, sems, (groups, nq))
        upper = upper_ref[...]
        causal = _row_parts(causal_ref[...] > 0.5) * HEADS_PER_GROUP
        qs = _chains(q_ref[...] * scale)
        heads = range(len(qs))
        tr = t // ATTN_ROW_SPLITS

        def block(kb, runs, accs, diag):
            rows = pl.ds(pl.multiple_of(kb * t, t), t)
            k = k_ref[rows, :]
            v = v_ref[rows, :]
            ncs = [(h % ATTN_ROW_SPLITS + 1) * tr if diag else t for h in heads]
            live = [{} for _ in heads]
            new_runs, new_accs = [None] * len(heads), [None] * len(heads)

            def scores(h):
                live[h]["z"] = _dot_nt(qs[h], k[0:ncs[h]])

            def gates(h):
                nc = ncs[h]
                log_b, log_keep = _log2_gates(live[h].pop("z"))
                if diag:
                    log_keep = jnp.where(causal[h][:, 0:nc], log_keep, 0.0)
                hi, lo = _split_hi_lo(log_keep)
                live[h]["log_w"] = log_b + runs[h]
                live[h]["between"] = _dot(hi, upper[0:nc, 0:nc]) + _dot(lo, upper[0:nc, 0:nc])
                new_runs[h] = runs[h] + jnp.sum(log_keep, axis=1, keepdims=True)

            def weights(h):
                nc = ncs[h]
                w = jnp.exp2(live[h].pop("log_w") + live[h].pop("between"))
                if diag:
                    w = jnp.where(causal[h][:, 0:nc], w, 0.0)
                new_accs[h] = accs[h] + _dot(w.astype(BF16), v[0:nc])

            _by_stage(len(heads), [scores, gates, weights])
            return tuple(new_runs), tuple(new_accs)

        carry = block(qi, [jnp.zeros((tr, 1), F32)] * len(heads), [jnp.zeros((tr, LANES), F32)] * len(heads), True)
        _, accs = _while_weights_live(qi, lambda kb, carry: block(kb, *carry, False), carry)
        o = _merge_chains(accs)
        o_ref[...] = o
        ob_ref[...] = o.astype(BF16)
        _exchange_wait(ex, ex_in, ex_out, sems, (groups, nq))

    return pl.pallas_call(
        body, name="attn_fwd", grid=(groups, nq),
        in_specs=[pl.BlockSpec((t, LANES), lambda g, i: (i, g)),
                  pl.BlockSpec((s, LANES), lambda g, i: (0, groups + g)),
                  pl.BlockSpec((s, LANES), lambda g, i: (0, 2 * groups + g)), mask_spec, mask_spec]
        + [ANY] * len(ex.arrays),
        out_specs=[pl.BlockSpec((t, LANES), lambda g, i: (i, g))] * 2 + [ANY] * len(ex.out_shapes),
        out_shape=[jax.ShapeDtypeStruct((s, aw), F32), jax.ShapeDtypeStruct((s, aw), BF16)] + ex.out_shapes,
        scratch_shapes=_exchange_sems(ex),
        compiler_params=_params(2),
    )(qkv, qkv, qkv, causal, upper, *ex.arrays)


def _attn_bwd(qkv, o, do, aw, t, exchange=None):
    s = qkv.shape[0]
    groups = aw // LANES
    nq = s // t
    scale = HEAD_DIM ** -0.5
    ex = exchange or _NO_EXCHANGE

    def body(q_ref, k_ref, v_ref, o_ref, do_ref, causal_ref, upper_ref, lower_ref, *rest):
        ex_in, (dq_ref, dk_ref, dv_ref), ex_out, (dk_acc, dv_acc, *sems) = _split_refs(rest, ex, 3)
        qi = pl.program_id(1)
        _exchange_start(ex, ex_in, ex_out, sems, (groups, nq))

        @pl.when(qi == 0)
        def _():
            dk_acc[...] = jnp.zeros_like(dk_acc)
            dv_acc[...] = jnp.zeros_like(dv_acc)

        upper = upper_ref[...]
        lower_incl = lower_ref[...]
        causal = _row_parts(causal_ref[...] > 0.5) * HEADS_PER_GROUP
        q = q_ref[...] * scale
        do_b = do_ref[...]
        qs = _chains(q)
        dos = _chains(do_b)
        qs_all = jnp.concatenate(qs, axis=0)
        dos_all = jnp.concatenate(dos, axis=0)
        e_totals = [jnp.sum(part, axis=1, keepdims=True) for part in _chains(do_b.astype(F32) * o_ref[...])]
        heads = range(len(qs))
        tr = t // ATTN_ROW_SPLITS

        def block(kb, runs, e_runs, dqs, diag):
            rows = pl.ds(pl.multiple_of(kb * t, t), t)
            k = k_ref[rows, :]
            v = v_ref[rows, :]
            ncs = [(h % ATTN_ROW_SPLITS + 1) * tr if diag else t for h in heads]
            live = [{} for _ in heads]
            none = [None] * len(heads)
            new_runs, new_e_runs, new_dqs, dzbs, wbs = list(none), list(none), list(none), list(none), list(none)

            def scores(h):
                live[h]["z"] = _dot_nt(qs[h], k[0:ncs[h]])
                live[h]["dw"] = _dot_nt(dos[h], v[0:ncs[h]])

            def gates(h):
                nc = ncs[h]
                log_b, log_keep = _log2_gates(live[h].pop("z"))
                live[h]["beta"] = jnp.exp2(log_b)
                live[h]["keep"] = jnp.exp2(log_keep)
                if diag:
                    log_keep = jnp.where(causal[h][:, 0:nc], log_keep, 0.0)
                hi, lo = _split_hi_lo(log_keep)
                live[h]["log_w"] = log_b + runs[h]
                live[h]["between"] = _dot(hi, upper[0:nc, 0:nc]) + _dot(lo, upper[0:nc, 0:nc])
                new_runs[h] = runs[h] + jnp.sum(log_keep, axis=1, keepdims=True)

            def weights(h):
                nc = ncs[h]
                w = jnp.exp2(live[h].pop("log_w") + live[h].pop("between"))
                if diag:
                    w = jnp.where(causal[h][:, 0:nc], w, 0.0)
                wb = w.astype(BF16)
                e = live[h].pop("dw") * wb.astype(F32)
                hi, lo = _split_hi_lo(e)
                live[h]["e"] = e
                live[h]["e_suffix"] = _dot(hi, lower_incl[0:nc, 0:nc]) + _dot(lo, lower_incl[0:nc, 0:nc]) + e_runs[h]
                wbs[h] = wb

            def score_grads(h):
                nc = ncs[h]
                e_suffix = live[h].pop("e_suffix")
                dz = live[h].pop("e") * live[h].pop("keep") - (e_totals[h] - e_suffix) * live[h].pop("beta")
                if diag:
                    dz = jnp.where(causal[h][:, 0:nc], dz, 0.0)
                dzb = dz.astype(BF16)
                new_dqs[h] = dqs[h] + _dot(dzb, k[0:nc])
                new_e_runs[h] = e_suffix[:, 0:1]
                if nc < t:
                    unseen = jnp.zeros((tr, t - nc), BF16)
                    dzb = jnp.concatenate([dzb, unseen], axis=1)
                    wbs[h] = jnp.concatenate([wbs[h], unseen], axis=1)
                dzbs[h] = dzb

            _by_stage(len(heads), [scores, gates, weights, score_grads])
            dk_acc[rows, :] += _dot_tn(jnp.concatenate(dzbs, axis=0), qs_all)
            dv_acc[rows, :] += _dot_tn(jnp.concatenate(wbs, axis=0), dos_all)
            return tuple(new_runs), tuple(new_e_runs), tuple(new_dqs)

        zero_cols = [jnp.zeros((tr, 1), F32)] * len(heads)
        carry = block(qi, zero_cols, zero_cols, [jnp.zeros((tr, LANES), F32)] * len(heads), True)
        _, _, dqs = _while_weights_live(qi, lambda kb, carry: block(kb, *carry, False), carry)
        dq_ref[...] = (_merge_chains(dqs) * scale).astype(BF16)

        @pl.when(qi == nq - 1)
        def _():
            dk_ref[...] = dk_acc[...].astype(BF16)
            dv_ref[...] = dv_acc[...].astype(BF16)

        _exchange_wait(ex, ex_in, ex_out, sems, (groups, nq))

    blk = pl.BlockSpec((t, LANES), lambda g, i: (i, g))
    slab = pl.BlockSpec((s, LANES), lambda g, i: (0, g))
    mask_spec = pl.BlockSpec((t, t), lambda g, i: (0, 0))
    return pl.pallas_call(
        body, name="attn_bwd", grid=(groups, nq),
        in_specs=[blk, pl.BlockSpec((s, LANES), lambda g, i: (0, groups + g)),
                  pl.BlockSpec((s, LANES), lambda g, i: (0, 2 * groups + g)), blk, blk, mask_spec, mask_spec, mask_spec]
        + [ANY] * len(ex.arrays),
        out_specs=[blk, slab, slab] + [ANY] * len(ex.out_shapes),
        out_shape=[jax.ShapeDtypeStruct((s, aw), BF16)] * 3 + ex.out_shapes,
        scratch_shapes=[pltpu.VMEM((s, LANES), F32), pltpu.VMEM((s, LANES), F32)] + _exchange_sems(ex),
        compiler_params=_params(2),
    )(qkv, qkv, qkv, o, do, *_attn_masks(t), *ex.arrays)


def _branches(o_b, conv, conv_prev, wconv, w_ao, w_co, cw, first):
    conv = conv.astype(F32)
    conv_prev = conv_prev.astype(F32)
    cb = conv[:, 0:cw]
    cm = conv[:, cw:2 * cw] * conv[:, 2 * cw:3 * cw]
    cm_prev = conv_prev[:, cw:2 * cw] * conv_prev[:, 2 * cw:3 * cw]
    cm_prev = jnp.where(first, 0.0, cm_prev)
    cv, cm1, cm2 = _conv_taps(cm, cm_prev, wconv)
    conv_in = (cb * cv).astype(BF16)
    return _dot(o_b, w_ao), _dot(conv_in, w_co), conv_in, cb, cv, cm, cm1, cm2


def _mix_fwd(x, o, conv, gate, wconv, g_post, w_ao, w_co, w_o, tm):
    s, d = x.shape
    aw, cw = w_ao.shape[0], w_co.shape[0]

    def body(x_ref, o_ref, conv_ref, prev_ref, gate_ref, wc_ref, g_ref, wao_hbm, wco_hbm, wo_hbm,
             x1_ref, mixed_ref, mixin_ref, convin_ref, wao, wco, wo, sem):
        _load_resident([(wao_hbm, wao), (wco_hbm, wco), (wo_hbm, wo)], sem)
        y_attn, y_conv, conv_in, *_ = _branches(
            o_ref[...].astype(BF16), conv_ref[...], prev_ref[...], wc_ref[...], wao[...], wco[...], cw,
            pl.program_id(0) == 0)
        mix_in = (gate_ref[:, 0:d].astype(F32) * y_attn + gate_ref[:, d:2 * d].astype(F32) * y_conv).astype(BF16)
        mixed = _dot(mix_in, wo[...])
        x1_ref[...] = x_ref[...] + mixed * _rms_scale(mixed) * g_ref[...]
        mixed_ref[...] = mixed
        mixin_ref[...] = mix_in
        convin_ref[...] = conv_in

    return pl.pallas_call(
        body, name="mix_fwd", grid=(s // tm,),
        in_specs=[_row_spec(tm, d), _row_spec(tm, aw), _row_spec(tm, 3 * cw), _prev_halo_spec(tm, 3 * cw, HALO_BF16),
                  _row_spec(tm, 2 * d), _const_spec((CONV_K, cw)), _const_spec((1, d)), ANY, ANY, ANY],
        out_specs=[_row_spec(tm, d), _row_spec(tm, d), _row_spec(tm, d), _row_spec(tm, cw)],
        out_shape=[jax.ShapeDtypeStruct((s, d), F32), jax.ShapeDtypeStruct((s, d), F32),
                   jax.ShapeDtypeStruct((s, d), BF16), jax.ShapeDtypeStruct((s, cw), BF16)],
        scratch_shapes=[pltpu.VMEM(w_ao.shape, BF16), pltpu.VMEM(w_co.shape, BF16), pltpu.VMEM(w_o.shape, BF16),
                        pltpu.SemaphoreType.DMA((3,))],
        compiler_params=_params(1),
    )(x, o, conv, conv, gate, wconv, g_post, w_ao, w_co, w_o)


def _mix_bwd(dx1, mixed, o, conv, gate, wconv, g_post, w_ao, w_co, w_o, tm):
    s, d = dx1.shape
    aw, cw = w_ao.shape[0], w_co.shape[0]
    n = s // tm
    per = tm // HALO_BF16

    def body(dx1_ref, mixed_ref, o_ref, conv_ref, prev_ref, gate_ref, wc_ref, g_ref, wao_hbm, wco_hbm, wo_hbm,
             dmixed_ref, dattn_ref, dconvout_ref, do_ref, drest_ref, dg_ref, dbias_ref, dwc_ref,
             wao, wco, wo, dcv_next, sem):
        i = pl.program_id(0)
        _load_resident([(wao_hbm, wao), (wco_hbm, wco), (wo_hbm, wo)], sem)

        @pl.when(i == 0)
        def _():
            dg_ref[...] = jnp.zeros_like(dg_ref)
            dbias_ref[...] = jnp.zeros_like(dbias_ref)
            dwc_ref[...] = jnp.zeros_like(dwc_ref)
            dcv_next[...] = jnp.zeros_like(dcv_next)

        mixed = mixed_ref[...]
        r = _rms_scale(mixed)
        mhat = mixed * r
        dn = dx1_ref[...]
        dg_ref[...] += jnp.sum(dn * mhat, axis=0, keepdims=True)
        dmixed = _rms_bwd(mhat, r, g_ref[...], dn).astype(BF16)
        dmixed_ref[...] = dmixed
        dmi = _dot_nt(dmixed, wo[...])

        wc = wc_ref[...]
        conv = conv_ref[...].astype(F32)
        y_attn, y_conv, _, cb, cv, cm, cm1, cm2 = _branches(
            o_ref[...].astype(BF16), conv, prev_ref[...], wc, wao[...], wco[...], cw, i == n - 1)
        ga = gate_ref[:, 0:d].astype(F32)
        gc = gate_ref[:, d:2 * d].astype(F32)
        dpre_a = dmi * y_attn * ga * (1.0 - ga)
        dpre_c = dmi * y_conv * gc * (1.0 - gc)
        drest_ref[:, 3 * cw:3 * cw + d] = dpre_a.astype(BF16)
        drest_ref[:, 3 * cw + d:3 * cw + 2 * d] = dpre_c.astype(BF16)
        dbias_ref[:, 0:d] += jnp.sum(dpre_a, axis=0, keepdims=True)
        dbias_ref[:, d:2 * d] += jnp.sum(dpre_c, axis=0, keepdims=True)

        dattn = (dmi * ga).astype(BF16)
        dattn_ref[...] = dattn
        do_ref[...] = _dot_nt(dattn, wao[...]).astype(BF16)
        dconvout = (dmi * gc).astype(BF16)
        dconvout_ref[...] = dconvout
        dconv_in = _dot_nt(dconvout, wco[...])
        drest_ref[:, 0:cw] = (dconv_in * cv).astype(BF16)

        dcv = dconv_in * cb
        following = dcv_next[...]
        dcm = wc[2:3, :] * dcv + wc[1:2, :] * _shift_up(dcv, following, 1) + wc[0:1, :] * _shift_up(dcv, following, 2)
        drest_ref[:, cw:2 * cw] = (dcm * conv[:, 2 * cw:3 * cw]).astype(BF16)
        drest_ref[:, 2 * cw:3 * cw] = (dcm * conv[:, cw:2 * cw]).astype(BF16)
        for tap, shifted in enumerate((cm2, cm1, cm)):
            dwc_ref[tap:tap + 1, :] += jnp.sum(dcv * shifted, axis=0, keepdims=True)
        dcv_next[...] = dcv[0:HALO, :]

    def rows(width):
        return pl.BlockSpec((tm, width), lambda i: (n - 1 - i, 0))

    prev_halo = pl.BlockSpec((HALO_BF16, 3 * cw), lambda i: (jnp.maximum((n - 1 - i) * per - 1, 0), 0))
    n_rest = 3 * cw + 2 * d
    return pl.pallas_call(
        body, name="mix_bwd", grid=(n,),
        in_specs=[rows(d), rows(d), rows(aw), rows(3 * cw), prev_halo, rows(2 * d), _const_spec((CONV_K, cw)),
                  _const_spec((1, d)), ANY, ANY, ANY],
        out_specs=[rows(d), rows(d), rows(d), rows(aw), rows(n_rest), _const_spec((1, d)), _const_spec((1, 2 * d)),
                   _const_spec((CONV_K, cw))],
        out_shape=[jax.ShapeDtypeStruct((s, d), BF16), jax.ShapeDtypeStruct((s, d), BF16),
                   jax.ShapeDtypeStruct((s, d), BF16), jax.ShapeDtypeStruct((s, aw), BF16),
                   jax.ShapeDtypeStruct((s, n_rest), BF16), jax.ShapeDtypeStruct((1, d), F32),
                   jax.ShapeDtypeStruct((1, 2 * d), F32), jax.ShapeDtypeStruct((CONV_K, cw), F32)],
        scratch_shapes=[pltpu.VMEM(w_ao.shape, BF16), pltpu.VMEM(w_co.shape, BF16), pltpu.VMEM(w_o.shape, BF16),
                        pltpu.VMEM((HALO, cw), F32), pltpu.SemaphoreType.DMA((3,))],
        compiler_params=_params(1),
    )(dx1, mixed, o, conv, conv, gate, wconv, g_post, w_ao, w_co, w_o)


def _mlp_ple_loss(x1, p, target, g_pre, g_post, g_ple, w_up, w_dn, w_pg, w_pp, tm):
    s, d = x1.shape
    ff = w_up.shape[1]
    pd = p.shape[1]
    fc = FF_CHUNK

    def body(x1_ref, p_ref, t_ref, gpre_ref, gpost_ref, gple_ref, wup_hbm, wdn_hbm, wpg_hbm, wpp_hbm,
             dx1_ref, h2_ref, du_ref, a_ref, df_ref, h3_ref, ds3_ref, dpp_ref, loss_ref, dgpre_ref, dgpost_ref,
             dgple_ref, wup, wdn, wpg, wpp, u_scr, sem):
        _load_resident([(wup_hbm, wup), (wdn_hbm, wdn), (wpg_hbm, wpg), (wpp_hbm, wpp)], sem)

        @pl.when(pl.program_id(0) == 0)
        def _():
            for ref in (loss_ref, dgpre_ref, dgpost_ref, dgple_ref):
                ref[...] = jnp.zeros_like(ref)

        x1v = x1_ref[...]
        r2 = _rms_scale(x1v)
        x1hat = x1v * r2
        h2 = (x1hat * gpre_ref[...]).astype(BF16)
        h2_ref[...] = h2
        f = jnp.zeros((tm, d), F32)
        for c0 in range(0, ff, fc):
            u = _dot(h2, wup[:, c0:c0 + fc])
            u_scr[:, c0:c0 + fc] = u
            a = jnp.square(jnp.maximum(u, 0.0)).astype(BF16)
            a_ref[:, c0:c0 + fc] = a
            f = f + _dot(a, wdn[c0:c0 + fc, :])
        rf = _rms_scale(f)
        fhat = f * rf
        x2 = x1v + fhat * gpost_ref[...]
        r3 = _rms_scale(x2)
        x2hat = x2 * r3
        h3 = (x2hat * gple_ref[...]).astype(BF16)
        h3_ref[...] = h3
        pg = _sigmoid(_dot(h3, wpg[...]))
        pp = _dot(p_ref[...].astype(BF16), wpp[...])
        diff = x2 + pg * pp - t_ref[...]
        loss_ref[...] += 0.5 * jnp.sum(jnp.mean(diff * diff, axis=-1, keepdims=True), axis=0, keepdims=True)

        dy = diff * (1.0 / d)
        dpp_ref[...] = (dy * pg).astype(BF16)
        ds3 = (dy * pp * pg * (1.0 - pg)).astype(BF16)
        ds3_ref[...] = ds3
        dh3 = _dot_nt(ds3, wpg[...])
        dgple_ref[...] += jnp.sum(dh3 * x2hat, axis=0, keepdims=True)
        dx2 = dy + _rms_bwd(x2hat, r3, gple_ref[...], dh3)
        dgpost_ref[...] += jnp.sum(dx2 * fhat, axis=0, keepdims=True)
        df = _rms_bwd(fhat, rf, gpost_ref[...], dx2).astype(BF16)
        df_ref[...] = df
        dh2 = jnp.zeros((tm, d), F32)
        for c0 in range(0, ff, fc):
            da = _dot_nt(df, wdn[c0:c0 + fc, :])
            du = (da * (2.0 * jnp.maximum(u_scr[:, c0:c0 + fc], 0.0))).astype(BF16)
            du_ref[:, c0:c0 + fc] = du
            dh2 = dh2 + _dot_nt(du, wup[:, c0:c0 + fc])
        dgpre_ref[...] += jnp.sum(dh2 * x1hat, axis=0, keepdims=True)
        dx1_ref[...] = dx2 + _rms_bwd(x1hat, r2, gpre_ref[...], dh2)

    vec = _const_spec((1, d))
    return pl.pallas_call(
        body, name="mlp_ple_loss", grid=(s // tm,),
        in_specs=[_row_spec(tm, d), _row_spec(tm, pd), _row_spec(tm, d), vec, vec, vec, ANY, ANY, ANY, ANY],
        out_specs=[_row_spec(tm, d), _row_spec(tm, d), _row_spec(tm, ff), _row_spec(tm, ff), _row_spec(tm, d),
                   _row_spec(tm, d), _row_spec(tm, d), _row_spec(tm, d), _const_spec((1, 1)), vec, vec, vec],
        out_shape=[jax.ShapeDtypeStruct((s, d), F32), jax.ShapeDtypeStruct((s, d), BF16),
                   jax.ShapeDtypeStruct((s, ff), BF16), jax.ShapeDtypeStruct((s, ff), BF16),
                   jax.ShapeDtypeStruct((s, d), BF16), jax.ShapeDtypeStruct((s, d), BF16),
                   jax.ShapeDtypeStruct((s, d), BF16), jax.ShapeDtypeStruct((s, d), BF16),
                   jax.ShapeDtypeStruct((1, 1), F32), jax.ShapeDtypeStruct((1, d), F32),
                   jax.ShapeDtypeStruct((1, d), F32), jax.ShapeDtypeStruct((1, d), F32)],
        scratch_shapes=[pltpu.VMEM(w_up.shape, BF16), pltpu.VMEM(w_dn.shape, BF16), pltpu.VMEM(w_pg.shape, BF16),
                        pltpu.VMEM(w_pp.shape, BF16), pltpu.VMEM((tm, ff), F32), pltpu.SemaphoreType.DMA((4,))],
        compiler_params=_params(1),
    )(x1, p, target, g_pre, g_post, g_ple, w_up, w_dn, w_pg, w_pp)


def _in_proj_bwd(x, dx1, pieces, g1, w_in, tm, exchange=None):
    s, d = x.shape
    ni = w_in.shape[1]
    widths = [p.shape[1] for p in pieces]
    grid = (s // tm,)
    ex = exchange or _NO_EXCHANGE

    def body(x_ref, dx1_ref, *rest):
        piece_refs, rest = rest[:len(pieces)], rest[len(pieces):]
        g_ref, w_hbm = rest[0], rest[1]
        ex_in, (dx_ref, dg_ref), ex_out, (w_vmem, sem, *sems) = _split_refs(rest[2:], ex, 2)
        _exchange_start(ex, ex_in, ex_out, sems, grid)
        _load_resident([(w_hbm, w_vmem)], sem)

        @pl.when(pl.program_id(0) == 0)
        def _():
            dg_ref[...] = jnp.zeros_like(dg_ref)

        dh = jnp.zeros((tm, d), F32)
        c0 = 0
        for ref, width in zip(piece_refs, widths):
            dh = dh + _dot_nt(ref[...], w_vmem[:, c0:c0 + width])
            c0 += width
        xv = x_ref[...]
        r = _rms_scale(xv)
        xhat = xv * r
        dg_ref[...] += jnp.sum(dh * xhat, axis=0, keepdims=True)
        dx_ref[...] = dx1_ref[...] + _rms_bwd(xhat, r, g_ref[...], dh)
        _exchange_wait(ex, ex_in, ex_out, sems, grid)

    return pl.pallas_call(
        body, name="in_proj_bwd", grid=grid,
        in_specs=[_row_spec(tm, d), _row_spec(tm, d)] + [_row_spec(tm, w) for w in widths]
        + [_const_spec((1, d)), ANY] + [ANY] * len(ex.arrays),
        out_specs=[_row_spec(tm, d), _const_spec((1, d))] + [ANY] * len(ex.out_shapes),
        out_shape=[jax.ShapeDtypeStruct((s, d), F32), jax.ShapeDtypeStruct((1, d), F32)] + ex.out_shapes,
        scratch_shapes=[pltpu.VMEM((d, ni), BF16), pltpu.SemaphoreType.DMA((1,))] + _exchange_sems(ex),
        compiler_params=_params(1),
    )(x, dx1, *pieces, g1, w_in, *ex.arrays)


def _hand_over(a, name):
    def body(src_ref, dst_ref):
        del src_ref, dst_ref

    return pl.pallas_call(
        body, name=name, in_specs=[ANY], out_specs=ANY, out_shape=jax.ShapeDtypeStruct(a.shape, a.dtype),
        input_output_aliases={0: 0},
    )(a)


def _weight_grad(a, b, name, into=None, col0=0, n_total=None):
    s, m = a.shape
    n = b.shape[1]
    tm = min(m, DW_TILE)
    tn = min(n, DW_TILE) if n_total is None else DW_PIECE_TILE
    tk = min(s, DW_TOKENS * (2 if (m // tm) * (n // tn) >= DW_MANY_TILES else 1))
    nk = s // tk
    j0 = col0 // tn
    assert m % tm == 0 and n % tn == 0 and col0 % tn == 0

    def body(a_ref, b_ref, *rest):
        o_ref, acc = rest[-2:]
        k = pl.program_id(2)

        @pl.when(k == 0)
        def _():
            acc[...] = jnp.zeros_like(acc)

        acc[...] += _dot_tn(a_ref[...].astype(BF16), b_ref[...].astype(BF16))

        @pl.when(k == nk - 1)
        def _():
            o_ref[...] = acc[...].astype(BF16)

    extra = [] if into is None else [into]
    return pl.pallas_call(
        body, name=name, grid=(m // tm, n // tn, nk),
        in_specs=[pl.BlockSpec((tk, tm), lambda i, j, k: (k, i)), pl.BlockSpec((tk, tn), lambda i, j, k: (k, j))]
        + [ANY] * len(extra),
        out_specs=pl.BlockSpec((tm, tn), lambda i, j, k: (i, j0 + j)),
        out_shape=jax.ShapeDtypeStruct((m, n_total or n), BF16),
        input_output_aliases={2: 0} if extra else {},
        scratch_shapes=[pltpu.VMEM((tm, tn), F32)],
        compiler_params=_params(3),
    )(a, b, *extra)


def _mesh_position():
    return tuple(lax.axis_index(a) for a in MESH_AXES)


def _peer(me, k):
    bits = ((k >> 2) & 1, (k >> 1) & 1, k & 1)
    pos = tuple(1 - m if b else m for m, b in zip(me, bits))
    return pos, 4 * pos[0] + 2 * pos[1] + pos[2]


class _Exchange:
    def __init__(self, arrays, out_shapes, src, dst, relayed=None):
        self.arrays, self.out_shapes, self.src, self.dst = list(arrays), list(out_shapes), src, dst
        self.relayed = list(relayed) if relayed is not None else [False] * len(self.arrays)


_NO_EXCHANGE = _Exchange([], [], None, None)


def _exchange_sems(ex):
    n = len(ex.arrays)
    if n == 0:
        return []
    return [pltpu.SemaphoreType.DMA((n, N_DEV - 1)), pltpu.SemaphoreType.DMA((n, N_DEV - 1)),
            pltpu.SemaphoreType.DMA((n,))]


def _split_refs(rest, ex, n_own_outs):
    n_in, n_out = len(ex.arrays), len(ex.out_shapes)
    ex_in, rest = rest[:n_in], rest[n_in:]
    own, rest = rest[:n_own_outs], rest[n_own_outs:]
    return ex_in, own, rest[:n_out], rest[n_out:]


def _direct_steps(ex, w, in_refs, out_refs, sems):
    send_sems, recv_sems, local_sems = sems
    me = _mesh_position()
    mine = 4 * me[0] + 2 * me[1] + me[2]

    def copy(k):
        landing = ex.dst(w, out_refs, mine)
        if k == 0:
            return pltpu.make_async_copy(ex.src(w, in_refs, mine), landing, local_sems.at[w])
        peer, peer_idx = _peer(me, k)
        return pltpu.make_async_remote_copy(
            src_ref=ex.src(w, in_refs, peer_idx), dst_ref=landing, send_sem=send_sems.at[w, k - 1],
            recv_sem=recv_sems.at[w, k - 1], device_id=peer, device_id_type=pl.DeviceIdType.MESH)

    ks = range(N_DEV)
    return [lambda k=k: copy(k).start() for k in ks], [], [lambda k=k: copy(k).wait() for k in ks]


def _relayed_steps(ex, w, in_refs, out_refs, sems):
    send_sems, recv_sems, local_sems = sems
    x, y, c = _mesh_position()
    chips = [(1 - x, y), (x, 1 - y), (1 - x, 1 - y)]
    sibling = (x, y, 1 - c)
    js = range(len(chips))

    def block(px, py, pc):
        return ex.dst(w, out_refs, 4 * px + 2 * py + pc)

    def copy(k, dst, to, src=None):
        return pltpu.make_async_remote_copy(
            src_ref=ex.src(w, in_refs, None) if src is None else src, dst_ref=dst, send_sem=send_sems.at[w, k],
            recv_sem=recv_sems.at[w, k], device_id=to, device_id_type=pl.DeviceIdType.MESH)

    def local():
        return pltpu.make_async_copy(ex.src(w, in_refs, None), block(x, y, c), local_sems.at[w])

    def own(k):
        return copy(k, block(x, y, c), sibling if k == 0 else (*chips[k - 1], c))

    def came(j):
        return copy(1 + j, block(*chips[j], c), (*chips[j], c))

    def passed(j):
        return copy(4 + j, block(*chips[j], c), sibling, src=block(*chips[j], c))

    def from_sibling(k):
        return copy(k, block(x, y, 1 - c) if k == 0 else block(*chips[k - 4], 1 - c), sibling)

    start = [lambda: local().start()] + [lambda k=k: own(k).start() for k in range(4)]
    relay = [step for j in js for step in (lambda j=j: came(j).wait_recv(), lambda j=j: passed(j).start())]
    finish = ([lambda: local().wait()] + [lambda k=k: own(k).wait_send() for k in range(4)]
              + [lambda j=j: passed(j).wait_send() for j in js]
              + [lambda k=k: from_sibling(k).wait_recv() for k in (0, 4, 5, 6)])
    return start, relay, finish


def _exchange_steps(ex, in_refs, out_refs, sems):
    start, relay, finish = [], [], []
    for w in range(len(ex.arrays)):
        steps = (_relayed_steps if ex.relayed[w] else _direct_steps)(ex, w, in_refs, out_refs, sems)
        start += steps[0]
        relay += steps[1]
        finish += steps[2]
    return start, relay, finish


def _run(steps):
    for step in steps:
        step()


def _at_grid_step(grid, where):
    target = {"first": [0] * len(grid), "middle": [grid[0] // 2] + [0] * (len(grid) - 1),
              "last": [g - 1 for g in grid]}[where]
    hit = pl.program_id(0) == target[0]
    for axis in range(1, len(grid)):
        hit = jnp.logical_and(hit, pl.program_id(axis) == target[axis])
    return hit


def _exchange_start(ex, in_refs, out_refs, sems, grid):
    if ex.arrays:
        @pl.when(_at_grid_step(grid, "first"))
        def _():
            _run(_exchange_steps(ex, in_refs, out_refs, sems)[0])

        if any(ex.relayed):
            assert grid[0] >= 2

            @pl.when(_at_grid_step(grid, "middle"))
            def _():
                _run(_exchange_steps(ex, in_refs, out_refs, sems)[1])


def _exchange_wait(ex, in_refs, out_refs, sems, grid):
    if ex.arrays:
        @pl.when(_at_grid_step(grid, "last"))
        def _():
            _run(_exchange_steps(ex, in_refs, out_refs, sems)[2])


def _shard_block(ref, shard_shape, by_col, idx):
    r, c = shard_shape
    if by_col:
        return ref.at[:, pl.ds(pl.multiple_of(idx * c, LANES), c)]
    return ref.at[pl.ds(pl.multiple_of(idx * r, 16), r), :]


def _full_shape(shard_shape, by_col):
    r, c = shard_shape
    return (r, N_DEV * c) if by_col else (N_DEV * r, c)


def _gather_exchange(shards, col_sharded):
    shapes = [a.shape for a in shards]
    return _Exchange(
        shards, [jax.ShapeDtypeStruct(_full_shape(sh, bc), a.dtype) for a, sh, bc in zip(shards, shapes, col_sharded)],
        lambda w, refs, idx: refs[w],
        lambda w, refs, idx: _shard_block(refs[w], shapes[w], col_sharded[w], idx), [True] * len(shards))


def _scatter_exchange(grads, col_sharded):
    shapes = []
    for g, by_col in zip(grads, col_sharded):
        r, c = g.shape
        shapes.append((r, c // N_DEV) if by_col else (r // N_DEV, c))
    return _Exchange(
        grads, [jax.ShapeDtypeStruct((N_DEV,) + sh, g.dtype) for g, sh in zip(grads, shapes)],
        lambda w, refs, idx: _shard_block(refs[w], shapes[w], col_sharded[w], idx),
        lambda w, refs, mine: refs[w].at[mine])


def _broadcast_exchange(arrays):
    return _Exchange(arrays, [jax.ShapeDtypeStruct((N_DEV,) + a.shape, a.dtype) for a in arrays],
                     lambda w, refs, idx: refs[w], lambda w, refs, mine: refs[w].at[mine])


def _join(*exs):
    arrays, shapes, owner = [], [], []
    for e in exs:
        for w in range(len(e.arrays)):
            owner.append((e, w, len(arrays), len(shapes)))
        arrays += e.arrays
        shapes += e.out_shapes

    def src(w, refs, idx):
        e, w0, i0, _ = owner[w]
        return e.src(w0, refs[i0:i0 + len(e.arrays)], idx)

    def dst(w, refs, idx):
        e, w0, _, o0 = owner[w]
        return e.dst(w0, refs[o0:o0 + len(e.out_shapes)], idx)

    return _Exchange(arrays, shapes, src, dst, [flag for e in exs for flag in e.relayed])


def _exchange_call(ex, name):
    n_in = len(ex.arrays)

    def body(*refs):
        in_refs, _, out_refs, sems = _split_refs(refs, ex, 0)
        for steps in _exchange_steps(ex, in_refs, out_refs, sems):
            _run(steps)

    return pl.pallas_call(
        body, name=name, in_specs=[ANY] * n_in, out_specs=[ANY] * len(ex.out_shapes), out_shape=ex.out_shapes,
        scratch_shapes=_exchange_sems(ex), compiler_params=pltpu.CompilerParams(vmem_limit_bytes=VMEM_LIMIT),
    )(*ex.arrays)


def _to_bf16(arrays):
    def body(*refs):
        for src, dst in zip(refs[:len(arrays)], refs[len(arrays):]):
            dst[...] = src[...].astype(BF16)

    vmem = pl.BlockSpec(memory_space=pltpu.VMEM)
    return pl.pallas_call(
        body, name="weights_to_bf16", in_specs=[vmem] * len(arrays), out_specs=[vmem] * len(arrays),
        out_shape=[jax.ShapeDtypeStruct(a.shape, BF16) for a in arrays],
        compiler_params=pltpu.CompilerParams(vmem_limit_bytes=VMEM_LIMIT),
    )(*arrays)


def _adamw(w, g, m, v):
    m = ADAM_B1 * m + (1.0 - ADAM_B1) * g
    v = ADAM_B2 * v + (1.0 - ADAM_B2) * jnp.square(g)
    m_hat = m / (1.0 - ADAM_B1 ** ADAM_STEP)
    v_hat = v / (1.0 - ADAM_B2 ** ADAM_STEP)
    delta = -ADAM_LR * (m_hat / (jnp.sqrt(v_hat) + ADAM_EPS) + ADAM_WD * w)
    return delta, m, v


def _sum_and_adamw(parts, w, m, v, name):
    r, c = w.shape
    tr = min(r, 256)

    def body(p_ref, w_ref, m_ref, v_ref, g_out, d_out, m_out, v_out):
        g = p_ref[0].astype(F32)
        for dev in range(1, N_DEV):
            g = g + p_ref[dev].astype(F32)
        g_out[...] = g
        d_out[...], m_out[...], v_out[...] = _adamw(w_ref[...], g, m_ref[...], v_ref[...])

    blk = pl.BlockSpec((tr, c), lambda i: (i, 0))
    return pl.pallas_call(
        body, name=name, grid=(r // tr,),
        in_specs=[pl.BlockSpec((N_DEV, tr, c), lambda i: (0, i, 0)), blk, blk, blk],
        out_specs=[blk] * 4, out_shape=[jax.ShapeDtypeStruct((r, c), F32)] * 4,
        compiler_params=_params(1),
    )(parts, w, m, v)


BIG = ("w_in", "w_attn_out", "w_conv_out", "w_o", "w_up", "w_down", "w_ple_gate", "w_ple_proj")
COL_SHARDED = {"w_in": True, "w_attn_out": True, "w_conv_out": True, "w_o": False, "w_up": True, "w_down": False,
               "w_ple_gate": False, "w_ple_proj": True}
SMALL = ("g_pre_mix", "b_gate", "g_post_mix", "g_pre_mlp", "g_post_mlp", "g_ple")


REST = BIG[1:]


def _local_grads(x, p, target, small, wconv, full, aw, cw, tm, t, gather_rest=None, scatter_rest=None,
                 scatter_in=None):
    full = dict(full)
    tm_wide = min(WIDE_BLOCKS * tm, x.shape[0])
    qkv, conv, gate, h1 = _in_proj_fwd(x, small["g_pre_mix"], small["b_gate"], full["w_in"], aw, cw, tm_wide)
    o, o_b, *rest = _attn_fwd(qkv, aw, t, gather_rest)
    full.update(zip(REST, rest))
    x1, mixed, mix_in, conv_in = _mix_fwd(x, o_b, conv, gate, wconv, small["g_post_mix"], full["w_attn_out"],
                                          full["w_conv_out"], full["w_o"], tm_wide)
    (dx1, h2, du, a, df, h3, ds3, dpp, loss, dg_pre_mlp, dg_post_mlp, dg_ple) = _mlp_ple_loss(
        x1, p, target, small["g_pre_mlp"], small["g_post_mlp"], small["g_ple"], full["w_up"], full["w_down"],
        full["w_ple_gate"], full["w_ple_proj"], tm)
    big = {"w_up": _weight_grad(h2, du, "dw_up"), "w_down": _weight_grad(a, df, "dw_down"),
           "w_ple_gate": _weight_grad(h3, ds3, "dw_ple_gate"), "w_ple_proj": _weight_grad(p, dpp, "dw_ple_proj")}
    (dmixed, dattn, dconvout, do, drest, dg_post_mix, db_gate, dwconv) = _mix_bwd(
        dx1, mixed, o_b, conv, gate, wconv, small["g_post_mix"], full["w_attn_out"], full["w_conv_out"], full["w_o"],
        tm_wide)
    big.update({"w_attn_out": _weight_grad(o_b, dattn, "dw_attn_out"),
                "w_conv_out": _weight_grad(conv_in, dconvout, "dw_conv_out"),
                "w_o": _weight_grad(mix_in, dmixed, "dw_o")})
    dq, dk, dv, *scattered = _attn_bwd(qkv, o, do, aw, t, scatter_rest and scatter_rest([big[n] for n in REST]))
    pieces = [dq, dk, dv, drest]
    dw_in, col0, ni = None, 0, full["w_in"].shape[1]
    for i, piece in enumerate(pieces):
        dw_in = _weight_grad(h1, piece, "dw_in_%d" % i, dw_in, col0, ni)
        col0 += piece.shape[1]
    big["w_in"] = dw_in
    dx, dg_pre_mix, *scattered_in = _in_proj_bwd(x, dx1, pieces, small["g_pre_mix"], full["w_in"], tm_wide,
                                                scatter_in and scatter_in(dw_in))
    dx = _hand_over(dx, "hand_over_grad_x")
    small_grads = {"g_pre_mix": dg_pre_mix, "b_gate": db_gate, "g_post_mix": dg_post_mix, "g_pre_mlp": dg_pre_mlp,
                   "g_post_mlp": dg_post_mlp, "g_ple": dg_ple, "w_conv": dwconv}
    return loss[0, 0], dx, big, small_grads, scattered_in + scattered


PACK_ROWS = 16


def _pack_layout(shapes, d):
    slots, at = [], 0
    for i, (r, c) in enumerate(shapes):
        assert d % c == 0
        for row in range(r):
            slots.append((i, row, at // d, at % d))
            at += c
        at = -(-at // d) * d
    assert at <= PACK_ROWS * d
    return slots


def _pack_small(groups, d):
    shapes = [a.shape for a in groups[0]]
    slots = _pack_layout(shapes, d)
    n = len(shapes)

    def body(*refs):
        ins, outs = refs[:n * len(groups)], refs[n * len(groups):]
        for g, out in enumerate(outs):
            out[...] = jnp.zeros_like(out)
            for i, row, pr, pc in slots:
                src = ins[g * n + i]
                out[pr:pr + 1, pc:pc + shapes[i][1]] = src[row:row + 1, :]

    vmem = pl.BlockSpec(memory_space=pltpu.VMEM)
    return pl.pallas_call(
        body, name="pack_small", in_specs=[vmem] * (n * len(groups)), out_specs=[vmem] * len(groups),
        out_shape=[jax.ShapeDtypeStruct((PACK_ROWS, d), F32)] * len(groups),
    )(*[a for group in groups for a in group])


def _unpack_small(pack, shapes, d):
    slots = _pack_layout(shapes, d)
    return [jnp.stack([pack[pr, pc:pc + shapes[i][1]] for j, row, pr, pc in slots if j == i])
            for i in range(len(shapes))]


def kernel(x, p, g_pre_mix, w_in, b_gate, w_conv, w_attn_out, w_conv_out, w_o, g_post_mix, g_pre_mlp, w_up, w_down, g_post_mlp, g_ple, w_ple_gate, w_ple_proj, loss_target, m_g_pre_mix, m_w_in, m_b_gate, m_w_conv, m_w_attn_out, m_w_conv_out, m_w_o, m_g_post_mix, m_g_pre_mlp, m_w_up, m_w_down, m_g_post_mlp, m_g_ple, m_w_ple_gate, m_w_ple_proj, v_g_pre_mix, v_w_in, v_b_gate, v_w_conv, v_w_attn_out, v_w_conv_out, v_w_o, v_g_post_mix, v_g_pre_mlp, v_w_up, v_w_down, v_g_post_mlp, v_g_ple, v_w_ple_gate, v_w_ple_proj):
    given = dict(locals())
    order = ["g_pre_mix", "w_in", "b_gate", "w_conv", "w_attn_out", "w_conv_out", "w_o", "g_post_mix", "g_pre_mlp",
             "w_up", "w_down", "g_post_mlp", "g_ple", "w_ple_gate", "w_ple_proj"]
    d = x.shape[-1]
    me = 4 * lax.axis_index("x") + 2 * lax.axis_index("y") + lax.axis_index("c")

    col = [COL_SHARDED[n] for n in BIG]
    shards = _to_bf16([given[n][0] for n in BIG])
    cw_shard = w_conv.shape[-1]
    conv_tile = jnp.pad(w_conv[0], ((0, HALO - CONV_K), (0, LANES - cw_shard)))
    w_in_full, conv_g = _exchange_call(
        _join(_gather_exchange(shards[:1], col[:1]), _broadcast_exchange([conv_tile])), "gather_w_in")
    wconv = jnp.concatenate([conv_g[dev, :CONV_K, :cw_shard] for dev in range(N_DEV)], axis=1)

    small = {n: given[n] for n in SMALL}
    loss, dx, big_grads, small_grads, parts = _local_grads(
        x[0], p[0, 0], loss_target[0], small, wconv, {"w_in": w_in_full}, w_attn_out.shape[1], w_conv_out.shape[1],
        ROW_BLOCK, ATTN_BLOCK,
        _gather_exchange(shards[1:], col[1:]), lambda grads: _scatter_exchange(grads, col[1:]),
        lambda grad: _scatter_exchange([grad], col[:1]))
    small_names = list(SMALL) + ["w_conv"]
    two_d = lambda a: a.reshape(-1, d) if a.shape[-1] > d else a.reshape(-1, a.shape[-1])
    full_conv = lambda a: lax.dynamic_update_slice(jnp.zeros((CONV_K, N_DEV * cw_shard), F32), a[0],
                                                   (jnp.int32(0), me * cw_shard))
    groups = [[two_d(small_grads[n]) for n in small_names] + [loss.reshape(1, 1)]]
    for pre in ("", "m_", "v_"):
        groups.append([two_d(given[pre + n]) for n in SMALL] + [full_conv(given[pre + "w_conv"]), jnp.zeros((1, 1), F32)])
    pack, *state = _pack_small(groups, d)
    packs, = _exchange_call(_broadcast_exchange([pack]), "share_small_grads")

    grads, deltas, new_m, new_v = {}, {}, {}, {}
    for n, part in zip(BIG, parts):
        grads[n], deltas[n], new_m[n], new_v[n] = (
            a[None] for a in _sum_and_adamw(part, given[n][0], given["m_" + n][0], given["v_" + n][0], "adamw_" + n))

    outs = _sum_and_adamw(packs, *state, "adamw_small")
    shapes = [a.shape for a in groups[0]]
    for res, dst in zip(outs, (grads, deltas, new_m, new_v)):
        for n, a in zip(small_names + ["loss"], _unpack_small(res, shapes, d)):
            if n == "w_conv":
                a = lax.dynamic_slice(a, (jnp.int32(0), me * cw_shard), (CONV_K, cw_shard))[None]
            dst[n] = a.reshape(given[n].shape) if n in SMALL else a
    loss = grads["loss"][0, 0]

    return (loss, dx[None], *[grads[n] for n in order], *[deltas[n] for n in order],
            *[new_m[n] for n in order], *[new_v[n] for n in order])
```

```python
import jax
import jax.numpy as jnp
from jax import lax
from jax.experimental import pallas as pl
from jax.experimental.pallas import tpu as pltpu

F32 = jnp.float32
BF16 = jnp.bfloat16
RMS_EPS = 1e-6
N_DEV = 8
MESH_AXES = ("x", "y", "c")
LANES = 128
HEAD_DIM = 64
HEADS_PER_GROUP = LANES // HEAD_DIM
CONV_K = 3
HALO = 8
HALO_BF16 = 16
VMEM_LIMIT = 56 * 1024 * 1024
EXP2_ZERO = -150.0
LOG2_E = 1.4426950408889634

ADAM_LR = 0.001
ADAM_B1 = 0.9
ADAM_B2 = 0.999
ADAM_EPS = 1e-08
ADAM_WD = 0.01
ADAM_STEP = 10

ROW_BLOCK = 256
WIDE_BLOCKS = 2
ATTN_BLOCK = 256
ATTN_ROW_SPLITS = 2
DW_TOKENS = 2048
DW_TILE = 1024
DW_MANY_TILES = 6
DW_PIECE_TILE = 512
FF_CHUNK = 1024
PROJ_CHUNK = 512


def _dot(a, b):
    return lax.dot_general(a, b, (((1,), (0,)), ((), ())), preferred_element_type=F32)


def _dot_nt(a, b):
    return lax.dot_general(a, b, (((1,), (1,)), ((), ())), preferred_element_type=F32)


def _dot_tn(a, b):
    return lax.dot_general(a, b, (((0,), (0,)), ((), ())), preferred_element_type=F32)


def _sigmoid(z):
    return 1.0 / (1.0 + jnp.exp(-z))


def _rms_scale(x):
    return lax.rsqrt(jnp.mean(x * x, axis=-1, keepdims=True) + RMS_EPS)


def _rms_bwd(xhat, r, g, dy):
    gd = dy * g
    return r * (gd - xhat * jnp.mean(gd * xhat, axis=-1, keepdims=True))


def _params(n_axes, **kw):
    return pltpu.CompilerParams(dimension_semantics=("arbitrary",) * n_axes, vmem_limit_bytes=VMEM_LIMIT, **kw)


def _load_resident(pairs, sem):
    @pl.when(pl.program_id(0) == 0)
    def _():
        copies = [pltpu.make_async_copy(src, dst, sem.at[i]) for i, (src, dst) in enumerate(pairs)]
        for cp in copies:
            cp.start()
        for cp in copies:
            cp.wait()


def _row_spec(tm, width):
    return pl.BlockSpec((tm, width), lambda i: (i, 0))


def _prev_halo_spec(tm, width, rows):
    per = tm // rows
    return pl.BlockSpec((rows, width), lambda i: (jnp.maximum(i * per - 1, 0), 0))


def _const_spec(shape):
    return pl.BlockSpec(shape, lambda i: (0,) * len(shape))


ANY = pl.BlockSpec(memory_space=pl.ANY)


def _shift_down(cur, prev, n):
    rows = lax.broadcasted_iota(jnp.int32, cur.shape, 0)
    out = pltpu.roll(cur, n, 0)
    for j in range(n):
        out = jnp.where(rows == j, prev[prev.shape[0] - n + j:prev.shape[0] - n + j + 1, :], out)
    return out


def _shift_up(cur, nxt, n):
    tm = cur.shape[0]
    rows = lax.broadcasted_iota(jnp.int32, cur.shape, 0)
    out = pltpu.roll(cur, tm - n, 0)
    for j in range(n):
        out = jnp.where(rows == tm - n + j, nxt[j:j + 1, :], out)
    return out


def _conv_taps(cm, cm_prev, wconv):
    cm1 = _shift_down(cm, cm_prev, 1)
    cm2 = _shift_down(cm, cm_prev, 2)
    cv = wconv[2:3, :] * cm + wconv[1:2, :] * cm1 + wconv[0:1, :] * cm2
    return cv, cm1, cm2


def _pre_norm(x, g1, tm, exchange):
    s, d = x.shape
    grid = (s // tm,)
    ex = exchange

    def body(x_ref, g_ref, *rest):
        ex_in, (h_ref,), ex_out, sems = _split_refs(rest, ex, 1)
        _exchange_start(ex, ex_in, ex_out, sems, grid)
        xv = x_ref[...]
        h_ref[...] = (xv * _rms_scale(xv) * g_ref[...]).astype(BF16)
        _exchange_wait(ex, ex_in, ex_out, sems, grid)

    return pl.pallas_call(
        body, name="pre_norm", grid=grid,
        in_specs=[_row_spec(tm, d), _const_spec((1, d))] + [ANY] * len(ex.arrays),
        out_specs=[_row_spec(tm, d)] + [ANY] * len(ex.out_shapes),
        out_shape=[jax.ShapeDtypeStruct((s, d), BF16)] + ex.out_shapes,
        scratch_shapes=_exchange_sems(ex),
        compiler_params=_params(1),
    )(x, g1, *ex.arrays)


def _in_proj_fwd(h1, b_gate, w_in, aw, cw, tm):
    s, d = h1.shape
    ni = w_in.shape[1]
    n_qkv, n_conv = 3 * aw, 3 * cw
    ch = PROJ_CHUNK

    def body(h_ref, b_ref, w_hbm, qkv_ref, conv_ref, gate_ref, w_vmem, sem):
        _load_resident([(w_hbm, w_vmem)], sem)
        h = h_ref[...]
        for c0 in range(0, ni, ch):
            pc = _dot(h, w_vmem[:, c0:c0 + ch])
            if c0 < n_qkv:
                qkv_ref[:, c0:c0 + ch] = pc.astype(BF16)
            elif c0 < n_qkv + n_conv:
                conv_ref[:, c0 - n_qkv:c0 - n_qkv + ch] = pc.astype(BF16)
            else:
                g0 = c0 - n_qkv - n_conv
                gate_ref[:, g0:g0 + ch] = _sigmoid(pc + b_ref[:, g0:g0 + ch]).astype(BF16)

    return pl.pallas_call(
        body, name="in_proj_fwd", grid=(s // tm,),
        in_specs=[_row_spec(tm, d), _const_spec((1, 2 * d)), ANY],
        out_specs=[_row_spec(tm, n_qkv), _row_spec(tm, n_conv), _row_spec(tm, 2 * d)],
        out_shape=[jax.ShapeDtypeStruct((s, n_qkv), BF16), jax.ShapeDtypeStruct((s, n_conv), BF16),
                   jax.ShapeDtypeStruct((s, 2 * d), BF16)],
        scratch_shapes=[pltpu.VMEM((d, ni), BF16), pltpu.SemaphoreType.DMA((1,))],
        compiler_params=_params(1),
    )(h1, b_gate, w_in)


def _split_hi_lo(a):
    hi = a.astype(BF16)
    return hi, (a - hi.astype(F32)).astype(BF16)


def _log2_gates(z):
    z2 = z * LOG2_E
    nz2 = -z2
    log_keep = jnp.minimum(nz2, 0.0) - jnp.log2(1.0 + jnp.exp2(jnp.minimum(z2, nz2)))
    return log_keep + z2, log_keep


def _attn_masks(t):
    row = lax.broadcasted_iota(jnp.int32, (t, t), 0)
    col = lax.broadcasted_iota(jnp.int32, (t, t), 1)
    return (col < row).astype(F32), (row > col).astype(BF16), (row >= col).astype(BF16)


def _chains(a):
    tr = a.shape[0] // ATTN_ROW_SPLITS
    return [jnp.where(_head_lanes(h), a[r * tr:(r + 1) * tr], jnp.zeros((tr, LANES), a.dtype))
            for h in range(HEADS_PER_GROUP) for r in range(ATTN_ROW_SPLITS)]


def _merge_chains(parts):
    rows = []
    for r in range(ATTN_ROW_SPLITS):
        out = parts[r]
        for h in range(1, HEADS_PER_GROUP):
            out = jnp.where(_head_lanes(h), parts[h * ATTN_ROW_SPLITS + r], out)
        rows.append(out)
    return jnp.concatenate(rows, axis=0)


def _by_stage(n_chains, stages):
    for stage in stages:
        for c in range(n_chains):
            stage(c)


def _row_parts(a):
    tr = a.shape[0] // ATTN_ROW_SPLITS
    return [a[r * tr:(r + 1) * tr] for r in range(ATTN_ROW_SPLITS)]


def _while_weights_live(qi, block, carry):
    def cond(state):
        j, carry = state
        live = jnp.max(carry[0][0])
        for run in carry[0][1:]:
            live = jnp.maximum(live, jnp.max(run))
        return jnp.logical_and(j < qi, live >= EXP2_ZERO)

    def step(state):
        j, carry = state
        return j + 1, block(qi - 1 - j, carry)

    return lax.while_loop(cond, step, (jnp.int32(0), carry))[1]


def _head_lanes(h):
    lane = lax.broadcasted_iota(jnp.int32, (1, LANES), 1)
    return (lane >= HEAD_DIM * h) & (lane < HEAD_DIM * (h + 1))


def _attn_fwd(qkv, aw, t, exchange=None):
    s = qkv.shape[0]
    groups = aw // LANES
    nq = s // t
    scale = HEAD_DIM ** -0.5
    ex = exchange or _NO_EXCHANGE
    causal, upper, _ = _attn_masks(t)
    mask_spec = pl.BlockSpec((t, t), lambda g, i: (0, 0))

    def body(q_ref, k_ref, v_ref, causal_ref, upper_ref, *rest):
        ex_in, (o_ref, ob_ref), ex_out, sems = _split_refs(rest, ex, 2)
        qi = pl.program_id(1)
        _exchange_start(ex, ex_in, ex_out, sems, (groups, nq))
        upper = upper_ref[...]
        causal = _row_parts(causal_ref[...] > 0.5) * HEADS_PER_GROUP
        qs = _chains(q_ref[...] * scale)
        heads = range(len(qs))
        tr = t // ATTN_ROW_SPLITS

        def block(kb, runs, accs, diag):
            rows = pl.ds(pl.multiple_of(kb * t, t), t)
            k = k_ref[rows, :]
            v = v_ref[rows, :]
            ncs = [(h % ATTN_ROW_SPLITS + 1) * tr if diag else t for h in heads]
            live = [{} for _ in heads]
            new_runs, new_accs = [None] * len(heads), [None] * len(heads)

            def scores(h):
                live[h]["z"] = _dot_nt(qs[h], k[0:ncs[h]])

            def gates(h):
                nc = ncs[h]
                log_b, log_keep = _log2_gates(live[h].pop("z"))
                if diag:
                    log_keep = jnp.where(causal[h][:, 0:nc], log_keep, 0.0)
                hi, lo = _split_hi_lo(log_keep)
                live[h]["log_w"] = log_b + runs[h]
                live[h]["between"] = _dot(hi, upper[0:nc, 0:nc]) + _dot(lo, upper[0:nc, 0:nc])
                new_runs[h] = runs[h] + jnp.sum(log_keep, axis=1, keepdims=True)

            def weights(h):
                nc = ncs[h]
                w = jnp.exp2(live[h].pop("log_w") + live[h].pop("between"))
                if diag:
                    w = jnp.where(causal[h][:, 0:nc], w, 0.0)
                new_accs[h] = accs[h] + _dot(w.astype(BF16), v[0:nc])

            _by_stage(len(heads), [scores, gates, weights])
            return tuple(new_runs), tuple(new_accs)

        carry = block(qi, [jnp.zeros((tr, 1), F32)] * len(heads), [jnp.zeros((tr, LANES), F32)] * len(heads), True)
        _, accs = _while_weights_live(qi, lambda kb, carry: block(kb, *carry, False), carry)
        o = _merge_chains(accs)
        o_ref[...] = o
        ob_ref[...] = o.astype(BF16)
        _exchange_wait(ex, ex_in, ex_out, sems, (groups, nq))

    return pl.pallas_call(
        body, name="attn_fwd", grid=(groups, nq),
        in_specs=[pl.BlockSpec((t, LANES), lambda g, i: (i, g)),
                  pl.BlockSpec((s, LANES), lambda g, i: (0, groups + g)),
                  pl.BlockSpec((s, LANES), lambda g, i: (0, 2 * groups + g)), mask_spec, mask_spec]
        + [ANY] * len(ex.arrays),
        out_specs=[pl.BlockSpec((t, LANES), lambda g, i: (i, g))] * 2 + [ANY] * len(ex.out_shapes),
        out_shape=[jax.ShapeDtypeStruct((s, aw), F32), jax.ShapeDtypeStruct((s, aw), BF16)] + ex.out_shapes,
        scratch_shapes=_exchange_sems(ex),
        compiler_params=_params(2),
    )(qkv, qkv, qkv, causal, upper, *ex.arrays)


def _attn_bwd(qkv, o, do, aw, t, exchange=None):
    s = qkv.shape[0]
    groups = aw // LANES
    nq = s // t
    scale = HEAD_DIM ** -0.5
    ex = exchange or _NO_EXCHANGE

    def body(q_ref, k_ref, v_ref, o_ref, do_ref, causal_ref, upper_ref, lower_ref, *rest):
        ex_in, (dq_ref, dk_ref, dv_ref), ex_out, (dk_acc, dv_acc, *sems) = _split_refs(rest, ex, 3)
        qi = pl.program_id(1)
        _exchange_start(ex, ex_in, ex_out, sems, (groups, nq))

        @pl.when(qi == 0)
        def _():
            dk_acc[...] = jnp.zeros_like(dk_acc)
            dv_acc[...] = jnp.zeros_like(dv_acc)

        upper = upper_ref[...]
        lower_incl = lower_ref[...]
        causal = _row_parts(causal_ref[...] > 0.5) * HEADS_PER_GROUP
        q = q_ref[...] * scale
        do_b = do_ref[...]
        qs = _chains(q)
        dos = _chains(do_b)
        qs_all = jnp.concatenate(qs, axis=0)
        dos_all = jnp.concatenate(dos, axis=0)
        e_totals = [jnp.sum(part, axis=1, keepdims=True) for part in _chains(do_b.astype(F32) * o_ref[...])]
        heads = range(len(qs))
        tr = t // ATTN_ROW_SPLITS

        def block(kb, runs, e_runs, dqs, diag):
            rows = pl.ds(pl.multiple_of(kb * t, t), t)
            k = k_ref[rows, :]
            v = v_ref[rows, :]
            ncs = [(h % ATTN_ROW_SPLITS + 1) * tr if diag else t for h in heads]
            live = [{} for _ in heads]
            none = [None] * len(heads)
            new_runs, new_e_runs, new_dqs, dzbs, wbs = list(none), list(none), list(none), list(none), list(none)

            def scores(h):
                live[h]["z"] = _dot_nt(qs[h], k[0:ncs[h]])
                live[h]["dw"] = _dot_nt(dos[h], v[0:ncs[h]])

            def gates(h):
                nc = ncs[h]
                log_b, log_keep = _log2_gates(live[h].pop("z"))
                live[h]["beta"] = jnp.exp2(log_b)
                live[h]["keep"] = jnp.exp2(log_keep)
                if diag:
                    log_keep = jnp.where(causal[h][:, 0:nc], log_keep, 0.0)
                hi, lo = _split_hi_lo(log_keep)
                live[h]["log_w"] = log_b + runs[h]
                live[h]["between"] = _dot(hi, upper[0:nc, 0:nc]) + _dot(lo, upper[0:nc, 0:nc])
                new_runs[h] = runs[h] + jnp.sum(log_keep, axis=1, keepdims=True)

            def weights(h):
                nc = ncs[h]
                w = jnp.exp2(live[h].pop("log_w") + live[h].pop("between"))
                if diag:
                    w = jnp.where(causal[h][:, 0:nc], w, 0.0)
                wb = w.astype(BF16)
                e = live[h].pop("dw") * wb.astype(F32)
                hi, lo = _split_hi_lo(e)
                live[h]["e"] = e
                live[h]["e_suffix"] = _dot(hi, lower_incl[0:nc, 0:nc]) + _dot(lo, lower_incl[0:nc, 0:nc]) + e_runs[h]
                wbs[h] = wb

            def score_grads(h):
                nc = ncs[h]
                e_suffix = live[h].pop("e_suffix")
                dz = live[h].pop("e") * live[h].pop("keep") - (e_totals[h] - e_suffix) * live[h].pop("beta")
                if diag:
                    dz = jnp.where(causal[h][:, 0:nc], dz, 0.0)
                dzb = dz.astype(BF16)
                new_dqs[h] = dqs[h] + _dot(dzb, k[0:nc])
                new_e_runs[h] = e_suffix[:, 0:1]
                if nc < t:
                    unseen = jnp.zeros((tr, t - nc), BF16)
                    dzb = jnp.concatenate([dzb, unseen], axis=1)
                    wbs[h] = jnp.concatenate([wbs[h], unseen], axis=1)
                dzbs[h] = dzb

            _by_stage(len(heads), [scores, gates, weights, score_grads])
            dk_acc[rows, :] += _dot_tn(jnp.concatenate(dzbs, axis=0), qs_all)
            dv_acc[rows, :] += _dot_tn(jnp.concatenate(wbs, axis=0), dos_all)
            return tuple(new_runs), tuple(new_e_runs), tuple(new_dqs)

        zero_cols = [jnp.zeros((tr, 1), F32)] * len(heads)
        carry = block(qi, zero_cols, zero_cols, [jnp.zeros((tr, LANES), F32)] * len(heads), True)
        _, _, dqs = _while_weights_live(qi, lambda kb, carry: block(kb, *carry, False), carry)
        dq_ref[...] = (_merge_chains(dqs) * scale).astype(BF16)

        @pl.when(qi == nq - 1)
        def _():
            dk_ref[...] = dk_acc[...].astype(BF16)
            dv_ref[...] = dv_acc[...].astype(BF16)

        _exchange_wait(ex, ex_in, ex_out, sems, (groups, nq))

    blk = pl.BlockSpec((t, LANES), lambda g, i: (i, g))
    slab = pl.BlockSpec((s, LANES), lambda g, i: (0, g))
    mask_spec = pl.BlockSpec((t, t), lambda g, i: (0, 0))
    return pl.pallas_call(
        body, name="attn_bwd", grid=(groups, nq),
        in_specs=[blk, pl.BlockSpec((s, LANES), lambda g, i: (0, groups + g)),
                  pl.BlockSpec((s, LANES), lambda g, i: (0, 2 * groups + g)), blk, blk, mask_spec, mask_spec, mask_spec]
        + [ANY] * len(ex.arrays),
        out_specs=[blk, slab, slab] + [ANY] * len(ex.out_shapes),
        out_shape=[jax.ShapeDtypeStruct((s, aw), BF16)] * 3 + ex.out_shapes,
        scratch_shapes=[pltpu.VMEM((s, LANES), F32), pltpu.VMEM((s, LANES), F32)] + _exchange_sems(ex),
        compiler_params=_params(2),
    )(qkv, qkv, qkv, o, do, *_attn_masks(t), *ex.arrays)


def _branches(o_b, conv, conv_prev, wconv, w_ao, w_co, cw, first):
    conv = conv.astype(F32)
    conv_prev = conv_prev.astype(F32)
    cb = conv[:, 0:cw]
    cm = conv[:, cw:2 * cw] * conv[:, 2 * cw:3 * cw]
    cm_prev = conv_prev[:, cw:2 * cw] * conv_prev[:, 2 * cw:3 * cw]
    cm_prev = jnp.where(first, 0.0, cm_prev)
    cv, cm1, cm2 = _conv_taps(cm, cm_prev, wconv)
    conv_in = (cb * cv).astype(BF16)
    return _dot(o_b, w_ao), _dot(conv_in, w_co), conv_in, cb, cv, cm, cm1, cm2


def _mix_fwd(x, o, conv, gate, wconv, g_post, w_ao, w_co, w_o, tm):
    s, d = x.shape
    aw, cw = w_ao.shape[0], w_co.shape[0]

    def body(x_ref, o_ref, conv_ref, prev_ref, gate_ref, wc_ref, g_ref, wao_hbm, wco_hbm, wo_hbm,
             x1_ref, mixed_ref, mixin_ref, convin_ref, wao, wco, wo, sem):
        _load_resident([(wao_hbm, wao), (wco_hbm, wco), (wo_hbm, wo)], sem)
        y_attn, y_conv, conv_in, *_ = _branches(
            o_ref[...].astype(BF16), conv_ref[...], prev_ref[...], wc_ref[...], wao[...], wco[...], cw,
            pl.program_id(0) == 0)
        mix_in = (gate_ref[:, 0:d].astype(F32) * y_attn + gate_ref[:, d:2 * d].astype(F32) * y_conv).astype(BF16)
        mixed = _dot(mix_in, wo[...])
        x1_ref[...] = x_ref[...] + mixed * _rms_scale(mixed) * g_ref[...]
        mixed_ref[...] = mixed
        mixin_ref[...] = mix_in
        convin_ref[...] = conv_in

    return pl.pallas_call(
        body, name="mix_fwd", grid=(s // tm,),
        in_specs=[_row_spec(tm, d), _row_spec(tm, aw), _row_spec(tm, 3 * cw), _prev_halo_spec(tm, 3 * cw, HALO_BF16),
                  _row_spec(tm, 2 * d), _const_spec((CONV_K, cw)), _const_spec((1, d)), ANY, ANY, ANY],
        out_specs=[_row_spec(tm, d), _row_spec(tm, d), _row_spec(tm, d), _row_spec(tm, cw)],
        out_shape=[jax.ShapeDtypeStruct((s, d), F32), jax.ShapeDtypeStruct((s, d), F32),
                   jax.ShapeDtypeStruct((s, d), BF16), jax.ShapeDtypeStruct((s, cw), BF16)],
        scratch_shapes=[pltpu.VMEM(w_ao.shape, BF16), pltpu.VMEM(w_co.shape, BF16), pltpu.VMEM(w_o.shape, BF16),
                        pltpu.SemaphoreType.DMA((3,))],
        compiler_params=_params(1),
    )(x, o, conv, conv, gate, wconv, g_post, w_ao, w_co, w_o)


def _mix_bwd(dx1, mixed, o, conv, gate, wconv, g_post, w_ao, w_co, w_o, tm):
    s, d = dx1.shape
    aw, cw = w_ao.shape[0], w_co.shape[0]
    n = s // tm
    per = tm // HALO_BF16

    def body(dx1_ref, mixed_ref, o_ref, conv_ref, prev_ref, gate_ref, wc_ref, g_ref, wao_hbm, wco_hbm, wo_hbm,
             dmixed_ref, dattn_ref, dconvout_ref, do_ref, drest_ref, dg_ref, dbias_ref, dwc_ref,
             wao, wco, wo, dcv_next, sem):
        i = pl.program_id(0)
        _load_resident([(wao_hbm, wao), (wco_hbm, wco), (wo_hbm, wo)], sem)

        @pl.when(i == 0)
        def _():
            dg_ref[...] = jnp.zeros_like(dg_ref)
            dbias_ref[...] = jnp.zeros_like(dbias_ref)
            dwc_ref[...] = jnp.zeros_like(dwc_ref)
            dcv_next[...] = jnp.zeros_like(dcv_next)

        mixed = mixed_ref[...]
        r = _rms_scale(mixed)
        mhat = mixed * r
        dn = dx1_ref[...]
        dg_ref[...] += jnp.sum(dn * mhat, axis=0, keepdims=True)
        dmixed = _rms_bwd(mhat, r, g_ref[...], dn).astype(BF16)
        dmixed_ref[...] = dmixed
        dmi = _dot_nt(dmixed, wo[...])

        wc = wc_ref[...]
        conv = conv_ref[...].astype(F32)
        y_attn, y_conv, _, cb, cv, cm, cm1, cm2 = _branches(
            o_ref[...].astype(BF16), conv, prev_ref[...], wc, wao[...], wco[...], cw, i == n - 1)
        ga = gate_ref[:, 0:d].astype(F32)
        gc = gate_ref[:, d:2 * d].astype(F32)
        dpre_a = dmi * y_attn * ga * (1.0 - ga)
        dpre_c = dmi * y_conv * gc * (1.0 - gc)
        drest_ref[:, 3 * cw:3 * cw + d] = dpre_a.astype(BF16)
        drest_ref[:, 3 * cw + d:3 * cw + 2 * d] = dpre_c.astype(BF16)
        dbias_ref[:, 0:d] += jnp.sum(dpre_a, axis=0, keepdims=True)
        dbias_ref[:, d:2 * d] += jnp.sum(dpre_c, axis=0, keepdims=True)

        dattn = (dmi * ga).astype(BF16)
        dattn_ref[...] = dattn
        do_ref[...] = _dot_nt(dattn, wao[...]).astype(BF16)
        dconvout = (dmi * gc).astype(BF16)
        dconvout_ref[...] = dconvout
        dconv_in = _dot_nt(dconvout, wco[...])
        drest_ref[:, 0:cw] = (dconv_in * cv).astype(BF16)

        dcv = dconv_in * cb
        following = dcv_next[...]
        dcm = wc[2:3, :] * dcv + wc[1:2, :] * _shift_up(dcv, following, 1) + wc[0:1, :] * _shift_up(dcv, following, 2)
        drest_ref[:, cw:2 * cw] = (dcm * conv[:, 2 * cw:3 * cw]).astype(BF16)
        drest_ref[:, 2 * cw:3 * cw] = (dcm * conv[:, cw:2 * cw]).astype(BF16)
        for tap, shifted in enumerate((cm2, cm1, cm)):
            dwc_ref[tap:tap + 1, :] += jnp.sum(dcv * shifted, axis=0, keepdims=True)
        dcv_next[...] = dcv[0:HALO, :]

    def rows(width):
        return pl.BlockSpec((tm, width), lambda i: (n - 1 - i, 0))

    prev_halo = pl.BlockSpec((HALO_BF16, 3 * cw), lambda i: (jnp.maximum((n - 1 - i) * per - 1, 0), 0))
    n_rest = 3 * cw + 2 * d
    return pl.pallas_call(
        body, name="mix_bwd", grid=(n,),
        in_specs=[rows(d), rows(d), rows(aw), rows(3 * cw), prev_halo, rows(2 * d), _const_spec((CONV_K, cw)),
                  _const_spec((1, d)), ANY, ANY, ANY],
        out_specs=[rows(d), rows(d), rows(d), rows(aw), rows(n_rest), _const_spec((1, d)), _const_spec((1, 2 * d)),
                   _const_spec((CONV_K, cw))],
        out_shape=[jax.ShapeDtypeStruct((s, d), BF16), jax.ShapeDtypeStruct((s, d), BF16),
                   jax.ShapeDtypeStruct((s, d), BF16), jax.ShapeDtypeStruct((s, aw), BF16),
                   jax.ShapeDtypeStruct((s, n_rest), BF16), jax.ShapeDtypeStruct((1, d), F32),
                   jax.ShapeDtypeStruct((1, 2 * d), F32), jax.ShapeDtypeStruct((CONV_K, cw), F32)],
        scratch_shapes=[pltpu.VMEM(w_ao.shape, BF16), pltpu.VMEM(w_co.shape, BF16), pltpu.VMEM(w_o.shape, BF16),
                        pltpu.VMEM((HALO, cw), F32), pltpu.SemaphoreType.DMA((3,))],
        compiler_params=_params(1),
    )(dx1, mixed, o, conv, conv, gate, wconv, g_post, w_ao, w_co, w_o)


def _mlp_ple_loss(x1, p, target, g_pre, g_post, g_ple, w_up, w_dn, w_pg, w_pp, tm):
    s, d = x1.shape
    ff = w_up.shape[1]
    pd = p.shape[1]
    fc = FF_CHUNK

    def body(x1_ref, p_ref, t_ref, gpre_ref, gpost_ref, gple_ref, wup_hbm, wdn_hbm, wpg_hbm, wpp_hbm,
             dx1_ref, h2_ref, du_ref, a_ref, df_ref, h3_ref, ds3_ref, dpp_ref, loss_ref, dgpre_ref, dgpost_ref,
             dgple_ref, wup, wdn, wpg, wpp, u_scr, sem):
        _load_resident([(wup_hbm, wup), (wdn_hbm, wdn), (wpg_hbm, wpg), (wpp_hbm, wpp)], sem)

        @pl.when(pl.program_id(0) == 0)
        def _():
            for ref in (loss_ref, dgpre_ref, dgpost_ref, dgple_ref):
                ref[...] = jnp.zeros_like(ref)

        x1v = x1_ref[...]
        r2 = _rms_scale(x1v)
        x1hat = x1v * r2
        h2 = (x1hat * gpre_ref[...]).astype(BF16)
        h2_ref[...] = h2
        f = jnp.zeros((tm, d), F32)
        for c0 in range(0, ff, fc):
            u = _dot(h2, wup[:, c0:c0 + fc])
            u_scr[:, c0:c0 + fc] = u
            a = jnp.square(jnp.maximum(u, 0.0)).astype(BF16)
            a_ref[:, c0:c0 + fc] = a
            f = f + _dot(a, wdn[c0:c0 + fc, :])
        rf = _rms_scale(f)
        fhat = f * rf
        x2 = x1v + fhat * gpost_ref[...]
        r3 = _rms_scale(x2)
        x2hat = x2 * r3
        h3 = (x2hat * gple_ref[...]).astype(BF16)
        h3_ref[...] = h3
        pg = _sigmoid(_dot(h3, wpg[...]))
        pp = _dot(p_ref[...].astype(BF16), wpp[...])
        diff = x2 + pg * pp - t_ref[...]
        loss_ref[...] += 0.5 * jnp.sum(jnp.mean(diff * diff, axis=-1, keepdims=True), axis=0, keepdims=True)

        dy = diff * (1.0 / d)
        dpp_ref[...] = (dy * pg).astype(BF16)
        ds3 = (dy * pp * pg * (1.0 - pg)).astype(BF16)
        ds3_ref[...] = ds3
        dh3 = _dot_nt(ds3, wpg[...])
        dgple_ref[...] += jnp.sum(dh3 * x2hat, axis=0, keepdims=True)
        dx2 = dy + _rms_bwd(x2hat, r3, gple_ref[...], dh3)
        dgpost_ref[...] += jnp.sum(dx2 * fhat, axis=0, keepdims=True)
        df = _rms_bwd(fhat, rf, gpost_ref[...], dx2).astype(BF16)
        df_ref[...] = df
        dh2 = jnp.zeros((tm, d), F32)
        for c0 in range(0, ff, fc):
            da = _dot_nt(df, wdn[c0:c0 + fc, :])
            du = (da * (2.0 * jnp.maximum(u_scr[:, c0:c0 + fc], 0.0))).astype(BF16)
            du_ref[:, c0:c0 + fc] = du
            dh2 = dh2 + _dot_nt(du, wup[:, c0:c0 + fc])
        dgpre_ref[...] += jnp.sum(dh2 * x1hat, axis=0, keepdims=True)
        dx1_ref[...] = dx2 + _rms_bwd(x1hat, r2, gpre_ref[...], dh2)

    vec = _const_spec((1, d))
    return pl.pallas_call(
        body, name="mlp_ple_loss", grid=(s // tm,),
        in_specs=[_row_spec(tm, d), _row_spec(tm, pd), _row_spec(tm, d), vec, vec, vec, ANY, ANY, ANY, ANY],
        out_specs=[_row_spec(tm, d), _row_spec(tm, d), _row_spec(tm, ff), _row_spec(tm, ff), _row_spec(tm, d),
                   _row_spec(tm, d), _row_spec(tm, d), _row_spec(tm, d), _const_spec((1, 1)), vec, vec, vec],
        out_shape=[jax.ShapeDtypeStruct((s, d), F32), jax.ShapeDtypeStruct((s, d), BF16),
                   jax.ShapeDtypeStruct((s, ff), BF16), jax.ShapeDtypeStruct((s, ff), BF16),
                   jax.ShapeDtypeStruct((s, d), BF16), jax.ShapeDtypeStruct((s, d), BF16),
                   jax.ShapeDtypeStruct((s, d), BF16), jax.ShapeDtypeStruct((s, d), BF16),
                   jax.ShapeDtypeStruct((1, 1), F32), jax.ShapeDtypeStruct((1, d), F32),
                   jax.ShapeDtypeStruct((1, d), F32), jax.ShapeDtypeStruct((1, d), F32)],
        scratch_shapes=[pltpu.VMEM(w_up.shape, BF16), pltpu.VMEM(w_dn.shape, BF16), pltpu.VMEM(w_pg.shape, BF16),
                        pltpu.VMEM(w_pp.shape, BF16), pltpu.VMEM((tm, ff), F32), pltpu.SemaphoreType.DMA((4,))],
        compiler_params=_params(1),
    )(x1, p, target, g_pre, g_post, g_ple, w_up, w_dn, w_pg, w_pp)


def _in_proj_bwd(x, dx1, pieces, g1, w_in, tm, exchange=None):
    s, d = x.shape
    ni = w_in.shape[1]
    widths = [p.shape[1] for p in pieces]
    grid = (s // tm,)
    ex = exchange or _NO_EXCHANGE

    def body(x_ref, dx1_ref, *rest):
        piece_refs, rest = rest[:len(pieces)], rest[len(pieces):]
        g_ref, w_hbm = rest[0], rest[1]
        ex_in, (dx_ref, dg_ref), ex_out, (w_vmem, sem, *sems) = _split_refs(rest[2:], ex, 2)
        _exchange_start(ex, ex_in, ex_out, sems, grid)
        _load_resident([(w_hbm, w_vmem)], sem)

        @pl.when(pl.program_id(0) == 0)
        def _():
            dg_ref[...] = jnp.zeros_like(dg_ref)

        dh = jnp.zeros((tm, d), F32)
        c0 = 0
        for ref, width in zip(piece_refs, widths):
            dh = dh + _dot_nt(ref[...], w_vmem[:, c0:c0 + width])
            c0 += width
        xv = x_ref[...]
        r = _rms_scale(xv)
        xhat = xv * r
        dg_ref[...] += jnp.sum(dh * xhat, axis=0, keepdims=True)
        dx_ref[...] = dx1_ref[...] + _rms_bwd(xhat, r, g_ref[...], dh)
        _exchange_wait(ex, ex_in, ex_out, sems, grid)

    return pl.pallas_call(
        body, name="in_proj_bwd", grid=grid,
        in_specs=[_row_spec(tm, d), _row_spec(tm, d)] + [_row_spec(tm, w) for w in widths]
        + [_const_spec((1, d)), ANY] + [ANY] * len(ex.arrays),
        out_specs=[_row_spec(tm, d), _const_spec((1, d))] + [ANY] * len(ex.out_shapes),
        out_shape=[jax.ShapeDtypeStruct((s, d), F32), jax.ShapeDtypeStruct((1, d), F32)] + ex.out_shapes,
        scratch_shapes=[pltpu.VMEM((d, ni), BF16), pltpu.SemaphoreType.DMA((1,))] + _exchange_sems(ex),
        compiler_params=_params(1),
    )(x, dx1, *pieces, g1, w_in, *ex.arrays)


def _weight_grad(a, b, name, into=None, col0=0, n_total=None):
    s, m = a.shape
    n = b.shape[1]
    tm = min(m, DW_TILE)
    tn = min(n, DW_TILE) if n_total is None else DW_PIECE_TILE
    tk = min(s, DW_TOKENS * (2 if (m // tm) * (n // tn) >= DW_MANY_TILES else 1))
    nk = s // tk
    j0 = col0 // tn
    assert m % tm == 0 and n % tn == 0 and col0 % tn == 0

    def body(a_ref, b_ref, *rest):
        o_ref, acc = rest[-2:]
        k = pl.program_id(2)

        @pl.when(k == 0)
        def _():
            acc[...] = jnp.zeros_like(acc)

        acc[...] += _dot_tn(a_ref[...].astype(BF16), b_ref[...].astype(BF16))

        @pl.when(k == nk - 1)
        def _():
            o_ref[...] = acc[...].astype(BF16)

    extra = [] if into is None else [into]
    return pl.pallas_call(
        body, name=name, grid=(m // tm, n // tn, nk),
        in_specs=[pl.BlockSpec((tk, tm), lambda i, j, k: (k, i)), pl.BlockSpec((tk, tn), lambda i, j, k: (k, j))]
        + [ANY] * len(extra),
        out_specs=pl.BlockSpec((tm, tn), lambda i, j, k: (i, j0 + j)),
        out_shape=jax.ShapeDtypeStruct((m, n_total or n), BF16),
        input_output_aliases={2: 0} if extra else {},
        scratch_shapes=[pltpu.VMEM((tm, tn), F32)],
        compiler_params=_params(3),
    )(a, b, *extra)


def _mesh_position():
    return tuple(lax.axis_index(a) for a in MESH_AXES)


def _peer(me, k):
    bits = ((k >> 2) & 1, (k >> 1) & 1, k & 1)
    pos = tuple(1 - m if b else m for m, b in zip(me, bits))
    return pos, 4 * pos[0] + 2 * pos[1] + pos[2]


class _Exchange:
    def __init__(self, arrays, out_shapes, src, dst, relayed=None):
        self.arrays, self.out_shapes, self.src, self.dst = list(arrays), list(out_shapes), src, dst
        self.relayed = list(relayed) if relayed is not None else [False] * len(self.arrays)


_NO_EXCHANGE = _Exchange([], [], None, None)


def _exchange_sems(ex):
    n = len(ex.arrays)
    if n == 0:
        return []
    return [pltpu.SemaphoreType.DMA((n, N_DEV - 1)), pltpu.SemaphoreType.DMA((n, N_DEV - 1)),
            pltpu.SemaphoreType.DMA((n,))]


def _split_refs(rest, ex, n_own_outs):
    n_in, n_out = len(ex.arrays), len(ex.out_shapes)
    ex_in, rest = rest[:n_in], rest[n_in:]
    own, rest = rest[:n_own_outs], rest[n_own_outs:]
    return ex_in, own, rest[:n_out], rest[n_out:]


def _direct_steps(ex, w, in_refs, out_refs, sems):
    send_sems, recv_sems, local_sems = sems
    me = _mesh_position()
    mine = 4 * me[0] + 2 * me[1] + me[2]

    def copy(k):
        landing = ex.dst(w, out_refs, mine)
        if k == 0:
            return pltpu.make_async_copy(ex.src(w, in_refs, mine), landing, local_sems.at[w])
        peer, peer_idx = _peer(me, k)
        return pltpu.make_async_remote_copy(
            src_ref=ex.src(w, in_refs, peer_idx), dst_ref=landing, send_sem=send_sems.at[w, k - 1],
            recv_sem=recv_sems.at[w, k - 1], device_id=peer, device_id_type=pl.DeviceIdType.MESH)

    ks = range(N_DEV)
    return [lambda k=k: copy(k).start() for k in ks], [], [lambda k=k: copy(k).wait() for k in ks]


def _relayed_steps(ex, w, in_refs, out_refs, sems):
    send_sems, recv_sems, local_sems = sems
    x, y, c = _mesh_position()
    chips = [(1 - x, y), (x, 1 - y), (1 - x, 1 - y)]
    sibling = (x, y, 1 - c)
    js = range(len(chips))

    def block(px, py, pc):
        return ex.dst(w, out_refs, 4 * px + 2 * py + pc)

    def copy(k, dst, to, src=None):
        return pltpu.make_async_remote_copy(
            src_ref=ex.src(w, in_refs, None) if src is None else src, dst_ref=dst, send_sem=send_sems.at[w, k],
            recv_sem=recv_sems.at[w, k], device_id=to, device_id_type=pl.DeviceIdType.MESH)

    def local():
        return pltpu.make_async_copy(ex.src(w, in_refs, None), block(x, y, c), local_sems.at[w])

    def own(k):
        return copy(k, block(x, y, c), sibling if k == 0 else (*chips[k - 1], c))

    def came(j):
        return copy(1 + j, block(*chips[j], c), (*chips[j], c))

    def passed(j):
        return copy(4 + j, block(*chips[j], c), sibling, src=block(*chips[j], c))

    def from_sibling(k):
        return copy(k, block(x, y, 1 - c) if k == 0 else block(*chips[k - 4], 1 - c), sibling)

    start = [lambda: local().start()] + [lambda k=k: own(k).start() for k in range(4)]
    relay = [step for j in js for step in (lambda j=j: came(j).wait_recv(), lambda j=j: passed(j).start())]
    finish = ([lambda: local().wait()] + [lambda k=k: own(k).wait_send() for k in range(4)]
              + [lambda j=j: passed(j).wait_send() for j in js]
              + [lambda k=k: from_sibling(k).wait_recv() for k in (0, 4, 5, 6)])
    return start, relay, finish


def _exchange_steps(ex, in_refs, out_refs, sems):
    start, relay, finish = [], [], []
    for w in range(len(ex.arrays)):
        steps = (_relayed_steps if ex.relayed[w] else _direct_steps)(ex, w, in_refs, out_refs, sems)
        start += steps[0]
        relay += steps[1]
        finish += steps[2]
    return start, relay, finish


def _run(steps):
    for step in steps:
        step()


def _at_grid_step(grid, where):
    target = {"first": [0] * len(grid), "middle": [grid[0] // 2] + [0] * (len(grid) - 1),
              "last": [g - 1 for g in grid]}[where]
    hit = pl.program_id(0) == target[0]
    for axis in range(1, len(grid)):
        hit = jnp.logical_and(hit, pl.program_id(axis) == target[axis])
    return hit


def _exchange_start(ex, in_refs, out_refs, sems, grid):
    if ex.arrays:
        @pl.when(_at_grid_step(grid, "first"))
        def _():
            _run(_exchange_steps(ex, in_refs, out_refs, sems)[0])

        if any(ex.relayed):
            assert grid[0] >= 2

            @pl.when(_at_grid_step(grid, "middle"))
            def _():
                _run(_exchange_steps(ex, in_refs, out_refs, sems)[1])


def _exchange_wait(ex, in_refs, out_refs, sems, grid):
    if ex.arrays:
        @pl.when(_at_grid_step(grid, "last"))
        def _():
            _run(_exchange_steps(ex, in_refs, out_refs, sems)[2])


def _shard_block(ref, shard_shape, by_col, idx):
    r, c = shard_shape
    if by_col:
        return ref.at[:, pl.ds(pl.multiple_of(idx * c, LANES), c)]
    return ref.at[pl.ds(pl.multiple_of(idx * r, 16), r), :]


def _full_shape(shard_shape, by_col):
    r, c = shard_shape
    return (r, N_DEV * c) if by_col else (N_DEV * r, c)


def _gather_exchange(shards, col_sharded):
    shapes = [a.shape for a in shards]
    return _Exchange(
        shards, [jax.ShapeDtypeStruct(_full_shape(sh, bc), a.dtype) for a, sh, bc in zip(shards, shapes, col_sharded)],
        lambda w, refs, idx: refs[w],
        lambda w, refs, idx: _shard_block(refs[w], shapes[w], col_sharded[w], idx), [True] * len(shards))


def _scatter_exchange(grads, col_sharded):
    shapes = []
    for g, by_col in zip(grads, col_sharded):
        r, c = g.shape
        shapes.append((r, c // N_DEV) if by_col else (r // N_DEV, c))
    return _Exchange(
        grads, [jax.ShapeDtypeStruct((N_DEV,) + sh, g.dtype) for g, sh in zip(grads, shapes)],
        lambda w, refs, idx: _shard_block(refs[w], shapes[w], col_sharded[w], idx),
        lambda w, refs, mine: refs[w].at[mine])


def _broadcast_exchange(arrays):
    return _Exchange(arrays, [jax.ShapeDtypeStruct((N_DEV,) + a.shape, a.dtype) for a in arrays],
                     lambda w, refs, idx: refs[w], lambda w, refs, mine: refs[w].at[mine])


def _join(*exs):
    arrays, shapes, owner = [], [], []
    for e in exs:
        for w in range(len(e.arrays)):
            owner.append((e, w, len(arrays), len(shapes)))
        arrays += e.arrays
        shapes += e.out_shapes

    def src(w, refs, idx):
        e, w0, i0, _ = owner[w]
        return e.src(w0, refs[i0:i0 + len(e.arrays)], idx)

    def dst(w, refs, idx):
        e, w0, _, o0 = owner[w]
        return e.dst(w0, refs[o0:o0 + len(e.out_shapes)], idx)

    return _Exchange(arrays, shapes, src, dst, [flag for e in exs for flag in e.relayed])


def _exchange_call(ex, name):
    n_in = len(ex.arrays)

    def body(*refs):
        in_refs, _, out_refs, sems = _split_refs(refs, ex, 0)
        for steps in _exchange_steps(ex, in_refs, out_refs, sems):
            _run(steps)

    return pl.pallas_call(
        body, name=name, in_specs=[ANY] * n_in, out_specs=[ANY] * len(ex.out_shapes), out_shape=ex.out_shapes,
        scratch_shapes=_exchange_sems(ex), compiler_params=pltpu.CompilerParams(vmem_limit_bytes=VMEM_LIMIT),
    )(*ex.arrays)


def _to_bf16(arrays):
    def body(*refs):
        for src, dst in zip(refs[:len(arrays)], refs[len(arrays):]):
            dst[...] = src[...].astype(BF16)

    vmem = pl.BlockSpec(memory_space=pltpu.VMEM)
    return pl.pallas_call(
        body, name="weights_to_bf16", in_specs=[vmem] * len(arrays), out_specs=[vmem] * len(arrays),
        out_shape=[jax.ShapeDtypeStruct(a.shape, BF16) for a in arrays],
        compiler_params=pltpu.CompilerParams(vmem_limit_bytes=VMEM_LIMIT),
    )(*arrays)


def _adamw(w, g, m, v):
    m = ADAM_B1 * m + (1.0 - ADAM_B1) * g
    v = ADAM_B2 * v + (1.0 - ADAM_B2) * jnp.square(g)
    m_hat = m / (1.0 - ADAM_B1 ** ADAM_STEP)
    v_hat = v / (1.0 - ADAM_B2 ** ADAM_STEP)
    delta = -ADAM_LR * (m_hat / (jnp.sqrt(v_hat) + ADAM_EPS) + ADAM_WD * w)
    return delta, m, v


def _sum_and_adamw(parts, w, m, v, name):
    r, c = w.shape
    tr = min(r, 256)

    def body(p_ref, w_ref, m_ref, v_ref, g_out, d_out, m_out, v_out):
        g = p_ref[0].astype(F32)
        for dev in range(1, N_DEV):
            g = g + p_ref[dev].astype(F32)
        g_out[...] = g
        d_out[...], m_out[...], v_out[...] = _adamw(w_ref[...], g, m_ref[...], v_ref[...])

    blk = pl.BlockSpec((tr, c), lambda i: (i, 0))
    return pl.pallas_call(
        body, name=name, grid=(r // tr,),
        in_specs=[pl.BlockSpec((N_DEV, tr, c), lambda i: (0, i, 0)), blk, blk, blk],
        out_specs=[blk] * 4, out_shape=[jax.ShapeDtypeStruct((r, c), F32)] * 4,
        compiler_params=_params(1),
    )(parts, w, m, v)


BIG = ("w_in", "w_attn_out", "w_conv_out", "w_o", "w_up", "w_down", "w_ple_gate", "w_ple_proj")
COL_SHARDED = {"w_in": True, "w_attn_out": True, "w_conv_out": True, "w_o": False, "w_up": True, "w_down": False,
               "w_ple_gate": False, "w_ple_proj": True}
SMALL = ("g_pre_mix", "b_gate", "g_post_mix", "g_pre_mlp", "g_post_mlp", "g_ple")


REST = BIG[1:]


def _local_grads(x, h1, p, target, small, wconv, full, aw, cw, tm, t, gather_rest=None, scatter_rest=None,
                 scatter_in=None):
    full = dict(full)
    tm_wide = min(WIDE_BLOCKS * tm, x.shape[0])
    qkv, conv, gate = _in_proj_fwd(h1, small["b_gate"], full["w_in"], aw, cw, tm_wide)
    o, o_b, *rest = _attn_fwd(qkv, aw, t, gather_rest)
    full.update(zip(REST, rest))
    x1, mixed, mix_in, conv_in = _mix_fwd(x, o_b, conv, gate, wconv, small["g_post_mix"], full["w_attn_out"],
                                          full["w_conv_out"], full["w_o"], tm_wide)
    (dx1, h2, du, a, df, h3, ds3, dpp, loss, dg_pre_mlp, dg_post_mlp, dg_ple) = _mlp_ple_loss(
        x1, p, target, small["g_pre_mlp"], small["g_post_mlp"], small["g_ple"], full["w_up"], full["w_down"],
        full["w_ple_gate"], full["w_ple_proj"], tm)
    big = {"w_up": _weight_grad(h2, du, "dw_up"), "w_down": _weight_grad(a, df, "dw_down"),
           "w_ple_gate": _weight_grad(h3, ds3, "dw_ple_gate"), "w_ple_proj": _weight_grad(p, dpp, "dw_ple_proj")}
    (dmixed, dattn, dconvout, do, drest, dg_post_mix, db_gate, dwconv) = _mix_bwd(
        dx1, mixed, o_b, conv, gate, wconv, small["g_post_mix"], full["w_attn_out"], full["w_conv_out"], full["w_o"],
        tm_wide)
    big.update({"w_attn_out": _weight_grad(o_b, dattn, "dw_attn_out"),
                "w_conv_out": _weight_grad(conv_in, dconvout, "dw_conv_out"),
                "w_o": _weight_grad(mix_in, dmixed, "dw_o")})
    dq, dk, dv, *scattered = _attn_bwd(qkv, o, do, aw, t, scatter_rest and scatter_rest([big[n] for n in REST]))
    pieces = [dq, dk, dv, drest]
    dw_in, col0, ni = None, 0, full["w_in"].shape[1]
    for i, piece in enumerate(pieces):
        dw_in = _weight_grad(h1, piece, "dw_in_%d" % i, dw_in, col0, ni)
        col0 += piece.shape[1]
    big["w_in"] = dw_in
    dx, dg_pre_mix, *scattered_in = _in_proj_bwd(x, dx1, pieces, small["g_pre_mix"], full["w_in"], tm_wide,
                                                scatter_in and scatter_in(dw_in))
    small_grads = {"g_pre_mix": dg_pre_mix, "b_gate": db_gate, "g_post_mix": dg_post_mix, "g_pre_mlp": dg_pre_mlp,
                   "g_post_mlp": dg_post_mlp, "g_ple": dg_ple, "w_conv": dwconv}
    return loss[0, 0], dx, big, small_grads, scattered_in + scattered


PACK_ROWS = 16


def _pack_layout(shapes, d):
    slots, at = [], 0
    for i, (r, c) in enumerate(shapes):
        assert d % c == 0
        for row in range(r):
            slots.append((i, row, at // d, at % d))
            at += c
        at = -(-at // d) * d
    assert at <= PACK_ROWS * d
    return slots


def _pack_small(groups, d):
    shapes = [a.shape for a in groups[0]]
    slots = _pack_layout(shapes, d)
    n = len(shapes)

    def body(*refs):
        ins, outs = refs[:n * len(groups)], refs[n * len(groups):]
        for g, out in enumerate(outs):
            out[...] = jnp.zeros_like(out)
            for i, row, pr, pc in slots:
                src = ins[g * n + i]
                out[pr:pr + 1, pc:pc + shapes[i][1]] = src[row:row + 1, :]

    vmem = pl.BlockSpec(memory_space=pltpu.VMEM)
    return pl.pallas_call(
        body, name="pack_small", in_specs=[vmem] * (n * len(groups)), out_specs=[vmem] * len(groups),
        out_shape=[jax.ShapeDtypeStruct((PACK_ROWS, d), F32)] * len(groups),
    )(*[a for group in groups for a in group])


def _unpack_small(pack, shapes, d):
    slots = _pack_layout(shapes, d)
    return [jnp.stack([pack[pr, pc:pc + shapes[i][1]] for j, row, pr, pc in slots if j == i])
            for i in range(len(shapes))]


def kernel(x, p, g_pre_mix, w_in, b_gate, w_conv, w_attn_out, w_conv_out, w_o, g_post_mix, g_pre_mlp, w_up, w_down, g_post_mlp, g_ple, w_ple_gate, w_ple_proj, loss_target, m_g_pre_mix, m_w_in, m_b_gate, m_w_conv, m_w_attn_out, m_w_conv_out, m_w_o, m_g_post_mix, m_g_pre_mlp, m_w_up, m_w_down, m_g_post_mlp, m_g_ple, m_w_ple_gate, m_w_ple_proj, v_g_pre_mix, v_w_in, v_b_gate, v_w_conv, v_w_attn_out, v_w_conv_out, v_w_o, v_g_post_mix, v_g_pre_mlp, v_w_up, v_w_down, v_g_post_mlp, v_g_ple, v_w_ple_gate, v_w_ple_proj):
    given = dict(locals())
    order = ["g_pre_mix", "w_in", "b_gate", "w_conv", "w_attn_out", "w_conv_out", "w_o", "g_post_mix", "g_pre_mlp",
             "w_up", "w_down", "g_post_mlp", "g_ple", "w_ple_gate", "w_ple_proj"]
    d = x.shape[-1]
    me = 4 * lax.axis_index("x") + 2 * lax.axis_index("y") + lax.axis_index("c")

    col = [COL_SHARDED[n] for n in BIG]
    shards = _to_bf16([given[n][0] for n in BIG])
    cw_shard = w_conv.shape[-1]
    conv_tile = jnp.pad(w_conv[0], ((0, HALO - CONV_K), (0, LANES - cw_shard)))
    h1, w_in_full, conv_g = _pre_norm(
        x[0], g_pre_mix, min(WIDE_BLOCKS * ROW_BLOCK, x.shape[1]),
        _join(_gather_exchange(shards[:1], col[:1]), _broadcast_exchange([conv_tile])))
    wconv = jnp.concatenate([conv_g[dev, :CONV_K, :cw_shard] for dev in range(N_DEV)], axis=1)

    small = {n: given[n] for n in SMALL}
    loss, dx, big_grads, small_grads, parts = _local_grads(
        x[0], h1, p[0, 0], loss_target[0], small, wconv, {"w_in": w_in_full}, w_attn_out.shape[1], w_conv_out.shape[1],
        ROW_BLOCK, ATTN_BLOCK,
        _gather_exchange(shards[1:], col[1:]), lambda grads: _scatter_exchange(grads, col[1:]),
        lambda grad: _scatter_exchange([grad], col[:1]))
    small_names = list(SMALL) + ["w_conv"]
    two_d = lambda a: a.reshape(-1, d) if a.shape[-1] > d else a.reshape(-1, a.shape[-1])
    full_conv = lambda a: lax.dynamic_update_slice(jnp.zeros((CONV_K, N_DEV * cw_shard), F32), a[0],
                                                   (jnp.int32(0), me * cw_shard))
    groups = [[two_d(small_grads[n]) for n in small_names] + [loss.reshape(1, 1)]]
    for pre in ("", "m_", "v_"):
        groups.append([two_d(given[pre + n]) for n in SMALL] + [full_conv(given[pre + "w_conv"]), jnp.zeros((1, 1), F32)])
    pack, *state = _pack_small(groups, d)
    packs, = _exchange_call(_broadcast_exchange([pack]), "share_small_grads")

    grads, deltas, new_m, new_v = {}, {}, {}, {}
    for n, part in zip(BIG, parts):
        grads[n], deltas[n], new_m[n], new_v[n] = (
            a[None] for a in _sum_and_adamw(part, given[n][0], given["m_" + n][0], given["v_" + n][0], "adamw_" + n))

    outs = _sum_and_adamw(packs, *state, "adamw_small")
    shapes = [a.shape for a in groups[0]]
    for res, dst in zip(outs, (grads, deltas, new_m, new_v)):
        for n, a in zip(small_names + ["loss"], _unpack_small(res, shapes, d)):
            if n == "w_conv":
                a = lax.dynamic_slice(a, (jnp.int32(0), me * cw_shard), (CONV_K, cw_shard))[None]
            dst[n] = a.reshape(given[n].shape) if n in SMALL else a
    loss = grads["loss"][0, 0]

    return (loss, dx[None], *[grads[n] for n in order], *[deltas[n] for n in order],
            *[new_m[n] for n in order], *[new_v[n] for n in order])
```

```python
import jax
import jax.numpy as jnp
from jax import lax
from jax.experimental import pallas as pl
from jax.experimental.pallas import tpu as pltpu

F32 = jnp.float32
BF16 = jnp.bfloat16
RMS_EPS = 1e-6
N_DEV = 8
MESH_AXES = ("x", "y", "c")
LANES = 128
HEAD_DIM = 64
HEADS_PER_GROUP = LANES // HEAD_DIM
CONV_K = 3
HALO = 8
HALO_BF16 = 16
VMEM_LIMIT = 56 * 1024 * 1024
EXP2_ZERO = -150.0
LOG2_E = 1.4426950408889634

ADAM_LR = 0.001
ADAM_B1 = 0.9
ADAM_B2 = 0.999
ADAM_EPS = 1e-08
ADAM_WD = 0.01
ADAM_STEP = 10

ROW_BLOCK = 256
WIDE_BLOCKS = 2
ATTN_BLOCK = 256
ATTN_ROW_SPLITS = 2
DW_TOKENS = 2048
DW_TILE = 1024
DW_MANY_TILES = 6
DW_PIECE_TILE = 512
FF_CHUNK = 1024
PROJ_CHUNK = 512


def _dot(a, b):
    return lax.dot_general(a, b, (((1,), (0,)), ((), ())), preferred_element_type=F32)


def _dot_nt(a, b):
    return lax.dot_general(a, b, (((1,), (1,)), ((), ())), preferred_element_type=F32)


def _dot_tn(a, b):
    return lax.dot_general(a, b, (((0,), (0,)), ((), ())), preferred_element_type=F32)


def _sigmoid(z):
    return 1.0 / (1.0 + jnp.exp(-z))


def _rms_scale(x):
    return lax.rsqrt(jnp.mean(x * x, axis=-1, keepdims=True) + RMS_EPS)


def _rms_bwd(xhat, r, g, dy):
    gd = dy * g
    return r * (gd - xhat * jnp.mean(gd * xhat, axis=-1, keepdims=True))


def _params(n_axes, **kw):
    return pltpu.CompilerParams(dimension_semantics=("arbitrary",) * n_axes, vmem_limit_bytes=VMEM_LIMIT, **kw)


def _load_resident(pairs, sem):
    @pl.when(pl.program_id(0) == 0)
    def _():
        copies = [pltpu.make_async_copy(src, dst, sem.at[i]) for i, (src, dst) in enumerate(pairs)]
        for cp in copies:
            cp.start()
        for cp in copies:
            cp.wait()


def _row_spec(tm, width):
    return pl.BlockSpec((tm, width), lambda i: (i, 0))


def _prev_halo_spec(tm, width, rows):
    per = tm // rows
    return pl.BlockSpec((rows, width), lambda i: (jnp.maximum(i * per - 1, 0), 0))


def _const_spec(shape):
    return pl.BlockSpec(shape, lambda i: (0,) * len(shape))


ANY = pl.BlockSpec(memory_space=pl.ANY)


def _shift_down(cur, prev, n):
    rows = lax.broadcasted_iota(jnp.int32, cur.shape, 0)
    out = pltpu.roll(cur, n, 0)
    for j in range(n):
        out = jnp.where(rows == j, prev[prev.shape[0] - n + j:prev.shape[0] - n + j + 1, :], out)
    return out


def _shift_up(cur, nxt, n):
    tm = cur.shape[0]
    rows = lax.broadcasted_iota(jnp.int32, cur.shape, 0)
    out = pltpu.roll(cur, tm - n, 0)
    for j in range(n):
        out = jnp.where(rows == tm - n + j, nxt[j:j + 1, :], out)
    return out


def _conv_taps(cm, cm_prev, wconv):
    cm1 = _shift_down(cm, cm_prev, 1)
    cm2 = _shift_down(cm, cm_prev, 2)
    cv = wconv[2:3, :] * cm + wconv[1:2, :] * cm1 + wconv[0:1, :] * cm2
    return cv, cm1, cm2


def _pre_norm(x, g1, tm, exchange):
    s, d = x.shape
    grid = (s // tm,)
    ex = exchange

    def body(x_ref, g_ref, *rest):
        ex_in, (h_ref,), ex_out, sems = _split_refs(rest, ex, 1)

        @pl.when(_at_grid_step(grid, "first"))
        def _():
            _run(_exchange_steps(ex, ex_in, ex_out, sems)[0])

        xv = x_ref[...]
        h_ref[...] = (xv * _rms_scale(xv) * g_ref[...]).astype(BF16)

        @pl.when(_at_grid_step(grid, "last"))
        def _():
            _, relay, finish = _exchange_steps(ex, ex_in, ex_out, sems)
            _run(relay)
            _run(finish)

    return pl.pallas_call(
        body, name="pre_norm", grid=grid,
        in_specs=[_row_spec(tm, d), _const_spec((1, d))] + [ANY] * len(ex.arrays),
        out_specs=[_row_spec(tm, d)] + [ANY] * len(ex.out_shapes),
        out_shape=[jax.ShapeDtypeStruct((s, d), BF16)] + ex.out_shapes,
        scratch_shapes=_exchange_sems(ex),
        compiler_params=_params(1),
    )(x, g1, *ex.arrays)


def _in_proj_fwd(h1, b_gate, w_in, aw, cw, tm):
    s, d = h1.shape
    ni = w_in.shape[1]
    n_qkv, n_conv = 3 * aw, 3 * cw
    ch = PROJ_CHUNK

    def body(h_ref, b_ref, w_hbm, qkv_ref, conv_ref, gate_ref, w_vmem, sem):
        _load_resident([(w_hbm, w_vmem)], sem)
        h = h_ref[...]
        for c0 in range(0, ni, ch):
            pc = _dot(h, w_vmem[:, c0:c0 + ch])
            if c0 < n_qkv:
                qkv_ref[:, c0:c0 + ch] = pc.astype(BF16)
            elif c0 < n_qkv + n_conv:
                conv_ref[:, c0 - n_qkv:c0 - n_qkv + ch] = pc.astype(BF16)
            else:
                g0 = c0 - n_qkv - n_conv
                gate_ref[:, g0:g0 + ch] = _sigmoid(pc + b_ref[:, g0:g0 + ch]).astype(BF16)

    return pl.pallas_call(
        body, name="in_proj_fwd", grid=(s // tm,),
        in_specs=[_row_spec(tm, d), _const_spec((1, 2 * d)), ANY],
        out_specs=[_row_spec(tm, n_qkv), _row_spec(tm, n_conv), _row_spec(tm, 2 * d)],
        out_shape=[jax.ShapeDtypeStruct((s, n_qkv), BF16), jax.ShapeDtypeStruct((s, n_conv), BF16),
                   jax.ShapeDtypeStruct((s, 2 * d), BF16)],
        scratch_shapes=[pltpu.VMEM((d, ni), BF16), pltpu.SemaphoreType.DMA((1,))],
        compiler_params=_params(1),
    )(h1, b_gate, w_in)


def _split_hi_lo(a):
    hi = a.astype(BF16)
    return hi, (a - hi.astype(F32)).astype(BF16)


def _log2_gates(z):
    z2 = z * LOG2_E
    nz2 = -z2
    log_keep = jnp.minimum(nz2, 0.0) - jnp.log2(1.0 + jnp.exp2(jnp.minimum(z2, nz2)))
    return log_keep + z2, log_keep


def _attn_masks(t):
    row = lax.broadcasted_iota(jnp.int32, (t, t), 0)
    col = lax.broadcasted_iota(jnp.int32, (t, t), 1)
    return (col < row).astype(F32), (row > col).astype(BF16), (row >= col).astype(BF16)


def _chains(a):
    tr = a.shape[0] // ATTN_ROW_SPLITS
    return [jnp.where(_head_lanes(h), a[r * tr:(r + 1) * tr], jnp.zeros((tr, LANES), a.dtype))
            for h in range(HEADS_PER_GROUP) for r in range(ATTN_ROW_SPLITS)]


def _merge_chains(parts):
    rows = []
    for r in range(ATTN_ROW_SPLITS):
        out = parts[r]
        for h in range(1, HEADS_PER_GROUP):
            out = jnp.where(_head_lanes(h), parts[h * ATTN_ROW_SPLITS + r], out)
        rows.append(out)
    return jnp.concatenate(rows, axis=0)


def _by_stage(n_chains, stages):
    for stage in stages:
        for c in range(n_chains):
            stage(c)


def _row_parts(a):
    tr = a.shape[0] // ATTN_ROW_SPLITS
    return [a[r * tr:(r + 1) * tr] for r in range(ATTN_ROW_SPLITS)]


def _while_weights_live(qi, block, carry):
    def cond(state):
        j, carry = state
        live = jnp.max(carry[0][0])
        for run in carry[0][1:]:
            live = jnp.maximum(live, jnp.max(run))
        return jnp.logical_and(j < qi, live >= EXP2_ZERO)

    def step(state):
        j, carry = state
        return j + 1, block(qi - 1 - j, carry)

    return lax.while_loop(cond, step, (jnp.int32(0), carry))[1]


def _head_lanes(h):
    lane = lax.broadcasted_iota(jnp.int32, (1, LANES), 1)
    return (lane >= HEAD_DIM * h) & (lane < HEAD_DIM * (h + 1))


def _attn_fwd(qkv, aw, t, exchange=None):
    s = qkv.shape[0]
    groups = aw // LANES
    nq = s // t
    scale = HEAD_DIM ** -0.5
    ex = exchange or _NO_EXCHANGE
    causal, upper, _ = _attn_masks(t)
    mask_spec = pl.BlockSpec((t, t), lambda g, i: (0, 0))

    def body(q_ref, k_ref, v_ref, causal_ref, upper_ref, *rest):
        ex_in, (o_ref, ob_ref), ex_out, sems = _split_refs(rest, ex, 2)
        qi = pl.program_id(1)
        _exchange_start(ex, ex_in, ex_out, sems, (groups, nq))
        upper = upper_ref[...]
        causal = _row_parts(causal_ref[...] > 0.5) * HEADS_PER_GROUP
        qs = _chains(q_ref[...] * scale)
        heads = range(len(qs))
        tr = t // ATTN_ROW_SPLITS

        def block(kb, runs, accs, diag):
            rows = pl.ds(pl.multiple_of(kb * t, t), t)
            k = k_ref[rows, :]
            v = v_ref[rows, :]
            ncs = [(h % ATTN_ROW_SPLITS + 1) * tr if diag else t for h in heads]
            live = [{} for _ in heads]
            new_runs, new_accs = [None] * len(heads), [None] * len(heads)

            def scores(h):
                live[h]["z"] = _dot_nt(qs[h], k[0:ncs[h]])

            def gates(h):
                nc = ncs[h]
                log_b, log_keep = _log2_gates(live[h].pop("z"))
                if diag:
                    log_keep = jnp.where(causal[h][:, 0:nc], log_keep, 0.0)
                hi, lo = _split_hi_lo(log_keep)
                live[h]["log_w"] = log_b + runs[h]
                live[h]["between"] = _dot(hi, upper[0:nc, 0:nc]) + _dot(lo, upper[0:nc, 0:nc])
                new_runs[h] = runs[h] + jnp.sum(log_keep, axis=1, keepdims=True)

            def weights(h):
                nc = ncs[h]
                w = jnp.exp2(live[h].pop("log_w") + live[h].pop("between"))
                if diag:
                    w = jnp.where(causal[h][:, 0:nc], w, 0.0)
                new_accs[h] = accs[h] + _dot(w.astype(BF16), v[0:nc])

            _by_stage(len(heads), [scores, gates, weights])
            return tuple(new_runs), tuple(new_accs)

        carry = block(qi, [jnp.zeros((tr, 1), F32)] * len(heads), [jnp.zeros((tr, LANES), F32)] * len(heads), True)
        _, accs = _while_weights_live(qi, lambda kb, carry: block(kb, *carry, False), carry)
        o = _merge_chains(accs)
        o_ref[...] = o
        ob_ref[...] = o.astype(BF16)
        _exchange_wait(ex, ex_in, ex_out, sems, (groups, nq))

    return pl.pallas_call(
        body, name="attn_fwd", grid=(groups, nq),
        in_specs=[pl.BlockSpec((t, LANES), lambda g, i: (i, g)),
                  pl.BlockSpec((s, LANES), lambda g, i: (0, groups + g)),
                  pl.BlockSpec((s, LANES), lambda g, i: (0, 2 * groups + g)), mask_spec, mask_spec]
        + [ANY] * len(ex.arrays),
        out_specs=[pl.BlockSpec((t, LANES), lambda g, i: (i, g))] * 2 + [ANY] * len(ex.out_shapes),
        out_shape=[jax.ShapeDtypeStruct((s, aw), F32), jax.ShapeDtypeStruct((s, aw), BF16)] + ex.out_shapes,
        scratch_shapes=_exchange_sems(ex),
        compiler_params=_params(2),
    )(qkv, qkv, qkv, causal, upper, *ex.arrays)


def _attn_bwd(qkv, o, do, aw, t, exchange=None):
    s = qkv.shape[0]
    groups = aw // LANES
    nq = s // t
    scale = HEAD_DIM ** -0.5
    ex = exchange or _NO_EXCHANGE

    def body(q_ref, k_ref, v_ref, o_ref, do_ref, causal_ref, upper_ref, lower_ref, *rest):
        ex_in, (dq_ref, dk_ref, dv_ref), ex_out, (dk_acc, dv_acc, *sems) = _split_refs(rest, ex, 3)
        qi = pl.program_id(1)
        _exchange_start(ex, ex_in, ex_out, sems, (groups, nq))

        @pl.when(qi == 0)
        def _():
            dk_acc[...] = jnp.zeros_like(dk_acc)
            dv_acc[...] = jnp.zeros_like(dv_acc)

        upper = upper_ref[...]
        lower_incl = lower_ref[...]
        causal = _row_parts(causal_ref[...] > 0.5) * HEADS_PER_GROUP
        q = q_ref[...] * scale
        do_b = do_ref[...]
        qs = _chains(q)
        dos = _chains(do_b)
        qs_all = jnp.concatenate(qs, axis=0)
        dos_all = jnp.concatenate(dos, axis=0)
        e_totals = [jnp.sum(part, axis=1, keepdims=True) for part in _chains(do_b.astype(F32) * o_ref[...])]
        heads = range(len(qs))
        tr = t // ATTN_ROW_SPLITS

        def block(kb, runs, e_runs, dqs, diag):
            rows = pl.ds(pl.multiple_of(kb * t, t), t)
            k = k_ref[rows, :]
            v = v_ref[rows, :]
            ncs = [(h % ATTN_ROW_SPLITS + 1) * tr if diag else t for h in heads]
            live = [{} for _ in heads]
            none = [None] * len(heads)
            new_runs, new_e_runs, new_dqs, dzbs, wbs = list(none), list(none), list(none), list(none), list(none)

            def scores(h):
                live[h]["z"] = _dot_nt(qs[h], k[0:ncs[h]])
                live[h]["dw"] = _dot_nt(dos[h], v[0:ncs[h]])

            def gates(h):
                nc = ncs[h]
                log_b, log_keep = _log2_gates(live[h].pop("z"))
                live[h]["beta"] = jnp.exp2(log_b)
                live[h]["keep"] = jnp.exp2(log_keep)
                if diag:
                    log_keep = jnp.where(causal[h][:, 0:nc], log_keep, 0.0)
                hi, lo = _split_hi_lo(log_keep)
                live[h]["log_w"] = log_b + runs[h]
                live[h]["between"] = _dot(hi, upper[0:nc, 0:nc]) + _dot(lo, upper[0:nc, 0:nc])
                new_runs[h] = runs[h] + jnp.sum(log_keep, axis=1, keepdims=True)

            def weights(h):
                nc = ncs[h]
                w = jnp.exp2(live[h].pop("log_w") + live[h].pop("between"))
                if diag:
                    w = jnp.where(causal[h][:, 0:nc], w, 0.0)
                wb = w.astype(BF16)
                e = live[h].pop("dw") * wb.astype(F32)
                hi, lo = _split_hi_lo(e)
                live[h]["e"] = e
                live[h]["e_suffix"] = _dot(hi, lower_incl[0:nc, 0:nc]) + _dot(lo, lower_incl[0:nc, 0:nc]) + e_runs[h]
                wbs[h] = wb

            def score_grads(h):
                nc = ncs[h]
                e_suffix = live[h].pop("e_suffix")
                dz = live[h].pop("e") * live[h].pop("keep") - (e_totals[h] - e_suffix) * live[h].pop("beta")
                if diag:
                    dz = jnp.where(causal[h][:, 0:nc], dz, 0.0)
                dzb = dz.astype(BF16)
                new_dqs[h] = dqs[h] + _dot(dzb, k[0:nc])
                new_e_runs[h] = e_suffix[:, 0:1]
                if nc < t:
                    unseen = jnp.zeros((tr, t - nc), BF16)
                    dzb = jnp.concatenate([dzb, unseen], axis=1)
                    wbs[h] = jnp.concatenate([wbs[h], unseen], axis=1)
                dzbs[h] = dzb

            _by_stage(len(heads), [scores, gates, weights, score_grads])
            dk_acc[rows, :] += _dot_tn(jnp.concatenate(dzbs, axis=0), qs_all)
            dv_acc[rows, :] += _dot_tn(jnp.concatenate(wbs, axis=0), dos_all)
            return tuple(new_runs), tuple(new_e_runs), tuple(new_dqs)

        zero_cols = [jnp.zeros((tr, 1), F32)] * len(heads)
        carry = block(qi, zero_cols, zero_cols, [jnp.zeros((tr, LANES), F32)] * len(heads), True)
        _, _, dqs = _while_weights_live(qi, lambda kb, carry: block(kb, *carry, False), carry)
        dq_ref[...] = (_merge_chains(dqs) * scale).astype(BF16)

        @pl.when(qi == nq - 1)
        def _():
            dk_ref[...] = dk_acc[...].astype(BF16)
            dv_ref[...] = dv_acc[...].astype(BF16)

        _exchange_wait(ex, ex_in, ex_out, sems, (groups, nq))

    blk = pl.BlockSpec((t, LANES), lambda g, i: (i, g))
    slab = pl.BlockSpec((s, LANES), lambda g, i: (0, g))
    mask_spec = pl.BlockSpec((t, t), lambda g, i: (0, 0))
    return pl.pallas_call(
        body, name="attn_bwd", grid=(groups, nq),
        in_specs=[blk, pl.BlockSpec((s, LANES), lambda g, i: (0, groups + g)),
                  pl.BlockSpec((s, LANES), lambda g, i: (0, 2 * groups + g)), blk, blk, mask_spec, mask_spec, mask_spec]
        + [ANY] * len(ex.arrays),
        out_specs=[blk, slab, slab] + [ANY] * len(ex.out_shapes),
        out_shape=[jax.ShapeDtypeStruct((s, aw), BF16)] * 3 + ex.out_shapes,
        scratch_shapes=[pltpu.VMEM((s, LANES), F32), pltpu.VMEM((s, LANES), F32)] + _exchange_sems(ex),
        compiler_params=_params(2),
    )(qkv, qkv, qkv, o, do, *_attn_masks(t), *ex.arrays)


def _branches(o_b, conv, conv_prev, wconv, w_ao, w_co, cw, first):
    conv = conv.astype(F32)
    conv_prev = conv_prev.astype(F32)
    cb = conv[:, 0:cw]
    cm = conv[:, cw:2 * cw] * conv[:, 2 * cw:3 * cw]
    cm_prev = conv_prev[:, cw:2 * cw] * conv_prev[:, 2 * cw:3 * cw]
    cm_prev = jnp.where(first, 0.0, cm_prev)
    cv, cm1, cm2 = _conv_taps(cm, cm_prev, wconv)
    conv_in = (cb * cv).astype(BF16)
    return _dot(o_b, w_ao), _dot(conv_in, w_co), conv_in, cb, cv, cm, cm1, cm2


def _mix_fwd(x, o, conv, gate, wconv, g_post, w_ao, w_co, w_o, tm):
    s, d = x.shape
    aw, cw = w_ao.shape[0], w_co.shape[0]

    def body(x_ref, o_ref, conv_ref, prev_ref, gate_ref, wc_ref, g_ref, wao_hbm, wco_hbm, wo_hbm,
             x1_ref, mixed_ref, mixin_ref, convin_ref, wao, wco, wo, sem):
        _load_resident([(wao_hbm, wao), (wco_hbm, wco), (wo_hbm, wo)], sem)
        y_attn, y_conv, conv_in, *_ = _branches(
            o_ref[...].astype(BF16), conv_ref[...], prev_ref[...], wc_ref[...], wao[...], wco[...], cw,
            pl.program_id(0) == 0)
        mix_in = (gate_ref[:, 0:d].astype(F32) * y_attn + gate_ref[:, d:2 * d].astype(F32) * y_conv).astype(BF16)
        mixed = _dot(mix_in, wo[...])
        x1_ref[...] = x_ref[...] + mixed * _rms_scale(mixed) * g_ref[...]
        mixed_ref[...] = mixed
        mixin_ref[...] = mix_in
        convin_ref[...] = conv_in

    return pl.pallas_call(
        body, name="mix_fwd", grid=(s // tm,),
        in_specs=[_row_spec(tm, d), _row_spec(tm, aw), _row_spec(tm, 3 * cw), _prev_halo_spec(tm, 3 * cw, HALO_BF16),
                  _row_spec(tm, 2 * d), _const_spec((CONV_K, cw)), _const_spec((1, d)), ANY, ANY, ANY],
        out_specs=[_row_spec(tm, d), _row_spec(tm, d), _row_spec(tm, d), _row_spec(tm, cw)],
        out_shape=[jax.ShapeDtypeStruct((s, d), F32), jax.ShapeDtypeStruct((s, d), F32),
                   jax.ShapeDtypeStruct((s, d), BF16), jax.ShapeDtypeStruct((s, cw), BF16)],
        scratch_shapes=[pltpu.VMEM(w_ao.shape, BF16), pltpu.VMEM(w_co.shape, BF16), pltpu.VMEM(w_o.shape, BF16),
                        pltpu.SemaphoreType.DMA((3,))],
        compiler_params=_params(1),
    )(x, o, conv, conv, gate, wconv, g_post, w_ao, w_co, w_o)


def _mix_bwd(dx1, mixed, o, conv, gate, wconv, g_post, w_ao, w_co, w_o, tm):
    s, d = dx1.shape
    aw, cw = w_ao.shape[0], w_co.shape[0]
    n = s // tm
    per = tm // HALO_BF16

    def body(dx1_ref, mixed_ref, o_ref, conv_ref, prev_ref, gate_ref, wc_ref, g_ref, wao_hbm, wco_hbm, wo_hbm,
             dmixed_ref, dattn_ref, dconvout_ref, do_ref, drest_ref, dg_ref, dbias_ref, dwc_ref,
             wao, wco, wo, dcv_next, sem):
        i = pl.program_id(0)
        _load_resident([(wao_hbm, wao), (wco_hbm, wco), (wo_hbm, wo)], sem)

        @pl.when(i == 0)
        def _():
            dg_ref[...] = jnp.zeros_like(dg_ref)
            dbias_ref[...] = jnp.zeros_like(dbias_ref)
            dwc_ref[...] = jnp.zeros_like(dwc_ref)
            dcv_next[...] = jnp.zeros_like(dcv_next)

        mixed = mixed_ref[...]
        r = _rms_scale(mixed)
        mhat = mixed * r
        dn = dx1_ref[...]
        dg_ref[...] += jnp.sum(dn * mhat, axis=0, keepdims=True)
        dmixed = _rms_bwd(mhat, r, g_ref[...], dn).astype(BF16)
        dmixed_ref[...] = dmixed
        dmi = _dot_nt(dmixed, wo[...])

        wc = wc_ref[...]
        conv = conv_ref[...].astype(F32)
        y_attn, y_conv, _, cb, cv, cm, cm1, cm2 = _branches(
            o_ref[...].astype(BF16), conv, prev_ref[...], wc, wao[...], wco[...], cw, i == n - 1)
        ga = gate_ref[:, 0:d].astype(F32)
        gc = gate_ref[:, d:2 * d].astype(F32)
        dpre_a = dmi * y_attn * ga * (1.0 - ga)
        dpre_c = dmi * y_conv * gc * (1.0 - gc)
        drest_ref[:, 3 * cw:3 * cw + d] = dpre_a.astype(BF16)
        drest_ref[:, 3 * cw + d:3 * cw + 2 * d] = dpre_c.astype(BF16)
        dbias_ref[:, 0:d] += jnp.sum(dpre_a, axis=0, keepdims=True)
        dbias_ref[:, d:2 * d] += jnp.sum(dpre_c, axis=0, keepdims=True)

        dattn = (dmi * ga).astype(BF16)
        dattn_ref[...] = dattn
        do_ref[...] = _dot_nt(dattn, wao[...]).astype(BF16)
        dconvout = (dmi * gc).astype(BF16)
        dconvout_ref[...] = dconvout
        dconv_in = _dot_nt(dconvout, wco[...])
        drest_ref[:, 0:cw] = (dconv_in * cv).astype(BF16)

        dcv = dconv_in * cb
        following = dcv_next[...]
        dcm = wc[2:3, :] * dcv + wc[1:2, :] * _shift_up(dcv, following, 1) + wc[0:1, :] * _shift_up(dcv, following, 2)
        drest_ref[:, cw:2 * cw] = (dcm * conv[:, 2 * cw:3 * cw]).astype(BF16)
        drest_ref[:, 2 * cw:3 * cw] = (dcm * conv[:, cw:2 * cw]).astype(BF16)
        for tap, shifted in enumerate((cm2, cm1, cm)):
            dwc_ref[tap:tap + 1, :] += jnp.sum(dcv * shifted, axis=0, keepdims=True)
        dcv_next[...] = dcv[0:HALO, :]

    def rows(width):
        return pl.BlockSpec((tm, width), lambda i: (n - 1 - i, 0))

    prev_halo = pl.BlockSpec((HALO_BF16, 3 * cw), lambda i: (jnp.maximum((n - 1 - i) * per - 1, 0), 0))
    n_rest = 3 * cw + 2 * d
    return pl.pallas_call(
        body, name="mix_bwd", grid=(n,),
        in_specs=[rows(d), rows(d), rows(aw), rows(3 * cw), prev_halo, rows(2 * d), _const_spec((CONV_K, cw)),
                  _const_spec((1, d)), ANY, ANY, ANY],
        out_specs=[rows(d), rows(d), rows(d), rows(aw), rows(n_rest), _const_spec((1, d)), _const_spec((1, 2 * d)),
                   _const_spec((CONV_K, cw))],
        out_shape=[jax.ShapeDtypeStruct((s, d), BF16), jax.ShapeDtypeStruct((s, d), BF16),
                   jax.ShapeDtypeStruct((s, d), BF16), jax.ShapeDtypeStruct((s, aw), BF16),
                   jax.ShapeDtypeStruct((s, n_rest), BF16), jax.ShapeDtypeStruct((1, d), F32),
                   jax.ShapeDtypeStruct((1, 2 * d), F32), jax.ShapeDtypeStruct((CONV_K, cw), F32)],
        scratch_shapes=[pltpu.VMEM(w_ao.shape, BF16), pltpu.VMEM(w_co.shape, BF16), pltpu.VMEM(w_o.shape, BF16),
                        pltpu.VMEM((HALO, cw), F32), pltpu.SemaphoreType.DMA((3,))],
        compiler_params=_params(1),
    )(dx1, mixed, o, conv, conv, gate, wconv, g_post, w_ao, w_co, w_o)


def _mlp_ple_loss(x1, p, target, g_pre, g_post, g_ple, w_up, w_dn, w_pg, w_pp, tm):
    s, d = x1.shape
    ff = w_up.shape[1]
    pd = p.shape[1]
    fc = FF_CHUNK

    def body(x1_ref, p_ref, t_ref, gpre_ref, gpost_ref, gple_ref, wup_hbm, wdn_hbm, wpg_hbm, wpp_hbm,
             dx1_ref, h2_ref, du_ref, a_ref, df_ref, h3_ref, ds3_ref, dpp_ref, loss_ref, dgpre_ref, dgpost_ref,
             dgple_ref, wup, wdn, wpg, wpp, u_scr, sem):
        _load_resident([(wup_hbm, wup), (wdn_hbm, wdn), (wpg_hbm, wpg), (wpp_hbm, wpp)], sem)

        @pl.when(pl.program_id(0) == 0)
        def _():
            for ref in (loss_ref, dgpre_ref, dgpost_ref, dgple_ref):
                ref[...] = jnp.zeros_like(ref)

        x1v = x1_ref[...]
        r2 = _rms_scale(x1v)
        x1hat = x1v * r2
        h2 = (x1hat * gpre_ref[...]).astype(BF16)
        h2_ref[...] = h2
        f = jnp.zeros((tm, d), F32)
        for c0 in range(0, ff, fc):
            u = _dot(h2, wup[:, c0:c0 + fc])
            u_scr[:, c0:c0 + fc] = u
            a = jnp.square(jnp.maximum(u, 0.0)).astype(BF16)
            a_ref[:, c0:c0 + fc] = a
            f = f + _dot(a, wdn[c0:c0 + fc, :])
        rf = _rms_scale(f)
        fhat = f * rf
        x2 = x1v + fhat * gpost_ref[...]
        r3 = _rms_scale(x2)
        x2hat = x2 * r3
        h3 = (x2hat * gple_ref[...]).astype(BF16)
        h3_ref[...] = h3
        pg = _sigmoid(_dot(h3, wpg[...]))
        pp = _dot(p_ref[...].astype(BF16), wpp[...])
        diff = x2 + pg * pp - t_ref[...]
        loss_ref[...] += 0.5 * jnp.sum(jnp.mean(diff * diff, axis=-1, keepdims=True), axis=0, keepdims=True)

        dy = diff * (1.0 / d)
        dpp_ref[...] = (dy * pg).astype(BF16)
        ds3 = (dy * pp * pg * (1.0 - pg)).astype(BF16)
        ds3_ref[...] = ds3
        dh3 = _dot_nt(ds3, wpg[...])
        dgple_ref[...] += jnp.sum(dh3 * x2hat, axis=0, keepdims=True)
        dx2 = dy + _rms_bwd(x2hat, r3, gple_ref[...], dh3)
        dgpost_ref[...] += jnp.sum(dx2 * fhat, axis=0, keepdims=True)
        df = _rms_bwd(fhat, rf, gpost_ref[...], dx2).astype(BF16)
        df_ref[...] = df
        dh2 = jnp.zeros((tm, d), F32)
        for c0 in range(0, ff, fc):
            da = _dot_nt(df, wdn[c0:c0 + fc, :])
            du = (da * (2.0 * jnp.maximum(u_scr[:, c0:c0 + fc], 0.0))).astype(BF16)
            du_ref[:, c0:c0 + fc] = du
            dh2 = dh2 + _dot_nt(du, wup[:, c0:c0 + fc])
        dgpre_ref[...] += jnp.sum(dh2 * x1hat, axis=0, keepdims=True)
        dx1_ref[...] = dx2 + _rms_bwd(x1hat, r2, gpre_ref[...], dh2)

    vec = _const_spec((1, d))
    return pl.pallas_call(
        body, name="mlp_ple_loss", grid=(s // tm,),
        in_specs=[_row_spec(tm, d), _row_spec(tm, pd), _row_spec(tm, d), vec, vec, vec, ANY, ANY, ANY, ANY],
        out_specs=[_row_spec(tm, d), _row_spec(tm, d), _row_spec(tm, ff), _row_spec(tm, ff), _row_spec(tm, d),
                   _row_spec(tm, d), _row_spec(tm, d), _row_spec(tm, d), _const_spec((1, 1)), vec, vec, vec],
        out_shape=[jax.ShapeDtypeStruct((s, d), F32), jax.ShapeDtypeStruct((s, d), BF16),
                   jax.ShapeDtypeStruct((s, ff), BF16), jax.ShapeDtypeStruct((s, ff), BF16),
                   jax.ShapeDtypeStruct((s, d), BF16), jax.ShapeDtypeStruct((s, d), BF16),
                   jax.ShapeDtypeStruct((s, d), BF16), jax.ShapeDtypeStruct((s, d), BF16),
                   jax.ShapeDtypeStruct((1, 1), F32), jax.ShapeDtypeStruct((1, d), F32),
                   jax.ShapeDtypeStruct((1, d), F32), jax.ShapeDtypeStruct((1, d), F32)],
        scratch_shapes=[pltpu.VMEM(w_up.shape, BF16), pltpu.VMEM(w_dn.shape, BF16), pltpu.VMEM(w_pg.shape, BF16),
                        pltpu.VMEM(w_pp.shape, BF16), pltpu.VMEM((tm, ff), F32), pltpu.SemaphoreType.DMA((4,))],
        compiler_params=_params(1),
    )(x1, p, target, g_pre, g_post, g_ple, w_up, w_dn, w_pg, w_pp)


def _in_proj_bwd(x, dx1, pieces, g1, w_in, tm, exchange=None):
    s, d = x.shape
    ni = w_in.shape[1]
    widths = [p.shape[1] for p in pieces]
    grid = (s // tm,)
    ex = exchange or _NO_EXCHANGE

    def body(x_ref, dx1_ref, *rest):
        piece_refs, rest = rest[:len(pieces)], rest[len(pieces):]
        g_ref, w_hbm = rest[0], rest[1]
        ex_in, (dx_ref, dg_ref), ex_out, (w_vmem, sem, *sems) = _split_refs(rest[2:], ex, 2)
        _exchange_start(ex, ex_in, ex_out, sems, grid)
        _load_resident([(w_hbm, w_vmem)], sem)

        @pl.when(pl.program_id(0) == 0)
        def _():
            dg_ref[...] = jnp.zeros_like(dg_ref)

        dh = jnp.zeros((tm, d), F32)
        c0 = 0
        for ref, width in zip(piece_refs, widths):
            dh = dh + _dot_nt(ref[...], w_vmem[:, c0:c0 + width])
            c0 += width
        xv = x_ref[...]
        r = _rms_scale(xv)
        xhat = xv * r
        dg_ref[...] += jnp.sum(dh * xhat, axis=0, keepdims=True)
        dx_ref[...] = dx1_ref[...] + _rms_bwd(xhat, r, g_ref[...], dh)
        _exchange_wait(ex, ex_in, ex_out, sems, grid)

    return pl.pallas_call(
        body, name="in_proj_bwd", grid=grid,
        in_specs=[_row_spec(tm, d), _row_spec(tm, d)] + [_row_spec(tm, w) for w in widths]
        + [_const_spec((1, d)), ANY] + [ANY] * len(ex.arrays),
        out_specs=[_row_spec(tm, d), _const_spec((1, d))] + [ANY] * len(ex.out_shapes),
        out_shape=[jax.ShapeDtypeStruct((s, d), F32), jax.ShapeDtypeStruct((1, d), F32)] + ex.out_shapes,
        scratch_shapes=[pltpu.VMEM((d, ni), BF16), pltpu.SemaphoreType.DMA((1,))] + _exchange_sems(ex),
        compiler_params=_params(1),
    )(x, dx1, *pieces, g1, w_in, *ex.arrays)


def _weight_grad(a, b, name, into=None, col0=0, n_total=None):
    s, m = a.shape
    n = b.shape[1]
    tm = min(m, DW_TILE)
    tn = min(n, DW_TILE) if n_total is None else DW_PIECE_TILE
    tk = min(s, DW_TOKENS * (2 if (m // tm) * (n // tn) >= DW_MANY_TILES else 1))
    nk = s // tk
    j0 = col0 // tn
    assert m % tm == 0 and n % tn == 0 and col0 % tn == 0

    def body(a_ref, b_ref, *rest):
        o_ref, acc = rest[-2:]
        k = pl.program_id(2)

        @pl.when(k == 0)
        def _():
            acc[...] = jnp.zeros_like(acc)

        acc[...] += _dot_tn(a_ref[...].astype(BF16), b_ref[...].astype(BF16))

        @pl.when(k == nk - 1)
        def _():
            o_ref[...] = acc[...].astype(BF16)

    extra = [] if into is None else [into]
    return pl.pallas_call(
        body, name=name, grid=(m // tm, n // tn, nk),
        in_specs=[pl.BlockSpec((tk, tm), lambda i, j, k: (k, i)), pl.BlockSpec((tk, tn), lambda i, j, k: (k, j))]
        + [ANY] * len(extra),
        out_specs=pl.BlockSpec((tm, tn), lambda i, j, k: (i, j0 + j)),
        out_shape=jax.ShapeDtypeStruct((m, n_total or n), BF16),
        input_output_aliases={2: 0} if extra else {},
        scratch_shapes=[pltpu.VMEM((tm, tn), F32)],
        compiler_params=_params(3),
    )(a, b, *extra)


def _mesh_position():
    return tuple(lax.axis_index(a) for a in MESH_AXES)


def _peer(me, k):
    bits = ((k >> 2) & 1, (k >> 1) & 1, k & 1)
    pos = tuple(1 - m if b else m for m, b in zip(me, bits))
    return pos, 4 * pos[0] + 2 * pos[1] + pos[2]


class _Exchange:
    def __init__(self, arrays, out_shapes, src, dst, relayed=None):
        self.arrays, self.out_shapes, self.src, self.dst = list(arrays), list(out_shapes), src, dst
        self.relayed = list(relayed) if relayed is not None else [False] * len(self.arrays)


_NO_EXCHANGE = _Exchange([], [], None, None)


def _exchange_sems(ex):
    n = len(ex.arrays)
    if n == 0:
        return []
    return [pltpu.SemaphoreType.DMA((n, N_DEV - 1)), pltpu.SemaphoreType.DMA((n, N_DEV - 1)),
            pltpu.SemaphoreType.DMA((n,))]


def _split_refs(rest, ex, n_own_outs):
    n_in, n_out = len(ex.arrays), len(ex.out_shapes)
    ex_in, rest = rest[:n_in], rest[n_in:]
    own, rest = rest[:n_own_outs], rest[n_own_outs:]
    return ex_in, own, rest[:n_out], rest[n_out:]


def _direct_steps(ex, w, in_refs, out_refs, sems):
    send_sems, recv_sems, local_sems = sems
    me = _mesh_position()
    mine = 4 * me[0] + 2 * me[1] + me[2]

    def copy(k):
        landing = ex.dst(w, out_refs, mine)
        if k == 0:
            return pltpu.make_async_copy(ex.src(w, in_refs, mine), landing, local_sems.at[w])
        peer, peer_idx = _peer(me, k)
        return pltpu.make_async_remote_copy(
            src_ref=ex.src(w, in_refs, peer_idx), dst_ref=landing, send_sem=send_sems.at[w, k - 1],
            recv_sem=recv_sems.at[w, k - 1], device_id=peer, device_id_type=pl.DeviceIdType.MESH)

    ks = range(N_DEV)
    return [lambda k=k: copy(k).start() for k in ks], [], [lambda k=k: copy(k).wait() for k in ks]


def _relayed_steps(ex, w, in_refs, out_refs, sems):
    send_sems, recv_sems, local_sems = sems
    x, y, c = _mesh_position()
    chips = [(1 - x, y), (x, 1 - y), (1 - x, 1 - y)]
    sibling = (x, y, 1 - c)
    js = range(len(chips))

    def block(px, py, pc):
        return ex.dst(w, out_refs, 4 * px + 2 * py + pc)

    def copy(k, dst, to, src=None):
        return pltpu.make_async_remote_copy(
            src_ref=ex.src(w, in_refs, None) if src is None else src, dst_ref=dst, send_sem=send_sems.at[w, k],
            recv_sem=recv_sems.at[w, k], device_id=to, device_id_type=pl.DeviceIdType.MESH)

    def local():
        return pltpu.make_async_copy(ex.src(w, in_refs, None), block(x, y, c), local_sems.at[w])

    def own(k):
        return copy(k, block(x, y, c), sibling if k == 0 else (*chips[k - 1], c))

    def came(j):
        return copy(1 + j, block(*chips[j], c), (*chips[j], c))

    def passed(j):
        return copy(4 + j, block(*chips[j], c), sibling, src=block(*chips[j], c))

    def from_sibling(k):
        return copy(k, block(x, y, 1 - c) if k == 0 else block(*chips[k - 4], 1 - c), sibling)

    start = [lambda: local().start()] + [lambda k=k: own(k).start() for k in range(4)]
    relay = [step for j in js for step in (lambda j=j: came(j).wait_recv(), lambda j=j: passed(j).start())]
    finish = ([lambda: local().wait()] + [lambda k=k: own(k).wait_send() for k in range(4)]
              + [lambda j=j: passed(j).wait_send() for j in js]
              + [lambda k=k: from_sibling(k).wait_recv() for k in (0, 4, 5, 6)])
    return start, relay, finish


def _exchange_steps(ex, in_refs, out_refs, sems):
    start, relay, finish = [], [], []
    for w in range(len(ex.arrays)):
        steps = (_relayed_steps if ex.relayed[w] else _direct_steps)(ex, w, in_refs, out_refs, sems)
        start += steps[0]
        relay += steps[1]
        finish += steps[2]
    return start, relay, finish


def _run(steps):
    for step in steps:
        step()


def _at_grid_step(grid, where):
    target = {"first": [0] * len(grid), "middle": [grid[0] // 2] + [0] * (len(grid) - 1),
              "last": [g - 1 for g in grid]}[where]
    hit = pl.program_id(0) == target[0]
    for axis in range(1, len(grid)):
        hit = jnp.logical_and(hit, pl.program_id(axis) == target[axis])
    return hit


def _exchange_start(ex, in_refs, out_refs, sems, grid):
    if ex.arrays:
        @pl.when(_at_grid_step(grid, "first"))
        def _():
            _run(_exchange_steps(ex, in_refs, out_refs, sems)[0])

        if any(ex.relayed):
            assert grid[0] >= 2

            @pl.when(_at_grid_step(grid, "middle"))
            def _():
                _run(_exchange_steps(ex, in_refs, out_refs, sems)[1])


def _exchange_wait(ex, in_refs, out_refs, sems, grid):
    if ex.arrays:
        @pl.when(_at_grid_step(grid, "last"))
        def _():
            _run(_exchange_steps(ex, in_refs, out_refs, sems)[2])


def _shard_block(ref, shard_shape, by_col, idx):
    r, c = shard_shape
    if by_col:
        return ref.at[:, pl.ds(pl.multiple_of(idx * c, LANES), c)]
    return ref.at[pl.ds(pl.multiple_of(idx * r, 16), r), :]


def _full_shape(shard_shape, by_col):
    r, c = shard_shape
    return (r, N_DEV * c) if by_col else (N_DEV * r, c)


def _gather_exchange(shards, col_sharded):
    shapes = [a.shape for a in shards]
    return _Exchange(
        shards, [jax.ShapeDtypeStruct(_full_shape(sh, bc), a.dtype) for a, sh, bc in zip(shards, shapes, col_sharded)],
        lambda w, refs, idx: refs[w],
        lambda w, refs, idx: _shard_block(refs[w], shapes[w], col_sharded[w], idx), [True] * len(shards))


def _scatter_exchange(grads, col_sharded):
    shapes = []
    for g, by_col in zip(grads, col_sharded):
        r, c = g.shape
        shapes.append((r, c // N_DEV) if by_col else (r // N_DEV, c))
    return _Exchange(
        grads, [jax.ShapeDtypeStruct((N_DEV,) + sh, g.dtype) for g, sh in zip(grads, shapes)],
        lambda w, refs, idx: _shard_block(refs[w], shapes[w], col_sharded[w], idx),
        lambda w, refs, mine: refs[w].at[mine])


def _broadcast_exchange(arrays):
    return _Exchange(arrays, [jax.ShapeDtypeStruct((N_DEV,) + a.shape, a.dtype) for a in arrays],
                     lambda w, refs, idx: refs[w], lambda w, refs, mine: refs[w].at[mine])


def _join(*exs):
    arrays, shapes, owner = [], [], []
    for e in exs:
        for w in range(len(e.arrays)):
            owner.append((e, w, len(arrays), len(shapes)))
        arrays += e.arrays
        shapes += e.out_shapes

    def src(w, refs, idx):
        e, w0, i0, _ = owner[w]
        return e.src(w0, refs[i0:i0 + len(e.arrays)], idx)

    def dst(w, refs, idx):
        e, w0, _, o0 = owner[w]
        return e.dst(w0, refs[o0:o0 + len(e.out_shapes)], idx)

    return _Exchange(arrays, shapes, src, dst, [flag for e in exs for flag in e.relayed])


def _exchange_call(ex, name):
    n_in = len(ex.arrays)

    def body(*refs):
        in_refs, _, out_refs, sems = _split_refs(refs, ex, 0)
        for steps in _exchange_steps(ex, in_refs, out_refs, sems):
            _run(steps)

    return pl.pallas_call(
        body, name=name, in_specs=[ANY] * n_in, out_specs=[ANY] * len(ex.out_shapes), out_shape=ex.out_shapes,
        scratch_shapes=_exchange_sems(ex), compiler_params=pltpu.CompilerParams(vmem_limit_bytes=VMEM_LIMIT),
    )(*ex.arrays)


def _to_bf16(arrays):
    def body(*refs):
        for src, dst in zip(refs[:len(arrays)], refs[len(arrays):]):
            dst[...] = src[...].astype(BF16)

    vmem = pl.BlockSpec(memory_space=pltpu.VMEM)
    return pl.pallas_call(
        body, name="weights_to_bf16", in_specs=[vmem] * len(arrays), out_specs=[vmem] * len(arrays),
        out_shape=[jax.ShapeDtypeStruct(a.shape, BF16) for a in arrays],
        compiler_params=pltpu.CompilerParams(vmem_limit_bytes=VMEM_LIMIT),
    )(*arrays)


def _adamw(w, g, m, v):
    m = ADAM_B1 * m + (1.0 - ADAM_B1) * g
    v = ADAM_B2 * v + (1.0 - ADAM_B2) * jnp.square(g)
    m_hat = m / (1.0 - ADAM_B1 ** ADAM_STEP)
    v_hat = v / (1.0 - ADAM_B2 ** ADAM_STEP)
    delta = -ADAM_LR * (m_hat / (jnp.sqrt(v_hat) + ADAM_EPS) + ADAM_WD * w)
    return delta, m, v


def _sum_and_adamw(parts, w, m, v, name):
    r, c = w.shape
    tr = min(r, 256)

    def body(p_ref, w_ref, m_ref, v_ref, g_out, d_out, m_out, v_out):
        g = p_ref[0].astype(F32)
        for dev in range(1, N_DEV):
            g = g + p_ref[dev].astype(F32)
        g_out[...] = g
        d_out[...], m_out[...], v_out[...] = _adamw(w_ref[...], g, m_ref[...], v_ref[...])

    blk = pl.BlockSpec((tr, c), lambda i: (i, 0))
    return pl.pallas_call(
        body, name=name, grid=(r // tr,),
        in_specs=[pl.BlockSpec((N_DEV, tr, c), lambda i: (0, i, 0)), blk, blk, blk],
        out_specs=[blk] * 4, out_shape=[jax.ShapeDtypeStruct((r, c), F32)] * 4,
        compiler_params=_params(1),
    )(parts, w, m, v)


BIG = ("w_in", "w_attn_out", "w_conv_out", "w_o", "w_up", "w_down", "w_ple_gate", "w_ple_proj")
COL_SHARDED = {"w_in": True, "w_attn_out": True, "w_conv_out": True, "w_o": False, "w_up": True, "w_down": False,
               "w_ple_gate": False, "w_ple_proj": True}
SMALL = ("g_pre_mix", "b_gate", "g_post_mix", "g_pre_mlp", "g_post_mlp", "g_ple")


REST = BIG[1:]


def _local_grads(x, h1, p, target, small, wconv, full, aw, cw, tm, t, gather_rest=None, scatter_rest=None,
                 scatter_in=None):
    full = dict(full)
    tm_wide = min(WIDE_BLOCKS * tm, x.shape[0])
    qkv, conv, gate = _in_proj_fwd(h1, small["b_gate"], full["w_in"], aw, cw, tm_wide)
    o, o_b, *rest = _attn_fwd(qkv, aw, t, gather_rest)
    full.update(zip(REST, rest))
    x1, mixed, mix_in, conv_in = _mix_fwd(x, o_b, conv, gate, wconv, small["g_post_mix"], full["w_attn_out"],
                                          full["w_conv_out"], full["w_o"], tm_wide)
    (dx1, h2, du, a, df, h3, ds3, dpp, loss, dg_pre_mlp, dg_post_mlp, dg_ple) = _mlp_ple_loss(
        x1, p, target, small["g_pre_mlp"], small["g_post_mlp"], small["g_ple"], full["w_up"], full["w_down"],
        full["w_ple_gate"], full["w_ple_proj"], tm)
    big = {"w_up": _weight_grad(h2, du, "dw_up"), "w_down": _weight_grad(a, df, "dw_down"),
           "w_ple_gate": _weight_grad(h3, ds3, "dw_ple_gate"), "w_ple_proj": _weight_grad(p, dpp, "dw_ple_proj")}
    (dmixed, dattn, dconvout, do, drest, dg_post_mix, db_gate, dwconv) = _mix_bwd(
        dx1, mixed, o_b, conv, gate, wconv, small["g_post_mix"], full["w_attn_out"], full["w_conv_out"], full["w_o"],
        tm_wide)
    big.update({"w_attn_out": _weight_grad(o_b, dattn, "dw_attn_out"),
                "w_conv_out": _weight_grad(conv_in, dconvout, "dw_conv_out"),
                "w_o": _weight_grad(mix_in, dmixed, "dw_o")})
    dq, dk, dv, *scattered = _attn_bwd(qkv, o, do, aw, t, scatter_rest and scatter_rest([big[n] for n in REST]))
    pieces = [dq, dk, dv, drest]
    dw_in, col0, ni = None, 0, full["w_in"].shape[1]
    for i, piece in enumerate(pieces):
        dw_in = _weight_grad(h1, piece, "dw_in_%d" % i, dw_in, col0, ni)
        col0 += piece.shape[1]
    big["w_in"] = dw_in
    dx, dg_pre_mix, *scattered_in = _in_proj_bwd(x, dx1, pieces, small["g_pre_mix"], full["w_in"], tm_wide,
                                                scatter_in and scatter_in(dw_in))
    small_grads = {"g_pre_mix": dg_pre_mix, "b_gate": db_gate, "g_post_mix": dg_post_mix, "g_pre_mlp": dg_pre_mlp,
                   "g_post_mlp": dg_post_mlp, "g_ple": dg_ple, "w_conv": dwconv}
    return loss[0, 0], dx, big, small_grads, scattered_in + scattered


PACK_ROWS = 16


def _pack_layout(shapes, d):
    slots, at = [], 0
    for i, (r, c) in enumerate(shapes):
        assert d % c == 0
        for row in range(r):
            slots.append((i, row, at // d, at % d))
            at += c
        at = -(-at // d) * d
    assert at <= PACK_ROWS * d
    return slots


def _pack_small(groups, d):
    shapes = [a.shape for a in groups[0]]
    slots = _pack_layout(shapes, d)
    n = len(shapes)

    def body(*refs):
        ins, outs = refs[:n * len(groups)], refs[n * len(groups):]
        for g, out in enumerate(outs):
            out[...] = jnp.zeros_like(out)
            for i, row, pr, pc in slots:
                src = ins[g * n + i]
                out[pr:pr + 1, pc:pc + shapes[i][1]] = src[row:row + 1, :]

    vmem = pl.BlockSpec(memory_space=pltpu.VMEM)
    return pl.pallas_call(
        body, name="pack_small", in_specs=[vmem] * (n * len(groups)), out_specs=[vmem] * len(groups),
        out_shape=[jax.ShapeDtypeStruct((PACK_ROWS, d), F32)] * len(groups),
    )(*[a for group in groups for a in group])


def _unpack_small(pack, shapes, d):
    slots = _pack_layout(shapes, d)
    return [jnp.stack([pack[pr, pc:pc + shapes[i][1]] for j, row, pr, pc in slots if j == i])
            for i in range(len(shapes))]


def kernel(x, p, g_pre_mix, w_in, b_gate, w_conv, w_attn_out, w_conv_out, w_o, g_post_mix, g_pre_mlp, w_up, w_down, g_post_mlp, g_ple, w_ple_gate, w_ple_proj, loss_target, m_g_pre_mix, m_w_in, m_b_gate, m_w_conv, m_w_attn_out, m_w_conv_out, m_w_o, m_g_post_mix, m_g_pre_mlp, m_w_up, m_w_down, m_g_post_mlp, m_g_ple, m_w_ple_gate, m_w_ple_proj, v_g_pre_mix, v_w_in, v_b_gate, v_w_conv, v_w_attn_out, v_w_conv_out, v_w_o, v_g_post_mix, v_g_pre_mlp, v_w_up, v_w_down, v_g_post_mlp, v_g_ple, v_w_ple_gate, v_w_ple_proj):
    given = dict(locals())
    order = ["g_pre_mix", "w_in", "b_gate", "w_conv", "w_attn_out", "w_conv_out", "w_o", "g_post_mix", "g_pre_mlp",
             "w_up", "w_down", "g_post_mlp", "g_ple", "w_ple_gate", "w_ple_proj"]
    d = x.shape[-1]
    me = 4 * lax.axis_index("x") + 2 * lax.axis_index("y") + lax.axis_index("c")

    col = [COL_SHARDED[n] for n in BIG]
    shards = _to_bf16([given[n][0] for n in BIG])
    cw_shard = w_conv.shape[-1]
    conv_tile = jnp.pad(w_conv[0], ((0, HALO - CONV_K), (0, LANES - cw_shard)))
    h1, w_in_full, conv_g = _pre_norm(
        x[0], g_pre_mix, min(WIDE_BLOCKS * ROW_BLOCK, x.shape[1]),
        _join(_gather_exchange(shards[:1], col[:1]), _broadcast_exchange([conv_tile])))
    wconv = jnp.concatenate([conv_g[dev, :CONV_K, :cw_shard] for dev in range(N_DEV)], axis=1)

    small = {n: given[n] for n in SMALL}
    loss, dx, big_grads, small_grads, parts = _local_grads(
        x[0], h1, p[0, 0], loss_target[0], small, wconv, {"w_in": w_in_full}, w_attn_out.shape[1], w_conv_out.shape[1],
        ROW_BLOCK, ATTN_BLOCK,
        _gather_exchange(shards[1:], col[1:]), lambda grads: _scatter_exchange(grads, col[1:]),
        lambda grad: _scatter_exchange([grad], col[:1]))
    small_names = list(SMALL) + ["w_conv"]
    two_d = lambda a: a.reshape(-1, d) if a.shape[-1] > d else a.reshape(-1, a.shape[-1])
    full_conv = lambda a: lax.dynamic_update_slice(jnp.zeros((CONV_K, N_DEV * cw_shard), F32), a[0],
                                                   (jnp.int32(0), me * cw_shard))
    groups = [[two_d(small_grads[n]) for n in small_names] + [loss.reshape(1, 1)]]
    for pre in ("", "m_", "v_"):
        groups.append([two_d(given[pre + n]) for n in SMALL] + [full_conv(given[pre + "w_conv"]), jnp.zeros((1, 1), F32)])
    pack, *state = _pack_small(groups, d)
    packs, = _exchange_call(_broadcast_exchange([pack]), "share_small_grads")

    grads, deltas, new_m, new_v = {}, {}, {}, {}
    for n, part in zip(BIG, parts):
        grads[n], deltas[n], new_m[n], new_v[n] = (
            a[None] for a in _sum_and_adamw(part, given[n][0], given["m_" + n][0], given["v_" + n][0], "adamw_" + n))

    outs = _sum_and_adamw(packs, *state, "adamw_small")
    shapes = [a.shape for a in groups[0]]
    for res, dst in zip(outs, (grads, deltas, new_m, new_v)):
        for n, a in zip(small_names + ["loss"], _unpack_small(res, shapes, d)):
            if n == "w_conv":
                a = lax.dynamic_slice(a, (jnp.int32(0), me * cw_shard), (CONV_K, cw_shard))[None]
            dst[n] = a.reshape(given[n].shape) if n in SMALL else a
    loss = grads["loss"][0, 0]

    return (loss, dx[None], *[grads[n] for n in order], *[deltas[n] for n in order],
            *[new_m[n] for n in order], *[new_v[n] for n in order])
```

```python
import jax
import jax.numpy as jnp
from jax import lax
from jax.experimental import pallas as pl
from jax.experimental.pallas import tpu as pltpu

F32 = jnp.float32
BF16 = jnp.bfloat16
RMS_EPS = 1e-6
N_DEV = 8
MESH_AXES = ("x", "y", "c")
LANES = 128
HEAD_DIM = 64
HEADS_PER_GROUP = LANES // HEAD_DIM
CONV_K = 3
HALO = 8
HALO_BF16 = 16
VMEM_LIMIT = 56 * 1024 * 1024
EXP2_ZERO = -150.0
LOG2_E = 1.4426950408889634

ADAM_LR = 0.001
ADAM_B1 = 0.9
ADAM_B2 = 0.999
ADAM_EPS = 1e-08
ADAM_WD = 0.01
ADAM_STEP = 10

ROW_BLOCK = 256
WIDE_BLOCKS = 2
ATTN_BLOCK = 256
ATTN_ROW_SPLITS = 2
DW_TOKENS = 2048
DW_TILE = 1024
DW_MANY_TILES = 6
DW_PIECE_TILE = 512
FF_CHUNK = 1024
PROJ_CHUNK = 512


def _dot(a, b):
    return lax.dot_general(a, b, (((1,), (0,)), ((), ())), preferred_element_type=F32)


def _dot_nt(a, b):
    return lax.dot_general(a, b, (((1,), (1,)), ((), ())), preferred_element_type=F32)


def _dot_tn(a, b):
    return lax.dot_general(a, b, (((0,), (0,)), ((), ())), preferred_element_type=F32)


def _sigmoid(z):
    return 1.0 / (1.0 + jnp.exp(-z))


def _rms_scale(x):
    return lax.rsqrt(jnp.mean(x * x, axis=-1, keepdims=True) + RMS_EPS)


def _rms_bwd(xhat, r, g, dy):
    gd = dy * g
    return r * (gd - xhat * jnp.mean(gd * xhat, axis=-1, keepdims=True))


def _params(n_axes, **kw):
    return pltpu.CompilerParams(dimension_semantics=("arbitrary",) * n_axes, vmem_limit_bytes=VMEM_LIMIT, **kw)


def _load_resident(pairs, sem):
    @pl.when(pl.program_id(0) == 0)
    def _():
        copies = [pltpu.make_async_copy(src, dst, sem.at[i]) for i, (src, dst) in enumerate(pairs)]
        for cp in copies:
            cp.start()
        for cp in copies:
            cp.wait()


def _row_spec(tm, width):
    return pl.BlockSpec((tm, width), lambda i: (i, 0))


def _prev_halo_spec(tm, width, rows):
    per = tm // rows
    return pl.BlockSpec((rows, width), lambda i: (jnp.maximum(i * per - 1, 0), 0))


def _const_spec(shape):
    return pl.BlockSpec(shape, lambda i: (0,) * len(shape))


ANY = pl.BlockSpec(memory_space=pl.ANY)


def _shift_down(cur, prev, n):
    rows = lax.broadcasted_iota(jnp.int32, cur.shape, 0)
    out = pltpu.roll(cur, n, 0)
    for j in range(n):
        out = jnp.where(rows == j, prev[prev.shape[0] - n + j:prev.shape[0] - n + j + 1, :], out)
    return out


def _shift_up(cur, nxt, n):
    tm = cur.shape[0]
    rows = lax.broadcasted_iota(jnp.int32, cur.shape, 0)
    out = pltpu.roll(cur, tm - n, 0)
    for j in range(n):
        out = jnp.where(rows == tm - n + j, nxt[j:j + 1, :], out)
    return out


def _conv_taps(cm, cm_prev, wconv):
    cm1 = _shift_down(cm, cm_prev, 1)
    cm2 = _shift_down(cm, cm_prev, 2)
    cv = wconv[2:3, :] * cm + wconv[1:2, :] * cm1 + wconv[0:1, :] * cm2
    return cv, cm1, cm2


def _in_proj_fwd(x, g1, b_gate, w_in, aw, cw, tm):
    s, d = x.shape
    ni = w_in.shape[1]
    n_qkv, n_conv = 3 * aw, 3 * cw
    ch = PROJ_CHUNK

    def body(x_ref, g_ref, b_ref, w_hbm, qkv_ref, conv_ref, gate_ref, h_ref, w_vmem, sem):
        _load_resident([(w_hbm, w_vmem)], sem)
        xv = x_ref[...]
        h = (xv * _rms_scale(xv) * g_ref[...]).astype(BF16)
        h_ref[...] = h
        for c0 in range(0, ni, ch):
            pc = _dot(h, w_vmem[:, c0:c0 + ch])
            if c0 < n_qkv:
                qkv_ref[:, c0:c0 + ch] = pc.astype(BF16)
            elif c0 < n_qkv + n_conv:
                conv_ref[:, c0 - n_qkv:c0 - n_qkv + ch] = pc.astype(BF16)
            else:
                g0 = c0 - n_qkv - n_conv
                gate_ref[:, g0:g0 + ch] = _sigmoid(pc + b_ref[:, g0:g0 + ch]).astype(BF16)

    return pl.pallas_call(
        body, name="in_proj_fwd", grid=(s // tm,),
        in_specs=[_row_spec(tm, d), _const_spec((1, d)), _const_spec((1, 2 * d)), ANY],
        out_specs=[_row_spec(tm, n_qkv), _row_spec(tm, n_conv), _row_spec(tm, 2 * d), _row_spec(tm, d)],
        out_shape=[jax.ShapeDtypeStruct((s, n_qkv), BF16), jax.ShapeDtypeStruct((s, n_conv), BF16),
                   jax.ShapeDtypeStruct((s, 2 * d), BF16), jax.ShapeDtypeStruct((s, d), BF16)],
        scratch_shapes=[pltpu.VMEM((d, ni), BF16), pltpu.SemaphoreType.DMA((1,))],
        compiler_params=_params(1),
    )(x, g1, b_gate, w_in)


def _split_hi_lo(a):
    hi = a.astype(BF16)
    return hi, (a - hi.astype(F32)).astype(BF16)


def _log2_gates(z):
    z2 = z * LOG2_E
    nz2 = -z2
    log_keep = jnp.minimum(nz2, 0.0) - jnp.log2(1.0 + jnp.exp2(jnp.minimum(z2, nz2)))
    return log_keep + z2, log_keep


def _attn_masks(t):
    row = lax.broadcasted_iota(jnp.int32, (t, t), 0)
    col = lax.broadcasted_iota(jnp.int32, (t, t), 1)
    return (col < row).astype(F32), (row > col).astype(BF16), (row >= col).astype(BF16)


def _chains(a):
    tr = a.shape[0] // ATTN_ROW_SPLITS
    return [jnp.where(_head_lanes(h), a[r * tr:(r + 1) * tr], jnp.zeros((tr, LANES), a.dtype))
            for h in range(HEADS_PER_GROUP) for r in range(ATTN_ROW_SPLITS)]


def _merge_chains(parts):
    rows = []
    for r in range(ATTN_ROW_SPLITS):
        out = parts[r]
        for h in range(1, HEADS_PER_GROUP):
            out = jnp.where(_head_lanes(h), parts[h * ATTN_ROW_SPLITS + r], out)
        rows.append(out)
    return jnp.concatenate(rows, axis=0)


def _by_stage(n_chains, stages):
    for stage in stages:
        for c in range(n_chains):
            stage(c)


def _row_parts(a):
    tr = a.shape[0] // ATTN_ROW_SPLITS
    return [a[r * tr:(r + 1) * tr] for r in range(ATTN_ROW_SPLITS)]


def _while_weights_live(qi, block, carry):
    def cond(state):
        j, carry = state
        live = jnp.max(carry[0][0])
        for run in carry[0][1:]:
            live = jnp.maximum(live, jnp.max(run))
        return jnp.logical_and(j < qi, live >= EXP2_ZERO)

    def step(state):
        j, carry = state
        return j + 1, block(qi - 1 - j, carry)

    return lax.while_loop(cond, step, (jnp.int32(0), carry))[1]


def _head_lanes(h):
    lane = lax.broadcasted_iota(jnp.int32, (1, LANES), 1)
    return (lane >= HEAD_DIM * h) & (lane < HEAD_DIM * (h + 1))


def _attn_fwd(qkv, aw, t, exchange=None):
    s = qkv.shape[0]
    groups = aw // LANES
    nq = s // t
    scale = HEAD_DIM ** -0.5
    ex = exchange or _NO_EXCHANGE
    causal, upper, _ = _attn_masks(t)
    mask_spec = pl.BlockSpec((t, t), lambda g, i: (0, 0))

    def body(q_ref, k_ref, v_ref, causal_ref, upper_ref, *rest):
        ex_in, (o_ref, ob_ref), ex_out, sems = _split_refs(rest, ex, 2)
        qi = pl.program_id(1)
        _exchange_start(ex, ex_in, ex_out, sems, (groups, nq))
        upper = upper_ref[...]
        causal = _row_parts(causal_ref[...] > 0.5) * HEADS_PER_GROUP
        qs = _chains(q_ref[...] * scale)
        heads = range(len(qs))
        tr = t // ATTN_ROW_SPLITS

        def block(kb, runs, accs, diag):
            rows = pl.ds(pl.multiple_of(kb * t, t), t)
            k = k_ref[rows, :]
            v = v_ref[rows, :]
            ncs = [(h % ATTN_ROW_SPLITS + 1) * tr if diag else t for h in heads]
            live = [{} for _ in heads]
            new_runs, new_accs = [None] * len(heads), [None] * len(heads)

            def scores(h):
                live[h]["z"] = _dot_nt(qs[h], k[0:ncs[h]])

            def gates(h):
                nc = ncs[h]
                log_b, log_keep = _log2_gates(live[h].pop("z"))
                if diag:
                    log_keep = jnp.where(causal[h][:, 0:nc], log_keep, 0.0)
                hi, lo = _split_hi_lo(log_keep)
                live[h]["log_w"] = log_b + runs[h]
                live[h]["between"] = _dot(hi, upper[0:nc, 0:nc]) + _dot(lo, upper[0:nc, 0:nc])
                new_runs[h] = runs[h] + jnp.sum(log_keep, axis=1, keepdims=True)

            def weights(h):
                nc = ncs[h]
                w = jnp.exp2(live[h].pop("log_w") + live[h].pop("between"))
                if diag:
                    w = jnp.where(causal[h][:, 0:nc], w, 0.0)
                new_accs[h] = accs[h] + _dot(w.astype(BF16), v[0:nc])

            _by_stage(len(heads), [scores, gates, weights])
            return tuple(new_runs), tuple(new_accs)

        carry = block(qi, [jnp.zeros((tr, 1), F32)] * len(heads), [jnp.zeros((tr, LANES), F32)] * len(heads), True)
        _, accs = _while_weights_live(qi, lambda kb, carry: block(kb, *carry, False), carry)
        o = _merge_chains(accs)
        o_ref[...] = o
        ob_ref[...] = o.astype(BF16)
        _exchange_wait(ex, ex_in, ex_out, sems, (groups, nq))

    return pl.pallas_call(
        body, name="attn_fwd", grid=(groups, nq),
        in_specs=[pl.BlockSpec((t, LANES), lambda g, i: (i, g)),
                  pl.BlockSpec((s, LANES), lambda g, i: (0, groups + g)),
                  pl.BlockSpec((s, LANES), lambda g, i: (0, 2 * groups + g)), mask_spec, mask_spec]
        + [ANY] * len(ex.arrays),
        out_specs=[pl.BlockSpec((t, LANES), lambda g, i: (i, g))] * 2 + [ANY] * len(ex.out_shapes),
        out_shape=[jax.ShapeDtypeStruct((s, aw), F32), jax.ShapeDtypeStruct((s, aw), BF16)] + ex.out_shapes,
        scratch_shapes=_exchange_sems(ex),
        compiler_params=_params(2),
    )(qkv, qkv, qkv, causal, upper, *ex.arrays)


def _attn_bwd(qkv, o, do, aw, t, exchange=None):
    s = qkv.shape[0]
    groups = aw // LANES
    nq = s // t
    scale = HEAD_DIM ** -0.5
    ex = exchange or _NO_EXCHANGE

    def body(q_ref, k_ref, v_ref, o_ref, do_ref, causal_ref, upper_ref, lower_ref, *rest):
        ex_in, (dq_ref, dk_ref, dv_ref), ex_out, (dk_acc, dv_acc, *sems) = _split_refs(rest, ex, 3)
        qi = pl.program_id(1)
        _exchange_start(ex, ex_in, ex_out, sems, (groups, nq))

        @pl.when(qi == 0)
        def _():
            dk_acc[...] = jnp.zeros_like(dk_acc)
            dv_acc[...] = jnp.zeros_like(dv_acc)

        upper = upper_ref[...]
        lower_incl = lower_ref[...]
        causal = _row_parts(causal_ref[...] > 0.5) * HEADS_PER_GROUP
        q = q_ref[...] * scale
        do_b = do_ref[...]
        qs = _chains(q)
        dos = _chains(do_b)
        qs_all = jnp.concatenate(qs, axis=0)
        dos_all = jnp.concatenate(dos, axis=0)
        e_totals = [jnp.sum(part, axis=1, keepdims=True) for part in _chains(do_b.astype(F32) * o_ref[...])]
        heads = range(len(qs))
        tr = t // ATTN_ROW_SPLITS

        def block(kb, runs, e_runs, dqs, diag):
            rows = pl.ds(pl.multiple_of(kb * t, t), t)
            k = k_ref[rows, :]
            v = v_ref[rows, :]
            ncs = [(h % ATTN_ROW_SPLITS + 1) * tr if diag else t for h in heads]
            live = [{} for _ in heads]
            none = [None] * len(heads)
            new_runs, new_e_runs, new_dqs, dzbs, wbs = list(none), list(none), list(none), list(none), list(none)

            def scores(h):
                live[h]["z"] = _dot_nt(qs[h], k[0:ncs[h]])
                live[h]["dw"] = _dot_nt(dos[h], v[0:ncs[h]])

            def gates(h):
                nc = ncs[h]
                log_b, log_keep = _log2_gates(live[h].pop("z"))
                live[h]["beta"] = jnp.exp2(log_b)
                live[h]["keep"] = jnp.exp2(log_keep)
                if diag:
                    log_keep = jnp.where(causal[h][:, 0:nc], log_keep, 0.0)
                hi, lo = _split_hi_lo(log_keep)
                live[h]["log_w"] = log_b + runs[h]
                live[h]["between"] = _dot(hi, upper[0:nc, 0:nc]) + _dot(lo, upper[0:nc, 0:nc])
                new_runs[h] = runs[h] + jnp.sum(log_keep, axis=1, keepdims=True)

            def weights(h):
                nc = ncs[h]
                w = jnp.exp2(live[h].pop("log_w") + live[h].pop("between"))
                if diag:
                    w = jnp.where(causal[h][:, 0:nc], w, 0.0)
                wb = w.astype(BF16)
                e = live[h].pop("dw") * wb.astype(F32)
                hi, lo = _split_hi_lo(e)
                live[h]["e"] = e
                live[h]["e_suffix"] = _dot(hi, lower_incl[0:nc, 0:nc]) + _dot(lo, lower_incl[0:nc, 0:nc]) + e_runs[h]
                wbs[h] = wb

            def score_grads(h):
                nc = ncs[h]
                e_suffix = live[h].pop("e_suffix")
                dz = live[h].pop("e") * live[h].pop("keep") - (e_totals[h] - e_suffix) * live[h].pop("beta")
                if diag:
                    dz = jnp.where(causal[h][:, 0:nc], dz, 0.0)
                dzb = dz.astype(BF16)
                new_dqs[h] = dqs[h] + _dot(dzb, k[0:nc])
                new_e_runs[h] = e_suffix[:, 0:1]
                if nc < t:
                    unseen = jnp.zeros((tr, t - nc), BF16)
                    dzb = jnp.concatenate([dzb, unseen], axis=1)
                    wbs[h] = jnp.concatenate([wbs[h], unseen], axis=1)
                dzbs[h] = dzb

            _by_stage(len(heads), [scores, gates, weights, score_grads])
            dk_acc[rows, :] += _dot_tn(jnp.concatenate(dzbs, axis=0), qs_all)
            dv_acc[rows, :] += _dot_tn(jnp.concatenate(wbs, axis=0), dos_all)
            return tuple(new_runs), tuple(new_e_runs), tuple(new_dqs)

        zero_cols = [jnp.zeros((tr, 1), F32)] * len(heads)
        carry = block(qi, zero_cols, zero_cols, [jnp.zeros((tr, LANES), F32)] * len(heads), True)
        _, _, dqs = _while_weights_live(qi, lambda kb, carry: block(kb, *carry, False), carry)
        dq_ref[...] = (_merge_chains(dqs) * scale).astype(BF16)

        @pl.when(qi == nq - 1)
        def _():
            dk_ref[...] = dk_acc[...].astype(BF16)
            dv_ref[...] = dv_acc[...].astype(BF16)

        _exchange_wait(ex, ex_in, ex_out, sems, (groups, nq))

    blk = pl.BlockSpec((t, LANES), lambda g, i: (i, g))
    slab = pl.BlockSpec((s, LANES), lambda g, i: (0, g))
    mask_spec = pl.BlockSpec((t, t), lambda g, i: (0, 0))
    return pl.pallas_call(
        body, name="attn_bwd", grid=(groups, nq),
        in_specs=[blk, pl.BlockSpec((s, LANES), lambda g, i: (0, groups + g)),
                  pl.BlockSpec((s, LANES), lambda g, i: (0, 2 * groups + g)), blk, blk, mask_spec, mask_spec, mask_spec]
        + [ANY] * len(ex.arrays),
        out_specs=[blk, slab, slab] + [ANY] * len(ex.out_shapes),
        out_shape=[jax.ShapeDtypeStruct((s, aw), BF16)] * 3 + ex.out_shapes,
        scratch_shapes=[pltpu.VMEM((s, LANES), F32), pltpu.VMEM((s, LANES), F32)] + _exchange_sems(ex),
        compiler_params=_params(2),
    )(qkv, qkv, qkv, o, do, *_attn_masks(t), *ex.arrays)


def _branches(o_b, conv, conv_prev, wconv, w_ao, w_co, cw, first):
    conv = conv.astype(F32)
    conv_prev = conv_prev.astype(F32)
    cb = conv[:, 0:cw]
    cm = conv[:, cw:2 * cw] * conv[:, 2 * cw:3 * cw]
    cm_prev = conv_prev[:, cw:2 * cw] * conv_prev[:, 2 * cw:3 * cw]
    cm_prev = jnp.where(first, 0.0, cm_prev)
    cv, cm1, cm2 = _conv_taps(cm, cm_prev, wconv)
    conv_in = (cb * cv).astype(BF16)
    return _dot(o_b, w_ao), _dot(conv_in, w_co), conv_in, cb, cv, cm, cm1, cm2


def _mix_fwd(x, o, conv, gate, wconv, g_post, w_ao, w_co, w_o, tm):
    s, d = x.shape
    aw, cw = w_ao.shape[0], w_co.shape[0]

    def body(x_ref, o_ref, conv_ref, prev_ref, gate_ref, wc_ref, g_ref, wao_hbm, wco_hbm, wo_hbm,
             x1_ref, mixed_ref, mixin_ref, convin_ref, wao, wco, wo, sem):
        _load_resident([(wao_hbm, wao), (wco_hbm, wco), (wo_hbm, wo)], sem)
        y_attn, y_conv, conv_in, *_ = _branches(
            o_ref[...].astype(BF16), conv_ref[...], prev_ref[...], wc_ref[...], wao[...], wco[...], cw,
            pl.program_id(0) == 0)
        mix_in = (gate_ref[:, 0:d].astype(F32) * y_attn + gate_ref[:, d:2 * d].astype(F32) * y_conv).astype(BF16)
        mixed = _dot(mix_in, wo[...])
        x1_ref[...] = x_ref[...] + mixed * _rms_scale(mixed) * g_ref[...]
        mixed_ref[...] = mixed
        mixin_ref[...] = mix_in
        convin_ref[...] = conv_in

    return pl.pallas_call(
        body, name="mix_fwd", grid=(s // tm,),
        in_specs=[_row_spec(tm, d), _row_spec(tm, aw), _row_spec(tm, 3 * cw), _prev_halo_spec(tm, 3 * cw, HALO_BF16),
                  _row_spec(tm, 2 * d), _const_spec((CONV_K, cw)), _const_spec((1, d)), ANY, ANY, ANY],
        out_specs=[_row_spec(tm, d), _row_spec(tm, d), _row_spec(tm, d), _row_spec(tm, cw)],
        out_shape=[jax.ShapeDtypeStruct((s, d), F32), jax.ShapeDtypeStruct((s, d), F32),
                   jax.ShapeDtypeStruct((s, d), BF16), jax.ShapeDtypeStruct((s, cw), BF16)],
        scratch_shapes=[pltpu.VMEM(w_ao.shape, BF16), pltpu.VMEM(w_co.shape, BF16), pltpu.VMEM(w_o.shape, BF16),
                        pltpu.SemaphoreType.DMA((3,))],
        compiler_params=_params(1),
    )(x, o, conv, conv, gate, wconv, g_post, w_ao, w_co, w_o)


def _mix_bwd(dx1, mixed, o, conv, gate, wconv, g_post, w_ao, w_co, w_o, tm):
    s, d = dx1.shape
    aw, cw = w_ao.shape[0], w_co.shape[0]
    n = s // tm
    per = tm // HALO_BF16

    def body(dx1_ref, mixed_ref, o_ref, conv_ref, prev_ref, gate_ref, wc_ref, g_ref, wao_hbm, wco_hbm, wo_hbm,
             dmixed_ref, dattn_ref, dconvout_ref, do_ref, drest_ref, dg_ref, dbias_ref, dwc_ref,
             wao, wco, wo, dcv_next, sem):
        i = pl.program_id(0)
        _load_resident([(wao_hbm, wao), (wco_hbm, wco), (wo_hbm, wo)], sem)

        @pl.when(i == 0)
        def _():
            dg_ref[...] = jnp.zeros_like(dg_ref)
            dbias_ref[...] = jnp.zeros_like(dbias_ref)
            dwc_ref[...] = jnp.zeros_like(dwc_ref)
            dcv_next[...] = jnp.zeros_like(dcv_next)

        mixed = mixed_ref[...]
        r = _rms_scale(mixed)
        mhat = mixed * r
        dn = dx1_ref[...]
        dg_ref[...] += jnp.sum(dn * mhat, axis=0, keepdims=True)
        dmixed = _rms_bwd(mhat, r, g_ref[...], dn).astype(BF16)
        dmixed_ref[...] = dmixed
        dmi = _dot_nt(dmixed, wo[...])

        wc = wc_ref[...]
        conv = conv_ref[...].astype(F32)
        y_attn, y_conv, _, cb, cv, cm, cm1, cm2 = _branches(
            o_ref[...].astype(BF16), conv, prev_ref[...], wc, wao[...], wco[...], cw, i == n - 1)
        ga = gate_ref[:, 0:d].astype(F32)
        gc = gate_ref[:, d:2 * d].astype(F32)
        dpre_a = dmi * y_attn * ga * (1.0 - ga)
        dpre_c = dmi * y_conv * gc * (1.0 - gc)
        drest_ref[:, 3 * cw:3 * cw + d] = dpre_a.astype(BF16)
        drest_ref[:, 3 * cw + d:3 * cw + 2 * d] = dpre_c.astype(BF16)
        dbias_ref[:, 0:d] += jnp.sum(dpre_a, axis=0, keepdims=True)
        dbias_ref[:, d:2 * d] += jnp.sum(dpre_c, axis=0, keepdims=True)

        dattn = (dmi * ga).astype(BF16)
        dattn_ref[...] = dattn
        do_ref[...] = _dot_nt(dattn, wao[...]).astype(BF16)
        dconvout = (dmi * gc).astype(BF16)
        dconvout_ref[...] = dconvout
        dconv_in = _dot_nt(dconvout, wco[...])
        drest_ref[:, 0:cw] = (dconv_in * cv).astype(BF16)

        dcv = dconv_in * cb
        following = dcv_next[...]
        dcm = wc[2:3, :] * dcv + wc[1:2, :] * _shift_up(dcv, following, 1) + wc[0:1, :] * _shift_up(dcv, following, 2)
        drest_ref[:, cw:2 * cw] = (dcm * conv[:, 2 * cw:3 * cw]).astype(BF16)
        drest_ref[:, 2 * cw:3 * cw] = (dcm * conv[:, cw:2 * cw]).astype(BF16)
        for tap, shifted in enumerate((cm2, cm1, cm)):
            dwc_ref[tap:tap + 1, :] += jnp.sum(dcv * shifted, axis=0, keepdims=True)
        dcv_next[...] = dcv[0:HALO, :]

    def rows(width):
        return pl.BlockSpec((tm, width), lambda i: (n - 1 - i, 0))

    prev_halo = pl.BlockSpec((HALO_BF16, 3 * cw), lambda i: (jnp.maximum((n - 1 - i) * per - 1, 0), 0))
    n_rest = 3 * cw + 2 * d
    return pl.pallas_call(
        body, name="mix_bwd", grid=(n,),
        in_specs=[rows(d), rows(d), rows(aw), rows(3 * cw), prev_halo, rows(2 * d), _const_spec((CONV_K, cw)),
                  _const_spec((1, d)), ANY, ANY, ANY],
        out_specs=[rows(d), rows(d), rows(d), rows(aw), rows(n_rest), _const_spec((1, d)), _const_spec((1, 2 * d)),
                   _const_spec((CONV_K, cw))],
        out_shape=[jax.ShapeDtypeStruct((s, d), BF16), jax.ShapeDtypeStruct((s, d), BF16),
                   jax.ShapeDtypeStruct((s, d), BF16), jax.ShapeDtypeStruct((s, aw), BF16),
                   jax.ShapeDtypeStruct((s, n_rest), BF16), jax.ShapeDtypeStruct((1, d), F32),
                   jax.ShapeDtypeStruct((1, 2 * d), F32), jax.ShapeDtypeStruct((CONV_K, cw), F32)],
        scratch_shapes=[pltpu.VMEM(w_ao.shape, BF16), pltpu.VMEM(w_co.shape, BF16), pltpu.VMEM(w_o.shape, BF16),
                        pltpu.VMEM((HALO, cw), F32), pltpu.SemaphoreType.DMA((3,))],
        compiler_params=_params(1),
    )(dx1, mixed, o, conv, conv, gate, wconv, g_post, w_ao, w_co, w_o)


def _mlp_ple_loss(x1, p, target, g_pre, g_post, g_ple, w_up, w_dn, w_pg, w_pp, tm):
    s, d = x1.shape
    ff = w_up.shape[1]
    pd = p.shape[1]
    fc = FF_CHUNK

    def body(x1_ref, p_ref, t_ref, gpre_ref, gpost_ref, gple_ref, wup_hbm, wdn_hbm, wpg_hbm, wpp_hbm,
             dx1_ref, h2_ref, du_ref, a_ref, df_ref, h3_ref, ds3_ref, dpp_ref, loss_ref, dgpre_ref, dgpost_ref,
             dgple_ref, wup, wdn, wpg, wpp, u_scr, sem):
        _load_resident([(wup_hbm, wup), (wdn_hbm, wdn), (wpg_hbm, wpg), (wpp_hbm, wpp)], sem)

        @pl.when(pl.program_id(0) == 0)
        def _():
            for ref in (loss_ref, dgpre_ref, dgpost_ref, dgple_ref):
                ref[...] = jnp.zeros_like(ref)

        x1v = x1_ref[...]
        r2 = _rms_scale(x1v)
        x1hat = x1v * r2
        h2 = (x1hat * gpre_ref[...]).astype(BF16)
        h2_ref[...] = h2
        f = jnp.zeros((tm, d), F32)
        for c0 in range(0, ff, fc):
            u = _dot(h2, wup[:, c0:c0 + fc])
            u_scr[:, c0:c0 + fc] = u
            a = jnp.square(jnp.maximum(u, 0.0)).astype(BF16)
            a_ref[:, c0:c0 + fc] = a
            f = f + _dot(a, wdn[c0:c0 + fc, :])
        rf = _rms_scale(f)
        fhat = f * rf
        x2 = x1v + fhat * gpost_ref[...]
        r3 = _rms_scale(x2)
        x2hat = x2 * r3
        h3 = (x2hat * gple_ref[...]).astype(BF16)
        h3_ref[...] = h3
        pg = _sigmoid(_dot(h3, wpg[...]))
        pp = _dot(p_ref[...].astype(BF16), wpp[...])
        diff = x2 + pg * pp - t_ref[...]
        loss_ref[...] += 0.5 * jnp.sum(jnp.mean(diff * diff, axis=-1, keepdims=True), axis=0, keepdims=True)

        dy = diff * (1.0 / d)
        dpp_ref[...] = (dy * pg).astype(BF16)
        ds3 = (dy * pp * pg * (1.0 - pg)).astype(BF16)
        ds3_ref[...] = ds3
        dh3 = _dot_nt(ds3, wpg[...])
        dgple_ref[...] += jnp.sum(dh3 * x2hat, axis=0, keepdims=True)
        dx2 = dy + _rms_bwd(x2hat, r3, gple_ref[...], dh3)
        dgpost_ref[...] += jnp.sum(dx2 * fhat, axis=0, keepdims=True)
        df = _rms_bwd(fhat, rf, gpost_ref[...], dx2).astype(BF16)
        df_ref[...] = df
        dh2 = jnp.zeros((tm, d), F32)
        for c0 in range(0, ff, fc):
            da = _dot_nt(df, wdn[c0:c0 + fc, :])
            du = (da * (2.0 * jnp.maximum(u_scr[:, c0:c0 + fc], 0.0))).astype(BF16)
            du_ref[:, c0:c0 + fc] = du
            dh2 = dh2 + _dot_nt(du, wup[:, c0:c0 + fc])
        dgpre_ref[...] += jnp.sum(dh2 * x1hat, axis=0, keepdims=True)
        dx1_ref[...] = dx2 + _rms_bwd(x1hat, r2, gpre_ref[...], dh2)

    vec = _const_spec((1, d))
    return pl.pallas_call(
        body, name="mlp_ple_loss", grid=(s // tm,),
        in_specs=[_row_spec(tm, d), _row_spec(tm, pd), _row_spec(tm, d), vec, vec, vec, ANY, ANY, ANY, ANY],
        out_specs=[_row_spec(tm, d), _row_spec(tm, d), _row_spec(tm, ff), _row_spec(tm, ff), _row_spec(tm, d),
                   _row_spec(tm, d), _row_spec(tm, d), _row_spec(tm, d), _const_spec((1, 1)), vec, vec, vec],
        out_shape=[jax.ShapeDtypeStruct((s, d), F32), jax.ShapeDtypeStruct((s, d), BF16),
                   jax.ShapeDtypeStruct((s, ff), BF16), jax.ShapeDtypeStruct((s, ff), BF16),
                   jax.ShapeDtypeStruct((s, d), BF16), jax.ShapeDtypeStruct((s, d), BF16),
                   jax.ShapeDtypeStruct((s, d), BF16), jax.ShapeDtypeStruct((s, d), BF16),
                   jax.ShapeDtypeStruct((1, 1), F32), jax.ShapeDtypeStruct((1, d), F32),
                   jax.ShapeDtypeStruct((1, d), F32), jax.ShapeDtypeStruct((1, d), F32)],
        scratch_shapes=[pltpu.VMEM(w_up.shape, BF16), pltpu.VMEM(w_dn.shape, BF16), pltpu.VMEM(w_pg.shape, BF16),
                        pltpu.VMEM(w_pp.shape, BF16), pltpu.VMEM((tm, ff), F32), pltpu.SemaphoreType.DMA((4,))],
        compiler_params=_params(1),
    )(x1, p, target, g_pre, g_post, g_ple, w_up, w_dn, w_pg, w_pp)


def _in_proj_bwd(x, dx1, pieces, g1, w_in, tm, exchange=None):
    s, d = x.shape
    ni = w_in.shape[1]
    widths = [p.shape[1] for p in pieces]
    grid = (s // tm,)
    ex = exchange or _NO_EXCHANGE

    def body(x_ref, dx1_ref, *rest):
        piece_refs, rest = rest[:len(pieces)], rest[len(pieces):]
        g_ref, w_hbm = rest[0], rest[1]
        ex_in, (dx_ref, dg_ref), ex_out, (w_vmem, sem, *sems) = _split_refs(rest[2:], ex, 2)
        _exchange_start(ex, ex_in, ex_out, sems, grid)
        _load_resident([(w_hbm, w_vmem)], sem)

        @pl.when(pl.program_id(0) == 0)
        def _():
            dg_ref[...] = jnp.zeros_like(dg_ref)

        dh = jnp.zeros((tm, d), F32)
        c0 = 0
        for ref, width in zip(piece_refs, widths):
            dh = dh + _dot_nt(ref[...], w_vmem[:, c0:c0 + width])
            c0 += width
        xv = x_ref[...]
        r = _rms_scale(xv)
        xhat = xv * r
        dg_ref[...] += jnp.sum(dh * xhat, axis=0, keepdims=True)
        dx_ref[...] = dx1_ref[...] + _rms_bwd(xhat, r, g_ref[...], dh)
        _exchange_wait(ex, ex_in, ex_out, sems, grid)

    return pl.pallas_call(
        body, name="in_proj_bwd", grid=grid,
        in_specs=[_row_spec(tm, d), _row_spec(tm, d)] + [_row_spec(tm, w) for w in widths]
        + [_const_spec((1, d)), ANY] + [ANY] * len(ex.arrays),
        out_specs=[_row_spec(tm, d), _const_spec((1, d))] + [ANY] * len(ex.out_shapes),
        out_shape=[jax.ShapeDtypeStruct((s, d), F32), jax.ShapeDtypeStruct((1, d), F32)] + ex.out_shapes,
        scratch_shapes=[pltpu.VMEM((d, ni), BF16), pltpu.SemaphoreType.DMA((1,))] + _exchange_sems(ex),
        compiler_params=_params(1),
    )(x, dx1, *pieces, g1, w_in, *ex.arrays)


def _weight_grad(a, b, name, into=None, col0=0, n_total=None):
    s, m = a.shape
    n = b.shape[1]
    tm = min(m, DW_TILE)
    tn = min(n, DW_TILE) if n_total is None else DW_PIECE_TILE
    tk = min(s, DW_TOKENS * (2 if (m // tm) * (n // tn) >= DW_MANY_TILES else 1))
    nk = s // tk
    j0 = col0 // tn
    assert m % tm == 0 and n % tn == 0 and col0 % tn == 0

    def body(a_ref, b_ref, *rest):
        o_ref, acc = rest[-2:]
        k = pl.program_id(2)

        @pl.when(k == 0)
        def _():
            acc[...] = jnp.zeros_like(acc)

        acc[...] += _dot_tn(a_ref[...].astype(BF16), b_ref[...].astype(BF16))

        @pl.when(k == nk - 1)
        def _():
            o_ref[...] = acc[...].astype(BF16)

    extra = [] if into is None else [into]
    return pl.pallas_call(
        body, name=name, grid=(m // tm, n // tn, nk),
        in_specs=[pl.BlockSpec((tk, tm), lambda i, j, k: (k, i)), pl.BlockSpec((tk, tn), lambda i, j, k: (k, j))]
        + [ANY] * len(extra),
        out_specs=pl.BlockSpec((tm, tn), lambda i, j, k: (i, j0 + j)),
        out_shape=jax.ShapeDtypeStruct((m, n_total or n), BF16),
        input_output_aliases={2: 0} if extra else {},
        scratch_shapes=[pltpu.VMEM((tm, tn), F32)],
        compiler_params=_params(3),
    )(a, b, *extra)


def _mesh_position():
    return tuple(lax.axis_index(a) for a in MESH_AXES)


def _peer(me, k):
    bits = ((k >> 2) & 1, (k >> 1) & 1, k & 1)
    pos = tuple(1 - m if b else m for m, b in zip(me, bits))
    return pos, 4 * pos[0] + 2 * pos[1] + pos[2]


class _Exchange:
    def __init__(self, arrays, out_shapes, src, dst, relayed=None):
        self.arrays, self.out_shapes, self.src, self.dst = list(arrays), list(out_shapes), src, dst
        self.relayed = list(relayed) if relayed is not None else [False] * len(self.arrays)


_NO_EXCHANGE = _Exchange([], [], None, None)


def _exchange_sems(ex):
    n = len(ex.arrays)
    if n == 0:
        return []
    return [pltpu.SemaphoreType.DMA((n, N_DEV - 1)), pltpu.SemaphoreType.DMA((n, N_DEV - 1)),
            pltpu.SemaphoreType.DMA((n,))]


def _split_refs(rest, ex, n_own_outs):
    n_in, n_out = len(ex.arrays), len(ex.out_shapes)
    ex_in, rest = rest[:n_in], rest[n_in:]
    own, rest = rest[:n_own_outs], rest[n_own_outs:]
    return ex_in, own, rest[:n_out], rest[n_out:]


def _direct_steps(ex, w, in_refs, out_refs, sems):
    send_sems, recv_sems, local_sems = sems
    me = _mesh_position()
    mine = 4 * me[0] + 2 * me[1] + me[2]

    def copy(k):
        landing = ex.dst(w, out_refs, mine)
        if k == 0:
            return pltpu.make_async_copy(ex.src(w, in_refs, mine), landing, local_sems.at[w])
        peer, peer_idx = _peer(me, k)
        return pltpu.make_async_remote_copy(
            src_ref=ex.src(w, in_refs, peer_idx), dst_ref=landing, send_sem=send_sems.at[w, k - 1],
            recv_sem=recv_sems.at[w, k - 1], device_id=peer, device_id_type=pl.DeviceIdType.MESH)

    ks = range(N_DEV)
    return [lambda k=k: copy(k).start() for k in ks], [], [lambda k=k: copy(k).wait() for k in ks]


def _relayed_steps(ex, w, in_refs, out_refs, sems):
    send_sems, recv_sems, local_sems = sems
    x, y, c = _mesh_position()
    chips = [(1 - x, y), (x, 1 - y), (1 - x, 1 - y)]
    sibling = (x, y, 1 - c)
    js = range(len(chips))

    def block(px, py, pc):
        return ex.dst(w, out_refs, 4 * px + 2 * py + pc)

    def copy(k, dst, to, src=None):
        return pltpu.make_async_remote_copy(
            src_ref=ex.src(w, in_refs, None) if src is None else src, dst_ref=dst, send_sem=send_sems.at[w, k],
            recv_sem=recv_sems.at[w, k], device_id=to, device_id_type=pl.DeviceIdType.MESH)

    def local():
        return pltpu.make_async_copy(ex.src(w, in_refs, None), block(x, y, c), local_sems.at[w])

    def own(k):
        return copy(k, block(x, y, c), sibling if k == 0 else (*chips[k - 1], c))

    def came(j):
        return copy(1 + j, block(*chips[j], c), (*chips[j], c))

    def passed(j):
        return copy(4 + j, block(*chips[j], c), sibling, src=block(*chips[j], c))

    def from_sibling(k):
        return copy(k, block(x, y, 1 - c) if k == 0 else block(*chips[k - 4], 1 - c), sibling)

    start = [lambda: local().start()] + [lambda k=k: own(k).start() for k in range(4)]
    relay = [step for j in js for step in (lambda j=j: came(j).wait_recv(), lambda j=j: passed(j).start())]
    finish = ([lambda: local().wait()] + [lambda k=k: own(k).wait_send() for k in range(4)]
              + [lambda j=j: passed(j).wait_send() for j in js]
              + [lambda k=k: from_sibling(k).wait_recv() for k in (0, 4, 5, 6)])
    return start, relay, finish


def _exchange_steps(ex, in_refs, out_refs, sems):
    start, relay, finish = [], [], []
    for w in range(len(ex.arrays)):
        steps = (_relayed_steps if ex.relayed[w] else _direct_steps)(ex, w, in_refs, out_refs, sems)
        start += steps[0]
        relay += steps[1]
        finish += steps[2]
    return start, relay, finish


def _run(steps):
    for step in steps:
        step()


def _at_grid_step(grid, where):
    target = {"first": [0] * len(grid), "middle": [(3 * grid[0]) // 4] + [0] * (len(grid) - 1),
              "last": [g - 1 for g in grid]}[where]
    hit = pl.program_id(0) == target[0]
    for axis in range(1, len(grid)):
        hit = jnp.logical_and(hit, pl.program_id(axis) == target[axis])
    return hit


def _exchange_start(ex, in_refs, out_refs, sems, grid):
    if ex.arrays:
        @pl.when(_at_grid_step(grid, "first"))
        def _():
            _run(_exchange_steps(ex, in_refs, out_refs, sems)[0])

        if any(ex.relayed):
            assert grid[0] >= 2

            @pl.when(_at_grid_step(grid, "middle"))
            def _():
                _run(_exchange_steps(ex, in_refs, out_refs, sems)[1])


def _exchange_wait(ex, in_refs, out_refs, sems, grid):
    if ex.arrays:
        @pl.when(_at_grid_step(grid, "last"))
        def _():
            _run(_exchange_steps(ex, in_refs, out_refs, sems)[2])


def _shard_block(ref, shard_shape, by_col, idx):
    r, c = shard_shape
    if by_col:
        return ref.at[:, pl.ds(pl.multiple_of(idx * c, LANES), c)]
    return ref.at[pl.ds(pl.multiple_of(idx * r, 16), r), :]


def _full_shape(shard_shape, by_col):
    r, c = shard_shape
    return (r, N_DEV * c) if by_col else (N_DEV * r, c)


def _gather_exchange(shards, col_sharded):
    shapes = [a.shape for a in shards]
    return _Exchange(
        shards, [jax.ShapeDtypeStruct(_full_shape(sh, bc), a.dtype) for a, sh, bc in zip(shards, shapes, col_sharded)],
        lambda w, refs, idx: refs[w],
        lambda w, refs, idx: _shard_block(refs[w], shapes[w], col_sharded[w], idx), [True] * len(shards))


def _scatter_exchange(grads, col_sharded):
    shapes = []
    for g, by_col in zip(grads, col_sharded):
        r, c = g.shape
        shapes.append((r, c // N_DEV) if by_col else (r // N_DEV, c))
    return _Exchange(
        grads, [jax.ShapeDtypeStruct((N_DEV,) + sh, g.dtype) for g, sh in zip(grads, shapes)],
        lambda w, refs, idx: _shard_block(refs[w], shapes[w], col_sharded[w], idx),
        lambda w, refs, mine: refs[w].at[mine])


def _broadcast_exchange(arrays):
    return _Exchange(arrays, [jax.ShapeDtypeStruct((N_DEV,) + a.shape, a.dtype) for a in arrays],
                     lambda w, refs, idx: refs[w], lambda w, refs, mine: refs[w].at[mine])


def _join(*exs):
    arrays, shapes, owner = [], [], []
    for e in exs:
        for w in range(len(e.arrays)):
            owner.append((e, w, len(arrays), len(shapes)))
        arrays += e.arrays
        shapes += e.out_shapes

    def src(w, refs, idx):
        e, w0, i0, _ = owner[w]
        return e.src(w0, refs[i0:i0 + len(e.arrays)], idx)

    def dst(w, refs, idx):
        e, w0, _, o0 = owner[w]
        return e.dst(w0, refs[o0:o0 + len(e.out_shapes)], idx)

    return _Exchange(arrays, shapes, src, dst, [flag for e in exs for flag in e.relayed])


def _exchange_call(ex, name):
    n_in = len(ex.arrays)

    def body(*refs):
        in_refs, _, out_refs, sems = _split_refs(refs, ex, 0)
        for steps in _exchange_steps(ex, in_refs, out_refs, sems):
            _run(steps)

    return pl.pallas_call(
        body, name=name, in_specs=[ANY] * n_in, out_specs=[ANY] * len(ex.out_shapes), out_shape=ex.out_shapes,
        scratch_shapes=_exchange_sems(ex), compiler_params=pltpu.CompilerParams(vmem_limit_bytes=VMEM_LIMIT),
    )(*ex.arrays)


def _to_bf16(arrays):
    def body(*refs):
        for src, dst in zip(refs[:len(arrays)], refs[len(arrays):]):
            dst[...] = src[...].astype(BF16)

    vmem = pl.BlockSpec(memory_space=pltpu.VMEM)
    return pl.pallas_call(
        body, name="weights_to_bf16", in_specs=[vmem] * len(arrays), out_specs=[vmem] * len(arrays),
        out_shape=[jax.ShapeDtypeStruct(a.shape, BF16) for a in arrays],
        compiler_params=pltpu.CompilerParams(vmem_limit_bytes=VMEM_LIMIT),
    )(*arrays)


def _adamw(w, g, m, v):
    m = ADAM_B1 * m + (1.0 - ADAM_B1) * g
    v = ADAM_B2 * v + (1.0 - ADAM_B2) * jnp.square(g)
    m_hat = m / (1.0 - ADAM_B1 ** ADAM_STEP)
    v_hat = v / (1.0 - ADAM_B2 ** ADAM_STEP)
    delta = -ADAM_LR * (m_hat / (jnp.sqrt(v_hat) + ADAM_EPS) + ADAM_WD * w)
    return delta, m, v


def _sum_and_adamw(parts, w, m, v, name):
    r, c = w.shape
    tr = min(r, 256)

    def body(p_ref, w_ref, m_ref, v_ref, g_out, d_out, m_out, v_out):
        g = p_ref[0].astype(F32)
        for dev in range(1, N_DEV):
            g = g + p_ref[dev].astype(F32)
        g_out[...] = g
        d_out[...], m_out[...], v_out[...] = _adamw(w_ref[...], g, m_ref[...], v_ref[...])

    blk = pl.BlockSpec((tr, c), lambda i: (i, 0))
    return pl.pallas_call(
        body, name=name, grid=(r // tr,),
        in_specs=[pl.BlockSpec((N_DEV, tr, c), lambda i: (0, i, 0)), blk, blk, blk],
        out_specs=[blk] * 4, out_shape=[jax.ShapeDtypeStruct((r, c), F32)] * 4,
        compiler_params=_params(1),
    )(parts, w, m, v)


BIG = ("w_in", "w_attn_out", "w_conv_out", "w_o", "w_up", "w_down", "w_ple_gate", "w_ple_proj")
COL_SHARDED = {"w_in": True, "w_attn_out": True, "w_conv_out": True, "w_o": False, "w_up": True, "w_down": False,
               "w_ple_gate": False, "w_ple_proj": True}
SMALL = ("g_pre_mix", "b_gate", "g_post_mix", "g_pre_mlp", "g_post_mlp", "g_ple")


REST = BIG[1:]


def _local_grads(x, p, target, small, wconv, full, aw, cw, tm, t, gather_rest=None, scatter_rest=None,
                 scatter_in=None):
    full = dict(full)
    tm_wide = min(WIDE_BLOCKS * tm, x.shape[0])
    qkv, conv, gate, h1 = _in_proj_fwd(x, small["g_pre_mix"], small["b_gate"], full["w_in"], aw, cw, tm_wide)
    o, o_b, *rest = _attn_fwd(qkv, aw, t, gather_rest)
    full.update(zip(REST, rest))
    x1, mixed, mix_in, conv_in = _mix_fwd(x, o_b, conv, gate, wconv, small["g_post_mix"], full["w_attn_out"],
                                          full["w_conv_out"], full["w_o"], tm_wide)
    (dx1, h2, du, a, df, h3, ds3, dpp, loss, dg_pre_mlp, dg_post_mlp, dg_ple) = _mlp_ple_loss(
        x1, p, target, small["g_pre_mlp"], small["g_post_mlp"], small["g_ple"], full["w_up"], full["w_down"],
        full["w_ple_gate"], full["w_ple_proj"], tm)
    big = {"w_up": _weight_grad(h2, du, "dw_up"), "w_down": _weight_grad(a, df, "dw_down"),
           "w_ple_gate": _weight_grad(h3, ds3, "dw_ple_gate"), "w_ple_proj": _weight_grad(p, dpp, "dw_ple_proj")}
    (dmixed, dattn, dconvout, do, drest, dg_post_mix, db_gate, dwconv) = _mix_bwd(
        dx1, mixed, o_b, conv, gate, wconv, small["g_post_mix"], full["w_attn_out"], full["w_conv_out"], full["w_o"],
        tm_wide)
    big.update({"w_attn_out": _weight_grad(o_b, dattn, "dw_attn_out"),
                "w_conv_out": _weight_grad(conv_in, dconvout, "dw_conv_out"),
                "w_o": _weight_grad(mix_in, dmixed, "dw_o")})
    dq, dk, dv, *scattered = _attn_bwd(qkv, o, do, aw, t, scatter_rest and scatter_rest([big[n] for n in REST]))
    pieces = [dq, dk, dv, drest]
    dw_in, col0, ni = None, 0, full["w_in"].shape[1]
    for i, piece in enumerate(pieces):
        dw_in = _weight_grad(h1, piece, "dw_in_%d" % i, dw_in, col0, ni)
        col0 += piece.shape[1]
    big["w_in"] = dw_in
    dx, dg_pre_mix, *scattered_in = _in_proj_bwd(x, dx1, pieces, small["g_pre_mix"], full["w_in"], tm_wide,
                                                scatter_in and scatter_in(dw_in))
    small_grads = {"g_pre_mix": dg_pre_mix, "b_gate": db_gate, "g_post_mix": dg_post_mix, "g_pre_mlp": dg_pre_mlp,
                   "g_post_mlp": dg_post_mlp, "g_ple": dg_ple, "w_conv": dwconv}
    return loss[0, 0], dx, big, small_grads, scattered_in + scattered


PACK_ROWS = 16


def _pack_layout(shapes, d):
    slots, at = [], 0
    for i, (r, c) in enumerate(shapes):
        assert d % c == 0
        for row in range(r):
            slots.append((i, row, at // d, at % d))
            at += c
        at = -(-at // d) * d
    assert at <= PACK_ROWS * d
    return slots


def _pack_small(groups, d):
    shapes = [a.shape for a in groups[0]]
    slots = _pack_layout(shapes, d)
    n = len(shapes)

    def body(*refs):
        ins, outs = refs[:n * len(groups)], refs[n * len(groups):]
        for g, out in enumerate(outs):
            out[...] = jnp.zeros_like(out)
            for i, row, pr, pc in slots:
                src = ins[g * n + i]
                out[pr:pr + 1, pc:pc + shapes[i][1]] = src[row:row + 1, :]

    vmem = pl.BlockSpec(memory_space=pltpu.VMEM)
    return pl.pallas_call(
        body, name="pack_small", in_specs=[vmem] * (n * len(groups)), out_specs=[vmem] * len(groups),
        out_shape=[jax.ShapeDtypeStruct((PACK_ROWS, d), F32)] * len(groups),
    )(*[a for group in groups for a in group])


def _unpack_small(pack, shapes, d):
    slots = _pack_layout(shapes, d)
    return [jnp.stack([pack[pr, pc:pc + shapes[i][1]] for j, row, pr, pc in slots if j == i])
            for i in range(len(shapes))]


def kernel(x, p, g_pre_mix, w_in, b_gate, w_conv, w_attn_out, w_conv_out, w_o, g_post_mix, g_pre_mlp, w_up, w_down, g_post_mlp, g_ple, w_ple_gate, w_ple_proj, loss_target, m_g_pre_mix, m_w_in, m_b_gate, m_w_conv, m_w_attn_out, m_w_conv_out, m_w_o, m_g_post_mix, m_g_pre_mlp, m_w_up, m_w_down, m_g_post_mlp, m_g_ple, m_w_ple_gate, m_w_ple_proj, v_g_pre_mix, v_w_in, v_b_gate, v_w_conv, v_w_attn_out, v_w_conv_out, v_w_o, v_g_post_mix, v_g_pre_mlp, v_w_up, v_w_down, v_g_post_mlp, v_g_ple, v_w_ple_gate, v_w_ple_proj):
    given = dict(locals())
    order = ["g_pre_mix", "w_in", "b_gate", "w_conv", "w_attn_out", "w_conv_out", "w_o", "g_post_mix", "g_pre_mlp",
             "w_up", "w_down", "g_post_mlp", "g_ple", "w_ple_gate", "w_ple_proj"]
    d = x.shape[-1]
    me = 4 * lax.axis_index("x") + 2 * lax.axis_index("y") + lax.axis_index("c")

    col = [COL_SHARDED[n] for n in BIG]
    shards = _to_bf16([given[n][0] for n in BIG])
    cw_shard = w_conv.shape[-1]
    conv_tile = jnp.pad(w_conv[0], ((0, HALO - CONV_K), (0, LANES - cw_shard)))
    w_in_full, conv_g = _exchange_call(
        _join(_gather_exchange(shards[:1], col[:1]), _broadcast_exchange([conv_tile])), "gather_w_in")
    wconv = jnp.concatenate([conv_g[dev, :CONV_K, :cw_shard] for dev in range(N_DEV)], axis=1)

    small = {n: given[n] for n in SMALL}
    loss, dx, big_grads, small_grads, parts = _local_grads(
        x[0], p[0, 0], loss_target[0], small, wconv, {"w_in": w_in_full}, w_attn_out.shape[1], w_conv_out.shape[1],
        ROW_BLOCK, ATTN_BLOCK,
        _gather_exchange(shards[1:], col[1:]), lambda grads: _scatter_exchange(grads, col[1:]),
        lambda grad: _scatter_exchange([grad], col[:1]))
    small_names = list(SMALL) + ["w_conv"]
    two_d = lambda a: a.reshape(-1, d) if a.shape[-1] > d else a.reshape(-1, a.shape[-1])
    full_conv = lambda a: lax.dynamic_update_slice(jnp.zeros((CONV_K, N_DEV * cw_shard), F32), a[0],
                                                   (jnp.int32(0), me * cw_shard))
    groups = [[two_d(small_grads[n]) for n in small_names] + [loss.reshape(1, 1)]]
    for pre in ("", "m_", "v_"):
        groups.append([two_d(given[pre + n]) for n in SMALL] + [full_conv(given[pre + "w_conv"]), jnp.zeros((1, 1), F32)])
    pack, *state = _pack_small(groups, d)
    packs, = _exchange_call(_broadcast_exchange([pack]), "share_small_grads")

    grads, deltas, new_m, new_v = {}, {}, {}, {}
    for n, part in zip(BIG, parts):
        grads[n], deltas[n], new_m[n], new_v[n] = (
            a[None] for a in _sum_and_adamw(part, given[n][0], given["m_" + n][0], given["v_" + n][0], "adamw_" + n))

    outs = _sum_and_adamw(packs, *state, "adamw_small")
    shapes = [a.shape for a in groups[0]]
    for res, dst in zip(outs, (grads, deltas, new_m, new_v)):
        for n, a in zip(small_names + ["loss"], _unpack_small(res, shapes, d)):
            if n == "w_conv":
                a = lax.dynamic_slice(a, (jnp.int32(0), me * cw_shard), (CONV_K, cw_shard))[None]
            dst[n] = a.reshape(given[n].shape) if n in SMALL else a
    loss = grads["loss"][0, 0]

    return (loss, dx[None], *[grads[n] for n in order], *[deltas[n] for n in order],
            *[new_m[n] for n in order], *[new_v[n] for n in order])
```
